```python
import math
import jax, jax.numpy as jnp
from jax import lax
import numpy as np

D_MODEL = 1024
BATCH = 32
SEQ = 2048
DEPTH = 1

D_SSM = D_MODEL // 2
SSM_GROUP = 16
N_SSM_GROUPS = D_SSM // SSM_GROUP
SSM_STATE = 64
D_CONV = D_MODEL // 2
CONV_WIDTH = 3
D_FF = 4 * D_MODEL
N_BRANCH = 2
N_MOD = 6
RMS_EPS = 1e-6
DT_MIN = 1e-3
DT_MAX = 1e-1
IN_COLS = D_SSM + 3 * D_CONV + N_BRANCH * D_MODEL

kernel_name = "hybrid_s5_shortconv_gated_adaln_block"


def rmsnorm(x, g):
    xf = x.astype(jnp.float32)
    y = xf * lax.rsqrt(jnp.mean(xf * xf, axis=-1, keepdims=True) + RMS_EPS)
    return (y * g.astype(jnp.float32)).astype(x.dtype)


def modulate(h, shift, scale):
    return h * (1 + scale[:, None, :]) + shift[:, None, :]


def s5_mimo(u, lam_re, lam_im, log_dt, b_re, b_im, c_re, c_im, d_skip):
    f32 = jnp.float32
    n_len = u.shape[1]
    u32 = u.astype(f32)
    lam = lax.complex(lam_re.astype(f32), lam_im.astype(f32))
    dt = jnp.exp(log_dt.astype(f32))[:, None]
    lam_bar = jnp.exp(lam * dt)
    b = lax.complex(b_re.astype(f32), b_im.astype(f32))
    b_bar = ((lam_bar - 1) / lam)[..., None] * b
    bu = jnp.einsum('blgh,gph->blgp', u32.astype(jnp.complex64), b_bar)
    a = jnp.broadcast_to(lam_bar[None, None], (1, n_len) + lam_bar.shape)

    def combine(e1, e2):
        a1, s1 = e1
        a2, s2 = e2
        return a1 * a2, a2 * s1 + s2

    _, states = lax.associative_scan(combine, (a, bu), axis=1)
    c = lax.complex(c_re.astype(f32), c_im.astype(f32))
    y = jnp.real(jnp.einsum('blgp,ghp->blgh', states, c))
    y = y + d_skip.astype(f32) * u32
    return y.astype(u.dtype)


def short_gated_conv(cx, cb, cc, conv_w):
    v = cc * cx
    w = conv_w.astype(v.dtype)[:, None, :]
    y = lax.conv_general_dilated(
        v, w, window_strides=(1,), padding=[(CONV_WIDTH - 1, 0)],
        dimension_numbers=('NWC', 'WIO', 'NWC'), feature_group_count=D_CONV)
    return cb * y


def _fwd_setup_inputs(seed: int = 0) -> dict:
    key = jax.random.key(seed)
    ks = jax.random.split(key, 24)
    f32 = jnp.float32
    G, H, P = N_SSM_GROUPS, SSM_GROUP, SSM_STATE

    def nrm(k, shape, scale):
        return jax.random.normal(k, shape, f32) * scale

    x = jax.random.normal(ks[0], (BATCH, SEQ, D_MODEL), f32)
    c = jax.random.normal(ks[1], (BATCH, D_MODEL), f32)
    norm1_g = 1.0 + nrm(ks[2], (DEPTH, D_MODEL), 0.02)
    norm2_g = 1.0 + nrm(ks[3], (DEPTH, D_MODEL), 0.02)
    w_ada = nrm(ks[4], (DEPTH, D_MODEL, N_MOD * D_MODEL), 0.5 * D_MODEL ** -0.5)
    b_ada = nrm(ks[5], (DEPTH, N_MOD * D_MODEL), 0.01)
    w_in = nrm(ks[6], (DEPTH, D_MODEL, IN_COLS), D_MODEL ** -0.5)
    lam_re = -0.5 + nrm(ks[7], (DEPTH, G, P), 0.02)
    lam_im = math.pi * jnp.arange(P, dtype=f32)[None, None, :] + nrm(ks[8], (DEPTH, G, P), 0.02)
    log_dt = jax.random.uniform(ks[9], (DEPTH, G), f32, math.log(DT_MIN), math.log(DT_MAX))
    b_re = nrm(ks[10], (DEPTH, G, P, H), (2 * H) ** -0.5)
    b_im = nrm(ks[11], (DEPTH, G, P, H), (2 * H) ** -0.5)
    c_re = nrm(ks[12], (DEPTH, G, H, P), P ** -0.5)
    c_im = nrm(ks[13], (DEPTH, G, H, P), P ** -0.5)
    d_skip = nrm(ks[14], (DEPTH, D_SSM), 1.0)
    w_glu = nrm(ks[15], (DEPTH, D_SSM, D_SSM), D_SSM ** -0.5)
    b_glu = nrm(ks[16], (DEPTH, D_SSM), 0.01)
    conv_w = nrm(ks[17], (DEPTH, CONV_WIDTH, D_CONV), CONV_WIDTH ** -0.5)
    w_proj_ssm = nrm(ks[18], (DEPTH, D_SSM, D_MODEL), D_SSM ** -0.5)
    w_proj_conv = nrm(ks[19], (DEPTH, D_CONV, D_MODEL), D_CONV ** -0.5)
    w_out = nrm(ks[20], (DEPTH, D_MODEL, D_MODEL), D_MODEL ** -0.5)
    w_ff1 = nrm(ks[21], (DEPTH, D_MODEL, D_FF), D_MODEL ** -0.5)
    w_ff2 = nrm(ks[22], (DEPTH, D_FF, D_MODEL), D_FF ** -0.5)
    final_g = 1.0 + nrm(ks[23], (D_MODEL,), 0.02)
    return {"x": x, "c": c, "norm1_g": norm1_g, "norm2_g": norm2_g,
            "w_ada": w_ada, "b_ada": b_ada, "w_in": w_in,
            "lam_re": lam_re, "lam_im": lam_im, "log_dt": log_dt,
            "b_re": b_re, "b_im": b_im, "c_re": c_re, "c_im": c_im,
            "d_skip": d_skip, "w_glu": w_glu, "b_glu": b_glu, "conv_w": conv_w,
            "w_proj_ssm": w_proj_ssm, "w_proj_conv": w_proj_conv, "w_out": w_out,
            "w_ff1": w_ff1, "w_ff2": w_ff2, "final_g": final_g}


def _fwd_reference(x, c, norm1_g, norm2_g, w_ada, b_ada, w_in, lam_re, lam_im, log_dt,
              b_re, b_im, c_re, c_im, d_skip, w_glu, b_glu, conv_w,
              w_proj_ssm, w_proj_conv, w_out, w_ff1, w_ff2, final_g):
    n_b, n_len, _ = x.shape
    split_at = [D_SSM, D_SSM + D_CONV, D_SSM + 2 * D_CONV, D_SSM + 3 * D_CONV,
                D_SSM + 3 * D_CONV + D_MODEL]
    c_act = jax.nn.silu(c)
    for l in range(DEPTH):
        mod = c_act @ w_ada[l] + b_ada[l]
        sh1, sc1, g1, sh2, sc2, g2 = jnp.split(mod, N_MOD, axis=-1)

        h = modulate(rmsnorm(x, norm1_g[l]), sh1, sc1)
        p = h @ w_in[l]
        u_s, cb, cc, cx, gate_s, gate_c = jnp.split(p, split_at, axis=-1)

        u_g = u_s.reshape(n_b, n_len, N_SSM_GROUPS, SSM_GROUP)
        y_s = s5_mimo(u_g, lam_re[l], lam_im[l], log_dt[l], b_re[l], b_im[l],
                      c_re[l], c_im[l], d_skip[l].reshape(N_SSM_GROUPS, SSM_GROUP))
        y_s = jax.nn.gelu(y_s.reshape(n_b, n_len, D_SSM))
        y_s = y_s * jax.nn.sigmoid(y_s @ w_glu[l] + b_glu[l])

        y_c = short_gated_conv(cx, cb, cc, conv_w[l])

        merged = (jax.nn.sigmoid(gate_s) * (y_s @ w_proj_ssm[l])
                  + jax.nn.sigmoid(gate_c) * (y_c @ w_proj_conv[l]))
        x = x + g1[:, None, :] * (merged @ w_out[l])

        h2 = modulate(rmsnorm(x, norm2_g[l]), sh2, sc2)
        f = jnp.square(jax.nn.relu(h2 @ w_ff1[l])) @ w_ff2[l]
        x = x + g2[:, None, :] * f
    return rmsnorm(x, final_g)


import jax as _jax
import jax.numpy as _jnp

TWIN_FORMAT = 'train_step'
FWD_PARAMS = ['x', 'c', 'norm1_g', 'norm2_g', 'w_ada', 'b_ada', 'w_in', 'lam_re', 'lam_im', 'log_dt', 'b_re', 'b_im', 'c_re', 'c_im', 'd_skip', 'w_glu', 'b_glu', 'conv_w', 'w_proj_ssm', 'w_proj_conv', 'w_out', 'w_ff1', 'w_ff2', 'final_g']
TWIN_WEIGHTS = ['norm1_g', 'norm2_g', 'w_ada', 'b_ada', 'w_in', 'lam_re', 'lam_im', 'log_dt', 'b_re', 'b_im', 'c_re', 'c_im', 'd_skip', 'w_glu', 'b_glu', 'conv_w', 'w_proj_ssm', 'w_proj_conv', 'w_out', 'w_ff1', 'w_ff2', 'final_g']
TWIN_DIFF_INPUT = 'x'
TWIN_INPUTS = ['x', 'c', 'norm1_g', 'norm2_g', 'w_ada', 'b_ada', 'w_in', 'lam_re', 'lam_im', 'log_dt', 'b_re', 'b_im', 'c_re', 'c_im', 'd_skip', 'w_glu', 'b_glu', 'conv_w', 'w_proj_ssm', 'w_proj_conv', 'w_out', 'w_ff1', 'w_ff2', 'final_g', 'loss_target', 'm_norm1_g', 'm_norm2_g', 'm_w_ada', 'm_b_ada', 'm_w_in', 'm_lam_re', 'm_lam_im', 'm_log_dt', 'm_b_re', 'm_b_im', 'm_c_re', 'm_c_im', 'm_d_skip', 'm_w_glu', 'm_b_glu', 'm_conv_w', 'm_w_proj_ssm', 'm_w_proj_conv', 'm_w_out', 'm_w_ff1', 'm_w_ff2', 'm_final_g', 'v_norm1_g', 'v_norm2_g', 'v_w_ada', 'v_b_ada', 'v_w_in', 'v_lam_re', 'v_lam_im', 'v_log_dt', 'v_b_re', 'v_b_im', 'v_c_re', 'v_c_im', 'v_d_skip', 'v_w_glu', 'v_b_glu', 'v_conv_w', 'v_w_proj_ssm', 'v_w_proj_conv', 'v_w_out', 'v_w_ff1', 'v_w_ff2', 'v_final_g']
TWIN_OUTPUTS = ['loss', 'grad_x', 'grad_norm1_g', 'grad_norm2_g', 'grad_w_ada', 'grad_b_ada', 'grad_w_in', 'grad_lam_re', 'grad_lam_im', 'grad_log_dt', 'grad_b_re', 'grad_b_im', 'grad_c_re', 'grad_c_im', 'grad_d_skip', 'grad_w_glu', 'grad_b_glu', 'grad_conv_w', 'grad_w_proj_ssm', 'grad_w_proj_conv', 'grad_w_out', 'grad_w_ff1', 'grad_w_ff2', 'grad_final_g', 'delta_norm1_g', 'delta_norm2_g', 'delta_w_ada', 'delta_b_ada', 'delta_w_in', 'delta_lam_re', 'delta_lam_im', 'delta_log_dt', 'delta_b_re', 'delta_b_im', 'delta_c_re', 'delta_c_im', 'delta_d_skip', 'delta_w_glu', 'delta_b_glu', 'delta_conv_w', 'delta_w_proj_ssm', 'delta_w_proj_conv', 'delta_w_out', 'delta_w_ff1', 'delta_w_ff2', 'delta_final_g', 'new_m_norm1_g', 'new_m_norm2_g', 'new_m_w_ada', 'new_m_b_ada', 'new_m_w_in', 'new_m_lam_re', 'new_m_lam_im', 'new_m_log_dt', 'new_m_b_re', 'new_m_b_im', 'new_m_c_re', 'new_m_c_im', 'new_m_d_skip', 'new_m_w_glu', 'new_m_b_glu', 'new_m_conv_w', 'new_m_w_proj_ssm', 'new_m_w_proj_conv', 'new_m_w_out', 'new_m_w_ff1', 'new_m_w_ff2', 'new_m_final_g', 'new_v_norm1_g', 'new_v_norm2_g', 'new_v_w_ada', 'new_v_b_ada', 'new_v_w_in', 'new_v_lam_re', 'new_v_lam_im', 'new_v_log_dt', 'new_v_b_re', 'new_v_b_im', 'new_v_c_re', 'new_v_c_im', 'new_v_d_skip', 'new_v_w_glu', 'new_v_b_glu', 'new_v_conv_w', 'new_v_w_proj_ssm', 'new_v_w_proj_conv', 'new_v_w_out', 'new_v_w_ff1', 'new_v_w_ff2', 'new_v_final_g']
TWIN_LEAF_KINDS = {'loss': 'loss', 'grad_x': 'grad_x', 'grad_norm1_g': 'grad_w', 'grad_norm2_g': 'grad_w', 'grad_w_ada': 'grad_w', 'grad_b_ada': 'grad_w', 'grad_w_in': 'grad_w', 'grad_lam_re': 'grad_w', 'grad_lam_im': 'grad_w', 'grad_log_dt': 'grad_w', 'grad_b_re': 'grad_w', 'grad_b_im': 'grad_w', 'grad_c_re': 'grad_w', 'grad_c_im': 'grad_w', 'grad_d_skip': 'grad_w', 'grad_w_glu': 'grad_w', 'grad_b_glu': 'grad_w', 'grad_conv_w': 'grad_w', 'grad_w_proj_ssm': 'grad_w', 'grad_w_proj_conv': 'grad_w', 'grad_w_out': 'grad_w', 'grad_w_ff1': 'grad_w', 'grad_w_ff2': 'grad_w', 'grad_final_g': 'grad_w', 'delta_norm1_g': 'delta_w', 'delta_norm2_g': 'delta_w', 'delta_w_ada': 'delta_w', 'delta_b_ada': 'delta_w', 'delta_w_in': 'delta_w', 'delta_lam_re': 'delta_w', 'delta_lam_im': 'delta_w', 'delta_log_dt': 'delta_w', 'delta_b_re': 'delta_w', 'delta_b_im': 'delta_w', 'delta_c_re': 'delta_w', 'delta_c_im': 'delta_w', 'delta_d_skip': 'delta_w', 'delta_w_glu': 'delta_w', 'delta_b_glu': 'delta_w', 'delta_conv_w': 'delta_w', 'delta_w_proj_ssm': 'delta_w', 'delta_w_proj_conv': 'delta_w', 'delta_w_out': 'delta_w', 'delta_w_ff1': 'delta_w', 'delta_w_ff2': 'delta_w', 'delta_final_g': 'delta_w', 'new_m_norm1_g': 'new_m', 'new_m_norm2_g': 'new_m', 'new_m_w_ada': 'new_m', 'new_m_b_ada': 'new_m', 'new_m_w_in': 'new_m', 'new_m_lam_re': 'new_m', 'new_m_lam_im': 'new_m', 'new_m_log_dt': 'new_m', 'new_m_b_re': 'new_m', 'new_m_b_im': 'new_m', 'new_m_c_re': 'new_m', 'new_m_c_im': 'new_m', 'new_m_d_skip': 'new_m', 'new_m_w_glu': 'new_m', 'new_m_b_glu': 'new_m', 'new_m_conv_w': 'new_m', 'new_m_w_proj_ssm': 'new_m', 'new_m_w_proj_conv': 'new_m', 'new_m_w_out': 'new_m', 'new_m_w_ff1': 'new_m', 'new_m_w_ff2': 'new_m', 'new_m_final_g': 'new_m', 'new_v_norm1_g': 'new_v', 'new_v_norm2_g': 'new_v', 'new_v_w_ada': 'new_v', 'new_v_b_ada': 'new_v', 'new_v_w_in': 'new_v', 'new_v_lam_re': 'new_v', 'new_v_lam_im': 'new_v', 'new_v_log_dt': 'new_v', 'new_v_b_re': 'new_v', 'new_v_b_im': 'new_v', 'new_v_c_re': 'new_v', 'new_v_c_im': 'new_v', 'new_v_d_skip': 'new_v', 'new_v_w_glu': 'new_v', 'new_v_b_glu': 'new_v', 'new_v_conv_w': 'new_v', 'new_v_w_proj_ssm': 'new_v', 'new_v_w_proj_conv': 'new_v', 'new_v_w_out': 'new_v', 'new_v_w_ff1': 'new_v', 'new_v_w_ff2': 'new_v', 'new_v_final_g': 'new_v'}


def _forward(args):
    return _fwd_reference(*[args[k] for k in FWD_PARAMS])


def _output_shape():
    out = _jax.eval_shape(lambda: _forward(_fwd_setup_inputs(0)))
    return out.shape, out.dtype

N_MICROBATCH = 1
ADAM_LR = 0.001
ADAM_B1 = 0.9
ADAM_B2 = 0.999
ADAM_EPS = 1e-08
ADAM_WD = 0.01
ADAM_STEP = 10
PER_EXAMPLE_BATCH_AXIS = {'x': 0, 'c': 0, 'loss_target': 0}
SHARED_INPUTS = []
_WEIGHT_DTYPES = {'norm1_g': _jnp.float32, 'norm2_g': _jnp.float32, 'w_ada': _jnp.float32, 'b_ada': _jnp.float32, 'w_in': _jnp.float32, 'lam_re': _jnp.float32, 'lam_im': _jnp.float32, 'log_dt': _jnp.float32, 'b_re': _jnp.float32, 'b_im': _jnp.float32, 'c_re': _jnp.float32, 'c_im': _jnp.float32, 'd_skip': _jnp.float32, 'w_glu': _jnp.float32, 'b_glu': _jnp.float32, 'conv_w': _jnp.float32, 'w_proj_ssm': _jnp.float32, 'w_proj_conv': _jnp.float32, 'w_out': _jnp.float32, 'w_ff1': _jnp.float32, 'w_ff2': _jnp.float32, 'final_g': _jnp.float32}
MOMENT_SCALE = {'norm1_g': 9.536854e-02, 'norm2_g': 1.118593e-01, 'w_ada': 1.385150e-01, 'b_ada': 2.257130e-01, 'w_in': 4.878325e-02, 'lam_re': 2.956878e-03, 'lam_im': 2.829283e-03, 'log_dt': 1.133446e+00, 'b_re': 1.376825e-03, 'b_im': 1.593460e-03, 'c_re': 2.120552e-03, 'c_im': 2.016818e-03, 'd_skip': 2.531890e-02, 'w_glu': 7.132338e-03, 'b_glu': 1.010821e-02, 'conv_w': 7.746632e-02, 'w_proj_ssm': 1.645992e-02, 'w_proj_conv': 5.330457e-02, 'w_out': 5.619001e-02, 'w_ff1': 5.963380e-02, 'w_ff2': 1.090375e-01, 'final_g': 6.456673e+01}


def _to_microbatches(a, axis):
    t = _jnp.moveaxis(a, axis, 0)
    t = t.reshape((N_MICROBATCH, t.shape[0] // N_MICROBATCH) + t.shape[1:])
    return _jnp.moveaxis(t, 1, axis + 1)


def setup_inputs(seed: int = 0) -> dict:
    inp = _fwd_setup_inputs(seed)
    key = _jax.random.fold_in(_jax.random.key(seed), 7919)
    shape, _ = _output_shape()
    out = dict(inp)
    out["loss_target"] = _jax.random.normal(_jax.random.fold_in(key, 0), shape, _jnp.float32)
    for i, name in enumerate(TWIN_WEIGHTS):
        w = inp[name].astype(_jnp.float32)
        if MOMENT_SCALE is None:
            s = _jnp.sqrt(_jnp.mean(_jnp.square(w)) + 1e-30)
        else:
            s = MOMENT_SCALE[name]
        km, kv = _jax.random.split(_jax.random.fold_in(key, i + 1))
        out[name] = w
        out["m_" + name] = s * _jax.random.normal(km, w.shape, _jnp.float32)
        out["v_" + name] = (s * s) * _jax.random.uniform(kv, w.shape, _jnp.float32, 0.5, 1.5)
    if N_MICROBATCH > 1:
        for name, axis in PER_EXAMPLE_BATCH_AXIS.items():
            out[name] = _to_microbatches(out[name], axis)
    return {'x': out['x'], 'c': out['c'], 'norm1_g': out['norm1_g'], 'norm2_g': out['norm2_g'], 'w_ada': out['w_ada'], 'b_ada': out['b_ada'], 'w_in': out['w_in'], 'lam_re': out['lam_re'], 'lam_im': out['lam_im'], 'log_dt': out['log_dt'], 'b_re': out['b_re'], 'b_im': out['b_im'], 'c_re': out['c_re'], 'c_im': out['c_im'], 'd_skip': out['d_skip'], 'w_glu': out['w_glu'], 'b_glu': out['b_glu'], 'conv_w': out['conv_w'], 'w_proj_ssm': out['w_proj_ssm'], 'w_proj_conv': out['w_proj_conv'], 'w_out': out['w_out'], 'w_ff1': out['w_ff1'], 'w_ff2': out['w_ff2'], 'final_g': out['final_g'], 'loss_target': out['loss_target'], 'm_norm1_g': out['m_norm1_g'], 'm_norm2_g': out['m_norm2_g'], 'm_w_ada': out['m_w_ada'], 'm_b_ada': out['m_b_ada'], 'm_w_in': out['m_w_in'], 'm_lam_re': out['m_lam_re'], 'm_lam_im': out['m_lam_im'], 'm_log_dt': out['m_log_dt'], 'm_b_re': out['m_b_re'], 'm_b_im': out['m_b_im'], 'm_c_re': out['m_c_re'], 'm_c_im': out['m_c_im'], 'm_d_skip': out['m_d_skip'], 'm_w_glu': out['m_w_glu'], 'm_b_glu': out['m_b_glu'], 'm_conv_w': out['m_conv_w'], 'm_w_proj_ssm': out['m_w_proj_ssm'], 'm_w_proj_conv': out['m_w_proj_conv'], 'm_w_out': out['m_w_out'], 'm_w_ff1': out['m_w_ff1'], 'm_w_ff2': out['m_w_ff2'], 'm_final_g': out['m_final_g'], 'v_norm1_g': out['v_norm1_g'], 'v_norm2_g': out['v_norm2_g'], 'v_w_ada': out['v_w_ada'], 'v_b_ada': out['v_b_ada'], 'v_w_in': out['v_w_in'], 'v_lam_re': out['v_lam_re'], 'v_lam_im': out['v_lam_im'], 'v_log_dt': out['v_log_dt'], 'v_b_re': out['v_b_re'], 'v_b_im': out['v_b_im'], 'v_c_re': out['v_c_re'], 'v_c_im': out['v_c_im'], 'v_d_skip': out['v_d_skip'], 'v_w_glu': out['v_w_glu'], 'v_b_glu': out['v_b_glu'], 'v_conv_w': out['v_conv_w'], 'v_w_proj_ssm': out['v_w_proj_ssm'], 'v_w_proj_conv': out['v_w_proj_conv'], 'v_w_out': out['v_w_out'], 'v_w_ff1': out['v_w_ff1'], 'v_w_ff2': out['v_w_ff2'], 'v_final_g': out['v_final_g']}


def _loss(weights, diff, rest, loss_target):
    with _jax.named_scope("forward"):
        args = {**rest, TWIN_DIFF_INPUT: diff, **{k: w.astype(_WEIGHT_DTYPES[k]) for k, w in weights.items()}}
        y = _forward(args)
    with _jax.named_scope("loss_head"):
        err = _jnp.square(y.astype(_jnp.float32) - loss_target)
        return 0.5 * _jnp.sum(_jnp.mean(err, axis=-1)) if err.ndim else 0.5 * err


def _adamw(w, g, m, v):
    m = ADAM_B1 * m + (1.0 - ADAM_B1) * g
    v = ADAM_B2 * v + (1.0 - ADAM_B2) * _jnp.square(g)
    m_hat = m / (1.0 - ADAM_B1 ** ADAM_STEP)
    v_hat = v / (1.0 - ADAM_B2 ** ADAM_STEP)
    delta = -ADAM_LR * (m_hat / (_jnp.sqrt(v_hat) + ADAM_EPS) + ADAM_WD * w)
    return delta, m, v


def reference(x, c, norm1_g, norm2_g, w_ada, b_ada, w_in, lam_re, lam_im, log_dt, b_re, b_im, c_re, c_im, d_skip, w_glu, b_glu, conv_w, w_proj_ssm, w_proj_conv, w_out, w_ff1, w_ff2, final_g, loss_target, m_norm1_g, m_norm2_g, m_w_ada, m_b_ada, m_w_in, m_lam_re, m_lam_im, m_log_dt, m_b_re, m_b_im, m_c_re, m_c_im, m_d_skip, m_w_glu, m_b_glu, m_conv_w, m_w_proj_ssm, m_w_proj_conv, m_w_out, m_w_ff1, m_w_ff2, m_final_g, v_norm1_g, v_norm2_g, v_w_ada, v_b_ada, v_w_in, v_lam_re, v_lam_im, v_log_dt, v_b_re, v_b_im, v_c_re, v_c_im, v_d_skip, v_w_glu, v_b_glu, v_conv_w, v_w_proj_ssm, v_w_proj_conv, v_w_out, v_w_ff1, v_w_ff2, v_final_g):
    given = dict(x=x, c=c, norm1_g=norm1_g, norm2_g=norm2_g, w_ada=w_ada, b_ada=b_ada, w_in=w_in, lam_re=lam_re, lam_im=lam_im, log_dt=log_dt, b_re=b_re, b_im=b_im, c_re=c_re, c_im=c_im, d_skip=d_skip, w_glu=w_glu, b_glu=b_glu, conv_w=conv_w, w_proj_ssm=w_proj_ssm, w_proj_conv=w_proj_conv, w_out=w_out, w_ff1=w_ff1, w_ff2=w_ff2, final_g=final_g, loss_target=loss_target, m_norm1_g=m_norm1_g, m_norm2_g=m_norm2_g, m_w_ada=m_w_ada, m_b_ada=m_b_ada, m_w_in=m_w_in, m_lam_re=m_lam_re, m_lam_im=m_lam_im, m_log_dt=m_log_dt, m_b_re=m_b_re, m_b_im=m_b_im, m_c_re=m_c_re, m_c_im=m_c_im, m_d_skip=m_d_skip, m_w_glu=m_w_glu, m_b_glu=m_b_glu, m_conv_w=m_conv_w, m_w_proj_ssm=m_w_proj_ssm, m_w_proj_conv=m_w_proj_conv, m_w_out=m_w_out, m_w_ff1=m_w_ff1, m_w_ff2=m_w_ff2, m_final_g=m_final_g, v_norm1_g=v_norm1_g, v_norm2_g=v_norm2_g, v_w_ada=v_w_ada, v_b_ada=v_b_ada, v_w_in=v_w_in, v_lam_re=v_lam_re, v_lam_im=v_lam_im, v_log_dt=v_log_dt, v_b_re=v_b_re, v_b_im=v_b_im, v_c_re=v_c_re, v_c_im=v_c_im, v_d_skip=v_d_skip, v_w_glu=v_w_glu, v_b_glu=v_b_glu, v_conv_w=v_conv_w, v_w_proj_ssm=v_w_proj_ssm, v_w_proj_conv=v_w_proj_conv, v_w_out=v_w_out, v_w_ff1=v_w_ff1, v_w_ff2=v_w_ff2, v_final_g=v_final_g)
    weights = {n: given[n] for n in TWIN_WEIGHTS}
    shared = {n: given[n] for n in SHARED_INPUTS}
    per_example = {n: given[n] for n in ['x', 'c']}
    grad_fn = _jax.value_and_grad(_loss, argnums=(0, 1))

    def one_microbatch(ex, loss_target):
        ex = dict(ex)
        diff = ex.pop(TWIN_DIFF_INPUT)
        return grad_fn(weights, diff, {**shared, **ex}, loss_target)

    if N_MICROBATCH == 1:
        loss, (grad_w, grad_x) = one_microbatch(per_example, given["loss_target"])
    else:
        def body(carry, xs):
            loss_sum, grad_sum = carry
            l_k, (gw_k, gx_k) = one_microbatch(xs[0], xs[1])
            with _jax.named_scope("update"):
                return (loss_sum + l_k, _jax.tree.map(_jnp.add, grad_sum, gw_k)), gx_k

        init = (_jnp.zeros((), _jnp.float32), _jax.tree.map(_jnp.zeros_like, weights))
        (loss, grad_w), grad_x = _jax.lax.scan(body, init, (per_example, given["loss_target"]))
    with _jax.named_scope("update"):
        delta_w, new_m, new_v = {}, {}, {}
        for n in TWIN_WEIGHTS:
            delta_w[n], new_m[n], new_v[n] = _adamw(weights[n], grad_w[n], given["m_" + n], given["v_" + n])
    return (loss, grad_x, *[grad_w[n] for n in TWIN_WEIGHTS], *[delta_w[n] for n in TWIN_WEIGHTS],
            *[new_m[n] for n in TWIN_WEIGHTS], *[new_v[n] for n in TWIN_WEIGHTS])
```

```python
import functools

import jax
import jax.numpy as jnp
from jax import lax
from jax.experimental import pallas as pl
from jax.experimental.pallas import tpu as pltpu

F32 = jnp.float32
BF16 = jnp.bfloat16
MESH = pl.DeviceIdType.MESH
N_CHIPS = 4
N_DEV = 8
LANES = 128
SUBLANES = 8
V7X_VMEM_BYTES = 64 * 1024 * 1024
VMEM_LIMIT = V7X_VMEM_BYTES - 6 * 1024 * 1024
SSM_GROUP = 16
SSM_STATE = 64
RMS_EPS = 1e-6
ADAM_LR, ADAM_B1, ADAM_B2, ADAM_EPS, ADAM_WD, ADAM_STEP = 0.001, 0.9, 0.999, 1e-08, 0.01, 10

ANY = pl.BlockSpec(memory_space=pl.ANY)
VMEM_SPEC = pl.BlockSpec(memory_space=pltpu.VMEM)


def _cparams(**kw):
    return pltpu.CompilerParams(vmem_limit_bytes=VMEM_LIMIT, **kw)


def _dot(a, b):
    return jnp.dot(a, b, preferred_element_type=F32)


def _dot_nt(a, b):
    return lax.dot_general(a, b, (((1,), (1,)), ((), ())), preferred_element_type=F32)


def _dot_tn(a, b):
    return lax.dot_general(a, b, (((0,), (0,)), ((), ())), preferred_element_type=F32)


def _mesh_pos():
    return lax.axis_index("x"), lax.axis_index("y"), lax.axis_index("c")


def _allgather8(v, name):
    r, c = v.shape

    def body(x_ref, out_ref, send_sems, recv_sems, local_sem):
        x, y, cc = _mesh_pos()
        me, sibling = (x, y, cc), (x, y, 1 - cc)
        chips = [(1 - x, y), (x, 1 - y), (1 - x, 1 - y)]

        def slot(px, py, pc):
            return out_ref.at[4 * px + 2 * py + pc]

        def copy(k, block, to, src=None):
            return pltpu.make_async_remote_copy(
                src_ref=slot(*block) if src is None else src, dst_ref=slot(*block),
                send_sem=send_sems.at[k], recv_sem=recv_sems.at[k], device_id=to, device_id_type=MESH)

        mine = pltpu.make_async_copy(x_ref, slot(*me), local_sem)
        mine.start()
        first = [copy(0, me, sibling, src=x_ref)]
        first += [copy(1 + j, me, (*chip, cc), src=x_ref) for j, chip in enumerate(chips)]
        for cp in first:
            cp.start()
        passed = [copy(4 + j, (*chip, cc), sibling) for j, chip in enumerate(chips)]
        for j, chip in enumerate(chips):
            copy(1 + j, (*chip, cc), me).wait_recv()
            passed[j].start()
        copy(0, sibling, me).wait_recv()
        for j, chip in enumerate(chips):
            copy(4 + j, (*chip, 1 - cc), me).wait_recv()
        for cp in first + passed:
            cp.wait_send()
        mine.wait()

    return pl.pallas_call(
        body, out_shape=jax.ShapeDtypeStruct((N_DEV, r, c), v.dtype),
        in_specs=[VMEM_SPEC], out_specs=VMEM_SPEC,
        scratch_shapes=[pltpu.SemaphoreType.DMA((7,)), pltpu.SemaphoreType.DMA((7,)), pltpu.SemaphoreType.DMA],
        name=name)(v)


def _shard_region(ref, axis, shard_shape, q, half):
    R, C = shard_shape
    r0, nr = (0, R) if half is None else (half * (R // 2), R // 2)
    if axis == 1:
        return ref.at[pl.ds(r0, nr), pl.ds(q * C, C)]
    return ref.at[pl.ds(q * R + r0, nr), :]


def _allgather_weights(shards, axes):
    n = len(shards)
    shapes = [s.shape for s in shards]

    def body(*refs):
        ins, outs = refs[:n], refs[n:2 * n]
        send_s, recv_s, fsend_s, frecv_s, local_s = refs[2 * n:]
        x, y, c = _mesh_pos()
        q = 2 * x + y
        sibling = (x, y, 1 - c)
        chips = [(1 - x, y), (x, 1 - y), (1 - x, 1 - y)]

        def region(i, qq, half):
            return _shard_region(outs[i], axes[i], shapes[i], qq, half)

        def remote(src, dst, ss, rs, to):
            return pltpu.make_async_remote_copy(src_ref=src, dst_ref=dst, send_sem=ss, recv_sem=rs,
                                                device_id=to, device_id_type=MESH)

        local = []
        for i in range(n):
            cp = pltpu.make_async_copy(ins[i], region(i, q, None), local_s.at[i])
            cp.start()
            local.append(cp)
        sends = []
        for i in range(n):
            rh = shapes[i][0] // 2
            for j, (cx, cy) in enumerate(chips):
                cp = remote(ins[i].at[pl.ds(c * rh, rh), :], region(i, q, c),
                            send_s.at[3 * i + j], recv_s.at[3 * i + j], (cx, cy, c))
                cp.start()
                sends.append(cp)
        for i in range(n):
            for j, (cx, cy) in enumerate(chips):
                reg = region(i, 2 * cx + cy, c)
                remote(reg, reg, send_s.at[3 * i + j], recv_s.at[3 * i + j], (cx, cy, c)).wait_recv()
                cp = remote(reg, reg, fsend_s.at[3 * i + j], frecv_s.at[3 * i + j], sibling)
                cp.start()
                sends.append(cp)
        for i in range(n):
            for j, (cx, cy) in enumerate(chips):
                reg = region(i, 2 * cx + cy, 1 - c)
                remote(reg, reg, fsend_s.at[3 * i + j], frecv_s.at[3 * i + j], sibling).wait_recv()
        for cp in sends:
            cp.wait_send()
        for cp in local:
            cp.wait()

    out_shape = [jax.ShapeDtypeStruct((R, N_CHIPS * C) if ax == 1 else (N_CHIPS * R, C), BF16)
                 for (R, C), ax in zip(shapes, axes)]
    return pl.pallas_call(
        body, out_shape=out_shape, in_specs=[ANY] * n, out_specs=[ANY] * n,
        scratch_shapes=[pltpu.SemaphoreType.DMA((3 * n,))] * 4 + [pltpu.SemaphoreType.DMA((n,))],
        name="allgather_weights")(*shards)


def _rs_sibling_exchange(grads, axes, shapes):
    n = len(grads)

    def body(*refs):
        ins, own, theirs = refs[:n], refs[n:2 * n], refs[2 * n:3 * n]
        local_s, send_s, recv_s = refs[3 * n:]
        x, y, c = _mesh_pos()
        sibling = (x, y, 1 - c)
        cps = []
        for i in range(n):
            for qq in range(N_CHIPS):
                k = N_CHIPS * i + qq
                lc = pltpu.make_async_copy(_shard_region(ins[i], axes[i], shapes[i], qq, c), own[i].at[qq], local_s.at[k])
                lc.start()
                rc = pltpu.make_async_remote_copy(
                    src_ref=_shard_region(ins[i], axes[i], shapes[i], qq, 1 - c), dst_ref=theirs[i].at[qq],
                    send_sem=send_s.at[k], recv_sem=recv_s.at[k], device_id=sibling, device_id_type=MESH)
                rc.start()
                cps.append((lc, rc))
        for lc, rc in cps:
            lc.wait()
            rc.wait()

    stacked = [jax.ShapeDtypeStruct((N_CHIPS, R // 2, C), F32) for (R, C) in shapes]
    res = pl.pallas_call(
        body, out_shape=stacked + stacked, in_specs=[ANY] * n, out_specs=[ANY] * (2 * n),
        scratch_shapes=[pltpu.SemaphoreType.DMA((N_CHIPS * n,))] * 3, name="rs_sibling_exchange")(*grads)
    return res[:n], res[n:]


def _rs_chip_scatter(parts):
    n = len(parts)

    def body(*refs):
        ins, outs = refs[:n], refs[n:2 * n]
        local_s, send_s, recv_s = refs[2 * n:]
        x, y, c = _mesh_pos()
        q = 2 * x + y
        chips = [(1 - x, y), (x, 1 - y), (1 - x, 1 - y)]
        local, sends = [], []
        for i in range(n):
            lc = pltpu.make_async_copy(ins[i].at[q], outs[i].at[q], local_s.at[i])
            lc.start()
            local.append(lc)
            for j, (cx, cy) in enumerate(chips):
                cp = pltpu.make_async_remote_copy(
                    src_ref=ins[i].at[2 * cx + cy], dst_ref=outs[i].at[q],
                    send_sem=send_s.at[3 * i + j], recv_sem=recv_s.at[3 * i + j],
                    device_id=(cx, cy, c), device_id_type=MESH)
                cp.start()
                sends.append(cp)
        for i in range(n):
            for j, (cx, cy) in enumerate(chips):
                slot = outs[i].at[2 * cx + cy]
                pltpu.make_async_remote_copy(
                    src_ref=slot, dst_ref=slot, send_sem=send_s.at[3 * i + j], recv_sem=recv_s.at[3 * i + j],
                    device_id=(cx, cy, c), device_id_type=MESH).wait_recv()
        for cp in sends:
            cp.wait_send()
        for lc in local:
            lc.wait()

    return pl.pallas_call(
        body, out_shape=[jax.ShapeDtypeStruct(p.shape, p.dtype) for p in parts],
        in_specs=[ANY] * n, out_specs=[ANY] * n,
        scratch_shapes=[pltpu.SemaphoreType.DMA((n,)), pltpu.SemaphoreType.DMA((3 * n,)), pltpu.SemaphoreType.DMA((3 * n,))],
        name="rs_chip_scatter")(*parts)


def _rs_sibling_share(halves):
    n = len(halves)

    def body(*refs):
        ins, outs = refs[:n], refs[n:2 * n]
        local_s, send_s, recv_s = refs[2 * n:]
        x, y, c = _mesh_pos()
        sibling = (x, y, 1 - c)
        cps = []
        for i in range(n):
            rh = halves[i].shape[0]
            mine = outs[i].at[pl.ds(c * rh, rh), :]
            lc = pltpu.make_async_copy(ins[i], mine, local_s.at[i])
            lc.start()
            rc = pltpu.make_async_remote_copy(src_ref=ins[i], dst_ref=mine, send_sem=send_s.at[i], recv_sem=recv_s.at[i],
                                              device_id=sibling, device_id_type=MESH)
            rc.start()
            cps.append((lc, rc))
        for i, (lc, rc) in enumerate(cps):
            rh = halves[i].shape[0]
            other = outs[i].at[pl.ds((1 - c) * rh, rh), :]
            pltpu.make_async_remote_copy(src_ref=other, dst_ref=other, send_sem=send_s.at[i], recv_sem=recv_s.at[i],
                                         device_id=sibling, device_id_type=MESH).wait_recv()
            rc.wait_send()
            lc.wait()

    return pl.pallas_call(
        body, out_shape=[jax.ShapeDtypeStruct((2 * h.shape[0], h.shape[1]), F32) for h in halves],
        in_specs=[ANY] * n, out_specs=[ANY] * n,
        scratch_shapes=[pltpu.SemaphoreType.DMA((n,))] * 3, name="rs_sibling_share")(*halves)


def _row_block(rows, target=256):
    return target if rows % target == 0 else rows


def _cast_bf16(w, name):
    R, C = w.shape
    rb = _row_block(R)

    def body(w_ref, o_ref):
        o_ref[...] = w_ref[...].astype(BF16)

    return pl.pallas_call(body, out_shape=jax.ShapeDtypeStruct((R, C), BF16), grid=(R // rb,),
                          in_specs=[pl.BlockSpec((rb, C), lambda i: (i, 0))],
                          out_specs=pl.BlockSpec((rb, C), lambda i: (i, 0)), name=name)(w)


def _add_to_bf16(a, b, name):
    K, R, C = a.shape
    rb = _row_block(R)

    def body(a_ref, b_ref, o_ref):
        o_ref[...] = (a_ref[...] + b_ref[...]).astype(BF16)

    spec = pl.BlockSpec((None, rb, C), lambda k, i: (k, i, 0))
    return pl.pallas_call(body, out_shape=jax.ShapeDtypeStruct((K, R, C), BF16), grid=(K, R // rb),
                          in_specs=[spec, spec], out_specs=spec, name=name)(a, b)


def _sum_chips(parts, name):
    K, R, C = parts.shape
    rb = _row_block(R)

    def body(p_ref, o_ref):
        acc = p_ref[0].astype(F32)
        for k in range(1, K):
            acc = acc + p_ref[k].astype(F32)
        o_ref[...] = acc

    return pl.pallas_call(body, out_shape=jax.ShapeDtypeStruct((R, C), F32), grid=(R // rb,),
                          in_specs=[pl.BlockSpec((K, rb, C), lambda i: (0, i, 0))],
                          out_specs=pl.BlockSpec((rb, C), lambda i: (i, 0)), name=name)(parts)


def _sum_devices(parts, name):
    K, R, C = parts.shape

    def body(p_ref, o_ref):
        acc = p_ref[0]
        for k in range(1, K):
            acc = acc + p_ref[k]
        o_ref[...] = acc

    return pl.pallas_call(body, out_shape=jax.ShapeDtypeStruct((R, C), F32), name=name)(parts)


def _adamw(w, g, m, v, name):
    R, C = w.shape
    rb = _row_block(R)

    def body(w_ref, g_ref, m_ref, v_ref, d_ref, nm_ref, nv_ref):
        gv = g_ref[...]
        nm = ADAM_B1 * m_ref[...] + (1.0 - ADAM_B1) * gv
        nv = ADAM_B2 * v_ref[...] + (1.0 - ADAM_B2) * (gv * gv)
        m_hat = nm / (1.0 - ADAM_B1 ** ADAM_STEP)
        v_hat = nv / (1.0 - ADAM_B2 ** ADAM_STEP)
        d_ref[...] = -ADAM_LR * (m_hat / (jnp.sqrt(v_hat) + ADAM_EPS) + ADAM_WD * w_ref[...])
        nm_ref[...] = nm
        nv_ref[...] = nv

    spec = pl.BlockSpec((rb, C), lambda i: (i, 0))
    return pl.pallas_call(body, out_shape=[jax.ShapeDtypeStruct((R, C), F32)] * 3, grid=(R // rb,),
                          in_specs=[spec] * 4, out_specs=[spec] * 3, name=name)(w, g, m, v)


def _silu(v):
    return v * jax.nn.sigmoid(v)


def _ada_fwd(c_all, w_sh, b_sh):
    S, D = c_all.shape
    Ca = w_sh.shape[1]
    cb = 512 if Ca % 512 == 0 else Ca

    def body(c_ref, w_ref, b_ref, o_ref):
        act = _silu(c_ref[...]).astype(BF16)
        o_ref[...] = _dot(act, w_ref[...].astype(BF16)) + b_ref[...]

    return pl.pallas_call(
        body, out_shape=jax.ShapeDtypeStruct((S, Ca), F32), grid=(Ca // cb,),
        in_specs=[pl.BlockSpec((S, D), lambda j: (0, 0)), pl.BlockSpec((D, cb), lambda j: (0, j)),
                  pl.BlockSpec((1, cb), lambda j: (0, j))],
        out_specs=pl.BlockSpec((S, cb), lambda j: (0, j)), name="ada_fwd")(c_all, w_sh, b_sh)


def _ada_bwd(c_all, dmod_sh, dmod_all):
    S, D = c_all.shape
    Ca = dmod_sh.shape[1]
    C6 = dmod_all.shape[1]

    def body(c_ref, ds_ref, da_ref, gw_ref, gb_ref):
        act = _silu(c_ref[...]).astype(BF16)
        gw_ref[...] = _dot_tn(act, ds_ref[...].astype(BF16))
        gb_ref[...] = jnp.sum(da_ref[...], axis=0, keepdims=True)

    return pl.pallas_call(
        body, out_shape=[jax.ShapeDtypeStruct((D, Ca), F32), jax.ShapeDtypeStruct((1, C6), F32)],
        compiler_params=_cparams(), name="ada_bwd")(c_all, dmod_sh, dmod_all)


def _rms_fwd(xv):
    r = lax.rsqrt(jnp.mean(xv * xv, axis=-1, keepdims=True) + RMS_EPS)
    return xv * r, r


def _rms_bwd(dxh, xh, r):
    return r * (dxh - xh * jnp.mean(dxh * xh, axis=-1, keepdims=True))


def _const_spec(shape):
    nd = len(shape)
    return pl.BlockSpec(shape, lambda *_: (0,) * nd)


def _seq_spec(D, bps, rev_blocks=None):
    if rev_blocks is None:
        return pl.BlockSpec((None, 1, D), lambda i: (i // bps, 0, 0))
    return pl.BlockSpec((None, 1, D), lambda i: ((rev_blocks - 1 - i) // bps, 0, 0))


def _inproj_fwd(x2, sh1, sc1, n1g, w_in):
    N, D = x2.shape
    IN = w_in.shape[1]
    Bl = sh1.shape[0]
    TB = _row_block(N // Bl)
    bps = (N // Bl) // TB

    def body(x_ref, sh_ref, sc_ref, g_ref, w_ref, p_ref):
        xh, _ = _rms_fwd(x_ref[...])
        h = (xh * g_ref[...]) * (1.0 + sc_ref[...]) + sh_ref[...]
        p_ref[...] = _dot(h.astype(BF16), w_ref[...])

    return pl.pallas_call(
        body, out_shape=jax.ShapeDtypeStruct((N, IN), F32), grid=(N // TB,),
        in_specs=[pl.BlockSpec((TB, D), lambda i: (i, 0)), _seq_spec(D, bps), _seq_spec(D, bps),
                  _const_spec((1, D)), _const_spec((D, IN))],
        out_specs=pl.BlockSpec((TB, IN), lambda i: (i, 0)),
        compiler_params=_cparams(), name="inproj_fwd")(x2, sh1, sc1, n1g, w_in)


def _s5_dims(Bl, L, Ds):
    G = Ds // SSM_GROUP
    GP = G * SSM_STATE
    NP = GP // LANES
    T = min(64, L // 2)
    nb = 2 if (Ds // 2) % LANES == 0 else 1
    return G, GP, NP, T, nb


def _s5_disc_math(lr, li, ldt, bt_r, bt_i):
    dt = jnp.exp(ldt)
    er = jnp.exp(lr * dt)
    lbr = er * jnp.cos(li * dt)
    lbi = er * jnp.sin(li * dt)
    den = lr * lr + li * li
    fr = ((lbr - 1.0) * lr + lbi * li) / den
    fi = (lbi * lr - (lbr - 1.0) * li) / den
    return lbr, lbi, fr[None] * bt_r - fi[None] * bt_i, fr[None] * bt_i + fi[None] * bt_r


def _s5_disc(lr, li, ldt, bt_r, bt_i):
    def body(lr_ref, li_ref, ldt_ref, br_ref, bi_ref, o0, o1, o2, o3):
        res = _s5_disc_math(lr_ref[...], li_ref[...], ldt_ref[...], br_ref[...], bi_ref[...])
        for o, v in zip((o0, o1, o2, o3), res):
            o[...] = v

    S = jax.ShapeDtypeStruct
    return pl.pallas_call(body, out_shape=[S(lr.shape, F32)] * 2 + [S(bt_r.shape, F32)] * 2, name="s5_disc")(lr, li, ldt, bt_r, bt_i)


def _s5_disc_bwd(lr, li, ldt, bt_r, bt_i, dlbr, dlbi, dbbr, dbbi):
    def body(lr_ref, li_ref, ldt_ref, br_ref, bi_ref, g0, g1, g2, g3, o0, o1, o2, o3, o4):
        _, vjp = jax.vjp(_s5_disc_math, lr_ref[...], li_ref[...], ldt_ref[...], br_ref[...], bi_ref[...])
        res = vjp((g0[...], g1[...], g2[...], g3[...]))
        for o, v in zip((o0, o1, o2, o3, o4), res):
            o[...] = v

    S = jax.ShapeDtypeStruct
    return pl.pallas_call(body, out_shape=[S(lr.shape, F32)] * 2 + [S(ldt.shape, F32)] + [S(bt_r.shape, F32)] * 2,
                          name="s5_disc_bwd")(lr, li, ldt, bt_r, bt_i, dlbr, dlbi, dbbr, dbbi)


def _s5_fwd(p3, Bm_r, Bm_i, Cm_r, Cm_i, lam_r, lam_i, dsk):
    Bl, L, _ = p3.shape
    Ds = dsk.shape[1]
    G, GP, NP, T, nb = _s5_dims(Bl, L, Ds)
    nT = L // T
    BT = Bl * T
    dsb, gpb, npb = Ds // nb, GP // nb, NP // nb
    PG = min(8, NP)

    def body(u_ref, br_ref, bi_ref, cr_ref, ci_ref, lr_ref, li_ref, dsk_ref, sr_ref, si_ref, sb_r, sb_i, y_ref, car_r, car_i):
        i = pl.program_id(0)

        @pl.when(i == 0)
        def _():
            car_r[...] = jnp.zeros_like(car_r)
            car_i[...] = jnp.zeros_like(car_i)

        u = u_ref[...].reshape(BT, Ds)
        ub = u.astype(BF16)
        for blk in range(nb):
            ubb = ub[:, blk * dsb:(blk + 1) * dsb]
            for s_ref, b_ref in ((sr_ref, br_ref), (si_ref, bi_ref)):
                res = _dot(ubb, b_ref[blk])
                for kk in range(npb):
                    k = blk * npb + kk
                    s_ref[pl.ds(k * BT, BT), :] = res[:, kk * LANES:(kk + 1) * LANES]

        for k0 in range(0, NP, PG):
            ks = list(range(k0, k0 + PG))
            lr = [jnp.broadcast_to(lr_ref[pl.ds(k, 1), :], (Bl, LANES)) for k in ks]
            li = [jnp.broadcast_to(li_ref[pl.ds(k, 1), :], (Bl, LANES)) for k in ks]

            def step(t, carry):
                out = []
                for j, k in enumerate(ks):
                    ar, ai = carry[j]
                    idx = pl.ds(k * BT + t, Bl, stride=T)
                    nr = lr[j] * ar - li[j] * ai + sr_ref[idx, :]
                    ni = lr[j] * ai + li[j] * ar + si_ref[idx, :]
                    sr_ref[idx, :] = nr
                    si_ref[idx, :] = ni
                    out.append((nr, ni))
                return tuple(out)

            init = tuple((car_r[pl.ds(k * SUBLANES, Bl), :], car_i[pl.ds(k * SUBLANES, Bl), :]) for k in ks)
            fin = lax.fori_loop(0, T, step, init, unroll=2)
            for j, k in enumerate(ks):
                car_r[pl.ds(k * SUBLANES, Bl), :] = fin[j][0]
                car_i[pl.ds(k * SUBLANES, Bl), :] = fin[j][1]
        sb_r[...] = car_r[...]
        sb_i[...] = car_i[...]

        for blk in range(nb):
            s_r = jnp.concatenate([sr_ref[pl.ds((blk * npb + kk) * BT, BT), :] for kk in range(npb)], axis=1).astype(BF16)
            s_i = jnp.concatenate([si_ref[pl.ds((blk * npb + kk) * BT, BT), :] for kk in range(npb)], axis=1).astype(BF16)
            cols = slice(blk * dsb, (blk + 1) * dsb)
            yv = _dot(s_r, cr_ref[blk]) + _dot(s_i, ci_ref[blk]) + dsk_ref[:, cols] * u[:, cols]
            y_ref[:, :, cols] = yv.reshape(Bl, T, dsb)

    S = jax.ShapeDtypeStruct
    state = S((nT, NP * BT, LANES), F32)
    bound = S((nT, NP * SUBLANES, LANES), F32)
    sspec = pl.BlockSpec((None, NP * BT, LANES), lambda i: (i, 0, 0))
    bspec = pl.BlockSpec((None, NP * SUBLANES, LANES), lambda i: (i, 0, 0))
    return pl.pallas_call(
        body, out_shape=[state, state, bound, bound, S((Bl, L, Ds), F32)], grid=(nT,),
        in_specs=[pl.BlockSpec((Bl, T, Ds), lambda i: (0, i, 0)),
                  _const_spec((nb, dsb, gpb)), _const_spec((nb, dsb, gpb)),
                  _const_spec((nb, gpb, dsb)), _const_spec((nb, gpb, dsb)),
                  _const_spec((NP, LANES)), _const_spec((NP, LANES)), _const_spec((1, Ds))],
        out_specs=[sspec, sspec, bspec, bspec, pl.BlockSpec((Bl, T, Ds), lambda i: (0, i, 0))],
        scratch_shapes=[pltpu.VMEM((NP * SUBLANES, LANES), F32)] * 2,
        compiler_params=_cparams(dimension_semantics=("arbitrary",)), name="s5_fwd",
    )(p3, Bm_r, Bm_i, Cm_r, Cm_i, lam_r, lam_i, dsk)


def _s5_bwd(dy3, p3, Sr, Si, Sb_r, Sb_i, Bm_r, Bm_i, Cm_r, Cm_i, lam_r, lam_i, dsk):
    Bl, L, Ds = dy3.shape
    G, GP, NP, T, nb = _s5_dims(Bl, L, Ds)
    nT = L // T
    BT = Bl * T
    dsb, gpb, npb = Ds // nb, GP // nb, NP // nb
    PG = min(4, NP)

    def body(dy_ref, u_ref, sr_ref, si_ref, sbr_ref, sbi_ref, br_ref, bi_ref, cr_ref, ci_ref, lr_ref, li_ref, dsk_ref,
             du_ref, dbr_ref, dbi_ref, dcr_ref, dci_ref, dlr_ref, dli_ref, ddsk_ref,
             a_r, a_i, car_r, car_i, acc_r, acc_i):
        i = pl.program_id(0)

        @pl.when(i == 0)
        def _():
            for ref in (car_r, car_i, acc_r, acc_i, dbr_ref, dbi_ref, dcr_ref, dci_ref, ddsk_ref):
                ref[...] = jnp.zeros_like(ref)

        dy = dy_ref[...].reshape(BT, Ds)
        dyb = dy.astype(BF16)
        u = u_ref[...].reshape(BT, Ds)
        ub = u.astype(BF16)
        for blk in range(nb):
            dyb_b = dyb[:, blk * dsb:(blk + 1) * dsb]
            for q_ref, c_ref in ((a_r, cr_ref), (a_i, ci_ref)):
                res = _dot_nt(dyb_b, c_ref[blk])
                for kk in range(npb):
                    q_ref[pl.ds((blk * npb + kk) * BT, BT), :] = res[:, kk * LANES:(kk + 1) * LANES]

        first_block = (i == nT - 1)
        for k0 in range(0, NP, PG):
            ks = list(range(k0, k0 + PG))
            lr = [jnp.broadcast_to(lr_ref[pl.ds(k, 1), :], (Bl, LANES)) for k in ks]
            li = [jnp.broadcast_to(li_ref[pl.ds(k, 1), :], (Bl, LANES)) for k in ks]

            def adj(j, k, t, ar, ai):
                idx = pl.ds(k * BT + t, Bl, stride=T)
                nr = a_r[idx, :] + lr[j] * ar + li[j] * ai
                ni = a_i[idx, :] + lr[j] * ai - li[j] * ar
                a_r[idx, :] = nr
                a_i[idx, :] = ni
                return nr, ni

            def step(tt, carry):
                t = T - 1 - tt
                out = []
                for j, k in enumerate(ks):
                    ar, ai, cr, ci = carry[j]
                    nr, ni = adj(j, k, t, ar, ai)
                    pidx = pl.ds(k * BT + t - 1, Bl, stride=T)
                    spr, spi = sr_ref[pidx, :], si_ref[pidx, :]
                    out.append((nr, ni, cr + nr * spr + ni * spi, ci + ni * spr - nr * spi))
                return tuple(out)

            init = tuple((car_r[pl.ds(k * SUBLANES, Bl), :], car_i[pl.ds(k * SUBLANES, Bl), :],
                          acc_r[pl.ds(k * SUBLANES, Bl), :], acc_i[pl.ds(k * SUBLANES, Bl), :]) for k in ks)
            fin = lax.fori_loop(0, T - 1, step, init, unroll=2)
            for j, k in enumerate(ks):
                ar, ai, cr, ci = fin[j]
                nr, ni = adj(j, k, 0, ar, ai)
                rows = pl.ds(k * SUBLANES, Bl)
                spr = jnp.where(first_block, 0.0, sbr_ref[rows, :])
                spi = jnp.where(first_block, 0.0, sbi_ref[rows, :])
                car_r[rows, :] = nr
                car_i[rows, :] = ni
                acc_r[rows, :] = cr + nr * spr + ni * spi
                acc_i[rows, :] = ci + ni * spr - nr * spi

        ddsk_ref[...] += jnp.sum(dy * u, axis=0, keepdims=True)
        for blk in range(nb):
            cols = slice(blk * dsb, (blk + 1) * dsb)
            rows = [pl.ds((blk * npb + kk) * BT, BT) for kk in range(npb)]
            av_r = jnp.concatenate([a_r[r, :] for r in rows], axis=1).astype(BF16)
            av_i = jnp.concatenate([a_i[r, :] for r in rows], axis=1).astype(BF16)
            duv = _dot_nt(av_r, br_ref[blk]) + _dot_nt(av_i, bi_ref[blk]) + dy[:, cols] * dsk_ref[:, cols]
            du_ref[:, :, cols] = duv.reshape(Bl, T, dsb)
            dbr_ref[blk] += _dot_tn(ub[:, cols], av_r)
            dbi_ref[blk] += _dot_tn(ub[:, cols], av_i)
            sv_r = jnp.concatenate([sr_ref[r, :] for r in rows], axis=1).astype(BF16)
            sv_i = jnp.concatenate([si_ref[r, :] for r in rows], axis=1).astype(BF16)
            dcr_ref[blk] += _dot_tn(sv_r, dyb[:, cols])
            dci_ref[blk] += _dot_tn(sv_i, dyb[:, cols])

        @pl.when(i == nT - 1)
        def _():
            for k in range(NP):
                dlr_ref[pl.ds(k, 1), :] = jnp.sum(acc_r[pl.ds(k * SUBLANES, Bl), :], axis=0, keepdims=True)
                dli_ref[pl.ds(k, 1), :] = jnp.sum(acc_i[pl.ds(k * SUBLANES, Bl), :], axis=0, keepdims=True)

    S = jax.ShapeDtypeStruct
    rev = lambda i: nT - 1 - i
    sspec = pl.BlockSpec((None, NP * BT, LANES), lambda i: (rev(i), 0, 0))
    bspec = pl.BlockSpec((None, NP * SUBLANES, LANES), lambda i: (jnp.maximum(rev(i) - 1, 0), 0, 0))
    tspec = pl.BlockSpec((Bl, T, Ds), lambda i: (0, rev(i), 0))
    return pl.pallas_call(
        body,
        out_shape=[S((Bl, L, Ds), F32), S((nb, dsb, gpb), F32), S((nb, dsb, gpb), F32),
                   S((nb, gpb, dsb), F32), S((nb, gpb, dsb), F32), S((NP, LANES), F32), S((NP, LANES), F32), S((1, Ds), F32)],
        grid=(nT,),
        in_specs=[tspec, tspec, sspec, sspec, bspec, bspec,
                  _const_spec((nb, dsb, gpb)), _const_spec((nb, dsb, gpb)),
                  _const_spec((nb, gpb, dsb)), _const_spec((nb, gpb, dsb)),
                  _const_spec((NP, LANES)), _const_spec((NP, LANES)), _const_spec((1, Ds))],
        out_specs=[tspec, _const_spec((nb, dsb, gpb)), _const_spec((nb, dsb, gpb)),
                   _const_spec((nb, gpb, dsb)), _const_spec((nb, gpb, dsb)),
                   _const_spec((NP, LANES)), _const_spec((NP, LANES)), _const_spec((1, Ds))],
        scratch_shapes=[pltpu.VMEM((NP * BT, LANES), F32)] * 2 + [pltpu.VMEM((NP * SUBLANES, LANES), F32)] * 4,
        compiler_params=_cparams(dimension_semantics=("arbitrary",)), name="s5_bwd",
    )(dy3, p3, Sr, Si, Sb_r, Sb_i, Bm_r, Bm_i, Cm_r, Cm_i, lam_r, lam_i, dsk)


def _mix_values(ylin, cb, cc, cx, gs, gc, halo_v, wglu, bglu, cw, wps, wpc, wout):
    yg, gelu_vjp = jax.vjp(jax.nn.gelu, ylin)
    sz = jax.nn.sigmoid(_dot(yg.astype(BF16), wglu) + bglu)
    ys = yg * sz
    v = cc * cx
    rows = lax.broadcasted_iota(jnp.int32, v.shape, 0)
    h6 = halo_v[SUBLANES - 2:SUBLANES - 1, :]
    h7 = halo_v[SUBLANES - 1:SUBLANES, :]
    v1 = jnp.where(rows == 0, h7, pltpu.roll(v, 1, 0))
    v2 = jnp.where(rows == 0, h6, jnp.where(rows == 1, h7, pltpu.roll(v, 2, 0)))
    cv = cw[0:1, :] * v2 + cw[1:2, :] * v1 + cw[2:3, :] * v
    yc = cb * cv
    ps = _dot(ys.astype(BF16), wps)
    pc = _dot(yc.astype(BF16), wpc)
    sgs = jax.nn.sigmoid(gs)
    sgc = jax.nn.sigmoid(gc)
    merged = sgs * ps + sgc * pc
    mo = _dot(merged.astype(BF16), wout)
    return dict(yg=yg, gelu_vjp=gelu_vjp, sz=sz, ys=ys, v=v, v1=v1, v2=v2, cv=cv, yc=yc, ps=ps, pc=pc,
                sgs=sgs, sgc=sgc, merged=merged, mo=mo)


def _mix_in_specs(TB, D, Ds, Dc, bps, blk):
    hb = TB // SUBLANES
    halo = lambda col: pl.BlockSpec((SUBLANES, Dc), lambda i: (jnp.maximum(blk(i) * hb - 1, 0), col))
    return [pl.BlockSpec((TB, Dc), lambda i: (blk(i), 1)), pl.BlockSpec((TB, Dc), lambda i: (blk(i), 2)),
            pl.BlockSpec((TB, Dc), lambda i: (blk(i), 3)), pl.BlockSpec((TB, D), lambda i: (blk(i), 2)),
            pl.BlockSpec((TB, D), lambda i: (blk(i), 3)), halo(2), halo(3),
            pl.BlockSpec((TB, Ds), lambda i: (blk(i), 0))]


def _mix_fwd(p2, ylin2, x2, g1, wglu, bglu, cw, wps, wpc, wout):
    N, D = x2.shape
    Ds = ylin2.shape[1]
    Dc = Ds
    Bl = g1.shape[0]
    TB = _row_block(N // Bl)
    bps = (N // Bl) // TB

    def body(cb_ref, cc_ref, cx_ref, gs_ref, gc_ref, hcc_ref, hcx_ref, yl_ref, x_ref, g1_ref,
             wglu_ref, bglu_ref, cw_ref, wps_ref, wpc_ref, wout_ref, x1_ref):
        i = pl.program_id(0)
        halo_v = jnp.where(i % bps == 0, 0.0, hcc_ref[...] * hcx_ref[...])
        f = _mix_values(yl_ref[...], cb_ref[...], cc_ref[...], cx_ref[...], gs_ref[...], gc_ref[...], halo_v,
                        wglu_ref[...], bglu_ref[...], cw_ref[...], wps_ref[...], wpc_ref[...], wout_ref[...])
        x1_ref[...] = x_ref[...] + g1_ref[...] * f["mo"]

    return pl.pallas_call(
        body, out_shape=jax.ShapeDtypeStruct((N, D), F32), grid=(N // TB,),
        in_specs=_mix_in_specs(TB, D, Ds, Dc, bps, lambda i: i) + [
            pl.BlockSpec((TB, D), lambda i: (i, 0)), _seq_spec(D, bps),
            _const_spec((Ds, Ds)), _const_spec((1, Ds)), _const_spec((SUBLANES, Dc)),
            _const_spec((Ds, D)), _const_spec((Dc, D)), _const_spec((D, D))],
        out_specs=pl.BlockSpec((TB, D), lambda i: (i, 0)),
        compiler_params=_cparams(), name="mix_fwd",
    )(p2, p2, p2, p2, p2, p2, p2, ylin2, x2, g1, wglu, bglu, cw, wps, wpc, wout)


def _mix_bwd(p2, ylin2, dx1, g1, wglu, bglu, cw, wps, wpc, wout):
    N, D = dx1.shape
    Ds = ylin2.shape[1]
    Dc = Ds
    IN = p2.shape[1]
    Bl = g1.shape[0]
    TB = _row_block(N // Bl)
    bps = (N // Bl) // TB
    nblk = N // TB
    rev = lambda i: nblk - 1 - i

    def body(cb_ref, cc_ref, cx_ref, gs_ref, gc_ref, hcc_ref, hcx_ref, yl_ref, dx1_ref, g1_ref,
             wglu_ref, bglu_ref, cw_ref, wps_ref, wpc_ref, wout_ref,
             dyl_ref, dp_ref, gwout_ref, gwps_ref, gwpc_ref, gwglu_ref, gbglu_ref, gcw_ref, dg1_ref, nxt):
        i = pl.program_id(0)
        blk = rev(i)

        @pl.when(i == 0)
        def _():
            for ref in (gwout_ref, gwps_ref, gwpc_ref, gwglu_ref, gbglu_ref, gcw_ref):
                ref[...] = jnp.zeros_like(ref)

        @pl.when(i % bps == 0)
        def _():
            nxt[...] = jnp.zeros_like(nxt)
            dg1_ref[...] = jnp.zeros_like(dg1_ref)

        cb, cc, cx = cb_ref[...], cc_ref[...], cx_ref[...]
        halo_v = jnp.where(blk % bps == 0, 0.0, hcc_ref[...] * hcx_ref[...])
        wglu, wps, wpc, wout, cw = wglu_ref[...], wps_ref[...], wpc_ref[...], wout_ref[...], cw_ref[...]
        f = _mix_values(yl_ref[...], cb, cc, cx, gs_ref[...], gc_ref[...], halo_v, wglu, bglu_ref[...], cw, wps, wpc, wout)

        dx1v = dx1_ref[...]
        dg1_ref[...] += jnp.sum(dx1v * f["mo"], axis=0, keepdims=True)
        dmo = (g1_ref[...] * dx1v).astype(BF16)
        gwout_ref[...] += _dot_tn(f["merged"].astype(BF16), dmo)
        dmerged = _dot_nt(dmo, wout)
        dps = dmerged * f["sgs"]
        dpc = dmerged * f["sgc"]
        dgs = dmerged * f["ps"] * f["sgs"] * (1.0 - f["sgs"])
        dgc = dmerged * f["pc"] * f["sgc"] * (1.0 - f["sgc"])
        dpsb, dpcb = dps.astype(BF16), dpc.astype(BF16)
        gwps_ref[...] += _dot_tn(f["ys"].astype(BF16), dpsb)
        gwpc_ref[...] += _dot_tn(f["yc"].astype(BF16), dpcb)
        dys = _dot_nt(dpsb, wps)
        dyc = _dot_nt(dpcb, wpc)

        dcb = dyc * f["cv"]
        dcv = dyc * cb
        rows = lax.broadcasted_iota(jnp.int32, dcv.shape, 0)
        n0, n1 = nxt[0:1, :], nxt[1:2, :]
        d1 = jnp.where(rows == TB - 1, n0, pltpu.roll(dcv, TB - 1, 0))
        d2 = jnp.where(rows == TB - 2, n0, jnp.where(rows == TB - 1, n1, pltpu.roll(dcv, TB - 2, 0)))
        dv = cw[2:3, :] * dcv + cw[1:2, :] * d1 + cw[0:1, :] * d2
        nxt[0:2, :] = dcv[0:2, :]
        gcw_ref[0:1, :] += jnp.sum(dcv * f["v2"], axis=0, keepdims=True)
        gcw_ref[1:2, :] += jnp.sum(dcv * f["v1"], axis=0, keepdims=True)
        gcw_ref[2:3, :] += jnp.sum(dcv * f["v"], axis=0, keepdims=True)

        dz = dys * f["yg"] * f["sz"] * (1.0 - f["sz"])
        dzb = dz.astype(BF16)
        gwglu_ref[...] += _dot_tn(f["yg"].astype(BF16), dzb)
        gbglu_ref[...] += jnp.sum(dz, axis=0, keepdims=True)
        dyg = dys * f["sz"] + _dot_nt(dzb, wglu)
        dyl_ref[...] = f["gelu_vjp"](dyg)[0]

        dp_ref[:, 0:Dc] = dcb
        dp_ref[:, Dc:2 * Dc] = dv * cx
        dp_ref[:, 2 * Dc:3 * Dc] = dv * cc
        dp_ref[:, 3 * Dc:3 * Dc + D] = dgs
        dp_ref[:, 3 * Dc + D:3 * Dc + 2 * D] = dgc

    S = jax.ShapeDtypeStruct
    return pl.pallas_call(
        body,
        out_shape=[S((N, Ds), F32), S((N, IN - Ds), F32), S((D, D), F32), S((Ds, D), F32), S((Dc, D), F32),
                   S((Ds, Ds), F32), S((1, Ds), F32), S((SUBLANES, Dc), F32), S((Bl, 1, D), F32)],
        grid=(nblk,),
        in_specs=_mix_in_specs(TB, D, Ds, Dc, bps, rev) + [
            pl.BlockSpec((TB, D), lambda i: (rev(i), 0)), _seq_spec(D, bps, nblk),
            _const_spec((Ds, Ds)), _const_spec((1, Ds)), _const_spec((SUBLANES, Dc)),
            _const_spec((Ds, D)), _const_spec((Dc, D)), _const_spec((D, D))],
        out_specs=[pl.BlockSpec((TB, Ds), lambda i: (rev(i), 0)), pl.BlockSpec((TB, IN - Ds), lambda i: (rev(i), 0)),
                   _const_spec((D, D)), _const_spec((Ds, D)), _const_spec((Dc, D)), _const_spec((Ds, Ds)),
                   _const_spec((1, Ds)), _const_spec((SUBLANES, Dc)), _seq_spec(D, bps, nblk)],
        scratch_shapes=[pltpu.VMEM((SUBLANES, Dc), F32)],
        compiler_params=_cparams(dimension_semantics=("arbitrary",)), name="mix_bwd",
    )(p2, p2, p2, p2, p2, p2, p2, ylin2, dx1, g1, wglu, bglu, cw, wps, wpc, wout)


def _mlp_fwd_bwd(x1, tgt, sh2, sc2, g2, n2g, fg, w1, w2):
    N, D = x1.shape
    Dff = w1.shape[1]
    Bl = sh2.shape[0]
    TB = _row_block(N // Bl)
    bps = (N // Bl) // TB

    def body(x1_ref, t_ref, sh_ref, sc_ref, g2_ref, n2_ref, fg_ref, w1_ref, w2_ref,
             dx1_ref, h2_ref, da_ref, sq_ref, df_ref, loss_ref, gfg_ref, gn2_ref, dsh_ref, dsc_ref, dg2_ref):
        i = pl.program_id(0)

        @pl.when(i == 0)
        def _():
            for ref in (loss_ref, gfg_ref, gn2_ref):
                ref[...] = jnp.zeros_like(ref)

        @pl.when(i % bps == 0)
        def _():
            for ref in (dsh_ref, dsc_ref, dg2_ref):
                ref[...] = jnp.zeros_like(ref)

        x1v = x1_ref[...]
        sc, g2v, n2 = sc_ref[...], g2_ref[...], n2_ref[...]
        xh2, r2 = _rms_fwd(x1v)
        xn2 = xh2 * n2
        h2 = (xn2 * (1.0 + sc) + sh_ref[...]).astype(BF16)
        a = _dot(h2, w1_ref[...])
        ra = jnp.maximum(a, 0.0)
        sq = (ra * ra).astype(BF16)
        fv = _dot(sq, w2_ref[...])
        x2 = x1v + g2v * fv
        xh3, r3 = _rms_fwd(x2)
        err = xh3 * fg_ref[...] - t_ref[...]
        loss_ref[...] += 0.5 * jnp.sum(jnp.mean(err * err, axis=-1, keepdims=True), axis=0, keepdims=True)
        dy = err * (1.0 / D)
        gfg_ref[...] += jnp.sum(dy * xh3, axis=0, keepdims=True)
        dx2 = _rms_bwd(dy * fg_ref[...], xh3, r3)
        dg2_ref[...] += jnp.sum(dx2 * fv, axis=0, keepdims=True)
        df = (g2v * dx2).astype(BF16)
        dsq = _dot_nt(df, w2_ref[...])
        da = (2.0 * ra * dsq).astype(BF16)
        dh2 = _dot_nt(da, w1_ref[...])
        dsh_ref[...] += jnp.sum(dh2, axis=0, keepdims=True)
        dsc_ref[...] += jnp.sum(dh2 * xn2, axis=0, keepdims=True)
        dxn2 = dh2 * (1.0 + sc)
        gn2_ref[...] += jnp.sum(dxn2 * xh2, axis=0, keepdims=True)
        dx1_ref[...] = dx2 + _rms_bwd(dxn2 * n2, xh2, r2)
        h2_ref[...] = h2
        da_ref[...] = da
        sq_ref[...] = sq
        df_ref[...] = df

    S = jax.ShapeDtypeStruct
    row = lambda w: pl.BlockSpec((TB, w), lambda i: (i, 0))
    return pl.pallas_call(
        body,
        out_shape=[S((N, D), F32), S((N, D), BF16), S((N, Dff), BF16), S((N, Dff), BF16), S((N, D), BF16),
                   S((1, 1), F32), S((1, D), F32), S((1, D), F32), S((Bl, 1, D), F32), S((Bl, 1, D), F32), S((Bl, 1, D), F32)],
        grid=(N // TB,),
        in_specs=[row(D), row(D), _seq_spec(D, bps), _seq_spec(D, bps), _seq_spec(D, bps),
                  _const_spec((1, D)), _const_spec((1, D)), _const_spec((D, Dff)), _const_spec((Dff, D))],
        out_specs=[row(D), row(D), row(Dff), row(Dff), row(D), _const_spec((1, 1)), _const_spec((1, D)), _const_spec((1, D)),
                   _seq_spec(D, bps), _seq_spec(D, bps), _seq_spec(D, bps)],
        compiler_params=_cparams(dimension_semantics=("arbitrary",)), name="mlp_fwd_bwd",
    )(x1, tgt, sh2, sc2, g2, n2g, fg, w1, w2)


def _grad_w(a, b, name):
    N, K1 = a.shape
    K2 = b.shape[1]
    t1 = 1024 if K1 % 1024 == 0 else K1
    t2 = 1024 if K2 % 1024 == 0 else K2
    tn = 512 if N % 512 == 0 else N

    def body(a_ref, b_ref, o_ref):
        @pl.when(pl.program_id(2) == 0)
        def _():
            o_ref[...] = jnp.zeros_like(o_ref)

        o_ref[...] += _dot_tn(a_ref[...], b_ref[...])

    return pl.pallas_call(
        body, out_shape=jax.ShapeDtypeStruct((K1, K2), F32), grid=(K1 // t1, K2 // t2, N // tn),
        in_specs=[pl.BlockSpec((tn, t1), lambda i, j, k: (k, i)), pl.BlockSpec((tn, t2), lambda i, j, k: (k, j))],
        out_specs=pl.BlockSpec((t1, t2), lambda i, j, k: (i, j)),
        compiler_params=_cparams(dimension_semantics=("arbitrary", "arbitrary", "arbitrary")), name=name)(a, b)


def _inproj_bwd(x2, dx1, du, dprest, sh1, sc1, n1g, w_in):
    N, D = x2.shape
    IN = w_in.shape[1]
    Ds = du.shape[1]
    Bl = sh1.shape[0]
    TB = _row_block(N // Bl)
    bps = (N // Bl) // TB

    def body(x_ref, dx1_ref, du_ref, dpr_ref, sh_ref, sc_ref, g_ref, w_ref,
             gx_ref, h_ref, dp_ref, gn1_ref, dsh_ref, dsc_ref):
        i = pl.program_id(0)

        @pl.when(i == 0)
        def _():
            gn1_ref[...] = jnp.zeros_like(gn1_ref)

        @pl.when(i % bps == 0)
        def _():
            dsh_ref[...] = jnp.zeros_like(dsh_ref)
            dsc_ref[...] = jnp.zeros_like(dsc_ref)

        sc, n1 = sc_ref[...], g_ref[...]
        xh, r = _rms_fwd(x_ref[...])
        xn = xh * n1
        h_ref[...] = (xn * (1.0 + sc) + sh_ref[...]).astype(BF16)
        dub = du_ref[...].astype(BF16)
        dprb = dpr_ref[...].astype(BF16)
        dp_ref[:, 0:Ds] = dub
        dp_ref[:, Ds:IN] = dprb
        dh = _dot_nt(dub, w_ref[:, 0:Ds]) + _dot_nt(dprb, w_ref[:, Ds:IN])
        dsh_ref[...] += jnp.sum(dh, axis=0, keepdims=True)
        dsc_ref[...] += jnp.sum(dh * xn, axis=0, keepdims=True)
        dxn = dh * (1.0 + sc)
        gn1_ref[...] += jnp.sum(dxn * xh, axis=0, keepdims=True)
        gx_ref[...] = dx1_ref[...] + _rms_bwd(dxn * n1, xh, r)

    S = jax.ShapeDtypeStruct
    row = lambda w: pl.BlockSpec((TB, w), lambda i: (i, 0))
    return pl.pallas_call(
        body,
        out_shape=[S((N, D), F32), S((N, D), BF16), S((N, IN), BF16), S((1, D), F32), S((Bl, 1, D), F32), S((Bl, 1, D), F32)],
        grid=(N // TB,),
        in_specs=[row(D), row(D), row(Ds), row(IN - Ds), _seq_spec(D, bps), _seq_spec(D, bps),
                  _const_spec((1, D)), _const_spec((D, IN))],
        out_specs=[row(D), row(D), row(IN), _const_spec((1, D)), _seq_spec(D, bps), _seq_spec(D, bps)],
        compiler_params=_cparams(dimension_semantics=("arbitrary",)), name="inproj_bwd",
    )(x2, dx1, du, dprest, sh1, sc1, n1g, w_in)


def _diag_blocks_from_groups(m, nb):
    G, a, b = m.shape
    gb = G // nb
    eye = jnp.eye(gb, dtype=m.dtype)
    mm = m.reshape(nb, gb, a, b)
    return (mm[:, :, :, None, :] * eye[None, :, None, :, None]).reshape(nb, gb * a, gb * b)


def _groups_from_diag_blocks(d, G, a, b):
    nb = d.shape[0]
    gb = G // nb
    dd = d.reshape(nb, gb, a, gb, b)
    idx = jnp.arange(gb)
    return dd[:, idx, :, idx, :].transpose(1, 0, 2, 3).reshape(G, a, b)


def _pad_rows(v, rows):
    return jnp.concatenate([v, jnp.zeros((rows - v.shape[0],) + v.shape[1:], v.dtype)], axis=0)


def _pack(vs):
    flat = jnp.concatenate([v.reshape(-1) for v in vs])
    n = flat.shape[0]
    tile = SUBLANES * LANES
    npad = -(-n // tile) * tile
    flat = jnp.concatenate([flat, jnp.zeros((npad - n,), flat.dtype)])
    return flat.reshape(npad // LANES, LANES)


def _unpack(packed, shapes):
    flat = packed.reshape(-1)
    out, off = [], 0
    for s in shapes:
        n = 1
        for d in s:
            n *= d
        out.append(flat[off:off + n].reshape(s))
        off += n
    return out


def kernel(x, c, norm1_g, norm2_g, w_ada, b_ada, w_in, lam_re, lam_im, log_dt, b_re, b_im, c_re, c_im, d_skip, w_glu, b_glu, conv_w, w_proj_ssm, w_proj_conv, w_out, w_ff1, w_ff2, final_g, loss_target, m_norm1_g, m_norm2_g, m_w_ada, m_b_ada, m_w_in, m_lam_re, m_lam_im, m_log_dt, m_b_re, m_b_im, m_c_re, m_c_im, m_d_skip, m_w_glu, m_b_glu, m_conv_w, m_w_proj_ssm, m_w_proj_conv, m_w_out, m_w_ff1, m_w_ff2, m_final_g, v_norm1_g, v_norm2_g, v_w_ada, v_b_ada, v_w_in, v_lam_re, v_lam_im, v_log_dt, v_b_re, v_b_im, v_c_re, v_c_im, v_d_skip, v_w_glu, v_b_glu, v_conv_w, v_w_proj_ssm, v_w_proj_conv, v_w_out, v_w_ff1, v_w_ff2, v_final_g):
    Bl, L, D = x.shape
    N = Bl * L
    Ds = Dc = D // 2
    G, H, P = Ds // SSM_GROUP, SSM_GROUP, SSM_STATE
    GP = G * P
    NP = GP // LANES
    nb = _s5_dims(Bl, L, Ds)[4]
    IN = Ds + 3 * Dc + 2 * D
    ax, ay, ac = _mesh_pos()
    q = 2 * ax + ay
    dev = 2 * q + ac

    big_names = ["w_in", "w_ff1", "w_ff2", "w_out", "w_proj_ssm", "w_proj_conv", "w_glu"]
    big_w = dict(w_in=w_in[0], w_ff1=w_ff1[0], w_ff2=w_ff2[0], w_out=w_out[0],
                 w_proj_ssm=w_proj_ssm[0], w_proj_conv=w_proj_conv[0], w_glu=w_glu[0])
    big_axis = dict(w_in=1, w_ff1=1, w_ff2=0, w_out=0, w_proj_ssm=1, w_proj_conv=1, w_glu=0)
    axes = [big_axis[k] for k in big_names]
    shard_shapes = [big_w[k].shape for k in big_names]
    shards_bf16 = [_cast_bf16(big_w[k], "cast_" + k) for k in big_names]
    full = dict(zip(big_names, _allgather_weights(shards_bf16, axes)))

    Dcs = conv_w.shape[2]
    first = _allgather8(_pack([c, conv_w[0]]), "allgather_c").reshape(N_DEV, -1)
    c_all = first[:, :Bl * D].reshape(N_DEV * Bl, D)
    cw = first[0::2, Bl * D:Bl * D + 3 * Dcs].reshape(N_CHIPS, 3, Dcs).transpose(1, 0, 2).reshape(3, Dc)
    cw8 = _pad_rows(cw, SUBLANES)
    Ca = w_ada.shape[2]
    b_ada_sh = lax.dynamic_slice_in_dim(b_ada, q * Ca, Ca, axis=1)
    mod_part = _ada_fwd(c_all, w_ada[0], b_ada_sh)
    mod_g = _allgather8(mod_part, "allgather_mod")
    mod_all = mod_g[0::2].transpose(1, 0, 2).reshape(N_DEV * Bl, N_CHIPS * Ca)
    mod = lax.dynamic_slice_in_dim(mod_all, dev * Bl, Bl, axis=0)
    sh1, sc1, g1, sh2, sc2, g2 = [mod[:, k * D:(k + 1) * D].reshape(Bl, 1, D) for k in range(6)]

    ldt_c = log_dt[0].reshape(G, 1)
    bt_r = b_re[0].transpose(2, 0, 1)
    bt_i = b_im[0].transpose(2, 0, 1)
    lbr, lbi, bbt_r, bbt_i = _s5_disc(lam_re[0], lam_im[0], ldt_c, bt_r, bt_i)
    lam_r_p = lbr.reshape(NP, LANES)
    lam_i_p = lbi.reshape(NP, LANES)
    Bm_r = _diag_blocks_from_groups(bbt_r.transpose(1, 0, 2), nb).astype(BF16)
    Bm_i = _diag_blocks_from_groups(bbt_i.transpose(1, 0, 2), nb).astype(BF16)
    Cm_r = _diag_blocks_from_groups(c_re[0].transpose(0, 2, 1), nb).astype(BF16)
    Cm_i = _diag_blocks_from_groups(-c_im[0].transpose(0, 2, 1), nb).astype(BF16)

    x2 = x.reshape(N, D)
    p2 = _inproj_fwd(x2, sh1, sc1, norm1_g, full["w_in"])
    p3 = p2.reshape(Bl, L, IN)
    Sr, Si, Sb_r, Sb_i, ylin3 = _s5_fwd(p3, Bm_r, Bm_i, Cm_r, Cm_i, lam_r_p, lam_i_p, d_skip)
    ylin2 = ylin3.reshape(N, Ds)
    mix_w = (full["w_glu"], b_glu, cw8, full["w_proj_ssm"], full["w_proj_conv"], full["w_out"])
    x1 = _mix_fwd(p2, ylin2, x2, g1, *mix_w)

    (dx1, h2b, dab, sqb, dfb, loss_p, g_fg, g_n2, dsh2, dsc2, dg2) = _mlp_fwd_bwd(
        x1, loss_target.reshape(N, D), sh2, sc2, g2, norm2_g, final_g.reshape(1, D), full["w_ff1"], full["w_ff2"])
    gw_ff1 = _grad_w(h2b, dab, "grad_w_ff1")
    gw_ff2 = _grad_w(sqb, dfb, "grad_w_ff2")

    (dyl2, dprest, gw_out, gw_ps, gw_pc, gw_glu, gb_glu, gcw8, dg1) = _mix_bwd(p2, ylin2, dx1, g1, *mix_w)
    (du3, dBm_r, dBm_i, dCm_r, dCm_i, dlam_r_p, dlam_i_p, g_dsk) = _s5_bwd(
        dyl2.reshape(Bl, L, Ds), p3, Sr, Si, Sb_r, Sb_i, Bm_r, Bm_i, Cm_r, Cm_i, lam_r_p, lam_i_p, d_skip)
    (grad_x2, hb, dpb, g_n1, dsh1, dsc1) = _inproj_bwd(x2, dx1, du3.reshape(N, Ds), dprest, sh1, sc1, norm1_g, full["w_in"])
    gw_in = _grad_w(hb, dpb, "grad_w_in")

    dbbt_r = _groups_from_diag_blocks(dBm_r, G, H, P).transpose(1, 0, 2)
    dbbt_i = _groups_from_diag_blocks(dBm_i, G, H, P).transpose(1, 0, 2)
    dc_re = _groups_from_diag_blocks(dCm_r, G, P, H).transpose(0, 2, 1)
    dc_im = -_groups_from_diag_blocks(dCm_i, G, P, H).transpose(0, 2, 1)
    dmod = jnp.concatenate([dsh1, dsc1, dg1, dsh2, dsc2, dg2], axis=-1).reshape(Bl, 6 * D)
    small = [g_n1, g_n2, g_fg, g_dsk, gb_glu, gcw8[:3], dlam_r_p, dlam_i_p, dbbt_r, dbbt_i, dc_re, dc_im]
    small_shapes = [v.shape for v in small]
    n_small = sum(int(v.size) for v in small)
    gathered = _allgather8(_pack(small + [dmod]), "allgather_small")
    red = _unpack(_sum_devices(gathered, "sum_small"), small_shapes)
    (r_n1, r_n2, r_fg, r_dsk, r_bglu, r_cw, r_dlr, r_dli, r_dbr, r_dbi, r_cre, r_cim) = red
    dmod_all = gathered.reshape(N_DEV, -1)[:, n_small:n_small + Bl * 6 * D].reshape(N_DEV * Bl, 6 * D)
    gw_ada, gb_ada = _ada_bwd(c_all, lax.dynamic_slice_in_dim(dmod_all, q * Ca, Ca, axis=1), dmod_all)
    g_lr, g_li, g_ldt, g_bt_r, g_bt_i = _s5_disc_bwd(lam_re[0], lam_im[0], ldt_c, bt_r, bt_i,
                                                   r_dlr.reshape(G, P), r_dli.reshape(G, P), r_dbr, r_dbi)

    grads_full = [gw_in, gw_ff1, gw_ff2, gw_out, gw_ps, gw_pc, gw_glu]
    own, theirs = _rs_sibling_exchange(grads_full, axes, shard_shapes)
    parts = [_add_to_bf16(o, t, "presum_" + k) for o, t, k in zip(own, theirs, big_names)]
    recv = _rs_chip_scatter(parts)
    halves = [_sum_chips(r, "sum_" + k) for r, k in zip(recv, big_names)]
    reduced = dict(zip(big_names, _rs_sibling_share(halves)))

    grads = dict(
        norm1_g=r_n1, norm2_g=r_n2, w_ada=gw_ada, b_ada=gb_ada, lam_re=g_lr, lam_im=g_li, log_dt=g_ldt.reshape(1, G),
        b_re=g_bt_r.transpose(1, 2, 0), b_im=g_bt_i.transpose(1, 2, 0), c_re=r_cre, c_im=r_cim, d_skip=r_dsk,
        b_glu=r_bglu, conv_w=lax.dynamic_slice_in_dim(r_cw, q * Dcs, Dcs, axis=1), final_g=r_fg, **reduced)
    weights = dict(norm1_g=norm1_g, norm2_g=norm2_g, w_ada=w_ada, b_ada=b_ada, w_in=w_in, lam_re=lam_re, lam_im=lam_im,
                   log_dt=log_dt, b_re=b_re, b_im=b_im, c_re=c_re, c_im=c_im, d_skip=d_skip, w_glu=w_glu, b_glu=b_glu,
                   conv_w=conv_w, w_proj_ssm=w_proj_ssm, w_proj_conv=w_proj_conv, w_out=w_out, w_ff1=w_ff1, w_ff2=w_ff2,
                   final_g=final_g)
    m_in = dict(norm1_g=m_norm1_g, norm2_g=m_norm2_g, w_ada=m_w_ada, b_ada=m_b_ada, w_in=m_w_in, lam_re=m_lam_re,
                lam_im=m_lam_im, log_dt=m_log_dt, b_re=m_b_re, b_im=m_b_im, c_re=m_c_re, c_im=m_c_im, d_skip=m_d_skip,
                w_glu=m_w_glu, b_glu=m_b_glu, conv_w=m_conv_w, w_proj_ssm=m_w_proj_ssm, w_proj_conv=m_w_proj_conv,
                w_out=m_w_out, w_ff1=m_w_ff1, w_ff2=m_w_ff2, final_g=m_final_g)
    v_in = dict(norm1_g=v_norm1_g, norm2_g=v_norm2_g, w_ada=v_w_ada, b_ada=v_b_ada, w_in=v_w_in, lam_re=v_lam_re,
                lam_im=v_lam_im, log_dt=v_log_dt, b_re=v_b_re, b_im=v_b_im, c_re=v_c_re, c_im=v_c_im, d_skip=v_d_skip,
                w_glu=v_w_glu, b_glu=v_b_glu, conv_w=v_conv_w, w_proj_ssm=v_w_proj_ssm, w_proj_conv=v_w_proj_conv,
                w_out=v_w_out, w_ff1=v_w_ff1, w_ff2=v_w_ff2, final_g=v_final_g)
    names = list(weights)
    grads = {k: grads[k].reshape(weights[k].shape) for k in names}

    big_upd = big_names + ["w_ada"]
    delta, new_m, new_v = {}, {}, {}
    for k in big_upd:
        shp = weights[k].shape
        two_d = lambda a: a.reshape(shp[-2], shp[-1])
        d_, m_, v_ = _adamw(two_d(weights[k]), two_d(grads[k]), two_d(m_in[k]), two_d(v_in[k]), "adamw_" + k)
        delta[k], new_m[k], new_v[k] = d_.reshape(shp), m_.reshape(shp), v_.reshape(shp)
    small_upd = [k for k in names if k not in big_upd]
    shapes_s = [weights[k].shape for k in small_upd]
    d_, m_, v_ = _adamw(_pack([weights[k] for k in small_upd]), _pack([grads[k] for k in small_upd]),
                        _pack([m_in[k] for k in small_upd]), _pack([v_in[k] for k in small_upd]), "adamw_small")
    for k, dd, mm, vv in zip(small_upd, _unpack(d_, shapes_s), _unpack(m_, shapes_s), _unpack(v_, shapes_s)):
        delta[k], new_m[k], new_v[k] = dd, mm, vv

    loss = lax.psum(loss_p[0, 0], ("x", "y", "c"))
    grad_x = grad_x2.reshape(Bl, L, D)
    return (loss, grad_x, *[grads[k] for k in names], *[delta[k] for k in names],
            *[new_m[k] for k in names], *[new_v[k] for k in names])
```

```python
import functools

import jax
import jax.numpy as jnp
from jax import lax
from jax.experimental import pallas as pl
from jax.experimental.pallas import tpu as pltpu

F32 = jnp.float32
BF16 = jnp.bfloat16
MESH = pl.DeviceIdType.MESH
N_CHIPS = 4
N_DEV = 8
LANES = 128
SUBLANES = 8
V7X_VMEM_BYTES = 64 * 1024 * 1024
VMEM_LIMIT = V7X_VMEM_BYTES - 6 * 1024 * 1024
SSM_GROUP = 16
SSM_STATE = 64
S5_ROW_PAD = 4
RMS_EPS = 1e-6
ADAM_LR, ADAM_B1, ADAM_B2, ADAM_EPS, ADAM_WD, ADAM_STEP = 0.001, 0.9, 0.999, 1e-08, 0.01, 10

ANY = pl.BlockSpec(memory_space=pl.ANY)
VMEM_SPEC = pl.BlockSpec(memory_space=pltpu.VMEM)


def _cparams(**kw):
    return pltpu.CompilerParams(vmem_limit_bytes=VMEM_LIMIT, **kw)


def _dot(a, b):
    return jnp.dot(a, b, preferred_element_type=F32)


def _dot_nt(a, b):
    return lax.dot_general(a, b, (((1,), (1,)), ((), ())), preferred_element_type=F32)


def _dot_tn(a, b):
    return lax.dot_general(a, b, (((0,), (0,)), ((), ())), preferred_element_type=F32)


def _mesh_pos():
    return lax.axis_index("x"), lax.axis_index("y"), lax.axis_index("c")


def _allgather8(v, name):
    r, c = v.shape

    def body(x_ref, out_ref, send_sems, recv_sems, local_sem):
        x, y, cc = _mesh_pos()
        me, sibling = (x, y, cc), (x, y, 1 - cc)
        chips = [(1 - x, y), (x, 1 - y), (1 - x, 1 - y)]

        def slot(px, py, pc):
            return out_ref.at[4 * px + 2 * py + pc]

        def copy(k, block, to, src=None):
            return pltpu.make_async_remote_copy(
                src_ref=slot(*block) if src is None else src, dst_ref=slot(*block),
                send_sem=send_sems.at[k], recv_sem=recv_sems.at[k], device_id=to, device_id_type=MESH)

        mine = pltpu.make_async_copy(x_ref, slot(*me), local_sem)
        mine.start()
        first = [copy(0, me, sibling, src=x_ref)]
        first += [copy(1 + j, me, (*chip, cc), src=x_ref) for j, chip in enumerate(chips)]
        for cp in first:
            cp.start()
        passed = [copy(4 + j, (*chip, cc), sibling) for j, chip in enumerate(chips)]
        for j, chip in enumerate(chips):
            copy(1 + j, (*chip, cc), me).wait_recv()
            passed[j].start()
        copy(0, sibling, me).wait_recv()
        for j, chip in enumerate(chips):
            copy(4 + j, (*chip, 1 - cc), me).wait_recv()
        for cp in first + passed:
            cp.wait_send()
        mine.wait()

    return pl.pallas_call(
        body, out_shape=jax.ShapeDtypeStruct((N_DEV, r, c), v.dtype),
        in_specs=[VMEM_SPEC], out_specs=VMEM_SPEC,
        scratch_shapes=[pltpu.SemaphoreType.DMA((7,)), pltpu.SemaphoreType.DMA((7,)), pltpu.SemaphoreType.DMA],
        name=name)(v)


def _shard_region(ref, axis, shard_shape, q, half):
    R, C = shard_shape
    r0, nr = (0, R) if half is None else (half * (R // 2), R // 2)
    if axis == 1:
        return ref.at[pl.ds(r0, nr), pl.ds(q * C, C)]
    return ref.at[pl.ds(q * R + r0, nr), :]


def _allgather_weights(fulls, axes, shapes):
    n = len(fulls)

    def body(*refs):
        outs = refs[n:2 * n]
        send_s, recv_s, fsend_s, frecv_s = refs[2 * n:]
        x, y, c = _mesh_pos()
        q = 2 * x + y
        sibling = (x, y, 1 - c)
        chips = [(1 - x, y), (x, 1 - y), (1 - x, 1 - y)]

        def region(i, qq, half):
            return _shard_region(outs[i], axes[i], shapes[i], qq, half)

        def remote(src, dst, ss, rs, to):
            return pltpu.make_async_remote_copy(src_ref=src, dst_ref=dst, send_sem=ss, recv_sem=rs,
                                                device_id=to, device_id_type=MESH)

        sends = []
        for i in range(n):
            for j, (cx, cy) in enumerate(chips):
                mine = region(i, q, c)
                cp = remote(mine, mine, send_s.at[3 * i + j], recv_s.at[3 * i + j], (cx, cy, c))
                cp.start()
                sends.append(cp)
        for i in range(n):
            for j, (cx, cy) in enumerate(chips):
                reg = region(i, 2 * cx + cy, c)
                remote(reg, reg, send_s.at[3 * i + j], recv_s.at[3 * i + j], (cx, cy, c)).wait_recv()
                cp = remote(reg, reg, fsend_s.at[3 * i + j], frecv_s.at[3 * i + j], sibling)
                cp.start()
                sends.append(cp)
        for i in range(n):
            for j, (cx, cy) in enumerate(chips):
                reg = region(i, 2 * cx + cy, 1 - c)
                remote(reg, reg, fsend_s.at[3 * i + j], frecv_s.at[3 * i + j], sibling).wait_recv()
        for cp in sends:
            cp.wait_send()

    return pl.pallas_call(
        body, out_shape=[jax.ShapeDtypeStruct(f.shape, f.dtype) for f in fulls], in_specs=[ANY] * n, out_specs=[ANY] * n,
        input_output_aliases={i: i for i in range(n)},
        scratch_shapes=[pltpu.SemaphoreType.DMA((3 * n,))] * 4, name="allgather_weights")(*fulls)


def _rs_sibling_exchange(grads, axes, shapes):
    n = len(grads)

    def body(*refs):
        ins, theirs = refs[:n], refs[n:2 * n]
        send_s, recv_s = refs[2 * n:]
        x, y, c = _mesh_pos()
        sibling = (x, y, 1 - c)
        cps = []
        for i in range(n):
            for qq in range(N_CHIPS):
                k = N_CHIPS * i + qq
                rc = pltpu.make_async_remote_copy(
                    src_ref=_shard_region(ins[i], axes[i], shapes[i], qq, 1 - c), dst_ref=theirs[i].at[qq],
                    send_sem=send_s.at[k], recv_sem=recv_s.at[k], device_id=sibling, device_id_type=MESH)
                rc.start()
                cps.append(rc)
        for rc in cps:
            rc.wait()

    stacked = [jax.ShapeDtypeStruct((N_CHIPS, R // 2, C), F32) for (R, C) in shapes]
    return pl.pallas_call(
        body, out_shape=stacked, in_specs=[ANY] * n, out_specs=[ANY] * n,
        scratch_shapes=[pltpu.SemaphoreType.DMA((N_CHIPS * n,))] * 2, name="rs_sibling_exchange")(*grads)


def _rs_chip_scatter(parts):
    n = len(parts)

    def body(*refs):
        ins, outs = refs[:n], refs[n:2 * n]
        send_s, recv_s = refs[2 * n:]
        x, y, c = _mesh_pos()
        chips = [(1 - x, y), (x, 1 - y), (1 - x, 1 - y)]
        sends = []
        for i in range(n):
            for j, (cx, cy) in enumerate(chips):
                cp = pltpu.make_async_remote_copy(
                    src_ref=ins[i].at[2 * cx + cy], dst_ref=outs[i].at[j],
                    send_sem=send_s.at[3 * i + j], recv_sem=recv_s.at[3 * i + j],
                    device_id=(cx, cy, c), device_id_type=MESH)
                cp.start()
                sends.append(cp)
        for cp in sends:
            cp.wait()

    return pl.pallas_call(
        body, out_shape=[jax.ShapeDtypeStruct((3,) + p.shape[1:], p.dtype) for p in parts],
        in_specs=[ANY] * n, out_specs=[ANY] * n,
        scratch_shapes=[pltpu.SemaphoreType.DMA((3 * n,))] * 2, name="rs_chip_scatter")(*parts)


def _rs_sibling_share(fulls):
    n = len(fulls)

    def body(*refs):
        outs = refs[n:2 * n]
        send_s, recv_s = refs[2 * n:]
        x, y, c = _mesh_pos()
        sibling = (x, y, 1 - c)
        cps = []
        for i in range(n):
            rh = fulls[i].shape[0] // 2
            mine = outs[i].at[pl.ds(c * rh, rh), :]
            rc = pltpu.make_async_remote_copy(src_ref=mine, dst_ref=mine, send_sem=send_s.at[i], recv_sem=recv_s.at[i],
                                              device_id=sibling, device_id_type=MESH)
            rc.start()
            cps.append(rc)
        for i, rc in enumerate(cps):
            rh = fulls[i].shape[0] // 2
            other = outs[i].at[pl.ds((1 - c) * rh, rh), :]
            pltpu.make_async_remote_copy(src_ref=other, dst_ref=other, send_sem=send_s.at[i], recv_sem=recv_s.at[i],
                                         device_id=sibling, device_id_type=MESH).wait_recv()
            rc.wait_send()

    return pl.pallas_call(
        body, out_shape=[jax.ShapeDtypeStruct(f.shape, f.dtype) for f in fulls],
        in_specs=[ANY] * n, out_specs=[ANY] * n, input_output_aliases={i: i for i in range(n)},
        scratch_shapes=[pltpu.SemaphoreType.DMA((n,))] * 2, name="rs_sibling_share")(*fulls)


def _row_block(rows, target=256):
    return target if rows % target == 0 else rows


def _cast_into_full(w, axis, pos, name):
    R, C = w.shape
    rb = _row_block(R)
    nrb = R // rb
    if axis == 1:
        shape, omap = (R, N_CHIPS * C), lambda i, s: (i, s[0])
    else:
        shape, omap = (N_CHIPS * R, C), lambda i, s: (s[0] * nrb + i, 0)

    def body(s_ref, w_ref, o_ref):
        o_ref[...] = w_ref[...].astype(BF16)

    return pl.pallas_call(
        body, out_shape=jax.ShapeDtypeStruct(shape, BF16),
        grid_spec=pltpu.PrefetchScalarGridSpec(
            num_scalar_prefetch=1, grid=(nrb,), in_specs=[pl.BlockSpec((rb, C), lambda i, s: (i, 0))],
            out_specs=pl.BlockSpec((rb, C), omap)),
        name=name)(pos, w)


def _presum(g, theirs, axis, shape, pos, name):
    R, C = shape
    Rh = R // 2
    rb = _row_block(Rh)
    nr = Rh // rb
    if axis == 1:
        gmap = lambda k, i, s: (s[1] * nr + i, k)
    else:
        gmap = lambda k, i, s: (k * (R // rb) + s[1] * nr + i, 0)

    def body(s_ref, g_ref, t_ref, o_ref):
        o_ref[...] = (g_ref[...] + t_ref[...]).astype(BF16)

    spec = pl.BlockSpec((None, rb, C), lambda k, i, s: (k, i, 0))
    return pl.pallas_call(
        body, out_shape=jax.ShapeDtypeStruct((N_CHIPS, Rh, C), BF16),
        grid_spec=pltpu.PrefetchScalarGridSpec(
            num_scalar_prefetch=1, grid=(N_CHIPS, nr), in_specs=[pl.BlockSpec((rb, C), gmap), spec], out_specs=spec),
        name=name)(pos, g, theirs)


def _sum_chips(own, recv, pos, name):
    _, Rh, C = own.shape
    rb = _row_block(Rh)
    nr = Rh // rb

    def body(s_ref, o_ref, r_ref, out_ref):
        acc = o_ref[...].astype(F32)
        for k in range(3):
            acc = acc + r_ref[k].astype(F32)
        out_ref[...] = acc

    return pl.pallas_call(
        body, out_shape=jax.ShapeDtypeStruct((2 * Rh, C), F32),
        grid_spec=pltpu.PrefetchScalarGridSpec(
            num_scalar_prefetch=1, grid=(nr,),
            in_specs=[pl.BlockSpec((None, rb, C), lambda i, s: (s[0], i, 0)), pl.BlockSpec((3, rb, C), lambda i, s: (0, i, 0))],
            out_specs=pl.BlockSpec((rb, C), lambda i, s: (s[1] * nr + i, 0))),
        name=name)(pos, own, recv)


def _sum_devices(parts, name):
    K, R, C = parts.shape

    def body(p_ref, o_ref):
        acc = p_ref[0]
        for k in range(1, K):
            acc = acc + p_ref[k]
        o_ref[...] = acc

    return pl.pallas_call(body, out_shape=jax.ShapeDtypeStruct((R, C), F32), name=name)(parts)


def _adamw(w, g, m, v, name):
    R, C = w.shape
    rb = _row_block(R)

    def body(w_ref, g_ref, m_ref, v_ref, d_ref, nm_ref, nv_ref):
        gv = g_ref[...]
        nm = ADAM_B1 * m_ref[...] + (1.0 - ADAM_B1) * gv
        nv = ADAM_B2 * v_ref[...] + (1.0 - ADAM_B2) * (gv * gv)
        m_hat = nm / (1.0 - ADAM_B1 ** ADAM_STEP)
        v_hat = nv / (1.0 - ADAM_B2 ** ADAM_STEP)
        d_ref[...] = -ADAM_LR * (m_hat / (jnp.sqrt(v_hat) + ADAM_EPS) + ADAM_WD * w_ref[...])
        nm_ref[...] = nm
        nv_ref[...] = nv

    spec = pl.BlockSpec((rb, C), lambda i: (i, 0))
    return pl.pallas_call(body, out_shape=[jax.ShapeDtypeStruct((R, C), F32)] * 3, grid=(R // rb,),
                          in_specs=[spec] * 4, out_specs=[spec] * 3, name=name)(w, g, m, v)


def _silu(v):
    return v * jax.nn.sigmoid(v)


def _ada_fwd(c_all, w_sh, b_sh):
    S, D = c_all.shape
    Ca = w_sh.shape[1]
    cb = 512 if Ca % 512 == 0 else Ca

    def body(c_ref, w_ref, b_ref, o_ref):
        act = _silu(c_ref[...]).astype(BF16)
        o_ref[...] = _dot(act, w_ref[...].astype(BF16)) + b_ref[...]

    return pl.pallas_call(
        body, out_shape=jax.ShapeDtypeStruct((S, Ca), F32), grid=(Ca // cb,),
        in_specs=[pl.BlockSpec((S, D), lambda j: (0, 0)), pl.BlockSpec((D, cb), lambda j: (0, j)),
                  pl.BlockSpec((1, cb), lambda j: (0, j))],
        out_specs=pl.BlockSpec((S, cb), lambda j: (0, j)), name="ada_fwd")(c_all, w_sh, b_sh)


def _ada_bwd(c_all, dmod_sh, dmod_all):
    S, D = c_all.shape
    Ca = dmod_sh.shape[1]
    C6 = dmod_all.shape[1]

    def body(c_ref, ds_ref, da_ref, gw_ref, gb_ref):
        act = _silu(c_ref[...]).astype(BF16)
        gw_ref[...] = _dot_tn(act, ds_ref[...].astype(BF16))
        gb_ref[...] = jnp.sum(da_ref[...], axis=0, keepdims=True)

    return pl.pallas_call(
        body, out_shape=[jax.ShapeDtypeStruct((D, Ca), F32), jax.ShapeDtypeStruct((1, C6), F32)],
        compiler_params=_cparams(), name="ada_bwd")(c_all, dmod_sh, dmod_all)


def _rms_fwd(xv):
    r = lax.rsqrt(jnp.mean(xv * xv, axis=-1, keepdims=True) + RMS_EPS)
    return xv * r, r


def _rms_bwd(dxh, xh, r):
    return r * (dxh - xh * jnp.mean(dxh * xh, axis=-1, keepdims=True))


def _const_spec(shape):
    nd = len(shape)
    return pl.BlockSpec(shape, lambda *_: (0,) * nd)


def _seq_spec(D, bps, rev_blocks=None):
    if rev_blocks is None:
        return pl.BlockSpec((None, 1, D), lambda i: (i // bps, 0, 0))
    return pl.BlockSpec((None, 1, D), lambda i: ((rev_blocks - 1 - i) // bps, 0, 0))


def _inproj_fwd(x2, sh1, sc1, n1g, w_in):
    N, D = x2.shape
    IN = w_in.shape[1]
    Bl = sh1.shape[0]
    TB = _row_block(N // Bl)
    bps = (N // Bl) // TB

    def body(x_ref, sh_ref, sc_ref, g_ref, w_ref, p_ref):
        xh, _ = _rms_fwd(x_ref[...])
        h = (xh * g_ref[...]) * (1.0 + sc_ref[...]) + sh_ref[...]
        p_ref[...] = _dot(h.astype(BF16), w_ref[...])

    return pl.pallas_call(
        body, out_shape=jax.ShapeDtypeStruct((N, IN), F32), grid=(N // TB,),
        in_specs=[pl.BlockSpec((TB, D), lambda i: (i, 0)), _seq_spec(D, bps), _seq_spec(D, bps),
                  _const_spec((1, D)), _const_spec((D, IN))],
        out_specs=pl.BlockSpec((TB, IN), lambda i: (i, 0)),
        compiler_params=_cparams(), name="inproj_fwd")(x2, sh1, sc1, n1g, w_in)


def _s5_dims(Bl, L, Ds):
    G = Ds // SSM_GROUP
    GP = G * SSM_STATE
    NP = GP // LANES
    T = min(64, L // 2)
    nb = 2 if (Ds // 2) % LANES == 0 else 1
    return G, GP, NP, T, nb


def _s5_disc_math(lr, li, ldt, bt_r, bt_i):
    dt = jnp.exp(ldt)
    er = jnp.exp(lr * dt)
    lbr = er * jnp.cos(li * dt)
    lbi = er * jnp.sin(li * dt)
    den = lr * lr + li * li
    fr = ((lbr - 1.0) * lr + lbi * li) / den
    fi = (lbi * lr - (lbr - 1.0) * li) / den
    return lbr, lbi, fr[None] * bt_r - fi[None] * bt_i, fr[None] * bt_i + fi[None] * bt_r


def _s5_disc(lr, li, ldt, bt_r, bt_i):
    def body(lr_ref, li_ref, ldt_ref, br_ref, bi_ref, o0, o1, o2, o3):
        res = _s5_disc_math(lr_ref[...], li_ref[...], ldt_ref[...], br_ref[...], bi_ref[...])
        for o, v in zip((o0, o1, o2, o3), res):
            o[...] = v

    S = jax.ShapeDtypeStruct
    return pl.pallas_call(body, out_shape=[S(lr.shape, F32)] * 2 + [S(bt_r.shape, F32)] * 2, name="s5_disc")(lr, li, ldt, bt_r, bt_i)


def _s5_disc_bwd(lr, li, ldt, bt_r, bt_i, dlbr, dlbi, dbbr, dbbi):
    def body(lr_ref, li_ref, ldt_ref, br_ref, bi_ref, g0, g1, g2, g3, o0, o1, o2, o3, o4):
        _, vjp = jax.vjp(_s5_disc_math, lr_ref[...], li_ref[...], ldt_ref[...], br_ref[...], bi_ref[...])
        res = vjp((g0[...], g1[...], g2[...], g3[...]))
        for o, v in zip((o0, o1, o2, o3, o4), res):
            o[...] = v

    S = jax.ShapeDtypeStruct
    return pl.pallas_call(body, out_shape=[S(lr.shape, F32)] * 2 + [S(ldt.shape, F32)] + [S(bt_r.shape, F32)] * 2,
                          name="s5_disc_bwd")(lr, li, ldt, bt_r, bt_i, dlbr, dlbi, dbbr, dbbi)


def _s5_fwd(p3, Bm_r, Bm_i, Cm_r, Cm_i, lam_r, lam_i, dsk):
    Bl, L, _ = p3.shape
    Ds = dsk.shape[1]
    G, GP, NP, T, nb = _s5_dims(Bl, L, Ds)
    nT = L // T
    TP = T + S5_ROW_PAD
    BT = Bl * TP
    dsb, gpb, npb = Ds // nb, GP // nb, NP // nb
    PG = min(8, NP)

    def body(u_ref, br_ref, bi_ref, cr_ref, ci_ref, lr_ref, li_ref, dsk_ref, sr_ref, si_ref, sb_r, sb_i, y_ref,
             car_r, car_i, upad, ypad):
        i = pl.program_id(0)

        @pl.when(i == 0)
        def _():
            car_r[...] = jnp.zeros_like(car_r)
            car_i[...] = jnp.zeros_like(car_i)
            upad[...] = jnp.zeros_like(upad)

        for b in range(Bl):
            upad[pl.ds(b * TP, T), :] = u_ref[b]
        u = upad[...]
        ub = u.astype(BF16)
        for blk in range(nb):
            ubb = ub[:, blk * dsb:(blk + 1) * dsb]
            for s_ref, b_ref in ((sr_ref, br_ref), (si_ref, bi_ref)):
                res = _dot(ubb, b_ref[blk])
                for kk in range(npb):
                    k = blk * npb + kk
                    s_ref[pl.ds(k * BT, BT), :] = res[:, kk * LANES:(kk + 1) * LANES]

        for k0 in range(0, NP, PG):
            ks = list(range(k0, k0 + PG))
            lr = [jnp.broadcast_to(lr_ref[pl.ds(k, 1), :], (Bl, LANES)) for k in ks]
            li = [jnp.broadcast_to(li_ref[pl.ds(k, 1), :], (Bl, LANES)) for k in ks]

            def step(t, carry):
                out = []
                for j, k in enumerate(ks):
                    ar, ai = carry[j]
                    idx = pl.ds(k * BT + t, Bl, stride=TP)
                    nr = lr[j] * ar - li[j] * ai + sr_ref[idx, :]
                    ni = lr[j] * ai + li[j] * ar + si_ref[idx, :]
                    sr_ref[idx, :] = nr
                    si_ref[idx, :] = ni
                    out.append((nr, ni))
                return tuple(out)

            init = tuple((car_r[pl.ds(k * SUBLANES, Bl), :], car_i[pl.ds(k * SUBLANES, Bl), :]) for k in ks)
            fin = lax.fori_loop(0, T, step, init, unroll=2)
            for j, k in enumerate(ks):
                car_r[pl.ds(k * SUBLANES, Bl), :] = fin[j][0]
                car_i[pl.ds(k * SUBLANES, Bl), :] = fin[j][1]
        sb_r[...] = car_r[...]
        sb_i[...] = car_i[...]

        for blk in range(nb):
            s_r = jnp.concatenate([sr_ref[pl.ds((blk * npb + kk) * BT, BT), :] for kk in range(npb)], axis=1).astype(BF16)
            s_i = jnp.concatenate([si_ref[pl.ds((blk * npb + kk) * BT, BT), :] for kk in range(npb)], axis=1).astype(BF16)
            cols = slice(blk * dsb, (blk + 1) * dsb)
            ypad[:, cols] = _dot(s_r, cr_ref[blk]) + _dot(s_i, ci_ref[blk]) + dsk_ref[:, cols] * u[:, cols]
        for b in range(Bl):
            y_ref[b] = ypad[pl.ds(b * TP, T), :]

    S = jax.ShapeDtypeStruct
    state = S((nT, NP * BT, LANES), F32)
    bound = S((nT, NP * SUBLANES, LANES), F32)
    sspec = pl.BlockSpec((None, NP * BT, LANES), lambda i: (i, 0, 0))
    bspec = pl.BlockSpec((None, NP * SUBLANES, LANES), lambda i: (i, 0, 0))
    return pl.pallas_call(
        body, out_shape=[state, state, bound, bound, S((Bl, L, Ds), F32)], grid=(nT,),
        in_specs=[pl.BlockSpec((Bl, T, Ds), lambda i: (0, i, 0)),
                  _const_spec((nb, dsb, gpb)), _const_spec((nb, dsb, gpb)),
                  _const_spec((nb, gpb, dsb)), _const_spec((nb, gpb, dsb)),
                  _const_spec((NP, LANES)), _const_spec((NP, LANES)), _const_spec((1, Ds))],
        out_specs=[sspec, sspec, bspec, bspec, pl.BlockSpec((Bl, T, Ds), lambda i: (0, i, 0))],
        scratch_shapes=[pltpu.VMEM((NP * SUBLANES, LANES), F32)] * 2 + [pltpu.VMEM((BT, Ds), F32)] * 2,
        compiler_params=_cparams(dimension_semantics=("arbitrary",)), name="s5_fwd",
    )(p3, Bm_r, Bm_i, Cm_r, Cm_i, lam_r, lam_i, dsk)


def _s5_bwd(dy3, p3, Sr, Si, Sb_r, Sb_i, Bm_r, Bm_i, Cm_r, Cm_i, lam_r, lam_i, dsk):
    Bl, L, Ds = dy3.shape
    G, GP, NP, T, nb = _s5_dims(Bl, L, Ds)
    nT = L // T
    TP = T + S5_ROW_PAD
    BT = Bl * TP
    dsb, gpb, npb = Ds // nb, GP // nb, NP // nb
    PG = min(4, NP)

    def body(dy_ref, u_ref, sr_ref, si_ref, sbr_ref, sbi_ref, br_ref, bi_ref, cr_ref, ci_ref, lr_ref, li_ref, dsk_ref,
             du_ref, dbr_ref, dbi_ref, dcr_ref, dci_ref, dlr_ref, dli_ref, ddsk_ref,
             a_r, a_i, car_r, car_i, acc_r, acc_i, dypad, upad, dupad):
        i = pl.program_id(0)

        @pl.when(i == 0)
        def _():
            for ref in (car_r, car_i, acc_r, acc_i, dbr_ref, dbi_ref, dcr_ref, dci_ref, ddsk_ref, dypad, upad):
                ref[...] = jnp.zeros_like(ref)

        for b in range(Bl):
            dypad[pl.ds(b * TP, T), :] = dy_ref[b]
            upad[pl.ds(b * TP, T), :] = u_ref[b]
        dy = dypad[...]
        dyb = dy.astype(BF16)
        u = upad[...]
        ub = u.astype(BF16)
        for blk in range(nb):
            dyb_b = dyb[:, blk * dsb:(blk + 1) * dsb]
            for q_ref, c_ref in ((a_r, cr_ref), (a_i, ci_ref)):
                res = _dot_nt(dyb_b, c_ref[blk])
                for kk in range(npb):
                    q_ref[pl.ds((blk * npb + kk) * BT, BT), :] = res[:, kk * LANES:(kk + 1) * LANES]

        first_block = (i == nT - 1)
        for k0 in range(0, NP, PG):
            ks = list(range(k0, k0 + PG))
            lr = [jnp.broadcast_to(lr_ref[pl.ds(k, 1), :], (Bl, LANES)) for k in ks]
            li = [jnp.broadcast_to(li_ref[pl.ds(k, 1), :], (Bl, LANES)) for k in ks]

            def adj(j, k, t, ar, ai):
                idx = pl.ds(k * BT + t, Bl, stride=TP)
                nr = a_r[idx, :] + lr[j] * ar + li[j] * ai
                ni = a_i[idx, :] + lr[j] * ai - li[j] * ar
                a_r[idx, :] = nr
                a_i[idx, :] = ni
                return nr, ni

            def step(tt, carry):
                t = T - 1 - tt
                out = []
                for j, k in enumerate(ks):
                    ar, ai, cr, ci = carry[j]
                    nr, ni = adj(j, k, t, ar, ai)
                    pidx = pl.ds(k * BT + t - 1, Bl, stride=TP)
                    spr, spi = sr_ref[pidx, :], si_ref[pidx, :]
                    out.append((nr, ni, cr + nr * spr + ni * spi, ci + ni * spr - nr * spi))
                return tuple(out)

            init = tuple((car_r[pl.ds(k * SUBLANES, Bl), :], car_i[pl.ds(k * SUBLANES, Bl), :],
                          acc_r[pl.ds(k * SUBLANES, Bl), :], acc_i[pl.ds(k * SUBLANES, Bl), :]) for k in ks)
            fin = lax.fori_loop(0, T - 1, step, init, unroll=2)
            for j, k in enumerate(ks):
                ar, ai, cr, ci = fin[j]
                nr, ni = adj(j, k, 0, ar, ai)
                rows = pl.ds(k * SUBLANES, Bl)
                spr = jnp.where(first_block, 0.0, sbr_ref[rows, :])
                spi = jnp.where(first_block, 0.0, sbi_ref[rows, :])
                car_r[rows, :] = nr
                car_i[rows, :] = ni
                acc_r[rows, :] = cr + nr * spr + ni * spi
                acc_i[rows, :] = ci + ni * spr - nr * spi

        ddsk_ref[...] += jnp.sum(dy * u, axis=0, keepdims=True)
        for blk in range(nb):
            cols = slice(blk * dsb, (blk + 1) * dsb)
            rows = [pl.ds((blk * npb + kk) * BT, BT) for kk in range(npb)]
            av_r = jnp.concatenate([a_r[r, :] for r in rows], axis=1).astype(BF16)
            av_i = jnp.concatenate([a_i[r, :] for r in rows], axis=1).astype(BF16)
            dupad[:, cols] = _dot_nt(av_r, br_ref[blk]) + _dot_nt(av_i, bi_ref[blk]) + dy[:, cols] * dsk_ref[:, cols]
            dbr_ref[blk] += _dot_tn(ub[:, cols], av_r)
            dbi_ref[blk] += _dot_tn(ub[:, cols], av_i)
            sv_r = jnp.concatenate([sr_ref[r, :] for r in rows], axis=1).astype(BF16)
            sv_i = jnp.concatenate([si_ref[r, :] for r in rows], axis=1).astype(BF16)
            dcr_ref[blk] += _dot_tn(sv_r, dyb[:, cols])
            dci_ref[blk] += _dot_tn(sv_i, dyb[:, cols])
        for b in range(Bl):
            du_ref[b] = dupad[pl.ds(b * TP, T), :]

        @pl.when(i == nT - 1)
        def _():
            for k in range(NP):
                dlr_ref[pl.ds(k, 1), :] = jnp.sum(acc_r[pl.ds(k * SUBLANES, Bl), :], axis=0, keepdims=True)
                dli_ref[pl.ds(k, 1), :] = jnp.sum(acc_i[pl.ds(k * SUBLANES, Bl), :], axis=0, keepdims=True)

    S = jax.ShapeDtypeStruct
    rev = lambda i: nT - 1 - i
    sspec = pl.BlockSpec((None, NP * BT, LANES), lambda i: (rev(i), 0, 0))
    bspec = pl.BlockSpec((None, NP * SUBLANES, LANES), lambda i: (jnp.maximum(rev(i) - 1, 0), 0, 0))
    tspec = pl.BlockSpec((Bl, T, Ds), lambda i: (0, rev(i), 0))
    return pl.pallas_call(
        body,
        out_shape=[S((Bl, L, Ds), F32), S((nb, dsb, gpb), F32), S((nb, dsb, gpb), F32),
                   S((nb, gpb, dsb), F32), S((nb, gpb, dsb), F32), S((NP, LANES), F32), S((NP, LANES), F32), S((1, Ds), F32)],
        grid=(nT,),
        in_specs=[tspec, tspec, sspec, sspec, bspec, bspec,
                  _const_spec((nb, dsb, gpb)), _const_spec((nb, dsb, gpb)),
                  _const_spec((nb, gpb, dsb)), _const_spec((nb, gpb, dsb)),
                  _const_spec((NP, LANES)), _const_spec((NP, LANES)), _const_spec((1, Ds))],
        out_specs=[tspec, _const_spec((nb, dsb, gpb)), _const_spec((nb, dsb, gpb)),
                   _const_spec((nb, gpb, dsb)), _const_spec((nb, gpb, dsb)),
                   _const_spec((NP, LANES)), _const_spec((NP, LANES)), _const_spec((1, Ds))],
        scratch_shapes=[pltpu.VMEM((NP * BT, LANES), F32)] * 2 + [pltpu.VMEM((NP * SUBLANES, LANES), F32)] * 4
        + [pltpu.VMEM((BT, Ds), F32)] * 3,
        compiler_params=_cparams(dimension_semantics=("arbitrary",)), name="s5_bwd",
    )(dy3, p3, Sr, Si, Sb_r, Sb_i, Bm_r, Bm_i, Cm_r, Cm_i, lam_r, lam_i, dsk)


def _mix_values(ylin, cb, cc, cx, gs, gc, halo_v, wglu, bglu, cw, wps, wpc, wout):
    yg, gelu_vjp = jax.vjp(jax.nn.gelu, ylin)
    sz = jax.nn.sigmoid(_dot(yg.astype(BF16), wglu) + bglu)
    ys = yg * sz
    v = cc * cx
    rows = lax.broadcasted_iota(jnp.int32, v.shape, 0)
    h6 = halo_v[SUBLANES - 2:SUBLANES - 1, :]
    h7 = halo_v[SUBLANES - 1:SUBLANES, :]
    v1 = jnp.where(rows == 0, h7, pltpu.roll(v, 1, 0))
    v2 = jnp.where(rows == 0, h6, jnp.where(rows == 1, h7, pltpu.roll(v, 2, 0)))
    cv = cw[0:1, :] * v2 + cw[1:2, :] * v1 + cw[2:3, :] * v
    yc = cb * cv
    ps = _dot(ys.astype(BF16), wps)
    pc = _dot(yc.astype(BF16), wpc)
    sgs = jax.nn.sigmoid(gs)
    sgc = jax.nn.sigmoid(gc)
    merged = sgs * ps + sgc * pc
    mo = _dot(merged.astype(BF16), wout)
    return dict(yg=yg, gelu_vjp=gelu_vjp, sz=sz, ys=ys, v=v, v1=v1, v2=v2, cv=cv, yc=yc, ps=ps, pc=pc,
                sgs=sgs, sgc=sgc, merged=merged, mo=mo)


def _mix_in_specs(TB, D, Ds, Dc, bps, blk):
    hb = TB // SUBLANES
    halo = lambda col: pl.BlockSpec((SUBLANES, Dc), lambda i: (jnp.maximum(blk(i) * hb - 1, 0), col))
    return [pl.BlockSpec((TB, Dc), lambda i: (blk(i), 1)), pl.BlockSpec((TB, Dc), lambda i: (blk(i), 2)),
            pl.BlockSpec((TB, Dc), lambda i: (blk(i), 3)), pl.BlockSpec((TB, D), lambda i: (blk(i), 2)),
            pl.BlockSpec((TB, D), lambda i: (blk(i), 3)), halo(2), halo(3),
            pl.BlockSpec((TB, Ds), lambda i: (blk(i), 0))]


def _mix_fwd(p2, ylin2, x2, g1, wglu, bglu, cw, wps, wpc, wout):
    N, D = x2.shape
    Ds = ylin2.shape[1]
    Dc = Ds
    Bl = g1.shape[0]
    TB = _row_block(N // Bl)
    bps = (N // Bl) // TB

    def body(cb_ref, cc_ref, cx_ref, gs_ref, gc_ref, hcc_ref, hcx_ref, yl_ref, x_ref, g1_ref,
             wglu_ref, bglu_ref, cw_ref, wps_ref, wpc_ref, wout_ref, x1_ref):
        i = pl.program_id(0)
        halo_v = jnp.where(i % bps == 0, 0.0, hcc_ref[...] * hcx_ref[...])
        f = _mix_values(yl_ref[...], cb_ref[...], cc_ref[...], cx_ref[...], gs_ref[...], gc_ref[...], halo_v,
                        wglu_ref[...], bglu_ref[...], cw_ref[...], wps_ref[...], wpc_ref[...], wout_ref[...])
        x1_ref[...] = x_ref[...] + g1_ref[...] * f["mo"]

    return pl.pallas_call(
        body, out_shape=jax.ShapeDtypeStruct((N, D), F32), grid=(N // TB,),
        in_specs=_mix_in_specs(TB, D, Ds, Dc, bps, lambda i: i) + [
            pl.BlockSpec((TB, D), lambda i: (i, 0)), _seq_spec(D, bps),
            _const_spec((Ds, Ds)), _const_spec((1, Ds)), _const_spec((SUBLANES, Dc)),
            _const_spec((Ds, D)), _const_spec((Dc, D)), _const_spec((D, D))],
        out_specs=pl.BlockSpec((TB, D), lambda i: (i, 0)),
        compiler_params=_cparams(), name="mix_fwd",
    )(p2, p2, p2, p2, p2, p2, p2, ylin2, x2, g1, wglu, bglu, cw, wps, wpc, wout)


def _mix_bwd(p2, ylin2, dx1, g1, wglu, bglu, cw, wps, wpc, wout):
    N, D = dx1.shape
    Ds = ylin2.shape[1]
    Dc = Ds
    IN = p2.shape[1]
    Bl = g1.shape[0]
    TB = _row_block(N // Bl)
    bps = (N // Bl) // TB
    nblk = N // TB
    rev = lambda i: nblk - 1 - i

    def body(cb_ref, cc_ref, cx_ref, gs_ref, gc_ref, hcc_ref, hcx_ref, yl_ref, dx1_ref, g1_ref,
             wglu_ref, bglu_ref, cw_ref, wps_ref, wpc_ref, wout_ref,
             dyl_ref, dp_ref, gwout_ref, gwps_ref, gwpc_ref, gwglu_ref, gbglu_ref, gcw_ref, dg1_ref, nxt):
        i = pl.program_id(0)
        blk = rev(i)

        @pl.when(i == 0)
        def _():
            for ref in (gwout_ref, gwps_ref, gwpc_ref, gwglu_ref, gbglu_ref, gcw_ref):
                ref[...] = jnp.zeros_like(ref)

        @pl.when(i % bps == 0)
        def _():
            nxt[...] = jnp.zeros_like(nxt)
            dg1_ref[...] = jnp.zeros_like(dg1_ref)

        cb, cc, cx = cb_ref[...], cc_ref[...], cx_ref[...]
        halo_v = jnp.where(blk % bps == 0, 0.0, hcc_ref[...] * hcx_ref[...])
        wglu, wps, wpc, wout, cw = wglu_ref[...], wps_ref[...], wpc_ref[...], wout_ref[...], cw_ref[...]
        f = _mix_values(yl_ref[...], cb, cc, cx, gs_ref[...], gc_ref[...], halo_v, wglu, bglu_ref[...], cw, wps, wpc, wout)

        dx1v = dx1_ref[...]
        dg1_ref[...] += jnp.sum(dx1v * f["mo"], axis=0, keepdims=True)
        dmo = (g1_ref[...] * dx1v).astype(BF16)
        gwout_ref[...] += _dot_tn(f["merged"].astype(BF16), dmo)
        dmerged = _dot_nt(dmo, wout)
        dps = dmerged * f["sgs"]
        dpc = dmerged * f["sgc"]
        dgs = dmerged * f["ps"] * f["sgs"] * (1.0 - f["sgs"])
        dgc = dmerged * f["pc"] * f["sgc"] * (1.0 - f["sgc"])
        dpsb, dpcb = dps.astype(BF16), dpc.astype(BF16)
        gwps_ref[...] += _dot_tn(f["ys"].astype(BF16), dpsb)
        gwpc_ref[...] += _dot_tn(f["yc"].astype(BF16), dpcb)
        dys = _dot_nt(dpsb, wps)
        dyc = _dot_nt(dpcb, wpc)

        dcb = dyc * f["cv"]
        dcv = dyc * cb
        rows = lax.broadcasted_iota(jnp.int32, dcv.shape, 0)
        n0, n1 = nxt[0:1, :], nxt[1:2, :]
        d1 = jnp.where(rows == TB - 1, n0, pltpu.roll(dcv, TB - 1, 0))
        d2 = jnp.where(rows == TB - 2, n0, jnp.where(rows == TB - 1, n1, pltpu.roll(dcv, TB - 2, 0)))
        dv = cw[2:3, :] * dcv + cw[1:2, :] * d1 + cw[0:1, :] * d2
        nxt[0:2, :] = dcv[0:2, :]
        gcw_ref[0:1, :] += jnp.sum(dcv * f["v2"], axis=0, keepdims=True)
        gcw_ref[1:2, :] += jnp.sum(dcv * f["v1"], axis=0, keepdims=True)
        gcw_ref[2:3, :] += jnp.sum(dcv * f["v"], axis=0, keepdims=True)

        dz = dys * f["yg"] * f["sz"] * (1.0 - f["sz"])
        dzb = dz.astype(BF16)
        gwglu_ref[...] += _dot_tn(f["yg"].astype(BF16), dzb)
        gbglu_ref[...] += jnp.sum(dz, axis=0, keepdims=True)
        dyg = dys * f["sz"] + _dot_nt(dzb, wglu)
        dyl_ref[...] = f["gelu_vjp"](dyg)[0]

        dp_ref[:, 0:Dc] = dcb
        dp_ref[:, Dc:2 * Dc] = dv * cx
        dp_ref[:, 2 * Dc:3 * Dc] = dv * cc
        dp_ref[:, 3 * Dc:3 * Dc + D] = dgs
        dp_ref[:, 3 * Dc + D:3 * Dc + 2 * D] = dgc

    S = jax.ShapeDtypeStruct
    return pl.pallas_call(
        body,
        out_shape=[S((N, Ds), F32), S((N, IN - Ds), F32), S((D, D), F32), S((Ds, D), F32), S((Dc, D), F32),
                   S((Ds, Ds), F32), S((1, Ds), F32), S((SUBLANES, Dc), F32), S((Bl, 1, D), F32)],
        grid=(nblk,),
        in_specs=_mix_in_specs(TB, D, Ds, Dc, bps, rev) + [
            pl.BlockSpec((TB, D), lambda i: (rev(i), 0)), _seq_spec(D, bps, nblk),
            _const_spec((Ds, Ds)), _const_spec((1, Ds)), _const_spec((SUBLANES, Dc)),
            _const_spec((Ds, D)), _const_spec((Dc, D)), _const_spec((D, D))],
        out_specs=[pl.BlockSpec((TB, Ds), lambda i: (rev(i), 0)), pl.BlockSpec((TB, IN - Ds), lambda i: (rev(i), 0)),
                   _const_spec((D, D)), _const_spec((Ds, D)), _const_spec((Dc, D)), _const_spec((Ds, Ds)),
                   _const_spec((1, Ds)), _const_spec((SUBLANES, Dc)), _seq_spec(D, bps, nblk)],
        scratch_shapes=[pltpu.VMEM((SUBLANES, Dc), F32)],
        compiler_params=_cparams(dimension_semantics=("arbitrary",)), name="mix_bwd",
    )(p2, p2, p2, p2, p2, p2, p2, ylin2, dx1, g1, wglu, bglu, cw, wps, wpc, wout)


def _mlp_fwd_bwd(x1, tgt, sh2, sc2, g2, n2g, fg, w1, w2):
    N, D = x1.shape
    Dff = w1.shape[1]
    Bl = sh2.shape[0]
    TB = _row_block(N // Bl)
    bps = (N // Bl) // TB

    def body(x1_ref, t_ref, sh_ref, sc_ref, g2_ref, n2_ref, fg_ref, w1_ref, w2_ref,
             dx1_ref, h2_ref, da_ref, sq_ref, df_ref, loss_ref, gfg_ref, gn2_ref, dsh_ref, dsc_ref, dg2_ref):
        i = pl.program_id(0)

        @pl.when(i == 0)
        def _():
            for ref in (loss_ref, gfg_ref, gn2_ref):
                ref[...] = jnp.zeros_like(ref)

        @pl.when(i % bps == 0)
        def _():
            for ref in (dsh_ref, dsc_ref, dg2_ref):
                ref[...] = jnp.zeros_like(ref)

        x1v = x1_ref[...]
        sc, g2v, n2 = sc_ref[...], g2_ref[...], n2_ref[...]
        xh2, r2 = _rms_fwd(x1v)
        xn2 = xh2 * n2
        h2 = (xn2 * (1.0 + sc) + sh_ref[...]).astype(BF16)
        a = _dot(h2, w1_ref[...])
        ra = jnp.maximum(a, 0.0)
        sq = (ra * ra).astype(BF16)
        fv = _dot(sq, w2_ref[...])
        x2 = x1v + g2v * fv
        xh3, r3 = _rms_fwd(x2)
        err = xh3 * fg_ref[...] - t_ref[...]
        loss_ref[...] += 0.5 * jnp.sum(jnp.mean(err * err, axis=-1, keepdims=True), axis=0, keepdims=True)
        dy = err * (1.0 / D)
        gfg_ref[...] += jnp.sum(dy * xh3, axis=0, keepdims=True)
        dx2 = _rms_bwd(dy * fg_ref[...], xh3, r3)
        dg2_ref[...] += jnp.sum(dx2 * fv, axis=0, keepdims=True)
        df = (g2v * dx2).astype(BF16)
        dsq = _dot_nt(df, w2_ref[...])
        da = (2.0 * ra * dsq).astype(BF16)
        dh2 = _dot_nt(da, w1_ref[...])
        dsh_ref[...] += jnp.sum(dh2, axis=0, keepdims=True)
        dsc_ref[...] += jnp.sum(dh2 * xn2, axis=0, keepdims=True)
        dxn2 = dh2 * (1.0 + sc)
        gn2_ref[...] += jnp.sum(dxn2 * xh2, axis=0, keepdims=True)
        dx1_ref[...] = dx2 + _rms_bwd(dxn2 * n2, xh2, r2)
        h2_ref[...] = h2
        da_ref[...] = da
        sq_ref[...] = sq
        df_ref[...] = df

    S = jax.ShapeDtypeStruct
    row = lambda w: pl.BlockSpec((TB, w), lambda i: (i, 0))
    return pl.pallas_call(
        body,
        out_shape=[S((N, D), F32), S((N, D), BF16), S((N, Dff), BF16), S((N, Dff), BF16), S((N, D), BF16),
                   S((1, 1), F32), S((1, D), F32), S((1, D), F32), S((Bl, 1, D), F32), S((Bl, 1, D), F32), S((Bl, 1, D), F32)],
        grid=(N // TB,),
        in_specs=[row(D), row(D), _seq_spec(D, bps), _seq_spec(D, bps), _seq_spec(D, bps),
                  _const_spec((1, D)), _const_spec((1, D)), _const_spec((D, Dff)), _const_spec((Dff, D))],
        out_specs=[row(D), row(D), row(Dff), row(Dff), row(D), _const_spec((1, 1)), _const_spec((1, D)), _const_spec((1, D)),
                   _seq_spec(D, bps), _seq_spec(D, bps), _seq_spec(D, bps)],
        compiler_params=_cparams(dimension_semantics=("arbitrary",)), name="mlp_fwd_bwd",
    )(x1, tgt, sh2, sc2, g2, n2g, fg, w1, w2)


def _grad_w(a, b, name):
    N, K1 = a.shape
    K2 = b.shape[1]
    t1 = 1024 if K1 % 1024 == 0 else K1
    t2 = 1024 if K2 % 1024 == 0 else K2
    tn = 512 if N % 512 == 0 else N

    def body(a_ref, b_ref, o_ref):
        @pl.when(pl.program_id(2) == 0)
        def _():
            o_ref[...] = jnp.zeros_like(o_ref)

        o_ref[...] += _dot_tn(a_ref[...], b_ref[...])

    return pl.pallas_call(
        body, out_shape=jax.ShapeDtypeStruct((K1, K2), F32), grid=(K1 // t1, K2 // t2, N // tn),
        in_specs=[pl.BlockSpec((tn, t1), lambda i, j, k: (k, i)), pl.BlockSpec((tn, t2), lambda i, j, k: (k, j))],
        out_specs=pl.BlockSpec((t1, t2), lambda i, j, k: (i, j)),
        compiler_params=_cparams(dimension_semantics=("arbitrary", "arbitrary", "arbitrary")), name=name)(a, b)


def _inproj_bwd(x2, dx1, du, dprest, sh1, sc1, n1g, w_in):
    N, D = x2.shape
    IN = w_in.shape[1]
    Ds = du.shape[1]
    Bl = sh1.shape[0]
    TB = _row_block(N // Bl)
    bps = (N // Bl) // TB

    def body(x_ref, dx1_ref, du_ref, dpr_ref, sh_ref, sc_ref, g_ref, w_ref,
             gx_ref, h_ref, dp_ref, gn1_ref, dsh_ref, dsc_ref):
        i = pl.program_id(0)

        @pl.when(i == 0)
        def _():
            gn1_ref[...] = jnp.zeros_like(gn1_ref)

        @pl.when(i % bps == 0)
        def _():
            dsh_ref[...] = jnp.zeros_like(dsh_ref)
            dsc_ref[...] = jnp.zeros_like(dsc_ref)

        sc, n1 = sc_ref[...], g_ref[...]
        xh, r = _rms_fwd(x_ref[...])
        xn = xh * n1
        h_ref[...] = (xn * (1.0 + sc) + sh_ref[...]).astype(BF16)
        dub = du_ref[...].astype(BF16)
        dprb = dpr_ref[...].astype(BF16)
        dp_ref[:, 0:Ds] = dub
        dp_ref[:, Ds:IN] = dprb
        dh = _dot_nt(dub, w_ref[:, 0:Ds]) + _dot_nt(dprb, w_ref[:, Ds:IN])
        dsh_ref[...] += jnp.sum(dh, axis=0, keepdims=True)
        dsc_ref[...] += jnp.sum(dh * xn, axis=0, keepdims=True)
        dxn = dh * (1.0 + sc)
        gn1_ref[...] += jnp.sum(dxn * xh, axis=0, keepdims=True)
        gx_ref[...] = dx1_ref[...] + _rms_bwd(dxn * n1, xh, r)

    S = jax.ShapeDtypeStruct
    row = lambda w: pl.BlockSpec((TB, w), lambda i: (i, 0))
    return pl.pallas_call(
        body,
        out_shape=[S((N, D), F32), S((N, D), BF16), S((N, IN), BF16), S((1, D), F32), S((Bl, 1, D), F32), S((Bl, 1, D), F32)],
        grid=(N // TB,),
        in_specs=[row(D), row(D), row(Ds), row(IN - Ds), _seq_spec(D, bps), _seq_spec(D, bps),
                  _const_spec((1, D)), _const_spec((D, IN))],
        out_specs=[row(D), row(D), row(IN), _const_spec((1, D)), _seq_spec(D, bps), _seq_spec(D, bps)],
        compiler_params=_cparams(dimension_semantics=("arbitrary",)), name="inproj_bwd",
    )(x2, dx1, du, dprest, sh1, sc1, n1g, w_in)


def _diag_blocks_from_groups(m, nb):
    G, a, b = m.shape
    gb = G // nb
    eye = jnp.eye(gb, dtype=m.dtype)
    mm = m.reshape(nb, gb, a, b)
    return (mm[:, :, :, None, :] * eye[None, :, None, :, None]).reshape(nb, gb * a, gb * b)


def _groups_from_diag_blocks(d, G, a, b):
    nb = d.shape[0]
    gb = G // nb
    dd = d.reshape(nb, gb, a, gb, b)
    idx = jnp.arange(gb)
    return dd[:, idx, :, idx, :].transpose(1, 0, 2, 3).reshape(G, a, b)


def _pad_rows(v, rows):
    return jnp.concatenate([v, jnp.zeros((rows - v.shape[0],) + v.shape[1:], v.dtype)], axis=0)


def _pack(vs):
    flat = jnp.concatenate([v.reshape(-1) for v in vs])
    n = flat.shape[0]
    tile = SUBLANES * LANES
    npad = -(-n // tile) * tile
    flat = jnp.concatenate([flat, jnp.zeros((npad - n,), flat.dtype)])
    return flat.reshape(npad // LANES, LANES)


def _unpack(packed, shapes):
    flat = packed.reshape(-1)
    out, off = [], 0
    for s in shapes:
        n = 1
        for d in s:
            n *= d
        out.append(flat[off:off + n].reshape(s))
        off += n
    return out


def kernel(x, c, norm1_g, norm2_g, w_ada, b_ada, w_in, lam_re, lam_im, log_dt, b_re, b_im, c_re, c_im, d_skip, w_glu, b_glu, conv_w, w_proj_ssm, w_proj_conv, w_out, w_ff1, w_ff2, final_g, loss_target, m_norm1_g, m_norm2_g, m_w_ada, m_b_ada, m_w_in, m_lam_re, m_lam_im, m_log_dt, m_b_re, m_b_im, m_c_re, m_c_im, m_d_skip, m_w_glu, m_b_glu, m_conv_w, m_w_proj_ssm, m_w_proj_conv, m_w_out, m_w_ff1, m_w_ff2, m_final_g, v_norm1_g, v_norm2_g, v_w_ada, v_b_ada, v_w_in, v_lam_re, v_lam_im, v_log_dt, v_b_re, v_b_im, v_c_re, v_c_im, v_d_skip, v_w_glu, v_b_glu, v_conv_w, v_w_proj_ssm, v_w_proj_conv, v_w_out, v_w_ff1, v_w_ff2, v_final_g):
    Bl, L, D = x.shape
    N = Bl * L
    Ds = Dc = D // 2
    G, H, P = Ds // SSM_GROUP, SSM_GROUP, SSM_STATE
    GP = G * P
    NP = GP // LANES
    nb = _s5_dims(Bl, L, Ds)[4]
    IN = Ds + 3 * Dc + 2 * D
    ax, ay, ac = _mesh_pos()
    q = 2 * ax + ay
    dev = 2 * q + ac

    big_names = ["w_in", "w_ff1", "w_ff2", "w_out", "w_proj_ssm", "w_proj_conv", "w_glu"]
    big_w = dict(w_in=w_in[0], w_ff1=w_ff1[0], w_ff2=w_ff2[0], w_out=w_out[0],
                 w_proj_ssm=w_proj_ssm[0], w_proj_conv=w_proj_conv[0], w_glu=w_glu[0])
    big_axis = dict(w_in=1, w_ff1=1, w_ff2=0, w_out=0, w_proj_ssm=1, w_proj_conv=1, w_glu=0)
    axes = [big_axis[k] for k in big_names]
    shard_shapes = [big_w[k].shape for k in big_names]
    pos = jnp.stack([q, ac]).astype(jnp.int32)
    own_only = [_cast_into_full(big_w[k], big_axis[k], pos, "cast_" + k) for k in big_names]
    full = dict(zip(big_names, _allgather_weights(own_only, axes, shard_shapes)))

    Dcs = conv_w.shape[2]
    first = _allgather8(_pack([c, conv_w[0]]), "allgather_c").reshape(N_DEV, -1)
    c_all = first[:, :Bl * D].reshape(N_DEV * Bl, D)
    cw = first[0::2, Bl * D:Bl * D + 3 * Dcs].reshape(N_CHIPS, 3, Dcs).transpose(1, 0, 2).reshape(3, Dc)
    cw8 = _pad_rows(cw, SUBLANES)
    Ca = w_ada.shape[2]
    b_ada_sh = lax.dynamic_slice_in_dim(b_ada, q * Ca, Ca, axis=1)
    mod_part = _ada_fwd(c_all, w_ada[0], b_ada_sh)
    mod_g = _allgather8(mod_part, "allgather_mod")
    mod_all = mod_g[0::2].transpose(1, 0, 2).reshape(N_DEV * Bl, N_CHIPS * Ca)
    mod = lax.dynamic_slice_in_dim(mod_all, dev * Bl, Bl, axis=0)
    sh1, sc1, g1, sh2, sc2, g2 = [mod[:, k * D:(k + 1) * D].reshape(Bl, 1, D) for k in range(6)]

    ldt_c = log_dt[0].reshape(G, 1)
    bt_r = b_re[0].transpose(2, 0, 1)
    bt_i = b_im[0].transpose(2, 0, 1)
    lbr, lbi, bbt_r, bbt_i = _s5_disc(lam_re[0], lam_im[0], ldt_c, bt_r, bt_i)
    lam_r_p = lbr.reshape(NP, LANES)
    lam_i_p = lbi.reshape(NP, LANES)
    Bm_r = _diag_blocks_from_groups(bbt_r.transpose(1, 0, 2), nb).astype(BF16)
    Bm_i = _diag_blocks_from_groups(bbt_i.transpose(1, 0, 2), nb).astype(BF16)
    Cm_r = _diag_blocks_from_groups(c_re[0].transpose(0, 2, 1), nb).astype(BF16)
    Cm_i = _diag_blocks_from_groups(-c_im[0].transpose(0, 2, 1), nb).astype(BF16)

    x2 = x.reshape(N, D)
    p2 = _inproj_fwd(x2, sh1, sc1, norm1_g, full["w_in"])
    p3 = p2.reshape(Bl, L, IN)
    Sr, Si, Sb_r, Sb_i, ylin3 = _s5_fwd(p3, Bm_r, Bm_i, Cm_r, Cm_i, lam_r_p, lam_i_p, d_skip)
    ylin2 = ylin3.reshape(N, Ds)
    mix_w = (full["w_glu"], b_glu, cw8, full["w_proj_ssm"], full["w_proj_conv"], full["w_out"])
    x1 = _mix_fwd(p2, ylin2, x2, g1, *mix_w)

    (dx1, h2b, dab, sqb, dfb, loss_p, g_fg, g_n2, dsh2, dsc2, dg2) = _mlp_fwd_bwd(
        x1, loss_target.reshape(N, D), sh2, sc2, g2, norm2_g, final_g.reshape(1, D), full["w_ff1"], full["w_ff2"])
    gw_ff1 = _grad_w(h2b, dab, "grad_w_ff1")
    gw_ff2 = _grad_w(sqb, dfb, "grad_w_ff2")

    (dyl2, dprest, gw_out, gw_ps, gw_pc, gw_glu, gb_glu, gcw8, dg1) = _mix_bwd(p2, ylin2, dx1, g1, *mix_w)
    (du3, dBm_r, dBm_i, dCm_r, dCm_i, dlam_r_p, dlam_i_p, g_dsk) = _s5_bwd(
        dyl2.reshape(Bl, L, Ds), p3, Sr, Si, Sb_r, Sb_i, Bm_r, Bm_i, Cm_r, Cm_i, lam_r_p, lam_i_p, d_skip)
    (grad_x2, hb, dpb, g_n1, dsh1, dsc1) = _inproj_bwd(x2, dx1, du3.reshape(N, Ds), dprest, sh1, sc1, norm1_g, full["w_in"])
    gw_in = _grad_w(hb, dpb, "grad_w_in")

    dbbt_r = _groups_from_diag_blocks(dBm_r, G, H, P).transpose(1, 0, 2)
    dbbt_i = _groups_from_diag_blocks(dBm_i, G, H, P).transpose(1, 0, 2)
    dc_re = _groups_from_diag_blocks(dCm_r, G, P, H).transpose(0, 2, 1)
    dc_im = -_groups_from_diag_blocks(dCm_i, G, P, H).transpose(0, 2, 1)
    dmod = jnp.concatenate([dsh1, dsc1, dg1, dsh2, dsc2, dg2], axis=-1).reshape(Bl, 6 * D)
    small = [g_n1, g_n2, g_fg, g_dsk, gb_glu, gcw8[:3], dlam_r_p, dlam_i_p, dbbt_r, dbbt_i, dc_re, dc_im]
    small_shapes = [v.shape for v in small]
    n_small = sum(int(v.size) for v in small)
    gathered = _allgather8(_pack(small + [dmod]), "allgather_small")
    red = _unpack(_sum_devices(gathered, "sum_small"), small_shapes)
    (r_n1, r_n2, r_fg, r_dsk, r_bglu, r_cw, r_dlr, r_dli, r_dbr, r_dbi, r_cre, r_cim) = red
    dmod_all = gathered.reshape(N_DEV, -1)[:, n_small:n_small + Bl * 6 * D].reshape(N_DEV * Bl, 6 * D)
    gw_ada, gb_ada = _ada_bwd(c_all, lax.dynamic_slice_in_dim(dmod_all, q * Ca, Ca, axis=1), dmod_all)
    g_lr, g_li, g_ldt, g_bt_r, g_bt_i = _s5_disc_bwd(lam_re[0], lam_im[0], ldt_c, bt_r, bt_i,
                                                   r_dlr.reshape(G, P), r_dli.reshape(G, P), r_dbr, r_dbi)

    grads_full = [gw_in, gw_ff1, gw_ff2, gw_out, gw_ps, gw_pc, gw_glu]
    theirs = _rs_sibling_exchange(grads_full, axes, shard_shapes)
    parts = [_presum(g, t, a, s, pos, "presum_" + k) for g, t, a, s, k in zip(grads_full, theirs, axes, shard_shapes, big_names)]
    recv = _rs_chip_scatter(parts)
    halves = [_sum_chips(p, r, pos, "sum_" + k) for p, r, k in zip(parts, recv, big_names)]
    reduced = dict(zip(big_names, _rs_sibling_share(halves)))

    grads = dict(
        norm1_g=r_n1, norm2_g=r_n2, w_ada=gw_ada, b_ada=gb_ada, lam_re=g_lr, lam_im=g_li, log_dt=g_ldt.reshape(1, G),
        b_re=g_bt_r.transpose(1, 2, 0), b_im=g_bt_i.transpose(1, 2, 0), c_re=r_cre, c_im=r_cim, d_skip=r_dsk,
        b_glu=r_bglu, conv_w=lax.dynamic_slice_in_dim(r_cw, q * Dcs, Dcs, axis=1), final_g=r_fg, **reduced)
    weights = dict(norm1_g=norm1_g, norm2_g=norm2_g, w_ada=w_ada, b_ada=b_ada, w_in=w_in, lam_re=lam_re, lam_im=lam_im,
                   log_dt=log_dt, b_re=b_re, b_im=b_im, c_re=c_re, c_im=c_im, d_skip=d_skip, w_glu=w_glu, b_glu=b_glu,
                   conv_w=conv_w, w_proj_ssm=w_proj_ssm, w_proj_conv=w_proj_conv, w_out=w_out, w_ff1=w_ff1, w_ff2=w_ff2,
                   final_g=final_g)
    m_in = dict(norm1_g=m_norm1_g, norm2_g=m_norm2_g, w_ada=m_w_ada, b_ada=m_b_ada, w_in=m_w_in, lam_re=m_lam_re,
                lam_im=m_lam_im, log_dt=m_log_dt, b_re=m_b_re, b_im=m_b_im, c_re=m_c_re, c_im=m_c_im, d_skip=m_d_skip,
                w_glu=m_w_glu, b_glu=m_b_glu, conv_w=m_conv_w, w_proj_ssm=m_w_proj_ssm, w_proj_conv=m_w_proj_conv,
                w_out=m_w_out, w_ff1=m_w_ff1, w_ff2=m_w_ff2, final_g=m_final_g)
    v_in = dict(norm1_g=v_norm1_g, norm2_g=v_norm2_g, w_ada=v_w_ada, b_ada=v_b_ada, w_in=v_w_in, lam_re=v_lam_re,
                lam_im=v_lam_im, log_dt=v_log_dt, b_re=v_b_re, b_im=v_b_im, c_re=v_c_re, c_im=v_c_im, d_skip=v_d_skip,
                w_glu=v_w_glu, b_glu=v_b_glu, conv_w=v_conv_w, w_proj_ssm=v_w_proj_ssm, w_proj_conv=v_w_proj_conv,
                w_out=v_w_out, w_ff1=v_w_ff1, w_ff2=v_w_ff2, final_g=v_final_g)
    names = list(weights)
    grads = {k: grads[k].reshape(weights[k].shape) for k in names}

    big_upd = big_names + ["w_ada"]
    delta, new_m, new_v = {}, {}, {}
    for k in big_upd:
        shp = weights[k].shape
        two_d = lambda a: a.reshape(shp[-2], shp[-1])
        d_, m_, v_ = _adamw(two_d(weights[k]), two_d(grads[k]), two_d(m_in[k]), two_d(v_in[k]), "adamw_" + k)
        delta[k], new_m[k], new_v[k] = d_.reshape(shp), m_.reshape(shp), v_.reshape(shp)
    small_upd = [k for k in names if k not in big_upd]
    shapes_s = [weights[k].shape for k in small_upd]
    d_, m_, v_ = _adamw(_pack([weights[k] for k in small_upd]), _pack([grads[k] for k in small_upd]),
                        _pack([m_in[k] for k in small_upd]), _pack([v_in[k] for k in small_upd]), "adamw_small")
    for k, dd, mm, vv in zip(small_upd, _unpack(d_, shapes_s), _unpack(m_, shapes_s), _unpack(v_, shapes_s)):
        delta[k], new_m[k], new_v[k] = dd, mm, vv

    loss = lax.psum(loss_p[0, 0], ("x", "y", "c"))
    grad_x = grad_x2.reshape(Bl, L, D)
    return (loss, grad_x, *[grads[k] for k in names], *[delta[k] for k in names],
            *[new_m[k] for k in names], *[new_v[k] for k in names])
```

```python
import functools

import jax
import jax.numpy as jnp
from jax import lax
from jax.experimental import pallas as pl
from jax.experimental.pallas import tpu as pltpu

F32 = jnp.float32
BF16 = jnp.bfloat16
MESH = pl.DeviceIdType.MESH
N_CHIPS = 4
N_DEV = 8
LANES = 128
SUBLANES = 8
V7X_VMEM_BYTES = 64 * 1024 * 1024
VMEM_LIMIT = V7X_VMEM_BYTES - 6 * 1024 * 1024
SSM_GROUP = 16
SSM_STATE = 64
S5_ROW_PAD = 4
RMS_EPS = 1e-6
ADAM_LR, ADAM_B1, ADAM_B2, ADAM_EPS, ADAM_WD, ADAM_STEP = 0.001, 0.9, 0.999, 1e-08, 0.01, 10

ANY = pl.BlockSpec(memory_space=pl.ANY)
VMEM_SPEC = pl.BlockSpec(memory_space=pltpu.VMEM)


def _cparams(**kw):
    return pltpu.CompilerParams(vmem_limit_bytes=VMEM_LIMIT, **kw)


def _dot(a, b):
    return jnp.dot(a, b, preferred_element_type=F32)


def _dot_nt(a, b):
    return lax.dot_general(a, b, (((1,), (1,)), ((), ())), preferred_element_type=F32)


def _dot_tn(a, b):
    return lax.dot_general(a, b, (((0,), (0,)), ((), ())), preferred_element_type=F32)


def _mesh_pos():
    return lax.axis_index("x"), lax.axis_index("y"), lax.axis_index("c")


def _allgather8(v, name):
    r, c = v.shape

    def body(x_ref, out_ref, send_sems, recv_sems, local_sem):
        x, y, cc = _mesh_pos()
        me, sibling = (x, y, cc), (x, y, 1 - cc)
        chips = [(1 - x, y), (x, 1 - y), (1 - x, 1 - y)]

        def slot(px, py, pc):
            return out_ref.at[4 * px + 2 * py + pc]

        def copy(k, block, to, src=None):
            return pltpu.make_async_remote_copy(
                src_ref=slot(*block) if src is None else src, dst_ref=slot(*block),
                send_sem=send_sems.at[k], recv_sem=recv_sems.at[k], device_id=to, device_id_type=MESH)

        mine = pltpu.make_async_copy(x_ref, slot(*me), local_sem)
        mine.start()
        first = [copy(0, me, sibling, src=x_ref)]
        first += [copy(1 + j, me, (*chip, cc), src=x_ref) for j, chip in enumerate(chips)]
        for cp in first:
            cp.start()
        passed = [copy(4 + j, (*chip, cc), sibling) for j, chip in enumerate(chips)]
        for j, chip in enumerate(chips):
            copy(1 + j, (*chip, cc), me).wait_recv()
            passed[j].start()
        copy(0, sibling, me).wait_recv()
        for j, chip in enumerate(chips):
            copy(4 + j, (*chip, 1 - cc), me).wait_recv()
        for cp in first + passed:
            cp.wait_send()
        mine.wait()

    return pl.pallas_call(
        body, out_shape=jax.ShapeDtypeStruct((N_DEV, r, c), v.dtype),
        in_specs=[VMEM_SPEC], out_specs=VMEM_SPEC,
        scratch_shapes=[pltpu.SemaphoreType.DMA((7,)), pltpu.SemaphoreType.DMA((7,)), pltpu.SemaphoreType.DMA],
        name=name)(v)


def _shard_region(ref, axis, shard_shape, q, half):
    R, C = shard_shape
    r0, nr = (0, R) if half is None else (half * (R // 2), R // 2)
    if axis == 1:
        return ref.at[pl.ds(r0, nr), pl.ds(q * C, C)]
    return ref.at[pl.ds(q * R + r0, nr), :]


def _allgather_weights(fulls, axes, shapes):
    n = len(fulls)

    def body(*refs):
        start, finish = _allgather_weights_steps(refs[n:2 * n], axes, shapes, *refs[2 * n:])
        start()
        finish()

    return pl.pallas_call(
        body, out_shape=[jax.ShapeDtypeStruct(f.shape, f.dtype) for f in fulls], in_specs=[ANY] * n, out_specs=[ANY] * n,
        input_output_aliases={i: i for i in range(n)},
        scratch_shapes=_allgather_weights_sems(n), name="allgather_weights")(*fulls)


def _allgather_weights_sems(n):
    return [pltpu.SemaphoreType.DMA((3 * n,))] * 4


def _allgather_weights_steps(outs, axes, shapes, send_s, recv_s, fsend_s, frecv_s):
    n = len(outs)
    x, y, c = _mesh_pos()
    q = 2 * x + y
    sibling = (x, y, 1 - c)
    chips = [(1 - x, y), (x, 1 - y), (1 - x, 1 - y)]

    def region(i, qq, half):
        return _shard_region(outs[i], axes[i], shapes[i], qq, half)

    def remote(src, dst, ss, rs, to):
        return pltpu.make_async_remote_copy(src_ref=src, dst_ref=dst, send_sem=ss, recv_sem=rs,
                                            device_id=to, device_id_type=MESH)

    def ici(i, j, qq):
        cx, cy = chips[j]
        reg = region(i, qq, c)
        return remote(reg, reg, send_s.at[3 * i + j], recv_s.at[3 * i + j], (cx, cy, c))

    def d2d(i, j, half):
        cx, cy = chips[j]
        reg = region(i, 2 * cx + cy, half)
        return remote(reg, reg, fsend_s.at[3 * i + j], frecv_s.at[3 * i + j], sibling)

    def start():
        for i in range(n):
            for j in range(3):
                ici(i, j, q).start()

    def finish():
        for i in range(n):
            for j, (cx, cy) in enumerate(chips):
                ici(i, j, 2 * cx + cy).wait_recv()
                d2d(i, j, c).start()
        for i in range(n):
            for j in range(3):
                d2d(i, j, 1 - c).wait_recv()
        for i in range(n):
            for j in range(3):
                ici(i, j, q).wait_send()
                d2d(i, j, c).wait_send()

    return start, finish


def _rs_sibling_exchange(grads, axes, shapes):
    n = len(grads)

    def body(*refs):
        ins, theirs = refs[:n], refs[n:2 * n]
        send_s, recv_s = refs[2 * n:]
        x, y, c = _mesh_pos()
        sibling = (x, y, 1 - c)
        cps = []
        for i in range(n):
            for qq in range(N_CHIPS):
                k = N_CHIPS * i + qq
                rc = pltpu.make_async_remote_copy(
                    src_ref=_shard_region(ins[i], axes[i], shapes[i], qq, 1 - c), dst_ref=theirs[i].at[qq],
                    send_sem=send_s.at[k], recv_sem=recv_s.at[k], device_id=sibling, device_id_type=MESH)
                rc.start()
                cps.append(rc)
        for rc in cps:
            rc.wait()

    stacked = [jax.ShapeDtypeStruct((N_CHIPS, R // 2, C), F32) for (R, C) in shapes]
    return pl.pallas_call(
        body, out_shape=stacked, in_specs=[ANY] * n, out_specs=[ANY] * n,
        scratch_shapes=[pltpu.SemaphoreType.DMA((N_CHIPS * n,))] * 2, name="rs_sibling_exchange")(*grads)


def _rs_chip_scatter(parts):
    n = len(parts)

    def body(*refs):
        ins, outs = refs[:n], refs[n:2 * n]
        send_s, recv_s = refs[2 * n:]
        x, y, c = _mesh_pos()
        chips = [(1 - x, y), (x, 1 - y), (1 - x, 1 - y)]
        sends = []
        for i in range(n):
            for j, (cx, cy) in enumerate(chips):
                cp = pltpu.make_async_remote_copy(
                    src_ref=ins[i].at[2 * cx + cy], dst_ref=outs[i].at[j],
                    send_sem=send_s.at[3 * i + j], recv_sem=recv_s.at[3 * i + j],
                    device_id=(cx, cy, c), device_id_type=MESH)
                cp.start()
                sends.append(cp)
        for cp in sends:
            cp.wait()

    return pl.pallas_call(
        body, out_shape=[jax.ShapeDtypeStruct((3,) + p.shape[1:], p.dtype) for p in parts],
        in_specs=[ANY] * n, out_specs=[ANY] * n,
        scratch_shapes=[pltpu.SemaphoreType.DMA((3 * n,))] * 2, name="rs_chip_scatter")(*parts)


def _rs_sibling_share(fulls):
    n = len(fulls)

    def body(*refs):
        outs = refs[n:2 * n]
        send_s, recv_s = refs[2 * n:]
        x, y, c = _mesh_pos()
        sibling = (x, y, 1 - c)
        cps = []
        for i in range(n):
            rh = fulls[i].shape[0] // 2
            mine = outs[i].at[pl.ds(c * rh, rh), :]
            rc = pltpu.make_async_remote_copy(src_ref=mine, dst_ref=mine, send_sem=send_s.at[i], recv_sem=recv_s.at[i],
                                              device_id=sibling, device_id_type=MESH)
            rc.start()
            cps.append(rc)
        for i, rc in enumerate(cps):
            rh = fulls[i].shape[0] // 2
            other = outs[i].at[pl.ds((1 - c) * rh, rh), :]
            pltpu.make_async_remote_copy(src_ref=other, dst_ref=other, send_sem=send_s.at[i], recv_sem=recv_s.at[i],
                                         device_id=sibling, device_id_type=MESH).wait_recv()
            rc.wait_send()

    return pl.pallas_call(
        body, out_shape=[jax.ShapeDtypeStruct(f.shape, f.dtype) for f in fulls],
        in_specs=[ANY] * n, out_specs=[ANY] * n, input_output_aliases={i: i for i in range(n)},
        scratch_shapes=[pltpu.SemaphoreType.DMA((n,))] * 2, name="rs_sibling_share")(*fulls)


def _row_block(rows, target=256):
    return target if rows % target == 0 else rows


def _cast_into_full(w, axis, pos, name):
    R, C = w.shape
    rb = _row_block(R)
    nrb = R // rb
    if axis == 1:
        shape, omap = (R, N_CHIPS * C), lambda i, s: (i, s[0])
    else:
        shape, omap = (N_CHIPS * R, C), lambda i, s: (s[0] * nrb + i, 0)

    def body(s_ref, w_ref, o_ref):
        o_ref[...] = w_ref[...].astype(BF16)

    return pl.pallas_call(
        body, out_shape=jax.ShapeDtypeStruct(shape, BF16),
        grid_spec=pltpu.PrefetchScalarGridSpec(
            num_scalar_prefetch=1, grid=(nrb,), in_specs=[pl.BlockSpec((rb, C), lambda i, s: (i, 0))],
            out_specs=pl.BlockSpec((rb, C), omap)),
        name=name)(pos, w)


def _presum(g, theirs, axis, shape, pos, name):
    R, C = shape
    Rh = R // 2
    rb = _row_block(Rh)
    nr = Rh // rb
    if axis == 1:
        gmap = lambda k, i, s: (s[1] * nr + i, k)
    else:
        gmap = lambda k, i, s: (k * (R // rb) + s[1] * nr + i, 0)

    def body(s_ref, g_ref, t_ref, o_ref):
        o_ref[...] = (g_ref[...] + t_ref[...]).astype(BF16)

    spec = pl.BlockSpec((None, rb, C), lambda k, i, s: (k, i, 0))
    return pl.pallas_call(
        body, out_shape=jax.ShapeDtypeStruct((N_CHIPS, Rh, C), BF16),
        grid_spec=pltpu.PrefetchScalarGridSpec(
            num_scalar_prefetch=1, grid=(N_CHIPS, nr), in_specs=[pl.BlockSpec((rb, C), gmap), spec], out_specs=spec),
        name=name)(pos, g, theirs)


def _sum_chips(own, recv, pos, name):
    _, Rh, C = own.shape
    rb = _row_block(Rh)
    nr = Rh // rb

    def body(s_ref, o_ref, r_ref, out_ref):
        acc = o_ref[...].astype(F32)
        for k in range(3):
            acc = acc + r_ref[k].astype(F32)
        out_ref[...] = acc

    return pl.pallas_call(
        body, out_shape=jax.ShapeDtypeStruct((2 * Rh, C), F32),
        grid_spec=pltpu.PrefetchScalarGridSpec(
            num_scalar_prefetch=1, grid=(nr,),
            in_specs=[pl.BlockSpec((None, rb, C), lambda i, s: (s[0], i, 0)), pl.BlockSpec((3, rb, C), lambda i, s: (0, i, 0))],
            out_specs=pl.BlockSpec((rb, C), lambda i, s: (s[1] * nr + i, 0))),
        name=name)(pos, own, recv)


def _sum_devices(parts, name):
    K, R, C = parts.shape

    def body(p_ref, o_ref):
        acc = p_ref[0]
        for k in range(1, K):
            acc = acc + p_ref[k]
        o_ref[...] = acc

    return pl.pallas_call(body, out_shape=jax.ShapeDtypeStruct((R, C), F32), name=name)(parts)


def _adamw(w, g, m, v, name):
    R, C = w.shape
    rb = _row_block(R)

    def body(w_ref, g_ref, m_ref, v_ref, d_ref, nm_ref, nv_ref):
        gv = g_ref[...]
        nm = ADAM_B1 * m_ref[...] + (1.0 - ADAM_B1) * gv
        nv = ADAM_B2 * v_ref[...] + (1.0 - ADAM_B2) * (gv * gv)
        m_hat = nm / (1.0 - ADAM_B1 ** ADAM_STEP)
        v_hat = nv / (1.0 - ADAM_B2 ** ADAM_STEP)
        d_ref[...] = -ADAM_LR * (m_hat / (jnp.sqrt(v_hat) + ADAM_EPS) + ADAM_WD * w_ref[...])
        nm_ref[...] = nm
        nv_ref[...] = nv

    spec = pl.BlockSpec((rb, C), lambda i: (i, 0))
    return pl.pallas_call(body, out_shape=[jax.ShapeDtypeStruct((R, C), F32)] * 3, grid=(R // rb,),
                          in_specs=[spec] * 4, out_specs=[spec] * 3, name=name)(w, g, m, v)


def _silu(v):
    return v * jax.nn.sigmoid(v)


def _ada_fwd(c_all, w_sh, b_sh):
    S, D = c_all.shape
    Ca = w_sh.shape[1]
    cb = 512 if Ca % 512 == 0 else Ca

    def body(c_ref, w_ref, b_ref, o_ref):
        act = _silu(c_ref[...]).astype(BF16)
        o_ref[...] = _dot(act, w_ref[...].astype(BF16)) + b_ref[...]

    return pl.pallas_call(
        body, out_shape=jax.ShapeDtypeStruct((S, Ca), F32), grid=(Ca // cb,),
        in_specs=[pl.BlockSpec((S, D), lambda j: (0, 0)), pl.BlockSpec((D, cb), lambda j: (0, j)),
                  pl.BlockSpec((1, cb), lambda j: (0, j))],
        out_specs=pl.BlockSpec((S, cb), lambda j: (0, j)), name="ada_fwd")(c_all, w_sh, b_sh)


def _ada_bwd(c_all, dmod_sh, dmod_all):
    S, D = c_all.shape
    Ca = dmod_sh.shape[1]
    C6 = dmod_all.shape[1]

    def body(c_ref, ds_ref, da_ref, gw_ref, gb_ref):
        act = _silu(c_ref[...]).astype(BF16)
        gw_ref[...] = _dot_tn(act, ds_ref[...].astype(BF16))
        gb_ref[...] = jnp.sum(da_ref[...], axis=0, keepdims=True)

    return pl.pallas_call(
        body, out_shape=[jax.ShapeDtypeStruct((D, Ca), F32), jax.ShapeDtypeStruct((1, C6), F32)],
        compiler_params=_cparams(), name="ada_bwd")(c_all, dmod_sh, dmod_all)


def _rms_fwd(xv):
    r = lax.rsqrt(jnp.mean(xv * xv, axis=-1, keepdims=True) + RMS_EPS)
    return xv * r, r


def _rms_bwd(dxh, xh, r):
    return r * (dxh - xh * jnp.mean(dxh * xh, axis=-1, keepdims=True))


def _const_spec(shape):
    nd = len(shape)
    return pl.BlockSpec(shape, lambda *_: (0,) * nd)


def _seq_spec(D, bps, rev_blocks=None):
    if rev_blocks is None:
        return pl.BlockSpec((None, 1, D), lambda i: (i // bps, 0, 0))
    return pl.BlockSpec((None, 1, D), lambda i: ((rev_blocks - 1 - i) // bps, 0, 0))


def _inproj_fwd(x2, sh1, sc1, n1g, w_in, later_w, later_axes, later_shapes):
    N, D = x2.shape
    IN = w_in.shape[1]
    Bl = sh1.shape[0]
    TB = _row_block(N // Bl)
    bps = (N // Bl) // TB
    nblk = N // TB
    n = len(later_w)

    def body(x_ref, sh_ref, sc_ref, g_ref, w_ref, *rest):
        p_ref = rest[n]
        start, finish = _allgather_weights_steps(rest[n + 1:2 * n + 1], later_axes, later_shapes, *rest[2 * n + 1:])
        i = pl.program_id(0)
        pl.when(i == 0)(start)
        xh, _ = _rms_fwd(x_ref[...])
        h = (xh * g_ref[...]) * (1.0 + sc_ref[...]) + sh_ref[...]
        p_ref[...] = _dot(h.astype(BF16), w_ref[...])
        pl.when(i == nblk - 1)(finish)

    res = pl.pallas_call(
        body, out_shape=[jax.ShapeDtypeStruct((N, IN), F32)] + [jax.ShapeDtypeStruct(w.shape, w.dtype) for w in later_w],
        grid=(nblk,),
        in_specs=[pl.BlockSpec((TB, D), lambda i: (i, 0)), _seq_spec(D, bps), _seq_spec(D, bps),
                  _const_spec((1, D)), _const_spec((D, IN))] + [ANY] * n,
        out_specs=[pl.BlockSpec((TB, IN), lambda i: (i, 0))] + [ANY] * n,
        input_output_aliases={5 + k: 1 + k for k in range(n)},
        scratch_shapes=_allgather_weights_sems(n),
        compiler_params=_cparams(dimension_semantics=("arbitrary",)), name="inproj_fwd",
    )(x2, sh1, sc1, n1g, w_in, *later_w)
    return res[0], res[1:]


def _s5_dims(Bl, L, Ds):
    G = Ds // SSM_GROUP
    GP = G * SSM_STATE
    NP = GP // LANES
    T = min(64, L // 2)
    nb = 2 if (Ds // 2) % LANES == 0 else 1
    return G, GP, NP, T, nb


def _s5_disc_math(lr, li, ldt, bt_r, bt_i):
    dt = jnp.exp(ldt)
    er = jnp.exp(lr * dt)
    lbr = er * jnp.cos(li * dt)
    lbi = er * jnp.sin(li * dt)
    den = lr * lr + li * li
    fr = ((lbr - 1.0) * lr + lbi * li) / den
    fi = (lbi * lr - (lbr - 1.0) * li) / den
    return lbr, lbi, fr[None] * bt_r - fi[None] * bt_i, fr[None] * bt_i + fi[None] * bt_r


def _s5_disc(lr, li, ldt, bt_r, bt_i):
    def body(lr_ref, li_ref, ldt_ref, br_ref, bi_ref, o0, o1, o2, o3):
        res = _s5_disc_math(lr_ref[...], li_ref[...], ldt_ref[...], br_ref[...], bi_ref[...])
        for o, v in zip((o0, o1, o2, o3), res):
            o[...] = v

    S = jax.ShapeDtypeStruct
    return pl.pallas_call(body, out_shape=[S(lr.shape, F32)] * 2 + [S(bt_r.shape, F32)] * 2, name="s5_disc")(lr, li, ldt, bt_r, bt_i)


def _s5_disc_bwd(lr, li, ldt, bt_r, bt_i, dlbr, dlbi, dbbr, dbbi):
    def body(lr_ref, li_ref, ldt_ref, br_ref, bi_ref, g0, g1, g2, g3, o0, o1, o2, o3, o4):
        _, vjp = jax.vjp(_s5_disc_math, lr_ref[...], li_ref[...], ldt_ref[...], br_ref[...], bi_ref[...])
        res = vjp((g0[...], g1[...], g2[...], g3[...]))
        for o, v in zip((o0, o1, o2, o3, o4), res):
            o[...] = v

    S = jax.ShapeDtypeStruct
    return pl.pallas_call(body, out_shape=[S(lr.shape, F32)] * 2 + [S(ldt.shape, F32)] + [S(bt_r.shape, F32)] * 2,
                          name="s5_disc_bwd")(lr, li, ldt, bt_r, bt_i, dlbr, dlbi, dbbr, dbbi)


def _s5_fwd(p3, Bm_r, Bm_i, Cm_r, Cm_i, lam_r, lam_i, dsk):
    Bl, L, _ = p3.shape
    Ds = dsk.shape[1]
    G, GP, NP, T, nb = _s5_dims(Bl, L, Ds)
    nT = L // T
    TP = T + S5_ROW_PAD
    BT = Bl * TP
    dsb, gpb, npb = Ds // nb, GP // nb, NP // nb
    PG = min(8, NP)

    def body(u_ref, br_ref, bi_ref, cr_ref, ci_ref, lr_ref, li_ref, dsk_ref, sr_ref, si_ref, sb_r, sb_i, y_ref,
             car_r, car_i, upad, ypad):
        i = pl.program_id(0)

        @pl.when(i == 0)
        def _():
            car_r[...] = jnp.zeros_like(car_r)
            car_i[...] = jnp.zeros_like(car_i)
            upad[...] = jnp.zeros_like(upad)

        for b in range(Bl):
            upad[pl.ds(b * TP, T), :] = u_ref[b]
        u = upad[...]
        ub = u.astype(BF16)
        for blk in range(nb):
            ubb = ub[:, blk * dsb:(blk + 1) * dsb]
            for s_ref, b_ref in ((sr_ref, br_ref), (si_ref, bi_ref)):
                res = _dot(ubb, b_ref[blk])
                for kk in range(npb):
                    k = blk * npb + kk
                    s_ref[pl.ds(k * BT, BT), :] = res[:, kk * LANES:(kk + 1) * LANES]

        for k0 in range(0, NP, PG):
            ks = list(range(k0, k0 + PG))
            lr = [jnp.broadcast_to(lr_ref[pl.ds(k, 1), :], (Bl, LANES)) for k in ks]
            li = [jnp.broadcast_to(li_ref[pl.ds(k, 1), :], (Bl, LANES)) for k in ks]

            def step(t, carry):
                out = []
                for j, k in enumerate(ks):
                    ar, ai = carry[j]
                    idx = pl.ds(k * BT + t, Bl, stride=TP)
                    nr = lr[j] * ar - li[j] * ai + sr_ref[idx, :]
                    ni = lr[j] * ai + li[j] * ar + si_ref[idx, :]
                    sr_ref[idx, :] = nr
                    si_ref[idx, :] = ni
                    out.append((nr, ni))
                return tuple(out)

            init = tuple((car_r[pl.ds(k * SUBLANES, Bl), :], car_i[pl.ds(k * SUBLANES, Bl), :]) for k in ks)
            fin = lax.fori_loop(0, T, step, init, unroll=2)
            for j, k in enumerate(ks):
                car_r[pl.ds(k * SUBLANES, Bl), :] = fin[j][0]
                car_i[pl.ds(k * SUBLANES, Bl), :] = fin[j][1]
        sb_r[...] = car_r[...]
        sb_i[...] = car_i[...]

        for blk in range(nb):
            s_r = jnp.concatenate([sr_ref[pl.ds((blk * npb + kk) * BT, BT), :] for kk in range(npb)], axis=1).astype(BF16)
            s_i = jnp.concatenate([si_ref[pl.ds((blk * npb + kk) * BT, BT), :] for kk in range(npb)], axis=1).astype(BF16)
            cols = slice(blk * dsb, (blk + 1) * dsb)
            ypad[:, cols] = _dot(s_r, cr_ref[blk]) + _dot(s_i, ci_ref[blk]) + dsk_ref[:, cols] * u[:, cols]
        for b in range(Bl):
            y_ref[b] = ypad[pl.ds(b * TP, T), :]

    S = jax.ShapeDtypeStruct
    state = S((nT, NP * BT, LANES), F32)
    bound = S((nT, NP * SUBLANES, LANES), F32)
    sspec = pl.BlockSpec((None, NP * BT, LANES), lambda i: (i, 0, 0))
    bspec = pl.BlockSpec((None, NP * SUBLANES, LANES), lambda i: (i, 0, 0))
    return pl.pallas_call(
        body, out_shape=[state, state, bound, bound, S((Bl, L, Ds), F32)], grid=(nT,),
        in_specs=[pl.BlockSpec((Bl, T, Ds), lambda i: (0, i, 0)),
                  _const_spec((nb, dsb, gpb)), _const_spec((nb, dsb, gpb)),
                  _const_spec((nb, gpb, dsb)), _const_spec((nb, gpb, dsb)),
                  _const_spec((NP, LANES)), _const_spec((NP, LANES)), _const_spec((1, Ds))],
        out_specs=[sspec, sspec, bspec, bspec, pl.BlockSpec((Bl, T, Ds), lambda i: (0, i, 0))],
        scratch_shapes=[pltpu.VMEM((NP * SUBLANES, LANES), F32)] * 2 + [pltpu.VMEM((BT, Ds), F32)] * 2,
        compiler_params=_cparams(dimension_semantics=("arbitrary",)), name="s5_fwd",
    )(p3, Bm_r, Bm_i, Cm_r, Cm_i, lam_r, lam_i, dsk)


def _s5_bwd(dy3, p3, Sr, Si, Sb_r, Sb_i, Bm_r, Bm_i, Cm_r, Cm_i, lam_r, lam_i, dsk):
    Bl, L, Ds = dy3.shape
    G, GP, NP, T, nb = _s5_dims(Bl, L, Ds)
    nT = L // T
    TP = T + S5_ROW_PAD
    BT = Bl * TP
    dsb, gpb, npb = Ds // nb, GP // nb, NP // nb
    PG = min(8, NP)

    def body(dy_ref, u_ref, sr_ref, si_ref, sbr_ref, sbi_ref, br_ref, bi_ref, cr_ref, ci_ref, lr_ref, li_ref, dsk_ref,
             du_ref, dbr_ref, dbi_ref, dcr_ref, dci_ref, dlr_ref, dli_ref, ddsk_ref,
             a_r, a_i, car_r, car_i, acc_r, acc_i, dypad, upad, dupad):
        i = pl.program_id(0)

        @pl.when(i == 0)
        def _():
            for ref in (car_r, car_i, acc_r, acc_i, dbr_ref, dbi_ref, dcr_ref, dci_ref, ddsk_ref, dypad, upad):
                ref[...] = jnp.zeros_like(ref)

        for b in range(Bl):
            dypad[pl.ds(b * TP, T), :] = dy_ref[b]
            upad[pl.ds(b * TP, T), :] = u_ref[b]
        dy = dypad[...]
        dyb = dy.astype(BF16)
        u = upad[...]
        ub = u.astype(BF16)
        for blk in range(nb):
            dyb_b = dyb[:, blk * dsb:(blk + 1) * dsb]
            for q_ref, c_ref in ((a_r, cr_ref), (a_i, ci_ref)):
                res = _dot_nt(dyb_b, c_ref[blk])
                for kk in range(npb):
                    q_ref[pl.ds((blk * npb + kk) * BT, BT), :] = res[:, kk * LANES:(kk + 1) * LANES]

        for k0 in range(0, NP, PG):
            ks = list(range(k0, k0 + PG))
            lr = [jnp.broadcast_to(lr_ref[pl.ds(k, 1), :], (Bl, LANES)) for k in ks]
            li = [jnp.broadcast_to(li_ref[pl.ds(k, 1), :], (Bl, LANES)) for k in ks]

            def step(tt, carry):
                t = T - 1 - tt
                out = []
                for j, k in enumerate(ks):
                    ar, ai = carry[j]
                    idx = pl.ds(k * BT + t, Bl, stride=TP)
                    nr = a_r[idx, :] + lr[j] * ar + li[j] * ai
                    ni = a_i[idx, :] + lr[j] * ai - li[j] * ar
                    a_r[idx, :] = nr
                    a_i[idx, :] = ni
                    out.append((nr, ni))
                return tuple(out)

            init = tuple((car_r[pl.ds(k * SUBLANES, Bl), :], car_i[pl.ds(k * SUBLANES, Bl), :]) for k in ks)
            fin = lax.fori_loop(0, T, step, init, unroll=2)
            for j, k in enumerate(ks):
                car_r[pl.ds(k * SUBLANES, Bl), :] = fin[j][0]
                car_i[pl.ds(k * SUBLANES, Bl), :] = fin[j][1]

        first_block = (i == nT - 1)
        for k in range(NP):
            rows = pl.ds(k * BT, BT)
            av_r, av_i = a_r[rows, :], a_i[rows, :]
            sp_r = pltpu.roll(sr_ref[rows, :], 1, 0)
            sp_i = pltpu.roll(si_ref[rows, :], 1, 0)
            acc = pl.ds(k * SUBLANES, SUBLANES)
            acc_r[acc, :] += jnp.sum((av_r * sp_r + av_i * sp_i).reshape(BT // SUBLANES, SUBLANES, LANES), axis=0)
            acc_i[acc, :] += jnp.sum((av_i * sp_r - av_r * sp_i).reshape(BT // SUBLANES, SUBLANES, LANES), axis=0)
            t0 = pl.ds(k * BT, Bl, stride=TP)
            a0_r, a0_i = a_r[t0, :], a_i[t0, :]
            brow = pl.ds(k * SUBLANES, Bl)
            sb_pr = jnp.where(first_block, 0.0, sbr_ref[brow, :])
            sb_pi = jnp.where(first_block, 0.0, sbi_ref[brow, :])
            acc_r[brow, :] += a0_r * sb_pr + a0_i * sb_pi
            acc_i[brow, :] += a0_i * sb_pr - a0_r * sb_pi

        ddsk_ref[...] += jnp.sum(dy * u, axis=0, keepdims=True)
        for blk in range(nb):
            cols = slice(blk * dsb, (blk + 1) * dsb)
            rows = [pl.ds((blk * npb + kk) * BT, BT) for kk in range(npb)]
            av_r = jnp.concatenate([a_r[r, :] for r in rows], axis=1).astype(BF16)
            av_i = jnp.concatenate([a_i[r, :] for r in rows], axis=1).astype(BF16)
            dupad[:, cols] = _dot_nt(av_r, br_ref[blk]) + _dot_nt(av_i, bi_ref[blk]) + dy[:, cols] * dsk_ref[:, cols]
            dbr_ref[blk] += _dot_tn(ub[:, cols], av_r)
            dbi_ref[blk] += _dot_tn(ub[:, cols], av_i)
            sv_r = jnp.concatenate([sr_ref[r, :] for r in rows], axis=1).astype(BF16)
            sv_i = jnp.concatenate([si_ref[r, :] for r in rows], axis=1).astype(BF16)
            dcr_ref[blk] += _dot_tn(sv_r, dyb[:, cols])
            dci_ref[blk] += _dot_tn(sv_i, dyb[:, cols])
        for b in range(Bl):
            du_ref[b] = dupad[pl.ds(b * TP, T), :]

        @pl.when(i == nT - 1)
        def _():
            for k in range(NP):
                dlr_ref[pl.ds(k, 1), :] = jnp.sum(acc_r[pl.ds(k * SUBLANES, SUBLANES), :], axis=0, keepdims=True)
                dli_ref[pl.ds(k, 1), :] = jnp.sum(acc_i[pl.ds(k * SUBLANES, SUBLANES), :], axis=0, keepdims=True)

    S = jax.ShapeDtypeStruct
    rev = lambda i: nT - 1 - i
    sspec = pl.BlockSpec((None, NP * BT, LANES), lambda i: (rev(i), 0, 0))
    bspec = pl.BlockSpec((None, NP * SUBLANES, LANES), lambda i: (jnp.maximum(rev(i) - 1, 0), 0, 0))
    tspec = pl.BlockSpec((Bl, T, Ds), lambda i: (0, rev(i), 0))
    return pl.pallas_call(
        body,
        out_shape=[S((Bl, L, Ds), F32), S((nb, dsb, gpb), F32), S((nb, dsb, gpb), F32),
                   S((nb, gpb, dsb), F32), S((nb, gpb, dsb), F32), S((NP, LANES), F32), S((NP, LANES), F32), S((1, Ds), F32)],
        grid=(nT,),
        in_specs=[tspec, tspec, sspec, sspec, bspec, bspec,
                  _const_spec((nb, dsb, gpb)), _const_spec((nb, dsb, gpb)),
                  _const_spec((nb, gpb, dsb)), _const_spec((nb, gpb, dsb)),
                  _const_spec((NP, LANES)), _const_spec((NP, LANES)), _const_spec((1, Ds))],
        out_specs=[tspec, _const_spec((nb, dsb, gpb)), _const_spec((nb, dsb, gpb)),
                   _const_spec((nb, gpb, dsb)), _const_spec((nb, gpb, dsb)),
                   _const_spec((NP, LANES)), _const_spec((NP, LANES)), _const_spec((1, Ds))],
        scratch_shapes=[pltpu.VMEM((NP * BT, LANES), F32)] * 2 + [pltpu.VMEM((NP * SUBLANES, LANES), F32)] * 4
        + [pltpu.VMEM((BT, Ds), F32)] * 3,
        compiler_params=_cparams(dimension_semantics=("arbitrary",)), name="s5_bwd",
    )(dy3, p3, Sr, Si, Sb_r, Sb_i, Bm_r, Bm_i, Cm_r, Cm_i, lam_r, lam_i, dsk)


def _mix_values(ylin, cb, cc, cx, gs, gc, halo_v, wglu, bglu, cw, wps, wpc, wout):
    yg, gelu_vjp = jax.vjp(jax.nn.gelu, ylin)
    sz = jax.nn.sigmoid(_dot(yg.astype(BF16), wglu) + bglu)
    ys = yg * sz
    v = cc * cx
    rows = lax.broadcasted_iota(jnp.int32, v.shape, 0)
    h6 = halo_v[SUBLANES - 2:SUBLANES - 1, :]
    h7 = halo_v[SUBLANES - 1:SUBLANES, :]
    v1 = jnp.where(rows == 0, h7, pltpu.roll(v, 1, 0))
    v2 = jnp.where(rows == 0, h6, jnp.where(rows == 1, h7, pltpu.roll(v, 2, 0)))
    cv = cw[0:1, :] * v2 + cw[1:2, :] * v1 + cw[2:3, :] * v
    yc = cb * cv
    ps = _dot(ys.astype(BF16), wps)
    pc = _dot(yc.astype(BF16), wpc)
    sgs = jax.nn.sigmoid(gs)
    sgc = jax.nn.sigmoid(gc)
    merged = sgs * ps + sgc * pc
    mo = _dot(merged.astype(BF16), wout)
    return dict(yg=yg, gelu_vjp=gelu_vjp, sz=sz, ys=ys, v=v, v1=v1, v2=v2, cv=cv, yc=yc, ps=ps, pc=pc,
                sgs=sgs, sgc=sgc, merged=merged, mo=mo)


def _mix_in_specs(TB, D, Ds, Dc, bps, blk):
    hb = TB // SUBLANES
    halo = lambda col: pl.BlockSpec((SUBLANES, Dc), lambda i: (jnp.maximum(blk(i) * hb - 1, 0), col))
    return [pl.BlockSpec((TB, Dc), lambda i: (blk(i), 1)), pl.BlockSpec((TB, Dc), lambda i: (blk(i), 2)),
            pl.BlockSpec((TB, Dc), lambda i: (blk(i), 3)), pl.BlockSpec((TB, D), lambda i: (blk(i), 2)),
            pl.BlockSpec((TB, D), lambda i: (blk(i), 3)), halo(2), halo(3),
            pl.BlockSpec((TB, Ds), lambda i: (blk(i), 0))]


def _mix_fwd(p2, ylin2, x2, g1, wglu, bglu, cw, wps, wpc, wout):
    N, D = x2.shape
    Ds = ylin2.shape[1]
    Dc = Ds
    Bl = g1.shape[0]
    TB = _row_block(N // Bl)
    bps = (N // Bl) // TB

    def body(cb_ref, cc_ref, cx_ref, gs_ref, gc_ref, hcc_ref, hcx_ref, yl_ref, x_ref, g1_ref,
             wglu_ref, bglu_ref, cw_ref, wps_ref, wpc_ref, wout_ref, x1_ref):
        i = pl.program_id(0)
        halo_v = jnp.where(i % bps == 0, 0.0, hcc_ref[...] * hcx_ref[...])
        f = _mix_values(yl_ref[...], cb_ref[...], cc_ref[...], cx_ref[...], gs_ref[...], gc_ref[...], halo_v,
                        wglu_ref[...], bglu_ref[...], cw_ref[...], wps_ref[...], wpc_ref[...], wout_ref[...])
        x1_ref[...] = x_ref[...] + g1_ref[...] * f["mo"]

    return pl.pallas_call(
        body, out_shape=jax.ShapeDtypeStruct((N, D), F32), grid=(N // TB,),
        in_specs=_mix_in_specs(TB, D, Ds, Dc, bps, lambda i: i) + [
            pl.BlockSpec((TB, D), lambda i: (i, 0)), _seq_spec(D, bps),
            _const_spec((Ds, Ds)), _const_spec((1, Ds)), _const_spec((SUBLANES, Dc)),
            _const_spec((Ds, D)), _const_spec((Dc, D)), _const_spec((D, D))],
        out_specs=pl.BlockSpec((TB, D), lambda i: (i, 0)),
        compiler_params=_cparams(), name="mix_fwd",
    )(p2, p2, p2, p2, p2, p2, p2, ylin2, x2, g1, wglu, bglu, cw, wps, wpc, wout)


def _mix_bwd(p2, ylin2, dx1, g1, wglu, bglu, cw, wps, wpc, wout):
    N, D = dx1.shape
    Ds = ylin2.shape[1]
    Dc = Ds
    IN = p2.shape[1]
    Bl = g1.shape[0]
    TB = _row_block(N // Bl)
    bps = (N // Bl) // TB
    nblk = N // TB
    rev = lambda i: nblk - 1 - i

    def body(cb_ref, cc_ref, cx_ref, gs_ref, gc_ref, hcc_ref, hcx_ref, yl_ref, dx1_ref, g1_ref,
             wglu_ref, bglu_ref, cw_ref, wps_ref, wpc_ref, wout_ref,
             dyl_ref, dp_ref, gwout_ref, gwps_ref, gwpc_ref, gwglu_ref, gbglu_ref, gcw_ref, dg1_ref, nxt):
        i = pl.program_id(0)
        blk = rev(i)

        @pl.when(i == 0)
        def _():
            for ref in (gwout_ref, gwps_ref, gwpc_ref, gwglu_ref, gbglu_ref, gcw_ref):
                ref[...] = jnp.zeros_like(ref)

        @pl.when(i % bps == 0)
        def _():
            nxt[...] = jnp.zeros_like(nxt)
            dg1_ref[...] = jnp.zeros_like(dg1_ref)

        cb, cc, cx = cb_ref[...], cc_ref[...], cx_ref[...]
        halo_v = jnp.where(blk % bps == 0, 0.0, hcc_ref[...] * hcx_ref[...])
        wglu, wps, wpc, wout, cw = wglu_ref[...], wps_ref[...], wpc_ref[...], wout_ref[...], cw_ref[...]
        f = _mix_values(yl_ref[...], cb, cc, cx, gs_ref[...], gc_ref[...], halo_v, wglu, bglu_ref[...], cw, wps, wpc, wout)

        dx1v = dx1_ref[...]
        dg1_ref[...] += jnp.sum(dx1v * f["mo"], axis=0, keepdims=True)
        dmo = (g1_ref[...] * dx1v).astype(BF16)
        gwout_ref[...] += _dot_tn(f["merged"].astype(BF16), dmo)
        dmerged = _dot_nt(dmo, wout)
        dps = dmerged * f["sgs"]
        dpc = dmerged * f["sgc"]
        dgs = dmerged * f["ps"] * f["sgs"] * (1.0 - f["sgs"])
        dgc = dmerged * f["pc"] * f["sgc"] * (1.0 - f["sgc"])
        dpsb, dpcb = dps.astype(BF16), dpc.astype(BF16)
        gwps_ref[...] += _dot_tn(f["ys"].astype(BF16), dpsb)
        gwpc_ref[...] += _dot_tn(f["yc"].astype(BF16), dpcb)
        dys = _dot_nt(dpsb, wps)
        dyc = _dot_nt(dpcb, wpc)

        dcb = dyc * f["cv"]
        dcv = dyc * cb
        rows = lax.broadcasted_iota(jnp.int32, dcv.shape, 0)
        n0, n1 = nxt[0:1, :], nxt[1:2, :]
        d1 = jnp.where(rows == TB - 1, n0, pltpu.roll(dcv, TB - 1, 0))
        d2 = jnp.where(rows == TB - 2, n0, jnp.where(rows == TB - 1, n1, pltpu.roll(dcv, TB - 2, 0)))
        dv = cw[2:3, :] * dcv + cw[1:2, :] * d1 + cw[0:1, :] * d2
        nxt[0:2, :] = dcv[0:2, :]
        gcw_ref[0:1, :] += jnp.sum(dcv * f["v2"], axis=0, keepdims=True)
        gcw_ref[1:2, :] += jnp.sum(dcv * f["v1"], axis=0, keepdims=True)
        gcw_ref[2:3, :] += jnp.sum(dcv * f["v"], axis=0, keepdims=True)

        dz = dys * f["yg"] * f["sz"] * (1.0 - f["sz"])
        dzb = dz.astype(BF16)
        gwglu_ref[...] += _dot_tn(f["yg"].astype(BF16), dzb)
        gbglu_ref[...] += jnp.sum(dz, axis=0, keepdims=True)
        dyg = dys * f["sz"] + _dot_nt(dzb, wglu)
        dyl_ref[...] = f["gelu_vjp"](dyg)[0]

        dp_ref[:, 0:Dc] = dcb
        dp_ref[:, Dc:2 * Dc] = dv * cx
        dp_ref[:, 2 * Dc:3 * Dc] = dv * cc
        dp_ref[:, 3 * Dc:3 * Dc + D] = dgs
        dp_ref[:, 3 * Dc + D:3 * Dc + 2 * D] = dgc

    S = jax.ShapeDtypeStruct
    return pl.pallas_call(
        body,
        out_shape=[S((N, Ds), F32), S((N, IN - Ds), F32), S((D, D), F32), S((Ds, D), F32), S((Dc, D), F32),
                   S((Ds, Ds), F32), S((1, Ds), F32), S((SUBLANES, Dc), F32), S((Bl, 1, D), F32)],
        grid=(nblk,),
        in_specs=_mix_in_specs(TB, D, Ds, Dc, bps, rev) + [
            pl.BlockSpec((TB, D), lambda i: (rev(i), 0)), _seq_spec(D, bps, nblk),
            _const_spec((Ds, Ds)), _const_spec((1, Ds)), _const_spec((SUBLANES, Dc)),
            _const_spec((Ds, D)), _const_spec((Dc, D)), _const_spec((D, D))],
        out_specs=[pl.BlockSpec((TB, Ds), lambda i: (rev(i), 0)), pl.BlockSpec((TB, IN - Ds), lambda i: (rev(i), 0)),
                   _const_spec((D, D)), _const_spec((Ds, D)), _const_spec((Dc, D)), _const_spec((Ds, Ds)),
                   _const_spec((1, Ds)), _const_spec((SUBLANES, Dc)), _seq_spec(D, bps, nblk)],
        scratch_shapes=[pltpu.VMEM((SUBLANES, Dc), F32)],
        compiler_params=_cparams(dimension_semantics=("arbitrary",)), name="mix_bwd",
    )(p2, p2, p2, p2, p2, p2, p2, ylin2, dx1, g1, wglu, bglu, cw, wps, wpc, wout)


def _mlp_fwd_bwd(x1, tgt, sh2, sc2, g2, n2g, fg, w1, w2):
    N, D = x1.shape
    Dff = w1.shape[1]
    Bl = sh2.shape[0]
    TB = _row_block(N // Bl)
    bps = (N // Bl) // TB

    def body(x1_ref, t_ref, sh_ref, sc_ref, g2_ref, n2_ref, fg_ref, w1_ref, w2_ref,
             dx1_ref, h2_ref, da_ref, sq_ref, df_ref, loss_ref, gfg_ref, gn2_ref, dsh_ref, dsc_ref, dg2_ref):
        i = pl.program_id(0)

        @pl.when(i == 0)
        def _():
            for ref in (loss_ref, gfg_ref, gn2_ref):
                ref[...] = jnp.zeros_like(ref)

        @pl.when(i % bps == 0)
        def _():
            for ref in (dsh_ref, dsc_ref, dg2_ref):
                ref[...] = jnp.zeros_like(ref)

        x1v = x1_ref[...]
        sc, g2v, n2 = sc_ref[...], g2_ref[...], n2_ref[...]
        xh2, r2 = _rms_fwd(x1v)
        xn2 = xh2 * n2
        h2 = (xn2 * (1.0 + sc) + sh_ref[...]).astype(BF16)
        a = _dot(h2, w1_ref[...])
        ra = jnp.maximum(a, 0.0)
        sq = (ra * ra).astype(BF16)
        fv = _dot(sq, w2_ref[...])
        x2 = x1v + g2v * fv
        xh3, r3 = _rms_fwd(x2)
        err = xh3 * fg_ref[...] - t_ref[...]
        loss_ref[...] += 0.5 * jnp.sum(jnp.mean(err * err, axis=-1, keepdims=True), axis=0, keepdims=True)
        dy = err * (1.0 / D)
        gfg_ref[...] += jnp.sum(dy * xh3, axis=0, keepdims=True)
        dx2 = _rms_bwd(dy * fg_ref[...], xh3, r3)
        dg2_ref[...] += jnp.sum(dx2 * fv, axis=0, keepdims=True)
        df = (g2v * dx2).astype(BF16)
        dsq = _dot_nt(df, w2_ref[...])
        da = (2.0 * ra * dsq).astype(BF16)
        dh2 = _dot_nt(da, w1_ref[...])
        dsh_ref[...] += jnp.sum(dh2, axis=0, keepdims=True)
        dsc_ref[...] += jnp.sum(dh2 * xn2, axis=0, keepdims=True)
        dxn2 = dh2 * (1.0 + sc)
        gn2_ref[...] += jnp.sum(dxn2 * xh2, axis=0, keepdims=True)
        dx1_ref[...] = dx2 + _rms_bwd(dxn2 * n2, xh2, r2)
        h2_ref[...] = h2
        da_ref[...] = da
        sq_ref[...] = sq
        df_ref[...] = df

    S = jax.ShapeDtypeStruct
    row = lambda w: pl.BlockSpec((TB, w), lambda i: (i, 0))
    return pl.pallas_call(
        body,
        out_shape=[S((N, D), F32), S((N, D), BF16), S((N, Dff), BF16), S((N, Dff), BF16), S((N, D), BF16),
                   S((1, 1), F32), S((1, D), F32), S((1, D), F32), S((Bl, 1, D), F32), S((Bl, 1, D), F32), S((Bl, 1, D), F32)],
        grid=(N // TB,),
        in_specs=[row(D), row(D), _seq_spec(D, bps), _seq_spec(D, bps), _seq_spec(D, bps),
                  _const_spec((1, D)), _const_spec((1, D)), _const_spec((D, Dff)), _const_spec((Dff, D))],
        out_specs=[row(D), row(D), row(Dff), row(Dff), row(D), _const_spec((1, 1)), _const_spec((1, D)), _const_spec((1, D)),
                   _seq_spec(D, bps), _seq_spec(D, bps), _seq_spec(D, bps)],
        compiler_params=_cparams(dimension_semantics=("arbitrary",)), name="mlp_fwd_bwd",
    )(x1, tgt, sh2, sc2, g2, n2g, fg, w1, w2)


def _grad_w(a, b, name):
    N, K1 = a.shape
    K2 = b.shape[1]
    t1 = 1024 if K1 % 1024 == 0 else K1
    t2 = 1024 if K2 % 1024 == 0 else K2
    tn = 2048 if N % 2048 == 0 else N

    def body(a_ref, b_ref, o_ref):
        @pl.when(pl.program_id(2) == 0)
        def _():
            o_ref[...] = jnp.zeros_like(o_ref)

        o_ref[...] += _dot_tn(a_ref[...], b_ref[...])

    return pl.pallas_call(
        body, out_shape=jax.ShapeDtypeStruct((K1, K2), F32), grid=(K1 // t1, K2 // t2, N // tn),
        in_specs=[pl.BlockSpec((tn, t1), lambda i, j, k: (k, i)), pl.BlockSpec((tn, t2), lambda i, j, k: (k, j))],
        out_specs=pl.BlockSpec((t1, t2), lambda i, j, k: (i, j)),
        compiler_params=_cparams(dimension_semantics=("arbitrary", "arbitrary", "arbitrary")), name=name)(a, b)


def _inproj_bwd(x2, dx1, du, dprest, sh1, sc1, n1g, w_in):
    N, D = x2.shape
    IN = w_in.shape[1]
    Ds = du.shape[1]
    Bl = sh1.shape[0]
    TB = _row_block(N // Bl)
    bps = (N // Bl) // TB

    def body(x_ref, dx1_ref, du_ref, dpr_ref, sh_ref, sc_ref, g_ref, w_ref,
             gx_ref, h_ref, dp_ref, gn1_ref, dsh_ref, dsc_ref):
        i = pl.program_id(0)

        @pl.when(i == 0)
        def _():
            gn1_ref[...] = jnp.zeros_like(gn1_ref)

        @pl.when(i % bps == 0)
        def _():
            dsh_ref[...] = jnp.zeros_like(dsh_ref)
            dsc_ref[...] = jnp.zeros_like(dsc_ref)

        sc, n1 = sc_ref[...], g_ref[...]
        xh, r = _rms_fwd(x_ref[...])
        xn = xh * n1
        h_ref[...] = (xn * (1.0 + sc) + sh_ref[...]).astype(BF16)
        dub = du_ref[...].astype(BF16)
        dprb = dpr_ref[...].astype(BF16)
        dp_ref[:, 0:Ds] = dub
        dp_ref[:, Ds:IN] = dprb
        dh = _dot_nt(dub, w_ref[:, 0:Ds]) + _dot_nt(dprb, w_ref[:, Ds:IN])
        dsh_ref[...] += jnp.sum(dh, axis=0, keepdims=True)
        dsc_ref[...] += jnp.sum(dh * xn, axis=0, keepdims=True)
        dxn = dh * (1.0 + sc)
        gn1_ref[...] += jnp.sum(dxn * xh, axis=0, keepdims=True)
        gx_ref[...] = dx1_ref[...] + _rms_bwd(dxn * n1, xh, r)

    S = jax.ShapeDtypeStruct
    row = lambda w: pl.BlockSpec((TB, w), lambda i: (i, 0))
    return pl.pallas_call(
        body,
        out_shape=[S((N, D), F32), S((N, D), BF16), S((N, IN), BF16), S((1, D), F32), S((Bl, 1, D), F32), S((Bl, 1, D), F32)],
        grid=(N // TB,),
        in_specs=[row(D), row(D), row(Ds), row(IN - Ds), _seq_spec(D, bps), _seq_spec(D, bps),
                  _const_spec((1, D)), _const_spec((D, IN))],
        out_specs=[row(D), row(D), row(IN), _const_spec((1, D)), _seq_spec(D, bps), _seq_spec(D, bps)],
        compiler_params=_cparams(dimension_semantics=("arbitrary",)), name="inproj_bwd",
    )(x2, dx1, du, dprest, sh1, sc1, n1g, w_in)


def _diag_blocks_from_groups(m, nb):
    G, a, b = m.shape
    gb = G // nb
    eye = jnp.eye(gb, dtype=m.dtype)
    mm = m.reshape(nb, gb, a, b)
    return (mm[:, :, :, None, :] * eye[None, :, None, :, None]).reshape(nb, gb * a, gb * b)


def _groups_from_diag_blocks(d, G, a, b):
    nb = d.shape[0]
    gb = G // nb
    dd = d.reshape(nb, gb, a, gb, b)
    idx = jnp.arange(gb)
    return dd[:, idx, :, idx, :].transpose(1, 0, 2, 3).reshape(G, a, b)


def _pad_rows(v, rows):
    return jnp.concatenate([v, jnp.zeros((rows - v.shape[0],) + v.shape[1:], v.dtype)], axis=0)


def _pack(vs):
    flat = jnp.concatenate([v.reshape(-1) for v in vs])
    n = flat.shape[0]
    tile = SUBLANES * LANES
    npad = -(-n // tile) * tile
    flat = jnp.concatenate([flat, jnp.zeros((npad - n,), flat.dtype)])
    return flat.reshape(npad // LANES, LANES)


def _unpack(packed, shapes):
    flat = packed.reshape(-1)
    out, off = [], 0
    for s in shapes:
        n = 1
        for d in s:
            n *= d
        out.append(flat[off:off + n].reshape(s))
        off += n
    return out


def kernel(x, c, norm1_g, norm2_g, w_ada, b_ada, w_in, lam_re, lam_im, log_dt, b_re, b_im, c_re, c_im, d_skip, w_glu, b_glu, conv_w, w_proj_ssm, w_proj_conv, w_out, w_ff1, w_ff2, final_g, loss_target, m_norm1_g, m_norm2_g, m_w_ada, m_b_ada, m_w_in, m_lam_re, m_lam_im, m_log_dt, m_b_re, m_b_im, m_c_re, m_c_im, m_d_skip, m_w_glu, m_b_glu, m_conv_w, m_w_proj_ssm, m_w_proj_conv, m_w_out, m_w_ff1, m_w_ff2, m_final_g, v_norm1_g, v_norm2_g, v_w_ada, v_b_ada, v_w_in, v_lam_re, v_lam_im, v_log_dt, v_b_re, v_b_im, v_c_re, v_c_im, v_d_skip, v_w_glu, v_b_glu, v_conv_w, v_w_proj_ssm, v_w_proj_conv, v_w_out, v_w_ff1, v_w_ff2, v_final_g):
    Bl, L, D = x.shape
    N = Bl * L
    Ds = Dc = D // 2
    G, H, P = Ds // SSM_GROUP, SSM_GROUP, SSM_STATE
    GP = G * P
    NP = GP // LANES
    nb = _s5_dims(Bl, L, Ds)[4]
    IN = Ds + 3 * Dc + 2 * D
    ax, ay, ac = _mesh_pos()
    q = 2 * ax + ay
    dev = 2 * q + ac

    big_names = ["w_in", "w_ff1", "w_ff2", "w_out", "w_proj_ssm", "w_proj_conv", "w_glu"]
    big_w = dict(w_in=w_in[0], w_ff1=w_ff1[0], w_ff2=w_ff2[0], w_out=w_out[0],
                 w_proj_ssm=w_proj_ssm[0], w_proj_conv=w_proj_conv[0], w_glu=w_glu[0])
    big_axis = dict(w_in=1, w_ff1=1, w_ff2=0, w_out=0, w_proj_ssm=1, w_proj_conv=1, w_glu=0)
    axes = [big_axis[k] for k in big_names]
    shard_shapes = [big_w[k].shape for k in big_names]
    pos = jnp.stack([q, ac]).astype(jnp.int32)
    own_only = {k: _cast_into_full(big_w[k], big_axis[k], pos, "cast_" + k) for k in big_names}
    full = {"w_in": _allgather_weights([own_only["w_in"]], [big_axis["w_in"]], [big_w["w_in"].shape])[0]}
    later = big_names[1:]

    Dcs = conv_w.shape[2]
    first = _allgather8(_pack([c, conv_w[0]]), "allgather_c").reshape(N_DEV, -1)
    c_all = first[:, :Bl * D].reshape(N_DEV * Bl, D)
    cw = first[0::2, Bl * D:Bl * D + 3 * Dcs].reshape(N_CHIPS, 3, Dcs).transpose(1, 0, 2).reshape(3, Dc)
    cw8 = _pad_rows(cw, SUBLANES)
    Ca = w_ada.shape[2]
    b_ada_sh = lax.dynamic_slice_in_dim(b_ada, q * Ca, Ca, axis=1)
    mod_part = _ada_fwd(c_all, w_ada[0], b_ada_sh)
    mod_g = _allgather8(mod_part, "allgather_mod")
    mod_all = mod_g[0::2].transpose(1, 0, 2).reshape(N_DEV * Bl, N_CHIPS * Ca)
    mod = lax.dynamic_slice_in_dim(mod_all, dev * Bl, Bl, axis=0)
    sh1, sc1, g1, sh2, sc2, g2 = [mod[:, k * D:(k + 1) * D].reshape(Bl, 1, D) for k in range(6)]

    ldt_c = log_dt[0].reshape(G, 1)
    bt_r = b_re[0].transpose(2, 0, 1)
    bt_i = b_im[0].transpose(2, 0, 1)
    lbr, lbi, bbt_r, bbt_i = _s5_disc(lam_re[0], lam_im[0], ldt_c, bt_r, bt_i)
    lam_r_p = lbr.reshape(NP, LANES)
    lam_i_p = lbi.reshape(NP, LANES)
    Bm_r = _diag_blocks_from_groups(bbt_r.transpose(1, 0, 2), nb).astype(BF16)
    Bm_i = _diag_blocks_from_groups(bbt_i.transpose(1, 0, 2), nb).astype(BF16)
    Cm_r = _diag_blocks_from_groups(c_re[0].transpose(0, 2, 1), nb).astype(BF16)
    Cm_i = _diag_blocks_from_groups(-c_im[0].transpose(0, 2, 1), nb).astype(BF16)

    x2 = x.reshape(N, D)
    p2, later_full = _inproj_fwd(x2, sh1, sc1, norm1_g, full["w_in"], [own_only[k] for k in later],
                                 [big_axis[k] for k in later], [big_w[k].shape for k in later])
    full.update(zip(later, later_full))
    p3 = p2.reshape(Bl, L, IN)
    Sr, Si, Sb_r, Sb_i, ylin3 = _s5_fwd(p3, Bm_r, Bm_i, Cm_r, Cm_i, lam_r_p, lam_i_p, d_skip)
    ylin2 = ylin3.reshape(N, Ds)
    mix_w = (full["w_glu"], b_glu, cw8, full["w_proj_ssm"], full["w_proj_conv"], full["w_out"])
    x1 = _mix_fwd(p2, ylin2, x2, g1, *mix_w)

    (dx1, h2b, dab, sqb, dfb, loss_p, g_fg, g_n2, dsh2, dsc2, dg2) = _mlp_fwd_bwd(
        x1, loss_target.reshape(N, D), sh2, sc2, g2, norm2_g, final_g.reshape(1, D), full["w_ff1"], full["w_ff2"])
    gw_ff1 = _grad_w(h2b, dab, "grad_w_ff1")
    gw_ff2 = _grad_w(sqb, dfb, "grad_w_ff2")

    (dyl2, dprest, gw_out, gw_ps, gw_pc, gw_glu, gb_glu, gcw8, dg1) = _mix_bwd(p2, ylin2, dx1, g1, *mix_w)
    (du3, dBm_r, dBm_i, dCm_r, dCm_i, dlam_r_p, dlam_i_p, g_dsk) = _s5_bwd(
        dyl2.reshape(Bl, L, Ds), p3, Sr, Si, Sb_r, Sb_i, Bm_r, Bm_i, Cm_r, Cm_i, lam_r_p, lam_i_p, d_skip)
    (grad_x2, hb, dpb, g_n1, dsh1, dsc1) = _inproj_bwd(x2, dx1, du3.reshape(N, Ds), dprest, sh1, sc1, norm1_g, full["w_in"])
    gw_in = _grad_w(hb, dpb, "grad_w_in")

    dbbt_r = _groups_from_diag_blocks(dBm_r, G, H, P).transpose(1, 0, 2)
    dbbt_i = _groups_from_diag_blocks(dBm_i, G, H, P).transpose(1, 0, 2)
    dc_re = _groups_from_diag_blocks(dCm_r, G, P, H).transpose(0, 2, 1)
    dc_im = -_groups_from_diag_blocks(dCm_i, G, P, H).transpose(0, 2, 1)
    dmod = jnp.concatenate([dsh1, dsc1, dg1, dsh2, dsc2, dg2], axis=-1).reshape(Bl, 6 * D)
    small = [g_n1, g_n2, g_fg, g_dsk, gb_glu, gcw8[:3], dlam_r_p, dlam_i_p, dbbt_r, dbbt_i, dc_re, dc_im]
    small_shapes = [v.shape for v in small]
    n_small = sum(int(v.size) for v in small)
    gathered = _allgather8(_pack(small + [dmod]), "allgather_small")
    red = _unpack(_sum_devices(gathered, "sum_small"), small_shapes)
    (r_n1, r_n2, r_fg, r_dsk, r_bglu, r_cw, r_dlr, r_dli, r_dbr, r_dbi, r_cre, r_cim) = red
    dmod_all = gathered.reshape(N_DEV, -1)[:, n_small:n_small + Bl * 6 * D].reshape(N_DEV * Bl, 6 * D)
    gw_ada, gb_ada = _ada_bwd(c_all, lax.dynamic_slice_in_dim(dmod_all, q * Ca, Ca, axis=1), dmod_all)
    g_lr, g_li, g_ldt, g_bt_r, g_bt_i = _s5_disc_bwd(lam_re[0], lam_im[0], ldt_c, bt_r, bt_i,
                                                   r_dlr.reshape(G, P), r_dli.reshape(G, P), r_dbr, r_dbi)

    grads_full = [gw_in, gw_ff1, gw_ff2, gw_out, gw_ps, gw_pc, gw_glu]
    theirs = _rs_sibling_exchange(grads_full, axes, shard_shapes)
    parts = [_presum(g, t, a, s, pos, "presum_" + k) for g, t, a, s, k in zip(grads_full, theirs, axes, shard_shapes, big_names)]
    recv = _rs_chip_scatter(parts)
    halves = [_sum_chips(p, r, pos, "sum_" + k) for p, r, k in zip(parts, recv, big_names)]
    reduced = dict(zip(big_names, _rs_sibling_share(halves)))

    grads = dict(
        norm1_g=r_n1, norm2_g=r_n2, w_ada=gw_ada, b_ada=gb_ada, lam_re=g_lr, lam_im=g_li, log_dt=g_ldt.reshape(1, G),
        b_re=g_bt_r.transpose(1, 2, 0), b_im=g_bt_i.transpose(1, 2, 0), c_re=r_cre, c_im=r_cim, d_skip=r_dsk,
        b_glu=r_bglu, conv_w=lax.dynamic_slice_in_dim(r_cw, q * Dcs, Dcs, axis=1), final_g=r_fg, **reduced)
    weights = dict(norm1_g=norm1_g, norm2_g=norm2_g, w_ada=w_ada, b_ada=b_ada, w_in=w_in, lam_re=lam_re, lam_im=lam_im,
                   log_dt=log_dt, b_re=b_re, b_im=b_im, c_re=c_re, c_im=c_im, d_skip=d_skip, w_glu=w_glu, b_glu=b_glu,
                   conv_w=conv_w, w_proj_ssm=w_proj_ssm, w_proj_conv=w_proj_conv, w_out=w_out, w_ff1=w_ff1, w_ff2=w_ff2,
                   final_g=final_g)
    m_in = dict(norm1_g=m_norm1_g, norm2_g=m_norm2_g, w_ada=m_w_ada, b_ada=m_b_ada, w_in=m_w_in, lam_re=m_lam_re,
                lam_im=m_lam_im, log_dt=m_log_dt, b_re=m_b_re, b_im=m_b_im, c_re=m_c_re, c_im=m_c_im, d_skip=m_d_skip,
                w_glu=m_w_glu, b_glu=m_b_glu, conv_w=m_conv_w, w_proj_ssm=m_w_proj_ssm, w_proj_conv=m_w_proj_conv,
                w_out=m_w_out, w_ff1=m_w_ff1, w_ff2=m_w_ff2, final_g=m_final_g)
    v_in = dict(norm1_g=v_norm1_g, norm2_g=v_norm2_g, w_ada=v_w_ada, b_ada=v_b_ada, w_in=v_w_in, lam_re=v_lam_re,
                lam_im=v_lam_im, log_dt=v_log_dt, b_re=v_b_re, b_im=v_b_im, c_re=v_c_re, c_im=v_c_im, d_skip=v_d_skip,
                w_glu=v_w_glu, b_glu=v_b_glu, conv_w=v_conv_w, w_proj_ssm=v_w_proj_ssm, w_proj_conv=v_w_proj_conv,
                w_out=v_w_out, w_ff1=v_w_ff1, w_ff2=v_w_ff2, final_g=v_final_g)
    names = list(weights)
    grads = {k: grads[k].reshape(weights[k].shape) for k in names}

    big_upd = big_names + ["w_ada"]
    delta, new_m, new_v = {}, {}, {}
    for k in big_upd:
        shp = weights[k].shape
        two_d = lambda a: a.reshape(shp[-2], shp[-1])
        d_, m_, v_ = _adamw(two_d(weights[k]), two_d(grads[k]), two_d(m_in[k]), two_d(v_in[k]), "adamw_" + k)
        delta[k], new_m[k], new_v[k] = d_.reshape(shp), m_.reshape(shp), v_.reshape(shp)
    small_upd = [k for k in names if k not in big_upd]
    shapes_s = [weights[k].shape for k in small_upd]
    d_, m_, v_ = _adamw(_pack([weights[k] for k in small_upd]), _pack([grads[k] for k in small_upd]),
                        _pack([m_in[k] for k in small_upd]), _pack([v_in[k] for k in small_upd]), "adamw_small")
    for k, dd, mm, vv in zip(small_upd, _unpack(d_, shapes_s), _unpack(m_, shapes_s), _unpack(v_, shapes_s)):
        delta[k], new_m[k], new_v[k] = dd, mm, vv

    loss = lax.psum(loss_p[0, 0], ("x", "y", "c"))
    grad_x = grad_x2.reshape(Bl, L, D)
    return (loss, grad_x, *[grads[k] for k in names], *[delta[k] for k in names],
            *[new_m[k] for k in names], *[new_v[k] for k in names])
```

```python
import functools

import jax
import jax.numpy as jnp
from jax import lax
from jax.experimental import pallas as pl
from jax.experimental.pallas import tpu as pltpu

F32 = jnp.float32
BF16 = jnp.bfloat16
MESH = pl.DeviceIdType.MESH
N_CHIPS = 4
N_DEV = 8
LANES = 128
SUBLANES = 8
V7X_VMEM_BYTES = 64 * 1024 * 1024
VMEM_LIMIT = V7X_VMEM_BYTES - 6 * 1024 * 1024
SSM_GROUP = 16
SSM_STATE = 64
S5_ROW_PAD = 4
RMS_EPS = 1e-6
ADAM_LR, ADAM_B1, ADAM_B2, ADAM_EPS, ADAM_WD, ADAM_STEP = 0.001, 0.9, 0.999, 1e-08, 0.01, 10

ANY = pl.BlockSpec(memory_space=pl.ANY)
VMEM_SPEC = pl.BlockSpec(memory_space=pltpu.VMEM)


def _cparams(**kw):
    return pltpu.CompilerParams(vmem_limit_bytes=VMEM_LIMIT, **kw)


def _dot(a, b):
    return jnp.dot(a, b, preferred_element_type=F32)


def _dot_nt(a, b):
    return lax.dot_general(a, b, (((1,), (1,)), ((), ())), preferred_element_type=F32)


def _dot_tn(a, b):
    return lax.dot_general(a, b, (((0,), (0,)), ((), ())), preferred_element_type=F32)


def _mesh_pos():
    return lax.axis_index("x"), lax.axis_index("y"), lax.axis_index("c")


def _allgather8(v, name):
    r, c = v.shape

    def body(x_ref, out_ref, send_sems, recv_sems, local_sem):
        x, y, cc = _mesh_pos()
        me, sibling = (x, y, cc), (x, y, 1 - cc)
        chips = [(1 - x, y), (x, 1 - y), (1 - x, 1 - y)]

        def slot(px, py, pc):
            return out_ref.at[4 * px + 2 * py + pc]

        def copy(k, block, to, src=None):
            return pltpu.make_async_remote_copy(
                src_ref=slot(*block) if src is None else src, dst_ref=slot(*block),
                send_sem=send_sems.at[k], recv_sem=recv_sems.at[k], device_id=to, device_id_type=MESH)

        mine = pltpu.make_async_copy(x_ref, slot(*me), local_sem)
        mine.start()
        first = [copy(0, me, sibling, src=x_ref)]
        first += [copy(1 + j, me, (*chip, cc), src=x_ref) for j, chip in enumerate(chips)]
        for cp in first:
            cp.start()
        passed = [copy(4 + j, (*chip, cc), sibling) for j, chip in enumerate(chips)]
        for j, chip in enumerate(chips):
            copy(1 + j, (*chip, cc), me).wait_recv()
            passed[j].start()
        copy(0, sibling, me).wait_recv()
        for j, chip in enumerate(chips):
            copy(4 + j, (*chip, 1 - cc), me).wait_recv()
        for cp in first + passed:
            cp.wait_send()
        mine.wait()

    return pl.pallas_call(
        body, out_shape=jax.ShapeDtypeStruct((N_DEV, r, c), v.dtype),
        in_specs=[VMEM_SPEC], out_specs=VMEM_SPEC,
        scratch_shapes=[pltpu.SemaphoreType.DMA((7,)), pltpu.SemaphoreType.DMA((7,)), pltpu.SemaphoreType.DMA],
        name=name)(v)


def _shard_region(ref, axis, shard_shape, q, half):
    R, C = shard_shape
    r0, nr = (0, R) if half is None else (half * (R // 2), R // 2)
    if axis == 1:
        return ref.at[pl.ds(r0, nr), pl.ds(q * C, C)]
    return ref.at[pl.ds(q * R + r0, nr), :]


def _allgather_weights(fulls, axes, shapes):
    n = len(fulls)

    def body(*refs):
        start, finish = _allgather_weights_steps(refs[n:2 * n], axes, shapes, *refs[2 * n:])
        start()
        finish()

    return pl.pallas_call(
        body, out_shape=[jax.ShapeDtypeStruct(f.shape, f.dtype) for f in fulls], in_specs=[ANY] * n, out_specs=[ANY] * n,
        input_output_aliases={i: i for i in range(n)},
        scratch_shapes=_allgather_weights_sems(n), name="allgather_weights")(*fulls)


def _allgather_weights_sems(n):
    return [pltpu.SemaphoreType.DMA((3 * n,))] * 4


def _allgather_weights_steps(outs, axes, shapes, send_s, recv_s, fsend_s, frecv_s):
    n = len(outs)
    x, y, c = _mesh_pos()
    q = 2 * x + y
    sibling = (x, y, 1 - c)
    chips = [(1 - x, y), (x, 1 - y), (1 - x, 1 - y)]

    def region(i, qq, half):
        return _shard_region(outs[i], axes[i], shapes[i], qq, half)

    def remote(src, dst, ss, rs, to):
        return pltpu.make_async_remote_copy(src_ref=src, dst_ref=dst, send_sem=ss, recv_sem=rs,
                                            device_id=to, device_id_type=MESH)

    def ici(i, j, qq):
        cx, cy = chips[j]
        reg = region(i, qq, c)
        return remote(reg, reg, send_s.at[3 * i + j], recv_s.at[3 * i + j], (cx, cy, c))

    def d2d(i, j, half):
        cx, cy = chips[j]
        reg = region(i, 2 * cx + cy, half)
        return remote(reg, reg, fsend_s.at[3 * i + j], frecv_s.at[3 * i + j], sibling)

    def start():
        for i in range(n):
            for j in range(3):
                ici(i, j, q).start()

    def finish():
        for i in range(n):
            for j, (cx, cy) in enumerate(chips):
                ici(i, j, 2 * cx + cy).wait_recv()
                d2d(i, j, c).start()
        for i in range(n):
            for j in range(3):
                d2d(i, j, 1 - c).wait_recv()
        for i in range(n):
            for j in range(3):
                ici(i, j, q).wait_send()
                d2d(i, j, c).wait_send()

    return start, finish


def _rs_sibling_exchange(grads, axes, shapes):
    n = len(grads)

    def body(*refs):
        ins, theirs = refs[:n], refs[n:2 * n]
        send_s, recv_s = refs[2 * n:]
        x, y, c = _mesh_pos()
        sibling = (x, y, 1 - c)
        cps = []
        for i in range(n):
            for qq in range(N_CHIPS):
                k = N_CHIPS * i + qq
                rc = pltpu.make_async_remote_copy(
                    src_ref=_shard_region(ins[i], axes[i], shapes[i], qq, 1 - c), dst_ref=theirs[i].at[qq],
                    send_sem=send_s.at[k], recv_sem=recv_s.at[k], device_id=sibling, device_id_type=MESH)
                rc.start()
                cps.append(rc)
        for rc in cps:
            rc.wait()

    stacked = [jax.ShapeDtypeStruct((N_CHIPS, R // 2, C), F32) for (R, C) in shapes]
    return pl.pallas_call(
        body, out_shape=stacked, in_specs=[ANY] * n, out_specs=[ANY] * n,
        scratch_shapes=[pltpu.SemaphoreType.DMA((N_CHIPS * n,))] * 2, name="rs_sibling_exchange")(*grads)


def _rs_chip_scatter(parts):
    n = len(parts)

    def body(*refs):
        ins, outs = refs[:n], refs[n:2 * n]
        send_s, recv_s = refs[2 * n:]
        x, y, c = _mesh_pos()
        chips = [(1 - x, y), (x, 1 - y), (1 - x, 1 - y)]
        sends = []
        for i in range(n):
            for j, (cx, cy) in enumerate(chips):
                cp = pltpu.make_async_remote_copy(
                    src_ref=ins[i].at[2 * cx + cy], dst_ref=outs[i].at[j],
                    send_sem=send_s.at[3 * i + j], recv_sem=recv_s.at[3 * i + j],
                    device_id=(cx, cy, c), device_id_type=MESH)
                cp.start()
                sends.append(cp)
        for cp in sends:
            cp.wait()

    return pl.pallas_call(
        body, out_shape=[jax.ShapeDtypeStruct((3,) + p.shape[1:], p.dtype) for p in parts],
        in_specs=[ANY] * n, out_specs=[ANY] * n,
        scratch_shapes=[pltpu.SemaphoreType.DMA((3 * n,))] * 2, name="rs_chip_scatter")(*parts)


def _rs_sibling_share(fulls):
    n = len(fulls)

    def body(*refs):
        outs = refs[n:2 * n]
        send_s, recv_s = refs[2 * n:]
        x, y, c = _mesh_pos()
        sibling = (x, y, 1 - c)
        cps = []
        for i in range(n):
            rh = fulls[i].shape[0] // 2
            mine = outs[i].at[pl.ds(c * rh, rh), :]
            rc = pltpu.make_async_remote_copy(src_ref=mine, dst_ref=mine, send_sem=send_s.at[i], recv_sem=recv_s.at[i],
                                              device_id=sibling, device_id_type=MESH)
            rc.start()
            cps.append(rc)
        for i, rc in enumerate(cps):
            rh = fulls[i].shape[0] // 2
            other = outs[i].at[pl.ds((1 - c) * rh, rh), :]
            pltpu.make_async_remote_copy(src_ref=other, dst_ref=other, send_sem=send_s.at[i], recv_sem=recv_s.at[i],
                                         device_id=sibling, device_id_type=MESH).wait_recv()
            rc.wait_send()

    return pl.pallas_call(
        body, out_shape=[jax.ShapeDtypeStruct(f.shape, f.dtype) for f in fulls],
        in_specs=[ANY] * n, out_specs=[ANY] * n, input_output_aliases={i: i for i in range(n)},
        scratch_shapes=[pltpu.SemaphoreType.DMA((n,))] * 2, name="rs_sibling_share")(*fulls)


def _row_block(rows, target=256):
    return target if rows % target == 0 else rows


def _cast_into_full(w, axis, pos, name):
    R, C = w.shape
    rb = _row_block(R)
    nrb = R // rb
    if axis == 1:
        shape, omap = (R, N_CHIPS * C), lambda i, s: (i, s[0])
    else:
        shape, omap = (N_CHIPS * R, C), lambda i, s: (s[0] * nrb + i, 0)

    def body(s_ref, w_ref, o_ref):
        o_ref[...] = w_ref[...].astype(BF16)

    return pl.pallas_call(
        body, out_shape=jax.ShapeDtypeStruct(shape, BF16),
        grid_spec=pltpu.PrefetchScalarGridSpec(
            num_scalar_prefetch=1, grid=(nrb,), in_specs=[pl.BlockSpec((rb, C), lambda i, s: (i, 0))],
            out_specs=pl.BlockSpec((rb, C), omap)),
        name=name)(pos, w)


def _presum(g, theirs, axis, shape, pos, name):
    R, C = shape
    Rh = R // 2
    rb = _row_block(Rh)
    nr = Rh // rb
    if axis == 1:
        gmap = lambda k, i, s: (s[1] * nr + i, k)
    else:
        gmap = lambda k, i, s: (k * (R // rb) + s[1] * nr + i, 0)

    def body(s_ref, g_ref, t_ref, o_ref):
        o_ref[...] = (g_ref[...] + t_ref[...]).astype(BF16)

    spec = pl.BlockSpec((None, rb, C), lambda k, i, s: (k, i, 0))
    return pl.pallas_call(
        body, out_shape=jax.ShapeDtypeStruct((N_CHIPS, Rh, C), BF16),
        grid_spec=pltpu.PrefetchScalarGridSpec(
            num_scalar_prefetch=1, grid=(N_CHIPS, nr), in_specs=[pl.BlockSpec((rb, C), gmap), spec], out_specs=spec),
        name=name)(pos, g, theirs)


def _sum_chips(own, recv, pos, name):
    _, Rh, C = own.shape
    rb = _row_block(Rh)
    nr = Rh // rb

    def body(s_ref, o_ref, r_ref, out_ref):
        acc = o_ref[...].astype(F32)
        for k in range(3):
            acc = acc + r_ref[k].astype(F32)
        out_ref[...] = acc

    return pl.pallas_call(
        body, out_shape=jax.ShapeDtypeStruct((2 * Rh, C), F32),
        grid_spec=pltpu.PrefetchScalarGridSpec(
            num_scalar_prefetch=1, grid=(nr,),
            in_specs=[pl.BlockSpec((None, rb, C), lambda i, s: (s[0], i, 0)), pl.BlockSpec((3, rb, C), lambda i, s: (0, i, 0))],
            out_specs=pl.BlockSpec((rb, C), lambda i, s: (s[1] * nr + i, 0))),
        name=name)(pos, own, recv)


def _sum_devices(parts, name):
    K, R, C = parts.shape

    def body(p_ref, o_ref):
        acc = p_ref[0]
        for k in range(1, K):
            acc = acc + p_ref[k]
        o_ref[...] = acc

    return pl.pallas_call(body, out_shape=jax.ShapeDtypeStruct((R, C), F32), name=name)(parts)


def _adamw_math(w, g, m, v):
    nm = ADAM_B1 * m + (1.0 - ADAM_B1) * g
    nv = ADAM_B2 * v + (1.0 - ADAM_B2) * (g * g)
    m_hat = nm / (1.0 - ADAM_B1 ** ADAM_STEP)
    v_hat = nv / (1.0 - ADAM_B2 ** ADAM_STEP)
    return -ADAM_LR * (m_hat / (jnp.sqrt(v_hat) + ADAM_EPS) + ADAM_WD * w), nm, nv


def _adamw_many(ws, gs, ms, vs):
    n = len(ws)

    def body(*refs):
        for k in range(n):
            w, g, m, v = (refs[j * n + k][...] for j in range(4))
            for j, val in enumerate(_adamw_math(w, g, m, v)):
                refs[(4 + j) * n + k][...] = val

    shapes = [jax.ShapeDtypeStruct(w.shape, F32) for w in ws]
    res = pl.pallas_call(body, out_shape=shapes * 3, compiler_params=_cparams(), name="adamw_small")(*ws, *gs, *ms, *vs)
    return res[:n], res[n:2 * n], res[2 * n:]


def _adamw(w, g, m, v, name):
    R, C = w.shape
    rb = _row_block(R)

    def body(w_ref, g_ref, m_ref, v_ref, d_ref, nm_ref, nv_ref):
        gv = g_ref[...]
        nm = ADAM_B1 * m_ref[...] + (1.0 - ADAM_B1) * gv
        nv = ADAM_B2 * v_ref[...] + (1.0 - ADAM_B2) * (gv * gv)
        m_hat = nm / (1.0 - ADAM_B1 ** ADAM_STEP)
        v_hat = nv / (1.0 - ADAM_B2 ** ADAM_STEP)
        d_ref[...] = -ADAM_LR * (m_hat / (jnp.sqrt(v_hat) + ADAM_EPS) + ADAM_WD * w_ref[...])
        nm_ref[...] = nm
        nv_ref[...] = nv

    spec = pl.BlockSpec((rb, C), lambda i: (i, 0))
    return pl.pallas_call(body, out_shape=[jax.ShapeDtypeStruct((R, C), F32)] * 3, grid=(R // rb,),
                          in_specs=[spec] * 4, out_specs=[spec] * 3, name=name)(w, g, m, v)


def _silu(v):
    return v * jax.nn.sigmoid(v)


def _ada_fwd(c_all, w_sh, b_sh):
    S, D = c_all.shape
    Ca = w_sh.shape[1]
    cb = 512 if Ca % 512 == 0 else Ca

    def body(c_ref, w_ref, b_ref, o_ref):
        act = _silu(c_ref[...]).astype(BF16)
        o_ref[...] = _dot(act, w_ref[...].astype(BF16)) + b_ref[...]

    return pl.pallas_call(
        body, out_shape=jax.ShapeDtypeStruct((S, Ca), F32), grid=(Ca // cb,),
        in_specs=[pl.BlockSpec((S, D), lambda j: (0, 0)), pl.BlockSpec((D, cb), lambda j: (0, j)),
                  pl.BlockSpec((1, cb), lambda j: (0, j))],
        out_specs=pl.BlockSpec((S, cb), lambda j: (0, j)), name="ada_fwd")(c_all, w_sh, b_sh)


def _ada_bwd(c_all, dmod_sh, dmod_all):
    S, D = c_all.shape
    Ca = dmod_sh.shape[1]
    C6 = dmod_all.shape[1]

    def body(c_ref, ds_ref, da_ref, gw_ref, gb_ref):
        act = _silu(c_ref[...]).astype(BF16)
        gw_ref[...] = _dot_tn(act, ds_ref[...].astype(BF16))
        gb_ref[...] = jnp.sum(da_ref[...], axis=0, keepdims=True)

    return pl.pallas_call(
        body, out_shape=[jax.ShapeDtypeStruct((D, Ca), F32), jax.ShapeDtypeStruct((1, C6), F32)],
        compiler_params=_cparams(), name="ada_bwd")(c_all, dmod_sh, dmod_all)


def _rms_fwd(xv):
    r = lax.rsqrt(jnp.mean(xv * xv, axis=-1, keepdims=True) + RMS_EPS)
    return xv * r, r


def _rms_bwd(dxh, xh, r):
    return r * (dxh - xh * jnp.mean(dxh * xh, axis=-1, keepdims=True))


def _const_spec(shape):
    nd = len(shape)
    return pl.BlockSpec(shape, lambda *_: (0,) * nd)


def _seq_spec(D, bps, rev_blocks=None):
    if rev_blocks is None:
        return pl.BlockSpec((None, 1, D), lambda i: (i // bps, 0, 0))
    return pl.BlockSpec((None, 1, D), lambda i: ((rev_blocks - 1 - i) // bps, 0, 0))


def _inproj_fwd(x2, sh1, sc1, n1g, w_in, later_w, later_axes, later_shapes):
    N, D = x2.shape
    IN = w_in.shape[1]
    Bl = sh1.shape[0]
    TB = _row_block(N // Bl)
    bps = (N // Bl) // TB
    nblk = N // TB
    n = len(later_w)

    def body(x_ref, sh_ref, sc_ref, g_ref, w_ref, *rest):
        p_ref = rest[n]
        start, finish = _allgather_weights_steps(rest[n + 1:2 * n + 1], later_axes, later_shapes, *rest[2 * n + 1:])
        i = pl.program_id(0)
        pl.when(i == 0)(start)
        xh, _ = _rms_fwd(x_ref[...])
        h = (xh * g_ref[...]) * (1.0 + sc_ref[...]) + sh_ref[...]
        p_ref[...] = _dot(h.astype(BF16), w_ref[...])
        pl.when(i == nblk - 1)(finish)

    res = pl.pallas_call(
        body, out_shape=[jax.ShapeDtypeStruct((N, IN), F32)] + [jax.ShapeDtypeStruct(w.shape, w.dtype) for w in later_w],
        grid=(nblk,),
        in_specs=[pl.BlockSpec((TB, D), lambda i: (i, 0)), _seq_spec(D, bps), _seq_spec(D, bps),
                  _const_spec((1, D)), _const_spec((D, IN))] + [ANY] * n,
        out_specs=[pl.BlockSpec((TB, IN), lambda i: (i, 0))] + [ANY] * n,
        input_output_aliases={5 + k: 1 + k for k in range(n)},
        scratch_shapes=_allgather_weights_sems(n),
        compiler_params=_cparams(dimension_semantics=("arbitrary",)), name="inproj_fwd",
    )(x2, sh1, sc1, n1g, w_in, *later_w)
    return res[0], res[1:]


def _s5_dims(Bl, L, Ds):
    G = Ds // SSM_GROUP
    GP = G * SSM_STATE
    NP = GP // LANES
    T = min(64, L // 2)
    nb = 2 if (Ds // 2) % LANES == 0 else 1
    return G, GP, NP, T, nb


def _s5_disc_math(lr, li, ldt, bt_r, bt_i):
    dt = jnp.exp(ldt)
    er = jnp.exp(lr * dt)
    lbr = er * jnp.cos(li * dt)
    lbi = er * jnp.sin(li * dt)
    den = lr * lr + li * li
    fr = ((lbr - 1.0) * lr + lbi * li) / den
    fi = (lbi * lr - (lbr - 1.0) * li) / den
    return lbr, lbi, fr[None] * bt_r - fi[None] * bt_i, fr[None] * bt_i + fi[None] * bt_r


def _s5_disc(lr, li, ldt, bt_r, bt_i):
    def body(lr_ref, li_ref, ldt_ref, br_ref, bi_ref, o0, o1, o2, o3):
        res = _s5_disc_math(lr_ref[...], li_ref[...], ldt_ref[...], br_ref[...], bi_ref[...])
        for o, v in zip((o0, o1, o2, o3), res):
            o[...] = v

    S = jax.ShapeDtypeStruct
    return pl.pallas_call(body, out_shape=[S(lr.shape, F32)] * 2 + [S(bt_r.shape, F32)] * 2, name="s5_disc")(lr, li, ldt, bt_r, bt_i)


def _s5_disc_bwd(lr, li, ldt, bt_r, bt_i, dlbr, dlbi, dbbr, dbbi):
    def body(lr_ref, li_ref, ldt_ref, br_ref, bi_ref, g0, g1, g2, g3, o0, o1, o2, o3, o4):
        _, vjp = jax.vjp(_s5_disc_math, lr_ref[...], li_ref[...], ldt_ref[...], br_ref[...], bi_ref[...])
        res = vjp((g0[...], g1[...], g2[...], g3[...]))
        for o, v in zip((o0, o1, o2, o3, o4), res):
            o[...] = v

    S = jax.ShapeDtypeStruct
    return pl.pallas_call(body, out_shape=[S(lr.shape, F32)] * 2 + [S(ldt.shape, F32)] + [S(bt_r.shape, F32)] * 2,
                          name="s5_disc_bwd")(lr, li, ldt, bt_r, bt_i, dlbr, dlbi, dbbr, dbbi)


S5_SCAN_PANELS = 4
S5_SCAN_STEPS = 4


def _panel_scan(src_r, src_i, dst_r, dst_i, lr_ref, li_ref, car_r, car_i, NP, Bl, T, TP, adjoint):
    BT = Bl * TP
    PG = min(S5_SCAN_PANELS, NP)
    CH = S5_SCAN_STEPS
    for k0 in range(0, NP, PG):
        ks = list(range(k0, k0 + PG))
        lr = [jnp.broadcast_to(lr_ref[pl.ds(k, 1), :], (Bl, LANES)) for k in ks]
        li = [jnp.broadcast_to(li_ref[pl.ds(k, 1), :], (Bl, LANES)) for k in ks]

        def trip(cc, carry):
            ts = [(T - 1 - (cc * CH + s)) if adjoint else (cc * CH + s) for s in range(CH)]
            idx = [[pl.ds(k * BT + t, Bl, stride=TP) for t in ts] for k in ks]
            loaded = [[(src_r[ix, :], src_i[ix, :]) for ix in idx[j]] for j in range(PG)]
            results, new_carry = [], []
            for j in range(PG):
                ar, ai = carry[j]
                res = []
                for s in range(CH):
                    br, bi = loaded[j][s]
                    if adjoint:
                        ar, ai = br + lr[j] * ar + li[j] * ai, bi + lr[j] * ai - li[j] * ar
                    else:
                        ar, ai = lr[j] * ar - li[j] * ai + br, lr[j] * ai + li[j] * ar + bi
                    res.append((ar, ai))
                results.append(res)
                new_carry.append((ar, ai))
            for j in range(PG):
                for s in range(CH):
                    dst_r[idx[j][s], :] = results[j][s][0]
                    dst_i[idx[j][s], :] = results[j][s][1]
            return tuple(new_carry)

        init = tuple((car_r[pl.ds(k * SUBLANES, Bl), :], car_i[pl.ds(k * SUBLANES, Bl), :]) for k in ks)
        fin = lax.fori_loop(0, T // CH, trip, init)
        for j, k in enumerate(ks):
            car_r[pl.ds(k * SUBLANES, Bl), :] = fin[j][0]
            car_i[pl.ds(k * SUBLANES, Bl), :] = fin[j][1]


def _s5_fwd(p3, Bm_r, Bm_i, Cm_r, Cm_i, lam_r, lam_i, dsk, later_w, later_axes, later_shapes):
    Bl, L, _ = p3.shape
    Ds = dsk.shape[1]
    G, GP, NP, T, nb = _s5_dims(Bl, L, Ds)
    nT = L // T
    TP = T + S5_ROW_PAD
    BT = Bl * TP
    dsb, gpb, npb = Ds // nb, GP // nb, NP // nb
    n = len(later_w)

    def body(u_ref, br_ref, bi_ref, cr_ref, ci_ref, lr_ref, li_ref, dsk_ref, *rest):
        sr_ref, si_ref, sb_r, sb_i, y_ref = rest[n:n + 5]
        car_r, car_i, upad, ypad, bu_r, bu_i = rest[2 * n + 5:2 * n + 11]
        start, finish = _allgather_weights_steps(rest[n + 5:2 * n + 5], later_axes, later_shapes, *rest[2 * n + 11:])
        i = pl.program_id(0)
        pl.when(i == 0)(start)

        @pl.when(i == 0)
        def _():
            car_r[...] = jnp.zeros_like(car_r)
            car_i[...] = jnp.zeros_like(car_i)
            upad[...] = jnp.zeros_like(upad)

        zpad = jnp.zeros((S5_ROW_PAD, LANES), F32)
        for k in range(NP):
            for b in range(Bl):
                sr_ref[pl.ds(k * BT + b * TP + T, S5_ROW_PAD), :] = zpad
                si_ref[pl.ds(k * BT + b * TP + T, S5_ROW_PAD), :] = zpad
        for b in range(Bl):
            upad[pl.ds(b * TP, T), :] = u_ref[b]
        u = upad[...]
        ub = u.astype(BF16)
        for blk in range(nb):
            ubb = ub[:, blk * dsb:(blk + 1) * dsb]
            for bu_ref, b_ref in ((bu_r, br_ref), (bu_i, bi_ref)):
                res = _dot(ubb, b_ref[blk])
                for kk in range(npb):
                    k = blk * npb + kk
                    bu_ref[pl.ds(k * BT, BT), :] = res[:, kk * LANES:(kk + 1) * LANES]

        _panel_scan(bu_r, bu_i, sr_ref, si_ref, lr_ref, li_ref, car_r, car_i, NP, Bl, T, TP, adjoint=False)
        sb_r[...] = car_r[...]
        sb_i[...] = car_i[...]

        for blk in range(nb):
            s_r = jnp.concatenate([sr_ref[pl.ds((blk * npb + kk) * BT, BT), :] for kk in range(npb)], axis=1).astype(BF16)
            s_i = jnp.concatenate([si_ref[pl.ds((blk * npb + kk) * BT, BT), :] for kk in range(npb)], axis=1).astype(BF16)
            cols = slice(blk * dsb, (blk + 1) * dsb)
            ypad[:, cols] = _dot(s_r, cr_ref[blk]) + _dot(s_i, ci_ref[blk]) + dsk_ref[:, cols] * u[:, cols]
        for b in range(Bl):
            y_ref[b] = ypad[pl.ds(b * TP, T), :]
        pl.when(i == nT - 1)(finish)

    S = jax.ShapeDtypeStruct
    state = S((nT, NP * BT, LANES), F32)
    bound = S((nT, NP * SUBLANES, LANES), F32)
    sspec = pl.BlockSpec((None, NP * BT, LANES), lambda i: (i, 0, 0))
    bspec = pl.BlockSpec((None, NP * SUBLANES, LANES), lambda i: (i, 0, 0))
    res = pl.pallas_call(
        body, out_shape=[state, state, bound, bound, S((Bl, L, Ds), F32)] + [S(w.shape, w.dtype) for w in later_w],
        grid=(nT,),
        in_specs=[pl.BlockSpec((Bl, T, Ds), lambda i: (0, i, 0)),
                  _const_spec((nb, dsb, gpb)), _const_spec((nb, dsb, gpb)),
                  _const_spec((nb, gpb, dsb)), _const_spec((nb, gpb, dsb)),
                  _const_spec((NP, LANES)), _const_spec((NP, LANES)), _const_spec((1, Ds))] + [ANY] * n,
        out_specs=[sspec, sspec, bspec, bspec, pl.BlockSpec((Bl, T, Ds), lambda i: (0, i, 0))] + [ANY] * n,
        input_output_aliases={8 + k: 5 + k for k in range(n)},
        scratch_shapes=[pltpu.VMEM((NP * SUBLANES, LANES), F32)] * 2 + [pltpu.VMEM((BT, Ds), F32)] * 2
        + [pltpu.VMEM((NP * BT, LANES), F32)] * 2 + _allgather_weights_sems(n),
        compiler_params=_cparams(dimension_semantics=("arbitrary",)), name="s5_fwd",
    )(p3, Bm_r, Bm_i, Cm_r, Cm_i, lam_r, lam_i, dsk, *later_w)
    return res[:5], res[5:]


def _s5_bwd(dy3, p3, Sr, Si, Sb_r, Sb_i, Bm_r, Bm_i, Cm_r, Cm_i, lam_r, lam_i, dsk):
    Bl, L, Ds = dy3.shape
    G, GP, NP, T, nb = _s5_dims(Bl, L, Ds)
    nT = L // T
    TP = T + S5_ROW_PAD
    BT = Bl * TP
    dsb, gpb, npb = Ds // nb, GP // nb, NP // nb
    PG = min(8, NP)

    def body(dy_ref, u_ref, sr_ref, si_ref, sbr_ref, sbi_ref, br_ref, bi_ref, cr_ref, ci_ref, lr_ref, li_ref, dsk_ref,
             du_ref, dbr_ref, dbi_ref, dcr_ref, dci_ref, dlr_ref, dli_ref, ddsk_ref,
             a_r, a_i, car_r, car_i, acc_r, acc_i, dypad, upad, dupad):
        i = pl.program_id(0)

        @pl.when(i == 0)
        def _():
            for ref in (car_r, car_i, acc_r, acc_i, dbr_ref, dbi_ref, dcr_ref, dci_ref, ddsk_ref, dypad, upad):
                ref[...] = jnp.zeros_like(ref)

        for b in range(Bl):
            dypad[pl.ds(b * TP, T), :] = dy_ref[b]
            upad[pl.ds(b * TP, T), :] = u_ref[b]
        dy = dypad[...]
        dyb = dy.astype(BF16)
        u = upad[...]
        ub = u.astype(BF16)
        for blk in range(nb):
            dyb_b = dyb[:, blk * dsb:(blk + 1) * dsb]
            for q_ref, c_ref in ((a_r, cr_ref), (a_i, ci_ref)):
                res = _dot_nt(dyb_b, c_ref[blk])
                for kk in range(npb):
                    q_ref[pl.ds((blk * npb + kk) * BT, BT), :] = res[:, kk * LANES:(kk + 1) * LANES]

        _panel_scan(a_r, a_i, a_r, a_i, lr_ref, li_ref, car_r, car_i, NP, Bl, T, TP, adjoint=True)

        first_block = (i == nT - 1)
        for k in range(NP):
            rows = pl.ds(k * BT, BT)
            av_r, av_i = a_r[rows, :], a_i[rows, :]
            sp_r = pltpu.roll(sr_ref[rows, :], 1, 0)
            sp_i = pltpu.roll(si_ref[rows, :], 1, 0)
            acc = pl.ds(k * SUBLANES, SUBLANES)
            acc_r[acc, :] += jnp.sum((av_r * sp_r + av_i * sp_i).reshape(BT // SUBLANES, SUBLANES, LANES), axis=0)
            acc_i[acc, :] += jnp.sum((av_i * sp_r - av_r * sp_i).reshape(BT // SUBLANES, SUBLANES, LANES), axis=0)
            t0 = pl.ds(k * BT, Bl, stride=TP)
            a0_r, a0_i = a_r[t0, :], a_i[t0, :]
            brow = pl.ds(k * SUBLANES, Bl)
            sb_pr = jnp.where(first_block, 0.0, sbr_ref[brow, :])
            sb_pi = jnp.where(first_block, 0.0, sbi_ref[brow, :])
            acc_r[brow, :] += a0_r * sb_pr + a0_i * sb_pi
            acc_i[brow, :] += a0_i * sb_pr - a0_r * sb_pi

        ddsk_ref[...] += jnp.sum(dy * u, axis=0, keepdims=True)
        for blk in range(nb):
            cols = slice(blk * dsb, (blk + 1) * dsb)
            rows = [pl.ds((blk * npb + kk) * BT, BT) for kk in range(npb)]
            av_r = jnp.concatenate([a_r[r, :] for r in rows], axis=1).astype(BF16)
            av_i = jnp.concatenate([a_i[r, :] for r in rows], axis=1).astype(BF16)
            dupad[:, cols] = _dot_nt(av_r, br_ref[blk]) + _dot_nt(av_i, bi_ref[blk]) + dy[:, cols] * dsk_ref[:, cols]
            dbr_ref[blk] += _dot_tn(ub[:, cols], av_r)
            dbi_ref[blk] += _dot_tn(ub[:, cols], av_i)
            sv_r = jnp.concatenate([sr_ref[r, :] for r in rows], axis=1).astype(BF16)
            sv_i = jnp.concatenate([si_ref[r, :] for r in rows], axis=1).astype(BF16)
            dcr_ref[blk] += _dot_tn(sv_r, dyb[:, cols])
            dci_ref[blk] += _dot_tn(sv_i, dyb[:, cols])
        for b in range(Bl):
            du_ref[b] = dupad[pl.ds(b * TP, T), :]

        @pl.when(i == nT - 1)
        def _():
            for k in range(NP):
                dlr_ref[pl.ds(k, 1), :] = jnp.sum(acc_r[pl.ds(k * SUBLANES, SUBLANES), :], axis=0, keepdims=True)
                dli_ref[pl.ds(k, 1), :] = jnp.sum(acc_i[pl.ds(k * SUBLANES, SUBLANES), :], axis=0, keepdims=True)

    S = jax.ShapeDtypeStruct
    rev = lambda i: nT - 1 - i
    sspec = pl.BlockSpec((None, NP * BT, LANES), lambda i: (rev(i), 0, 0))
    bspec = pl.BlockSpec((None, NP * SUBLANES, LANES), lambda i: (jnp.maximum(rev(i) - 1, 0), 0, 0))
    tspec = pl.BlockSpec((Bl, T, Ds), lambda i: (0, rev(i), 0))
    return pl.pallas_call(
        body,
        out_shape=[S((Bl, L, Ds), F32), S((nb, dsb, gpb), F32), S((nb, dsb, gpb), F32),
                   S((nb, gpb, dsb), F32), S((nb, gpb, dsb), F32), S((NP, LANES), F32), S((NP, LANES), F32), S((1, Ds), F32)],
        grid=(nT,),
        in_specs=[tspec, tspec, sspec, sspec, bspec, bspec,
                  _const_spec((nb, dsb, gpb)), _const_spec((nb, dsb, gpb)),
                  _const_spec((nb, gpb, dsb)), _const_spec((nb, gpb, dsb)),
                  _const_spec((NP, LANES)), _const_spec((NP, LANES)), _const_spec((1, Ds))],
        out_specs=[tspec, _const_spec((nb, dsb, gpb)), _const_spec((nb, dsb, gpb)),
                   _const_spec((nb, gpb, dsb)), _const_spec((nb, gpb, dsb)),
                   _const_spec((NP, LANES)), _const_spec((NP, LANES)), _const_spec((1, Ds))],
        scratch_shapes=[pltpu.VMEM((NP * BT, LANES), F32)] * 2 + [pltpu.VMEM((NP * SUBLANES, LANES), F32)] * 4
        + [pltpu.VMEM((BT, Ds), F32)] * 3,
        compiler_params=_cparams(dimension_semantics=("arbitrary",)), name="s5_bwd",
    )(dy3, p3, Sr, Si, Sb_r, Sb_i, Bm_r, Bm_i, Cm_r, Cm_i, lam_r, lam_i, dsk)


def _mix_values(ylin, cb, cc, cx, gs, gc, halo_v, wglu, bglu, cw, wps, wpc, wout):
    yg, gelu_vjp = jax.vjp(jax.nn.gelu, ylin)
    sz = jax.nn.sigmoid(_dot(yg.astype(BF16), wglu) + bglu)
    ys = yg * sz
    v = cc * cx
    rows = lax.broadcasted_iota(jnp.int32, v.shape, 0)
    h6 = halo_v[SUBLANES - 2:SUBLANES - 1, :]
    h7 = halo_v[SUBLANES - 1:SUBLANES, :]
    v1 = jnp.where(rows == 0, h7, pltpu.roll(v, 1, 0))
    v2 = jnp.where(rows == 0, h6, jnp.where(rows == 1, h7, pltpu.roll(v, 2, 0)))
    cv = cw[0:1, :] * v2 + cw[1:2, :] * v1 + cw[2:3, :] * v
    yc = cb * cv
    ps = _dot(ys.astype(BF16), wps)
    pc = _dot(yc.astype(BF16), wpc)
    sgs = jax.nn.sigmoid(gs)
    sgc = jax.nn.sigmoid(gc)
    merged = sgs * ps + sgc * pc
    mo = _dot(merged.astype(BF16), wout)
    return dict(yg=yg, gelu_vjp=gelu_vjp, sz=sz, ys=ys, v=v, v1=v1, v2=v2, cv=cv, yc=yc, ps=ps, pc=pc,
                sgs=sgs, sgc=sgc, merged=merged, mo=mo)


def _mix_in_specs(TB, D, Ds, Dc, bps, blk):
    hb = TB // SUBLANES
    halo = lambda col: pl.BlockSpec((SUBLANES, Dc), lambda i: (jnp.maximum(blk(i) * hb - 1, 0), col))
    return [pl.BlockSpec((TB, Dc), lambda i: (blk(i), 1)), pl.BlockSpec((TB, Dc), lambda i: (blk(i), 2)),
            pl.BlockSpec((TB, Dc), lambda i: (blk(i), 3)), pl.BlockSpec((TB, D), lambda i: (blk(i), 2)),
            pl.BlockSpec((TB, D), lambda i: (blk(i), 3)), halo(2), halo(3),
            pl.BlockSpec((TB, Ds), lambda i: (blk(i), 0))]


def _mix_fwd(p2, ylin2, x2, g1, wglu, bglu, cw, wps, wpc, wout):
    N, D = x2.shape
    Ds = ylin2.shape[1]
    Dc = Ds
    Bl = g1.shape[0]
    TB = _row_block(N // Bl)
    bps = (N // Bl) // TB

    def body(cb_ref, cc_ref, cx_ref, gs_ref, gc_ref, hcc_ref, hcx_ref, yl_ref, x_ref, g1_ref,
             wglu_ref, bglu_ref, cw_ref, wps_ref, wpc_ref, wout_ref, x1_ref):
        i = pl.program_id(0)
        halo_v = jnp.where(i % bps == 0, 0.0, hcc_ref[...] * hcx_ref[...])
        f = _mix_values(yl_ref[...], cb_ref[...], cc_ref[...], cx_ref[...], gs_ref[...], gc_ref[...], halo_v,
                        wglu_ref[...], bglu_ref[...], cw_ref[...], wps_ref[...], wpc_ref[...], wout_ref[...])
        x1_ref[...] = x_ref[...] + g1_ref[...] * f["mo"]

    return pl.pallas_call(
        body, out_shape=jax.ShapeDtypeStruct((N, D), F32), grid=(N // TB,),
        in_specs=_mix_in_specs(TB, D, Ds, Dc, bps, lambda i: i) + [
            pl.BlockSpec((TB, D), lambda i: (i, 0)), _seq_spec(D, bps),
            _const_spec((Ds, Ds)), _const_spec((1, Ds)), _const_spec((SUBLANES, Dc)),
            _const_spec((Ds, D)), _const_spec((Dc, D)), _const_spec((D, D))],
        out_specs=pl.BlockSpec((TB, D), lambda i: (i, 0)),
        compiler_params=_cparams(), name="mix_fwd",
    )(p2, p2, p2, p2, p2, p2, p2, ylin2, x2, g1, wglu, bglu, cw, wps, wpc, wout)


def _mix_bwd(p2, ylin2, dx1, g1, wglu, bglu, cw, wps, wpc, wout):
    N, D = dx1.shape
    Ds = ylin2.shape[1]
    Dc = Ds
    IN = p2.shape[1]
    Bl = g1.shape[0]
    TB = _row_block(N // Bl)
    bps = (N // Bl) // TB
    nblk = N // TB
    rev = lambda i: nblk - 1 - i

    def body(cb_ref, cc_ref, cx_ref, gs_ref, gc_ref, hcc_ref, hcx_ref, yl_ref, dx1_ref, g1_ref,
             wglu_ref, bglu_ref, cw_ref, wps_ref, wpc_ref, wout_ref,
             dyl_ref, dp_ref, gwout_ref, gwps_ref, gwpc_ref, gwglu_ref, gbglu_ref, gcw_ref, dg1_ref, nxt):
        i = pl.program_id(0)
        blk = rev(i)

        @pl.when(i == 0)
        def _():
            for ref in (gwout_ref, gwps_ref, gwpc_ref, gwglu_ref, gbglu_ref, gcw_ref):
                ref[...] = jnp.zeros_like(ref)

        @pl.when(i % bps == 0)
        def _():
            nxt[...] = jnp.zeros_like(nxt)
            dg1_ref[...] = jnp.zeros_like(dg1_ref)

        cb, cc, cx = cb_ref[...], cc_ref[...], cx_ref[...]
        halo_v = jnp.where(blk % bps == 0, 0.0, hcc_ref[...] * hcx_ref[...])
        wglu, wps, wpc, wout, cw = wglu_ref[...], wps_ref[...], wpc_ref[...], wout_ref[...], cw_ref[...]
        f = _mix_values(yl_ref[...], cb, cc, cx, gs_ref[...], gc_ref[...], halo_v, wglu, bglu_ref[...], cw, wps, wpc, wout)

        dx1v = dx1_ref[...]
        dg1_ref[...] += jnp.sum(dx1v * f["mo"], axis=0, keepdims=True)
        dmo = (g1_ref[...] * dx1v).astype(BF16)
        gwout_ref[...] += _dot_tn(f["merged"].astype(BF16), dmo)
        dmerged = _dot_nt(dmo, wout)
        dps = dmerged * f["sgs"]
        dpc = dmerged * f["sgc"]
        dgs = dmerged * f["ps"] * f["sgs"] * (1.0 - f["sgs"])
        dgc = dmerged * f["pc"] * f["sgc"] * (1.0 - f["sgc"])
        dpsb, dpcb = dps.astype(BF16), dpc.astype(BF16)
        gwps_ref[...] += _dot_tn(f["ys"].astype(BF16), dpsb)
        gwpc_ref[...] += _dot_tn(f["yc"].astype(BF16), dpcb)
        dys = _dot_nt(dpsb, wps)
        dyc = _dot_nt(dpcb, wpc)

        dcb = dyc * f["cv"]
        dcv = dyc * cb
        rows = lax.broadcasted_iota(jnp.int32, dcv.shape, 0)
        n0, n1 = nxt[0:1, :], nxt[1:2, :]
        d1 = jnp.where(rows == TB - 1, n0, pltpu.roll(dcv, TB - 1, 0))
        d2 = jnp.where(rows == TB - 2, n0, jnp.where(rows == TB - 1, n1, pltpu.roll(dcv, TB - 2, 0)))
        dv = cw[2:3, :] * dcv + cw[1:2, :] * d1 + cw[0:1, :] * d2
        nxt[0:2, :] = dcv[0:2, :]
        gcw_ref[0:1, :] += jnp.sum(dcv * f["v2"], axis=0, keepdims=True)
        gcw_ref[1:2, :] += jnp.sum(dcv * f["v1"], axis=0, keepdims=True)
        gcw_ref[2:3, :] += jnp.sum(dcv * f["v"], axis=0, keepdims=True)

        dz = dys * f["yg"] * f["sz"] * (1.0 - f["sz"])
        dzb = dz.astype(BF16)
        gwglu_ref[...] += _dot_tn(f["yg"].astype(BF16), dzb)
        gbglu_ref[...] += jnp.sum(dz, axis=0, keepdims=True)
        dyg = dys * f["sz"] + _dot_nt(dzb, wglu)
        dyl_ref[...] = f["gelu_vjp"](dyg)[0]

        dp_ref[:, 0:Dc] = dcb
        dp_ref[:, Dc:2 * Dc] = dv * cx
        dp_ref[:, 2 * Dc:3 * Dc] = dv * cc
        dp_ref[:, 3 * Dc:3 * Dc + D] = dgs
        dp_ref[:, 3 * Dc + D:3 * Dc + 2 * D] = dgc

    S = jax.ShapeDtypeStruct
    return pl.pallas_call(
        body,
        out_shape=[S((N, Ds), F32), S((N, IN - Ds), F32), S((D, D), F32), S((Ds, D), F32), S((Dc, D), F32),
                   S((Ds, Ds), F32), S((1, Ds), F32), S((SUBLANES, Dc), F32), S((Bl, 1, D), F32)],
        grid=(nblk,),
        in_specs=_mix_in_specs(TB, D, Ds, Dc, bps, rev) + [
            pl.BlockSpec((TB, D), lambda i: (rev(i), 0)), _seq_spec(D, bps, nblk),
            _const_spec((Ds, Ds)), _const_spec((1, Ds)), _const_spec((SUBLANES, Dc)),
            _const_spec((Ds, D)), _const_spec((Dc, D)), _const_spec((D, D))],
        out_specs=[pl.BlockSpec((TB, Ds), lambda i: (rev(i), 0)), pl.BlockSpec((TB, IN - Ds), lambda i: (rev(i), 0)),
                   _const_spec((D, D)), _const_spec((Ds, D)), _const_spec((Dc, D)), _const_spec((Ds, Ds)),
                   _const_spec((1, Ds)), _const_spec((SUBLANES, Dc)), _seq_spec(D, bps, nblk)],
        scratch_shapes=[pltpu.VMEM((SUBLANES, Dc), F32)],
        compiler_params=_cparams(dimension_semantics=("arbitrary",)), name="mix_bwd",
    )(p2, p2, p2, p2, p2, p2, p2, ylin2, dx1, g1, wglu, bglu, cw, wps, wpc, wout)


def _mlp_fwd_bwd(x1, tgt, sh2, sc2, g2, n2g, fg, w1, w2):
    N, D = x1.shape
    Dff = w1.shape[1]
    Bl = sh2.shape[0]
    TB = _row_block(N // Bl)
    bps = (N // Bl) // TB

    def body(x1_ref, t_ref, sh_ref, sc_ref, g2_ref, n2_ref, fg_ref, w1_ref, w2_ref,
             dx1_ref, h2_ref, da_ref, sq_ref, df_ref, loss_ref, gfg_ref, gn2_ref, dsh_ref, dsc_ref, dg2_ref):
        i = pl.program_id(0)

        @pl.when(i == 0)
        def _():
            for ref in (loss_ref, gfg_ref, gn2_ref):
                ref[...] = jnp.zeros_like(ref)

        @pl.when(i % bps == 0)
        def _():
            for ref in (dsh_ref, dsc_ref, dg2_ref):
                ref[...] = jnp.zeros_like(ref)

        x1v = x1_ref[...]
        sc, g2v, n2 = sc_ref[...], g2_ref[...], n2_ref[...]
        xh2, r2 = _rms_fwd(x1v)
        xn2 = xh2 * n2
        h2 = (xn2 * (1.0 + sc) + sh_ref[...]).astype(BF16)
        a = _dot(h2, w1_ref[...])
        ra = jnp.maximum(a, 0.0)
        sq = (ra * ra).astype(BF16)
        fv = _dot(sq, w2_ref[...])
        x2 = x1v + g2v * fv
        xh3, r3 = _rms_fwd(x2)
        err = xh3 * fg_ref[...] - t_ref[...]
        loss_ref[...] += 0.5 * jnp.sum(jnp.mean(err * err, axis=-1, keepdims=True), axis=0, keepdims=True)
        dy = err * (1.0 / D)
        gfg_ref[...] += jnp.sum(dy * xh3, axis=0, keepdims=True)
        dx2 = _rms_bwd(dy * fg_ref[...], xh3, r3)
        dg2_ref[...] += jnp.sum(dx2 * fv, axis=0, keepdims=True)
        df = (g2v * dx2).astype(BF16)
        dsq = _dot_nt(df, w2_ref[...])
        da = (2.0 * ra * dsq).astype(BF16)
        dh2 = _dot_nt(da, w1_ref[...])
        dsh_ref[...] += jnp.sum(dh2, axis=0, keepdims=True)
        dsc_ref[...] += jnp.sum(dh2 * xn2, axis=0, keepdims=True)
        dxn2 = dh2 * (1.0 + sc)
        gn2_ref[...] += jnp.sum(dxn2 * xh2, axis=0, keepdims=True)
        dx1_ref[...] = dx2 + _rms_bwd(dxn2 * n2, xh2, r2)
        h2_ref[...] = h2
        da_ref[...] = da
        sq_ref[...] = sq
        df_ref[...] = df

    S = jax.ShapeDtypeStruct
    row = lambda w: pl.BlockSpec((TB, w), lambda i: (i, 0))
    return pl.pallas_call(
        body,
        out_shape=[S((N, D), F32), S((N, D), BF16), S((N, Dff), BF16), S((N, Dff), BF16), S((N, D), BF16),
                   S((1, 1), F32), S((1, D), F32), S((1, D), F32), S((Bl, 1, D), F32), S((Bl, 1, D), F32), S((Bl, 1, D), F32)],
        grid=(N // TB,),
        in_specs=[row(D), row(D), _seq_spec(D, bps), _seq_spec(D, bps), _seq_spec(D, bps),
                  _const_spec((1, D)), _const_spec((1, D)), _const_spec((D, Dff)), _const_spec((Dff, D))],
        out_specs=[row(D), row(D), row(Dff), row(Dff), row(D), _const_spec((1, 1)), _const_spec((1, D)), _const_spec((1, D)),
                   _seq_spec(D, bps), _seq_spec(D, bps), _seq_spec(D, bps)],
        compiler_params=_cparams(dimension_semantics=("arbitrary",)), name="mlp_fwd_bwd",
    )(x1, tgt, sh2, sc2, g2, n2g, fg, w1, w2)


def _grad_w(a, b, name):
    N, K1 = a.shape
    K2 = b.shape[1]
    t1 = 1024 if K1 % 1024 == 0 else K1
    t2 = 1024 if K2 % 1024 == 0 else K2
    tn = 2048 if N % 2048 == 0 else N

    def body(a_ref, b_ref, o_ref):
        @pl.when(pl.program_id(2) == 0)
        def _():
            o_ref[...] = jnp.zeros_like(o_ref)

        o_ref[...] += _dot_tn(a_ref[...], b_ref[...])

    return pl.pallas_call(
        body, out_shape=jax.ShapeDtypeStruct((K1, K2), F32), grid=(K1 // t1, K2 // t2, N // tn),
        in_specs=[pl.BlockSpec((tn, t1), lambda i, j, k: (k, i)), pl.BlockSpec((tn, t2), lambda i, j, k: (k, j))],
        out_specs=pl.BlockSpec((t1, t2), lambda i, j, k: (i, j)),
        compiler_params=_cparams(dimension_semantics=("arbitrary", "arbitrary", "arbitrary")), name=name)(a, b)


def _inproj_bwd(x2, dx1, du, dprest, sh1, sc1, n1g, w_in):
    N, D = x2.shape
    IN = w_in.shape[1]
    Ds = du.shape[1]
    Bl = sh1.shape[0]
    TB = _row_block(N // Bl)
    bps = (N // Bl) // TB

    def body(x_ref, dx1_ref, du_ref, dpr_ref, sh_ref, sc_ref, g_ref, w_ref,
             gx_ref, h_ref, dp_ref, gn1_ref, dsh_ref, dsc_ref):
        i = pl.program_id(0)

        @pl.when(i == 0)
        def _():
            gn1_ref[...] = jnp.zeros_like(gn1_ref)

        @pl.when(i % bps == 0)
        def _():
            dsh_ref[...] = jnp.zeros_like(dsh_ref)
            dsc_ref[...] = jnp.zeros_like(dsc_ref)

        sc, n1 = sc_ref[...], g_ref[...]
        xh, r = _rms_fwd(x_ref[...])
        xn = xh * n1
        h_ref[...] = (xn * (1.0 + sc) + sh_ref[...]).astype(BF16)
        dub = du_ref[...].astype(BF16)
        dprb = dpr_ref[...].astype(BF16)
        dp_ref[:, 0:Ds] = dub
        dp_ref[:, Ds:IN] = dprb
        dh = _dot_nt(dub, w_ref[:, 0:Ds]) + _dot_nt(dprb, w_ref[:, Ds:IN])
        dsh_ref[...] += jnp.sum(dh, axis=0, keepdims=True)
        dsc_ref[...] += jnp.sum(dh * xn, axis=0, keepdims=True)
        dxn = dh * (1.0 + sc)
        gn1_ref[...] += jnp.sum(dxn * xh, axis=0, keepdims=True)
        gx_ref[...] = dx1_ref[...] + _rms_bwd(dxn * n1, xh, r)

    S = jax.ShapeDtypeStruct
    row = lambda w: pl.BlockSpec((TB, w), lambda i: (i, 0))
    return pl.pallas_call(
        body,
        out_shape=[S((N, D), F32), S((N, D), BF16), S((N, IN), BF16), S((1, D), F32), S((Bl, 1, D), F32), S((Bl, 1, D), F32)],
        grid=(N // TB,),
        in_specs=[row(D), row(D), row(Ds), row(IN - Ds), _seq_spec(D, bps), _seq_spec(D, bps),
                  _const_spec((1, D)), _const_spec((D, IN))],
        out_specs=[row(D), row(D), row(IN), _const_spec((1, D)), _seq_spec(D, bps), _seq_spec(D, bps)],
        compiler_params=_cparams(dimension_semantics=("arbitrary",)), name="inproj_bwd",
    )(x2, dx1, du, dprest, sh1, sc1, n1g, w_in)


def _diag_blocks_from_groups(m, nb):
    G, a, b = m.shape
    gb = G // nb
    eye = jnp.eye(gb, dtype=m.dtype)
    mm = m.reshape(nb, gb, a, b)
    return (mm[:, :, :, None, :] * eye[None, :, None, :, None]).reshape(nb, gb * a, gb * b)


def _groups_from_diag_blocks(d, G, a, b):
    nb = d.shape[0]
    gb = G // nb
    dd = d.reshape(nb, gb, a, gb, b)
    idx = jnp.arange(gb)
    return dd[:, idx, :, idx, :].transpose(1, 0, 2, 3).reshape(G, a, b)


def _pad_rows(v, rows):
    return jnp.concatenate([v, jnp.zeros((rows - v.shape[0],) + v.shape[1:], v.dtype)], axis=0)


def _pack(vs):
    flat = jnp.concatenate([v.reshape(-1) for v in vs])
    n = flat.shape[0]
    tile = SUBLANES * LANES
    npad = -(-n // tile) * tile
    flat = jnp.concatenate([flat, jnp.zeros((npad - n,), flat.dtype)])
    return flat.reshape(npad // LANES, LANES)


def _unpack(packed, shapes):
    flat = packed.reshape(-1)
    out, off = [], 0
    for s in shapes:
        n = 1
        for d in s:
            n *= d
        out.append(flat[off:off + n].reshape(s))
        off += n
    return out


def kernel(x, c, norm1_g, norm2_g, w_ada, b_ada, w_in, lam_re, lam_im, log_dt, b_re, b_im, c_re, c_im, d_skip, w_glu, b_glu, conv_w, w_proj_ssm, w_proj_conv, w_out, w_ff1, w_ff2, final_g, loss_target, m_norm1_g, m_norm2_g, m_w_ada, m_b_ada, m_w_in, m_lam_re, m_lam_im, m_log_dt, m_b_re, m_b_im, m_c_re, m_c_im, m_d_skip, m_w_glu, m_b_glu, m_conv_w, m_w_proj_ssm, m_w_proj_conv, m_w_out, m_w_ff1, m_w_ff2, m_final_g, v_norm1_g, v_norm2_g, v_w_ada, v_b_ada, v_w_in, v_lam_re, v_lam_im, v_log_dt, v_b_re, v_b_im, v_c_re, v_c_im, v_d_skip, v_w_glu, v_b_glu, v_conv_w, v_w_proj_ssm, v_w_proj_conv, v_w_out, v_w_ff1, v_w_ff2, v_final_g):
    Bl, L, D = x.shape
    N = Bl * L
    Ds = Dc = D // 2
    G, H, P = Ds // SSM_GROUP, SSM_GROUP, SSM_STATE
    GP = G * P
    NP = GP // LANES
    nb = _s5_dims(Bl, L, Ds)[4]
    IN = Ds + 3 * Dc + 2 * D
    ax, ay, ac = _mesh_pos()
    q = 2 * ax + ay
    dev = 2 * q + ac

    big_names = ["w_in", "w_ff1", "w_ff2", "w_out", "w_proj_ssm", "w_proj_conv", "w_glu"]
    big_w = dict(w_in=w_in[0], w_ff1=w_ff1[0], w_ff2=w_ff2[0], w_out=w_out[0],
                 w_proj_ssm=w_proj_ssm[0], w_proj_conv=w_proj_conv[0], w_glu=w_glu[0])
    big_axis = dict(w_in=1, w_ff1=1, w_ff2=0, w_out=0, w_proj_ssm=1, w_proj_conv=1, w_glu=0)
    axes = [big_axis[k] for k in big_names]
    shard_shapes = [big_w[k].shape for k in big_names]
    pos = jnp.stack([q, ac]).astype(jnp.int32)
    own_only = {k: _cast_into_full(big_w[k], big_axis[k], pos, "cast_" + k) for k in big_names}
    full = {"w_in": _allgather_weights([own_only["w_in"]], [big_axis["w_in"]], [big_w["w_in"].shape])[0]}

    Dcs = conv_w.shape[2]
    first = _allgather8(_pack([c, conv_w[0]]), "allgather_c").reshape(N_DEV, -1)
    c_all = first[:, :Bl * D].reshape(N_DEV * Bl, D)
    cw = first[0::2, Bl * D:Bl * D + 3 * Dcs].reshape(N_CHIPS, 3, Dcs).transpose(1, 0, 2).reshape(3, Dc)
    cw8 = _pad_rows(cw, SUBLANES)
    Ca = w_ada.shape[2]
    b_ada_sh = lax.dynamic_slice_in_dim(b_ada, q * Ca, Ca, axis=1)
    mod_part = _ada_fwd(c_all, w_ada[0], b_ada_sh)
    mod_g = _allgather8(mod_part, "allgather_mod")
    mod_all = mod_g[0::2].transpose(1, 0, 2).reshape(N_DEV * Bl, N_CHIPS * Ca)
    mod = lax.dynamic_slice_in_dim(mod_all, dev * Bl, Bl, axis=0)
    sh1, sc1, g1, sh2, sc2, g2 = [mod[:, k * D:(k + 1) * D].reshape(Bl, 1, D) for k in range(6)]

    ldt_c = log_dt[0].reshape(G, 1)
    bt_r = b_re[0].transpose(2, 0, 1)
    bt_i = b_im[0].transpose(2, 0, 1)
    lbr, lbi, bbt_r, bbt_i = _s5_disc(lam_re[0], lam_im[0], ldt_c, bt_r, bt_i)
    lam_r_p = lbr.reshape(NP, LANES)
    lam_i_p = lbi.reshape(NP, LANES)
    Bm_r = _diag_blocks_from_groups(bbt_r.transpose(1, 0, 2), nb).astype(BF16)
    Bm_i = _diag_blocks_from_groups(bbt_i.transpose(1, 0, 2), nb).astype(BF16)
    Cm_r = _diag_blocks_from_groups(c_re[0].transpose(0, 2, 1), nb).astype(BF16)
    Cm_i = _diag_blocks_from_groups(-c_im[0].transpose(0, 2, 1), nb).astype(BF16)

    x2 = x.reshape(N, D)
    carried = lambda ks: ([own_only[k] for k in ks], [big_axis[k] for k in ks], [big_w[k].shape for k in ks])
    mixer_w = ["w_out", "w_proj_ssm", "w_proj_conv", "w_glu"]
    mlp_w = ["w_ff1", "w_ff2"]
    p2, gathered = _inproj_fwd(x2, sh1, sc1, norm1_g, full["w_in"], *carried(mixer_w))
    full.update(zip(mixer_w, gathered))
    p3 = p2.reshape(Bl, L, IN)
    (Sr, Si, Sb_r, Sb_i, ylin3), gathered = _s5_fwd(p3, Bm_r, Bm_i, Cm_r, Cm_i, lam_r_p, lam_i_p, d_skip, *carried(mlp_w))
    full.update(zip(mlp_w, gathered))
    ylin2 = ylin3.reshape(N, Ds)
    mix_w = (full["w_glu"], b_glu, cw8, full["w_proj_ssm"], full["w_proj_conv"], full["w_out"])
    x1 = _mix_fwd(p2, ylin2, x2, g1, *mix_w)

    (dx1, h2b, dab, sqb, dfb, loss_p, g_fg, g_n2, dsh2, dsc2, dg2) = _mlp_fwd_bwd(
        x1, loss_target.reshape(N, D), sh2, sc2, g2, norm2_g, final_g.reshape(1, D), full["w_ff1"], full["w_ff2"])
    gw_ff1 = _grad_w(h2b, dab, "grad_w_ff1")
    gw_ff2 = _grad_w(sqb, dfb, "grad_w_ff2")

    (dyl2, dprest, gw_out, gw_ps, gw_pc, gw_glu, gb_glu, gcw8, dg1) = _mix_bwd(p2, ylin2, dx1, g1, *mix_w)
    (du3, dBm_r, dBm_i, dCm_r, dCm_i, dlam_r_p, dlam_i_p, g_dsk) = _s5_bwd(
        dyl2.reshape(Bl, L, Ds), p3, Sr, Si, Sb_r, Sb_i, Bm_r, Bm_i, Cm_r, Cm_i, lam_r_p, lam_i_p, d_skip)
    (grad_x2, hb, dpb, g_n1, dsh1, dsc1) = _inproj_bwd(x2, dx1, du3.reshape(N, Ds), dprest, sh1, sc1, norm1_g, full["w_in"])
    gw_in = _grad_w(hb, dpb, "grad_w_in")

    dbbt_r = _groups_from_diag_blocks(dBm_r, G, H, P).transpose(1, 0, 2)
    dbbt_i = _groups_from_diag_blocks(dBm_i, G, H, P).transpose(1, 0, 2)
    dc_re = _groups_from_diag_blocks(dCm_r, G, P, H).transpose(0, 2, 1)
    dc_im = -_groups_from_diag_blocks(dCm_i, G, P, H).transpose(0, 2, 1)
    dmod = jnp.concatenate([dsh1, dsc1, dg1, dsh2, dsc2, dg2], axis=-1).reshape(Bl, 6 * D)
    small = [g_n1, g_n2, g_fg, g_dsk, gb_glu, gcw8[:3], dlam_r_p, dlam_i_p, dbbt_r, dbbt_i, dc_re, dc_im]
    small_shapes = [v.shape for v in small]
    n_small = sum(int(v.size) for v in small)
    gathered = _allgather8(_pack(small + [dmod]), "allgather_small")
    red = _unpack(_sum_devices(gathered, "sum_small"), small_shapes)
    (r_n1, r_n2, r_fg, r_dsk, r_bglu, r_cw, r_dlr, r_dli, r_dbr, r_dbi, r_cre, r_cim) = red
    dmod_all = gathered.reshape(N_DEV, -1)[:, n_small:n_small + Bl * 6 * D].reshape(N_DEV * Bl, 6 * D)
    gw_ada, gb_ada = _ada_bwd(c_all, lax.dynamic_slice_in_dim(dmod_all, q * Ca, Ca, axis=1), dmod_all)
    g_lr, g_li, g_ldt, g_bt_r, g_bt_i = _s5_disc_bwd(lam_re[0], lam_im[0], ldt_c, bt_r, bt_i,
                                                   r_dlr.reshape(G, P), r_dli.reshape(G, P), r_dbr, r_dbi)

    grads_full = [gw_in, gw_ff1, gw_ff2, gw_out, gw_ps, gw_pc, gw_glu]
    theirs = _rs_sibling_exchange(grads_full, axes, shard_shapes)
    parts = [_presum(g, t, a, s, pos, "presum_" + k) for g, t, a, s, k in zip(grads_full, theirs, axes, shard_shapes, big_names)]
    recv = _rs_chip_scatter(parts)
    halves = [_sum_chips(p, r, pos, "sum_" + k) for p, r, k in zip(parts, recv, big_names)]
    reduced = dict(zip(big_names, _rs_sibling_share(halves)))

    grads = dict(
        norm1_g=r_n1, norm2_g=r_n2, w_ada=gw_ada, b_ada=gb_ada, lam_re=g_lr, lam_im=g_li, log_dt=g_ldt.reshape(1, G),
        b_re=g_bt_r.transpose(1, 2, 0), b_im=g_bt_i.transpose(1, 2, 0), c_re=r_cre, c_im=r_cim, d_skip=r_dsk,
        b_glu=r_bglu, conv_w=lax.dynamic_slice_in_dim(r_cw, q * Dcs, Dcs, axis=1), final_g=r_fg, **reduced)
    weights = dict(norm1_g=norm1_g, norm2_g=norm2_g, w_ada=w_ada, b_ada=b_ada, w_in=w_in, lam_re=lam_re, lam_im=lam_im,
                   log_dt=log_dt, b_re=b_re, b_im=b_im, c_re=c_re, c_im=c_im, d_skip=d_skip, w_glu=w_glu, b_glu=b_glu,
                   conv_w=conv_w, w_proj_ssm=w_proj_ssm, w_proj_conv=w_proj_conv, w_out=w_out, w_ff1=w_ff1, w_ff2=w_ff2,
                   final_g=final_g)
    m_in = dict(norm1_g=m_norm1_g, norm2_g=m_norm2_g, w_ada=m_w_ada, b_ada=m_b_ada, w_in=m_w_in, lam_re=m_lam_re,
                lam_im=m_lam_im, log_dt=m_log_dt, b_re=m_b_re, b_im=m_b_im, c_re=m_c_re, c_im=m_c_im, d_skip=m_d_skip,
                w_glu=m_w_glu, b_glu=m_b_glu, conv_w=m_conv_w, w_proj_ssm=m_w_proj_ssm, w_proj_conv=m_w_proj_conv,
                w_out=m_w_out, w_ff1=m_w_ff1, w_ff2=m_w_ff2, final_g=m_final_g)
    v_in = dict(norm1_g=v_norm1_g, norm2_g=v_norm2_g, w_ada=v_w_ada, b_ada=v_b_ada, w_in=v_w_in, lam_re=v_lam_re,
                lam_im=v_lam_im, log_dt=v_log_dt, b_re=v_b_re, b_im=v_b_im, c_re=v_c_re, c_im=v_c_im, d_skip=v_d_skip,
                w_glu=v_w_glu, b_glu=v_b_glu, conv_w=v_conv_w, w_proj_ssm=v_w_proj_ssm, w_proj_conv=v_w_proj_conv,
                w_out=v_w_out, w_ff1=v_w_ff1, w_ff2=v_w_ff2, final_g=v_final_g)
    names = list(weights)
    grads = {k: grads[k].reshape(weights[k].shape) for k in names}

    big_upd = big_names + ["w_ada"]
    delta, new_m, new_v = {}, {}, {}
    for k in big_upd:
        shp = weights[k].shape
        two_d = lambda a: a.reshape(shp[-2], shp[-1])
        d_, m_, v_ = _adamw(two_d(weights[k]), two_d(grads[k]), two_d(m_in[k]), two_d(v_in[k]), "adamw_" + k)
        delta[k], new_m[k], new_v[k] = d_.reshape(shp), m_.reshape(shp), v_.reshape(shp)
    small_upd = [k for k in names if k not in big_upd]
    flat2 = lambda a: a.reshape(-1, a.shape[-1])
    d_, m_, v_ = _adamw_many([flat2(weights[k]) for k in small_upd], [flat2(grads[k]) for k in small_upd],
                             [flat2(m_in[k]) for k in small_upd], [flat2(v_in[k]) for k in small_upd])
    for k, dd, mm, vv in zip(small_upd, d_, m_, v_):
        shp = weights[k].shape
        delta[k], new_m[k], new_v[k] = dd.reshape(shp), mm.reshape(shp), vv.reshape(shp)

    loss = lax.psum(loss_p[0, 0], ("x", "y", "c"))
    grad_x = grad_x2.reshape(Bl, L, D)
    return (loss, grad_x, *[grads[k] for k in names], *[delta[k] for k in names],
            *[new_m[k] for k in names], *[new_v[k] for k in names])
```

```python
from typing import Callable, NamedTuple

import jax
import jax.numpy as jnp
from jax import lax
from jax.experimental import pallas as pl
from jax.experimental.pallas import tpu as pltpu

F32 = jnp.float32
BF16 = jnp.bfloat16
MESH = pl.DeviceIdType.MESH
N_CHIPS = 4
N_DEV = 8
LANES = 128
SUBLANES = 8
V7X_VMEM_BYTES = 64 * 1024 * 1024
VMEM_LIMIT = V7X_VMEM_BYTES - 6 * 1024 * 1024
SSM_GROUP = 16
SSM_STATE = 64
S5_ROW_PAD = 4
RMS_EPS = 1e-6
ADAM_LR, ADAM_B1, ADAM_B2, ADAM_EPS, ADAM_WD, ADAM_STEP = 0.001, 0.9, 0.999, 1e-08, 0.01, 10

ANY = pl.BlockSpec(memory_space=pl.ANY)
VMEM_SPEC = pl.BlockSpec(memory_space=pltpu.VMEM)


def _cparams(**kw):
    return pltpu.CompilerParams(vmem_limit_bytes=VMEM_LIMIT, **kw)


def _dot(a, b):
    return jnp.dot(a, b, preferred_element_type=F32)


def _dot_nt(a, b):
    return lax.dot_general(a, b, (((1,), (1,)), ((), ())), preferred_element_type=F32)


def _dot_tn(a, b):
    return lax.dot_general(a, b, (((0,), (0,)), ((), ())), preferred_element_type=F32)


def _mesh_pos():
    return lax.axis_index("x"), lax.axis_index("y"), lax.axis_index("c")


def _allgather8(v, name):
    r, c = v.shape

    def body(x_ref, out_ref, send_sems, recv_sems, local_sem):
        x, y, cc = _mesh_pos()
        me, sibling = (x, y, cc), (x, y, 1 - cc)
        chips = [(1 - x, y), (x, 1 - y), (1 - x, 1 - y)]

        def slot(px, py, pc):
            return out_ref.at[4 * px + 2 * py + pc]

        def copy(k, block, to, src=None):
            return pltpu.make_async_remote_copy(
                src_ref=slot(*block) if src is None else src, dst_ref=slot(*block),
                send_sem=send_sems.at[k], recv_sem=recv_sems.at[k], device_id=to, device_id_type=MESH)

        mine = pltpu.make_async_copy(x_ref, slot(*me), local_sem)
        mine.start()
        first = [copy(0, me, sibling, src=x_ref)]
        first += [copy(1 + j, me, (*chip, cc), src=x_ref) for j, chip in enumerate(chips)]
        for cp in first:
            cp.start()
        passed = [copy(4 + j, (*chip, cc), sibling) for j, chip in enumerate(chips)]
        for j, chip in enumerate(chips):
            copy(1 + j, (*chip, cc), me).wait_recv()
            passed[j].start()
        copy(0, sibling, me).wait_recv()
        for j, chip in enumerate(chips):
            copy(4 + j, (*chip, 1 - cc), me).wait_recv()
        for cp in first + passed:
            cp.wait_send()
        mine.wait()

    return pl.pallas_call(
        body, out_shape=jax.ShapeDtypeStruct((N_DEV, r, c), v.dtype),
        in_specs=[VMEM_SPEC], out_specs=VMEM_SPEC,
        scratch_shapes=[pltpu.SemaphoreType.DMA((7,)), pltpu.SemaphoreType.DMA((7,)), pltpu.SemaphoreType.DMA],
        name=name)(v)


def _shard_region(ref, axis, shard_shape, q, half):
    R, C = shard_shape
    r0, nr = (0, R) if half is None else (half * (R // 2), R // 2)
    if axis == 1:
        return ref.at[pl.ds(r0, nr), pl.ds(q * C, C)]
    return ref.at[pl.ds(q * R + r0, nr), :]


class _Carried(NamedTuple):
    inputs: tuple
    out_shapes: tuple
    aliases: dict
    sems: tuple
    steps: Callable


def _carry_join(a, b):
    na_i, na_o, na_s = len(a.inputs), len(a.out_shapes), len(a.sems)

    def steps(ins, outs, sems):
        sa, fa = a.steps(ins[:na_i], outs[:na_o], sems[:na_s])
        sb, fb = b.steps(ins[na_i:], outs[na_o:], sems[na_s:])


        def start():
            sa()
            sb()

        def finish():
            fa()
            fb()

        return start, finish

    aliases = dict(a.aliases)
    aliases.update({na_i + i: na_o + o for i, o in b.aliases.items()})
    return _Carried(a.inputs + b.inputs, a.out_shapes + b.out_shapes, aliases, a.sems + b.sems, steps)


def _call_carrying(body, carried, args, *, out_shape, in_specs, out_specs, scratch_shapes=(), grid=None, aliases=None,
                   name, **kw):
    n_in, n_out, n_sc = len(in_specs), len(out_specs), len(scratch_shapes)
    n_ci, n_co = len(carried.inputs), len(carried.out_shapes)
    assert grid is None or len(grid) == 1

    def wrapped(*refs):
        ins, refs = refs[:n_in], refs[n_in:]
        c_ins, refs = refs[:n_ci], refs[n_ci:]
        outs, refs = refs[:n_out], refs[n_out:]
        c_outs, refs = refs[:n_co], refs[n_co:]
        scratch, c_sems = refs[:n_sc], refs[n_sc:]
        start, finish = carried.steps(c_ins, c_outs, c_sems)
        if grid is None:
            start()
            body(*ins, *outs, *scratch)
            finish()
        else:
            step = pl.program_id(0)
            pl.when(step == 0)(start)
            body(*ins, *outs, *scratch)
            pl.when(step == grid[0] - 1)(finish)

    if grid is not None:
        kw["grid"] = grid
    io_aliases = dict(aliases or {})
    io_aliases.update({n_in + i: n_out + o for i, o in carried.aliases.items()})
    res = pl.pallas_call(
        wrapped, out_shape=list(out_shape) + list(carried.out_shapes),
        in_specs=list(in_specs) + [ANY] * n_ci, out_specs=list(out_specs) + [ANY] * n_co,
        scratch_shapes=list(scratch_shapes) + list(carried.sems),
        input_output_aliases=io_aliases, name=name, **kw,
    )(*args, *carried.inputs)
    return res[:n_out], res[n_out:]


def _run_carried(carried, name):
    return _call_carrying(lambda: None, carried, (), out_shape=(), in_specs=(), out_specs=(), name=name)[1]


def _carry_allgather(fulls, axes, shapes):
    n = len(fulls)
    return _Carried(tuple(fulls), tuple(jax.ShapeDtypeStruct(f.shape, f.dtype) for f in fulls),
                    {i: i for i in range(n)}, (pltpu.SemaphoreType.DMA((3 * n,)),) * 4,
                    lambda ins, outs, sems: _allgather_weights_steps(outs, axes, shapes, *sems))


def _allgather_weights_steps(outs, axes, shapes, send_s, recv_s, fsend_s, frecv_s):
    n = len(outs)
    x, y, c = _mesh_pos()
    q = 2 * x + y
    sibling = (x, y, 1 - c)
    chips = [(1 - x, y), (x, 1 - y), (1 - x, 1 - y)]

    def region(i, qq, half):
        return _shard_region(outs[i], axes[i], shapes[i], qq, half)

    def remote(src, dst, ss, rs, to):
        return pltpu.make_async_remote_copy(src_ref=src, dst_ref=dst, send_sem=ss, recv_sem=rs,
                                            device_id=to, device_id_type=MESH)

    def ici(i, j, qq):
        cx, cy = chips[j]
        reg = region(i, qq, c)
        return remote(reg, reg, send_s.at[3 * i + j], recv_s.at[3 * i + j], (cx, cy, c))

    def d2d(i, j, half):
        cx, cy = chips[j]
        reg = region(i, 2 * cx + cy, half)
        return remote(reg, reg, fsend_s.at[3 * i + j], frecv_s.at[3 * i + j], sibling)

    def start():
        for i in range(n):
            for j in range(3):
                ici(i, j, q).start()

    def finish():
        for i in range(n):
            for j, (cx, cy) in enumerate(chips):
                ici(i, j, 2 * cx + cy).wait_recv()
                d2d(i, j, c).start()
        for i in range(n):
            for j in range(3):
                d2d(i, j, 1 - c).wait_recv()
        for i in range(n):
            for j in range(3):
                ici(i, j, q).wait_send()
                d2d(i, j, c).wait_send()

    return start, finish


def _carry_sibling_exchange(grads, axes, shapes):
    n = len(grads)

    def steps(ins, theirs, sems):
        send_s, recv_s = sems
        x, y, c = _mesh_pos()

        def copies():
            return [pltpu.make_async_remote_copy(
                src_ref=_shard_region(ins[i], axes[i], shapes[i], qq, 1 - c), dst_ref=theirs[i].at[qq],
                send_sem=send_s.at[N_CHIPS * i + qq], recv_sem=recv_s.at[N_CHIPS * i + qq],
                device_id=(x, y, 1 - c), device_id_type=MESH) for i in range(n) for qq in range(N_CHIPS)]

        def start():
            for cp in copies():
                cp.start()

        def finish():
            for cp in copies():
                cp.wait()

        return start, finish

    stacked = tuple(jax.ShapeDtypeStruct((N_CHIPS, R // 2, C), F32) for (R, C) in shapes)
    return _Carried(tuple(grads), stacked, {}, (pltpu.SemaphoreType.DMA((N_CHIPS * n,)),) * 2, steps)


def _carry_chip_scatter(parts):
    n = len(parts)

    def steps(ins, outs, sems):
        send_s, recv_s = sems
        x, y, c = _mesh_pos()
        chips = [(1 - x, y), (x, 1 - y), (1 - x, 1 - y)]

        def copies():
            return [pltpu.make_async_remote_copy(
                src_ref=ins[i].at[2 * cx + cy], dst_ref=outs[i].at[j],
                send_sem=send_s.at[3 * i + j], recv_sem=recv_s.at[3 * i + j],
                device_id=(cx, cy, c), device_id_type=MESH) for i in range(n) for j, (cx, cy) in enumerate(chips)]

        def start():
            for cp in copies():
                cp.start()

        def finish():
            for cp in copies():
                cp.wait()

        return start, finish

    return _Carried(tuple(parts), tuple(jax.ShapeDtypeStruct((3,) + p.shape[1:], p.dtype) for p in parts), {},
                    (pltpu.SemaphoreType.DMA((3 * n,)),) * 2, steps)


def _carry_sibling_share(fulls):
    n = len(fulls)

    def steps(ins, outs, sems):
        send_s, recv_s = sems
        x, y, c = _mesh_pos()

        def copy(i, half):
            rh = fulls[i].shape[0] // 2
            rows = outs[i].at[pl.ds(half * rh, rh), :]
            return pltpu.make_async_remote_copy(src_ref=rows, dst_ref=rows, send_sem=send_s.at[i], recv_sem=recv_s.at[i],
                                                device_id=(x, y, 1 - c), device_id_type=MESH)

        def start():
            for i in range(n):
                copy(i, c).start()

        def finish():
            for i in range(n):
                copy(i, 1 - c).wait_recv()
                copy(i, c).wait_send()

        return start, finish

    return _Carried(tuple(fulls), tuple(jax.ShapeDtypeStruct(f.shape, f.dtype) for f in fulls),
                    {i: i for i in range(n)}, (pltpu.SemaphoreType.DMA((n,)),) * 2, steps)


def _row_block(rows, target=256):
    return target if rows % target == 0 else rows


def _cast_into_full(w, axis, pos, name):
    R, C = w.shape
    rb = _row_block(R)
    nrb = R // rb
    if axis == 1:
        shape, omap = (R, N_CHIPS * C), lambda i, s: (i, s[0])
    else:
        shape, omap = (N_CHIPS * R, C), lambda i, s: (s[0] * nrb + i, 0)

    def body(s_ref, w_ref, o_ref):
        o_ref[...] = w_ref[...].astype(BF16)

    return pl.pallas_call(
        body, out_shape=jax.ShapeDtypeStruct(shape, BF16),
        grid_spec=pltpu.PrefetchScalarGridSpec(
            num_scalar_prefetch=1, grid=(nrb,), in_specs=[pl.BlockSpec((rb, C), lambda i, s: (i, 0))],
            out_specs=pl.BlockSpec((rb, C), omap)),
        name=name)(pos, w)


def _presum(g, theirs, axis, shape, pos, name):
    R, C = shape
    Rh = R // 2
    rb = _row_block(Rh)
    nr = Rh // rb
    if axis == 1:
        gmap = lambda k, i, s: (s[1] * nr + i, k)
    else:
        gmap = lambda k, i, s: (k * (R // rb) + s[1] * nr + i, 0)

    def body(s_ref, g_ref, t_ref, o_ref):
        o_ref[...] = (g_ref[...] + t_ref[...]).astype(BF16)

    spec = pl.BlockSpec((None, rb, C), lambda k, i, s: (k, i, 0))
    return pl.pallas_call(
        body, out_shape=jax.ShapeDtypeStruct((N_CHIPS, Rh, C), BF16),
        grid_spec=pltpu.PrefetchScalarGridSpec(
            num_scalar_prefetch=1, grid=(N_CHIPS, nr), in_specs=[pl.BlockSpec((rb, C), gmap), spec], out_specs=spec),
        name=name)(pos, g, theirs)


def _sum_chips(own, recv, pos, name):
    _, Rh, C = own.shape
    rb = _row_block(Rh)
    nr = Rh // rb

    def body(s_ref, o_ref, r_ref, out_ref):
        acc = o_ref[...].astype(F32)
        for k in range(3):
            acc = acc + r_ref[k].astype(F32)
        out_ref[...] = acc

    return pl.pallas_call(
        body, out_shape=jax.ShapeDtypeStruct((2 * Rh, C), F32),
        grid_spec=pltpu.PrefetchScalarGridSpec(
            num_scalar_prefetch=1, grid=(nr,),
            in_specs=[pl.BlockSpec((None, rb, C), lambda i, s: (s[0], i, 0)), pl.BlockSpec((3, rb, C), lambda i, s: (0, i, 0))],
            out_specs=pl.BlockSpec((rb, C), lambda i, s: (s[1] * nr + i, 0))),
        name=name)(pos, own, recv)


def _sum_devices(parts, name):
    K, R, C = parts.shape

    def body(p_ref, o_ref):
        acc = p_ref[0]
        for k in range(1, K):
            acc = acc + p_ref[k]
        o_ref[...] = acc

    return pl.pallas_call(body, out_shape=jax.ShapeDtypeStruct((R, C), F32), name=name)(parts)


def _adamw_math(w, g, m, v):
    nm = ADAM_B1 * m + (1.0 - ADAM_B1) * g
    nv = ADAM_B2 * v + (1.0 - ADAM_B2) * (g * g)
    m_hat = nm / (1.0 - ADAM_B1 ** ADAM_STEP)
    v_hat = nv / (1.0 - ADAM_B2 ** ADAM_STEP)
    return -ADAM_LR * (m_hat / (jnp.sqrt(v_hat) + ADAM_EPS) + ADAM_WD * w), nm, nv


def _adamw_many(ws, gs, ms, vs):
    n = len(ws)

    def body(*refs):
        for k in range(n):
            w, g, m, v = (refs[j * n + k][...] for j in range(4))
            for j, val in enumerate(_adamw_math(w, g, m, v)):
                refs[(4 + j) * n + k][...] = val

    shapes = [jax.ShapeDtypeStruct(w.shape, F32) for w in ws]
    res = pl.pallas_call(body, out_shape=shapes * 3, compiler_params=_cparams(), name="adamw_small")(*ws, *gs, *ms, *vs)
    return res[:n], res[n:2 * n], res[2 * n:]


def _adamw(w, g, m, v, name):
    R, C = w.shape
    rb = _row_block(R)

    def body(w_ref, g_ref, m_ref, v_ref, d_ref, nm_ref, nv_ref):
        gv = g_ref[...]
        nm = ADAM_B1 * m_ref[...] + (1.0 - ADAM_B1) * gv
        nv = ADAM_B2 * v_ref[...] + (1.0 - ADAM_B2) * (gv * gv)
        m_hat = nm / (1.0 - ADAM_B1 ** ADAM_STEP)
        v_hat = nv / (1.0 - ADAM_B2 ** ADAM_STEP)
        d_ref[...] = -ADAM_LR * (m_hat / (jnp.sqrt(v_hat) + ADAM_EPS) + ADAM_WD * w_ref[...])
        nm_ref[...] = nm
        nv_ref[...] = nv

    spec = pl.BlockSpec((rb, C), lambda i: (i, 0))
    return pl.pallas_call(body, out_shape=[jax.ShapeDtypeStruct((R, C), F32)] * 3, grid=(R // rb,),
                          in_specs=[spec] * 4, out_specs=[spec] * 3, name=name)(w, g, m, v)


def _silu(v):
    return v * jax.nn.sigmoid(v)


def _ada_fwd(c_all, w_sh, b_sh):
    S, D = c_all.shape
    Ca = w_sh.shape[1]
    cb = 512 if Ca % 512 == 0 else Ca

    def body(c_ref, w_ref, b_ref, o_ref):
        act = _silu(c_ref[...]).astype(BF16)
        o_ref[...] = _dot(act, w_ref[...].astype(BF16)) + b_ref[...]

    return pl.pallas_call(
        body, out_shape=jax.ShapeDtypeStruct((S, Ca), F32), grid=(Ca // cb,),
        in_specs=[pl.BlockSpec((S, D), lambda j: (0, 0)), pl.BlockSpec((D, cb), lambda j: (0, j)),
                  pl.BlockSpec((1, cb), lambda j: (0, j))],
        out_specs=pl.BlockSpec((S, cb), lambda j: (0, j)), name="ada_fwd")(c_all, w_sh, b_sh)


def _ada_bwd(c_all, dmod_sh, dmod_all):
    S, D = c_all.shape
    Ca = dmod_sh.shape[1]
    C6 = dmod_all.shape[1]

    def body(c_ref, ds_ref, da_ref, gw_ref, gb_ref):
        act = _silu(c_ref[...]).astype(BF16)
        gw_ref[...] = _dot_tn(act, ds_ref[...].astype(BF16))
        gb_ref[...] = jnp.sum(da_ref[...], axis=0, keepdims=True)

    return pl.pallas_call(
        body, out_shape=[jax.ShapeDtypeStruct((D, Ca), F32), jax.ShapeDtypeStruct((1, C6), F32)],
        compiler_params=_cparams(), name="ada_bwd")(c_all, dmod_sh, dmod_all)


def _rms_fwd(xv):
    r = lax.rsqrt(jnp.mean(xv * xv, axis=-1, keepdims=True) + RMS_EPS)
    return xv * r, r


def _rms_bwd(dxh, xh, r):
    return r * (dxh - xh * jnp.mean(dxh * xh, axis=-1, keepdims=True))


def _const_spec(shape):
    nd = len(shape)
    return pl.BlockSpec(shape, lambda *_: (0,) * nd)


def _seq_spec(D, bps, rev_blocks=None):
    if rev_blocks is None:
        return pl.BlockSpec((None, 1, D), lambda i: (i // bps, 0, 0))
    return pl.BlockSpec((None, 1, D), lambda i: ((rev_blocks - 1 - i) // bps, 0, 0))


def _inproj_fwd(x2, sh1, sc1, n1g, w_in, carried):
    N, D = x2.shape
    IN = w_in.shape[1]
    Bl = sh1.shape[0]
    TB = _row_block(N // Bl)
    bps = (N // Bl) // TB

    def body(x_ref, sh_ref, sc_ref, g_ref, w_ref, p_ref):
        xh, _ = _rms_fwd(x_ref[...])
        h = (xh * g_ref[...]) * (1.0 + sc_ref[...]) + sh_ref[...]
        p_ref[...] = _dot(h.astype(BF16), w_ref[...])

    (p2,), extra = _call_carrying(
        body, carried, (x2, sh1, sc1, n1g, w_in), out_shape=[jax.ShapeDtypeStruct((N, IN), F32)], grid=(N // TB,),
        in_specs=[pl.BlockSpec((TB, D), lambda i: (i, 0)), _seq_spec(D, bps), _seq_spec(D, bps),
                  _const_spec((1, D)), _const_spec((D, IN))],
        out_specs=[pl.BlockSpec((TB, IN), lambda i: (i, 0))],
        compiler_params=_cparams(dimension_semantics=("arbitrary",)), name="inproj_fwd")
    return p2, extra


def _s5_dims(Bl, L, Ds):
    G = Ds // SSM_GROUP
    GP = G * SSM_STATE
    NP = GP // LANES
    T = min(64, L // 2)
    nb = 2 if (Ds // 2) % LANES == 0 else 1
    return G, GP, NP, T, nb


def _s5_disc_math(lr, li, ldt, bt_r, bt_i):
    dt = jnp.exp(ldt)
    er = jnp.exp(lr * dt)
    lbr = er * jnp.cos(li * dt)
    lbi = er * jnp.sin(li * dt)
    den = lr * lr + li * li
    fr = ((lbr - 1.0) * lr + lbi * li) / den
    fi = (lbi * lr - (lbr - 1.0) * li) / den
    return lbr, lbi, fr[None] * bt_r - fi[None] * bt_i, fr[None] * bt_i + fi[None] * bt_r


def _s5_disc(lr, li, ldt, bt_r, bt_i):
    def body(lr_ref, li_ref, ldt_ref, br_ref, bi_ref, o0, o1, o2, o3):
        res = _s5_disc_math(lr_ref[...], li_ref[...], ldt_ref[...], br_ref[...], bi_ref[...])
        for o, v in zip((o0, o1, o2, o3), res):
            o[...] = v

    S = jax.ShapeDtypeStruct
    return pl.pallas_call(body, out_shape=[S(lr.shape, F32)] * 2 + [S(bt_r.shape, F32)] * 2, name="s5_disc")(lr, li, ldt, bt_r, bt_i)


def _s5_disc_bwd(lr, li, ldt, bt_r, bt_i, dlbr, dlbi, dbbr, dbbi):
    def body(lr_ref, li_ref, ldt_ref, br_ref, bi_ref, g0, g1, g2, g3, o0, o1, o2, o3, o4):
        _, vjp = jax.vjp(_s5_disc_math, lr_ref[...], li_ref[...], ldt_ref[...], br_ref[...], bi_ref[...])
        res = vjp((g0[...], g1[...], g2[...], g3[...]))
        for o, v in zip((o0, o1, o2, o3, o4), res):
            o[...] = v

    S = jax.ShapeDtypeStruct
    return pl.pallas_call(body, out_shape=[S(lr.shape, F32)] * 2 + [S(ldt.shape, F32)] + [S(bt_r.shape, F32)] * 2,
                          name="s5_disc_bwd")(lr, li, ldt, bt_r, bt_i, dlbr, dlbi, dbbr, dbbi)


S5_SCAN_PANELS = 4
S5_SCAN_STEPS = 4


def _panel_scan(src_r, src_i, dst_r, dst_i, lr_ref, li_ref, car_r, car_i, NP, Bl, T, TP, adjoint):
    BT = Bl * TP
    PG = min(S5_SCAN_PANELS, NP)
    CH = S5_SCAN_STEPS
    for k0 in range(0, NP, PG):
        ks = list(range(k0, k0 + PG))
        lr = [jnp.broadcast_to(lr_ref[pl.ds(k, 1), :], (Bl, LANES)) for k in ks]
        li = [jnp.broadcast_to(li_ref[pl.ds(k, 1), :], (Bl, LANES)) for k in ks]

        def trip(cc, carry):
            ts = [(T - 1 - (cc * CH + s)) if adjoint else (cc * CH + s) for s in range(CH)]
            idx = [[pl.ds(k * BT + t, Bl, stride=TP) for t in ts] for k in ks]
            loaded = [[(src_r[ix, :], src_i[ix, :]) for ix in idx[j]] for j in range(PG)]
            results, new_carry = [], []
            for j in range(PG):
                ar, ai = carry[j]
                res = []
                for s in range(CH):
                    br, bi = loaded[j][s]
                    if adjoint:
                        ar, ai = br + lr[j] * ar + li[j] * ai, bi + lr[j] * ai - li[j] * ar
                    else:
                        ar, ai = lr[j] * ar - li[j] * ai + br, lr[j] * ai + li[j] * ar + bi
                    res.append((ar, ai))
                results.append(res)
                new_carry.append((ar, ai))
            for j in range(PG):
                for s in range(CH):
                    dst_r[idx[j][s], :] = results[j][s][0]
                    dst_i[idx[j][s], :] = results[j][s][1]
            return tuple(new_carry)

        init = tuple((car_r[pl.ds(k * SUBLANES, Bl), :], car_i[pl.ds(k * SUBLANES, Bl), :]) for k in ks)
        fin = lax.fori_loop(0, T // CH, trip, init)
        for j, k in enumerate(ks):
            car_r[pl.ds(k * SUBLANES, Bl), :] = fin[j][0]
            car_i[pl.ds(k * SUBLANES, Bl), :] = fin[j][1]


def _s5_fwd(p3, Bm_r, Bm_i, Cm_r, Cm_i, lam_r, lam_i, dsk, carried):
    Bl, L, _ = p3.shape
    Ds = dsk.shape[1]
    G, GP, NP, T, nb = _s5_dims(Bl, L, Ds)
    nT = L // T
    TP = T + S5_ROW_PAD
    BT = Bl * TP
    dsb, gpb, npb = Ds // nb, GP // nb, NP // nb

    def body(u_ref, br_ref, bi_ref, cr_ref, ci_ref, lr_ref, li_ref, dsk_ref, sr_ref, si_ref, sb_r, sb_i, y_ref,
             car_r, car_i, upad, ypad, bu_r, bu_i):
        i = pl.program_id(0)

        @pl.when(i == 0)
        def _():
            car_r[...] = jnp.zeros_like(car_r)
            car_i[...] = jnp.zeros_like(car_i)
            upad[...] = jnp.zeros_like(upad)

        zpad = jnp.zeros((S5_ROW_PAD, LANES), F32)
        for k in range(NP):
            for b in range(Bl):
                sr_ref[pl.ds(k * BT + b * TP + T, S5_ROW_PAD), :] = zpad
                si_ref[pl.ds(k * BT + b * TP + T, S5_ROW_PAD), :] = zpad
        for b in range(Bl):
            upad[pl.ds(b * TP, T), :] = u_ref[b]
        u = upad[...]
        ub = u.astype(BF16)
        for blk in range(nb):
            ubb = ub[:, blk * dsb:(blk + 1) * dsb]
            for bu_ref, b_ref in ((bu_r, br_ref), (bu_i, bi_ref)):
                res = _dot(ubb, b_ref[blk])
                for kk in range(npb):
                    k = blk * npb + kk
                    bu_ref[pl.ds(k * BT, BT), :] = res[:, kk * LANES:(kk + 1) * LANES]

        _panel_scan(bu_r, bu_i, sr_ref, si_ref, lr_ref, li_ref, car_r, car_i, NP, Bl, T, TP, adjoint=False)
        sb_r[...] = car_r[...]
        sb_i[...] = car_i[...]

        for blk in range(nb):
            s_r = jnp.concatenate([sr_ref[pl.ds((blk * npb + kk) * BT, BT), :] for kk in range(npb)], axis=1).astype(BF16)
            s_i = jnp.concatenate([si_ref[pl.ds((blk * npb + kk) * BT, BT), :] for kk in range(npb)], axis=1).astype(BF16)
            cols = slice(blk * dsb, (blk + 1) * dsb)
            ypad[:, cols] = _dot(s_r, cr_ref[blk]) + _dot(s_i, ci_ref[blk]) + dsk_ref[:, cols] * u[:, cols]
        for b in range(Bl):
            y_ref[b] = ypad[pl.ds(b * TP, T), :]

    S = jax.ShapeDtypeStruct
    state = S((nT, NP * BT, LANES), F32)
    bound = S((nT, NP * SUBLANES, LANES), F32)
    sspec = pl.BlockSpec((None, NP * BT, LANES), lambda i: (i, 0, 0))
    bspec = pl.BlockSpec((None, NP * SUBLANES, LANES), lambda i: (i, 0, 0))
    return _call_carrying(
        body, carried, (p3, Bm_r, Bm_i, Cm_r, Cm_i, lam_r, lam_i, dsk),
        out_shape=[state, state, bound, bound, S((Bl, L, Ds), F32)], grid=(nT,),
        in_specs=[pl.BlockSpec((Bl, T, Ds), lambda i: (0, i, 0)),
                  _const_spec((nb, dsb, gpb)), _const_spec((nb, dsb, gpb)),
                  _const_spec((nb, gpb, dsb)), _const_spec((nb, gpb, dsb)),
                  _const_spec((NP, LANES)), _const_spec((NP, LANES)), _const_spec((1, Ds))],
        out_specs=[sspec, sspec, bspec, bspec, pl.BlockSpec((Bl, T, Ds), lambda i: (0, i, 0))],
        scratch_shapes=[pltpu.VMEM((NP * SUBLANES, LANES), F32)] * 2 + [pltpu.VMEM((BT, Ds), F32)] * 2
        + [pltpu.VMEM((NP * BT, LANES), F32)] * 2,
        compiler_params=_cparams(dimension_semantics=("arbitrary",)), name="s5_fwd")


def _s5_bwd(dy3, p3, dp3, Sr, Si, Sb_r, Sb_i, Bm_r, Bm_i, Cm_r, Cm_i, lam_r, lam_i, dsk, carried):
    Bl, L, Ds = dy3.shape
    G, GP, NP, T, nb = _s5_dims(Bl, L, Ds)
    nT = L // T
    TP = T + S5_ROW_PAD
    BT = Bl * TP
    dsb, gpb, npb = Ds // nb, GP // nb, NP // nb

    def body(dy_ref, u_ref, dp_ref, sr_ref, si_ref, sbr_ref, sbi_ref, br_ref, bi_ref, cr_ref, ci_ref, lr_ref, li_ref, dsk_ref,
             du_ref, dbr_ref, dbi_ref, dcr_ref, dci_ref, dlr_ref, dli_ref, ddsk_ref,
             a_r, a_i, car_r, car_i, acc_r, acc_i, dypad, upad, dupad):
        i = pl.program_id(0)

        @pl.when(i == 0)
        def _():
            for ref in (car_r, car_i, acc_r, acc_i, dbr_ref, dbi_ref, dcr_ref, dci_ref, ddsk_ref, dypad, upad):
                ref[...] = jnp.zeros_like(ref)

        for b in range(Bl):
            dypad[pl.ds(b * TP, T), :] = dy_ref[b]
            upad[pl.ds(b * TP, T), :] = u_ref[b]
        dy = dypad[...]
        dyb = dy.astype(BF16)
        u = upad[...]
        ub = u.astype(BF16)
        for blk in range(nb):
            dyb_b = dyb[:, blk * dsb:(blk + 1) * dsb]
            for q_ref, c_ref in ((a_r, cr_ref), (a_i, ci_ref)):
                res = _dot_nt(dyb_b, c_ref[blk])
                for kk in range(npb):
                    q_ref[pl.ds((blk * npb + kk) * BT, BT), :] = res[:, kk * LANES:(kk + 1) * LANES]

        _panel_scan(a_r, a_i, a_r, a_i, lr_ref, li_ref, car_r, car_i, NP, Bl, T, TP, adjoint=True)

        first_block = (i == nT - 1)
        for k in range(NP):
            rows = pl.ds(k * BT, BT)
            av_r, av_i = a_r[rows, :], a_i[rows, :]
            sp_r = pltpu.roll(sr_ref[rows, :], 1, 0)
            sp_i = pltpu.roll(si_ref[rows, :], 1, 0)
            acc = pl.ds(k * SUBLANES, SUBLANES)
            acc_r[acc, :] += jnp.sum((av_r * sp_r + av_i * sp_i).reshape(BT // SUBLANES, SUBLANES, LANES), axis=0)
            acc_i[acc, :] += jnp.sum((av_i * sp_r - av_r * sp_i).reshape(BT // SUBLANES, SUBLANES, LANES), axis=0)
            t0 = pl.ds(k * BT, Bl, stride=TP)
            a0_r, a0_i = a_r[t0, :], a_i[t0, :]
            brow = pl.ds(k * SUBLANES, Bl)
            sb_pr = jnp.where(first_block, 0.0, sbr_ref[brow, :])
            sb_pi = jnp.where(first_block, 0.0, sbi_ref[brow, :])
            acc_r[brow, :] += a0_r * sb_pr + a0_i * sb_pi
            acc_i[brow, :] += a0_i * sb_pr - a0_r * sb_pi

        ddsk_ref[...] += jnp.sum(dy * u, axis=0, keepdims=True)
        for blk in range(nb):
            cols = slice(blk * dsb, (blk + 1) * dsb)
            rows = [pl.ds((blk * npb + kk) * BT, BT) for kk in range(npb)]
            av_r = jnp.concatenate([a_r[r, :] for r in rows], axis=1).astype(BF16)
            av_i = jnp.concatenate([a_i[r, :] for r in rows], axis=1).astype(BF16)
            dupad[:, cols] = _dot_nt(av_r, br_ref[blk]) + _dot_nt(av_i, bi_ref[blk]) + dy[:, cols] * dsk_ref[:, cols]
            dbr_ref[blk] += _dot_tn(ub[:, cols], av_r)
            dbi_ref[blk] += _dot_tn(ub[:, cols], av_i)
            sv_r = jnp.concatenate([sr_ref[r, :] for r in rows], axis=1).astype(BF16)
            sv_i = jnp.concatenate([si_ref[r, :] for r in rows], axis=1).astype(BF16)
            dcr_ref[blk] += _dot_tn(sv_r, dyb[:, cols])
            dci_ref[blk] += _dot_tn(sv_i, dyb[:, cols])
        for b in range(Bl):
            du_ref[b] = dupad[pl.ds(b * TP, T), :].astype(BF16)

        @pl.when(i == nT - 1)
        def _():
            for k in range(NP):
                dlr_ref[pl.ds(k, 1), :] = jnp.sum(acc_r[pl.ds(k * SUBLANES, SUBLANES), :], axis=0, keepdims=True)
                dli_ref[pl.ds(k, 1), :] = jnp.sum(acc_i[pl.ds(k * SUBLANES, SUBLANES), :], axis=0, keepdims=True)

    S = jax.ShapeDtypeStruct
    rev = lambda i: nT - 1 - i
    sspec = pl.BlockSpec((None, NP * BT, LANES), lambda i: (rev(i), 0, 0))
    bspec = pl.BlockSpec((None, NP * SUBLANES, LANES), lambda i: (jnp.maximum(rev(i) - 1, 0), 0, 0))
    tspec = pl.BlockSpec((Bl, T, Ds), lambda i: (0, rev(i), 0))
    return _call_carrying(
        body, carried, (dy3, p3, dp3, Sr, Si, Sb_r, Sb_i, Bm_r, Bm_i, Cm_r, Cm_i, lam_r, lam_i, dsk),
        out_shape=[S(dp3.shape, dp3.dtype), S((nb, dsb, gpb), F32), S((nb, dsb, gpb), F32),
                   S((nb, gpb, dsb), F32), S((nb, gpb, dsb), F32), S((NP, LANES), F32), S((NP, LANES), F32), S((1, Ds), F32)],
        grid=(nT,),
        in_specs=[tspec, tspec, ANY, sspec, sspec, bspec, bspec,
                  _const_spec((nb, dsb, gpb)), _const_spec((nb, dsb, gpb)),
                  _const_spec((nb, gpb, dsb)), _const_spec((nb, gpb, dsb)),
                  _const_spec((NP, LANES)), _const_spec((NP, LANES)), _const_spec((1, Ds))],
        out_specs=[tspec, _const_spec((nb, dsb, gpb)), _const_spec((nb, dsb, gpb)),
                   _const_spec((nb, gpb, dsb)), _const_spec((nb, gpb, dsb)),
                   _const_spec((NP, LANES)), _const_spec((NP, LANES)), _const_spec((1, Ds))],
        aliases={2: 0},
        scratch_shapes=[pltpu.VMEM((NP * BT, LANES), F32)] * 2 + [pltpu.VMEM((NP * SUBLANES, LANES), F32)] * 4
        + [pltpu.VMEM((BT, Ds), F32)] * 3,
        compiler_params=_cparams(dimension_semantics=("arbitrary",)), name="s5_bwd")


def _mix_values(ylin, cb, cc, cx, gs, gc, halo_v, wglu, bglu, cw, wps, wpc, wout):
    yg, gelu_vjp = jax.vjp(jax.nn.gelu, ylin)
    sz = jax.nn.sigmoid(_dot(yg.astype(BF16), wglu) + bglu)
    ys = yg * sz
    v = cc * cx
    rows = lax.broadcasted_iota(jnp.int32, v.shape, 0)
    h6 = halo_v[SUBLANES - 2:SUBLANES - 1, :]
    h7 = halo_v[SUBLANES - 1:SUBLANES, :]
    v1 = jnp.where(rows == 0, h7, pltpu.roll(v, 1, 0))
    v2 = jnp.where(rows == 0, h6, jnp.where(rows == 1, h7, pltpu.roll(v, 2, 0)))
    cv = cw[0:1, :] * v2 + cw[1:2, :] * v1 + cw[2:3, :] * v
    yc = cb * cv
    ps = _dot(ys.astype(BF16), wps)
    pc = _dot(yc.astype(BF16), wpc)
    sgs = jax.nn.sigmoid(gs)
    sgc = jax.nn.sigmoid(gc)
    merged = sgs * ps + sgc * pc
    mo = _dot(merged.astype(BF16), wout)
    return dict(yg=yg, gelu_vjp=gelu_vjp, sz=sz, ys=ys, v=v, v1=v1, v2=v2, cv=cv, yc=yc, ps=ps, pc=pc,
                sgs=sgs, sgc=sgc, merged=merged, mo=mo)


def _mix_in_specs(TB, D, Ds, Dc, bps, blk):
    hb = TB // SUBLANES
    halo = lambda col: pl.BlockSpec((SUBLANES, Dc), lambda i: (jnp.maximum(blk(i) * hb - 1, 0), col))
    return [pl.BlockSpec((TB, Dc), lambda i: (blk(i), 1)), pl.BlockSpec((TB, Dc), lambda i: (blk(i), 2)),
            pl.BlockSpec((TB, Dc), lambda i: (blk(i), 3)), pl.BlockSpec((TB, D), lambda i: (blk(i), 2)),
            pl.BlockSpec((TB, D), lambda i: (blk(i), 3)), halo(2), halo(3),
            pl.BlockSpec((TB, Ds), lambda i: (blk(i), 0))]


def _mix_fwd(p2, ylin2, x2, g1, wglu, bglu, cw, wps, wpc, wout):
    N, D = x2.shape
    Ds = ylin2.shape[1]
    Dc = Ds
    Bl = g1.shape[0]
    TB = _row_block(N // Bl)
    bps = (N // Bl) // TB

    def body(cb_ref, cc_ref, cx_ref, gs_ref, gc_ref, hcc_ref, hcx_ref, yl_ref, x_ref, g1_ref,
             wglu_ref, bglu_ref, cw_ref, wps_ref, wpc_ref, wout_ref, x1_ref):
        i = pl.program_id(0)
        halo_v = jnp.where(i % bps == 0, 0.0, hcc_ref[...] * hcx_ref[...])
        f = _mix_values(yl_ref[...], cb_ref[...], cc_ref[...], cx_ref[...], gs_ref[...], gc_ref[...], halo_v,
                        wglu_ref[...], bglu_ref[...], cw_ref[...], wps_ref[...], wpc_ref[...], wout_ref[...])
        x1_ref[...] = x_ref[...] + g1_ref[...] * f["mo"]

    return pl.pallas_call(
        body, out_shape=jax.ShapeDtypeStruct((N, D), F32), grid=(N // TB,),
        in_specs=_mix_in_specs(TB, D, Ds, Dc, bps, lambda i: i) + [
            pl.BlockSpec((TB, D), lambda i: (i, 0)), _seq_spec(D, bps),
            _const_spec((Ds, Ds)), _const_spec((1, Ds)), _const_spec((SUBLANES, Dc)),
            _const_spec((Ds, D)), _const_spec((Dc, D)), _const_spec((D, D))],
        out_specs=pl.BlockSpec((TB, D), lambda i: (i, 0)),
        compiler_params=_cparams(), name="mix_fwd",
    )(p2, p2, p2, p2, p2, p2, p2, ylin2, x2, g1, wglu, bglu, cw, wps, wpc, wout)


def _mix_bwd(p2, ylin2, dx1, g1, wglu, bglu, cw, wps, wpc, wout, carried):
    N, D = dx1.shape
    Ds = ylin2.shape[1]
    Dc = Ds
    IN = p2.shape[1]
    Bl = g1.shape[0]
    TB = _row_block(N // Bl)
    bps = (N // Bl) // TB
    nblk = N // TB
    rev = lambda i: nblk - 1 - i

    def body(cb_ref, cc_ref, cx_ref, gs_ref, gc_ref, hcc_ref, hcx_ref, yl_ref, dx1_ref, g1_ref,
             wglu_ref, bglu_ref, cw_ref, wps_ref, wpc_ref, wout_ref,
             dyl_ref, dp_ref, gwout_ref, gwps_ref, gwpc_ref, gwglu_ref, gbglu_ref, gcw_ref, dg1_ref, nxt):
        i = pl.program_id(0)
        blk = rev(i)

        @pl.when(i == 0)
        def _():
            for ref in (gwout_ref, gwps_ref, gwpc_ref, gwglu_ref, gbglu_ref, gcw_ref):
                ref[...] = jnp.zeros_like(ref)

        @pl.when(i % bps == 0)
        def _():
            nxt[...] = jnp.zeros_like(nxt)
            dg1_ref[...] = jnp.zeros_like(dg1_ref)

        cb, cc, cx = cb_ref[...], cc_ref[...], cx_ref[...]
        halo_v = jnp.where(blk % bps == 0, 0.0, hcc_ref[...] * hcx_ref[...])
        wglu, wps, wpc, wout, cw = wglu_ref[...], wps_ref[...], wpc_ref[...], wout_ref[...], cw_ref[...]
        f = _mix_values(yl_ref[...], cb, cc, cx, gs_ref[...], gc_ref[...], halo_v, wglu, bglu_ref[...], cw, wps, wpc, wout)

        dx1v = dx1_ref[...]
        dg1_ref[...] += jnp.sum(dx1v * f["mo"], axis=0, keepdims=True)
        dmo = (g1_ref[...] * dx1v).astype(BF16)
        gwout_ref[...] += _dot_tn(f["merged"].astype(BF16), dmo)
        dmerged = _dot_nt(dmo, wout)
        dps = dmerged * f["sgs"]
        dpc = dmerged * f["sgc"]
        dgs = dmerged * f["ps"] * f["sgs"] * (1.0 - f["sgs"])
        dgc = dmerged * f["pc"] * f["sgc"] * (1.0 - f["sgc"])
        dpsb, dpcb = dps.astype(BF16), dpc.astype(BF16)
        gwps_ref[...] += _dot_tn(f["ys"].astype(BF16), dpsb)
        gwpc_ref[...] += _dot_tn(f["yc"].astype(BF16), dpcb)
        dys = _dot_nt(dpsb, wps)
        dyc = _dot_nt(dpcb, wpc)

        dcb = dyc * f["cv"]
        dcv = dyc * cb
        rows = lax.broadcasted_iota(jnp.int32, dcv.shape, 0)
        n0, n1 = nxt[0:1, :], nxt[1:2, :]
        d1 = jnp.where(rows == TB - 1, n0, pltpu.roll(dcv, TB - 1, 0))
        d2 = jnp.where(rows == TB - 2, n0, jnp.where(rows == TB - 1, n1, pltpu.roll(dcv, TB - 2, 0)))
        dv = cw[2:3, :] * dcv + cw[1:2, :] * d1 + cw[0:1, :] * d2
        nxt[0:2, :] = dcv[0:2, :]
        gcw_ref[0:1, :] += jnp.sum(dcv * f["v2"], axis=0, keepdims=True)
        gcw_ref[1:2, :] += jnp.sum(dcv * f["v1"], axis=0, keepdims=True)
        gcw_ref[2:3, :] += jnp.sum(dcv * f["v"], axis=0, keepdims=True)

        dz = dys * f["yg"] * f["sz"] * (1.0 - f["sz"])
        dzb = dz.astype(BF16)
        gwglu_ref[...] += _dot_tn(f["yg"].astype(BF16), dzb)
        gbglu_ref[...] += jnp.sum(dz, axis=0, keepdims=True)
        dyg = dys * f["sz"] + _dot_nt(dzb, wglu)
        dyl_ref[...] = f["gelu_vjp"](dyg)[0]

        dp_ref[:, Ds:Ds + Dc] = dcb.astype(BF16)
        dp_ref[:, Ds + Dc:Ds + 2 * Dc] = (dv * cx).astype(BF16)
        dp_ref[:, Ds + 2 * Dc:Ds + 3 * Dc] = (dv * cc).astype(BF16)
        dp_ref[:, Ds + 3 * Dc:Ds + 3 * Dc + D] = dgs.astype(BF16)
        dp_ref[:, Ds + 3 * Dc + D:IN] = dgc.astype(BF16)

    S = jax.ShapeDtypeStruct
    return _call_carrying(
        body, carried, (p2, p2, p2, p2, p2, p2, p2, ylin2, dx1, g1, wglu, bglu, cw, wps, wpc, wout),
        out_shape=[S((N, Ds), F32), S((N, IN), BF16), S((D, D), F32), S((Ds, D), F32), S((Dc, D), F32),
                   S((Ds, Ds), F32), S((1, Ds), F32), S((SUBLANES, Dc), F32), S((Bl, 1, D), F32)],
        grid=(nblk,),
        in_specs=_mix_in_specs(TB, D, Ds, Dc, bps, rev) + [
            pl.BlockSpec((TB, D), lambda i: (rev(i), 0)), _seq_spec(D, bps, nblk),
            _const_spec((Ds, Ds)), _const_spec((1, Ds)), _const_spec((SUBLANES, Dc)),
            _const_spec((Ds, D)), _const_spec((Dc, D)), _const_spec((D, D))],
        out_specs=[pl.BlockSpec((TB, Ds), lambda i: (rev(i), 0)), pl.BlockSpec((TB, IN), lambda i: (rev(i), 0)),
                   _const_spec((D, D)), _const_spec((Ds, D)), _const_spec((Dc, D)), _const_spec((Ds, Ds)),
                   _const_spec((1, Ds)), _const_spec((SUBLANES, Dc)), _seq_spec(D, bps, nblk)],
        scratch_shapes=[pltpu.VMEM((SUBLANES, Dc), F32)],
        compiler_params=_cparams(dimension_semantics=("arbitrary",)), name="mix_bwd")


def _mlp_fwd_bwd(x1, tgt, sh2, sc2, g2, n2g, fg, w1, w2):
    N, D = x1.shape
    Dff = w1.shape[1]
    Bl = sh2.shape[0]
    TB = _row_block(N // Bl)
    bps = (N // Bl) // TB

    def body(x1_ref, t_ref, sh_ref, sc_ref, g2_ref, n2_ref, fg_ref, w1_ref, w2_ref,
             dx1_ref, h2_ref, da_ref, sq_ref, df_ref, loss_ref, gfg_ref, gn2_ref, dsh_ref, dsc_ref, dg2_ref):
        i = pl.program_id(0)

        @pl.when(i == 0)
        def _():
            for ref in (loss_ref, gfg_ref, gn2_ref):
                ref[...] = jnp.zeros_like(ref)

        @pl.when(i % bps == 0)
        def _():
            for ref in (dsh_ref, dsc_ref, dg2_ref):
                ref[...] = jnp.zeros_like(ref)

        x1v = x1_ref[...]
        sc, g2v, n2 = sc_ref[...], g2_ref[...], n2_ref[...]
        xh2, r2 = _rms_fwd(x1v)
        xn2 = xh2 * n2
        h2 = (xn2 * (1.0 + sc) + sh_ref[...]).astype(BF16)
        a = _dot(h2, w1_ref[...])
        ra = jnp.maximum(a, 0.0)
        sq = (ra * ra).astype(BF16)
        fv = _dot(sq, w2_ref[...])
        x2 = x1v + g2v * fv
        xh3, r3 = _rms_fwd(x2)
        err = xh3 * fg_ref[...] - t_ref[...]
        loss_ref[...] += 0.5 * jnp.sum(jnp.mean(err * err, axis=-1, keepdims=True), axis=0, keepdims=True)
        dy = err * (1.0 / D)
        gfg_ref[...] += jnp.sum(dy * xh3, axis=0, keepdims=True)
        dx2 = _rms_bwd(dy * fg_ref[...], xh3, r3)
        dg2_ref[...] += jnp.sum(dx2 * fv, axis=0, keepdims=True)
        df = (g2v * dx2).astype(BF16)
        dsq = _dot_nt(df, w2_ref[...])
        da = (2.0 * ra * dsq).astype(BF16)
        dh2 = _dot_nt(da, w1_ref[...])
        dsh_ref[...] += jnp.sum(dh2, axis=0, keepdims=True)
        dsc_ref[...] += jnp.sum(dh2 * xn2, axis=0, keepdims=True)
        dxn2 = dh2 * (1.0 + sc)
        gn2_ref[...] += jnp.sum(dxn2 * xh2, axis=0, keepdims=True)
        dx1_ref[...] = dx2 + _rms_bwd(dxn2 * n2, xh2, r2)
        h2_ref[...] = h2
        da_ref[...] = da
        sq_ref[...] = sq
        df_ref[...] = df

    S = jax.ShapeDtypeStruct
    row = lambda w: pl.BlockSpec((TB, w), lambda i: (i, 0))
    return pl.pallas_call(
        body,
        out_shape=[S((N, D), F32), S((N, D), BF16), S((N, Dff), BF16), S((N, Dff), BF16), S((N, D), BF16),
                   S((1, 1), F32), S((1, D), F32), S((1, D), F32), S((Bl, 1, D), F32), S((Bl, 1, D), F32), S((Bl, 1, D), F32)],
        grid=(N // TB,),
        in_specs=[row(D), row(D), _seq_spec(D, bps), _seq_spec(D, bps), _seq_spec(D, bps),
                  _const_spec((1, D)), _const_spec((1, D)), _const_spec((D, Dff)), _const_spec((Dff, D))],
        out_specs=[row(D), row(D), row(Dff), row(Dff), row(D), _const_spec((1, 1)), _const_spec((1, D)), _const_spec((1, D)),
                   _seq_spec(D, bps), _seq_spec(D, bps), _seq_spec(D, bps)],
        compiler_params=_cparams(dimension_semantics=("arbitrary",)), name="mlp_fwd_bwd",
    )(x1, tgt, sh2, sc2, g2, n2g, fg, w1, w2)


def _grad_w(a, b, name):
    N, K1 = a.shape
    K2 = b.shape[1]
    t1 = 1024 if K1 % 1024 == 0 else K1
    t2 = 1024 if K2 % 1024 == 0 else K2
    tn = 2048 if N % 2048 == 0 else N

    def body(a_ref, b_ref, o_ref):
        @pl.when(pl.program_id(2) == 0)
        def _():
            o_ref[...] = jnp.zeros_like(o_ref)

        o_ref[...] += _dot_tn(a_ref[...], b_ref[...])

    return pl.pallas_call(
        body, out_shape=jax.ShapeDtypeStruct((K1, K2), F32), grid=(K1 // t1, K2 // t2, N // tn),
        in_specs=[pl.BlockSpec((tn, t1), lambda i, j, k: (k, i)), pl.BlockSpec((tn, t2), lambda i, j, k: (k, j))],
        out_specs=pl.BlockSpec((t1, t2), lambda i, j, k: (i, j)),
        compiler_params=_cparams(dimension_semantics=("arbitrary", "arbitrary", "arbitrary")), name=name)(a, b)


def _inproj_bwd(x2, dx1, dp, sh1, sc1, n1g, w_in, carried):
    N, D = x2.shape
    IN = w_in.shape[1]
    Bl = sh1.shape[0]
    TB = _row_block(N // Bl)
    bps = (N // Bl) // TB

    def body(x_ref, dx1_ref, dp_ref, sh_ref, sc_ref, g_ref, w_ref, gx_ref, h_ref, gn1_ref, dsh_ref, dsc_ref):
        i = pl.program_id(0)

        @pl.when(i == 0)
        def _():
            gn1_ref[...] = jnp.zeros_like(gn1_ref)

        @pl.when(i % bps == 0)
        def _():
            dsh_ref[...] = jnp.zeros_like(dsh_ref)
            dsc_ref[...] = jnp.zeros_like(dsc_ref)

        sc, n1 = sc_ref[...], g_ref[...]
        xh, r = _rms_fwd(x_ref[...])
        xn = xh * n1
        h_ref[...] = (xn * (1.0 + sc) + sh_ref[...]).astype(BF16)
        dh = _dot_nt(dp_ref[...], w_ref[...])
        dsh_ref[...] += jnp.sum(dh, axis=0, keepdims=True)
        dsc_ref[...] += jnp.sum(dh * xn, axis=0, keepdims=True)
        dxn = dh * (1.0 + sc)
        gn1_ref[...] += jnp.sum(dxn * xh, axis=0, keepdims=True)
        gx_ref[...] = dx1_ref[...] + _rms_bwd(dxn * n1, xh, r)

    S = jax.ShapeDtypeStruct
    row = lambda w: pl.BlockSpec((TB, w), lambda i: (i, 0))
    return _call_carrying(
        body, carried, (x2, dx1, dp, sh1, sc1, n1g, w_in),
        out_shape=[S((N, D), F32), S((N, D), BF16), S((1, D), F32), S((Bl, 1, D), F32), S((Bl, 1, D), F32)],
        grid=(N // TB,),
        in_specs=[row(D), row(D), row(IN), _seq_spec(D, bps), _seq_spec(D, bps), _const_spec((1, D)), _const_spec((D, IN))],
        out_specs=[row(D), row(D), _const_spec((1, D)), _seq_spec(D, bps), _seq_spec(D, bps)],
        compiler_params=_cparams(dimension_semantics=("arbitrary",)), name="inproj_bwd")


def _diag_blocks_from_groups(m, nb):
    G, a, b = m.shape
    gb = G // nb
    eye = jnp.eye(gb, dtype=m.dtype)
    mm = m.reshape(nb, gb, a, b)
    return (mm[:, :, :, None, :] * eye[None, :, None, :, None]).reshape(nb, gb * a, gb * b)


def _groups_from_diag_blocks(d, G, a, b):
    nb = d.shape[0]
    gb = G // nb
    dd = d.reshape(nb, gb, a, gb, b)
    idx = jnp.arange(gb)
    return dd[:, idx, :, idx, :].transpose(1, 0, 2, 3).reshape(G, a, b)


def _pad_rows(v, rows):
    return jnp.concatenate([v, jnp.zeros((rows - v.shape[0],) + v.shape[1:], v.dtype)], axis=0)


def _pack(vs):
    flat = jnp.concatenate([v.reshape(-1) for v in vs])
    n = flat.shape[0]
    tile = SUBLANES * LANES
    npad = -(-n // tile) * tile
    flat = jnp.concatenate([flat, jnp.zeros((npad - n,), flat.dtype)])
    return flat.reshape(npad // LANES, LANES)


def _unpack(packed, shapes):
    flat = packed.reshape(-1)
    out, off = [], 0
    for s in shapes:
        n = 1
        for d in s:
            n *= d
        out.append(flat[off:off + n].reshape(s))
        off += n
    return out


def kernel(x, c, norm1_g, norm2_g, w_ada, b_ada, w_in, lam_re, lam_im, log_dt, b_re, b_im, c_re, c_im, d_skip, w_glu, b_glu, conv_w, w_proj_ssm, w_proj_conv, w_out, w_ff1, w_ff2, final_g, loss_target, m_norm1_g, m_norm2_g, m_w_ada, m_b_ada, m_w_in, m_lam_re, m_lam_im, m_log_dt, m_b_re, m_b_im, m_c_re, m_c_im, m_d_skip, m_w_glu, m_b_glu, m_conv_w, m_w_proj_ssm, m_w_proj_conv, m_w_out, m_w_ff1, m_w_ff2, m_final_g, v_norm1_g, v_norm2_g, v_w_ada, v_b_ada, v_w_in, v_lam_re, v_lam_im, v_log_dt, v_b_re, v_b_im, v_c_re, v_c_im, v_d_skip, v_w_glu, v_b_glu, v_conv_w, v_w_proj_ssm, v_w_proj_conv, v_w_out, v_w_ff1, v_w_ff2, v_final_g):
    Bl, L, D = x.shape
    N = Bl * L
    Ds = Dc = D // 2
    G, H, P = Ds // SSM_GROUP, SSM_GROUP, SSM_STATE
    GP = G * P
    NP = GP // LANES
    nb = _s5_dims(Bl, L, Ds)[4]
    IN = Ds + 3 * Dc + 2 * D
    ax, ay, ac = _mesh_pos()
    q = 2 * ax + ay
    dev = 2 * q + ac

    big_names = ["w_in", "w_ff1", "w_ff2", "w_out", "w_proj_ssm", "w_proj_conv", "w_glu"]
    big_w = dict(w_in=w_in[0], w_ff1=w_ff1[0], w_ff2=w_ff2[0], w_out=w_out[0],
                 w_proj_ssm=w_proj_ssm[0], w_proj_conv=w_proj_conv[0], w_glu=w_glu[0])
    big_axis = dict(w_in=1, w_ff1=1, w_ff2=0, w_out=0, w_proj_ssm=1, w_proj_conv=1, w_glu=0)
    axes = [big_axis[k] for k in big_names]
    shard_shapes = [big_w[k].shape for k in big_names]
    pos = jnp.stack([q, ac]).astype(jnp.int32)
    own_only = {k: _cast_into_full(big_w[k], big_axis[k], pos, "cast_" + k) for k in big_names}
    full = {"w_in": _run_carried(_carry_allgather([own_only["w_in"]], [big_axis["w_in"]], [big_w["w_in"].shape]),
                                 "allgather_w_in")[0]}

    Dcs = conv_w.shape[2]
    first = _allgather8(_pack([c, conv_w[0]]), "allgather_c").reshape(N_DEV, -1)
    c_all = first[:, :Bl * D].reshape(N_DEV * Bl, D)
    cw = first[0::2, Bl * D:Bl * D + 3 * Dcs].reshape(N_CHIPS, 3, Dcs).transpose(1, 0, 2).reshape(3, Dc)
    cw8 = _pad_rows(cw, SUBLANES)
    Ca = w_ada.shape[2]
    b_ada_sh = lax.dynamic_slice_in_dim(b_ada, q * Ca, Ca, axis=1)
    mod_part = _ada_fwd(c_all, w_ada[0], b_ada_sh)
    mod_g = _allgather8(mod_part, "allgather_mod")
    mod_all = mod_g[0::2].transpose(1, 0, 2).reshape(N_DEV * Bl, N_CHIPS * Ca)
    mod = lax.dynamic_slice_in_dim(mod_all, dev * Bl, Bl, axis=0)
    sh1, sc1, g1, sh2, sc2, g2 = [mod[:, k * D:(k + 1) * D].reshape(Bl, 1, D) for k in range(6)]

    ldt_c = log_dt[0].reshape(G, 1)
    bt_r = b_re[0].transpose(2, 0, 1)
    bt_i = b_im[0].transpose(2, 0, 1)
    lbr, lbi, bbt_r, bbt_i = _s5_disc(lam_re[0], lam_im[0], ldt_c, bt_r, bt_i)
    lam_r_p = lbr.reshape(NP, LANES)
    lam_i_p = lbi.reshape(NP, LANES)
    Bm_r = _diag_blocks_from_groups(bbt_r.transpose(1, 0, 2), nb).astype(BF16)
    Bm_i = _diag_blocks_from_groups(bbt_i.transpose(1, 0, 2), nb).astype(BF16)
    Cm_r = _diag_blocks_from_groups(c_re[0].transpose(0, 2, 1), nb).astype(BF16)
    Cm_i = _diag_blocks_from_groups(-c_im[0].transpose(0, 2, 1), nb).astype(BF16)

    x2 = x.reshape(N, D)
    mixer_w = ["w_out", "w_proj_ssm", "w_proj_conv", "w_glu"]
    mlp_w = ["w_ff1", "w_ff2"]
    layout = lambda ks: ([big_axis[k] for k in ks], [big_w[k].shape for k in ks])
    gather = lambda ks: _carry_allgather([own_only[k] for k in ks], *layout(ks))
    p2, gathered = _inproj_fwd(x2, sh1, sc1, norm1_g, full["w_in"], gather(mixer_w))
    full.update(zip(mixer_w, gathered))
    p3 = p2.reshape(Bl, L, IN)
    (Sr, Si, Sb_r, Sb_i, ylin3), gathered = _s5_fwd(p3, Bm_r, Bm_i, Cm_r, Cm_i, lam_r_p, lam_i_p, d_skip, gather(mlp_w))
    full.update(zip(mlp_w, gathered))
    ylin2 = ylin3.reshape(N, Ds)
    mix_w = (full["w_glu"], b_glu, cw8, full["w_proj_ssm"], full["w_proj_conv"], full["w_out"])
    x1 = _mix_fwd(p2, ylin2, x2, g1, *mix_w)

    (dx1, h2b, dab, sqb, dfb, loss_p, g_fg, g_n2, dsh2, dsc2, dg2) = _mlp_fwd_bwd(
        x1, loss_target.reshape(N, D), sh2, sc2, g2, norm2_g, final_g.reshape(1, D), full["w_ff1"], full["w_ff2"])
    g_full = {"w_ff1": _grad_w(h2b, dab, "grad_w_ff1"), "w_ff2": _grad_w(sqb, dfb, "grad_w_ff2")}

    exchange = lambda ks: _carry_sibling_exchange([g_full[k] for k in ks], *layout(ks))
    presum = lambda ks, theirs: [_presum(g_full[k], t, big_axis[k], big_w[k].shape, pos, "presum_" + k) for k, t in zip(ks, theirs)]
    chip_sum = lambda ks, parts, recv: [_sum_chips(p, r, pos, "sum_" + k) for k, p, r in zip(ks, parts, recv)]

    (dyl2, dp2, gw_out, gw_ps, gw_pc, gw_glu, gb_glu, gcw8, dg1), theirs_mlp = _mix_bwd(
        p2, ylin2, dx1, g1, *mix_w, exchange(mlp_w))
    g_full.update(w_out=gw_out, w_proj_ssm=gw_ps, w_proj_conv=gw_pc, w_glu=gw_glu)
    parts_mlp = presum(mlp_w, theirs_mlp)
    (dp3, dBm_r, dBm_i, dCm_r, dCm_i, dlam_r_p, dlam_i_p, g_dsk), extra = _s5_bwd(
        dyl2.reshape(Bl, L, Ds), p3, dp2.reshape(Bl, L, IN), Sr, Si, Sb_r, Sb_i, Bm_r, Bm_i, Cm_r, Cm_i, lam_r_p, lam_i_p,
        d_skip, _carry_join(_carry_chip_scatter(parts_mlp), exchange(mixer_w)))
    recv_mlp, theirs_mix = extra[:len(mlp_w)], extra[len(mlp_w):]
    halves_mlp = chip_sum(mlp_w, parts_mlp, recv_mlp)
    parts_mix = presum(mixer_w, theirs_mix)
    dp_all = dp3.reshape(N, IN)
    (grad_x2, hb, g_n1, dsh1, dsc1), extra = _inproj_bwd(
        x2, dx1, dp_all, sh1, sc1, norm1_g, full["w_in"],
        _carry_join(_carry_sibling_share(halves_mlp), _carry_chip_scatter(parts_mix)))
    reduced = dict(zip(mlp_w, extra[:len(mlp_w)]))
    halves_mix = chip_sum(mixer_w, parts_mix, extra[len(mlp_w):])
    g_full["w_in"] = _grad_w(hb, dp_all, "grad_w_in")
    theirs_in = _run_carried(exchange(["w_in"]), "rs_exchange_w_in")
    parts_in = presum(["w_in"], theirs_in)
    recv_in = _run_carried(_carry_chip_scatter(parts_in), "rs_scatter_w_in")
    halves_in = chip_sum(["w_in"], parts_in, recv_in)
    reduced.update(zip(mixer_w + ["w_in"], _run_carried(_carry_sibling_share(halves_mix + halves_in), "rs_share_rest")))

    dbbt_r = _groups_from_diag_blocks(dBm_r, G, H, P).transpose(1, 0, 2)
    dbbt_i = _groups_from_diag_blocks(dBm_i, G, H, P).transpose(1, 0, 2)
    dc_re = _groups_from_diag_blocks(dCm_r, G, P, H).transpose(0, 2, 1)
    dc_im = -_groups_from_diag_blocks(dCm_i, G, P, H).transpose(0, 2, 1)
    dmod = jnp.concatenate([dsh1, dsc1, dg1, dsh2, dsc2, dg2], axis=-1).reshape(Bl, 6 * D)
    small = [g_n1, g_n2, g_fg, g_dsk, gb_glu, gcw8[:3], dlam_r_p, dlam_i_p, dbbt_r, dbbt_i, dc_re, dc_im]
    small_shapes = [v.shape for v in small]
    n_small = sum(int(v.size) for v in small)
    gathered = _allgather8(_pack(small + [dmod]), "allgather_small")
    red = _unpack(_sum_devices(gathered, "sum_small"), small_shapes)
    (r_n1, r_n2, r_fg, r_dsk, r_bglu, r_cw, r_dlr, r_dli, r_dbr, r_dbi, r_cre, r_cim) = red
    dmod_all = gathered.reshape(N_DEV, -1)[:, n_small:n_small + Bl * 6 * D].reshape(N_DEV * Bl, 6 * D)
    gw_ada, gb_ada = _ada_bwd(c_all, lax.dynamic_slice_in_dim(dmod_all, q * Ca, Ca, axis=1), dmod_all)
    g_lr, g_li, g_ldt, g_bt_r, g_bt_i = _s5_disc_bwd(lam_re[0], lam_im[0], ldt_c, bt_r, bt_i,
                                                   r_dlr.reshape(G, P), r_dli.reshape(G, P), r_dbr, r_dbi)

    grads = dict(
        norm1_g=r_n1, norm2_g=r_n2, w_ada=gw_ada, b_ada=gb_ada, lam_re=g_lr, lam_im=g_li, log_dt=g_ldt.reshape(1, G),
        b_re=g_bt_r.transpose(1, 2, 0), b_im=g_bt_i.transpose(1, 2, 0), c_re=r_cre, c_im=r_cim, d_skip=r_dsk,
        b_glu=r_bglu, conv_w=lax.dynamic_slice_in_dim(r_cw, q * Dcs, Dcs, axis=1), final_g=r_fg, **reduced)
    weights = dict(norm1_g=norm1_g, norm2_g=norm2_g, w_ada=w_ada, b_ada=b_ada, w_in=w_in, lam_re=lam_re, lam_im=lam_im,
                   log_dt=log_dt, b_re=b_re, b_im=b_im, c_re=c_re, c_im=c_im, d_skip=d_skip, w_glu=w_glu, b_glu=b_glu,
                   conv_w=conv_w, w_proj_ssm=w_proj_ssm, w_proj_conv=w_proj_conv, w_out=w_out, w_ff1=w_ff1, w_ff2=w_ff2,
                   final_g=final_g)
    m_in = dict(norm1_g=m_norm1_g, norm2_g=m_norm2_g, w_ada=m_w_ada, b_ada=m_b_ada, w_in=m_w_in, lam_re=m_lam_re,
                lam_im=m_lam_im, log_dt=m_log_dt, b_re=m_b_re, b_im=m_b_im, c_re=m_c_re, c_im=m_c_im, d_skip=m_d_skip,
                w_glu=m_w_glu, b_glu=m_b_glu, conv_w=m_conv_w, w_proj_ssm=m_w_proj_ssm, w_proj_conv=m_w_proj_conv,
                w_out=m_w_out, w_ff1=m_w_ff1, w_ff2=m_w_ff2, final_g=m_final_g)
    v_in = dict(norm1_g=v_norm1_g, norm2_g=v_norm2_g, w_ada=v_w_ada, b_ada=v_b_ada, w_in=v_w_in, lam_re=v_lam_re,
                lam_im=v_lam_im, log_dt=v_log_dt, b_re=v_b_re, b_im=v_b_im, c_re=v_c_re, c_im=v_c_im, d_skip=v_d_skip,
                w_glu=v_w_glu, b_glu=v_b_glu, conv_w=v_conv_w, w_proj_ssm=v_w_proj_ssm, w_proj_conv=v_w_proj_conv,
                w_out=v_w_out, w_ff1=v_w_ff1, w_ff2=v_w_ff2, final_g=v_final_g)
    names = list(weights)
    grads = {k: grads[k].reshape(weights[k].shape) for k in names}

    big_upd = big_names + ["w_ada"]
    delta, new_m, new_v = {}, {}, {}
    for k in big_upd:
        shp = weights[k].shape
        two_d = lambda a: a.reshape(shp[-2], shp[-1])
        d_, m_, v_ = _adamw(two_d(weights[k]), two_d(grads[k]), two_d(m_in[k]), two_d(v_in[k]), "adamw_" + k)
        delta[k], new_m[k], new_v[k] = d_.reshape(shp), m_.reshape(shp), v_.reshape(shp)
    small_upd = [k for k in names if k not in big_upd]
    flat2 = lambda a: a.reshape(-1, a.shape[-1])
    d_, m_, v_ = _adamw_many([flat2(weights[k]) for k in small_upd], [flat2(grads[k]) for k in small_upd],
                             [flat2(m_in[k]) for k in small_upd], [flat2(v_in[k]) for k in small_upd])
    for k, dd, mm, vv in zip(small_upd, d_, m_, v_):
        shp = weights[k].shape
        delta[k], new_m[k], new_v[k] = dd.reshape(shp), mm.reshape(shp), vv.reshape(shp)

    loss = lax.psum(loss_p[0, 0], ("x", "y", "c"))
    grad_x = grad_x2.reshape(Bl, L, D)
    return (loss, grad_x, *[grads[k] for k in names], *[delta[k] for k in names],
            *[new_m[k] for k in names], *[new_v[k] for k in names])
```

```python
import functools
from typing import Callable, NamedTuple

import jax
import jax.numpy as jnp
from jax import lax
from jax.experimental import pallas as pl
from jax.experimental.pallas import tpu as pltpu

F32 = jnp.float32
BF16 = jnp.bfloat16
MESH = pl.DeviceIdType.MESH
N_CHIPS = 4
N_DEV = 8
LANES = 128
SUBLANES = 8
V7X_VMEM_BYTES = 64 * 1024 * 1024
VMEM_LIMIT = V7X_VMEM_BYTES - 6 * 1024 * 1024
SSM_GROUP = 16
SSM_STATE = 64
S5_ROW_PAD = 4
HALO_ROWS = 16
RMS_EPS = 1e-6
ADAM_LR, ADAM_B1, ADAM_B2, ADAM_EPS, ADAM_WD, ADAM_STEP = 0.001, 0.9, 0.999, 1e-08, 0.01, 10

ANY = pl.BlockSpec(memory_space=pl.ANY)
VMEM_SPEC = pl.BlockSpec(memory_space=pltpu.VMEM)


def _cparams(**kw):
    return pltpu.CompilerParams(vmem_limit_bytes=VMEM_LIMIT, **kw)


def _dot(a, b):
    return jnp.dot(a, b, preferred_element_type=F32)


def _dot_nt(a, b):
    return lax.dot_general(a, b, (((1,), (1,)), ((), ())), preferred_element_type=F32)


def _dot_tn(a, b):
    return lax.dot_general(a, b, (((0,), (0,)), ((), ())), preferred_element_type=F32)


def _mesh_pos():
    return lax.axis_index("x"), lax.axis_index("y"), lax.axis_index("c")


def _allgather8(v, name):
    r, c = v.shape

    def body(x_ref, out_ref, send_sems, recv_sems, local_sem):
        x, y, cc = _mesh_pos()
        me, sibling = (x, y, cc), (x, y, 1 - cc)
        chips = [(1 - x, y), (x, 1 - y), (1 - x, 1 - y)]

        def slot(px, py, pc):
            return out_ref.at[4 * px + 2 * py + pc]

        def copy(k, block, to, src=None):
            return pltpu.make_async_remote_copy(
                src_ref=slot(*block) if src is None else src, dst_ref=slot(*block),
                send_sem=send_sems.at[k], recv_sem=recv_sems.at[k], device_id=to, device_id_type=MESH)

        mine = pltpu.make_async_copy(x_ref, slot(*me), local_sem)
        mine.start()
        first = [copy(0, me, sibling, src=x_ref)]
        first += [copy(1 + j, me, (*chip, cc), src=x_ref) for j, chip in enumerate(chips)]
        for cp in first:
            cp.start()
        passed = [copy(4 + j, (*chip, cc), sibling) for j, chip in enumerate(chips)]
        for j, chip in enumerate(chips):
            copy(1 + j, (*chip, cc), me).wait_recv()
            passed[j].start()
        copy(0, sibling, me).wait_recv()
        for j, chip in enumerate(chips):
            copy(4 + j, (*chip, 1 - cc), me).wait_recv()
        for cp in first + passed:
            cp.wait_send()
        mine.wait()

    return pl.pallas_call(
        body, out_shape=jax.ShapeDtypeStruct((N_DEV, r, c), v.dtype),
        in_specs=[VMEM_SPEC], out_specs=VMEM_SPEC,
        scratch_shapes=[pltpu.SemaphoreType.DMA((7,)), pltpu.SemaphoreType.DMA((7,)), pltpu.SemaphoreType.DMA],
        name=name)(v)


def _shard_region(ref, axis, shard_shape, q, half):
    R, C = shard_shape
    r0, nr = (0, R) if half is None else (half * (R // 2), R // 2)
    if axis == 1:
        return ref.at[pl.ds(r0, nr), pl.ds(q * C, C)]
    return ref.at[pl.ds(q * R + r0, nr), :]


class _Carried(NamedTuple):
    inputs: tuple
    out_shapes: tuple
    aliases: dict
    sems: tuple
    steps: Callable


def _carry_join(a, b):
    na_i, na_o, na_s = len(a.inputs), len(a.out_shapes), len(a.sems)

    def steps(ins, outs, sems):
        sa, fa = a.steps(ins[:na_i], outs[:na_o], sems[:na_s])
        sb, fb = b.steps(ins[na_i:], outs[na_o:], sems[na_s:])


        def start():
            sa()
            sb()

        def finish():
            fa()
            fb()

        return start, finish

    aliases = dict(a.aliases)
    aliases.update({na_i + i: na_o + o for i, o in b.aliases.items()})
    return _Carried(a.inputs + b.inputs, a.out_shapes + b.out_shapes, aliases, a.sems + b.sems, steps)


def _call_carrying(body, carried, args, *, out_shape, in_specs, out_specs, scratch_shapes=(), grid=None, aliases=None,
                   name, **kw):
    n_in, n_out, n_sc = len(in_specs), len(out_specs), len(scratch_shapes)
    n_ci, n_co = len(carried.inputs), len(carried.out_shapes)

    def wrapped(*refs):
        ins, refs = refs[:n_in], refs[n_in:]
        c_ins, refs = refs[:n_ci], refs[n_ci:]
        outs, refs = refs[:n_out], refs[n_out:]
        c_outs, refs = refs[:n_co], refs[n_co:]
        scratch, c_sems = refs[:n_sc], refs[n_sc:]
        start, finish = carried.steps(c_ins, c_outs, c_sems)
        if grid is None:
            start()
            body(*ins, *outs, *scratch)
            finish()
        else:
            ids = [pl.program_id(d) for d in range(len(grid))]
            first = functools.reduce(jnp.logical_and, [i == 0 for i in ids])
            last = functools.reduce(jnp.logical_and, [i == g - 1 for i, g in zip(ids, grid)])
            pl.when(first)(start)
            body(*ins, *outs, *scratch)
            pl.when(last)(finish)

    if grid is not None:
        kw["grid"] = grid
    io_aliases = dict(aliases or {})
    io_aliases.update({n_in + i: n_out + o for i, o in carried.aliases.items()})
    res = pl.pallas_call(
        wrapped, out_shape=list(out_shape) + list(carried.out_shapes),
        in_specs=list(in_specs) + [ANY] * n_ci, out_specs=list(out_specs) + [ANY] * n_co,
        scratch_shapes=list(scratch_shapes) + list(carried.sems),
        input_output_aliases=io_aliases, name=name, **kw,
    )(*args, *carried.inputs)
    return res[:n_out], res[n_out:]


def _run_carried(carried, name):
    return _call_carrying(lambda: None, carried, (), out_shape=(), in_specs=(), out_specs=(), name=name)[1]


def _carry_allgather(fulls, axes, shapes):
    n = len(fulls)
    return _Carried(tuple(fulls), tuple(jax.ShapeDtypeStruct(f.shape, f.dtype) for f in fulls),
                    {i: i for i in range(n)}, (pltpu.SemaphoreType.DMA((3 * n,)),) * 4,
                    lambda ins, outs, sems: _allgather_weights_steps(outs, axes, shapes, *sems))


def _allgather_weights_steps(outs, axes, shapes, send_s, recv_s, fsend_s, frecv_s):
    n = len(outs)
    x, y, c = _mesh_pos()
    q = 2 * x + y
    sibling = (x, y, 1 - c)
    chips = [(1 - x, y), (x, 1 - y), (1 - x, 1 - y)]

    def region(i, qq, half):
        return _shard_region(outs[i], axes[i], shapes[i], qq, half)

    def remote(src, dst, ss, rs, to):
        return pltpu.make_async_remote_copy(src_ref=src, dst_ref=dst, send_sem=ss, recv_sem=rs,
                                            device_id=to, device_id_type=MESH)

    def ici(i, j, qq):
        cx, cy = chips[j]
        reg = region(i, qq, c)
        return remote(reg, reg, send_s.at[3 * i + j], recv_s.at[3 * i + j], (cx, cy, c))

    def d2d(i, j, half):
        cx, cy = chips[j]
        reg = region(i, 2 * cx + cy, half)
        return remote(reg, reg, fsend_s.at[3 * i + j], frecv_s.at[3 * i + j], sibling)

    def start():
        for i in range(n):
            for j in range(3):
                ici(i, j, q).start()

    def finish():
        for i in range(n):
            for j, (cx, cy) in enumerate(chips):
                ici(i, j, 2 * cx + cy).wait_recv()
                d2d(i, j, c).start()
        for i in range(n):
            for j in range(3):
                d2d(i, j, 1 - c).wait_recv()
        for i in range(n):
            for j in range(3):
                ici(i, j, q).wait_send()
                d2d(i, j, c).wait_send()

    return start, finish


def _carry_sibling_exchange(grads, axes, shapes):
    n = len(grads)

    def steps(ins, theirs, sems):
        send_s, recv_s = sems
        x, y, c = _mesh_pos()

        def copies():
            return [pltpu.make_async_remote_copy(
                src_ref=_shard_region(ins[i], axes[i], shapes[i], qq, 1 - c), dst_ref=theirs[i].at[qq],
                send_sem=send_s.at[N_CHIPS * i + qq], recv_sem=recv_s.at[N_CHIPS * i + qq],
                device_id=(x, y, 1 - c), device_id_type=MESH) for i in range(n) for qq in range(N_CHIPS)]

        def start():
            for cp in copies():
                cp.start()

        def finish():
            for cp in copies():
                cp.wait()

        return start, finish

    stacked = tuple(jax.ShapeDtypeStruct((N_CHIPS, R // 2, C), F32) for (R, C) in shapes)
    return _Carried(tuple(grads), stacked, {}, (pltpu.SemaphoreType.DMA((N_CHIPS * n,)),) * 2, steps)


def _carry_chip_scatter(parts):
    n = len(parts)

    def steps(ins, outs, sems):
        send_s, recv_s = sems
        x, y, c = _mesh_pos()
        chips = [(1 - x, y), (x, 1 - y), (1 - x, 1 - y)]

        def copies():
            return [pltpu.make_async_remote_copy(
                src_ref=ins[i].at[2 * cx + cy], dst_ref=outs[i].at[j],
                send_sem=send_s.at[3 * i + j], recv_sem=recv_s.at[3 * i + j],
                device_id=(cx, cy, c), device_id_type=MESH) for i in range(n) for j, (cx, cy) in enumerate(chips)]

        def start():
            for cp in copies():
                cp.start()

        def finish():
            for cp in copies():
                cp.wait()

        return start, finish

    return _Carried(tuple(parts), tuple(jax.ShapeDtypeStruct((3,) + p.shape[1:], p.dtype) for p in parts), {},
                    (pltpu.SemaphoreType.DMA((3 * n,)),) * 2, steps)


def _carry_sibling_share(fulls):
    n = len(fulls)

    def steps(ins, outs, sems):
        send_s, recv_s = sems
        x, y, c = _mesh_pos()

        def copy(i, half):
            rh = fulls[i].shape[0] // 2
            rows = outs[i].at[pl.ds(half * rh, rh), :]
            return pltpu.make_async_remote_copy(src_ref=rows, dst_ref=rows, send_sem=send_s.at[i], recv_sem=recv_s.at[i],
                                                device_id=(x, y, 1 - c), device_id_type=MESH)

        def start():
            for i in range(n):
                copy(i, c).start()

        def finish():
            for i in range(n):
                copy(i, 1 - c).wait_recv()
                copy(i, c).wait_send()

        return start, finish

    return _Carried(tuple(fulls), tuple(jax.ShapeDtypeStruct(f.shape, f.dtype) for f in fulls),
                    {i: i for i in range(n)}, (pltpu.SemaphoreType.DMA((n,)),) * 2, steps)


def _row_block(rows, target=256):
    return target if rows % target == 0 else rows


def _cast_into_full(w, axis, pos, name):
    R, C = w.shape
    rb = _row_block(R)
    nrb = R // rb
    if axis == 1:
        shape, omap = (R, N_CHIPS * C), lambda i, s: (i, s[0])
    else:
        shape, omap = (N_CHIPS * R, C), lambda i, s: (s[0] * nrb + i, 0)

    def body(s_ref, w_ref, o_ref):
        o_ref[...] = w_ref[...].astype(BF16)

    return pl.pallas_call(
        body, out_shape=jax.ShapeDtypeStruct(shape, BF16),
        grid_spec=pltpu.PrefetchScalarGridSpec(
            num_scalar_prefetch=1, grid=(nrb,), in_specs=[pl.BlockSpec((rb, C), lambda i, s: (i, 0))],
            out_specs=pl.BlockSpec((rb, C), omap)),
        name=name)(pos, w)


def _presum(g, theirs, axis, shape, pos, name):
    R, C = shape
    Rh = R // 2
    rb = _row_block(Rh)
    nr = Rh // rb
    if axis == 1:
        gmap = lambda k, i, s: (s[1] * nr + i, k)
    else:
        gmap = lambda k, i, s: (k * (R // rb) + s[1] * nr + i, 0)

    def body(s_ref, g_ref, t_ref, o_ref):
        o_ref[...] = (g_ref[...] + t_ref[...]).astype(BF16)

    spec = pl.BlockSpec((None, rb, C), lambda k, i, s: (k, i, 0))
    return pl.pallas_call(
        body, out_shape=jax.ShapeDtypeStruct((N_CHIPS, Rh, C), BF16),
        grid_spec=pltpu.PrefetchScalarGridSpec(
            num_scalar_prefetch=1, grid=(N_CHIPS, nr), in_specs=[pl.BlockSpec((rb, C), gmap), spec], out_specs=spec),
        name=name)(pos, g, theirs)


def _sum_chips(own, recv, pos, name):
    _, Rh, C = own.shape
    rb = _row_block(Rh)
    nr = Rh // rb

    def body(s_ref, o_ref, r_ref, out_ref):
        acc = o_ref[...].astype(F32)
        for k in range(3):
            acc = acc + r_ref[k].astype(F32)
        out_ref[...] = acc

    return pl.pallas_call(
        body, out_shape=jax.ShapeDtypeStruct((2 * Rh, C), F32),
        grid_spec=pltpu.PrefetchScalarGridSpec(
            num_scalar_prefetch=1, grid=(nr,),
            in_specs=[pl.BlockSpec((None, rb, C), lambda i, s: (s[0], i, 0)), pl.BlockSpec((3, rb, C), lambda i, s: (0, i, 0))],
            out_specs=pl.BlockSpec((rb, C), lambda i, s: (s[1] * nr + i, 0))),
        name=name)(pos, own, recv)


def _sum_devices(parts, name):
    K, R, C = parts.shape

    def body(p_ref, o_ref):
        acc = p_ref[0]
        for k in range(1, K):
            acc = acc + p_ref[k]
        o_ref[...] = acc

    return pl.pallas_call(body, out_shape=jax.ShapeDtypeStruct((R, C), F32), name=name)(parts)


def _adamw_math(w, g, m, v):
    nm = ADAM_B1 * m + (1.0 - ADAM_B1) * g
    nv = ADAM_B2 * v + (1.0 - ADAM_B2) * (g * g)
    m_hat = nm / (1.0 - ADAM_B1 ** ADAM_STEP)
    v_hat = nv / (1.0 - ADAM_B2 ** ADAM_STEP)
    return -ADAM_LR * (m_hat / (jnp.sqrt(v_hat) + ADAM_EPS) + ADAM_WD * w), nm, nv


def _adamw_many(ws, gs, ms, vs):
    n = len(ws)

    def body(*refs):
        for k in range(n):
            w, g, m, v = (refs[j * n + k][...] for j in range(4))
            for j, val in enumerate(_adamw_math(w, g, m, v)):
                refs[(4 + j) * n + k][...] = val

    shapes = [jax.ShapeDtypeStruct(w.shape, F32) for w in ws]
    res = pl.pallas_call(body, out_shape=shapes * 3, compiler_params=_cparams(), name="adamw_small")(*ws, *gs, *ms, *vs)
    return res[:n], res[n:2 * n], res[2 * n:]


def _adamw(w, g, m, v, name):
    R, C = w.shape
    rb = _row_block(R)

    def body(w_ref, g_ref, m_ref, v_ref, d_ref, nm_ref, nv_ref):
        gv = g_ref[...]
        nm = ADAM_B1 * m_ref[...] + (1.0 - ADAM_B1) * gv
        nv = ADAM_B2 * v_ref[...] + (1.0 - ADAM_B2) * (gv * gv)
        m_hat = nm / (1.0 - ADAM_B1 ** ADAM_STEP)
        v_hat = nv / (1.0 - ADAM_B2 ** ADAM_STEP)
        d_ref[...] = -ADAM_LR * (m_hat / (jnp.sqrt(v_hat) + ADAM_EPS) + ADAM_WD * w_ref[...])
        nm_ref[...] = nm
        nv_ref[...] = nv

    spec = pl.BlockSpec((rb, C), lambda i: (i, 0))
    return pl.pallas_call(body, out_shape=[jax.ShapeDtypeStruct((R, C), F32)] * 3, grid=(R // rb,),
                          in_specs=[spec] * 4, out_specs=[spec] * 3, name=name)(w, g, m, v)


def _silu(v):
    return v * jax.nn.sigmoid(v)


def _ada_fwd(c_all, w_sh, b_sh):
    S, D = c_all.shape
    Ca = w_sh.shape[1]
    cb = 512 if Ca % 512 == 0 else Ca

    def body(c_ref, w_ref, b_ref, o_ref):
        act = _silu(c_ref[...]).astype(BF16)
        o_ref[...] = _dot(act, w_ref[...].astype(BF16)) + b_ref[...]

    return pl.pallas_call(
        body, out_shape=jax.ShapeDtypeStruct((S, Ca), F32), grid=(Ca // cb,),
        in_specs=[pl.BlockSpec((S, D), lambda j: (0, 0)), pl.BlockSpec((D, cb), lambda j: (0, j)),
                  pl.BlockSpec((1, cb), lambda j: (0, j))],
        out_specs=pl.BlockSpec((S, cb), lambda j: (0, j)), name="ada_fwd")(c_all, w_sh, b_sh)


def _ada_bwd(c_all, dmod_sh, dmod_all):
    S, D = c_all.shape
    Ca = dmod_sh.shape[1]
    C6 = dmod_all.shape[1]

    def body(c_ref, ds_ref, da_ref, gw_ref, gb_ref):
        act = _silu(c_ref[...]).astype(BF16)
        gw_ref[...] = _dot_tn(act, ds_ref[...].astype(BF16))
        gb_ref[...] = jnp.sum(da_ref[...], axis=0, keepdims=True)

    return pl.pallas_call(
        body, out_shape=[jax.ShapeDtypeStruct((D, Ca), F32), jax.ShapeDtypeStruct((1, C6), F32)],
        compiler_params=_cparams(), name="ada_bwd")(c_all, dmod_sh, dmod_all)


def _rms_fwd(xv):
    r = lax.rsqrt(jnp.mean(xv * xv, axis=-1, keepdims=True) + RMS_EPS)
    return xv * r, r


def _rms_bwd(dxh, xh, r):
    return r * (dxh - xh * jnp.mean(dxh * xh, axis=-1, keepdims=True))


def _const_spec(shape):
    nd = len(shape)
    return pl.BlockSpec(shape, lambda *_: (0,) * nd)


def _seq_spec(D, bps, rev_blocks=None):
    if rev_blocks is None:
        return pl.BlockSpec((None, 1, D), lambda i: (i // bps, 0, 0))
    return pl.BlockSpec((None, 1, D), lambda i: ((rev_blocks - 1 - i) // bps, 0, 0))


def _inproj_fwd(x2, sh1, sc1, n1g, w_in, carried):
    N, D = x2.shape
    IN = w_in.shape[1]
    Bl = sh1.shape[0]
    TB = _row_block(N // Bl)
    bps = (N // Bl) // TB

    def body(x_ref, sh_ref, sc_ref, g_ref, w_ref, p_ref):
        xh, _ = _rms_fwd(x_ref[...])
        h = (xh * g_ref[...]) * (1.0 + sc_ref[...]) + sh_ref[...]
        p_ref[...] = _dot(h.astype(BF16), w_ref[...]).astype(BF16)

    (p2,), extra = _call_carrying(
        body, carried, (x2, sh1, sc1, n1g, w_in), out_shape=[jax.ShapeDtypeStruct((N, IN), BF16)], grid=(N // TB,),
        in_specs=[pl.BlockSpec((TB, D), lambda i: (i, 0)), _seq_spec(D, bps), _seq_spec(D, bps),
                  _const_spec((1, D)), _const_spec((D, IN))],
        out_specs=[pl.BlockSpec((TB, IN), lambda i: (i, 0))],
        compiler_params=_cparams(dimension_semantics=("arbitrary",)), name="inproj_fwd")
    return p2, extra


def _s5_dims(Bl, L, Ds):
    G = Ds // SSM_GROUP
    GP = G * SSM_STATE
    NP = GP // LANES
    T = min(64, L // 2)
    nb = 2 if (Ds // 2) % LANES == 0 else 1
    return G, GP, NP, T, nb


def _s5_disc_math(lr, li, ldt, bt_r, bt_i):
    dt = jnp.exp(ldt)
    er = jnp.exp(lr * dt)
    lbr = er * jnp.cos(li * dt)
    lbi = er * jnp.sin(li * dt)
    den = lr * lr + li * li
    fr = ((lbr - 1.0) * lr + lbi * li) / den
    fi = (lbi * lr - (lbr - 1.0) * li) / den
    return lbr, lbi, fr[None] * bt_r - fi[None] * bt_i, fr[None] * bt_i + fi[None] * bt_r


def _s5_disc(lr, li, ldt, bt_r, bt_i):
    def body(lr_ref, li_ref, ldt_ref, br_ref, bi_ref, o0, o1, o2, o3):
        res = _s5_disc_math(lr_ref[...], li_ref[...], ldt_ref[...], br_ref[...], bi_ref[...])
        for o, v in zip((o0, o1, o2, o3), res):
            o[...] = v

    S = jax.ShapeDtypeStruct
    return pl.pallas_call(body, out_shape=[S(lr.shape, F32)] * 2 + [S(bt_r.shape, F32)] * 2, name="s5_disc")(lr, li, ldt, bt_r, bt_i)


def _s5_disc_bwd(lr, li, ldt, bt_r, bt_i, dlbr, dlbi, dbbr, dbbi):
    def body(lr_ref, li_ref, ldt_ref, br_ref, bi_ref, g0, g1, g2, g3, o0, o1, o2, o3, o4):
        _, vjp = jax.vjp(_s5_disc_math, lr_ref[...], li_ref[...], ldt_ref[...], br_ref[...], bi_ref[...])
        res = vjp((g0[...], g1[...], g2[...], g3[...]))
        for o, v in zip((o0, o1, o2, o3, o4), res):
            o[...] = v

    S = jax.ShapeDtypeStruct
    return pl.pallas_call(body, out_shape=[S(lr.shape, F32)] * 2 + [S(ldt.shape, F32)] + [S(bt_r.shape, F32)] * 2,
                          name="s5_disc_bwd")(lr, li, ldt, bt_r, bt_i, dlbr, dlbi, dbbr, dbbi)


S5_SCAN_PANELS = 4
S5_SCAN_STEPS = 4


def _panel_scan(src_r, src_i, dst_r, dst_i, lr_ref, li_ref, car_r, car_i, NP, Bl, T, TP, adjoint):
    BT = Bl * TP
    PG = min(S5_SCAN_PANELS, NP)
    CH = S5_SCAN_STEPS
    for k0 in range(0, NP, PG):
        ks = list(range(k0, k0 + PG))
        lr = [jnp.broadcast_to(lr_ref[pl.ds(k, 1), :], (Bl, LANES)) for k in ks]
        li = [jnp.broadcast_to(li_ref[pl.ds(k, 1), :], (Bl, LANES)) for k in ks]

        def trip(cc, carry):
            ts = [(T - 1 - (cc * CH + s)) if adjoint else (cc * CH + s) for s in range(CH)]
            idx = [[pl.ds(k * BT + t, Bl, stride=TP) for t in ts] for k in ks]
            loaded = [[(src_r[ix, :], src_i[ix, :]) for ix in idx[j]] for j in range(PG)]
            results, new_carry = [], []
            for j in range(PG):
                ar, ai = carry[j]
                res = []
                for s in range(CH):
                    br, bi = loaded[j][s]
                    if adjoint:
                        ar, ai = br + lr[j] * ar + li[j] * ai, bi + lr[j] * ai - li[j] * ar
                    else:
                        ar, ai = lr[j] * ar - li[j] * ai + br, lr[j] * ai + li[j] * ar + bi
                    res.append((ar, ai))
                results.append(res)
                new_carry.append((ar, ai))
            for j in range(PG):
                for s in range(CH):
                    dst_r[idx[j][s], :] = results[j][s][0]
                    dst_i[idx[j][s], :] = results[j][s][1]
            return tuple(new_carry)

        init = tuple((car_r[pl.ds(k * SUBLANES, Bl), :], car_i[pl.ds(k * SUBLANES, Bl), :]) for k in ks)
        fin = lax.fori_loop(0, T // CH, trip, init)
        for j, k in enumerate(ks):
            car_r[pl.ds(k * SUBLANES, Bl), :] = fin[j][0]
            car_i[pl.ds(k * SUBLANES, Bl), :] = fin[j][1]


def _s5_fwd(p3, Bm_r, Bm_i, Cm_r, Cm_i, lam_r, lam_i, dsk, carried):
    Bl, L, _ = p3.shape
    Ds = dsk.shape[1]
    G, GP, NP, T, nb = _s5_dims(Bl, L, Ds)
    nT = L // T
    TP = T + S5_ROW_PAD
    BT = Bl * TP
    dsb, gpb, npb = Ds // nb, GP // nb, NP // nb

    def body(u_ref, br_ref, bi_ref, cr_ref, ci_ref, lr_ref, li_ref, dsk_ref, sr_ref, si_ref, sb_r, sb_i, y_ref,
             car_r, car_i, upad, ypad, bu_r, bu_i):
        i = pl.program_id(0)

        @pl.when(i == 0)
        def _():
            car_r[...] = jnp.zeros_like(car_r)
            car_i[...] = jnp.zeros_like(car_i)
            upad[...] = jnp.zeros_like(upad)

        zpad = jnp.zeros((S5_ROW_PAD, LANES), F32)
        for k in range(NP):
            for b in range(Bl):
                sr_ref[pl.ds(k * BT + b * TP + T, S5_ROW_PAD), :] = zpad
                si_ref[pl.ds(k * BT + b * TP + T, S5_ROW_PAD), :] = zpad
        for b in range(Bl):
            upad[pl.ds(b * TP, T), :] = u_ref[b].astype(F32)
        u = upad[...]
        ub = u.astype(BF16)
        for blk in range(nb):
            ubb = ub[:, blk * dsb:(blk + 1) * dsb]
            for bu_ref, b_ref in ((bu_r, br_ref), (bu_i, bi_ref)):
                res = _dot(ubb, b_ref[blk])
                for kk in range(npb):
                    k = blk * npb + kk
                    bu_ref[pl.ds(k * BT, BT), :] = res[:, kk * LANES:(kk + 1) * LANES]

        _panel_scan(bu_r, bu_i, sr_ref, si_ref, lr_ref, li_ref, car_r, car_i, NP, Bl, T, TP, adjoint=False)
        sb_r[...] = car_r[...]
        sb_i[...] = car_i[...]

        for blk in range(nb):
            s_r = jnp.concatenate([sr_ref[pl.ds((blk * npb + kk) * BT, BT), :] for kk in range(npb)], axis=1).astype(BF16)
            s_i = jnp.concatenate([si_ref[pl.ds((blk * npb + kk) * BT, BT), :] for kk in range(npb)], axis=1).astype(BF16)
            cols = slice(blk * dsb, (blk + 1) * dsb)
            ypad[:, cols] = _dot(s_r, cr_ref[blk]) + _dot(s_i, ci_ref[blk]) + dsk_ref[:, cols] * u[:, cols]
        for b in range(Bl):
            y_ref[b] = ypad[pl.ds(b * TP, T), :]

    S = jax.ShapeDtypeStruct
    state = S((nT, NP * BT, LANES), F32)
    bound = S((nT, NP * SUBLANES, LANES), F32)
    sspec = pl.BlockSpec((None, NP * BT, LANES), lambda i: (i, 0, 0))
    bspec = pl.BlockSpec((None, NP * SUBLANES, LANES), lambda i: (i, 0, 0))
    return _call_carrying(
        body, carried, (p3, Bm_r, Bm_i, Cm_r, Cm_i, lam_r, lam_i, dsk),
        out_shape=[state, state, bound, bound, S((Bl, L, Ds), F32)], grid=(nT,),
        in_specs=[pl.BlockSpec((Bl, T, Ds), lambda i: (0, i, 0)),
                  _const_spec((nb, dsb, gpb)), _const_spec((nb, dsb, gpb)),
                  _const_spec((nb, gpb, dsb)), _const_spec((nb, gpb, dsb)),
                  _const_spec((NP, LANES)), _const_spec((NP, LANES)), _const_spec((1, Ds))],
        out_specs=[sspec, sspec, bspec, bspec, pl.BlockSpec((Bl, T, Ds), lambda i: (0, i, 0))],
        scratch_shapes=[pltpu.VMEM((NP * SUBLANES, LANES), F32)] * 2 + [pltpu.VMEM((BT, Ds), F32)] * 2
        + [pltpu.VMEM((NP * BT, LANES), F32)] * 2,
        compiler_params=_cparams(dimension_semantics=("arbitrary",)), name="s5_fwd")


def _s5_bwd(dy3, p3, dp3, Sr, Si, Sb_r, Sb_i, Bm_r, Bm_i, Cm_r, Cm_i, lam_r, lam_i, dsk, carried):
    Bl, L, Ds = dy3.shape
    G, GP, NP, T, nb = _s5_dims(Bl, L, Ds)
    nT = L // T
    TP = T + S5_ROW_PAD
    BT = Bl * TP
    dsb, gpb, npb = Ds // nb, GP // nb, NP // nb

    def body(dy_ref, u_ref, dp_ref, sr_ref, si_ref, sbr_ref, sbi_ref, br_ref, bi_ref, cr_ref, ci_ref, lr_ref, li_ref, dsk_ref,
             du_ref, dbr_ref, dbi_ref, dcr_ref, dci_ref, dlr_ref, dli_ref, ddsk_ref,
             a_r, a_i, car_r, car_i, acc_r, acc_i, dypad, upad, dupad):
        i = pl.program_id(0)

        @pl.when(i == 0)
        def _():
            for ref in (car_r, car_i, acc_r, acc_i, dbr_ref, dbi_ref, dcr_ref, dci_ref, ddsk_ref, dypad, upad):
                ref[...] = jnp.zeros_like(ref)

        for b in range(Bl):
            dypad[pl.ds(b * TP, T), :] = dy_ref[b]
            upad[pl.ds(b * TP, T), :] = u_ref[b].astype(F32)
        dy = dypad[...]
        dyb = dy.astype(BF16)
        u = upad[...]
        ub = u.astype(BF16)
        for blk in range(nb):
            dyb_b = dyb[:, blk * dsb:(blk + 1) * dsb]
            for q_ref, c_ref in ((a_r, cr_ref), (a_i, ci_ref)):
                res = _dot_nt(dyb_b, c_ref[blk])
                for kk in range(npb):
                    q_ref[pl.ds((blk * npb + kk) * BT, BT), :] = res[:, kk * LANES:(kk + 1) * LANES]

        _panel_scan(a_r, a_i, a_r, a_i, lr_ref, li_ref, car_r, car_i, NP, Bl, T, TP, adjoint=True)

        first_block = (i == nT - 1)
        for k in range(NP):
            rows = pl.ds(k * BT, BT)
            av_r, av_i = a_r[rows, :], a_i[rows, :]
            sp_r = pltpu.roll(sr_ref[rows, :], 1, 0)
            sp_i = pltpu.roll(si_ref[rows, :], 1, 0)
            acc = pl.ds(k * SUBLANES, SUBLANES)
            acc_r[acc, :] += jnp.sum((av_r * sp_r + av_i * sp_i).reshape(BT // SUBLANES, SUBLANES, LANES), axis=0)
            acc_i[acc, :] += jnp.sum((av_i * sp_r - av_r * sp_i).reshape(BT // SUBLANES, SUBLANES, LANES), axis=0)
            t0 = pl.ds(k * BT, Bl, stride=TP)
            a0_r, a0_i = a_r[t0, :], a_i[t0, :]
            brow = pl.ds(k * SUBLANES, Bl)
            sb_pr = jnp.where(first_block, 0.0, sbr_ref[brow, :])
            sb_pi = jnp.where(first_block, 0.0, sbi_ref[brow, :])
            acc_r[brow, :] += a0_r * sb_pr + a0_i * sb_pi
            acc_i[brow, :] += a0_i * sb_pr - a0_r * sb_pi

        ddsk_ref[...] += jnp.sum(dy * u, axis=0, keepdims=True)
        for blk in range(nb):
            cols = slice(blk * dsb, (blk + 1) * dsb)
            rows = [pl.ds((blk * npb + kk) * BT, BT) for kk in range(npb)]
            av_r = jnp.concatenate([a_r[r, :] for r in rows], axis=1).astype(BF16)
            av_i = jnp.concatenate([a_i[r, :] for r in rows], axis=1).astype(BF16)
            dupad[:, cols] = _dot_nt(av_r, br_ref[blk]) + _dot_nt(av_i, bi_ref[blk]) + dy[:, cols] * dsk_ref[:, cols]
            dbr_ref[blk] += _dot_tn(ub[:, cols], av_r)
            dbi_ref[blk] += _dot_tn(ub[:, cols], av_i)
            sv_r = jnp.concatenate([sr_ref[r, :] for r in rows], axis=1).astype(BF16)
            sv_i = jnp.concatenate([si_ref[r, :] for r in rows], axis=1).astype(BF16)
            dcr_ref[blk] += _dot_tn(sv_r, dyb[:, cols])
            dci_ref[blk] += _dot_tn(sv_i, dyb[:, cols])
        for b in range(Bl):
            du_ref[b] = dupad[pl.ds(b * TP, T), :].astype(BF16)

        @pl.when(i == nT - 1)
        def _():
            for k in range(NP):
                dlr_ref[pl.ds(k, 1), :] = jnp.sum(acc_r[pl.ds(k * SUBLANES, SUBLANES), :], axis=0, keepdims=True)
                dli_ref[pl.ds(k, 1), :] = jnp.sum(acc_i[pl.ds(k * SUBLANES, SUBLANES), :], axis=0, keepdims=True)

    S = jax.ShapeDtypeStruct
    rev = lambda i: nT - 1 - i
    sspec = pl.BlockSpec((None, NP * BT, LANES), lambda i: (rev(i), 0, 0))
    bspec = pl.BlockSpec((None, NP * SUBLANES, LANES), lambda i: (jnp.maximum(rev(i) - 1, 0), 0, 0))
    tspec = pl.BlockSpec((Bl, T, Ds), lambda i: (0, rev(i), 0))
    return _call_carrying(
        body, carried, (dy3, p3, dp3, Sr, Si, Sb_r, Sb_i, Bm_r, Bm_i, Cm_r, Cm_i, lam_r, lam_i, dsk),
        out_shape=[S(dp3.shape, dp3.dtype), S((nb, dsb, gpb), F32), S((nb, dsb, gpb), F32),
                   S((nb, gpb, dsb), F32), S((nb, gpb, dsb), F32), S((NP, LANES), F32), S((NP, LANES), F32), S((1, Ds), F32)],
        grid=(nT,),
        in_specs=[tspec, tspec, ANY, sspec, sspec, bspec, bspec,
                  _const_spec((nb, dsb, gpb)), _const_spec((nb, dsb, gpb)),
                  _const_spec((nb, gpb, dsb)), _const_spec((nb, gpb, dsb)),
                  _const_spec((NP, LANES)), _const_spec((NP, LANES)), _const_spec((1, Ds))],
        out_specs=[tspec, _const_spec((nb, dsb, gpb)), _const_spec((nb, dsb, gpb)),
                   _const_spec((nb, gpb, dsb)), _const_spec((nb, gpb, dsb)),
                   _const_spec((NP, LANES)), _const_spec((NP, LANES)), _const_spec((1, Ds))],
        aliases={2: 0},
        scratch_shapes=[pltpu.VMEM((NP * BT, LANES), F32)] * 2 + [pltpu.VMEM((NP * SUBLANES, LANES), F32)] * 4
        + [pltpu.VMEM((BT, Ds), F32)] * 3,
        compiler_params=_cparams(dimension_semantics=("arbitrary",)), name="s5_bwd")


def _mix_values(ylin, cb, cc, cx, gs, gc, halo_v, wglu, bglu, cw, wps, wpc, wout):
    yg, gelu_vjp = jax.vjp(jax.nn.gelu, ylin)
    sz = jax.nn.sigmoid(_dot(yg.astype(BF16), wglu) + bglu)
    ys = yg * sz
    v = cc * cx
    rows = lax.broadcasted_iota(jnp.int32, v.shape, 0)
    h6 = halo_v[HALO_ROWS - 2:HALO_ROWS - 1, :]
    h7 = halo_v[HALO_ROWS - 1:HALO_ROWS, :]
    v1 = jnp.where(rows == 0, h7, pltpu.roll(v, 1, 0))
    v2 = jnp.where(rows == 0, h6, jnp.where(rows == 1, h7, pltpu.roll(v, 2, 0)))
    cv = cw[0:1, :] * v2 + cw[1:2, :] * v1 + cw[2:3, :] * v
    yc = cb * cv
    ps = _dot(ys.astype(BF16), wps)
    pc = _dot(yc.astype(BF16), wpc)
    sgs = jax.nn.sigmoid(gs)
    sgc = jax.nn.sigmoid(gc)
    merged = sgs * ps + sgc * pc
    mo = _dot(merged.astype(BF16), wout)
    return dict(yg=yg, gelu_vjp=gelu_vjp, sz=sz, ys=ys, v=v, v1=v1, v2=v2, cv=cv, yc=yc, ps=ps, pc=pc,
                sgs=sgs, sgc=sgc, merged=merged, mo=mo)


def _mix_in_specs(TB, D, Ds, Dc, bps, blk):
    hb = TB // HALO_ROWS
    halo = lambda col: pl.BlockSpec((HALO_ROWS, Dc), lambda i: (jnp.maximum(blk(i) * hb - 1, 0), col))
    return [pl.BlockSpec((TB, Dc), lambda i: (blk(i), 1)), pl.BlockSpec((TB, Dc), lambda i: (blk(i), 2)),
            pl.BlockSpec((TB, Dc), lambda i: (blk(i), 3)), pl.BlockSpec((TB, D), lambda i: (blk(i), 2)),
            pl.BlockSpec((TB, D), lambda i: (blk(i), 3)), halo(2), halo(3),
            pl.BlockSpec((TB, Ds), lambda i: (blk(i), 0))]


def _mix_fwd(p2, ylin2, x2, g1, wglu, bglu, cw, wps, wpc, wout):
    N, D = x2.shape
    Ds = ylin2.shape[1]
    Dc = Ds
    Bl = g1.shape[0]
    TB = _row_block(N // Bl)
    bps = (N // Bl) // TB

    def body(cb_ref, cc_ref, cx_ref, gs_ref, gc_ref, hcc_ref, hcx_ref, yl_ref, x_ref, g1_ref,
             wglu_ref, bglu_ref, cw_ref, wps_ref, wpc_ref, wout_ref, x1_ref):
        i = pl.program_id(0)
        f32 = lambda ref: ref[...].astype(F32)
        halo_v = jnp.where(i % bps == 0, 0.0, f32(hcc_ref) * f32(hcx_ref))
        f = _mix_values(yl_ref[...], f32(cb_ref), f32(cc_ref), f32(cx_ref), f32(gs_ref), f32(gc_ref), halo_v,
                        wglu_ref[...], bglu_ref[...], cw_ref[...], wps_ref[...], wpc_ref[...], wout_ref[...])
        x1_ref[...] = x_ref[...] + g1_ref[...] * f["mo"]

    return pl.pallas_call(
        body, out_shape=jax.ShapeDtypeStruct((N, D), F32), grid=(N // TB,),
        in_specs=_mix_in_specs(TB, D, Ds, Dc, bps, lambda i: i) + [
            pl.BlockSpec((TB, D), lambda i: (i, 0)), _seq_spec(D, bps),
            _const_spec((Ds, Ds)), _const_spec((1, Ds)), _const_spec((SUBLANES, Dc)),
            _const_spec((Ds, D)), _const_spec((Dc, D)), _const_spec((D, D))],
        out_specs=pl.BlockSpec((TB, D), lambda i: (i, 0)),
        compiler_params=_cparams(), name="mix_fwd",
    )(p2, p2, p2, p2, p2, p2, p2, ylin2, x2, g1, wglu, bglu, cw, wps, wpc, wout)


def _mix_bwd(p2, ylin2, dx1, g1, wglu, bglu, cw, wps, wpc, wout, carried):
    N, D = dx1.shape
    Ds = ylin2.shape[1]
    Dc = Ds
    IN = p2.shape[1]
    Bl = g1.shape[0]
    TB = _row_block(N // Bl)
    bps = (N // Bl) // TB
    nblk = N // TB
    rev = lambda i: nblk - 1 - i

    def body(cb_ref, cc_ref, cx_ref, gs_ref, gc_ref, hcc_ref, hcx_ref, yl_ref, dx1_ref, g1_ref,
             wglu_ref, bglu_ref, cw_ref, wps_ref, wpc_ref, wout_ref,
             dyl_ref, dp_ref, gwout_ref, gwps_ref, gwpc_ref, gwglu_ref, gbglu_ref, gcw_ref, dg1_ref, nxt):
        i = pl.program_id(0)
        blk = rev(i)

        @pl.when(i == 0)
        def _():
            for ref in (gwout_ref, gwps_ref, gwpc_ref, gwglu_ref, gbglu_ref, gcw_ref):
                ref[...] = jnp.zeros_like(ref)

        @pl.when(i % bps == 0)
        def _():
            nxt[...] = jnp.zeros_like(nxt)
            dg1_ref[...] = jnp.zeros_like(dg1_ref)

        f32 = lambda ref: ref[...].astype(F32)
        cb, cc, cx = f32(cb_ref), f32(cc_ref), f32(cx_ref)
        halo_v = jnp.where(blk % bps == 0, 0.0, f32(hcc_ref) * f32(hcx_ref))
        wglu, wps, wpc, wout, cw = wglu_ref[...], wps_ref[...], wpc_ref[...], wout_ref[...], cw_ref[...]
        f = _mix_values(yl_ref[...], cb, cc, cx, f32(gs_ref), f32(gc_ref), halo_v, wglu, bglu_ref[...], cw, wps, wpc, wout)

        dx1v = dx1_ref[...]
        dg1_ref[...] += jnp.sum(dx1v * f["mo"], axis=0, keepdims=True)
        dmo = (g1_ref[...] * dx1v).astype(BF16)
        gwout_ref[...] += _dot_tn(f["merged"].astype(BF16), dmo)
        dmerged = _dot_nt(dmo, wout)
        dps = dmerged * f["sgs"]
        dpc = dmerged * f["sgc"]
        dgs = dmerged * f["ps"] * f["sgs"] * (1.0 - f["sgs"])
        dgc = dmerged * f["pc"] * f["sgc"] * (1.0 - f["sgc"])
        dpsb, dpcb = dps.astype(BF16), dpc.astype(BF16)
        gwps_ref[...] += _dot_tn(f["ys"].astype(BF16), dpsb)
        gwpc_ref[...] += _dot_tn(f["yc"].astype(BF16), dpcb)
        dys = _dot_nt(dpsb, wps)
        dyc = _dot_nt(dpcb, wpc)

        dcb = dyc * f["cv"]
        dcv = dyc * cb
        rows = lax.broadcasted_iota(jnp.int32, dcv.shape, 0)
        n0, n1 = nxt[0:1, :], nxt[1:2, :]
        d1 = jnp.where(rows == TB - 1, n0, pltpu.roll(dcv, TB - 1, 0))
        d2 = jnp.where(rows == TB - 2, n0, jnp.where(rows == TB - 1, n1, pltpu.roll(dcv, TB - 2, 0)))
        dv = cw[2:3, :] * dcv + cw[1:2, :] * d1 + cw[0:1, :] * d2
        nxt[0:2, :] = dcv[0:2, :]
        gcw_ref[0:1, :] += jnp.sum(dcv * f["v2"], axis=0, keepdims=True)
        gcw_ref[1:2, :] += jnp.sum(dcv * f["v1"], axis=0, keepdims=True)
        gcw_ref[2:3, :] += jnp.sum(dcv * f["v"], axis=0, keepdims=True)

        dz = dys * f["yg"] * f["sz"] * (1.0 - f["sz"])
        dzb = dz.astype(BF16)
        gwglu_ref[...] += _dot_tn(f["yg"].astype(BF16), dzb)
        gbglu_ref[...] += jnp.sum(dz, axis=0, keepdims=True)
        dyg = dys * f["sz"] + _dot_nt(dzb, wglu)
        dyl_ref[...] = f["gelu_vjp"](dyg)[0]

        dp_ref[:, Ds:Ds + Dc] = dcb.astype(BF16)
        dp_ref[:, Ds + Dc:Ds + 2 * Dc] = (dv * cx).astype(BF16)
        dp_ref[:, Ds + 2 * Dc:Ds + 3 * Dc] = (dv * cc).astype(BF16)
        dp_ref[:, Ds + 3 * Dc:Ds + 3 * Dc + D] = dgs.astype(BF16)
        dp_ref[:, Ds + 3 * Dc + D:IN] = dgc.astype(BF16)

    S = jax.ShapeDtypeStruct
    return _call_carrying(
        body, carried, (p2, p2, p2, p2, p2, p2, p2, ylin2, dx1, g1, wglu, bglu, cw, wps, wpc, wout),
        out_shape=[S((N, Ds), F32), S((N, IN), BF16), S((D, D), F32), S((Ds, D), F32), S((Dc, D), F32),
                   S((Ds, Ds), F32), S((1, Ds), F32), S((SUBLANES, Dc), F32), S((Bl, 1, D), F32)],
        grid=(nblk,),
        in_specs=_mix_in_specs(TB, D, Ds, Dc, bps, rev) + [
            pl.BlockSpec((TB, D), lambda i: (rev(i), 0)), _seq_spec(D, bps, nblk),
            _const_spec((Ds, Ds)), _const_spec((1, Ds)), _const_spec((SUBLANES, Dc)),
            _const_spec((Ds, D)), _const_spec((Dc, D)), _const_spec((D, D))],
        out_specs=[pl.BlockSpec((TB, Ds), lambda i: (rev(i), 0)), pl.BlockSpec((TB, IN), lambda i: (rev(i), 0)),
                   _const_spec((D, D)), _const_spec((Ds, D)), _const_spec((Dc, D)), _const_spec((Ds, Ds)),
                   _const_spec((1, Ds)), _const_spec((SUBLANES, Dc)), _seq_spec(D, bps, nblk)],
        scratch_shapes=[pltpu.VMEM((SUBLANES, Dc), F32)],
        compiler_params=_cparams(dimension_semantics=("arbitrary",)), name="mix_bwd")


def _mlp_fwd_bwd(x1, tgt, sh2, sc2, g2, n2g, fg, w1, w2):
    N, D = x1.shape
    Dff = w1.shape[1]
    Bl = sh2.shape[0]
    TB = _row_block(N // Bl)
    bps = (N // Bl) // TB

    def body(x1_ref, t_ref, sh_ref, sc_ref, g2_ref, n2_ref, fg_ref, w1_ref, w2_ref,
             dx1_ref, h2_ref, da_ref, sq_ref, df_ref, loss_ref, gfg_ref, gn2_ref, dsh_ref, dsc_ref, dg2_ref):
        i = pl.program_id(0)

        @pl.when(i == 0)
        def _():
            for ref in (loss_ref, gfg_ref, gn2_ref):
                ref[...] = jnp.zeros_like(ref)

        @pl.when(i % bps == 0)
        def _():
            for ref in (dsh_ref, dsc_ref, dg2_ref):
                ref[...] = jnp.zeros_like(ref)

        x1v = x1_ref[...]
        sc, g2v, n2 = sc_ref[...], g2_ref[...], n2_ref[...]
        xh2, r2 = _rms_fwd(x1v)
        xn2 = xh2 * n2
        h2 = (xn2 * (1.0 + sc) + sh_ref[...]).astype(BF16)
        a = _dot(h2, w1_ref[...])
        ra = jnp.maximum(a, 0.0)
        sq = (ra * ra).astype(BF16)
        fv = _dot(sq, w2_ref[...])
        x2 = x1v + g2v * fv
        xh3, r3 = _rms_fwd(x2)
        err = xh3 * fg_ref[...] - t_ref[...]
        loss_ref[...] += 0.5 * jnp.sum(jnp.mean(err * err, axis=-1, keepdims=True), axis=0, keepdims=True)
        dy = err * (1.0 / D)
        gfg_ref[...] += jnp.sum(dy * xh3, axis=0, keepdims=True)
        dx2 = _rms_bwd(dy * fg_ref[...], xh3, r3)
        dg2_ref[...] += jnp.sum(dx2 * fv, axis=0, keepdims=True)
        df = (g2v * dx2).astype(BF16)
        dsq = _dot_nt(df, w2_ref[...])
        da = (2.0 * ra * dsq).astype(BF16)
        dh2 = _dot_nt(da, w1_ref[...])
        dsh_ref[...] += jnp.sum(dh2, axis=0, keepdims=True)
        dsc_ref[...] += jnp.sum(dh2 * xn2, axis=0, keepdims=True)
        dxn2 = dh2 * (1.0 + sc)
        gn2_ref[...] += jnp.sum(dxn2 * xh2, axis=0, keepdims=True)
        dx1_ref[...] = dx2 + _rms_bwd(dxn2 * n2, xh2, r2)
        h2_ref[...] = h2
        da_ref[...] = da
        sq_ref[...] = sq
        df_ref[...] = df

    S = jax.ShapeDtypeStruct
    row = lambda w: pl.BlockSpec((TB, w), lambda i: (i, 0))
    return pl.pallas_call(
        body,
        out_shape=[S((N, D), F32), S((N, D), BF16), S((N, Dff), BF16), S((N, Dff), BF16), S((N, D), BF16),
                   S((1, 1), F32), S((1, D), F32), S((1, D), F32), S((Bl, 1, D), F32), S((Bl, 1, D), F32), S((Bl, 1, D), F32)],
        grid=(N // TB,),
        in_specs=[row(D), row(D), _seq_spec(D, bps), _seq_spec(D, bps), _seq_spec(D, bps),
                  _const_spec((1, D)), _const_spec((1, D)), _const_spec((D, Dff)), _const_spec((Dff, D))],
        out_specs=[row(D), row(D), row(Dff), row(Dff), row(D), _const_spec((1, 1)), _const_spec((1, D)), _const_spec((1, D)),
                   _seq_spec(D, bps), _seq_spec(D, bps), _seq_spec(D, bps)],
        compiler_params=_cparams(dimension_semantics=("arbitrary",)), name="mlp_fwd_bwd",
    )(x1, tgt, sh2, sc2, g2, n2g, fg, w1, w2)


_NO_EXCHANGE = _Carried((), (), {}, (), lambda ins, outs, sems: ((lambda: None), (lambda: None)))


def _grad_w(a, b, name, carried=_NO_EXCHANGE):
    N, K1 = a.shape
    K2 = b.shape[1]
    t1 = 1024 if K1 % 1024 == 0 else K1
    t2 = 1024 if K2 % 1024 == 0 else K2
    tn = 2048 if N % 2048 == 0 else N

    def body(a_ref, b_ref, o_ref):
        @pl.when(pl.program_id(2) == 0)
        def _():
            o_ref[...] = jnp.zeros_like(o_ref)

        o_ref[...] += _dot_tn(a_ref[...], b_ref[...])

    (g,), extra = _call_carrying(
        body, carried, (a, b), out_shape=[jax.ShapeDtypeStruct((K1, K2), F32)], grid=(K1 // t1, K2 // t2, N // tn),
        in_specs=[pl.BlockSpec((tn, t1), lambda i, j, k: (k, i)), pl.BlockSpec((tn, t2), lambda i, j, k: (k, j))],
        out_specs=[pl.BlockSpec((t1, t2), lambda i, j, k: (i, j))],
        compiler_params=_cparams(dimension_semantics=("arbitrary", "arbitrary", "arbitrary")), name=name)
    return g, extra


def _inproj_bwd(x2, dx1, dp, sh1, sc1, n1g, w_in, carried):
    N, D = x2.shape
    IN = w_in.shape[1]
    Bl = sh1.shape[0]
    TB = _row_block(N // Bl)
    bps = (N // Bl) // TB

    def body(x_ref, dx1_ref, dp_ref, sh_ref, sc_ref, g_ref, w_ref, gx_ref, h_ref, gn1_ref, dsh_ref, dsc_ref):
        i = pl.program_id(0)

        @pl.when(i == 0)
        def _():
            gn1_ref[...] = jnp.zeros_like(gn1_ref)

        @pl.when(i % bps == 0)
        def _():
            dsh_ref[...] = jnp.zeros_like(dsh_ref)
            dsc_ref[...] = jnp.zeros_like(dsc_ref)

        sc, n1 = sc_ref[...], g_ref[...]
        xh, r = _rms_fwd(x_ref[...])
        xn = xh * n1
        h_ref[...] = (xn * (1.0 + sc) + sh_ref[...]).astype(BF16)
        dh = _dot_nt(dp_ref[...], w_ref[...])
        dsh_ref[...] += jnp.sum(dh, axis=0, keepdims=True)
        dsc_ref[...] += jnp.sum(dh * xn, axis=0, keepdims=True)
        dxn = dh * (1.0 + sc)
        gn1_ref[...] += jnp.sum(dxn * xh, axis=0, keepdims=True)
        gx_ref[...] = dx1_ref[...] + _rms_bwd(dxn * n1, xh, r)

    S = jax.ShapeDtypeStruct
    row = lambda w: pl.BlockSpec((TB, w), lambda i: (i, 0))
    return _call_carrying(
        body, carried, (x2, dx1, dp, sh1, sc1, n1g, w_in),
        out_shape=[S((N, D), F32), S((N, D), BF16), S((1, D), F32), S((Bl, 1, D), F32), S((Bl, 1, D), F32)],
        grid=(N // TB,),
        in_specs=[row(D), row(D), row(IN), _seq_spec(D, bps), _seq_spec(D, bps), _const_spec((1, D)), _const_spec((D, IN))],
        out_specs=[row(D), row(D), _const_spec((1, D)), _seq_spec(D, bps), _seq_spec(D, bps)],
        compiler_params=_cparams(dimension_semantics=("arbitrary",)), name="inproj_bwd")


def _diag_blocks_from_groups(m, nb):
    G, a, b = m.shape
    gb = G // nb
    eye = jnp.eye(gb, dtype=m.dtype)
    mm = m.reshape(nb, gb, a, b)
    return (mm[:, :, :, None, :] * eye[None, :, None, :, None]).reshape(nb, gb * a, gb * b)


def _groups_from_diag_blocks(d, G, a, b):
    nb = d.shape[0]
    gb = G // nb
    dd = d.reshape(nb, gb, a, gb, b)
    idx = jnp.arange(gb)
    return dd[:, idx, :, idx, :].transpose(1, 0, 2, 3).reshape(G, a, b)


def _pad_rows(v, rows):
    return jnp.concatenate([v, jnp.zeros((rows - v.shape[0],) + v.shape[1:], v.dtype)], axis=0)


def _pack(vs):
    flat = jnp.concatenate([v.reshape(-1) for v in vs])
    n = flat.shape[0]
    tile = SUBLANES * LANES
    npad = -(-n // tile) * tile
    flat = jnp.concatenate([flat, jnp.zeros((npad - n,), flat.dtype)])
    return flat.reshape(npad // LANES, LANES)


def _unpack(packed, shapes):
    flat = packed.reshape(-1)
    out, off = [], 0
    for s in shapes:
        n = 1
        for d in s:
            n *= d
        out.append(flat[off:off + n].reshape(s))
        off += n
    return out


def kernel(x, c, norm1_g, norm2_g, w_ada, b_ada, w_in, lam_re, lam_im, log_dt, b_re, b_im, c_re, c_im, d_skip, w_glu, b_glu, conv_w, w_proj_ssm, w_proj_conv, w_out, w_ff1, w_ff2, final_g, loss_target, m_norm1_g, m_norm2_g, m_w_ada, m_b_ada, m_w_in, m_lam_re, m_lam_im, m_log_dt, m_b_re, m_b_im, m_c_re, m_c_im, m_d_skip, m_w_glu, m_b_glu, m_conv_w, m_w_proj_ssm, m_w_proj_conv, m_w_out, m_w_ff1, m_w_ff2, m_final_g, v_norm1_g, v_norm2_g, v_w_ada, v_b_ada, v_w_in, v_lam_re, v_lam_im, v_log_dt, v_b_re, v_b_im, v_c_re, v_c_im, v_d_skip, v_w_glu, v_b_glu, v_conv_w, v_w_proj_ssm, v_w_proj_conv, v_w_out, v_w_ff1, v_w_ff2, v_final_g):
    Bl, L, D = x.shape
    N = Bl * L
    Ds = Dc = D // 2
    G, H, P = Ds // SSM_GROUP, SSM_GROUP, SSM_STATE
    GP = G * P
    NP = GP // LANES
    nb = _s5_dims(Bl, L, Ds)[4]
    IN = Ds + 3 * Dc + 2 * D
    ax, ay, ac = _mesh_pos()
    q = 2 * ax + ay
    dev = 2 * q + ac

    big_names = ["w_in", "w_ff1", "w_ff2", "w_out", "w_proj_ssm", "w_proj_conv", "w_glu"]
    big_w = dict(w_in=w_in[0], w_ff1=w_ff1[0], w_ff2=w_ff2[0], w_out=w_out[0],
                 w_proj_ssm=w_proj_ssm[0], w_proj_conv=w_proj_conv[0], w_glu=w_glu[0])
    big_axis = dict(w_in=1, w_ff1=1, w_ff2=0, w_out=0, w_proj_ssm=1, w_proj_conv=1, w_glu=0)
    axes = [big_axis[k] for k in big_names]
    shard_shapes = [big_w[k].shape for k in big_names]
    pos = jnp.stack([q, ac]).astype(jnp.int32)
    own_only = {k: _cast_into_full(big_w[k], big_axis[k], pos, "cast_" + k) for k in big_names}
    full = {"w_in": _run_carried(_carry_allgather([own_only["w_in"]], [big_axis["w_in"]], [big_w["w_in"].shape]),
                                 "allgather_w_in")[0]}

    Dcs = conv_w.shape[2]
    first = _allgather8(_pack([c, conv_w[0]]), "allgather_c").reshape(N_DEV, -1)
    c_all = first[:, :Bl * D].reshape(N_DEV * Bl, D)
    cw = first[0::2, Bl * D:Bl * D + 3 * Dcs].reshape(N_CHIPS, 3, Dcs).transpose(1, 0, 2).reshape(3, Dc)
    cw8 = _pad_rows(cw, SUBLANES)
    Ca = w_ada.shape[2]
    b_ada_sh = lax.dynamic_slice_in_dim(b_ada, q * Ca, Ca, axis=1)
    mod_part = _ada_fwd(c_all, w_ada[0], b_ada_sh)
    mod_g = _allgather8(mod_part, "allgather_mod")
    mod_all = mod_g[0::2].transpose(1, 0, 2).reshape(N_DEV * Bl, N_CHIPS * Ca)
    mod = lax.dynamic_slice_in_dim(mod_all, dev * Bl, Bl, axis=0)
    sh1, sc1, g1, sh2, sc2, g2 = [mod[:, k * D:(k + 1) * D].reshape(Bl, 1, D) for k in range(6)]

    ldt_c = log_dt[0].reshape(G, 1)
    bt_r = b_re[0].transpose(2, 0, 1)
    bt_i = b_im[0].transpose(2, 0, 1)
    lbr, lbi, bbt_r, bbt_i = _s5_disc(lam_re[0], lam_im[0], ldt_c, bt_r, bt_i)
    lam_r_p = lbr.reshape(NP, LANES)
    lam_i_p = lbi.reshape(NP, LANES)
    Bm_r = _diag_blocks_from_groups(bbt_r.transpose(1, 0, 2), nb).astype(BF16)
    Bm_i = _diag_blocks_from_groups(bbt_i.transpose(1, 0, 2), nb).astype(BF16)
    Cm_r = _diag_blocks_from_groups(c_re[0].transpose(0, 2, 1), nb).astype(BF16)
    Cm_i = _diag_blocks_from_groups(-c_im[0].transpose(0, 2, 1), nb).astype(BF16)

    x2 = x.reshape(N, D)
    mixer_w = ["w_out", "w_proj_ssm", "w_proj_conv", "w_glu"]
    mlp_w = ["w_ff1", "w_ff2"]
    layout = lambda ks: ([big_axis[k] for k in ks], [big_w[k].shape for k in ks])
    gather = lambda ks: _carry_allgather([own_only[k] for k in ks], *layout(ks))
    p2, gathered = _inproj_fwd(x2, sh1, sc1, norm1_g, full["w_in"], gather(mixer_w))
    full.update(zip(mixer_w, gathered))
    p3 = p2.reshape(Bl, L, IN)
    (Sr, Si, Sb_r, Sb_i, ylin3), gathered = _s5_fwd(p3, Bm_r, Bm_i, Cm_r, Cm_i, lam_r_p, lam_i_p, d_skip, gather(mlp_w))
    full.update(zip(mlp_w, gathered))
    ylin2 = ylin3.reshape(N, Ds)
    mix_w = (full["w_glu"], b_glu, cw8, full["w_proj_ssm"], full["w_proj_conv"], full["w_out"])
    x1 = _mix_fwd(p2, ylin2, x2, g1, *mix_w)

    (dx1, h2b, dab, sqb, dfb, loss_p, g_fg, g_n2, dsh2, dsc2, dg2) = _mlp_fwd_bwd(
        x1, loss_target.reshape(N, D), sh2, sc2, g2, norm2_g, final_g.reshape(1, D), full["w_ff1"], full["w_ff2"])
    g_full = {"w_ff1": _grad_w(h2b, dab, "grad_w_ff1")[0], "w_ff2": _grad_w(sqb, dfb, "grad_w_ff2")[0]}

    exchange = lambda ks: _carry_sibling_exchange([g_full[k] for k in ks], *layout(ks))
    presum = lambda ks, theirs: [_presum(g_full[k], t, big_axis[k], big_w[k].shape, pos, "presum_" + k) for k, t in zip(ks, theirs)]
    chip_sum = lambda ks, parts, recv: [_sum_chips(p, r, pos, "sum_" + k) for k, p, r in zip(ks, parts, recv)]

    (dyl2, dp2, gw_out, gw_ps, gw_pc, gw_glu, gb_glu, gcw8, dg1), theirs_mlp = _mix_bwd(
        p2, ylin2, dx1, g1, *mix_w, exchange(mlp_w))
    g_full.update(w_out=gw_out, w_proj_ssm=gw_ps, w_proj_conv=gw_pc, w_glu=gw_glu)
    parts_mlp = presum(mlp_w, theirs_mlp)
    (dp3, dBm_r, dBm_i, dCm_r, dCm_i, dlam_r_p, dlam_i_p, g_dsk), extra = _s5_bwd(
        dyl2.reshape(Bl, L, Ds), p3, dp2.reshape(Bl, L, IN), Sr, Si, Sb_r, Sb_i, Bm_r, Bm_i, Cm_r, Cm_i, lam_r_p, lam_i_p,
        d_skip, _carry_join(_carry_chip_scatter(parts_mlp), exchange(mixer_w)))
    recv_mlp, theirs_mix = extra[:len(mlp_w)], extra[len(mlp_w):]
    halves_mlp = chip_sum(mlp_w, parts_mlp, recv_mlp)
    parts_mix = presum(mixer_w, theirs_mix)
    dp_all = dp3.reshape(N, IN)
    (grad_x2, hb, g_n1, dsh1, dsc1), _ = _inproj_bwd(x2, dx1, dp_all, sh1, sc1, norm1_g, full["w_in"], _NO_EXCHANGE)
    g_full["w_in"], extra = _grad_w(hb, dp_all, "grad_w_in",
                                    _carry_join(_carry_sibling_share(halves_mlp), _carry_chip_scatter(parts_mix)))
    reduced = dict(zip(mlp_w, extra[:len(mlp_w)]))
    halves_mix = chip_sum(mixer_w, parts_mix, extra[len(mlp_w):])
    theirs_in = _run_carried(exchange(["w_in"]), "rs_exchange_w_in")
    parts_in = presum(["w_in"], theirs_in)
    recv_in = _run_carried(_carry_chip_scatter(parts_in), "rs_scatter_w_in")
    halves_in = chip_sum(["w_in"], parts_in, recv_in)
    reduced.update(zip(mixer_w + ["w_in"], _run_carried(_carry_sibling_share(halves_mix + halves_in), "rs_share_rest")))

    dbbt_r = _groups_from_diag_blocks(dBm_r, G, H, P).transpose(1, 0, 2)
    dbbt_i = _groups_from_diag_blocks(dBm_i, G, H, P).transpose(1, 0, 2)
    dc_re = _groups_from_diag_blocks(dCm_r, G, P, H).transpose(0, 2, 1)
    dc_im = -_groups_from_diag_blocks(dCm_i, G, P, H).transpose(0, 2, 1)
    dmod = jnp.concatenate([dsh1, dsc1, dg1, dsh2, dsc2, dg2], axis=-1).reshape(Bl, 6 * D)
    small = [g_n1, g_n2, g_fg, g_dsk, gb_glu, gcw8[:3], dlam_r_p, dlam_i_p, dbbt_r, dbbt_i, dc_re, dc_im]
    small_shapes = [v.shape for v in small]
    n_small = sum(int(v.size) for v in small)
    gathered = _allgather8(_pack(small + [dmod]), "allgather_small")
    red = _unpack(_sum_devices(gathered, "sum_small"), small_shapes)
    (r_n1, r_n2, r_fg, r_dsk, r_bglu, r_cw, r_dlr, r_dli, r_dbr, r_dbi, r_cre, r_cim) = red
    dmod_all = gathered.reshape(N_DEV, -1)[:, n_small:n_small + Bl * 6 * D].reshape(N_DEV * Bl, 6 * D)
    gw_ada, gb_ada = _ada_bwd(c_all, lax.dynamic_slice_in_dim(dmod_all, q * Ca, Ca, axis=1), dmod_all)
    g_lr, g_li, g_ldt, g_bt_r, g_bt_i = _s5_disc_bwd(lam_re[0], lam_im[0], ldt_c, bt_r, bt_i,
                                                   r_dlr.reshape(G, P), r_dli.reshape(G, P), r_dbr, r_dbi)

    grads = dict(
        norm1_g=r_n1, norm2_g=r_n2, w_ada=gw_ada, b_ada=gb_ada, lam_re=g_lr, lam_im=g_li, log_dt=g_ldt.reshape(1, G),
        b_re=g_bt_r.transpose(1, 2, 0), b_im=g_bt_i.transpose(1, 2, 0), c_re=r_cre, c_im=r_cim, d_skip=r_dsk,
        b_glu=r_bglu, conv_w=lax.dynamic_slice_in_dim(r_cw, q * Dcs, Dcs, axis=1), final_g=r_fg, **reduced)
    weights = dict(norm1_g=norm1_g, norm2_g=norm2_g, w_ada=w_ada, b_ada=b_ada, w_in=w_in, lam_re=lam_re, lam_im=lam_im,
                   log_dt=log_dt, b_re=b_re, b_im=b_im, c_re=c_re, c_im=c_im, d_skip=d_skip, w_glu=w_glu, b_glu=b_glu,
                   conv_w=conv_w, w_proj_ssm=w_proj_ssm, w_proj_conv=w_proj_conv, w_out=w_out, w_ff1=w_ff1, w_ff2=w_ff2,
                   final_g=final_g)
    m_in = dict(norm1_g=m_norm1_g, norm2_g=m_norm2_g, w_ada=m_w_ada, b_ada=m_b_ada, w_in=m_w_in, lam_re=m_lam_re,
                lam_im=m_lam_im, log_dt=m_log_dt, b_re=m_b_re, b_im=m_b_im, c_re=m_c_re, c_im=m_c_im, d_skip=m_d_skip,
                w_glu=m_w_glu, b_glu=m_b_glu, conv_w=m_conv_w, w_proj_ssm=m_w_proj_ssm, w_proj_conv=m_w_proj_conv,
                w_out=m_w_out, w_ff1=m_w_ff1, w_ff2=m_w_ff2, final_g=m_final_g)
    v_in = dict(norm1_g=v_norm1_g, norm2_g=v_norm2_g, w_ada=v_w_ada, b_ada=v_b_ada, w_in=v_w_in, lam_re=v_lam_re,
                lam_im=v_lam_im, log_dt=v_log_dt, b_re=v_b_re, b_im=v_b_im, c_re=v_c_re, c_im=v_c_im, d_skip=v_d_skip,
                w_glu=v_w_glu, b_glu=v_b_glu, conv_w=v_conv_w, w_proj_ssm=v_w_proj_ssm, w_proj_conv=v_w_proj_conv,
                w_out=v_w_out, w_ff1=v_w_ff1, w_ff2=v_w_ff2, final_g=v_final_g)
    names = list(weights)
    grads = {k: grads[k].reshape(weights[k].shape) for k in names}

    big_upd = big_names + ["w_ada"]
    delta, new_m, new_v = {}, {}, {}
    for k in big_upd:
        shp = weights[k].shape
        two_d = lambda a: a.reshape(shp[-2], shp[-1])
        d_, m_, v_ = _adamw(two_d(weights[k]), two_d(grads[k]), two_d(m_in[k]), two_d(v_in[k]), "adamw_" + k)
        delta[k], new_m[k], new_v[k] = d_.reshape(shp), m_.reshape(shp), v_.reshape(shp)
    small_upd = [k for k in names if k not in big_upd]
    flat2 = lambda a: a.reshape(-1, a.shape[-1])
    d_, m_, v_ = _adamw_many([flat2(weights[k]) for k in small_upd], [flat2(grads[k]) for k in small_upd],
                             [flat2(m_in[k]) for k in small_upd], [flat2(v_in[k]) for k in small_upd])
    for k, dd, mm, vv in zip(small_upd, d_, m_, v_):
        shp = weights[k].shape
        delta[k], new_m[k], new_v[k] = dd.reshape(shp), mm.reshape(shp), vv.reshape(shp)

    loss = lax.psum(loss_p[0, 0], ("x", "y", "c"))
    grad_x = grad_x2.reshape(Bl, L, D)
    return (loss, grad_x, *[grads[k] for k in names], *[delta[k] for k in names],
            *[new_m[k] for k in names], *[new_v[k] for k in names])
```

```python
import functools
from typing import Callable, NamedTuple

import jax
import jax.numpy as jnp
from jax import lax
from jax.experimental import pallas as pl
from jax.experimental.pallas import tpu as pltpu

F32 = jnp.float32
BF16 = jnp.bfloat16
MESH = pl.DeviceIdType.MESH
N_CHIPS = 4
N_DEV = 8
LANES = 128
SUBLANES = 8
V7X_VMEM_BYTES = 64 * 1024 * 1024
VMEM_LIMIT = V7X_VMEM_BYTES - 6 * 1024 * 1024
SSM_GROUP = 16
SSM_STATE = 64
S5_ROW_PAD = 4
HALO_ROWS = 16
RMS_EPS = 1e-6
ADAM_LR, ADAM_B1, ADAM_B2, ADAM_EPS, ADAM_WD, ADAM_STEP = 0.001, 0.9, 0.999, 1e-08, 0.01, 10

ANY = pl.BlockSpec(memory_space=pl.ANY)
VMEM_SPEC = pl.BlockSpec(memory_space=pltpu.VMEM)


def _cparams(**kw):
    return pltpu.CompilerParams(vmem_limit_bytes=VMEM_LIMIT, **kw)


def _dot(a, b):
    return jnp.dot(a, b, preferred_element_type=F32)


def _dot_nt(a, b):
    return lax.dot_general(a, b, (((1,), (1,)), ((), ())), preferred_element_type=F32)


def _dot_tn(a, b):
    return lax.dot_general(a, b, (((0,), (0,)), ((), ())), preferred_element_type=F32)


def _mesh_pos():
    return lax.axis_index("x"), lax.axis_index("y"), lax.axis_index("c")


def _allgather8(v, name, carried=None):
    r, c = v.shape

    def body(x_ref, out_ref, send_sems, recv_sems, local_sem):
        x, y, cc = _mesh_pos()
        me, sibling = (x, y, cc), (x, y, 1 - cc)
        chips = [(1 - x, y), (x, 1 - y), (1 - x, 1 - y)]

        def slot(px, py, pc):
            return out_ref.at[4 * px + 2 * py + pc]

        def copy(k, block, to, src=None):
            return pltpu.make_async_remote_copy(
                src_ref=slot(*block) if src is None else src, dst_ref=slot(*block),
                send_sem=send_sems.at[k], recv_sem=recv_sems.at[k], device_id=to, device_id_type=MESH)

        mine = pltpu.make_async_copy(x_ref, slot(*me), local_sem)
        mine.start()
        first = [copy(0, me, sibling, src=x_ref)]
        first += [copy(1 + j, me, (*chip, cc), src=x_ref) for j, chip in enumerate(chips)]
        for cp in first:
            cp.start()
        passed = [copy(4 + j, (*chip, cc), sibling) for j, chip in enumerate(chips)]
        for j, chip in enumerate(chips):
            copy(1 + j, (*chip, cc), me).wait_recv()
            passed[j].start()
        copy(0, sibling, me).wait_recv()
        for j, chip in enumerate(chips):
            copy(4 + j, (*chip, 1 - cc), me).wait_recv()
        for cp in first + passed:
            cp.wait_send()
        mine.wait()

    sems = [pltpu.SemaphoreType.DMA((7,)), pltpu.SemaphoreType.DMA((7,)), pltpu.SemaphoreType.DMA]
    out_shape = jax.ShapeDtypeStruct((N_DEV, r, c), v.dtype)
    if carried is None:
        return pl.pallas_call(body, out_shape=out_shape, in_specs=[VMEM_SPEC], out_specs=VMEM_SPEC,
                              scratch_shapes=sems, name=name)(v)
    (out,), extra = _call_carrying(body, carried, (v,), out_shape=[out_shape], in_specs=[VMEM_SPEC],
                                   out_specs=[VMEM_SPEC], scratch_shapes=sems, name=name)
    return out, extra


def _shard_region(ref, axis, shard_shape, q, half):
    R, C = shard_shape
    r0, nr = (0, R) if half is None else (half * (R // 2), R // 2)
    if axis == 1:
        return ref.at[pl.ds(r0, nr), pl.ds(q * C, C)]
    return ref.at[pl.ds(q * R + r0, nr), :]


class _Carried(NamedTuple):
    inputs: tuple
    out_shapes: tuple
    aliases: dict
    sems: tuple
    steps: Callable


def _carry_join(a, b):
    na_i, na_o, na_s = len(a.inputs), len(a.out_shapes), len(a.sems)

    def steps(ins, outs, sems):
        sa, fa = a.steps(ins[:na_i], outs[:na_o], sems[:na_s])
        sb, fb = b.steps(ins[na_i:], outs[na_o:], sems[na_s:])


        def start():
            sa()
            sb()

        def finish():
            fa()
            fb()

        return start, finish

    aliases = dict(a.aliases)
    aliases.update({na_i + i: na_o + o for i, o in b.aliases.items()})
    return _Carried(a.inputs + b.inputs, a.out_shapes + b.out_shapes, aliases, a.sems + b.sems, steps)


def _call_carrying(body, carried, args, *, out_shape, in_specs, out_specs, scratch_shapes=(), grid=None, aliases=None,
                   name, **kw):
    n_in, n_out, n_sc = len(in_specs), len(out_specs), len(scratch_shapes)
    n_ci, n_co = len(carried.inputs), len(carried.out_shapes)

    def wrapped(*refs):
        ins, refs = refs[:n_in], refs[n_in:]
        c_ins, refs = refs[:n_ci], refs[n_ci:]
        outs, refs = refs[:n_out], refs[n_out:]
        c_outs, refs = refs[:n_co], refs[n_co:]
        scratch, c_sems = refs[:n_sc], refs[n_sc:]
        start, finish = carried.steps(c_ins, c_outs, c_sems)
        if grid is None:
            start()
            body(*ins, *outs, *scratch)
            finish()
        else:
            ids = [pl.program_id(d) for d in range(len(grid))]
            first = functools.reduce(jnp.logical_and, [i == 0 for i in ids])
            last = functools.reduce(jnp.logical_and, [i == g - 1 for i, g in zip(ids, grid)])
            pl.when(first)(start)
            body(*ins, *outs, *scratch)
            pl.when(last)(finish)

    if grid is not None:
        kw["grid"] = grid
    io_aliases = dict(aliases or {})
    io_aliases.update({n_in + i: n_out + o for i, o in carried.aliases.items()})
    res = pl.pallas_call(
        wrapped, out_shape=list(out_shape) + list(carried.out_shapes),
        in_specs=list(in_specs) + [ANY] * n_ci, out_specs=list(out_specs) + [ANY] * n_co,
        scratch_shapes=list(scratch_shapes) + list(carried.sems),
        input_output_aliases=io_aliases, name=name, **kw,
    )(*args, *carried.inputs)
    return res[:n_out], res[n_out:]


def _run_carried(carried, name):
    return _call_carrying(lambda: None, carried, (), out_shape=(), in_specs=(), out_specs=(), name=name)[1]


def _place_in_slot(v, dev_arr, name):
    r, c = v.shape

    def body(d_ref, v_ref, o_ref):
        o_ref[...] = v_ref[...]

    return pl.pallas_call(
        body, out_shape=jax.ShapeDtypeStruct((N_DEV, r, c), v.dtype),
        grid_spec=pltpu.PrefetchScalarGridSpec(
            num_scalar_prefetch=1, grid=(1,), in_specs=[pl.BlockSpec((r, c), lambda i, d: (0, 0))],
            out_specs=pl.BlockSpec((None, r, c), lambda i, d: (d[0], 0, 0))),
        name=name)(dev_arr, v)


def _carry_allgather8(buf):
    def steps(ins, outs, sems):
        send_s, recv_s = sems
        out = outs[0]
        x, y, cc = _mesh_pos()
        me, sibling = (x, y, cc), (x, y, 1 - cc)
        chips = [(1 - x, y), (x, 1 - y), (1 - x, 1 - y)]

        def copy(k, block, to):
            px, py, pc = block
            slot = out.at[4 * px + 2 * py + pc]
            return pltpu.make_async_remote_copy(src_ref=slot, dst_ref=slot, send_sem=send_s.at[k], recv_sem=recv_s.at[k],
                                                device_id=to, device_id_type=MESH)

        first = [copy(0, me, sibling)] + [copy(1 + j, me, (*chip, cc)) for j, chip in enumerate(chips)]
        passed = [copy(4 + j, (*chip, cc), sibling) for j, chip in enumerate(chips)]

        def start():
            for cp in first:
                cp.start()

        def finish():
            for j, chip in enumerate(chips):
                copy(1 + j, (*chip, cc), me).wait_recv()
                passed[j].start()
            copy(0, sibling, me).wait_recv()
            for j, chip in enumerate(chips):
                copy(4 + j, (*chip, 1 - cc), me).wait_recv()
            for cp in first + passed:
                cp.wait_send()

        return start, finish

    return _Carried((buf,), (jax.ShapeDtypeStruct(buf.shape, buf.dtype),), {0: 0}, (pltpu.SemaphoreType.DMA((7,)),) * 2, steps)


def _carry_allgather(fulls, axes, shapes):
    n = len(fulls)
    return _Carried(tuple(fulls), tuple(jax.ShapeDtypeStruct(f.shape, f.dtype) for f in fulls),
                    {i: i for i in range(n)}, (pltpu.SemaphoreType.DMA((3 * n,)),) * 4,
                    lambda ins, outs, sems: _allgather_weights_steps(outs, axes, shapes, *sems))


def _allgather_weights_steps(outs, axes, shapes, send_s, recv_s, fsend_s, frecv_s):
    n = len(outs)
    x, y, c = _mesh_pos()
    q = 2 * x + y
    sibling = (x, y, 1 - c)
    chips = [(1 - x, y), (x, 1 - y), (1 - x, 1 - y)]

    def region(i, qq, half):
        return _shard_region(outs[i], axes[i], shapes[i], qq, half)

    def remote(src, dst, ss, rs, to):
        return pltpu.make_async_remote_copy(src_ref=src, dst_ref=dst, send_sem=ss, recv_sem=rs,
                                            device_id=to, device_id_type=MESH)

    def ici(i, j, qq):
        cx, cy = chips[j]
        reg = region(i, qq, c)
        return remote(reg, reg, send_s.at[3 * i + j], recv_s.at[3 * i + j], (cx, cy, c))

    def d2d(i, j, half):
        cx, cy = chips[j]
        reg = region(i, 2 * cx + cy, half)
        return remote(reg, reg, fsend_s.at[3 * i + j], frecv_s.at[3 * i + j], sibling)

    def start():
        for i in range(n):
            for j in range(3):
                ici(i, j, q).start()

    def finish():
        for i in range(n):
            for j, (cx, cy) in enumerate(chips):
                ici(i, j, 2 * cx + cy).wait_recv()
                d2d(i, j, c).start()
        for i in range(n):
            for j in range(3):
                d2d(i, j, 1 - c).wait_recv()
        for i in range(n):
            for j in range(3):
                ici(i, j, q).wait_send()
                d2d(i, j, c).wait_send()

    return start, finish


def _carry_sibling_exchange(grads, axes, shapes):
    n = len(grads)

    def steps(ins, theirs, sems):
        send_s, recv_s = sems
        x, y, c = _mesh_pos()

        def copies():
            return [pltpu.make_async_remote_copy(
                src_ref=_shard_region(ins[i], axes[i], shapes[i], qq, 1 - c), dst_ref=theirs[i].at[qq],
                send_sem=send_s.at[N_CHIPS * i + qq], recv_sem=recv_s.at[N_CHIPS * i + qq],
                device_id=(x, y, 1 - c), device_id_type=MESH) for i in range(n) for qq in range(N_CHIPS)]

        def start():
            for cp in copies():
                cp.start()

        def finish():
            for cp in copies():
                cp.wait()

        return start, finish

    stacked = tuple(jax.ShapeDtypeStruct((N_CHIPS, R // 2, C), F32) for (R, C) in shapes)
    return _Carried(tuple(grads), stacked, {}, (pltpu.SemaphoreType.DMA((N_CHIPS * n,)),) * 2, steps)


def _carry_chip_scatter(parts):
    n = len(parts)

    def steps(ins, outs, sems):
        send_s, recv_s = sems
        x, y, c = _mesh_pos()
        chips = [(1 - x, y), (x, 1 - y), (1 - x, 1 - y)]

        def copies():
            return [pltpu.make_async_remote_copy(
                src_ref=ins[i].at[2 * cx + cy], dst_ref=outs[i].at[j],
                send_sem=send_s.at[3 * i + j], recv_sem=recv_s.at[3 * i + j],
                device_id=(cx, cy, c), device_id_type=MESH) for i in range(n) for j, (cx, cy) in enumerate(chips)]

        def start():
            for cp in copies():
                cp.start()

        def finish():
            for cp in copies():
                cp.wait()

        return start, finish

    return _Carried(tuple(parts), tuple(jax.ShapeDtypeStruct((3,) + p.shape[1:], p.dtype) for p in parts), {},
                    (pltpu.SemaphoreType.DMA((3 * n,)),) * 2, steps)


def _carry_sibling_share(fulls):
    n = len(fulls)

    def steps(ins, outs, sems):
        send_s, recv_s = sems
        x, y, c = _mesh_pos()

        def copy(i, half):
            rh = fulls[i].shape[0] // 2
            rows = outs[i].at[pl.ds(half * rh, rh), :]
            return pltpu.make_async_remote_copy(src_ref=rows, dst_ref=rows, send_sem=send_s.at[i], recv_sem=recv_s.at[i],
                                                device_id=(x, y, 1 - c), device_id_type=MESH)

        def start():
            for i in range(n):
                copy(i, c).start()

        def finish():
            for i in range(n):
                copy(i, 1 - c).wait_recv()
                copy(i, c).wait_send()

        return start, finish

    return _Carried(tuple(fulls), tuple(jax.ShapeDtypeStruct(f.shape, f.dtype) for f in fulls),
                    {i: i for i in range(n)}, (pltpu.SemaphoreType.DMA((n,)),) * 2, steps)


def _row_block(rows, target=256):
    return target if rows % target == 0 else rows


def _cast_into_full(w, axis, pos, name):
    R, C = w.shape
    rb = _row_block(R)
    nrb = R // rb
    if axis == 1:
        shape, omap = (R, N_CHIPS * C), lambda i, s: (i, s[0])
    else:
        shape, omap = (N_CHIPS * R, C), lambda i, s: (s[0] * nrb + i, 0)

    def body(s_ref, w_ref, o_ref):
        o_ref[...] = w_ref[...].astype(BF16)

    return pl.pallas_call(
        body, out_shape=jax.ShapeDtypeStruct(shape, BF16),
        grid_spec=pltpu.PrefetchScalarGridSpec(
            num_scalar_prefetch=1, grid=(nrb,), in_specs=[pl.BlockSpec((rb, C), lambda i, s: (i, 0))],
            out_specs=pl.BlockSpec((rb, C), omap)),
        name=name)(pos, w)


def _presum(g, theirs, axis, shape, pos, name):
    R, C = shape
    Rh = R // 2
    rb = _row_block(Rh)
    nr = Rh // rb
    if axis == 1:
        gmap = lambda k, i, s: (s[1] * nr + i, k)
    else:
        gmap = lambda k, i, s: (k * (R // rb) + s[1] * nr + i, 0)

    def body(s_ref, g_ref, t_ref, o_ref):
        o_ref[...] = (g_ref[...] + t_ref[...]).astype(BF16)

    spec = pl.BlockSpec((None, rb, C), lambda k, i, s: (k, i, 0))
    return pl.pallas_call(
        body, out_shape=jax.ShapeDtypeStruct((N_CHIPS, Rh, C), BF16),
        grid_spec=pltpu.PrefetchScalarGridSpec(
            num_scalar_prefetch=1, grid=(N_CHIPS, nr), in_specs=[pl.BlockSpec((rb, C), gmap), spec], out_specs=spec),
        name=name)(pos, g, theirs)


def _sum_chips(own, recv, pos, name):
    _, Rh, C = own.shape
    rb = _row_block(Rh)
    nr = Rh // rb

    def body(s_ref, o_ref, r_ref, out_ref):
        acc = o_ref[...].astype(F32)
        for k in range(3):
            acc = acc + r_ref[k].astype(F32)
        out_ref[...] = acc

    return pl.pallas_call(
        body, out_shape=jax.ShapeDtypeStruct((2 * Rh, C), F32),
        grid_spec=pltpu.PrefetchScalarGridSpec(
            num_scalar_prefetch=1, grid=(nr,),
            in_specs=[pl.BlockSpec((None, rb, C), lambda i, s: (s[0], i, 0)), pl.BlockSpec((3, rb, C), lambda i, s: (0, i, 0))],
            out_specs=pl.BlockSpec((rb, C), lambda i, s: (s[1] * nr + i, 0))),
        name=name)(pos, own, recv)


def _sum_devices(parts, name):
    K, R, C = parts.shape

    def body(p_ref, o_ref):
        acc = p_ref[0]
        for k in range(1, K):
            acc = acc + p_ref[k]
        o_ref[...] = acc

    return pl.pallas_call(body, out_shape=jax.ShapeDtypeStruct((R, C), F32), name=name)(parts)


def _adamw_math(w, g, m, v):
    nm = ADAM_B1 * m + (1.0 - ADAM_B1) * g
    nv = ADAM_B2 * v + (1.0 - ADAM_B2) * (g * g)
    m_hat = nm / (1.0 - ADAM_B1 ** ADAM_STEP)
    v_hat = nv / (1.0 - ADAM_B2 ** ADAM_STEP)
    return -ADAM_LR * (m_hat / (jnp.sqrt(v_hat) + ADAM_EPS) + ADAM_WD * w), nm, nv


def _adamw_many(ws, gs, ms, vs):
    n = len(ws)

    def body(*refs):
        for k in range(n):
            w, g, m, v = (refs[j * n + k][...] for j in range(4))
            for j, val in enumerate(_adamw_math(w, g, m, v)):
                refs[(4 + j) * n + k][...] = val

    shapes = [jax.ShapeDtypeStruct(w.shape, F32) for w in ws]
    res = pl.pallas_call(body, out_shape=shapes * 3, compiler_params=_cparams(), name="adamw_small")(*ws, *gs, *ms, *vs)
    return res[:n], res[n:2 * n], res[2 * n:]


def _adamw(w, g, m, v, name):
    R, C = w.shape
    rb = _row_block(R)

    def body(w_ref, g_ref, m_ref, v_ref, d_ref, nm_ref, nv_ref):
        gv = g_ref[...]
        nm = ADAM_B1 * m_ref[...] + (1.0 - ADAM_B1) * gv
        nv = ADAM_B2 * v_ref[...] + (1.0 - ADAM_B2) * (gv * gv)
        m_hat = nm / (1.0 - ADAM_B1 ** ADAM_STEP)
        v_hat = nv / (1.0 - ADAM_B2 ** ADAM_STEP)
        d_ref[...] = -ADAM_LR * (m_hat / (jnp.sqrt(v_hat) + ADAM_EPS) + ADAM_WD * w_ref[...])
        nm_ref[...] = nm
        nv_ref[...] = nv

    spec = pl.BlockSpec((rb, C), lambda i: (i, 0))
    return pl.pallas_call(body, out_shape=[jax.ShapeDtypeStruct((R, C), F32)] * 3, grid=(R // rb,),
                          in_specs=[spec] * 4, out_specs=[spec] * 3, name=name)(w, g, m, v)


def _silu(v):
    return v * jax.nn.sigmoid(v)


def _ada_fwd(c_all, w_sh, b_sh):
    S, D = c_all.shape
    Ca = w_sh.shape[1]
    cb = 512 if Ca % 512 == 0 else Ca

    def body(c_ref, w_ref, b_ref, o_ref):
        act = _silu(c_ref[...]).astype(BF16)
        o_ref[...] = _dot(act, w_ref[...].astype(BF16)) + b_ref[...]

    return pl.pallas_call(
        body, out_shape=jax.ShapeDtypeStruct((S, Ca), F32), grid=(Ca // cb,),
        in_specs=[pl.BlockSpec((S, D), lambda j: (0, 0)), pl.BlockSpec((D, cb), lambda j: (0, j)),
                  pl.BlockSpec((1, cb), lambda j: (0, j))],
        out_specs=pl.BlockSpec((S, cb), lambda j: (0, j)), name="ada_fwd")(c_all, w_sh, b_sh)


def _ada_bwd(c_all, dmod_sh, dmod_all):
    S, D = c_all.shape
    Ca = dmod_sh.shape[1]
    C6 = dmod_all.shape[1]

    def body(c_ref, ds_ref, da_ref, gw_ref, gb_ref):
        act = _silu(c_ref[...]).astype(BF16)
        gw_ref[...] = _dot_tn(act, ds_ref[...].astype(BF16))
        gb_ref[...] = jnp.sum(da_ref[...], axis=0, keepdims=True)

    return pl.pallas_call(
        body, out_shape=[jax.ShapeDtypeStruct((D, Ca), F32), jax.ShapeDtypeStruct((1, C6), F32)],
        compiler_params=_cparams(), name="ada_bwd")(c_all, dmod_sh, dmod_all)


def _rms_fwd(xv):
    r = lax.rsqrt(jnp.mean(xv * xv, axis=-1, keepdims=True) + RMS_EPS)
    return xv * r, r


def _rms_bwd(dxh, xh, r):
    return r * (dxh - xh * jnp.mean(dxh * xh, axis=-1, keepdims=True))


def _const_spec(shape):
    nd = len(shape)
    return pl.BlockSpec(shape, lambda *_: (0,) * nd)


def _seq_spec(D, bps, rev_blocks=None):
    if rev_blocks is None:
        return pl.BlockSpec((None, 1, D), lambda i: (i // bps, 0, 0))
    return pl.BlockSpec((None, 1, D), lambda i: ((rev_blocks - 1 - i) // bps, 0, 0))


def _inproj_fwd(x2, sh1, sc1, n1g, w_in, carried):
    N, D = x2.shape
    IN = w_in.shape[1]
    Bl = sh1.shape[0]
    TB = _row_block(N // Bl, 512)
    bps = (N // Bl) // TB

    def body(x_ref, sh_ref, sc_ref, g_ref, w_ref, p_ref):
        xh, _ = _rms_fwd(x_ref[...])
        h = (xh * g_ref[...]) * (1.0 + sc_ref[...]) + sh_ref[...]
        p_ref[...] = _dot(h.astype(BF16), w_ref[...]).astype(BF16)

    (p2,), extra = _call_carrying(
        body, carried, (x2, sh1, sc1, n1g, w_in), out_shape=[jax.ShapeDtypeStruct((N, IN), BF16)], grid=(N // TB,),
        in_specs=[pl.BlockSpec((TB, D), lambda i: (i, 0)), _seq_spec(D, bps), _seq_spec(D, bps),
                  _const_spec((1, D)), _const_spec((D, IN))],
        out_specs=[pl.BlockSpec((TB, IN), lambda i: (i, 0))],
        compiler_params=_cparams(dimension_semantics=("arbitrary",)), name="inproj_fwd")
    return p2, extra


def _s5_dims(Bl, L, Ds):
    G = Ds // SSM_GROUP
    GP = G * SSM_STATE
    NP = GP // LANES
    T = min(64, L // 2)
    nb = 2 if (Ds // 2) % LANES == 0 else 1
    return G, GP, NP, T, nb


def _s5_disc_math(lr, li, ldt, bt_r, bt_i):
    dt = jnp.exp(ldt)
    er = jnp.exp(lr * dt)
    lbr = er * jnp.cos(li * dt)
    lbi = er * jnp.sin(li * dt)
    den = lr * lr + li * li
    fr = ((lbr - 1.0) * lr + lbi * li) / den
    fi = (lbi * lr - (lbr - 1.0) * li) / den
    return lbr, lbi, fr[None] * bt_r - fi[None] * bt_i, fr[None] * bt_i + fi[None] * bt_r


def _s5_disc(lr, li, ldt, bt_r, bt_i):
    def body(lr_ref, li_ref, ldt_ref, br_ref, bi_ref, o0, o1, o2, o3):
        res = _s5_disc_math(lr_ref[...], li_ref[...], ldt_ref[...], br_ref[...], bi_ref[...])
        for o, v in zip((o0, o1, o2, o3), res):
            o[...] = v

    S = jax.ShapeDtypeStruct
    return pl.pallas_call(body, out_shape=[S(lr.shape, F32)] * 2 + [S(bt_r.shape, F32)] * 2, name="s5_disc")(lr, li, ldt, bt_r, bt_i)


def _s5_disc_bwd(lr, li, ldt, bt_r, bt_i, dlbr, dlbi, dbbr, dbbi):
    def body(lr_ref, li_ref, ldt_ref, br_ref, bi_ref, g0, g1, g2, g3, o0, o1, o2, o3, o4):
        _, vjp = jax.vjp(_s5_disc_math, lr_ref[...], li_ref[...], ldt_ref[...], br_ref[...], bi_ref[...])
        res = vjp((g0[...], g1[...], g2[...], g3[...]))
        for o, v in zip((o0, o1, o2, o3, o4), res):
            o[...] = v

    S = jax.ShapeDtypeStruct
    return pl.pallas_call(body, out_shape=[S(lr.shape, F32)] * 2 + [S(ldt.shape, F32)] + [S(bt_r.shape, F32)] * 2,
                          name="s5_disc_bwd")(lr, li, ldt, bt_r, bt_i, dlbr, dlbi, dbbr, dbbi)


S5_SCAN_PANELS = 4
S5_SCAN_STEPS = 4


def _panel_scan(src_r, src_i, dst_r, dst_i, lr_ref, li_ref, car_r, car_i, NP, Bl, T, TP, adjoint):
    BT = Bl * TP
    PG = min(S5_SCAN_PANELS, NP)
    CH = S5_SCAN_STEPS
    for k0 in range(0, NP, PG):
        ks = list(range(k0, k0 + PG))
        lr = [jnp.broadcast_to(lr_ref[pl.ds(k, 1), :], (Bl, LANES)) for k in ks]
        li = [jnp.broadcast_to(li_ref[pl.ds(k, 1), :], (Bl, LANES)) for k in ks]

        def trip(cc, carry):
            ts = [(T - 1 - (cc * CH + s)) if adjoint else (cc * CH + s) for s in range(CH)]
            idx = [[pl.ds(k * BT + t, Bl, stride=TP) for t in ts] for k in ks]
            loaded = [[(src_r[ix, :], src_i[ix, :]) for ix in idx[j]] for j in range(PG)]
            results, new_carry = [], []
            for j in range(PG):
                ar, ai = carry[j]
                res = []
                for s in range(CH):
                    br, bi = loaded[j][s]
                    if adjoint:
                        ar, ai = br + lr[j] * ar + li[j] * ai, bi + lr[j] * ai - li[j] * ar
                    else:
                        ar, ai = lr[j] * ar - li[j] * ai + br, lr[j] * ai + li[j] * ar + bi
                    res.append((ar, ai))
                results.append(res)
                new_carry.append((ar, ai))
            for j in range(PG):
                for s in range(CH):
                    dst_r[idx[j][s], :] = results[j][s][0]
                    dst_i[idx[j][s], :] = results[j][s][1]
            return tuple(new_carry)

        init = tuple((car_r[pl.ds(k * SUBLANES, Bl), :], car_i[pl.ds(k * SUBLANES, Bl), :]) for k in ks)
        fin = lax.fori_loop(0, T // CH, trip, init, unroll=2)
        for j, k in enumerate(ks):
            car_r[pl.ds(k * SUBLANES, Bl), :] = fin[j][0]
            car_i[pl.ds(k * SUBLANES, Bl), :] = fin[j][1]


def _s5_fwd(p3, Bm_r, Bm_i, Cm_r, Cm_i, lam_r, lam_i, dsk, carried):
    Bl, L, _ = p3.shape
    Ds = dsk.shape[1]
    G, GP, NP, T, nb = _s5_dims(Bl, L, Ds)
    nT = L // T
    TP = T + S5_ROW_PAD
    BT = Bl * TP
    dsb, gpb, npb = Ds // nb, GP // nb, NP // nb

    def body(u_ref, br_ref, bi_ref, cr_ref, ci_ref, lr_ref, li_ref, dsk_ref, sr_ref, si_ref, sb_r, sb_i, y_ref,
             car_r, car_i, upad, ypad, bu_r, bu_i):
        i = pl.program_id(0)

        @pl.when(i == 0)
        def _():
            car_r[...] = jnp.zeros_like(car_r)
            car_i[...] = jnp.zeros_like(car_i)
            upad[...] = jnp.zeros_like(upad)

        zpad = jnp.zeros((S5_ROW_PAD, LANES), F32)
        for k in range(NP):
            for b in range(Bl):
                sr_ref[pl.ds(k * BT + b * TP + T, S5_ROW_PAD), :] = zpad
                si_ref[pl.ds(k * BT + b * TP + T, S5_ROW_PAD), :] = zpad
        for b in range(Bl):
            upad[pl.ds(b * TP, T), :] = u_ref[b].astype(F32)
        u = upad[...]
        ub = u.astype(BF16)
        for blk in range(nb):
            ubb = ub[:, blk * dsb:(blk + 1) * dsb]
            for bu_ref, b_ref in ((bu_r, br_ref), (bu_i, bi_ref)):
                res = _dot(ubb, b_ref[blk])
                for kk in range(npb):
                    k = blk * npb + kk
                    bu_ref[pl.ds(k * BT, BT), :] = res[:, kk * LANES:(kk + 1) * LANES]

        _panel_scan(bu_r, bu_i, sr_ref, si_ref, lr_ref, li_ref, car_r, car_i, NP, Bl, T, TP, adjoint=False)
        sb_r[...] = car_r[...]
        sb_i[...] = car_i[...]

        for blk in range(nb):
            s_r = jnp.concatenate([sr_ref[pl.ds((blk * npb + kk) * BT, BT), :] for kk in range(npb)], axis=1).astype(BF16)
            s_i = jnp.concatenate([si_ref[pl.ds((blk * npb + kk) * BT, BT), :] for kk in range(npb)], axis=1).astype(BF16)
            cols = slice(blk * dsb, (blk + 1) * dsb)
            ypad[:, cols] = _dot(s_r, cr_ref[blk]) + _dot(s_i, ci_ref[blk]) + dsk_ref[:, cols] * u[:, cols]
        for b in range(Bl):
            y_ref[b] = ypad[pl.ds(b * TP, T), :]

    S = jax.ShapeDtypeStruct
    state = S((nT, NP * BT, LANES), F32)
    bound = S((nT, NP * SUBLANES, LANES), F32)
    sspec = pl.BlockSpec((None, NP * BT, LANES), lambda i: (i, 0, 0))
    bspec = pl.BlockSpec((None, NP * SUBLANES, LANES), lambda i: (i, 0, 0))
    return _call_carrying(
        body, carried, (p3, Bm_r, Bm_i, Cm_r, Cm_i, lam_r, lam_i, dsk),
        out_shape=[state, state, bound, bound, S((Bl, L, Ds), F32)], grid=(nT,),
        in_specs=[pl.BlockSpec((Bl, T, Ds), lambda i: (0, i, 0)),
                  _const_spec((nb, dsb, gpb)), _const_spec((nb, dsb, gpb)),
                  _const_spec((nb, gpb, dsb)), _const_spec((nb, gpb, dsb)),
                  _const_spec((NP, LANES)), _const_spec((NP, LANES)), _const_spec((1, Ds))],
        out_specs=[sspec, sspec, bspec, bspec, pl.BlockSpec((Bl, T, Ds), lambda i: (0, i, 0))],
        scratch_shapes=[pltpu.VMEM((NP * SUBLANES, LANES), F32)] * 2 + [pltpu.VMEM((BT, Ds), F32)] * 2
        + [pltpu.VMEM((NP * BT, LANES), F32)] * 2,
        compiler_params=_cparams(dimension_semantics=("arbitrary",)), name="s5_fwd")


def _s5_bwd(dy3, p3, dp3, Sr, Si, Sb_r, Sb_i, Bm_r, Bm_i, Cm_r, Cm_i, lam_r, lam_i, dsk, carried):
    Bl, L, Ds = dy3.shape
    G, GP, NP, T, nb = _s5_dims(Bl, L, Ds)
    nT = L // T
    TP = T + S5_ROW_PAD
    BT = Bl * TP
    dsb, gpb, npb = Ds // nb, GP // nb, NP // nb

    def body(dy_ref, u_ref, dp_ref, sr_ref, si_ref, sbr_ref, sbi_ref, br_ref, bi_ref, cr_ref, ci_ref, lr_ref, li_ref, dsk_ref,
             du_ref, dbr_ref, dbi_ref, dcr_ref, dci_ref, dlr_ref, dli_ref, ddsk_ref,
             a_r, a_i, car_r, car_i, acc_r, acc_i, dypad, upad, dupad):
        i = pl.program_id(0)

        @pl.when(i == 0)
        def _():
            for ref in (car_r, car_i, acc_r, acc_i, dbr_ref, dbi_ref, dcr_ref, dci_ref, ddsk_ref, dypad, upad):
                ref[...] = jnp.zeros_like(ref)

        for b in range(Bl):
            dypad[pl.ds(b * TP, T), :] = dy_ref[b]
            upad[pl.ds(b * TP, T), :] = u_ref[b].astype(F32)
        dy = dypad[...]
        dyb = dy.astype(BF16)
        u = upad[...]
        ub = u.astype(BF16)
        for blk in range(nb):
            dyb_b = dyb[:, blk * dsb:(blk + 1) * dsb]
            for q_ref, c_ref in ((a_r, cr_ref), (a_i, ci_ref)):
                res = _dot_nt(dyb_b, c_ref[blk])
                for kk in range(npb):
                    q_ref[pl.ds((blk * npb + kk) * BT, BT), :] = res[:, kk * LANES:(kk + 1) * LANES]

        _panel_scan(a_r, a_i, a_r, a_i, lr_ref, li_ref, car_r, car_i, NP, Bl, T, TP, adjoint=True)

        first_block = (i == nT - 1)
        for k in range(NP):
            rows = pl.ds(k * BT, BT)
            av_r, av_i = a_r[rows, :], a_i[rows, :]
            sp_r = pltpu.roll(sr_ref[rows, :], 1, 0)
            sp_i = pltpu.roll(si_ref[rows, :], 1, 0)
            acc = pl.ds(k * SUBLANES, SUBLANES)
            acc_r[acc, :] += jnp.sum((av_r * sp_r + av_i * sp_i).reshape(BT // SUBLANES, SUBLANES, LANES), axis=0)
            acc_i[acc, :] += jnp.sum((av_i * sp_r - av_r * sp_i).reshape(BT // SUBLANES, SUBLANES, LANES), axis=0)
            t0 = pl.ds(k * BT, Bl, stride=TP)
            a0_r, a0_i = a_r[t0, :], a_i[t0, :]
            brow = pl.ds(k * SUBLANES, Bl)
            sb_pr = jnp.where(first_block, 0.0, sbr_ref[brow, :])
            sb_pi = jnp.where(first_block, 0.0, sbi_ref[brow, :])
            acc_r[brow, :] += a0_r * sb_pr + a0_i * sb_pi
            acc_i[brow, :] += a0_i * sb_pr - a0_r * sb_pi

        ddsk_ref[...] += jnp.sum(dy * u, axis=0, keepdims=True)
        for blk in range(nb):
            cols = slice(blk * dsb, (blk + 1) * dsb)
            rows = [pl.ds((blk * npb + kk) * BT, BT) for kk in range(npb)]
            av_r = jnp.concatenate([a_r[r, :] for r in rows], axis=1).astype(BF16)
            av_i = jnp.concatenate([a_i[r, :] for r in rows], axis=1).astype(BF16)
            dupad[:, cols] = _dot_nt(av_r, br_ref[blk]) + _dot_nt(av_i, bi_ref[blk]) + dy[:, cols] * dsk_ref[:, cols]
            dbr_ref[blk] += _dot_tn(ub[:, cols], av_r)
            dbi_ref[blk] += _dot_tn(ub[:, cols], av_i)
            sv_r = jnp.concatenate([sr_ref[r, :] for r in rows], axis=1).astype(BF16)
            sv_i = jnp.concatenate([si_ref[r, :] for r in rows], axis=1).astype(BF16)
            dcr_ref[blk] += _dot_tn(sv_r, dyb[:, cols])
            dci_ref[blk] += _dot_tn(sv_i, dyb[:, cols])
        for b in range(Bl):
            du_ref[b] = dupad[pl.ds(b * TP, T), :].astype(BF16)

        @pl.when(i == nT - 1)
        def _():
            for k in range(NP):
                dlr_ref[pl.ds(k, 1), :] = jnp.sum(acc_r[pl.ds(k * SUBLANES, SUBLANES), :], axis=0, keepdims=True)
                dli_ref[pl.ds(k, 1), :] = jnp.sum(acc_i[pl.ds(k * SUBLANES, SUBLANES), :], axis=0, keepdims=True)

    S = jax.ShapeDtypeStruct
    rev = lambda i: nT - 1 - i
    sspec = pl.BlockSpec((None, NP * BT, LANES), lambda i: (rev(i), 0, 0))
    bspec = pl.BlockSpec((None, NP * SUBLANES, LANES), lambda i: (jnp.maximum(rev(i) - 1, 0), 0, 0))
    tspec = pl.BlockSpec((Bl, T, Ds), lambda i: (0, rev(i), 0))
    return _call_carrying(
        body, carried, (dy3, p3, dp3, Sr, Si, Sb_r, Sb_i, Bm_r, Bm_i, Cm_r, Cm_i, lam_r, lam_i, dsk),
        out_shape=[S(dp3.shape, dp3.dtype), S((nb, dsb, gpb), F32), S((nb, dsb, gpb), F32),
                   S((nb, gpb, dsb), F32), S((nb, gpb, dsb), F32), S((NP, LANES), F32), S((NP, LANES), F32), S((1, Ds), F32)],
        grid=(nT,),
        in_specs=[tspec, tspec, ANY, sspec, sspec, bspec, bspec,
                  _const_spec((nb, dsb, gpb)), _const_spec((nb, dsb, gpb)),
                  _const_spec((nb, gpb, dsb)), _const_spec((nb, gpb, dsb)),
                  _const_spec((NP, LANES)), _const_spec((NP, LANES)), _const_spec((1, Ds))],
        out_specs=[tspec, _const_spec((nb, dsb, gpb)), _const_spec((nb, dsb, gpb)),
                   _const_spec((nb, gpb, dsb)), _const_spec((nb, gpb, dsb)),
                   _const_spec((NP, LANES)), _const_spec((NP, LANES)), _const_spec((1, Ds))],
        aliases={2: 0},
        scratch_shapes=[pltpu.VMEM((NP * BT, LANES), F32)] * 2 + [pltpu.VMEM((NP * SUBLANES, LANES), F32)] * 4
        + [pltpu.VMEM((BT, Ds), F32)] * 3,
        compiler_params=_cparams(dimension_semantics=("arbitrary",)), name="s5_bwd")


def _mix_values(ylin, cb, cc, cx, gs, gc, halo_v, wglu, bglu, cw, wps, wpc, wout):
    yg, gelu_vjp = jax.vjp(jax.nn.gelu, ylin)
    sz = jax.nn.sigmoid(_dot(yg.astype(BF16), wglu) + bglu)
    ys = yg * sz
    v = cc * cx
    rows = lax.broadcasted_iota(jnp.int32, v.shape, 0)
    h6 = halo_v[HALO_ROWS - 2:HALO_ROWS - 1, :]
    h7 = halo_v[HALO_ROWS - 1:HALO_ROWS, :]
    v1 = jnp.where(rows == 0, h7, pltpu.roll(v, 1, 0))
    v2 = jnp.where(rows == 0, h6, jnp.where(rows == 1, h7, pltpu.roll(v, 2, 0)))
    cv = cw[0:1, :] * v2 + cw[1:2, :] * v1 + cw[2:3, :] * v
    yc = cb * cv
    ps = _dot(ys.astype(BF16), wps)
    pc = _dot(yc.astype(BF16), wpc)
    sgs = jax.nn.sigmoid(gs)
    sgc = jax.nn.sigmoid(gc)
    merged = sgs * ps + sgc * pc
    mo = _dot(merged.astype(BF16), wout)
    return dict(yg=yg, gelu_vjp=gelu_vjp, sz=sz, ys=ys, v=v, v1=v1, v2=v2, cv=cv, yc=yc, ps=ps, pc=pc,
                sgs=sgs, sgc=sgc, merged=merged, mo=mo)


def _mix_in_specs(TB, D, Ds, Dc, bps, blk):
    hb = TB // HALO_ROWS
    halo = lambda col: pl.BlockSpec((HALO_ROWS, Dc), lambda i: (jnp.maximum(blk(i) * hb - 1, 0), col))
    return [pl.BlockSpec((TB, Dc), lambda i: (blk(i), 1)), pl.BlockSpec((TB, Dc), lambda i: (blk(i), 2)),
            pl.BlockSpec((TB, Dc), lambda i: (blk(i), 3)), pl.BlockSpec((TB, D), lambda i: (blk(i), 2)),
            pl.BlockSpec((TB, D), lambda i: (blk(i), 3)), halo(2), halo(3),
            pl.BlockSpec((TB, Ds), lambda i: (blk(i), 0))]


def _mix_fwd(p2, ylin2, x2, g1, wglu, bglu, cw, wps, wpc, wout):
    N, D = x2.shape
    Ds = ylin2.shape[1]
    Dc = Ds
    Bl = g1.shape[0]
    TB = _row_block(N // Bl)
    bps = (N // Bl) // TB

    def body(cb_ref, cc_ref, cx_ref, gs_ref, gc_ref, hcc_ref, hcx_ref, yl_ref, x_ref, g1_ref,
             wglu_ref, bglu_ref, cw_ref, wps_ref, wpc_ref, wout_ref, x1_ref):
        i = pl.program_id(0)
        f32 = lambda ref: ref[...].astype(F32)
        halo_v = jnp.where(i % bps == 0, 0.0, f32(hcc_ref) * f32(hcx_ref))
        f = _mix_values(yl_ref[...], f32(cb_ref), f32(cc_ref), f32(cx_ref), f32(gs_ref), f32(gc_ref), halo_v,
                        wglu_ref[...], bglu_ref[...], cw_ref[...], wps_ref[...], wpc_ref[...], wout_ref[...])
        x1_ref[...] = x_ref[...] + g1_ref[...] * f["mo"]

    return pl.pallas_call(
        body, out_shape=jax.ShapeDtypeStruct((N, D), F32), grid=(N // TB,),
        in_specs=_mix_in_specs(TB, D, Ds, Dc, bps, lambda i: i) + [
            pl.BlockSpec((TB, D), lambda i: (i, 0)), _seq_spec(D, bps),
            _const_spec((Ds, Ds)), _const_spec((1, Ds)), _const_spec((SUBLANES, Dc)),
            _const_spec((Ds, D)), _const_spec((Dc, D)), _const_spec((D, D))],
        out_specs=pl.BlockSpec((TB, D), lambda i: (i, 0)),
        compiler_params=_cparams(), name="mix_fwd",
    )(p2, p2, p2, p2, p2, p2, p2, ylin2, x2, g1, wglu, bglu, cw, wps, wpc, wout)


def _mix_bwd(p2, ylin2, dx1, g1, wglu, bglu, cw, wps, wpc, wout, carried):
    N, D = dx1.shape
    Ds = ylin2.shape[1]
    Dc = Ds
    IN = p2.shape[1]
    Bl = g1.shape[0]
    TB = _row_block(N // Bl)
    bps = (N // Bl) // TB
    nblk = N // TB
    rev = lambda i: nblk - 1 - i

    def body(cb_ref, cc_ref, cx_ref, gs_ref, gc_ref, hcc_ref, hcx_ref, yl_ref, dx1_ref, g1_ref,
             wglu_ref, bglu_ref, cw_ref, wps_ref, wpc_ref, wout_ref,
             dyl_ref, dp_ref, gwout_ref, gwps_ref, gwpc_ref, gwglu_ref, gbglu_ref, gcw_ref, dg1_ref, nxt):
        i = pl.program_id(0)
        blk = rev(i)

        @pl.when(i == 0)
        def _():
            for ref in (gwout_ref, gwps_ref, gwpc_ref, gwglu_ref, gbglu_ref, gcw_ref):
                ref[...] = jnp.zeros_like(ref)

        @pl.when(i % bps == 0)
        def _():
            nxt[...] = jnp.zeros_like(nxt)
            dg1_ref[...] = jnp.zeros_like(dg1_ref)

        f32 = lambda ref: ref[...].astype(F32)
        cb, cc, cx = f32(cb_ref), f32(cc_ref), f32(cx_ref)
        halo_v = jnp.where(blk % bps == 0, 0.0, f32(hcc_ref) * f32(hcx_ref))
        wglu, wps, wpc, wout, cw = wglu_ref[...], wps_ref[...], wpc_ref[...], wout_ref[...], cw_ref[...]
        f = _mix_values(yl_ref[...], cb, cc, cx, f32(gs_ref), f32(gc_ref), halo_v, wglu, bglu_ref[...], cw, wps, wpc, wout)

        dx1v = dx1_ref[...]
        dg1_ref[...] += jnp.sum(dx1v * f["mo"], axis=0, keepdims=True)
        dmo = (g1_ref[...] * dx1v).astype(BF16)
        gwout_ref[...] += _dot_tn(f["merged"].astype(BF16), dmo)
        dmerged = _dot_nt(dmo, wout)
        dps = dmerged * f["sgs"]
        dpc = dmerged * f["sgc"]
        dgs = dmerged * f["ps"] * f["sgs"] * (1.0 - f["sgs"])
        dgc = dmerged * f["pc"] * f["sgc"] * (1.0 - f["sgc"])
        dpsb, dpcb = dps.astype(BF16), dpc.astype(BF16)
        gwps_ref[...] += _dot_tn(f["ys"].astype(BF16), dpsb)
        gwpc_ref[...] += _dot_tn(f["yc"].astype(BF16), dpcb)
        dys = _dot_nt(dpsb, wps)
        dyc = _dot_nt(dpcb, wpc)

        dcb = dyc * f["cv"]
        dcv = dyc * cb
        rows = lax.broadcasted_iota(jnp.int32, dcv.shape, 0)
        n0, n1 = nxt[0:1, :], nxt[1:2, :]
        d1 = jnp.where(rows == TB - 1, n0, pltpu.roll(dcv, TB - 1, 0))
        d2 = jnp.where(rows == TB - 2, n0, jnp.where(rows == TB - 1, n1, pltpu.roll(dcv, TB - 2, 0)))
        dv = cw[2:3, :] * dcv + cw[1:2, :] * d1 + cw[0:1, :] * d2
        nxt[0:2, :] = dcv[0:2, :]
        gcw_ref[0:1, :] += jnp.sum(dcv * f["v2"], axis=0, keepdims=True)
        gcw_ref[1:2, :] += jnp.sum(dcv * f["v1"], axis=0, keepdims=True)
        gcw_ref[2:3, :] += jnp.sum(dcv * f["v"], axis=0, keepdims=True)

        dz = dys * f["yg"] * f["sz"] * (1.0 - f["sz"])
        dzb = dz.astype(BF16)
        gwglu_ref[...] += _dot_tn(f["yg"].astype(BF16), dzb)
        gbglu_ref[...] += jnp.sum(dz, axis=0, keepdims=True)
        dyg = dys * f["sz"] + _dot_nt(dzb, wglu)
        dyl_ref[...] = f["gelu_vjp"](dyg)[0]

        dp_ref[:, Ds:Ds + Dc] = dcb.astype(BF16)
        dp_ref[:, Ds + Dc:Ds + 2 * Dc] = (dv * cx).astype(BF16)
        dp_ref[:, Ds + 2 * Dc:Ds + 3 * Dc] = (dv * cc).astype(BF16)
        dp_ref[:, Ds + 3 * Dc:Ds + 3 * Dc + D] = dgs.astype(BF16)
        dp_ref[:, Ds + 3 * Dc + D:IN] = dgc.astype(BF16)

    S = jax.ShapeDtypeStruct
    return _call_carrying(
        body, carried, (p2, p2, p2, p2, p2, p2, p2, ylin2, dx1, g1, wglu, bglu, cw, wps, wpc, wout),
        out_shape=[S((N, Ds), F32), S((N, IN), BF16), S((D, D), F32), S((Ds, D), F32), S((Dc, D), F32),
                   S((Ds, Ds), F32), S((1, Ds), F32), S((SUBLANES, Dc), F32), S((Bl, 1, D), F32)],
        grid=(nblk,),
        in_specs=_mix_in_specs(TB, D, Ds, Dc, bps, rev) + [
            pl.BlockSpec((TB, D), lambda i: (rev(i), 0)), _seq_spec(D, bps, nblk),
            _const_spec((Ds, Ds)), _const_spec((1, Ds)), _const_spec((SUBLANES, Dc)),
            _const_spec((Ds, D)), _const_spec((Dc, D)), _const_spec((D, D))],
        out_specs=[pl.BlockSpec((TB, Ds), lambda i: (rev(i), 0)), pl.BlockSpec((TB, IN), lambda i: (rev(i), 0)),
                   _const_spec((D, D)), _const_spec((Ds, D)), _const_spec((Dc, D)), _const_spec((Ds, Ds)),
                   _const_spec((1, Ds)), _const_spec((SUBLANES, Dc)), _seq_spec(D, bps, nblk)],
        scratch_shapes=[pltpu.VMEM((SUBLANES, Dc), F32)],
        compiler_params=_cparams(dimension_semantics=("arbitrary",)), name="mix_bwd")


def _mlp_fwd_bwd(x1, tgt, sh2, sc2, g2, n2g, fg, w1, w2):
    N, D = x1.shape
    Dff = w1.shape[1]
    Bl = sh2.shape[0]
    TB = _row_block(N // Bl)
    bps = (N // Bl) // TB

    def body(x1_ref, t_ref, sh_ref, sc_ref, g2_ref, n2_ref, fg_ref, w1_ref, w2_ref,
             dx1_ref, h2_ref, da_ref, sq_ref, df_ref, loss_ref, gfg_ref, gn2_ref, dsh_ref, dsc_ref, dg2_ref):
        i = pl.program_id(0)

        @pl.when(i == 0)
        def _():
            for ref in (loss_ref, gfg_ref, gn2_ref):
                ref[...] = jnp.zeros_like(ref)

        @pl.when(i % bps == 0)
        def _():
            for ref in (dsh_ref, dsc_ref, dg2_ref):
                ref[...] = jnp.zeros_like(ref)

        x1v = x1_ref[...]
        sc, g2v, n2 = sc_ref[...], g2_ref[...], n2_ref[...]
        xh2, r2 = _rms_fwd(x1v)
        xn2 = xh2 * n2
        h2 = (xn2 * (1.0 + sc) + sh_ref[...]).astype(BF16)
        a = _dot(h2, w1_ref[...])
        ra = jnp.maximum(a, 0.0)
        sq = (ra * ra).astype(BF16)
        fv = _dot(sq, w2_ref[...])
        x2 = x1v + g2v * fv
        xh3, r3 = _rms_fwd(x2)
        err = xh3 * fg_ref[...] - t_ref[...]
        loss_ref[...] += 0.5 * jnp.sum(jnp.mean(err * err, axis=-1, keepdims=True), axis=0, keepdims=True)
        dy = err * (1.0 / D)
        gfg_ref[...] += jnp.sum(dy * xh3, axis=0, keepdims=True)
        dx2 = _rms_bwd(dy * fg_ref[...], xh3, r3)
        dg2_ref[...] += jnp.sum(dx2 * fv, axis=0, keepdims=True)
        df = (g2v * dx2).astype(BF16)
        dsq = _dot_nt(df, w2_ref[...])
        da = (2.0 * ra * dsq).astype(BF16)
        dh2 = _dot_nt(da, w1_ref[...])
        dsh_ref[...] += jnp.sum(dh2, axis=0, keepdims=True)
        dsc_ref[...] += jnp.sum(dh2 * xn2, axis=0, keepdims=True)
        dxn2 = dh2 * (1.0 + sc)
        gn2_ref[...] += jnp.sum(dxn2 * xh2, axis=0, keepdims=True)
        dx1_ref[...] = dx2 + _rms_bwd(dxn2 * n2, xh2, r2)
        h2_ref[...] = h2
        da_ref[...] = da
        sq_ref[...] = sq
        df_ref[...] = df

    S = jax.ShapeDtypeStruct
    row = lambda w: pl.BlockSpec((TB, w), lambda i: (i, 0))
    return pl.pallas_call(
        body,
        out_shape=[S((N, D), F32), S((N, D), BF16), S((N, Dff), BF16), S((N, Dff), BF16), S((N, D), BF16),
                   S((1, 1), F32), S((1, D), F32), S((1, D), F32), S((Bl, 1, D), F32), S((Bl, 1, D), F32), S((Bl, 1, D), F32)],
        grid=(N // TB,),
        in_specs=[row(D), row(D), _seq_spec(D, bps), _seq_spec(D, bps), _seq_spec(D, bps),
                  _const_spec((1, D)), _const_spec((1, D)), _const_spec((D, Dff)), _const_spec((Dff, D))],
        out_specs=[row(D), row(D), row(Dff), row(Dff), row(D), _const_spec((1, 1)), _const_spec((1, D)), _const_spec((1, D)),
                   _seq_spec(D, bps), _seq_spec(D, bps), _seq_spec(D, bps)],
        compiler_params=_cparams(dimension_semantics=("arbitrary",)), name="mlp_fwd_bwd",
    )(x1, tgt, sh2, sc2, g2, n2g, fg, w1, w2)


_NO_EXCHANGE = _Carried((), (), {}, (), lambda ins, outs, sems: ((lambda: None), (lambda: None)))


def _grad_w(a, b, name, carried=_NO_EXCHANGE):
    N, K1 = a.shape
    K2 = b.shape[1]
    t1 = 1024 if K1 % 1024 == 0 else K1
    t2 = 1024 if K2 % 1024 == 0 else K2
    tn = 2048 if N % 2048 == 0 else N

    def body(a_ref, b_ref, o_ref):
        @pl.when(pl.program_id(2) == 0)
        def _():
            o_ref[...] = jnp.zeros_like(o_ref)

        o_ref[...] += _dot_tn(a_ref[...], b_ref[...])

    (g,), extra = _call_carrying(
        body, carried, (a, b), out_shape=[jax.ShapeDtypeStruct((K1, K2), F32)], grid=(K1 // t1, K2 // t2, N // tn),
        in_specs=[pl.BlockSpec((tn, t1), lambda i, j, k: (k, i)), pl.BlockSpec((tn, t2), lambda i, j, k: (k, j))],
        out_specs=[pl.BlockSpec((t1, t2), lambda i, j, k: (i, j))],
        compiler_params=_cparams(dimension_semantics=("arbitrary", "arbitrary", "arbitrary")), name=name)
    return g, extra


def _inproj_bwd(x2, dx1, dp, sh1, sc1, n1g, w_in, carried):
    N, D = x2.shape
    IN = w_in.shape[1]
    Bl = sh1.shape[0]
    TB = _row_block(N // Bl, 512)
    bps = (N // Bl) // TB

    def body(x_ref, dx1_ref, dp_ref, sh_ref, sc_ref, g_ref, w_ref, gx_ref, h_ref, gn1_ref, dsh_ref, dsc_ref):
        i = pl.program_id(0)

        @pl.when(i == 0)
        def _():
            gn1_ref[...] = jnp.zeros_like(gn1_ref)

        @pl.when(i % bps == 0)
        def _():
            dsh_ref[...] = jnp.zeros_like(dsh_ref)
            dsc_ref[...] = jnp.zeros_like(dsc_ref)

        sc, n1 = sc_ref[...], g_ref[...]
        xh, r = _rms_fwd(x_ref[...])
        xn = xh * n1
        h_ref[...] = (xn * (1.0 + sc) + sh_ref[...]).astype(BF16)
        dh = _dot_nt(dp_ref[...], w_ref[...])
        dsh_ref[...] += jnp.sum(dh, axis=0, keepdims=True)
        dsc_ref[...] += jnp.sum(dh * xn, axis=0, keepdims=True)
        dxn = dh * (1.0 + sc)
        gn1_ref[...] += jnp.sum(dxn * xh, axis=0, keepdims=True)
        gx_ref[...] = dx1_ref[...] + _rms_bwd(dxn * n1, xh, r)

    S = jax.ShapeDtypeStruct
    row = lambda w: pl.BlockSpec((TB, w), lambda i: (i, 0))
    return _call_carrying(
        body, carried, (x2, dx1, dp, sh1, sc1, n1g, w_in),
        out_shape=[S((N, D), F32), S((N, D), BF16), S((1, D), F32), S((Bl, 1, D), F32), S((Bl, 1, D), F32)],
        grid=(N // TB,),
        in_specs=[row(D), row(D), row(IN), _seq_spec(D, bps), _seq_spec(D, bps), _const_spec((1, D)), _const_spec((D, IN))],
        out_specs=[row(D), row(D), _const_spec((1, D)), _seq_spec(D, bps), _seq_spec(D, bps)],
        compiler_params=_cparams(dimension_semantics=("arbitrary",)), name="inproj_bwd")


def _diag_blocks_from_groups(m, nb):
    G, a, b = m.shape
    gb = G // nb
    eye = jnp.eye(gb, dtype=m.dtype)
    mm = m.reshape(nb, gb, a, b)
    return (mm[:, :, :, None, :] * eye[None, :, None, :, None]).reshape(nb, gb * a, gb * b)


def _groups_from_diag_blocks(d, G, a, b):
    nb = d.shape[0]
    gb = G // nb
    dd = d.reshape(nb, gb, a, gb, b)
    idx = jnp.arange(gb)
    return dd[:, idx, :, idx, :].transpose(1, 0, 2, 3).reshape(G, a, b)


def _pad_rows(v, rows):
    return jnp.concatenate([v, jnp.zeros((rows - v.shape[0],) + v.shape[1:], v.dtype)], axis=0)


def _pack(vs):
    flat = jnp.concatenate([v.reshape(-1) for v in vs])
    n = flat.shape[0]
    tile = SUBLANES * LANES
    npad = -(-n // tile) * tile
    flat = jnp.concatenate([flat, jnp.zeros((npad - n,), flat.dtype)])
    return flat.reshape(npad // LANES, LANES)


def _unpack(packed, shapes):
    flat = packed.reshape(-1)
    out, off = [], 0
    for s in shapes:
        n = 1
        for d in s:
            n *= d
        out.append(flat[off:off + n].reshape(s))
        off += n
    return out


def kernel(x, c, norm1_g, norm2_g, w_ada, b_ada, w_in, lam_re, lam_im, log_dt, b_re, b_im, c_re, c_im, d_skip, w_glu, b_glu, conv_w, w_proj_ssm, w_proj_conv, w_out, w_ff1, w_ff2, final_g, loss_target, m_norm1_g, m_norm2_g, m_w_ada, m_b_ada, m_w_in, m_lam_re, m_lam_im, m_log_dt, m_b_re, m_b_im, m_c_re, m_c_im, m_d_skip, m_w_glu, m_b_glu, m_conv_w, m_w_proj_ssm, m_w_proj_conv, m_w_out, m_w_ff1, m_w_ff2, m_final_g, v_norm1_g, v_norm2_g, v_w_ada, v_b_ada, v_w_in, v_lam_re, v_lam_im, v_log_dt, v_b_re, v_b_im, v_c_re, v_c_im, v_d_skip, v_w_glu, v_b_glu, v_conv_w, v_w_proj_ssm, v_w_proj_conv, v_w_out, v_w_ff1, v_w_ff2, v_final_g):
    Bl, L, D = x.shape
    N = Bl * L
    Ds = Dc = D // 2
    G, H, P = Ds // SSM_GROUP, SSM_GROUP, SSM_STATE
    GP = G * P
    NP = GP // LANES
    nb = _s5_dims(Bl, L, Ds)[4]
    IN = Ds + 3 * Dc + 2 * D
    ax, ay, ac = _mesh_pos()
    q = 2 * ax + ay
    dev = 2 * q + ac

    big_names = ["w_in", "w_ff1", "w_ff2", "w_out", "w_proj_ssm", "w_proj_conv", "w_glu"]
    big_w = dict(w_in=w_in[0], w_ff1=w_ff1[0], w_ff2=w_ff2[0], w_out=w_out[0],
                 w_proj_ssm=w_proj_ssm[0], w_proj_conv=w_proj_conv[0], w_glu=w_glu[0])
    big_axis = dict(w_in=1, w_ff1=1, w_ff2=0, w_out=0, w_proj_ssm=1, w_proj_conv=1, w_glu=0)
    axes = [big_axis[k] for k in big_names]
    shard_shapes = [big_w[k].shape for k in big_names]
    pos = jnp.stack([q, ac]).astype(jnp.int32)
    own_only = {k: _cast_into_full(big_w[k], big_axis[k], pos, "cast_" + k) for k in big_names}
    Dcs = conv_w.shape[2]
    first, (w_in_full,) = _allgather8(_pack([c, conv_w[0]]), "allgather_c_w_in",
                                      _carry_allgather([own_only["w_in"]], [big_axis["w_in"]], [big_w["w_in"].shape]))
    full = {"w_in": w_in_full}
    first = first.reshape(N_DEV, -1)
    c_all = first[:, :Bl * D].reshape(N_DEV * Bl, D)
    cw = first[0::2, Bl * D:Bl * D + 3 * Dcs].reshape(N_CHIPS, 3, Dcs).transpose(1, 0, 2).reshape(3, Dc)
    cw8 = _pad_rows(cw, SUBLANES)
    Ca = w_ada.shape[2]
    b_ada_sh = lax.dynamic_slice_in_dim(b_ada, q * Ca, Ca, axis=1)
    mod_part = _ada_fwd(c_all, w_ada[0], b_ada_sh)
    mod_g = _allgather8(mod_part, "allgather_mod")
    mod_all = mod_g[0::2].transpose(1, 0, 2).reshape(N_DEV * Bl, N_CHIPS * Ca)
    mod = lax.dynamic_slice_in_dim(mod_all, dev * Bl, Bl, axis=0)
    sh1, sc1, g1, sh2, sc2, g2 = [mod[:, k * D:(k + 1) * D].reshape(Bl, 1, D) for k in range(6)]

    ldt_c = log_dt[0].reshape(G, 1)
    bt_r = b_re[0].transpose(2, 0, 1)
    bt_i = b_im[0].transpose(2, 0, 1)
    lbr, lbi, bbt_r, bbt_i = _s5_disc(lam_re[0], lam_im[0], ldt_c, bt_r, bt_i)
    lam_r_p = lbr.reshape(NP, LANES)
    lam_i_p = lbi.reshape(NP, LANES)
    Bm_r = _diag_blocks_from_groups(bbt_r.transpose(1, 0, 2), nb).astype(BF16)
    Bm_i = _diag_blocks_from_groups(bbt_i.transpose(1, 0, 2), nb).astype(BF16)
    Cm_r = _diag_blocks_from_groups(c_re[0].transpose(0, 2, 1), nb).astype(BF16)
    Cm_i = _diag_blocks_from_groups(-c_im[0].transpose(0, 2, 1), nb).astype(BF16)

    x2 = x.reshape(N, D)
    mixer_w = ["w_out", "w_proj_ssm", "w_proj_conv", "w_glu"]
    mlp_w = ["w_ff1", "w_ff2"]
    layout = lambda ks: ([big_axis[k] for k in ks], [big_w[k].shape for k in ks])
    gather = lambda ks: _carry_allgather([own_only[k] for k in ks], *layout(ks))
    p2, gathered = _inproj_fwd(x2, sh1, sc1, norm1_g, full["w_in"], gather(mixer_w))
    full.update(zip(mixer_w, gathered))
    p3 = p2.reshape(Bl, L, IN)
    (Sr, Si, Sb_r, Sb_i, ylin3), gathered = _s5_fwd(p3, Bm_r, Bm_i, Cm_r, Cm_i, lam_r_p, lam_i_p, d_skip, gather(mlp_w))
    full.update(zip(mlp_w, gathered))
    ylin2 = ylin3.reshape(N, Ds)
    mix_w = (full["w_glu"], b_glu, cw8, full["w_proj_ssm"], full["w_proj_conv"], full["w_out"])
    x1 = _mix_fwd(p2, ylin2, x2, g1, *mix_w)

    (dx1, h2b, dab, sqb, dfb, loss_p, g_fg, g_n2, dsh2, dsc2, dg2) = _mlp_fwd_bwd(
        x1, loss_target.reshape(N, D), sh2, sc2, g2, norm2_g, final_g.reshape(1, D), full["w_ff1"], full["w_ff2"])
    g_full = {"w_ff1": _grad_w(h2b, dab, "grad_w_ff1")[0], "w_ff2": _grad_w(sqb, dfb, "grad_w_ff2")[0]}

    exchange = lambda ks: _carry_sibling_exchange([g_full[k] for k in ks], *layout(ks))
    presum = lambda ks, theirs: [_presum(g_full[k], t, big_axis[k], big_w[k].shape, pos, "presum_" + k) for k, t in zip(ks, theirs)]
    chip_sum = lambda ks, parts, recv: [_sum_chips(p, r, pos, "sum_" + k) for k, p, r in zip(ks, parts, recv)]

    (dyl2, dp2, gw_out, gw_ps, gw_pc, gw_glu, gb_glu, gcw8, dg1), theirs_mlp = _mix_bwd(
        p2, ylin2, dx1, g1, *mix_w, exchange(mlp_w))
    g_full.update(w_out=gw_out, w_proj_ssm=gw_ps, w_proj_conv=gw_pc, w_glu=gw_glu)
    parts_mlp = presum(mlp_w, theirs_mlp)
    (dp3, dBm_r, dBm_i, dCm_r, dCm_i, dlam_r_p, dlam_i_p, g_dsk), extra = _s5_bwd(
        dyl2.reshape(Bl, L, Ds), p3, dp2.reshape(Bl, L, IN), Sr, Si, Sb_r, Sb_i, Bm_r, Bm_i, Cm_r, Cm_i, lam_r_p, lam_i_p,
        d_skip, _carry_join(_carry_chip_scatter(parts_mlp), exchange(mixer_w)))
    recv_mlp, theirs_mix = extra[:len(mlp_w)], extra[len(mlp_w):]
    halves_mlp = chip_sum(mlp_w, parts_mlp, recv_mlp)
    parts_mix = presum(mixer_w, theirs_mix)
    dp_all = dp3.reshape(N, IN)
    (grad_x2, hb, g_n1, dsh1, dsc1), _ = _inproj_bwd(x2, dx1, dp_all, sh1, sc1, norm1_g, full["w_in"], _NO_EXCHANGE)

    dbbt_r = _groups_from_diag_blocks(dBm_r, G, H, P).transpose(1, 0, 2)
    dbbt_i = _groups_from_diag_blocks(dBm_i, G, H, P).transpose(1, 0, 2)
    dc_re = _groups_from_diag_blocks(dCm_r, G, P, H).transpose(0, 2, 1)
    dc_im = -_groups_from_diag_blocks(dCm_i, G, P, H).transpose(0, 2, 1)
    dmod = jnp.concatenate([dsh1, dsc1, dg1, dsh2, dsc2, dg2], axis=-1).reshape(Bl, 6 * D)
    small = [g_n1, g_n2, g_fg, g_dsk, gb_glu, gcw8[:3], dlam_r_p, dlam_i_p, dbbt_r, dbbt_i, dc_re, dc_im]
    small_shapes = [v.shape for v in small]
    n_small = sum(int(v.size) for v in small)
    small_slots = _place_in_slot(_pack(small + [dmod]), jnp.reshape(dev, (1,)).astype(jnp.int32), "place_small")

    g_full["w_in"], extra = _grad_w(
        hb, dp_all, "grad_w_in",
        _carry_join(_carry_join(_carry_sibling_share(halves_mlp), _carry_chip_scatter(parts_mix)), _carry_allgather8(small_slots)))
    reduced = dict(zip(mlp_w, extra[:len(mlp_w)]))
    halves_mix = chip_sum(mixer_w, parts_mix, extra[len(mlp_w):len(mlp_w) + len(mixer_w)])
    gathered = extra[-1]
    theirs_in = _run_carried(exchange(["w_in"]), "rs_exchange_w_in")
    parts_in = presum(["w_in"], theirs_in)
    recv_in = _run_carried(_carry_chip_scatter(parts_in), "rs_scatter_w_in")
    halves_in = chip_sum(["w_in"], parts_in, recv_in)
    reduced.update(zip(mixer_w + ["w_in"], _run_carried(_carry_sibling_share(halves_mix + halves_in), "rs_share_rest")))

    red = _unpack(_sum_devices(gathered, "sum_small"), small_shapes)
    (r_n1, r_n2, r_fg, r_dsk, r_bglu, r_cw, r_dlr, r_dli, r_dbr, r_dbi, r_cre, r_cim) = red
    dmod_all = gathered.reshape(N_DEV, -1)[:, n_small:n_small + Bl * 6 * D].reshape(N_DEV * Bl, 6 * D)
    gw_ada, gb_ada = _ada_bwd(c_all, lax.dynamic_slice_in_dim(dmod_all, q * Ca, Ca, axis=1), dmod_all)
    g_lr, g_li, g_ldt, g_bt_r, g_bt_i = _s5_disc_bwd(lam_re[0], lam_im[0], ldt_c, bt_r, bt_i,
                                                   r_dlr.reshape(G, P), r_dli.reshape(G, P), r_dbr, r_dbi)

    grads = dict(
        norm1_g=r_n1, norm2_g=r_n2, w_ada=gw_ada, b_ada=gb_ada, lam_re=g_lr, lam_im=g_li, log_dt=g_ldt.reshape(1, G),
        b_re=g_bt_r.transpose(1, 2, 0), b_im=g_bt_i.transpose(1, 2, 0), c_re=r_cre, c_im=r_cim, d_skip=r_dsk,
        b_glu=r_bglu, conv_w=lax.dynamic_slice_in_dim(r_cw, q * Dcs, Dcs, axis=1), final_g=r_fg, **reduced)
    weights = dict(norm1_g=norm1_g, norm2_g=norm2_g, w_ada=w_ada, b_ada=b_ada, w_in=w_in, lam_re=lam_re, lam_im=lam_im,
                   log_dt=log_dt, b_re=b_re, b_im=b_im, c_re=c_re, c_im=c_im, d_skip=d_skip, w_glu=w_glu, b_glu=b_glu,
                   conv_w=conv_w, w_proj_ssm=w_proj_ssm, w_proj_conv=w_proj_conv, w_out=w_out, w_ff1=w_ff1, w_ff2=w_ff2,
                   final_g=final_g)
    m_in = dict(norm1_g=m_norm1_g, norm2_g=m_norm2_g, w_ada=m_w_ada, b_ada=m_b_ada, w_in=m_w_in, lam_re=m_lam_re,
                lam_im=m_lam_im, log_dt=m_log_dt, b_re=m_b_re, b_im=m_b_im, c_re=m_c_re, c_im=m_c_im, d_skip=m_d_skip,
                w_glu=m_w_glu, b_glu=m_b_glu, conv_w=m_conv_w, w_proj_ssm=m_w_proj_ssm, w_proj_conv=m_w_proj_conv,
                w_out=m_w_out, w_ff1=m_w_ff1, w_ff2=m_w_ff2, final_g=m_final_g)
    v_in = dict(norm1_g=v_norm1_g, norm2_g=v_norm2_g, w_ada=v_w_ada, b_ada=v_b_ada, w_in=v_w_in, lam_re=v_lam_re,
                lam_im=v_lam_im, log_dt=v_log_dt, b_re=v_b_re, b_im=v_b_im, c_re=v_c_re, c_im=v_c_im, d_skip=v_d_skip,
                w_glu=v_w_glu, b_glu=v_b_glu, conv_w=v_conv_w, w_proj_ssm=v_w_proj_ssm, w_proj_conv=v_w_proj_conv,
                w_out=v_w_out, w_ff1=v_w_ff1, w_ff2=v_w_ff2, final_g=v_final_g)
    names = list(weights)
    grads = {k: grads[k].reshape(weights[k].shape) for k in names}

    big_upd = big_names + ["w_ada"]
    delta, new_m, new_v = {}, {}, {}
    for k in big_upd:
        shp = weights[k].shape
        two_d = lambda a: a.reshape(shp[-2], shp[-1])
        d_, m_, v_ = _adamw(two_d(weights[k]), two_d(grads[k]), two_d(m_in[k]), two_d(v_in[k]), "adamw_" + k)
        delta[k], new_m[k], new_v[k] = d_.reshape(shp), m_.reshape(shp), v_.reshape(shp)
    small_upd = [k for k in names if k not in big_upd]
    flat2 = lambda a: a.reshape(-1, a.shape[-1])
    d_, m_, v_ = _adamw_many([flat2(weights[k]) for k in small_upd], [flat2(grads[k]) for k in small_upd],
                             [flat2(m_in[k]) for k in small_upd], [flat2(v_in[k]) for k in small_upd])
    for k, dd, mm, vv in zip(small_upd, d_, m_, v_):
        shp = weights[k].shape
        delta[k], new_m[k], new_v[k] = dd.reshape(shp), mm.reshape(shp), vv.reshape(shp)

    loss = lax.psum(loss_p[0, 0], ("x", "y", "c"))
    grad_x = grad_x2.reshape(Bl, L, D)
    return (loss, grad_x, *[grads[k] for k in names], *[delta[k] for k in names],
            *[new_m[k] for k in names], *[new_v[k] for k in names])
```

```python
import functools
from typing import Callable, NamedTuple

import jax
import jax.numpy as jnp
from jax import lax
from jax.experimental import pallas as pl
from jax.experimental.pallas import tpu as pltpu

F32 = jnp.float32
BF16 = jnp.bfloat16
MESH = pl.DeviceIdType.MESH
N_CHIPS = 4
N_DEV = 8
LANES = 128
SUBLANES = 8
V7X_VMEM_BYTES = 64 * 1024 * 1024
VMEM_LIMIT = V7X_VMEM_BYTES - 6 * 1024 * 1024
SSM_GROUP = 16
SSM_STATE = 64
S5_ROW_PAD = 4
HALO_ROWS = 16
RMS_EPS = 1e-6
ADAM_LR, ADAM_B1, ADAM_B2, ADAM_EPS, ADAM_WD, ADAM_STEP = 0.001, 0.9, 0.999, 1e-08, 0.01, 10

ANY = pl.BlockSpec(memory_space=pl.ANY)
VMEM_SPEC = pl.BlockSpec(memory_space=pltpu.VMEM)


def _cparams(**kw):
    return pltpu.CompilerParams(vmem_limit_bytes=VMEM_LIMIT, **kw)


def _dot(a, b):
    return jnp.dot(a, b, preferred_element_type=F32)


def _dot_nt(a, b):
    return lax.dot_general(a, b, (((1,), (1,)), ((), ())), preferred_element_type=F32)


def _dot_tn(a, b):
    return lax.dot_general(a, b, (((0,), (0,)), ((), ())), preferred_element_type=F32)


def _mesh_pos():
    return lax.axis_index("x"), lax.axis_index("y"), lax.axis_index("c")


def _allgather8(v, name, carried=None):
    r, c = v.shape

    def body(x_ref, out_ref, send_sems, recv_sems, local_sem):
        x, y, cc = _mesh_pos()
        me, sibling = (x, y, cc), (x, y, 1 - cc)
        chips = [(1 - x, y), (x, 1 - y), (1 - x, 1 - y)]

        def slot(px, py, pc):
            return out_ref.at[4 * px + 2 * py + pc]

        def copy(k, block, to, src=None):
            return pltpu.make_async_remote_copy(
                src_ref=slot(*block) if src is None else src, dst_ref=slot(*block),
                send_sem=send_sems.at[k], recv_sem=recv_sems.at[k], device_id=to, device_id_type=MESH)

        mine = pltpu.make_async_copy(x_ref, slot(*me), local_sem)
        mine.start()
        first = [copy(0, me, sibling, src=x_ref)]
        first += [copy(1 + j, me, (*chip, cc), src=x_ref) for j, chip in enumerate(chips)]
        for cp in first:
            cp.start()
        passed = [copy(4 + j, (*chip, cc), sibling) for j, chip in enumerate(chips)]
        for j, chip in enumerate(chips):
            copy(1 + j, (*chip, cc), me).wait_recv()
            passed[j].start()
        copy(0, sibling, me).wait_recv()
        for j, chip in enumerate(chips):
            copy(4 + j, (*chip, 1 - cc), me).wait_recv()
        for cp in first + passed:
            cp.wait_send()
        mine.wait()

    sems = [pltpu.SemaphoreType.DMA((7,)), pltpu.SemaphoreType.DMA((7,)), pltpu.SemaphoreType.DMA]
    out_shape = jax.ShapeDtypeStruct((N_DEV, r, c), v.dtype)
    if carried is None:
        return pl.pallas_call(body, out_shape=out_shape, in_specs=[VMEM_SPEC], out_specs=VMEM_SPEC,
                              scratch_shapes=sems, name=name)(v)
    (out,), extra = _call_carrying(body, carried, (v,), out_shape=[out_shape], in_specs=[VMEM_SPEC],
                                   out_specs=[VMEM_SPEC], scratch_shapes=sems, name=name)
    return out, extra


def _shard_region(ref, axis, shard_shape, q, half):
    R, C = shard_shape
    r0, nr = (0, R) if half is None else (half * (R // 2), R // 2)
    if axis == 1:
        return ref.at[pl.ds(r0, nr), pl.ds(q * C, C)]
    return ref.at[pl.ds(q * R + r0, nr), :]


class _Carried(NamedTuple):
    inputs: tuple
    out_shapes: tuple
    aliases: dict
    sems: tuple
    steps: Callable


def _carry_join(a, b):
    na_i, na_o, na_s = len(a.inputs), len(a.out_shapes), len(a.sems)

    def steps(ins, outs, sems):
        sa, fa = a.steps(ins[:na_i], outs[:na_o], sems[:na_s])
        sb, fb = b.steps(ins[na_i:], outs[na_o:], sems[na_s:])


        def start():
            sa()
            sb()

        def finish():
            fa()
            fb()

        return start, finish

    aliases = dict(a.aliases)
    aliases.update({na_i + i: na_o + o for i, o in b.aliases.items()})
    return _Carried(a.inputs + b.inputs, a.out_shapes + b.out_shapes, aliases, a.sems + b.sems, steps)


def _call_carrying(body, carried, args, *, out_shape, in_specs, out_specs, scratch_shapes=(), grid=None, aliases=None,
                   name, **kw):
    n_in, n_out, n_sc = len(in_specs), len(out_specs), len(scratch_shapes)
    n_ci, n_co = len(carried.inputs), len(carried.out_shapes)

    def wrapped(*refs):
        ins, refs = refs[:n_in], refs[n_in:]
        c_ins, refs = refs[:n_ci], refs[n_ci:]
        outs, refs = refs[:n_out], refs[n_out:]
        c_outs, refs = refs[:n_co], refs[n_co:]
        scratch, c_sems = refs[:n_sc], refs[n_sc:]
        start, finish = carried.steps(c_ins, c_outs, c_sems)
        if grid is None:
            start()
            body(*ins, *outs, *scratch)
            finish()
        else:
            ids = [pl.program_id(d) for d in range(len(grid))]
            first = functools.reduce(jnp.logical_and, [i == 0 for i in ids])
            last = functools.reduce(jnp.logical_and, [i == g - 1 for i, g in zip(ids, grid)])
            pl.when(first)(start)
            body(*ins, *outs, *scratch)
            pl.when(last)(finish)

    if grid is not None:
        kw["grid"] = grid
    io_aliases = dict(aliases or {})
    io_aliases.update({n_in + i: n_out + o for i, o in carried.aliases.items()})
    res = pl.pallas_call(
        wrapped, out_shape=list(out_shape) + list(carried.out_shapes),
        in_specs=list(in_specs) + [ANY] * n_ci, out_specs=list(out_specs) + [ANY] * n_co,
        scratch_shapes=list(scratch_shapes) + list(carried.sems),
        input_output_aliases=io_aliases, name=name, **kw,
    )(*args, *carried.inputs)
    return res[:n_out], res[n_out:]


def _run_carried(carried, name):
    return _call_carrying(lambda: None, carried, (), out_shape=(), in_specs=(), out_specs=(), name=name)[1]


def _place_in_slot(v, dev_arr, name):
    r, c = v.shape

    def body(d_ref, v_ref, o_ref):
        o_ref[...] = v_ref[...]

    return pl.pallas_call(
        body, out_shape=jax.ShapeDtypeStruct((N_DEV, r, c), v.dtype),
        grid_spec=pltpu.PrefetchScalarGridSpec(
            num_scalar_prefetch=1, grid=(1,), in_specs=[pl.BlockSpec((r, c), lambda i, d: (0, 0))],
            out_specs=pl.BlockSpec((None, r, c), lambda i, d: (d[0], 0, 0))),
        name=name)(dev_arr, v)


def _carry_allgather8(buf):
    def steps(ins, outs, sems):
        send_s, recv_s = sems
        out = outs[0]
        x, y, cc = _mesh_pos()
        me, sibling = (x, y, cc), (x, y, 1 - cc)
        chips = [(1 - x, y), (x, 1 - y), (1 - x, 1 - y)]

        def copy(k, block, to):
            px, py, pc = block
            slot = out.at[4 * px + 2 * py + pc]
            return pltpu.make_async_remote_copy(src_ref=slot, dst_ref=slot, send_sem=send_s.at[k], recv_sem=recv_s.at[k],
                                                device_id=to, device_id_type=MESH)

        first = [copy(0, me, sibling)] + [copy(1 + j, me, (*chip, cc)) for j, chip in enumerate(chips)]
        passed = [copy(4 + j, (*chip, cc), sibling) for j, chip in enumerate(chips)]

        def start():
            for cp in first:
                cp.start()

        def finish():
            for j, chip in enumerate(chips):
                copy(1 + j, (*chip, cc), me).wait_recv()
                passed[j].start()
            copy(0, sibling, me).wait_recv()
            for j, chip in enumerate(chips):
                copy(4 + j, (*chip, 1 - cc), me).wait_recv()
            for cp in first + passed:
                cp.wait_send()

        return start, finish

    return _Carried((buf,), (jax.ShapeDtypeStruct(buf.shape, buf.dtype),), {0: 0}, (pltpu.SemaphoreType.DMA((7,)),) * 2, steps)


def _carry_allgather(fulls, axes, shapes):
    n = len(fulls)
    return _Carried(tuple(fulls), tuple(jax.ShapeDtypeStruct(f.shape, f.dtype) for f in fulls),
                    {i: i for i in range(n)}, (pltpu.SemaphoreType.DMA((3 * n,)),) * 4,
                    lambda ins, outs, sems: _allgather_weights_steps(outs, axes, shapes, *sems))


def _allgather_weights_steps(outs, axes, shapes, send_s, recv_s, fsend_s, frecv_s):
    n = len(outs)
    x, y, c = _mesh_pos()
    q = 2 * x + y
    sibling = (x, y, 1 - c)
    chips = [(1 - x, y), (x, 1 - y), (1 - x, 1 - y)]

    def region(i, qq, half):
        return _shard_region(outs[i], axes[i], shapes[i], qq, half)

    def remote(src, dst, ss, rs, to):
        return pltpu.make_async_remote_copy(src_ref=src, dst_ref=dst, send_sem=ss, recv_sem=rs,
                                            device_id=to, device_id_type=MESH)

    def ici(i, j, qq):
        cx, cy = chips[j]
        reg = region(i, qq, c)
        return remote(reg, reg, send_s.at[3 * i + j], recv_s.at[3 * i + j], (cx, cy, c))

    def d2d(i, j, half):
        cx, cy = chips[j]
        reg = region(i, 2 * cx + cy, half)
        return remote(reg, reg, fsend_s.at[3 * i + j], frecv_s.at[3 * i + j], sibling)

    def start():
        for i in range(n):
            for j in range(3):
                ici(i, j, q).start()

    def finish():
        for i in range(n):
            for j, (cx, cy) in enumerate(chips):
                ici(i, j, 2 * cx + cy).wait_recv()
                d2d(i, j, c).start()
        for i in range(n):
            for j in range(3):
                d2d(i, j, 1 - c).wait_recv()
        for i in range(n):
            for j in range(3):
                ici(i, j, q).wait_send()
                d2d(i, j, c).wait_send()

    return start, finish


def _carry_sibling_exchange(grads, axes, shapes):
    n = len(grads)

    def steps(ins, theirs, sems):
        send_s, recv_s = sems
        x, y, c = _mesh_pos()

        def copies():
            return [pltpu.make_async_remote_copy(
                src_ref=_shard_region(ins[i], axes[i], shapes[i], qq, 1 - c), dst_ref=theirs[i].at[qq],
                send_sem=send_s.at[N_CHIPS * i + qq], recv_sem=recv_s.at[N_CHIPS * i + qq],
                device_id=(x, y, 1 - c), device_id_type=MESH) for i in range(n) for qq in range(N_CHIPS)]

        def start():
            for cp in copies():
                cp.start()

        def finish():
            for cp in copies():
                cp.wait()

        return start, finish

    stacked = tuple(jax.ShapeDtypeStruct((N_CHIPS, R // 2, C), F32) for (R, C) in shapes)
    return _Carried(tuple(grads), stacked, {}, (pltpu.SemaphoreType.DMA((N_CHIPS * n,)),) * 2, steps)


def _carry_chip_scatter(parts):
    n = len(parts)

    def steps(ins, outs, sems):
        send_s, recv_s = sems
        x, y, c = _mesh_pos()
        chips = [(1 - x, y), (x, 1 - y), (1 - x, 1 - y)]

        def copies():
            return [pltpu.make_async_remote_copy(
                src_ref=ins[i].at[2 * cx + cy], dst_ref=outs[i].at[j],
                send_sem=send_s.at[3 * i + j], recv_sem=recv_s.at[3 * i + j],
                device_id=(cx, cy, c), device_id_type=MESH) for i in range(n) for j, (cx, cy) in enumerate(chips)]

        def start():
            for cp in copies():
                cp.start()

        def finish():
            for cp in copies():
                cp.wait()

        return start, finish

    return _Carried(tuple(parts), tuple(jax.ShapeDtypeStruct((3,) + p.shape[1:], p.dtype) for p in parts), {},
                    (pltpu.SemaphoreType.DMA((3 * n,)),) * 2, steps)


def _carry_sibling_share(fulls):
    n = len(fulls)

    def steps(ins, outs, sems):
        send_s, recv_s = sems
        x, y, c = _mesh_pos()

        def copy(i, half):
            rh = fulls[i].shape[0] // 2
            rows = outs[i].at[pl.ds(half * rh, rh), :]
            return pltpu.make_async_remote_copy(src_ref=rows, dst_ref=rows, send_sem=send_s.at[i], recv_sem=recv_s.at[i],
                                                device_id=(x, y, 1 - c), device_id_type=MESH)

        def start():
            for i in range(n):
                copy(i, c).start()

        def finish():
            for i in range(n):
                copy(i, 1 - c).wait_recv()
                copy(i, c).wait_send()

        return start, finish

    return _Carried(tuple(fulls), tuple(jax.ShapeDtypeStruct(f.shape, f.dtype) for f in fulls),
                    {i: i for i in range(n)}, (pltpu.SemaphoreType.DMA((n,)),) * 2, steps)


def _row_block(rows, target=256):
    return target if rows % target == 0 else rows


BF16_TILE_ROWS = 16


def _common_steps(rows, most=8):
    ns = most
    while ns > 1 and any(r % (ns * BF16_TILE_ROWS) for r in rows):
        ns //= 2
    return ns


def _cast_into_full(ws, axes, pos, name):
    n = len(ws)
    ns = _common_steps([w.shape[0] for w in ws])
    in_specs, out_specs, out_shape = [], [], []
    for w, axis in zip(ws, axes):
        R, C = w.shape
        in_specs.append(pl.BlockSpec((R // ns, C), lambda i, s: (i, 0)))
        if axis == 1:
            out_shape.append(jax.ShapeDtypeStruct((R, N_CHIPS * C), BF16))
            out_specs.append(pl.BlockSpec((R // ns, C), lambda i, s: (i, s[0])))
        else:
            out_shape.append(jax.ShapeDtypeStruct((N_CHIPS * R, C), BF16))
            out_specs.append(pl.BlockSpec((R // ns, C), lambda i, s: (s[0] * ns + i, 0)))

    def body(s_ref, *refs):
        for k in range(n):
            refs[n + k][...] = refs[k][...].astype(BF16)

    return pl.pallas_call(
        body, out_shape=out_shape,
        grid_spec=pltpu.PrefetchScalarGridSpec(num_scalar_prefetch=1, grid=(ns,), in_specs=in_specs, out_specs=out_specs),
        compiler_params=_cparams(), name=name)(pos, *ws)


def _presum(gs, theirs, axes, shapes, pos, name):
    n = len(gs)
    ns = _common_steps([R // 2 for R, _ in shapes])
    in_specs, t_specs, out_shape = [], [], []
    for (R, C), axis in zip(shapes, axes):
        rb = R // 2 // ns
        if axis == 1:
            in_specs.append(pl.BlockSpec((rb, C), lambda k, i, s: (s[1] * ns + i, k)))
        else:
            in_specs.append(pl.BlockSpec((rb, C), lambda k, i, s: (k * 2 * ns + s[1] * ns + i, 0)))
        t_specs.append(pl.BlockSpec((None, rb, C), lambda k, i, s: (k, i, 0)))
        out_shape.append(jax.ShapeDtypeStruct((N_CHIPS, R // 2, C), BF16))

    def body(s_ref, *refs):
        for k in range(n):
            refs[2 * n + k][...] = (refs[k][...] + refs[n + k][...]).astype(BF16)

    return pl.pallas_call(
        body, out_shape=out_shape,
        grid_spec=pltpu.PrefetchScalarGridSpec(num_scalar_prefetch=1, grid=(N_CHIPS, ns), in_specs=in_specs + t_specs,
                                               out_specs=t_specs),
        compiler_params=_cparams(), name=name)(pos, *gs, *theirs)


def _sum_chips(owns, recvs, pos, name):
    n = len(owns)
    ns = _common_steps([o.shape[1] for o in owns])
    o_specs, r_specs, out_specs, out_shape = [], [], [], []
    for o in owns:
        _, Rh, C = o.shape
        rb = Rh // ns
        o_specs.append(pl.BlockSpec((None, rb, C), lambda i, s: (s[0], i, 0)))
        r_specs.append(pl.BlockSpec((3, rb, C), lambda i, s: (0, i, 0)))
        out_specs.append(pl.BlockSpec((rb, C), lambda i, s: (s[1] * ns + i, 0)))
        out_shape.append(jax.ShapeDtypeStruct((2 * Rh, C), F32))

    def body(s_ref, *refs):
        for k in range(n):
            acc = refs[k][...].astype(F32)
            for j in range(3):
                acc = acc + refs[n + k][j].astype(F32)
            refs[2 * n + k][...] = acc

    return pl.pallas_call(
        body, out_shape=out_shape,
        grid_spec=pltpu.PrefetchScalarGridSpec(num_scalar_prefetch=1, grid=(ns,), in_specs=o_specs + r_specs,
                                               out_specs=out_specs),
        compiler_params=_cparams(), name=name)(pos, *owns, *recvs)


def _sum_devices(parts, name):
    K, R, C = parts.shape

    def body(p_ref, o_ref):
        acc = p_ref[0]
        for k in range(1, K):
            acc = acc + p_ref[k]
        o_ref[...] = acc

    return pl.pallas_call(body, out_shape=jax.ShapeDtypeStruct((R, C), F32), name=name)(parts)


def _adamw_math(w, g, m, v):
    nm = ADAM_B1 * m + (1.0 - ADAM_B1) * g
    nv = ADAM_B2 * v + (1.0 - ADAM_B2) * (g * g)
    m_hat = nm / (1.0 - ADAM_B1 ** ADAM_STEP)
    v_hat = nv / (1.0 - ADAM_B2 ** ADAM_STEP)
    return -ADAM_LR * (m_hat / (jnp.sqrt(v_hat) + ADAM_EPS) + ADAM_WD * w), nm, nv


def _adamw_many(ws, gs, ms, vs):
    n = len(ws)

    def body(*refs):
        for k in range(n):
            w, g, m, v = (refs[j * n + k][...] for j in range(4))
            for j, val in enumerate(_adamw_math(w, g, m, v)):
                refs[(4 + j) * n + k][...] = val

    shapes = [jax.ShapeDtypeStruct(w.shape, F32) for w in ws]
    res = pl.pallas_call(body, out_shape=shapes * 3, compiler_params=_cparams(), name="adamw_small")(*ws, *gs, *ms, *vs)
    return res[:n], res[n:2 * n], res[2 * n:]


def _adamw(ws, gs, ms, vs, name):
    n = len(ws)
    ns = _common_steps([w.shape[0] for w in ws])
    specs = [pl.BlockSpec((w.shape[0] // ns, w.shape[1]), lambda i: (i, 0)) for w in ws]

    def body(*refs):
        for k in range(n):
            w, g, m, v = (refs[j * n + k][...] for j in range(4))
            for j, val in enumerate(_adamw_math(w, g, m, v)):
                refs[(4 + j) * n + k][...] = val

    shapes = [jax.ShapeDtypeStruct(w.shape, F32) for w in ws]
    res = pl.pallas_call(body, out_shape=shapes * 3, grid=(ns,), in_specs=specs * 4, out_specs=specs * 3,
                         compiler_params=_cparams(), name=name)(*ws, *gs, *ms, *vs)
    return res[:n], res[n:2 * n], res[2 * n:]


def _silu(v):
    return v * jax.nn.sigmoid(v)


def _ada_fwd(c_all, w_sh, b_sh):
    S, D = c_all.shape
    Ca = w_sh.shape[1]
    cb = 512 if Ca % 512 == 0 else Ca

    def body(c_ref, w_ref, b_ref, o_ref):
        act = _silu(c_ref[...]).astype(BF16)
        o_ref[...] = _dot(act, w_ref[...].astype(BF16)) + b_ref[...]

    return pl.pallas_call(
        body, out_shape=jax.ShapeDtypeStruct((S, Ca), F32), grid=(Ca // cb,),
        in_specs=[pl.BlockSpec((S, D), lambda j: (0, 0)), pl.BlockSpec((D, cb), lambda j: (0, j)),
                  pl.BlockSpec((1, cb), lambda j: (0, j))],
        out_specs=pl.BlockSpec((S, cb), lambda j: (0, j)), name="ada_fwd")(c_all, w_sh, b_sh)


def _ada_bwd(c_all, dmod_sh, dmod_all):
    S, D = c_all.shape
    Ca = dmod_sh.shape[1]
    C6 = dmod_all.shape[1]

    def body(c_ref, ds_ref, da_ref, gw_ref, gb_ref):
        act = _silu(c_ref[...]).astype(BF16)
        gw_ref[...] = _dot_tn(act, ds_ref[...].astype(BF16))
        gb_ref[...] = jnp.sum(da_ref[...], axis=0, keepdims=True)

    return pl.pallas_call(
        body, out_shape=[jax.ShapeDtypeStruct((D, Ca), F32), jax.ShapeDtypeStruct((1, C6), F32)],
        compiler_params=_cparams(), name="ada_bwd")(c_all, dmod_sh, dmod_all)


def _rms_fwd(xv):
    r = lax.rsqrt(jnp.mean(xv * xv, axis=-1, keepdims=True) + RMS_EPS)
    return xv * r, r


def _rms_bwd(dxh, xh, r):
    return r * (dxh - xh * jnp.mean(dxh * xh, axis=-1, keepdims=True))


def _const_spec(shape):
    nd = len(shape)
    return pl.BlockSpec(shape, lambda *_: (0,) * nd)


def _seq_spec(D, bps, rev_blocks=None):
    if rev_blocks is None:
        return pl.BlockSpec((None, 1, D), lambda i: (i // bps, 0, 0))
    return pl.BlockSpec((None, 1, D), lambda i: ((rev_blocks - 1 - i) // bps, 0, 0))


def _inproj_fwd(x2, sh1, sc1, n1g, w_in, carried):
    N, D = x2.shape
    IN = w_in.shape[1]
    Bl = sh1.shape[0]
    TB = _row_block(N // Bl, 512)
    bps = (N // Bl) // TB

    def body(x_ref, sh_ref, sc_ref, g_ref, w_ref, p_ref):
        xh, _ = _rms_fwd(x_ref[...])
        h = (xh * g_ref[...]) * (1.0 + sc_ref[...]) + sh_ref[...]
        p_ref[...] = _dot(h.astype(BF16), w_ref[...]).astype(BF16)

    (p2,), extra = _call_carrying(
        body, carried, (x2, sh1, sc1, n1g, w_in), out_shape=[jax.ShapeDtypeStruct((N, IN), BF16)], grid=(N // TB,),
        in_specs=[pl.BlockSpec((TB, D), lambda i: (i, 0)), _seq_spec(D, bps), _seq_spec(D, bps),
                  _const_spec((1, D)), _const_spec((D, IN))],
        out_specs=[pl.BlockSpec((TB, IN), lambda i: (i, 0))],
        compiler_params=_cparams(dimension_semantics=("arbitrary",)), name="inproj_fwd")
    return p2, extra


def _s5_dims(Bl, L, Ds):
    G = Ds // SSM_GROUP
    GP = G * SSM_STATE
    NP = GP // LANES
    T = min(64, L // 2)
    nb = 2 if (Ds // 2) % LANES == 0 else 1
    return G, GP, NP, T, nb


def _s5_disc_math(lr, li, ldt, bt_r, bt_i):
    dt = jnp.exp(ldt)
    er = jnp.exp(lr * dt)
    lbr = er * jnp.cos(li * dt)
    lbi = er * jnp.sin(li * dt)
    den = lr * lr + li * li
    fr = ((lbr - 1.0) * lr + lbi * li) / den
    fi = (lbi * lr - (lbr - 1.0) * li) / den
    return lbr, lbi, fr[None] * bt_r - fi[None] * bt_i, fr[None] * bt_i + fi[None] * bt_r


def _s5_disc(lr, li, ldt, bt_r, bt_i):
    def body(lr_ref, li_ref, ldt_ref, br_ref, bi_ref, o0, o1, o2, o3):
        res = _s5_disc_math(lr_ref[...], li_ref[...], ldt_ref[...], br_ref[...], bi_ref[...])
        for o, v in zip((o0, o1, o2, o3), res):
            o[...] = v

    S = jax.ShapeDtypeStruct
    return pl.pallas_call(body, out_shape=[S(lr.shape, F32)] * 2 + [S(bt_r.shape, F32)] * 2, name="s5_disc")(lr, li, ldt, bt_r, bt_i)


def _s5_disc_bwd(lr, li, ldt, bt_r, bt_i, dlbr, dlbi, dbbr, dbbi):
    def body(lr_ref, li_ref, ldt_ref, br_ref, bi_ref, g0, g1, g2, g3, o0, o1, o2, o3, o4):
        _, vjp = jax.vjp(_s5_disc_math, lr_ref[...], li_ref[...], ldt_ref[...], br_ref[...], bi_ref[...])
        res = vjp((g0[...], g1[...], g2[...], g3[...]))
        for o, v in zip((o0, o1, o2, o3, o4), res):
            o[...] = v

    S = jax.ShapeDtypeStruct
    return pl.pallas_call(body, out_shape=[S(lr.shape, F32)] * 2 + [S(ldt.shape, F32)] + [S(bt_r.shape, F32)] * 2,
                          name="s5_disc_bwd")(lr, li, ldt, bt_r, bt_i, dlbr, dlbi, dbbr, dbbi)


S5_SCAN_PANELS = 4
S5_SCAN_STEPS = 4


def _panel_scan(src_r, src_i, dst_r, dst_i, lr_ref, li_ref, car_r, car_i, NP, Bl, T, TP, adjoint):
    BT = Bl * TP
    PG = min(S5_SCAN_PANELS, NP)
    CH = S5_SCAN_STEPS
    for k0 in range(0, NP, PG):
        ks = list(range(k0, k0 + PG))
        lr = [jnp.broadcast_to(lr_ref[pl.ds(k, 1), :], (Bl, LANES)) for k in ks]
        li = [jnp.broadcast_to(li_ref[pl.ds(k, 1), :], (Bl, LANES)) for k in ks]

        def trip(cc, carry):
            ts = [(T - 1 - (cc * CH + s)) if adjoint else (cc * CH + s) for s in range(CH)]
            idx = [[pl.ds(k * BT + t, Bl, stride=TP) for t in ts] for k in ks]
            loaded = [[(src_r[ix, :], src_i[ix, :]) for ix in idx[j]] for j in range(PG)]
            results, new_carry = [], []
            for j in range(PG):
                ar, ai = carry[j]
                res = []
                for s in range(CH):
                    br, bi = loaded[j][s]
                    if adjoint:
                        ar, ai = br + lr[j] * ar + li[j] * ai, bi + lr[j] * ai - li[j] * ar
                    else:
                        ar, ai = lr[j] * ar - li[j] * ai + br, lr[j] * ai + li[j] * ar + bi
                    res.append((ar, ai))
                results.append(res)
                new_carry.append((ar, ai))
            for j in range(PG):
                for s in range(CH):
                    dst_r[idx[j][s], :] = results[j][s][0]
                    dst_i[idx[j][s], :] = results[j][s][1]
            return tuple(new_carry)

        init = tuple((car_r[pl.ds(k * SUBLANES, Bl), :], car_i[pl.ds(k * SUBLANES, Bl), :]) for k in ks)
        fin = lax.fori_loop(0, T // CH, trip, init, unroll=2)
        for j, k in enumerate(ks):
            car_r[pl.ds(k * SUBLANES, Bl), :] = fin[j][0]
            car_i[pl.ds(k * SUBLANES, Bl), :] = fin[j][1]


def _s5_fwd(p3, Bm_r, Bm_i, Cm_r, Cm_i, lam_r, lam_i, dsk, carried):
    Bl, L, _ = p3.shape
    Ds = dsk.shape[1]
    G, GP, NP, T, nb = _s5_dims(Bl, L, Ds)
    nT = L // T
    TP = T + S5_ROW_PAD
    BT = Bl * TP
    dsb, gpb, npb = Ds // nb, GP // nb, NP // nb

    def body(u_ref, br_ref, bi_ref, cr_ref, ci_ref, lr_ref, li_ref, dsk_ref, sr_ref, si_ref, sb_r, sb_i, y_ref,
             car_r, car_i, upad, ypad, bu_r, bu_i):
        i = pl.program_id(0)

        @pl.when(i == 0)
        def _():
            car_r[...] = jnp.zeros_like(car_r)
            car_i[...] = jnp.zeros_like(car_i)
            upad[...] = jnp.zeros_like(upad)

        zpad = jnp.zeros((S5_ROW_PAD, LANES), F32)
        for k in range(NP):
            for b in range(Bl):
                sr_ref[pl.ds(k * BT + b * TP + T, S5_ROW_PAD), :] = zpad
                si_ref[pl.ds(k * BT + b * TP + T, S5_ROW_PAD), :] = zpad
        for b in range(Bl):
            upad[pl.ds(b * TP, T), :] = u_ref[b].astype(F32)
        u = upad[...]
        ub = u.astype(BF16)
        for blk in range(nb):
            ubb = ub[:, blk * dsb:(blk + 1) * dsb]
            for bu_ref, b_ref in ((bu_r, br_ref), (bu_i, bi_ref)):
                res = _dot(ubb, b_ref[blk])
                for kk in range(npb):
                    k = blk * npb + kk
                    bu_ref[pl.ds(k * BT, BT), :] = res[:, kk * LANES:(kk + 1) * LANES]

        _panel_scan(bu_r, bu_i, sr_ref, si_ref, lr_ref, li_ref, car_r, car_i, NP, Bl, T, TP, adjoint=False)
        sb_r[...] = car_r[...]
        sb_i[...] = car_i[...]

        for blk in range(nb):
            s_r = jnp.concatenate([sr_ref[pl.ds((blk * npb + kk) * BT, BT), :] for kk in range(npb)], axis=1).astype(BF16)
            s_i = jnp.concatenate([si_ref[pl.ds((blk * npb + kk) * BT, BT), :] for kk in range(npb)], axis=1).astype(BF16)
            cols = slice(blk * dsb, (blk + 1) * dsb)
            ypad[:, cols] = _dot(s_r, cr_ref[blk]) + _dot(s_i, ci_ref[blk]) + dsk_ref[:, cols] * u[:, cols]
        for b in range(Bl):
            y_ref[b] = ypad[pl.ds(b * TP, T), :]

    S = jax.ShapeDtypeStruct
    state = S((nT, NP * BT, LANES), F32)
    bound = S((nT, NP * SUBLANES, LANES), F32)
    sspec = pl.BlockSpec((None, NP * BT, LANES), lambda i: (i, 0, 0))
    bspec = pl.BlockSpec((None, NP * SUBLANES, LANES), lambda i: (i, 0, 0))
    return _call_carrying(
        body, carried, (p3, Bm_r, Bm_i, Cm_r, Cm_i, lam_r, lam_i, dsk),
        out_shape=[state, state, bound, bound, S((Bl, L, Ds), F32)], grid=(nT,),
        in_specs=[pl.BlockSpec((Bl, T, Ds), lambda i: (0, i, 0)),
                  _const_spec((nb, dsb, gpb)), _const_spec((nb, dsb, gpb)),
                  _const_spec((nb, gpb, dsb)), _const_spec((nb, gpb, dsb)),
                  _const_spec((NP, LANES)), _const_spec((NP, LANES)), _const_spec((1, Ds))],
        out_specs=[sspec, sspec, bspec, bspec, pl.BlockSpec((Bl, T, Ds), lambda i: (0, i, 0))],
        scratch_shapes=[pltpu.VMEM((NP * SUBLANES, LANES), F32)] * 2 + [pltpu.VMEM((BT, Ds), F32)] * 2
        + [pltpu.VMEM((NP * BT, LANES), F32)] * 2,
        compiler_params=_cparams(dimension_semantics=("arbitrary",)), name="s5_fwd")


def _s5_bwd(dy3, p3, dp3, Sr, Si, Sb_r, Sb_i, Bm_r, Bm_i, Cm_r, Cm_i, lam_r, lam_i, dsk, carried):
    Bl, L, Ds = dy3.shape
    G, GP, NP, T, nb = _s5_dims(Bl, L, Ds)
    nT = L // T
    TP = T + S5_ROW_PAD
    BT = Bl * TP
    dsb, gpb, npb = Ds // nb, GP // nb, NP // nb

    def body(dy_ref, u_ref, dp_ref, sr_ref, si_ref, sbr_ref, sbi_ref, br_ref, bi_ref, cr_ref, ci_ref, lr_ref, li_ref, dsk_ref,
             du_ref, dbr_ref, dbi_ref, dcr_ref, dci_ref, dlr_ref, dli_ref, ddsk_ref,
             a_r, a_i, car_r, car_i, acc_r, acc_i, dypad, upad, dupad):
        i = pl.program_id(0)

        @pl.when(i == 0)
        def _():
            for ref in (car_r, car_i, acc_r, acc_i, dbr_ref, dbi_ref, dcr_ref, dci_ref, ddsk_ref, dypad, upad):
                ref[...] = jnp.zeros_like(ref)

        for b in range(Bl):
            dypad[pl.ds(b * TP, T), :] = dy_ref[b]
            upad[pl.ds(b * TP, T), :] = u_ref[b].astype(F32)
        dy = dypad[...]
        dyb = dy.astype(BF16)
        u = upad[...]
        ub = u.astype(BF16)
        for blk in range(nb):
            dyb_b = dyb[:, blk * dsb:(blk + 1) * dsb]
            for q_ref, c_ref in ((a_r, cr_ref), (a_i, ci_ref)):
                res = _dot_nt(dyb_b, c_ref[blk])
                for kk in range(npb):
                    q_ref[pl.ds((blk * npb + kk) * BT, BT), :] = res[:, kk * LANES:(kk + 1) * LANES]

        _panel_scan(a_r, a_i, a_r, a_i, lr_ref, li_ref, car_r, car_i, NP, Bl, T, TP, adjoint=True)

        first_block = (i == nT - 1)
        for k in range(NP):
            rows = pl.ds(k * BT, BT)
            av_r, av_i = a_r[rows, :], a_i[rows, :]
            sp_r = pltpu.roll(sr_ref[rows, :], 1, 0)
            sp_i = pltpu.roll(si_ref[rows, :], 1, 0)
            acc = pl.ds(k * SUBLANES, SUBLANES)
            acc_r[acc, :] += jnp.sum((av_r * sp_r + av_i * sp_i).reshape(BT // SUBLANES, SUBLANES, LANES), axis=0)
            acc_i[acc, :] += jnp.sum((av_i * sp_r - av_r * sp_i).reshape(BT // SUBLANES, SUBLANES, LANES), axis=0)
            t0 = pl.ds(k * BT, Bl, stride=TP)
            a0_r, a0_i = a_r[t0, :], a_i[t0, :]
            brow = pl.ds(k * SUBLANES, Bl)
            sb_pr = jnp.where(first_block, 0.0, sbr_ref[brow, :])
            sb_pi = jnp.where(first_block, 0.0, sbi_ref[brow, :])
            acc_r[brow, :] += a0_r * sb_pr + a0_i * sb_pi
            acc_i[brow, :] += a0_i * sb_pr - a0_r * sb_pi

        ddsk_ref[...] += jnp.sum(dy * u, axis=0, keepdims=True)
        for blk in range(nb):
            cols = slice(blk * dsb, (blk + 1) * dsb)
            rows = [pl.ds((blk * npb + kk) * BT, BT) for kk in range(npb)]
            av_r = jnp.concatenate([a_r[r, :] for r in rows], axis=1).astype(BF16)
            av_i = jnp.concatenate([a_i[r, :] for r in rows], axis=1).astype(BF16)
            dupad[:, cols] = _dot_nt(av_r, br_ref[blk]) + _dot_nt(av_i, bi_ref[blk]) + dy[:, cols] * dsk_ref[:, cols]
            dbr_ref[blk] += _dot_tn(ub[:, cols], av_r)
            dbi_ref[blk] += _dot_tn(ub[:, cols], av_i)
            sv_r = jnp.concatenate([sr_ref[r, :] for r in rows], axis=1).astype(BF16)
            sv_i = jnp.concatenate([si_ref[r, :] for r in rows], axis=1).astype(BF16)
            dcr_ref[blk] += _dot_tn(sv_r, dyb[:, cols])
            dci_ref[blk] += _dot_tn(sv_i, dyb[:, cols])
        for b in range(Bl):
            du_ref[b] = dupad[pl.ds(b * TP, T), :].astype(BF16)

        @pl.when(i == nT - 1)
        def _():
            for k in range(NP):
                dlr_ref[pl.ds(k, 1), :] = jnp.sum(acc_r[pl.ds(k * SUBLANES, SUBLANES), :], axis=0, keepdims=True)
                dli_ref[pl.ds(k, 1), :] = jnp.sum(acc_i[pl.ds(k * SUBLANES, SUBLANES), :], axis=0, keepdims=True)

    S = jax.ShapeDtypeStruct
    rev = lambda i: nT - 1 - i
    sspec = pl.BlockSpec((None, NP * BT, LANES), lambda i: (rev(i), 0, 0))
    bspec = pl.BlockSpec((None, NP * SUBLANES, LANES), lambda i: (jnp.maximum(rev(i) - 1, 0), 0, 0))
    tspec = pl.BlockSpec((Bl, T, Ds), lambda i: (0, rev(i), 0))
    return _call_carrying(
        body, carried, (dy3, p3, dp3, Sr, Si, Sb_r, Sb_i, Bm_r, Bm_i, Cm_r, Cm_i, lam_r, lam_i, dsk),
        out_shape=[S(dp3.shape, dp3.dtype), S((nb, dsb, gpb), F32), S((nb, dsb, gpb), F32),
                   S((nb, gpb, dsb), F32), S((nb, gpb, dsb), F32), S((NP, LANES), F32), S((NP, LANES), F32), S((1, Ds), F32)],
        grid=(nT,),
        in_specs=[tspec, tspec, ANY, sspec, sspec, bspec, bspec,
                  _const_spec((nb, dsb, gpb)), _const_spec((nb, dsb, gpb)),
                  _const_spec((nb, gpb, dsb)), _const_spec((nb, gpb, dsb)),
                  _const_spec((NP, LANES)), _const_spec((NP, LANES)), _const_spec((1, Ds))],
        out_specs=[tspec, _const_spec((nb, dsb, gpb)), _const_spec((nb, dsb, gpb)),
                   _const_spec((nb, gpb, dsb)), _const_spec((nb, gpb, dsb)),
                   _const_spec((NP, LANES)), _const_spec((NP, LANES)), _const_spec((1, Ds))],
        aliases={2: 0},
        scratch_shapes=[pltpu.VMEM((NP * BT, LANES), F32)] * 2 + [pltpu.VMEM((NP * SUBLANES, LANES), F32)] * 4
        + [pltpu.VMEM((BT, Ds), F32)] * 3,
        compiler_params=_cparams(dimension_semantics=("arbitrary",)), name="s5_bwd")


def _mix_values(ylin, cb, cc, cx, gs, gc, halo_v, wglu, bglu, cw, wps, wpc, wout):
    yg, gelu_vjp = jax.vjp(jax.nn.gelu, ylin)
    sz = jax.nn.sigmoid(_dot(yg.astype(BF16), wglu) + bglu)
    ys = yg * sz
    v = cc * cx
    rows = lax.broadcasted_iota(jnp.int32, v.shape, 0)
    h6 = halo_v[HALO_ROWS - 2:HALO_ROWS - 1, :]
    h7 = halo_v[HALO_ROWS - 1:HALO_ROWS, :]
    v1 = jnp.where(rows == 0, h7, pltpu.roll(v, 1, 0))
    v2 = jnp.where(rows == 0, h6, jnp.where(rows == 1, h7, pltpu.roll(v, 2, 0)))
    cv = cw[0:1, :] * v2 + cw[1:2, :] * v1 + cw[2:3, :] * v
    yc = cb * cv
    ps = _dot(ys.astype(BF16), wps)
    pc = _dot(yc.astype(BF16), wpc)
    sgs = jax.nn.sigmoid(gs)
    sgc = jax.nn.sigmoid(gc)
    merged = sgs * ps + sgc * pc
    mo = _dot(merged.astype(BF16), wout)
    return dict(yg=yg, gelu_vjp=gelu_vjp, sz=sz, ys=ys, v=v, v1=v1, v2=v2, cv=cv, yc=yc, ps=ps, pc=pc,
                sgs=sgs, sgc=sgc, merged=merged, mo=mo)


def _mix_in_specs(TB, D, Ds, Dc, bps, blk):
    hb = TB // HALO_ROWS
    halo = lambda col: pl.BlockSpec((HALO_ROWS, Dc), lambda i: (jnp.maximum(blk(i) * hb - 1, 0), col))
    return [pl.BlockSpec((TB, Dc), lambda i: (blk(i), 1)), pl.BlockSpec((TB, Dc), lambda i: (blk(i), 2)),
            pl.BlockSpec((TB, Dc), lambda i: (blk(i), 3)), pl.BlockSpec((TB, D), lambda i: (blk(i), 2)),
            pl.BlockSpec((TB, D), lambda i: (blk(i), 3)), halo(2), halo(3),
            pl.BlockSpec((TB, Ds), lambda i: (blk(i), 0))]


def _mix_fwd(p2, ylin2, x2, g1, wglu, bglu, cw, wps, wpc, wout):
    N, D = x2.shape
    Ds = ylin2.shape[1]
    Dc = Ds
    Bl = g1.shape[0]
    TB = _row_block(N // Bl, 512)
    bps = (N // Bl) // TB

    def body(cb_ref, cc_ref, cx_ref, gs_ref, gc_ref, hcc_ref, hcx_ref, yl_ref, x_ref, g1_ref,
             wglu_ref, bglu_ref, cw_ref, wps_ref, wpc_ref, wout_ref, x1_ref):
        i = pl.program_id(0)
        f32 = lambda ref: ref[...].astype(F32)
        halo_v = jnp.where(i % bps == 0, 0.0, f32(hcc_ref) * f32(hcx_ref))
        f = _mix_values(yl_ref[...], f32(cb_ref), f32(cc_ref), f32(cx_ref), f32(gs_ref), f32(gc_ref), halo_v,
                        wglu_ref[...], bglu_ref[...], cw_ref[...], wps_ref[...], wpc_ref[...], wout_ref[...])
        x1_ref[...] = x_ref[...] + g1_ref[...] * f["mo"]

    return pl.pallas_call(
        body, out_shape=jax.ShapeDtypeStruct((N, D), F32), grid=(N // TB,),
        in_specs=_mix_in_specs(TB, D, Ds, Dc, bps, lambda i: i) + [
            pl.BlockSpec((TB, D), lambda i: (i, 0)), _seq_spec(D, bps),
            _const_spec((Ds, Ds)), _const_spec((1, Ds)), _const_spec((SUBLANES, Dc)),
            _const_spec((Ds, D)), _const_spec((Dc, D)), _const_spec((D, D))],
        out_specs=pl.BlockSpec((TB, D), lambda i: (i, 0)),
        compiler_params=_cparams(), name="mix_fwd",
    )(p2, p2, p2, p2, p2, p2, p2, ylin2, x2, g1, wglu, bglu, cw, wps, wpc, wout)


def _mix_bwd(p2, ylin2, dx1, g1, wglu, bglu, cw, wps, wpc, wout, carried):
    N, D = dx1.shape
    Ds = ylin2.shape[1]
    Dc = Ds
    IN = p2.shape[1]
    Bl = g1.shape[0]
    TB = _row_block(N // Bl)
    bps = (N // Bl) // TB
    nblk = N // TB
    rev = lambda i: nblk - 1 - i

    def body(cb_ref, cc_ref, cx_ref, gs_ref, gc_ref, hcc_ref, hcx_ref, yl_ref, dx1_ref, g1_ref,
             wglu_ref, bglu_ref, cw_ref, wps_ref, wpc_ref, wout_ref,
             dyl_ref, dp_ref, gwout_ref, gwps_ref, gwpc_ref, gwglu_ref, gbglu_ref, gcw_ref, dg1_ref, nxt):
        i = pl.program_id(0)
        blk = rev(i)

        @pl.when(i == 0)
        def _():
            for ref in (gwout_ref, gwps_ref, gwpc_ref, gwglu_ref, gbglu_ref, gcw_ref):
                ref[...] = jnp.zeros_like(ref)

        @pl.when(i % bps == 0)
        def _():
            nxt[...] = jnp.zeros_like(nxt)
            dg1_ref[...] = jnp.zeros_like(dg1_ref)

        f32 = lambda ref: ref[...].astype(F32)
        cb, cc, cx = f32(cb_ref), f32(cc_ref), f32(cx_ref)
        halo_v = jnp.where(blk % bps == 0, 0.0, f32(hcc_ref) * f32(hcx_ref))
        wglu, wps, wpc, wout, cw = wglu_ref[...], wps_ref[...], wpc_ref[...], wout_ref[...], cw_ref[...]
        f = _mix_values(yl_ref[...], cb, cc, cx, f32(gs_ref), f32(gc_ref), halo_v, wglu, bglu_ref[...], cw, wps, wpc, wout)

        dx1v = dx1_ref[...]
        dg1_ref[...] += jnp.sum(dx1v * f["mo"], axis=0, keepdims=True)
        dmo = (g1_ref[...] * dx1v).astype(BF16)
        gwout_ref[...] += _dot_tn(f["merged"].astype(BF16), dmo)
        dmerged = _dot_nt(dmo, wout)
        dps = dmerged * f["sgs"]
        dpc = dmerged * f["sgc"]
        dgs = dmerged * f["ps"] * f["sgs"] * (1.0 - f["sgs"])
        dgc = dmerged * f["pc"] * f["sgc"] * (1.0 - f["sgc"])
        dpsb, dpcb = dps.astype(BF16), dpc.astype(BF16)
        gwps_ref[...] += _dot_tn(f["ys"].astype(BF16), dpsb)
        gwpc_ref[...] += _dot_tn(f["yc"].astype(BF16), dpcb)
        dys = _dot_nt(dpsb, wps)
        dyc = _dot_nt(dpcb, wpc)

        dcb = dyc * f["cv"]
        dcv = dyc * cb
        rows = lax.broadcasted_iota(jnp.int32, dcv.shape, 0)
        n0, n1 = nxt[0:1, :], nxt[1:2, :]
        d1 = jnp.where(rows == TB - 1, n0, pltpu.roll(dcv, TB - 1, 0))
        d2 = jnp.where(rows == TB - 2, n0, jnp.where(rows == TB - 1, n1, pltpu.roll(dcv, TB - 2, 0)))
        dv = cw[2:3, :] * dcv + cw[1:2, :] * d1 + cw[0:1, :] * d2
        nxt[0:2, :] = dcv[0:2, :]
        gcw_ref[0:1, :] += jnp.sum(dcv * f["v2"], axis=0, keepdims=True)
        gcw_ref[1:2, :] += jnp.sum(dcv * f["v1"], axis=0, keepdims=True)
        gcw_ref[2:3, :] += jnp.sum(dcv * f["v"], axis=0, keepdims=True)

        dz = dys * f["yg"] * f["sz"] * (1.0 - f["sz"])
        dzb = dz.astype(BF16)
        gwglu_ref[...] += _dot_tn(f["yg"].astype(BF16), dzb)
        gbglu_ref[...] += jnp.sum(dz, axis=0, keepdims=True)
        dyg = dys * f["sz"] + _dot_nt(dzb, wglu)
        dyl_ref[...] = f["gelu_vjp"](dyg)[0]

        dp_ref[:, Ds:Ds + Dc] = dcb.astype(BF16)
        dp_ref[:, Ds + Dc:Ds + 2 * Dc] = (dv * cx).astype(BF16)
        dp_ref[:, Ds + 2 * Dc:Ds + 3 * Dc] = (dv * cc).astype(BF16)
        dp_ref[:, Ds + 3 * Dc:Ds + 3 * Dc + D] = dgs.astype(BF16)
        dp_ref[:, Ds + 3 * Dc + D:IN] = dgc.astype(BF16)

    S = jax.ShapeDtypeStruct
    return _call_carrying(
        body, carried, (p2, p2, p2, p2, p2, p2, p2, ylin2, dx1, g1, wglu, bglu, cw, wps, wpc, wout),
        out_shape=[S((N, Ds), F32), S((N, IN), BF16), S((D, D), F32), S((Ds, D), F32), S((Dc, D), F32),
                   S((Ds, Ds), F32), S((1, Ds), F32), S((SUBLANES, Dc), F32), S((Bl, 1, D), F32)],
        grid=(nblk,),
        in_specs=_mix_in_specs(TB, D, Ds, Dc, bps, rev) + [
            pl.BlockSpec((TB, D), lambda i: (rev(i), 0)), _seq_spec(D, bps, nblk),
            _const_spec((Ds, Ds)), _const_spec((1, Ds)), _const_spec((SUBLANES, Dc)),
            _const_spec((Ds, D)), _const_spec((Dc, D)), _const_spec((D, D))],
        out_specs=[pl.BlockSpec((TB, Ds), lambda i: (rev(i), 0)), pl.BlockSpec((TB, IN), lambda i: (rev(i), 0)),
                   _const_spec((D, D)), _const_spec((Ds, D)), _const_spec((Dc, D)), _const_spec((Ds, Ds)),
                   _const_spec((1, Ds)), _const_spec((SUBLANES, Dc)), _seq_spec(D, bps, nblk)],
        scratch_shapes=[pltpu.VMEM((SUBLANES, Dc), F32)],
        compiler_params=_cparams(dimension_semantics=("arbitrary",)), name="mix_bwd")


def _mlp_fwd_bwd(x1, tgt, sh2, sc2, g2, n2g, fg, w1, w2):
    N, D = x1.shape
    Dff = w1.shape[1]
    Bl = sh2.shape[0]
    TB = _row_block(N // Bl)
    bps = (N // Bl) // TB

    def body(x1_ref, t_ref, sh_ref, sc_ref, g2_ref, n2_ref, fg_ref, w1_ref, w2_ref,
             dx1_ref, h2_ref, da_ref, sq_ref, df_ref, loss_ref, gfg_ref, gn2_ref, dsh_ref, dsc_ref, dg2_ref):
        i = pl.program_id(0)

        @pl.when(i == 0)
        def _():
            for ref in (loss_ref, gfg_ref, gn2_ref):
                ref[...] = jnp.zeros_like(ref)

        @pl.when(i % bps == 0)
        def _():
            for ref in (dsh_ref, dsc_ref, dg2_ref):
                ref[...] = jnp.zeros_like(ref)

        x1v = x1_ref[...]
        sc, g2v, n2 = sc_ref[...], g2_ref[...], n2_ref[...]
        xh2, r2 = _rms_fwd(x1v)
        xn2 = xh2 * n2
        h2 = (xn2 * (1.0 + sc) + sh_ref[...]).astype(BF16)
        a = _dot(h2, w1_ref[...])
        ra = jnp.maximum(a, 0.0)
        sq = (ra * ra).astype(BF16)
        fv = _dot(sq, w2_ref[...])
        x2 = x1v + g2v * fv
        xh3, r3 = _rms_fwd(x2)
        err = xh3 * fg_ref[...] - t_ref[...]
        loss_ref[...] += 0.5 * jnp.sum(jnp.mean(err * err, axis=-1, keepdims=True), axis=0, keepdims=True)
        dy = err * (1.0 / D)
        gfg_ref[...] += jnp.sum(dy * xh3, axis=0, keepdims=True)
        dx2 = _rms_bwd(dy * fg_ref[...], xh3, r3)
        dg2_ref[...] += jnp.sum(dx2 * fv, axis=0, keepdims=True)
        df = (g2v * dx2).astype(BF16)
        dsq = _dot_nt(df, w2_ref[...])
        da = (2.0 * ra * dsq).astype(BF16)
        dh2 = _dot_nt(da, w1_ref[...])
        dsh_ref[...] += jnp.sum(dh2, axis=0, keepdims=True)
        dsc_ref[...] += jnp.sum(dh2 * xn2, axis=0, keepdims=True)
        dxn2 = dh2 * (1.0 + sc)
        gn2_ref[...] += jnp.sum(dxn2 * xh2, axis=0, keepdims=True)
        dx1_ref[...] = dx2 + _rms_bwd(dxn2 * n2, xh2, r2)
        h2_ref[...] = h2
        da_ref[...] = da
        sq_ref[...] = sq
        df_ref[...] = df

    S = jax.ShapeDtypeStruct
    row = lambda w: pl.BlockSpec((TB, w), lambda i: (i, 0))
    return pl.pallas_call(
        body,
        out_shape=[S((N, D), F32), S((N, D), BF16), S((N, Dff), BF16), S((N, Dff), BF16), S((N, D), BF16),
                   S((1, 1), F32), S((1, D), F32), S((1, D), F32), S((Bl, 1, D), F32), S((Bl, 1, D), F32), S((Bl, 1, D), F32)],
        grid=(N // TB,),
        in_specs=[row(D), row(D), _seq_spec(D, bps), _seq_spec(D, bps), _seq_spec(D, bps),
                  _const_spec((1, D)), _const_spec((1, D)), _const_spec((D, Dff)), _const_spec((Dff, D))],
        out_specs=[row(D), row(D), row(Dff), row(Dff), row(D), _const_spec((1, 1)), _const_spec((1, D)), _const_spec((1, D)),
                   _seq_spec(D, bps), _seq_spec(D, bps), _seq_spec(D, bps)],
        compiler_params=_cparams(dimension_semantics=("arbitrary",)), name="mlp_fwd_bwd",
    )(x1, tgt, sh2, sc2, g2, n2g, fg, w1, w2)


_NO_EXCHANGE = _Carried((), (), {}, (), lambda ins, outs, sems: ((lambda: None), (lambda: None)))


def _grad_w(a, b, name, carried=_NO_EXCHANGE):
    N, K1 = a.shape
    K2 = b.shape[1]
    t1 = 1024 if K1 % 1024 == 0 else K1
    t2 = 1024 if K2 % 1024 == 0 else K2
    tn = 2048 if N % 2048 == 0 else N

    def body(a_ref, b_ref, o_ref):
        @pl.when(pl.program_id(2) == 0)
        def _():
            o_ref[...] = jnp.zeros_like(o_ref)

        o_ref[...] += _dot_tn(a_ref[...], b_ref[...])

    (g,), extra = _call_carrying(
        body, carried, (a, b), out_shape=[jax.ShapeDtypeStruct((K1, K2), F32)], grid=(K1 // t1, K2 // t2, N // tn),
        in_specs=[pl.BlockSpec((tn, t1), lambda i, j, k: (k, i)), pl.BlockSpec((tn, t2), lambda i, j, k: (k, j))],
        out_specs=[pl.BlockSpec((t1, t2), lambda i, j, k: (i, j))],
        compiler_params=_cparams(dimension_semantics=("arbitrary", "arbitrary", "arbitrary")), name=name)
    return g, extra


def _inproj_bwd(x2, dx1, dp, sh1, sc1, n1g, w_in, carried):
    N, D = x2.shape
    IN = w_in.shape[1]
    Bl = sh1.shape[0]
    TB = _row_block(N // Bl, 512)
    bps = (N // Bl) // TB

    def body(x_ref, dx1_ref, dp_ref, sh_ref, sc_ref, g_ref, w_ref, gx_ref, h_ref, gn1_ref, dsh_ref, dsc_ref):
        i = pl.program_id(0)

        @pl.when(i == 0)
        def _():
            gn1_ref[...] = jnp.zeros_like(gn1_ref)

        @pl.when(i % bps == 0)
        def _():
            dsh_ref[...] = jnp.zeros_like(dsh_ref)
            dsc_ref[...] = jnp.zeros_like(dsc_ref)

        sc, n1 = sc_ref[...], g_ref[...]
        xh, r = _rms_fwd(x_ref[...])
        xn = xh * n1
        h_ref[...] = (xn * (1.0 + sc) + sh_ref[...]).astype(BF16)
        dh = _dot_nt(dp_ref[...], w_ref[...])
        dsh_ref[...] += jnp.sum(dh, axis=0, keepdims=True)
        dsc_ref[...] += jnp.sum(dh * xn, axis=0, keepdims=True)
        dxn = dh * (1.0 + sc)
        gn1_ref[...] += jnp.sum(dxn * xh, axis=0, keepdims=True)
        gx_ref[...] = dx1_ref[...] + _rms_bwd(dxn * n1, xh, r)

    S = jax.ShapeDtypeStruct
    row = lambda w: pl.BlockSpec((TB, w), lambda i: (i, 0))
    return _call_carrying(
        body, carried, (x2, dx1, dp, sh1, sc1, n1g, w_in),
        out_shape=[S((N, D), F32), S((N, D), BF16), S((1, D), F32), S((Bl, 1, D), F32), S((Bl, 1, D), F32)],
        grid=(N // TB,),
        in_specs=[row(D), row(D), row(IN), _seq_spec(D, bps), _seq_spec(D, bps), _const_spec((1, D)), _const_spec((D, IN))],
        out_specs=[row(D), row(D), _const_spec((1, D)), _seq_spec(D, bps), _seq_spec(D, bps)],
        compiler_params=_cparams(dimension_semantics=("arbitrary",)), name="inproj_bwd")


def _diag_blocks_from_groups(m, nb):
    G, a, b = m.shape
    gb = G // nb
    eye = jnp.eye(gb, dtype=m.dtype)
    mm = m.reshape(nb, gb, a, b)
    return (mm[:, :, :, None, :] * eye[None, :, None, :, None]).reshape(nb, gb * a, gb * b)


def _groups_from_diag_blocks(d, G, a, b):
    nb = d.shape[0]
    gb = G // nb
    dd = d.reshape(nb, gb, a, gb, b)
    idx = jnp.arange(gb)
    return dd[:, idx, :, idx, :].transpose(1, 0, 2, 3).reshape(G, a, b)


def _pad_rows(v, rows):
    return jnp.concatenate([v, jnp.zeros((rows - v.shape[0],) + v.shape[1:], v.dtype)], axis=0)


def _pack(vs):
    flat = jnp.concatenate([v.reshape(-1) for v in vs])
    n = flat.shape[0]
    tile = SUBLANES * LANES
    npad = -(-n // tile) * tile
    flat = jnp.concatenate([flat, jnp.zeros((npad - n,), flat.dtype)])
    return flat.reshape(npad // LANES, LANES)


def _unpack(packed, shapes):
    flat = packed.reshape(-1)
    out, off = [], 0
    for s in shapes:
        n = 1
        for d in s:
            n *= d
        out.append(flat[off:off + n].reshape(s))
        off += n
    return out


def kernel(x, c, norm1_g, norm2_g, w_ada, b_ada, w_in, lam_re, lam_im, log_dt, b_re, b_im, c_re, c_im, d_skip, w_glu, b_glu, conv_w, w_proj_ssm, w_proj_conv, w_out, w_ff1, w_ff2, final_g, loss_target, m_norm1_g, m_norm2_g, m_w_ada, m_b_ada, m_w_in, m_lam_re, m_lam_im, m_log_dt, m_b_re, m_b_im, m_c_re, m_c_im, m_d_skip, m_w_glu, m_b_glu, m_conv_w, m_w_proj_ssm, m_w_proj_conv, m_w_out, m_w_ff1, m_w_ff2, m_final_g, v_norm1_g, v_norm2_g, v_w_ada, v_b_ada, v_w_in, v_lam_re, v_lam_im, v_log_dt, v_b_re, v_b_im, v_c_re, v_c_im, v_d_skip, v_w_glu, v_b_glu, v_conv_w, v_w_proj_ssm, v_w_proj_conv, v_w_out, v_w_ff1, v_w_ff2, v_final_g):
    Bl, L, D = x.shape
    N = Bl * L
    Ds = Dc = D // 2
    G, H, P = Ds // SSM_GROUP, SSM_GROUP, SSM_STATE
    GP = G * P
    NP = GP // LANES
    nb = _s5_dims(Bl, L, Ds)[4]
    IN = Ds + 3 * Dc + 2 * D
    ax, ay, ac = _mesh_pos()
    q = 2 * ax + ay
    dev = 2 * q + ac

    big_names = ["w_in", "w_ff1", "w_ff2", "w_out", "w_proj_ssm", "w_proj_conv", "w_glu"]
    big_w = dict(w_in=w_in[0], w_ff1=w_ff1[0], w_ff2=w_ff2[0], w_out=w_out[0],
                 w_proj_ssm=w_proj_ssm[0], w_proj_conv=w_proj_conv[0], w_glu=w_glu[0])
    big_axis = dict(w_in=1, w_ff1=1, w_ff2=0, w_out=0, w_proj_ssm=1, w_proj_conv=1, w_glu=0)
    axes = [big_axis[k] for k in big_names]
    shard_shapes = [big_w[k].shape for k in big_names]
    pos = jnp.stack([q, ac]).astype(jnp.int32)
    own_only = dict(zip(big_names, _cast_into_full([big_w[k] for k in big_names], axes, pos, "cast_weights")))
    Dcs = conv_w.shape[2]
    first, (w_in_full,) = _allgather8(_pack([c, conv_w[0]]), "allgather_c_w_in",
                                      _carry_allgather([own_only["w_in"]], [big_axis["w_in"]], [big_w["w_in"].shape]))
    full = {"w_in": w_in_full}
    first = first.reshape(N_DEV, -1)
    c_all = first[:, :Bl * D].reshape(N_DEV * Bl, D)
    cw = first[0::2, Bl * D:Bl * D + 3 * Dcs].reshape(N_CHIPS, 3, Dcs).transpose(1, 0, 2).reshape(3, Dc)
    cw8 = _pad_rows(cw, SUBLANES)
    Ca = w_ada.shape[2]
    b_ada_sh = lax.dynamic_slice_in_dim(b_ada, q * Ca, Ca, axis=1)
    mod_part = _ada_fwd(c_all, w_ada[0], b_ada_sh)
    mod_g = _allgather8(mod_part, "allgather_mod")
    mod_all = mod_g[0::2].transpose(1, 0, 2).reshape(N_DEV * Bl, N_CHIPS * Ca)
    mod = lax.dynamic_slice_in_dim(mod_all, dev * Bl, Bl, axis=0)
    sh1, sc1, g1, sh2, sc2, g2 = [mod[:, k * D:(k + 1) * D].reshape(Bl, 1, D) for k in range(6)]

    ldt_c = log_dt[0].reshape(G, 1)
    bt_r = b_re[0].transpose(2, 0, 1)
    bt_i = b_im[0].transpose(2, 0, 1)
    lbr, lbi, bbt_r, bbt_i = _s5_disc(lam_re[0], lam_im[0], ldt_c, bt_r, bt_i)
    lam_r_p = lbr.reshape(NP, LANES)
    lam_i_p = lbi.reshape(NP, LANES)
    Bm_r = _diag_blocks_from_groups(bbt_r.transpose(1, 0, 2), nb).astype(BF16)
    Bm_i = _diag_blocks_from_groups(bbt_i.transpose(1, 0, 2), nb).astype(BF16)
    Cm_r = _diag_blocks_from_groups(c_re[0].transpose(0, 2, 1), nb).astype(BF16)
    Cm_i = _diag_blocks_from_groups(-c_im[0].transpose(0, 2, 1), nb).astype(BF16)

    x2 = x.reshape(N, D)
    mixer_w = ["w_out", "w_proj_ssm", "w_proj_conv", "w_glu"]
    mlp_w = ["w_ff1", "w_ff2"]
    layout = lambda ks: ([big_axis[k] for k in ks], [big_w[k].shape for k in ks])
    gather = lambda ks: _carry_allgather([own_only[k] for k in ks], *layout(ks))
    p2, gathered = _inproj_fwd(x2, sh1, sc1, norm1_g, full["w_in"], gather(mixer_w))
    full.update(zip(mixer_w, gathered))
    p3 = p2.reshape(Bl, L, IN)
    (Sr, Si, Sb_r, Sb_i, ylin3), gathered = _s5_fwd(p3, Bm_r, Bm_i, Cm_r, Cm_i, lam_r_p, lam_i_p, d_skip, gather(mlp_w))
    full.update(zip(mlp_w, gathered))
    ylin2 = ylin3.reshape(N, Ds)
    mix_w = (full["w_glu"], b_glu, cw8, full["w_proj_ssm"], full["w_proj_conv"], full["w_out"])
    x1 = _mix_fwd(p2, ylin2, x2, g1, *mix_w)

    (dx1, h2b, dab, sqb, dfb, loss_p, g_fg, g_n2, dsh2, dsc2, dg2) = _mlp_fwd_bwd(
        x1, loss_target.reshape(N, D), sh2, sc2, g2, norm2_g, final_g.reshape(1, D), full["w_ff1"], full["w_ff2"])
    g_full = {"w_ff1": _grad_w(h2b, dab, "grad_w_ff1")[0], "w_ff2": _grad_w(sqb, dfb, "grad_w_ff2")[0]}

    exchange = lambda ks: _carry_sibling_exchange([g_full[k] for k in ks], *layout(ks))
    presum = lambda ks, theirs: _presum([g_full[k] for k in ks], list(theirs), *layout(ks), pos, "presum_" + ks[0])
    chip_sum = lambda ks, parts, recv: _sum_chips(list(parts), list(recv), pos, "sum_" + ks[0])

    (dyl2, dp2, gw_out, gw_ps, gw_pc, gw_glu, gb_glu, gcw8, dg1), theirs_mlp = _mix_bwd(
        p2, ylin2, dx1, g1, *mix_w, exchange(mlp_w))
    g_full.update(w_out=gw_out, w_proj_ssm=gw_ps, w_proj_conv=gw_pc, w_glu=gw_glu)
    parts_mlp = presum(mlp_w, theirs_mlp)
    (dp3, dBm_r, dBm_i, dCm_r, dCm_i, dlam_r_p, dlam_i_p, g_dsk), extra = _s5_bwd(
        dyl2.reshape(Bl, L, Ds), p3, dp2.reshape(Bl, L, IN), Sr, Si, Sb_r, Sb_i, Bm_r, Bm_i, Cm_r, Cm_i, lam_r_p, lam_i_p,
        d_skip, _carry_join(_carry_chip_scatter(parts_mlp), exchange(mixer_w)))
    recv_mlp, theirs_mix = extra[:len(mlp_w)], extra[len(mlp_w):]
    halves_mlp = chip_sum(mlp_w, parts_mlp, recv_mlp)
    parts_mix = presum(mixer_w, theirs_mix)
    dp_all = dp3.reshape(N, IN)
    (grad_x2, hb, g_n1, dsh1, dsc1), _ = _inproj_bwd(x2, dx1, dp_all, sh1, sc1, norm1_g, full["w_in"], _NO_EXCHANGE)

    dbbt_r = _groups_from_diag_blocks(dBm_r, G, H, P).transpose(1, 0, 2)
    dbbt_i = _groups_from_diag_blocks(dBm_i, G, H, P).transpose(1, 0, 2)
    dc_re = _groups_from_diag_blocks(dCm_r, G, P, H).transpose(0, 2, 1)
    dc_im = -_groups_from_diag_blocks(dCm_i, G, P, H).transpose(0, 2, 1)
    dmod = jnp.concatenate([dsh1, dsc1, dg1, dsh2, dsc2, dg2], axis=-1).reshape(Bl, 6 * D)
    small = [g_n1, g_n2, g_fg, g_dsk, gb_glu, gcw8[:3], dlam_r_p, dlam_i_p, dbbt_r, dbbt_i, dc_re, dc_im]
    small_shapes = [v.shape for v in small]
    n_small = sum(int(v.size) for v in small)
    small_slots = _place_in_slot(_pack(small + [dmod]), jnp.reshape(dev, (1,)).astype(jnp.int32), "place_small")

    g_full["w_in"], extra = _grad_w(
        hb, dp_all, "grad_w_in",
        _carry_join(_carry_join(_carry_sibling_share(halves_mlp), _carry_chip_scatter(parts_mix)), _carry_allgather8(small_slots)))
    reduced = dict(zip(mlp_w, extra[:len(mlp_w)]))
    halves_mix = chip_sum(mixer_w, parts_mix, extra[len(mlp_w):len(mlp_w) + len(mixer_w)])
    gathered = extra[-1]
    theirs_in = _run_carried(exchange(["w_in"]), "rs_exchange_w_in")
    parts_in = presum(["w_in"], theirs_in)
    recv_in = _run_carried(_carry_chip_scatter(parts_in), "rs_scatter_w_in")
    halves_in = chip_sum(["w_in"], parts_in, recv_in)
    reduced.update(zip(mixer_w + ["w_in"], _run_carried(_carry_sibling_share(halves_mix + halves_in), "rs_share_rest")))

    red = _unpack(_sum_devices(gathered, "sum_small"), small_shapes)
    (r_n1, r_n2, r_fg, r_dsk, r_bglu, r_cw, r_dlr, r_dli, r_dbr, r_dbi, r_cre, r_cim) = red
    dmod_all = gathered.reshape(N_DEV, -1)[:, n_small:n_small + Bl * 6 * D].reshape(N_DEV * Bl, 6 * D)
    gw_ada, gb_ada = _ada_bwd(c_all, lax.dynamic_slice_in_dim(dmod_all, q * Ca, Ca, axis=1), dmod_all)
    g_lr, g_li, g_ldt, g_bt_r, g_bt_i = _s5_disc_bwd(lam_re[0], lam_im[0], ldt_c, bt_r, bt_i,
                                                   r_dlr.reshape(G, P), r_dli.reshape(G, P), r_dbr, r_dbi)

    grads = dict(
        norm1_g=r_n1, norm2_g=r_n2, w_ada=gw_ada, b_ada=gb_ada, lam_re=g_lr, lam_im=g_li, log_dt=g_ldt.reshape(1, G),
        b_re=g_bt_r.transpose(1, 2, 0), b_im=g_bt_i.transpose(1, 2, 0), c_re=r_cre, c_im=r_cim, d_skip=r_dsk,
        b_glu=r_bglu, conv_w=lax.dynamic_slice_in_dim(r_cw, q * Dcs, Dcs, axis=1), final_g=r_fg, **reduced)
    weights = dict(norm1_g=norm1_g, norm2_g=norm2_g, w_ada=w_ada, b_ada=b_ada, w_in=w_in, lam_re=lam_re, lam_im=lam_im,
                   log_dt=log_dt, b_re=b_re, b_im=b_im, c_re=c_re, c_im=c_im, d_skip=d_skip, w_glu=w_glu, b_glu=b_glu,
                   conv_w=conv_w, w_proj_ssm=w_proj_ssm, w_proj_conv=w_proj_conv, w_out=w_out, w_ff1=w_ff1, w_ff2=w_ff2,
                   final_g=final_g)
    m_in = dict(norm1_g=m_norm1_g, norm2_g=m_norm2_g, w_ada=m_w_ada, b_ada=m_b_ada, w_in=m_w_in, lam_re=m_lam_re,
                lam_im=m_lam_im, log_dt=m_log_dt, b_re=m_b_re, b_im=m_b_im, c_re=m_c_re, c_im=m_c_im, d_skip=m_d_skip,
                w_glu=m_w_glu, b_glu=m_b_glu, conv_w=m_conv_w, w_proj_ssm=m_w_proj_ssm, w_proj_conv=m_w_proj_conv,
                w_out=m_w_out, w_ff1=m_w_ff1, w_ff2=m_w_ff2, final_g=m_final_g)
    v_in = dict(norm1_g=v_norm1_g, norm2_g=v_norm2_g, w_ada=v_w_ada, b_ada=v_b_ada, w_in=v_w_in, lam_re=v_lam_re,
                lam_im=v_lam_im, log_dt=v_log_dt, b_re=v_b_re, b_im=v_b_im, c_re=v_c_re, c_im=v_c_im, d_skip=v_d_skip,
                w_glu=v_w_glu, b_glu=v_b_glu, conv_w=v_conv_w, w_proj_ssm=v_w_proj_ssm, w_proj_conv=v_w_proj_conv,
                w_out=v_w_out, w_ff1=v_w_ff1, w_ff2=v_w_ff2, final_g=v_final_g)
    names = list(weights)
    grads = {k: grads[k].reshape(weights[k].shape) for k in names}

    big_upd = big_names + ["w_ada"]
    delta, new_m, new_v = {}, {}, {}
    flat2 = lambda a: a.reshape(-1, a.shape[-1])
    d_, m_, v_ = _adamw([flat2(weights[k]) for k in big_upd], [flat2(grads[k]) for k in big_upd],
                        [flat2(m_in[k]) for k in big_upd], [flat2(v_in[k]) for k in big_upd], "adamw_big")
    for k, dd, mm, vv in zip(big_upd, d_, m_, v_):
        shp = weights[k].shape
        delta[k], new_m[k], new_v[k] = dd.reshape(shp), mm.reshape(shp), vv.reshape(shp)
    small_upd = [k for k in names if k not in big_upd]
    d_, m_, v_ = _adamw_many([flat2(weights[k]) for k in small_upd], [flat2(grads[k]) for k in small_upd],
                             [flat2(m_in[k]) for k in small_upd], [flat2(v_in[k]) for k in small_upd])
    for k, dd, mm, vv in zip(small_upd, d_, m_, v_):
        shp = weights[k].shape
        delta[k], new_m[k], new_v[k] = dd.reshape(shp), mm.reshape(shp), vv.reshape(shp)

    loss = lax.psum(loss_p[0, 0], ("x", "y", "c"))
    grad_x = grad_x2.reshape(Bl, L, D)
    return (loss, grad_x, *[grads[k] for k in names], *[delta[k] for k in names],
            *[new_m[k] for k in names], *[new_v[k] for k in names])
```

```python
import functools
from typing import Callable, NamedTuple

import jax
import jax.numpy as jnp
from jax import lax
from jax.experimental import pallas as pl
from jax.experimental.pallas import tpu as pltpu

F32 = jnp.float32
BF16 = jnp.bfloat16
MESH = pl.DeviceIdType.MESH
N_CHIPS = 4
N_DEV = 8
LANES = 128
SUBLANES = 8
V7X_VMEM_BYTES = 64 * 1024 * 1024
VMEM_LIMIT = V7X_VMEM_BYTES - 6 * 1024 * 1024
SSM_GROUP = 16
SSM_STATE = 64
S5_ROW_PAD = 4
HALO_ROWS = 16
RMS_EPS = 1e-6
ADAM_LR, ADAM_B1, ADAM_B2, ADAM_EPS, ADAM_WD, ADAM_STEP = 0.001, 0.9, 0.999, 1e-08, 0.01, 10

ANY = pl.BlockSpec(memory_space=pl.ANY)
VMEM_SPEC = pl.BlockSpec(memory_space=pltpu.VMEM)


def _cparams(**kw):
    return pltpu.CompilerParams(vmem_limit_bytes=VMEM_LIMIT, **kw)


def _dot(a, b):
    return jnp.dot(a, b, preferred_element_type=F32)


def _dot_nt(a, b):
    return lax.dot_general(a, b, (((1,), (1,)), ((), ())), preferred_element_type=F32)


def _dot_tn(a, b):
    return lax.dot_general(a, b, (((0,), (0,)), ((), ())), preferred_element_type=F32)


def _mesh_pos():
    return lax.axis_index("x"), lax.axis_index("y"), lax.axis_index("c")


def _allgather8(v, name, carried=None):
    r, c = v.shape

    def body(x_ref, out_ref, send_sems, recv_sems, local_sem):
        x, y, cc = _mesh_pos()
        me, sibling = (x, y, cc), (x, y, 1 - cc)
        chips = [(1 - x, y), (x, 1 - y), (1 - x, 1 - y)]

        def slot(px, py, pc):
            return out_ref.at[4 * px + 2 * py + pc]

        def copy(k, block, to, src=None):
            return pltpu.make_async_remote_copy(
                src_ref=slot(*block) if src is None else src, dst_ref=slot(*block),
                send_sem=send_sems.at[k], recv_sem=recv_sems.at[k], device_id=to, device_id_type=MESH)

        mine = pltpu.make_async_copy(x_ref, slot(*me), local_sem)
        mine.start()
        first = [copy(0, me, sibling, src=x_ref)]
        first += [copy(1 + j, me, (*chip, cc), src=x_ref) for j, chip in enumerate(chips)]
        for cp in first:
            cp.start()
        passed = [copy(4 + j, (*chip, cc), sibling) for j, chip in enumerate(chips)]
        for j, chip in enumerate(chips):
            copy(1 + j, (*chip, cc), me).wait_recv()
            passed[j].start()
        copy(0, sibling, me).wait_recv()
        for j, chip in enumerate(chips):
            copy(4 + j, (*chip, 1 - cc), me).wait_recv()
        for cp in first + passed:
            cp.wait_send()
        mine.wait()

    sems = [pltpu.SemaphoreType.DMA((7,)), pltpu.SemaphoreType.DMA((7,)), pltpu.SemaphoreType.DMA]
    out_shape = jax.ShapeDtypeStruct((N_DEV, r, c), v.dtype)
    if carried is None:
        return pl.pallas_call(body, out_shape=out_shape, in_specs=[VMEM_SPEC], out_specs=VMEM_SPEC,
                              scratch_shapes=sems, name=name)(v)
    (out,), extra = _call_carrying(body, carried, (v,), out_shape=[out_shape], in_specs=[VMEM_SPEC],
                                   out_specs=[VMEM_SPEC], scratch_shapes=sems, name=name)
    return out, extra


def _shard_region(ref, axis, shard_shape, q, half):
    R, C = shard_shape
    r0, nr = (0, R) if half is None else (half * (R // 2), R // 2)
    if axis == 1:
        return ref.at[pl.ds(r0, nr), pl.ds(q * C, C)]
    return ref.at[pl.ds(q * R + r0, nr), :]


class _Carried(NamedTuple):
    inputs: tuple
    out_shapes: tuple
    aliases: dict
    sems: tuple
    steps: Callable


def _carry_join(a, b):
    na_i, na_o, na_s = len(a.inputs), len(a.out_shapes), len(a.sems)

    def steps(ins, outs, sems):
        sa, fa = a.steps(ins[:na_i], outs[:na_o], sems[:na_s])
        sb, fb = b.steps(ins[na_i:], outs[na_o:], sems[na_s:])


        def start():
            sa()
            sb()

        def finish():
            fa()
            fb()

        return start, finish

    aliases = dict(a.aliases)
    aliases.update({na_i + i: na_o + o for i, o in b.aliases.items()})
    return _Carried(a.inputs + b.inputs, a.out_shapes + b.out_shapes, aliases, a.sems + b.sems, steps)


def _call_carrying(body, carried, args, *, out_shape, in_specs, out_specs, scratch_shapes=(), grid=None, aliases=None,
                   name, **kw):
    n_in, n_out, n_sc = len(in_specs), len(out_specs), len(scratch_shapes)
    n_ci, n_co = len(carried.inputs), len(carried.out_shapes)

    def wrapped(*refs):
        ins, refs = refs[:n_in], refs[n_in:]
        c_ins, refs = refs[:n_ci], refs[n_ci:]
        outs, refs = refs[:n_out], refs[n_out:]
        c_outs, refs = refs[:n_co], refs[n_co:]
        scratch, c_sems = refs[:n_sc], refs[n_sc:]
        start, finish = carried.steps(c_ins, c_outs, c_sems)
        if grid is None:
            start()
            body(*ins, *outs, *scratch)
            finish()
        else:
            ids = [pl.program_id(d) for d in range(len(grid))]
            first = functools.reduce(jnp.logical_and, [i == 0 for i in ids])
            last = functools.reduce(jnp.logical_and, [i == g - 1 for i, g in zip(ids, grid)])
            pl.when(first)(start)
            body(*ins, *outs, *scratch)
            pl.when(last)(finish)

    if grid is not None:
        kw["grid"] = grid
    io_aliases = dict(aliases or {})
    io_aliases.update({n_in + i: n_out + o for i, o in carried.aliases.items()})
    res = pl.pallas_call(
        wrapped, out_shape=list(out_shape) + list(carried.out_shapes),
        in_specs=list(in_specs) + [ANY] * n_ci, out_specs=list(out_specs) + [ANY] * n_co,
        scratch_shapes=list(scratch_shapes) + list(carried.sems),
        input_output_aliases=io_aliases, name=name, **kw,
    )(*args, *carried.inputs)
    return res[:n_out], res[n_out:]


def _run_carried(carried, name):
    return _call_carrying(lambda: None, carried, (), out_shape=(), in_specs=(), out_specs=(), name=name)[1]


def _place_in_slot(v, dev_arr, name):
    r, c = v.shape

    def body(d_ref, v_ref, o_ref):
        o_ref[...] = v_ref[...]

    return pl.pallas_call(
        body, out_shape=jax.ShapeDtypeStruct((N_DEV, r, c), v.dtype),
        grid_spec=pltpu.PrefetchScalarGridSpec(
            num_scalar_prefetch=1, grid=(1,), in_specs=[pl.BlockSpec((r, c), lambda i, d: (0, 0))],
            out_specs=pl.BlockSpec((None, r, c), lambda i, d: (d[0], 0, 0))),
        name=name)(dev_arr, v)


def _carry_allgather8(buf):
    def steps(ins, outs, sems):
        send_s, recv_s = sems
        out = outs[0]
        x, y, cc = _mesh_pos()
        me, sibling = (x, y, cc), (x, y, 1 - cc)
        chips = [(1 - x, y), (x, 1 - y), (1 - x, 1 - y)]

        def copy(k, block, to):
            px, py, pc = block
            slot = out.at[4 * px + 2 * py + pc]
            return pltpu.make_async_remote_copy(src_ref=slot, dst_ref=slot, send_sem=send_s.at[k], recv_sem=recv_s.at[k],
                                                device_id=to, device_id_type=MESH)

        first = [copy(0, me, sibling)] + [copy(1 + j, me, (*chip, cc)) for j, chip in enumerate(chips)]
        passed = [copy(4 + j, (*chip, cc), sibling) for j, chip in enumerate(chips)]

        def start():
            for cp in first:
                cp.start()

        def finish():
            for j, chip in enumerate(chips):
                copy(1 + j, (*chip, cc), me).wait_recv()
                passed[j].start()
            copy(0, sibling, me).wait_recv()
            for j, chip in enumerate(chips):
                copy(4 + j, (*chip, 1 - cc), me).wait_recv()
            for cp in first + passed:
                cp.wait_send()

        return start, finish

    return _Carried((buf,), (jax.ShapeDtypeStruct(buf.shape, buf.dtype),), {0: 0}, (pltpu.SemaphoreType.DMA((7,)),) * 2, steps)


def _carry_allgather(fulls, axes, shapes):
    n = len(fulls)
    return _Carried(tuple(fulls), tuple(jax.ShapeDtypeStruct(f.shape, f.dtype) for f in fulls),
                    {i: i for i in range(n)}, (pltpu.SemaphoreType.DMA((3 * n,)),) * 4,
                    lambda ins, outs, sems: _allgather_weights_steps(outs, axes, shapes, *sems))


def _allgather_weights_steps(outs, axes, shapes, send_s, recv_s, fsend_s, frecv_s):
    n = len(outs)
    x, y, c = _mesh_pos()
    q = 2 * x + y
    sibling = (x, y, 1 - c)
    chips = [(1 - x, y), (x, 1 - y), (1 - x, 1 - y)]

    def region(i, qq, half):
        return _shard_region(outs[i], axes[i], shapes[i], qq, half)

    def remote(src, dst, ss, rs, to):
        return pltpu.make_async_remote_copy(src_ref=src, dst_ref=dst, send_sem=ss, recv_sem=rs,
                                            device_id=to, device_id_type=MESH)

    def ici(i, j, qq):
        cx, cy = chips[j]
        reg = region(i, qq, c)
        return remote(reg, reg, send_s.at[3 * i + j], recv_s.at[3 * i + j], (cx, cy, c))

    def d2d(i, j, half):
        cx, cy = chips[j]
        reg = region(i, 2 * cx + cy, half)
        return remote(reg, reg, fsend_s.at[3 * i + j], frecv_s.at[3 * i + j], sibling)

    def start():
        for i in range(n):
            for j in range(3):
                ici(i, j, q).start()

    def finish():
        for i in range(n):
            for j, (cx, cy) in enumerate(chips):
                ici(i, j, 2 * cx + cy).wait_recv()
                d2d(i, j, c).start()
        for i in range(n):
            for j in range(3):
                d2d(i, j, 1 - c).wait_recv()
        for i in range(n):
            for j in range(3):
                ici(i, j, q).wait_send()
                d2d(i, j, c).wait_send()

    return start, finish


def _carry_sibling_exchange(grads, axes, shapes):
    n = len(grads)

    def steps(ins, theirs, sems):
        send_s, recv_s = sems
        x, y, c = _mesh_pos()

        def copies():
            return [pltpu.make_async_remote_copy(
                src_ref=_shard_region(ins[i], axes[i], shapes[i], qq, 1 - c), dst_ref=theirs[i].at[qq],
                send_sem=send_s.at[N_CHIPS * i + qq], recv_sem=recv_s.at[N_CHIPS * i + qq],
                device_id=(x, y, 1 - c), device_id_type=MESH) for i in range(n) for qq in range(N_CHIPS)]

        def start():
            for cp in copies():
                cp.start()

        def finish():
            for cp in copies():
                cp.wait()

        return start, finish

    stacked = tuple(jax.ShapeDtypeStruct((N_CHIPS, R // 2, C), F32) for (R, C) in shapes)
    return _Carried(tuple(grads), stacked, {}, (pltpu.SemaphoreType.DMA((N_CHIPS * n,)),) * 2, steps)


def _carry_chip_scatter(parts):
    n = len(parts)

    def steps(ins, outs, sems):
        send_s, recv_s = sems
        x, y, c = _mesh_pos()
        chips = [(1 - x, y), (x, 1 - y), (1 - x, 1 - y)]

        def copies():
            return [pltpu.make_async_remote_copy(
                src_ref=ins[i].at[2 * cx + cy], dst_ref=outs[i].at[j],
                send_sem=send_s.at[3 * i + j], recv_sem=recv_s.at[3 * i + j],
                device_id=(cx, cy, c), device_id_type=MESH) for i in range(n) for j, (cx, cy) in enumerate(chips)]

        def start():
            for cp in copies():
                cp.start()

        def finish():
            for cp in copies():
                cp.wait()

        return start, finish

    return _Carried(tuple(parts), tuple(jax.ShapeDtypeStruct((3,) + p.shape[1:], p.dtype) for p in parts), {},
                    (pltpu.SemaphoreType.DMA((3 * n,)),) * 2, steps)


def _carry_sibling_share(fulls):
    n = len(fulls)

    def steps(ins, outs, sems):
        send_s, recv_s = sems
        x, y, c = _mesh_pos()

        def copy(i, half):
            rh = fulls[i].shape[0] // 2
            rows = outs[i].at[pl.ds(half * rh, rh), :]
            return pltpu.make_async_remote_copy(src_ref=rows, dst_ref=rows, send_sem=send_s.at[i], recv_sem=recv_s.at[i],
                                                device_id=(x, y, 1 - c), device_id_type=MESH)

        def start():
            for i in range(n):
                copy(i, c).start()

        def finish():
            for i in range(n):
                copy(i, 1 - c).wait_recv()
                copy(i, c).wait_send()

        return start, finish

    return _Carried(tuple(fulls), tuple(jax.ShapeDtypeStruct(f.shape, f.dtype) for f in fulls),
                    {i: i for i in range(n)}, (pltpu.SemaphoreType.DMA((n,)),) * 2, steps)


def _row_block(rows, target=256):
    return target if rows % target == 0 else rows


BF16_TILE_ROWS = 16


def _common_steps(rows, most=8):
    ns = most
    while ns > 1 and any(r % (ns * BF16_TILE_ROWS) for r in rows):
        ns //= 2
    return ns


def _cast_into_full(ws, axes, pos, name):
    n = len(ws)
    ns = _common_steps([w.shape[0] for w in ws])
    in_specs, out_specs, out_shape = [], [], []
    for w, axis in zip(ws, axes):
        R, C = w.shape
        in_specs.append(pl.BlockSpec((R // ns, C), lambda i, s: (i, 0)))
        if axis == 1:
            out_shape.append(jax.ShapeDtypeStruct((R, N_CHIPS * C), BF16))
            out_specs.append(pl.BlockSpec((R // ns, C), lambda i, s: (i, s[0])))
        else:
            out_shape.append(jax.ShapeDtypeStruct((N_CHIPS * R, C), BF16))
            out_specs.append(pl.BlockSpec((R // ns, C), lambda i, s: (s[0] * ns + i, 0)))

    def body(s_ref, *refs):
        for k in range(n):
            refs[n + k][...] = refs[k][...].astype(BF16)

    return pl.pallas_call(
        body, out_shape=out_shape,
        grid_spec=pltpu.PrefetchScalarGridSpec(num_scalar_prefetch=1, grid=(ns,), in_specs=in_specs, out_specs=out_specs),
        compiler_params=_cparams(), name=name)(pos, *ws)


def _presum(gs, theirs, axes, shapes, pos, name):
    n = len(gs)
    ns = _common_steps([R // 2 for R, _ in shapes], most=2)
    in_specs, t_specs, out_shape = [], [], []
    for (R, C), axis in zip(shapes, axes):
        rb = R // 2 // ns
        if axis == 1:
            in_specs.append(pl.BlockSpec((rb, C), lambda k, i, s: (s[1] * ns + i, k)))
        else:
            in_specs.append(pl.BlockSpec((rb, C), lambda k, i, s: (k * 2 * ns + s[1] * ns + i, 0)))
        t_specs.append(pl.BlockSpec((None, rb, C), lambda k, i, s: (k, i, 0)))
        out_shape.append(jax.ShapeDtypeStruct((N_CHIPS, R // 2, C), BF16))

    def body(s_ref, *refs):
        for k in range(n):
            refs[2 * n + k][...] = (refs[k][...] + refs[n + k][...]).astype(BF16)

    return pl.pallas_call(
        body, out_shape=out_shape,
        grid_spec=pltpu.PrefetchScalarGridSpec(num_scalar_prefetch=1, grid=(N_CHIPS, ns), in_specs=in_specs + t_specs,
                                               out_specs=t_specs),
        compiler_params=_cparams(), name=name)(pos, *gs, *theirs)


def _sum_chips(owns, recvs, pos, name):
    n = len(owns)
    ns = _common_steps([o.shape[1] for o in owns], most=2)
    o_specs, r_specs, out_specs, out_shape = [], [], [], []
    for o in owns:
        _, Rh, C = o.shape
        rb = Rh // ns
        o_specs.append(pl.BlockSpec((None, rb, C), lambda i, s: (s[0], i, 0)))
        r_specs.append(pl.BlockSpec((3, rb, C), lambda i, s: (0, i, 0)))
        out_specs.append(pl.BlockSpec((rb, C), lambda i, s: (s[1] * ns + i, 0)))
        out_shape.append(jax.ShapeDtypeStruct((2 * Rh, C), F32))

    def body(s_ref, *refs):
        for k in range(n):
            acc = refs[k][...].astype(F32)
            for j in range(3):
                acc = acc + refs[n + k][j].astype(F32)
            refs[2 * n + k][...] = acc

    return pl.pallas_call(
        body, out_shape=out_shape,
        grid_spec=pltpu.PrefetchScalarGridSpec(num_scalar_prefetch=1, grid=(ns,), in_specs=o_specs + r_specs,
                                               out_specs=out_specs),
        compiler_params=_cparams(), name=name)(pos, *owns, *recvs)


def _sum_devices(parts, name):
    K, R, C = parts.shape

    def body(p_ref, o_ref):
        acc = p_ref[0]
        for k in range(1, K):
            acc = acc + p_ref[k]
        o_ref[...] = acc

    return pl.pallas_call(body, out_shape=jax.ShapeDtypeStruct((R, C), F32), name=name)(parts)


def _adamw_math(w, g, m, v):
    nm = ADAM_B1 * m + (1.0 - ADAM_B1) * g
    nv = ADAM_B2 * v + (1.0 - ADAM_B2) * (g * g)
    m_hat = nm / (1.0 - ADAM_B1 ** ADAM_STEP)
    v_hat = nv / (1.0 - ADAM_B2 ** ADAM_STEP)
    return -ADAM_LR * (m_hat / (jnp.sqrt(v_hat) + ADAM_EPS) + ADAM_WD * w), nm, nv


def _adamw_many(ws, gs, ms, vs):
    n = len(ws)

    def body(*refs):
        for k in range(n):
            w, g, m, v = (refs[j * n + k][...] for j in range(4))
            for j, val in enumerate(_adamw_math(w, g, m, v)):
                refs[(4 + j) * n + k][...] = val

    shapes = [jax.ShapeDtypeStruct(w.shape, F32) for w in ws]
    res = pl.pallas_call(body, out_shape=shapes * 3, compiler_params=_cparams(), name="adamw_small")(*ws, *gs, *ms, *vs)
    return res[:n], res[n:2 * n], res[2 * n:]


def _adamw(ws, gs, ms, vs, name):
    n = len(ws)
    ns = _common_steps([w.shape[0] for w in ws])
    specs = [pl.BlockSpec((w.shape[0] // ns, w.shape[1]), lambda i: (i, 0)) for w in ws]

    def body(*refs):
        for k in range(n):
            w, g, m, v = (refs[j * n + k][...] for j in range(4))
            for j, val in enumerate(_adamw_math(w, g, m, v)):
                refs[(4 + j) * n + k][...] = val

    shapes = [jax.ShapeDtypeStruct(w.shape, F32) for w in ws]
    res = pl.pallas_call(body, out_shape=shapes * 3, grid=(ns,), in_specs=specs * 4, out_specs=specs * 3,
                         compiler_params=_cparams(), name=name)(*ws, *gs, *ms, *vs)
    return res[:n], res[n:2 * n], res[2 * n:]


def _silu(v):
    return v * jax.nn.sigmoid(v)


def _ada_fwd(c_all, w_sh, b_sh):
    S, D = c_all.shape
    Ca = w_sh.shape[1]
    cb = 512 if Ca % 512 == 0 else Ca

    def body(c_ref, w_ref, b_ref, o_ref):
        act = _silu(c_ref[...]).astype(BF16)
        o_ref[...] = _dot(act, w_ref[...].astype(BF16)) + b_ref[...]

    return pl.pallas_call(
        body, out_shape=jax.ShapeDtypeStruct((S, Ca), F32), grid=(Ca // cb,),
        in_specs=[pl.BlockSpec((S, D), lambda j: (0, 0)), pl.BlockSpec((D, cb), lambda j: (0, j)),
                  pl.BlockSpec((1, cb), lambda j: (0, j))],
        out_specs=pl.BlockSpec((S, cb), lambda j: (0, j)), name="ada_fwd")(c_all, w_sh, b_sh)


def _ada_bwd(c_all, dmod_sh, dmod_all):
    S, D = c_all.shape
    Ca = dmod_sh.shape[1]
    C6 = dmod_all.shape[1]

    def body(c_ref, ds_ref, da_ref, gw_ref, gb_ref):
        act = _silu(c_ref[...]).astype(BF16)
        gw_ref[...] = _dot_tn(act, ds_ref[...].astype(BF16))
        gb_ref[...] = jnp.sum(da_ref[...], axis=0, keepdims=True)

    return pl.pallas_call(
        body, out_shape=[jax.ShapeDtypeStruct((D, Ca), F32), jax.ShapeDtypeStruct((1, C6), F32)],
        compiler_params=_cparams(), name="ada_bwd")(c_all, dmod_sh, dmod_all)


def _rms_fwd(xv):
    r = lax.rsqrt(jnp.mean(xv * xv, axis=-1, keepdims=True) + RMS_EPS)
    return xv * r, r


def _rms_bwd(dxh, xh, r):
    return r * (dxh - xh * jnp.mean(dxh * xh, axis=-1, keepdims=True))


def _const_spec(shape):
    nd = len(shape)
    return pl.BlockSpec(shape, lambda *_: (0,) * nd)


def _seq_spec(D, bps, rev_blocks=None):
    if rev_blocks is None:
        return pl.BlockSpec((None, 1, D), lambda i: (i // bps, 0, 0))
    return pl.BlockSpec((None, 1, D), lambda i: ((rev_blocks - 1 - i) // bps, 0, 0))


def _inproj_fwd(x2, sh1, sc1, n1g, w_in, carried):
    N, D = x2.shape
    IN = w_in.shape[1]
    Bl = sh1.shape[0]
    TB = _row_block(N // Bl, 512)
    bps = (N // Bl) // TB

    def body(x_ref, sh_ref, sc_ref, g_ref, w_ref, p_ref):
        xh, _ = _rms_fwd(x_ref[...])
        h = (xh * g_ref[...]) * (1.0 + sc_ref[...]) + sh_ref[...]
        p_ref[...] = _dot(h.astype(BF16), w_ref[...]).astype(BF16)

    (p2,), extra = _call_carrying(
        body, carried, (x2, sh1, sc1, n1g, w_in), out_shape=[jax.ShapeDtypeStruct((N, IN), BF16)], grid=(N // TB,),
        in_specs=[pl.BlockSpec((TB, D), lambda i: (i, 0)), _seq_spec(D, bps), _seq_spec(D, bps),
                  _const_spec((1, D)), _const_spec((D, IN))],
        out_specs=[pl.BlockSpec((TB, IN), lambda i: (i, 0))],
        compiler_params=_cparams(dimension_semantics=("arbitrary",)), name="inproj_fwd")
    return p2, extra


def _s5_dims(Bl, L, Ds):
    G = Ds // SSM_GROUP
    GP = G * SSM_STATE
    NP = GP // LANES
    T = min(64, L // 2)
    nb = 2 if (Ds // 2) % LANES == 0 else 1
    return G, GP, NP, T, nb


def _s5_disc_math(lr, li, ldt, bt_r, bt_i):
    dt = jnp.exp(ldt)
    er = jnp.exp(lr * dt)
    lbr = er * jnp.cos(li * dt)
    lbi = er * jnp.sin(li * dt)
    den = lr * lr + li * li
    fr = ((lbr - 1.0) * lr + lbi * li) / den
    fi = (lbi * lr - (lbr - 1.0) * li) / den
    return lbr, lbi, fr[None] * bt_r - fi[None] * bt_i, fr[None] * bt_i + fi[None] * bt_r


def _s5_disc(lr, li, ldt, bt_r, bt_i):
    def body(lr_ref, li_ref, ldt_ref, br_ref, bi_ref, o0, o1, o2, o3):
        res = _s5_disc_math(lr_ref[...], li_ref[...], ldt_ref[...], br_ref[...], bi_ref[...])
        for o, v in zip((o0, o1, o2, o3), res):
            o[...] = v

    S = jax.ShapeDtypeStruct
    return pl.pallas_call(body, out_shape=[S(lr.shape, F32)] * 2 + [S(bt_r.shape, F32)] * 2, name="s5_disc")(lr, li, ldt, bt_r, bt_i)


def _s5_disc_bwd(lr, li, ldt, bt_r, bt_i, dlbr, dlbi, dbbr, dbbi):
    def body(lr_ref, li_ref, ldt_ref, br_ref, bi_ref, g0, g1, g2, g3, o0, o1, o2, o3, o4):
        _, vjp = jax.vjp(_s5_disc_math, lr_ref[...], li_ref[...], ldt_ref[...], br_ref[...], bi_ref[...])
        res = vjp((g0[...], g1[...], g2[...], g3[...]))
        for o, v in zip((o0, o1, o2, o3, o4), res):
            o[...] = v

    S = jax.ShapeDtypeStruct
    return pl.pallas_call(body, out_shape=[S(lr.shape, F32)] * 2 + [S(ldt.shape, F32)] + [S(bt_r.shape, F32)] * 2,
                          name="s5_disc_bwd")(lr, li, ldt, bt_r, bt_i, dlbr, dlbi, dbbr, dbbi)


S5_SCAN_PANELS = 4
S5_SCAN_STEPS = 4


def _panel_scan(src_r, src_i, dst_r, dst_i, lr_ref, li_ref, car_r, car_i, NP, Bl, T, TP, adjoint):
    BT = Bl * TP
    PG = min(S5_SCAN_PANELS, NP)
    CH = S5_SCAN_STEPS
    for k0 in range(0, NP, PG):
        ks = list(range(k0, k0 + PG))
        lr = [jnp.broadcast_to(lr_ref[pl.ds(k, 1), :], (Bl, LANES)) for k in ks]
        li = [jnp.broadcast_to(li_ref[pl.ds(k, 1), :], (Bl, LANES)) for k in ks]

        def trip(cc, carry):
            t0 = (T // CH - 1 - cc) * CH if adjoint else cc * CH
            ts = [t0 + (CH - 1 - s if adjoint else s) for s in range(CH)]
            idx = [[pl.ds(k * BT + t, Bl, stride=TP) for t in ts] for k in ks]
            loaded = [[(src_r[ix, :], src_i[ix, :]) for ix in idx[j]] for j in range(PG)]
            results, new_carry = [], []
            for j in range(PG):
                ar, ai = carry[j]
                res = []
                for s in range(CH):
                    br, bi = loaded[j][s]
                    if adjoint:
                        ar, ai = br + lr[j] * ar + li[j] * ai, bi + lr[j] * ai - li[j] * ar
                    else:
                        ar, ai = lr[j] * ar - li[j] * ai + br, lr[j] * ai + li[j] * ar + bi
                    res.append((ar, ai))
                results.append(res)
                new_carry.append((ar, ai))
            for j in range(PG):
                for s in range(CH):
                    dst_r[idx[j][s], :] = results[j][s][0]
                    dst_i[idx[j][s], :] = results[j][s][1]
            return tuple(new_carry)

        init = tuple((car_r[pl.ds(k * SUBLANES, Bl), :], car_i[pl.ds(k * SUBLANES, Bl), :]) for k in ks)
        fin = lax.fori_loop(0, T // CH, trip, init, unroll=2)
        for j, k in enumerate(ks):
            car_r[pl.ds(k * SUBLANES, Bl), :] = fin[j][0]
            car_i[pl.ds(k * SUBLANES, Bl), :] = fin[j][1]


def _s5_fwd(p3, Bm_r, Bm_i, Cm_r, Cm_i, lam_r, lam_i, dsk, carried):
    Bl, L, _ = p3.shape
    Ds = dsk.shape[1]
    G, GP, NP, T, nb = _s5_dims(Bl, L, Ds)
    nT = L // T
    TP = T + S5_ROW_PAD
    BT = Bl * TP
    dsb, gpb, npb = Ds // nb, GP // nb, NP // nb

    def body(u_ref, br_ref, bi_ref, cr_ref, ci_ref, lr_ref, li_ref, dsk_ref, sr_ref, si_ref, sb_r, sb_i, y_ref,
             car_r, car_i, upad, ypad, bu_r, bu_i):
        i = pl.program_id(0)

        @pl.when(i == 0)
        def _():
            car_r[...] = jnp.zeros_like(car_r)
            car_i[...] = jnp.zeros_like(car_i)
            upad[...] = jnp.zeros_like(upad)

        zpad = jnp.zeros((S5_ROW_PAD, LANES), F32)
        for k in range(NP):
            for b in range(Bl):
                sr_ref[pl.ds(k * BT + b * TP + T, S5_ROW_PAD), :] = zpad
                si_ref[pl.ds(k * BT + b * TP + T, S5_ROW_PAD), :] = zpad
        for b in range(Bl):
            upad[pl.ds(b * TP, T), :] = u_ref[b].astype(F32)
        u = upad[...]
        ub = u.astype(BF16)
        for blk in range(nb):
            ubb = ub[:, blk * dsb:(blk + 1) * dsb]
            for bu_ref, b_ref in ((bu_r, br_ref), (bu_i, bi_ref)):
                res = _dot(ubb, b_ref[blk])
                for kk in range(npb):
                    k = blk * npb + kk
                    bu_ref[pl.ds(k * BT, BT), :] = res[:, kk * LANES:(kk + 1) * LANES]

        _panel_scan(bu_r, bu_i, sr_ref, si_ref, lr_ref, li_ref, car_r, car_i, NP, Bl, T, TP, adjoint=False)
        sb_r[...] = car_r[...]
        sb_i[...] = car_i[...]

        for blk in range(nb):
            s_r = jnp.concatenate([sr_ref[pl.ds((blk * npb + kk) * BT, BT), :] for kk in range(npb)], axis=1).astype(BF16)
            s_i = jnp.concatenate([si_ref[pl.ds((blk * npb + kk) * BT, BT), :] for kk in range(npb)], axis=1).astype(BF16)
            cols = slice(blk * dsb, (blk + 1) * dsb)
            ypad[:, cols] = _dot(s_r, cr_ref[blk]) + _dot(s_i, ci_ref[blk]) + dsk_ref[:, cols] * u[:, cols]
        for b in range(Bl):
            y_ref[b] = ypad[pl.ds(b * TP, T), :]

    S = jax.ShapeDtypeStruct
    state = S((nT, NP * BT, LANES), F32)
    bound = S((nT, NP * SUBLANES, LANES), F32)
    sspec = pl.BlockSpec((None, NP * BT, LANES), lambda i: (i, 0, 0))
    bspec = pl.BlockSpec((None, NP * SUBLANES, LANES), lambda i: (i, 0, 0))
    return _call_carrying(
        body, carried, (p3, Bm_r, Bm_i, Cm_r, Cm_i, lam_r, lam_i, dsk),
        out_shape=[state, state, bound, bound, S((Bl, L, Ds), F32)], grid=(nT,),
        in_specs=[pl.BlockSpec((Bl, T, Ds), lambda i: (0, i, 0)),
                  _const_spec((nb, dsb, gpb)), _const_spec((nb, dsb, gpb)),
                  _const_spec((nb, gpb, dsb)), _const_spec((nb, gpb, dsb)),
                  _const_spec((NP, LANES)), _const_spec((NP, LANES)), _const_spec((1, Ds))],
        out_specs=[sspec, sspec, bspec, bspec, pl.BlockSpec((Bl, T, Ds), lambda i: (0, i, 0))],
        scratch_shapes=[pltpu.VMEM((NP * SUBLANES, LANES), F32)] * 2 + [pltpu.VMEM((BT, Ds), F32)] * 2
        + [pltpu.VMEM((NP * BT, LANES), F32)] * 2,
        compiler_params=_cparams(dimension_semantics=("arbitrary",)), name="s5_fwd")


def _s5_bwd(dy3, p3, dp3, Sr, Si, Sb_r, Sb_i, Bm_r, Bm_i, Cm_r, Cm_i, lam_r, lam_i, dsk, carried):
    Bl, L, Ds = dy3.shape
    G, GP, NP, T, nb = _s5_dims(Bl, L, Ds)
    nT = L // T
    TP = T + S5_ROW_PAD
    BT = Bl * TP
    dsb, gpb, npb = Ds // nb, GP // nb, NP // nb

    def body(dy_ref, u_ref, dp_ref, sr_ref, si_ref, sbr_ref, sbi_ref, br_ref, bi_ref, cr_ref, ci_ref, lr_ref, li_ref, dsk_ref,
             du_ref, dbr_ref, dbi_ref, dcr_ref, dci_ref, dlr_ref, dli_ref, ddsk_ref,
             a_r, a_i, car_r, car_i, acc_r, acc_i, dypad, upad, dupad, q_r, q_i):
        i = pl.program_id(0)

        @pl.when(i == 0)
        def _():
            for ref in (car_r, car_i, acc_r, acc_i, dbr_ref, dbi_ref, dcr_ref, dci_ref, ddsk_ref, dypad, upad, a_r, a_i):
                ref[...] = jnp.zeros_like(ref)

        for b in range(Bl):
            dypad[pl.ds(b * TP, T), :] = dy_ref[b]
            upad[pl.ds(b * TP, T), :] = u_ref[b].astype(F32)
        dy = dypad[...]
        dyb = dy.astype(BF16)
        u = upad[...]
        ub = u.astype(BF16)
        for blk in range(nb):
            dyb_b = dyb[:, blk * dsb:(blk + 1) * dsb]
            for q_ref, c_ref in ((q_r, cr_ref), (q_i, ci_ref)):
                res = _dot_nt(dyb_b, c_ref[blk])
                for kk in range(npb):
                    q_ref[pl.ds((blk * npb + kk) * BT, BT), :] = res[:, kk * LANES:(kk + 1) * LANES]

        _panel_scan(q_r, q_i, a_r, a_i, lr_ref, li_ref, car_r, car_i, NP, Bl, T, TP, adjoint=True)

        first_block = (i == nT - 1)
        for k in range(NP):
            rows = pl.ds(k * BT, BT)
            av_r, av_i = a_r[rows, :], a_i[rows, :]
            sp_r = pltpu.roll(sr_ref[rows, :], 1, 0)
            sp_i = pltpu.roll(si_ref[rows, :], 1, 0)
            acc = pl.ds(k * SUBLANES, SUBLANES)
            acc_r[acc, :] += jnp.sum((av_r * sp_r + av_i * sp_i).reshape(BT // SUBLANES, SUBLANES, LANES), axis=0)
            acc_i[acc, :] += jnp.sum((av_i * sp_r - av_r * sp_i).reshape(BT // SUBLANES, SUBLANES, LANES), axis=0)
            t0 = pl.ds(k * BT, Bl, stride=TP)
            a0_r, a0_i = a_r[t0, :], a_i[t0, :]
            brow = pl.ds(k * SUBLANES, Bl)
            sb_pr = jnp.where(first_block, 0.0, sbr_ref[brow, :])
            sb_pi = jnp.where(first_block, 0.0, sbi_ref[brow, :])
            acc_r[brow, :] += a0_r * sb_pr + a0_i * sb_pi
            acc_i[brow, :] += a0_i * sb_pr - a0_r * sb_pi

        ddsk_ref[...] += jnp.sum(dy * u, axis=0, keepdims=True)
        for blk in range(nb):
            cols = slice(blk * dsb, (blk + 1) * dsb)
            rows = [pl.ds((blk * npb + kk) * BT, BT) for kk in range(npb)]
            av_r = jnp.concatenate([a_r[r, :] for r in rows], axis=1).astype(BF16)
            av_i = jnp.concatenate([a_i[r, :] for r in rows], axis=1).astype(BF16)
            dupad[:, cols] = _dot_nt(av_r, br_ref[blk]) + _dot_nt(av_i, bi_ref[blk]) + dy[:, cols] * dsk_ref[:, cols]
            dbr_ref[blk] += _dot_tn(ub[:, cols], av_r)
            dbi_ref[blk] += _dot_tn(ub[:, cols], av_i)
            sv_r = jnp.concatenate([sr_ref[r, :] for r in rows], axis=1).astype(BF16)
            sv_i = jnp.concatenate([si_ref[r, :] for r in rows], axis=1).astype(BF16)
            dcr_ref[blk] += _dot_tn(sv_r, dyb[:, cols])
            dci_ref[blk] += _dot_tn(sv_i, dyb[:, cols])
        for b in range(Bl):
            du_ref[b] = dupad[pl.ds(b * TP, T), :].astype(BF16)

        @pl.when(i == nT - 1)
        def _():
            for k in range(NP):
                dlr_ref[pl.ds(k, 1), :] = jnp.sum(acc_r[pl.ds(k * SUBLANES, SUBLANES), :], axis=0, keepdims=True)
                dli_ref[pl.ds(k, 1), :] = jnp.sum(acc_i[pl.ds(k * SUBLANES, SUBLANES), :], axis=0, keepdims=True)

    S = jax.ShapeDtypeStruct
    rev = lambda i: nT - 1 - i
    sspec = pl.BlockSpec((None, NP * BT, LANES), lambda i: (rev(i), 0, 0))
    bspec = pl.BlockSpec((None, NP * SUBLANES, LANES), lambda i: (jnp.maximum(rev(i) - 1, 0), 0, 0))
    tspec = pl.BlockSpec((Bl, T, Ds), lambda i: (0, rev(i), 0))
    return _call_carrying(
        body, carried, (dy3, p3, dp3, Sr, Si, Sb_r, Sb_i, Bm_r, Bm_i, Cm_r, Cm_i, lam_r, lam_i, dsk),
        out_shape=[S(dp3.shape, dp3.dtype), S((nb, dsb, gpb), F32), S((nb, dsb, gpb), F32),
                   S((nb, gpb, dsb), F32), S((nb, gpb, dsb), F32), S((NP, LANES), F32), S((NP, LANES), F32), S((1, Ds), F32)],
        grid=(nT,),
        in_specs=[tspec, tspec, ANY, sspec, sspec, bspec, bspec,
                  _const_spec((nb, dsb, gpb)), _const_spec((nb, dsb, gpb)),
                  _const_spec((nb, gpb, dsb)), _const_spec((nb, gpb, dsb)),
                  _const_spec((NP, LANES)), _const_spec((NP, LANES)), _const_spec((1, Ds))],
        out_specs=[tspec, _const_spec((nb, dsb, gpb)), _const_spec((nb, dsb, gpb)),
                   _const_spec((nb, gpb, dsb)), _const_spec((nb, gpb, dsb)),
                   _const_spec((NP, LANES)), _const_spec((NP, LANES)), _const_spec((1, Ds))],
        aliases={2: 0},
        scratch_shapes=[pltpu.VMEM((NP * BT, LANES), F32)] * 2 + [pltpu.VMEM((NP * SUBLANES, LANES), F32)] * 4
        + [pltpu.VMEM((BT, Ds), F32)] * 3 + [pltpu.VMEM((NP * BT, LANES), F32)] * 2,
        compiler_params=_cparams(dimension_semantics=("arbitrary",)), name="s5_bwd")


def _mix_values(ylin, cb, cc, cx, gs, gc, halo_v, wglu, bglu, cw, wps, wpc, wout):
    yg, gelu_vjp = jax.vjp(jax.nn.gelu, ylin)
    sz = jax.nn.sigmoid(_dot(yg.astype(BF16), wglu) + bglu)
    ys = yg * sz
    v = cc * cx
    rows = lax.broadcasted_iota(jnp.int32, v.shape, 0)
    h6 = halo_v[HALO_ROWS - 2:HALO_ROWS - 1, :]
    h7 = halo_v[HALO_ROWS - 1:HALO_ROWS, :]
    v1 = jnp.where(rows == 0, h7, pltpu.roll(v, 1, 0))
    v2 = jnp.where(rows == 0, h6, jnp.where(rows == 1, h7, pltpu.roll(v, 2, 0)))
    cv = cw[0:1, :] * v2 + cw[1:2, :] * v1 + cw[2:3, :] * v
    yc = cb * cv
    ps = _dot(ys.astype(BF16), wps)
    pc = _dot(yc.astype(BF16), wpc)
    sgs = jax.nn.sigmoid(gs)
    sgc = jax.nn.sigmoid(gc)
    merged = sgs * ps + sgc * pc
    mo = _dot(merged.astype(BF16), wout)
    return dict(yg=yg, gelu_vjp=gelu_vjp, sz=sz, ys=ys, v=v, v1=v1, v2=v2, cv=cv, yc=yc, ps=ps, pc=pc,
                sgs=sgs, sgc=sgc, merged=merged, mo=mo)


def _mix_in_specs(TB, D, Ds, Dc, bps, blk):
    hb = TB // HALO_ROWS
    halo = lambda col: pl.BlockSpec((HALO_ROWS, Dc), lambda i: (jnp.maximum(blk(i) * hb - 1, 0), col))
    return [pl.BlockSpec((TB, Dc), lambda i: (blk(i), 1)), pl.BlockSpec((TB, Dc), lambda i: (blk(i), 2)),
            pl.BlockSpec((TB, Dc), lambda i: (blk(i), 3)), pl.BlockSpec((TB, D), lambda i: (blk(i), 2)),
            pl.BlockSpec((TB, D), lambda i: (blk(i), 3)), halo(2), halo(3),
            pl.BlockSpec((TB, Ds), lambda i: (blk(i), 0))]


def _mix_fwd(p2, ylin2, x2, g1, wglu, bglu, cw, wps, wpc, wout):
    N, D = x2.shape
    Ds = ylin2.shape[1]
    Dc = Ds
    Bl = g1.shape[0]
    TB = _row_block(N // Bl, 512)
    bps = (N // Bl) // TB

    def body(cb_ref, cc_ref, cx_ref, gs_ref, gc_ref, hcc_ref, hcx_ref, yl_ref, x_ref, g1_ref,
             wglu_ref, bglu_ref, cw_ref, wps_ref, wpc_ref, wout_ref, x1_ref):
        i = pl.program_id(0)
        f32 = lambda ref: ref[...].astype(F32)
        halo_v = jnp.where(i % bps == 0, 0.0, f32(hcc_ref) * f32(hcx_ref))
        f = _mix_values(yl_ref[...], f32(cb_ref), f32(cc_ref), f32(cx_ref), f32(gs_ref), f32(gc_ref), halo_v,
                        wglu_ref[...], bglu_ref[...], cw_ref[...], wps_ref[...], wpc_ref[...], wout_ref[...])
        x1_ref[...] = x_ref[...] + g1_ref[...] * f["mo"]

    return pl.pallas_call(
        body, out_shape=jax.ShapeDtypeStruct((N, D), F32), grid=(N // TB,),
        in_specs=_mix_in_specs(TB, D, Ds, Dc, bps, lambda i: i) + [
            pl.BlockSpec((TB, D), lambda i: (i, 0)), _seq_spec(D, bps),
            _const_spec((Ds, Ds)), _const_spec((1, Ds)), _const_spec((SUBLANES, Dc)),
            _const_spec((Ds, D)), _const_spec((Dc, D)), _const_spec((D, D))],
        out_specs=pl.BlockSpec((TB, D), lambda i: (i, 0)),
        compiler_params=_cparams(), name="mix_fwd",
    )(p2, p2, p2, p2, p2, p2, p2, ylin2, x2, g1, wglu, bglu, cw, wps, wpc, wout)


def _mix_bwd(p2, ylin2, dx1, g1, wglu, bglu, cw, wps, wpc, wout, carried):
    N, D = dx1.shape
    Ds = ylin2.shape[1]
    Dc = Ds
    IN = p2.shape[1]
    Bl = g1.shape[0]
    TB = _row_block(N // Bl)
    bps = (N // Bl) // TB
    nblk = N // TB
    rev = lambda i: nblk - 1 - i

    def body(cb_ref, cc_ref, cx_ref, gs_ref, gc_ref, hcc_ref, hcx_ref, yl_ref, dx1_ref, g1_ref,
             wglu_ref, bglu_ref, cw_ref, wps_ref, wpc_ref, wout_ref,
             dyl_ref, dp_ref, gwout_ref, gwps_ref, gwpc_ref, gwglu_ref, gbglu_ref, gcw_ref, dg1_ref, nxt):
        i = pl.program_id(0)
        blk = rev(i)

        @pl.when(i == 0)
        def _():
            for ref in (gwout_ref, gwps_ref, gwpc_ref, gwglu_ref, gbglu_ref, gcw_ref):
                ref[...] = jnp.zeros_like(ref)

        @pl.when(i % bps == 0)
        def _():
            nxt[...] = jnp.zeros_like(nxt)
            dg1_ref[...] = jnp.zeros_like(dg1_ref)

        f32 = lambda ref: ref[...].astype(F32)
        cb, cc, cx = f32(cb_ref), f32(cc_ref), f32(cx_ref)
        halo_v = jnp.where(blk % bps == 0, 0.0, f32(hcc_ref) * f32(hcx_ref))
        wglu, wps, wpc, wout, cw = wglu_ref[...], wps_ref[...], wpc_ref[...], wout_ref[...], cw_ref[...]
        f = _mix_values(yl_ref[...], cb, cc, cx, f32(gs_ref), f32(gc_ref), halo_v, wglu, bglu_ref[...], cw, wps, wpc, wout)

        dx1v = dx1_ref[...]
        dg1_ref[...] += jnp.sum(dx1v * f["mo"], axis=0, keepdims=True)
        dmo = (g1_ref[...] * dx1v).astype(BF16)
        gwout_ref[...] += _dot_tn(f["merged"].astype(BF16), dmo)
        dmerged = _dot_nt(dmo, wout)
        dps = dmerged * f["sgs"]
        dpc = dmerged * f["sgc"]
        dgs = dmerged * f["ps"] * f["sgs"] * (1.0 - f["sgs"])
        dgc = dmerged * f["pc"] * f["sgc"] * (1.0 - f["sgc"])
        dpsb, dpcb = dps.astype(BF16), dpc.astype(BF16)
        gwps_ref[...] += _dot_tn(f["ys"].astype(BF16), dpsb)
        gwpc_ref[...] += _dot_tn(f["yc"].astype(BF16), dpcb)
        dys = _dot_nt(dpsb, wps)
        dyc = _dot_nt(dpcb, wpc)

        dcb = dyc * f["cv"]
        dcv = dyc * cb
        rows = lax.broadcasted_iota(jnp.int32, dcv.shape, 0)
        n0, n1 = nxt[0:1, :], nxt[1:2, :]
        d1 = jnp.where(rows == TB - 1, n0, pltpu.roll(dcv, TB - 1, 0))
        d2 = jnp.where(rows == TB - 2, n0, jnp.where(rows == TB - 1, n1, pltpu.roll(dcv, TB - 2, 0)))
        dv = cw[2:3, :] * dcv + cw[1:2, :] * d1 + cw[0:1, :] * d2
        nxt[0:2, :] = dcv[0:2, :]
        gcw_ref[0:1, :] += jnp.sum(dcv * f["v2"], axis=0, keepdims=True)
        gcw_ref[1:2, :] += jnp.sum(dcv * f["v1"], axis=0, keepdims=True)
        gcw_ref[2:3, :] += jnp.sum(dcv * f["v"], axis=0, keepdims=True)

        dz = dys * f["yg"] * f["sz"] * (1.0 - f["sz"])
        dzb = dz.astype(BF16)
        gwglu_ref[...] += _dot_tn(f["yg"].astype(BF16), dzb)
        gbglu_ref[...] += jnp.sum(dz, axis=0, keepdims=True)
        dyg = dys * f["sz"] + _dot_nt(dzb, wglu)
        dyl_ref[...] = f["gelu_vjp"](dyg)[0]

        dp_ref[:, Ds:Ds + Dc] = dcb.astype(BF16)
        dp_ref[:, Ds + Dc:Ds + 2 * Dc] = (dv * cx).astype(BF16)
        dp_ref[:, Ds + 2 * Dc:Ds + 3 * Dc] = (dv * cc).astype(BF16)
        dp_ref[:, Ds + 3 * Dc:Ds + 3 * Dc + D] = dgs.astype(BF16)
        dp_ref[:, Ds + 3 * Dc + D:IN] = dgc.astype(BF16)

    S = jax.ShapeDtypeStruct
    return _call_carrying(
        body, carried, (p2, p2, p2, p2, p2, p2, p2, ylin2, dx1, g1, wglu, bglu, cw, wps, wpc, wout),
        out_shape=[S((N, Ds), F32), S((N, IN), BF16), S((D, D), F32), S((Ds, D), F32), S((Dc, D), F32),
                   S((Ds, Ds), F32), S((1, Ds), F32), S((SUBLANES, Dc), F32), S((Bl, 1, D), F32)],
        grid=(nblk,),
        in_specs=_mix_in_specs(TB, D, Ds, Dc, bps, rev) + [
            pl.BlockSpec((TB, D), lambda i: (rev(i), 0)), _seq_spec(D, bps, nblk),
            _const_spec((Ds, Ds)), _const_spec((1, Ds)), _const_spec((SUBLANES, Dc)),
            _const_spec((Ds, D)), _const_spec((Dc, D)), _const_spec((D, D))],
        out_specs=[pl.BlockSpec((TB, Ds), lambda i: (rev(i), 0)), pl.BlockSpec((TB, IN), lambda i: (rev(i), 0)),
                   _const_spec((D, D)), _const_spec((Ds, D)), _const_spec((Dc, D)), _const_spec((Ds, Ds)),
                   _const_spec((1, Ds)), _const_spec((SUBLANES, Dc)), _seq_spec(D, bps, nblk)],
        scratch_shapes=[pltpu.VMEM((SUBLANES, Dc), F32)],
        compiler_params=_cparams(dimension_semantics=("arbitrary",)), name="mix_bwd")


def _mlp_fwd_bwd(x1, tgt, sh2, sc2, g2, n2g, fg, w1, w2):
    N, D = x1.shape
    Dff = w1.shape[1]
    Bl = sh2.shape[0]
    TB = _row_block(N // Bl)
    bps = (N // Bl) // TB

    def body(x1_ref, t_ref, sh_ref, sc_ref, g2_ref, n2_ref, fg_ref, w1_ref, w2_ref,
             dx1_ref, h2_ref, da_ref, sq_ref, df_ref, loss_ref, gfg_ref, gn2_ref, dsh_ref, dsc_ref, dg2_ref):
        i = pl.program_id(0)

        @pl.when(i == 0)
        def _():
            for ref in (loss_ref, gfg_ref, gn2_ref):
                ref[...] = jnp.zeros_like(ref)

        @pl.when(i % bps == 0)
        def _():
            for ref in (dsh_ref, dsc_ref, dg2_ref):
                ref[...] = jnp.zeros_like(ref)

        x1v = x1_ref[...]
        sc, g2v, n2 = sc_ref[...], g2_ref[...], n2_ref[...]
        xh2, r2 = _rms_fwd(x1v)
        xn2 = xh2 * n2
        h2 = (xn2 * (1.0 + sc) + sh_ref[...]).astype(BF16)
        a = _dot(h2, w1_ref[...])
        ra = jnp.maximum(a, 0.0)
        sq = (ra * ra).astype(BF16)
        fv = _dot(sq, w2_ref[...])
        x2 = x1v + g2v * fv
        xh3, r3 = _rms_fwd(x2)
        err = xh3 * fg_ref[...] - t_ref[...]
        loss_ref[...] += 0.5 * jnp.sum(jnp.mean(err * err, axis=-1, keepdims=True), axis=0, keepdims=True)
        dy = err * (1.0 / D)
        gfg_ref[...] += jnp.sum(dy * xh3, axis=0, keepdims=True)
        dx2 = _rms_bwd(dy * fg_ref[...], xh3, r3)
        dg2_ref[...] += jnp.sum(dx2 * fv, axis=0, keepdims=True)
        df = (g2v * dx2).astype(BF16)
        dsq = _dot_nt(df, w2_ref[...])
        da = (2.0 * ra * dsq).astype(BF16)
        dh2 = _dot_nt(da, w1_ref[...])
        dsh_ref[...] += jnp.sum(dh2, axis=0, keepdims=True)
        dsc_ref[...] += jnp.sum(dh2 * xn2, axis=0, keepdims=True)
        dxn2 = dh2 * (1.0 + sc)
        gn2_ref[...] += jnp.sum(dxn2 * xh2, axis=0, keepdims=True)
        dx1_ref[...] = dx2 + _rms_bwd(dxn2 * n2, xh2, r2)
        h2_ref[...] = h2
        da_ref[...] = da
        sq_ref[...] = sq
        df_ref[...] = df

    S = jax.ShapeDtypeStruct
    row = lambda w: pl.BlockSpec((TB, w), lambda i: (i, 0))
    return pl.pallas_call(
        body,
        out_shape=[S((N, D), F32), S((N, D), BF16), S((N, Dff), BF16), S((N, Dff), BF16), S((N, D), BF16),
                   S((1, 1), F32), S((1, D), F32), S((1, D), F32), S((Bl, 1, D), F32), S((Bl, 1, D), F32), S((Bl, 1, D), F32)],
        grid=(N // TB,),
        in_specs=[row(D), row(D), _seq_spec(D, bps), _seq_spec(D, bps), _seq_spec(D, bps),
                  _const_spec((1, D)), _const_spec((1, D)), _const_spec((D, Dff)), _const_spec((Dff, D))],
        out_specs=[row(D), row(D), row(Dff), row(Dff), row(D), _const_spec((1, 1)), _const_spec((1, D)), _const_spec((1, D)),
                   _seq_spec(D, bps), _seq_spec(D, bps), _seq_spec(D, bps)],
        compiler_params=_cparams(dimension_semantics=("arbitrary",)), name="mlp_fwd_bwd",
    )(x1, tgt, sh2, sc2, g2, n2g, fg, w1, w2)


_NO_EXCHANGE = _Carried((), (), {}, (), lambda ins, outs, sems: ((lambda: None), (lambda: None)))


def _grad_w(a, b, name, carried=_NO_EXCHANGE):
    N, K1 = a.shape
    K2 = b.shape[1]
    t1 = 1024 if K1 % 1024 == 0 else K1
    t2 = 1024 if K2 % 1024 == 0 else K2
    tn = 2048 if N % 2048 == 0 else N

    def body(a_ref, b_ref, o_ref):
        @pl.when(pl.program_id(2) == 0)
        def _():
            o_ref[...] = jnp.zeros_like(o_ref)

        o_ref[...] += _dot_tn(a_ref[...], b_ref[...])

    (g,), extra = _call_carrying(
        body, carried, (a, b), out_shape=[jax.ShapeDtypeStruct((K1, K2), F32)], grid=(K1 // t1, K2 // t2, N // tn),
        in_specs=[pl.BlockSpec((tn, t1), lambda i, j, k: (k, i)), pl.BlockSpec((tn, t2), lambda i, j, k: (k, j))],
        out_specs=[pl.BlockSpec((t1, t2), lambda i, j, k: (i, j))],
        compiler_params=_cparams(dimension_semantics=("arbitrary", "arbitrary", "arbitrary")), name=name)
    return g, extra


def _inproj_bwd(x2, dx1, dp, sh1, sc1, n1g, w_in, carried):
    N, D = x2.shape
    IN = w_in.shape[1]
    Bl = sh1.shape[0]
    TB = _row_block(N // Bl, 512)
    bps = (N // Bl) // TB

    def body(x_ref, dx1_ref, dp_ref, sh_ref, sc_ref, g_ref, w_ref, gx_ref, h_ref, gn1_ref, dsh_ref, dsc_ref):
        i = pl.program_id(0)

        @pl.when(i == 0)
        def _():
            gn1_ref[...] = jnp.zeros_like(gn1_ref)

        @pl.when(i % bps == 0)
        def _():
            dsh_ref[...] = jnp.zeros_like(dsh_ref)
            dsc_ref[...] = jnp.zeros_like(dsc_ref)

        sc, n1 = sc_ref[...], g_ref[...]
        xh, r = _rms_fwd(x_ref[...])
        xn = xh * n1
        h_ref[...] = (xn * (1.0 + sc) + sh_ref[...]).astype(BF16)
        dh = _dot_nt(dp_ref[...], w_ref[...])
        dsh_ref[...] += jnp.sum(dh, axis=0, keepdims=True)
        dsc_ref[...] += jnp.sum(dh * xn, axis=0, keepdims=True)
        dxn = dh * (1.0 + sc)
        gn1_ref[...] += jnp.sum(dxn * xh, axis=0, keepdims=True)
        gx_ref[...] = dx1_ref[...] + _rms_bwd(dxn * n1, xh, r)

    S = jax.ShapeDtypeStruct
    row = lambda w: pl.BlockSpec((TB, w), lambda i: (i, 0))
    return _call_carrying(
        body, carried, (x2, dx1, dp, sh1, sc1, n1g, w_in),
        out_shape=[S((N, D), F32), S((N, D), BF16), S((1, D), F32), S((Bl, 1, D), F32), S((Bl, 1, D), F32)],
        grid=(N // TB,),
        in_specs=[row(D), row(D), row(IN), _seq_spec(D, bps), _seq_spec(D, bps), _const_spec((1, D)), _const_spec((D, IN))],
        out_specs=[row(D), row(D), _const_spec((1, D)), _seq_spec(D, bps), _seq_spec(D, bps)],
        compiler_params=_cparams(dimension_semantics=("arbitrary",)), name="inproj_bwd")


def _diag_blocks_from_groups(m, nb):
    G, a, b = m.shape
    gb = G // nb
    eye = jnp.eye(gb, dtype=m.dtype)
    mm = m.reshape(nb, gb, a, b)
    return (mm[:, :, :, None, :] * eye[None, :, None, :, None]).reshape(nb, gb * a, gb * b)


def _groups_from_diag_blocks(d, G, a, b):
    nb = d.shape[0]
    gb = G // nb
    dd = d.reshape(nb, gb, a, gb, b)
    idx = jnp.arange(gb)
    return dd[:, idx, :, idx, :].transpose(1, 0, 2, 3).reshape(G, a, b)


def _pad_rows(v, rows):
    return jnp.concatenate([v, jnp.zeros((rows - v.shape[0],) + v.shape[1:], v.dtype)], axis=0)


def _pack(vs):
    flat = jnp.concatenate([v.reshape(-1) for v in vs])
    n = flat.shape[0]
    tile = SUBLANES * LANES
    npad = -(-n // tile) * tile
    flat = jnp.concatenate([flat, jnp.zeros((npad - n,), flat.dtype)])
    return flat.reshape(npad // LANES, LANES)


def _unpack(packed, shapes):
    flat = packed.reshape(-1)
    out, off = [], 0
    for s in shapes:
        n = 1
        for d in s:
            n *= d
        out.append(flat[off:off + n].reshape(s))
        off += n
    return out


def kernel(x, c, norm1_g, norm2_g, w_ada, b_ada, w_in, lam_re, lam_im, log_dt, b_re, b_im, c_re, c_im, d_skip, w_glu, b_glu, conv_w, w_proj_ssm, w_proj_conv, w_out, w_ff1, w_ff2, final_g, loss_target, m_norm1_g, m_norm2_g, m_w_ada, m_b_ada, m_w_in, m_lam_re, m_lam_im, m_log_dt, m_b_re, m_b_im, m_c_re, m_c_im, m_d_skip, m_w_glu, m_b_glu, m_conv_w, m_w_proj_ssm, m_w_proj_conv, m_w_out, m_w_ff1, m_w_ff2, m_final_g, v_norm1_g, v_norm2_g, v_w_ada, v_b_ada, v_w_in, v_lam_re, v_lam_im, v_log_dt, v_b_re, v_b_im, v_c_re, v_c_im, v_d_skip, v_w_glu, v_b_glu, v_conv_w, v_w_proj_ssm, v_w_proj_conv, v_w_out, v_w_ff1, v_w_ff2, v_final_g):
    Bl, L, D = x.shape
    N = Bl * L
    Ds = Dc = D // 2
    G, H, P = Ds // SSM_GROUP, SSM_GROUP, SSM_STATE
    GP = G * P
    NP = GP // LANES
    nb = _s5_dims(Bl, L, Ds)[4]
    IN = Ds + 3 * Dc + 2 * D
    ax, ay, ac = _mesh_pos()
    q = 2 * ax + ay
    dev = 2 * q + ac

    big_names = ["w_in", "w_ff1", "w_ff2", "w_out", "w_proj_ssm", "w_proj_conv", "w_glu"]
    big_w = dict(w_in=w_in[0], w_ff1=w_ff1[0], w_ff2=w_ff2[0], w_out=w_out[0],
                 w_proj_ssm=w_proj_ssm[0], w_proj_conv=w_proj_conv[0], w_glu=w_glu[0])
    big_axis = dict(w_in=1, w_ff1=1, w_ff2=0, w_out=0, w_proj_ssm=1, w_proj_conv=1, w_glu=0)
    axes = [big_axis[k] for k in big_names]
    shard_shapes = [big_w[k].shape for k in big_names]
    pos = jnp.stack([q, ac]).astype(jnp.int32)
    own_only = dict(zip(big_names, _cast_into_full([big_w[k] for k in big_names], axes, pos, "cast_weights")))
    Dcs = conv_w.shape[2]
    first, (w_in_full,) = _allgather8(_pack([c, conv_w[0]]), "allgather_c_w_in",
                                      _carry_allgather([own_only["w_in"]], [big_axis["w_in"]], [big_w["w_in"].shape]))
    full = {"w_in": w_in_full}
    first = first.reshape(N_DEV, -1)
    c_all = first[:, :Bl * D].reshape(N_DEV * Bl, D)
    cw = first[0::2, Bl * D:Bl * D + 3 * Dcs].reshape(N_CHIPS, 3, Dcs).transpose(1, 0, 2).reshape(3, Dc)
    cw8 = _pad_rows(cw, SUBLANES)
    Ca = w_ada.shape[2]
    b_ada_sh = lax.dynamic_slice_in_dim(b_ada, q * Ca, Ca, axis=1)
    mod_part = _ada_fwd(c_all, w_ada[0], b_ada_sh)
    mod_g = _allgather8(mod_part, "allgather_mod")
    mod_all = mod_g[0::2].transpose(1, 0, 2).reshape(N_DEV * Bl, N_CHIPS * Ca)
    mod = lax.dynamic_slice_in_dim(mod_all, dev * Bl, Bl, axis=0)
    sh1, sc1, g1, sh2, sc2, g2 = [mod[:, k * D:(k + 1) * D].reshape(Bl, 1, D) for k in range(6)]

    ldt_c = log_dt[0].reshape(G, 1)
    bt_r = b_re[0].transpose(2, 0, 1)
    bt_i = b_im[0].transpose(2, 0, 1)
    lbr, lbi, bbt_r, bbt_i = _s5_disc(lam_re[0], lam_im[0], ldt_c, bt_r, bt_i)
    lam_r_p = lbr.reshape(NP, LANES)
    lam_i_p = lbi.reshape(NP, LANES)
    Bm_r = _diag_blocks_from_groups(bbt_r.transpose(1, 0, 2), nb).astype(BF16)
    Bm_i = _diag_blocks_from_groups(bbt_i.transpose(1, 0, 2), nb).astype(BF16)
    Cm_r = _diag_blocks_from_groups(c_re[0].transpose(0, 2, 1), nb).astype(BF16)
    Cm_i = _diag_blocks_from_groups(-c_im[0].transpose(0, 2, 1), nb).astype(BF16)

    x2 = x.reshape(N, D)
    mixer_w = ["w_out", "w_proj_ssm", "w_proj_conv", "w_glu"]
    mlp_w = ["w_ff1", "w_ff2"]
    layout = lambda ks: ([big_axis[k] for k in ks], [big_w[k].shape for k in ks])
    gather = lambda ks: _carry_allgather([own_only[k] for k in ks], *layout(ks))
    p2, gathered = _inproj_fwd(x2, sh1, sc1, norm1_g, full["w_in"], gather(mixer_w))
    full.update(zip(mixer_w, gathered))
    p3 = p2.reshape(Bl, L, IN)
    (Sr, Si, Sb_r, Sb_i, ylin3), gathered = _s5_fwd(p3, Bm_r, Bm_i, Cm_r, Cm_i, lam_r_p, lam_i_p, d_skip, gather(mlp_w))
    full.update(zip(mlp_w, gathered))
    ylin2 = ylin3.reshape(N, Ds)
    mix_w = (full["w_glu"], b_glu, cw8, full["w_proj_ssm"], full["w_proj_conv"], full["w_out"])
    x1 = _mix_fwd(p2, ylin2, x2, g1, *mix_w)

    (dx1, h2b, dab, sqb, dfb, loss_p, g_fg, g_n2, dsh2, dsc2, dg2) = _mlp_fwd_bwd(
        x1, loss_target.reshape(N, D), sh2, sc2, g2, norm2_g, final_g.reshape(1, D), full["w_ff1"], full["w_ff2"])
    g_full = {"w_ff1": _grad_w(h2b, dab, "grad_w_ff1")[0], "w_ff2": _grad_w(sqb, dfb, "grad_w_ff2")[0]}

    exchange = lambda ks: _carry_sibling_exchange([g_full[k] for k in ks], *layout(ks))
    presum = lambda ks, theirs: _presum([g_full[k] for k in ks], list(theirs), *layout(ks), pos, "presum_" + ks[0])
    chip_sum = lambda ks, parts, recv: _sum_chips(list(parts), list(recv), pos, "sum_" + ks[0])

    (dyl2, dp2, gw_out, gw_ps, gw_pc, gw_glu, gb_glu, gcw8, dg1), theirs_mlp = _mix_bwd(
        p2, ylin2, dx1, g1, *mix_w, exchange(mlp_w))
    g_full.update(w_out=gw_out, w_proj_ssm=gw_ps, w_proj_conv=gw_pc, w_glu=gw_glu)
    parts_mlp = presum(mlp_w, theirs_mlp)
    (dp3, dBm_r, dBm_i, dCm_r, dCm_i, dlam_r_p, dlam_i_p, g_dsk), extra = _s5_bwd(
        dyl2.reshape(Bl, L, Ds), p3, dp2.reshape(Bl, L, IN), Sr, Si, Sb_r, Sb_i, Bm_r, Bm_i, Cm_r, Cm_i, lam_r_p, lam_i_p,
        d_skip, _carry_join(_carry_chip_scatter(parts_mlp), exchange(mixer_w)))
    recv_mlp, theirs_mix = extra[:len(mlp_w)], extra[len(mlp_w):]
    halves_mlp = chip_sum(mlp_w, parts_mlp, recv_mlp)
    parts_mix = presum(mixer_w, theirs_mix)
    dp_all = dp3.reshape(N, IN)
    (grad_x2, hb, g_n1, dsh1, dsc1), _ = _inproj_bwd(x2, dx1, dp_all, sh1, sc1, norm1_g, full["w_in"], _NO_EXCHANGE)

    dbbt_r = _groups_from_diag_blocks(dBm_r, G, H, P).transpose(1, 0, 2)
    dbbt_i = _groups_from_diag_blocks(dBm_i, G, H, P).transpose(1, 0, 2)
    dc_re = _groups_from_diag_blocks(dCm_r, G, P, H).transpose(0, 2, 1)
    dc_im = -_groups_from_diag_blocks(dCm_i, G, P, H).transpose(0, 2, 1)
    dmod = jnp.concatenate([dsh1, dsc1, dg1, dsh2, dsc2, dg2], axis=-1).reshape(Bl, 6 * D)
    small = [g_n1, g_n2, g_fg, g_dsk, gb_glu, gcw8[:3], dlam_r_p, dlam_i_p, dbbt_r, dbbt_i, dc_re, dc_im]
    small_shapes = [v.shape for v in small]
    n_small = sum(int(v.size) for v in small)
    small_slots = _place_in_slot(_pack(small + [dmod]), jnp.reshape(dev, (1,)).astype(jnp.int32), "place_small")

    g_full["w_in"], extra = _grad_w(
        hb, dp_all, "grad_w_in",
        _carry_join(_carry_join(_carry_sibling_share(halves_mlp), _carry_chip_scatter(parts_mix)), _carry_allgather8(small_slots)))
    reduced = dict(zip(mlp_w, extra[:len(mlp_w)]))
    halves_mix = chip_sum(mixer_w, parts_mix, extra[len(mlp_w):len(mlp_w) + len(mixer_w)])
    gathered = extra[-1]
    theirs_in = _run_carried(exchange(["w_in"]), "rs_exchange_w_in")
    parts_in = presum(["w_in"], theirs_in)
    recv_in = _run_carried(_carry_chip_scatter(parts_in), "rs_scatter_w_in")
    halves_in = chip_sum(["w_in"], parts_in, recv_in)
    reduced.update(zip(mixer_w + ["w_in"], _run_carried(_carry_sibling_share(halves_mix + halves_in), "rs_share_rest")))

    red = _unpack(_sum_devices(gathered, "sum_small"), small_shapes)
    (r_n1, r_n2, r_fg, r_dsk, r_bglu, r_cw, r_dlr, r_dli, r_dbr, r_dbi, r_cre, r_cim) = red
    dmod_all = gathered.reshape(N_DEV, -1)[:, n_small:n_small + Bl * 6 * D].reshape(N_DEV * Bl, 6 * D)
    gw_ada, gb_ada = _ada_bwd(c_all, lax.dynamic_slice_in_dim(dmod_all, q * Ca, Ca, axis=1), dmod_all)
    g_lr, g_li, g_ldt, g_bt_r, g_bt_i = _s5_disc_bwd(lam_re[0], lam_im[0], ldt_c, bt_r, bt_i,
                                                   r_dlr.reshape(G, P), r_dli.reshape(G, P), r_dbr, r_dbi)

    grads = dict(
        norm1_g=r_n1, norm2_g=r_n2, w_ada=gw_ada, b_ada=gb_ada, lam_re=g_lr, lam_im=g_li, log_dt=g_ldt.reshape(1, G),
        b_re=g_bt_r.transpose(1, 2, 0), b_im=g_bt_i.transpose(1, 2, 0), c_re=r_cre, c_im=r_cim, d_skip=r_dsk,
        b_glu=r_bglu, conv_w=lax.dynamic_slice_in_dim(r_cw, q * Dcs, Dcs, axis=1), final_g=r_fg, **reduced)
    weights = dict(norm1_g=norm1_g, norm2_g=norm2_g, w_ada=w_ada, b_ada=b_ada, w_in=w_in, lam_re=lam_re, lam_im=lam_im,
                   log_dt=log_dt, b_re=b_re, b_im=b_im, c_re=c_re, c_im=c_im, d_skip=d_skip, w_glu=w_glu, b_glu=b_glu,
                   conv_w=conv_w, w_proj_ssm=w_proj_ssm, w_proj_conv=w_proj_conv, w_out=w_out, w_ff1=w_ff1, w_ff2=w_ff2,
                   final_g=final_g)
    m_in = dict(norm1_g=m_norm1_g, norm2_g=m_norm2_g, w_ada=m_w_ada, b_ada=m_b_ada, w_in=m_w_in, lam_re=m_lam_re,
                lam_im=m_lam_im, log_dt=m_log_dt, b_re=m_b_re, b_im=m_b_im, c_re=m_c_re, c_im=m_c_im, d_skip=m_d_skip,
                w_glu=m_w_glu, b_glu=m_b_glu, conv_w=m_conv_w, w_proj_ssm=m_w_proj_ssm, w_proj_conv=m_w_proj_conv,
                w_out=m_w_out, w_ff1=m_w_ff1, w_ff2=m_w_ff2, final_g=m_final_g)
    v_in = dict(norm1_g=v_norm1_g, norm2_g=v_norm2_g, w_ada=v_w_ada, b_ada=v_b_ada, w_in=v_w_in, lam_re=v_lam_re,
                lam_im=v_lam_im, log_dt=v_log_dt, b_re=v_b_re, b_im=v_b_im, c_re=v_c_re, c_im=v_c_im, d_skip=v_d_skip,
                w_glu=v_w_glu, b_glu=v_b_glu, conv_w=v_conv_w, w_proj_ssm=v_w_proj_ssm, w_proj_conv=v_w_proj_conv,
                w_out=v_w_out, w_ff1=v_w_ff1, w_ff2=v_w_ff2, final_g=v_final_g)
    names = list(weights)
    grads = {k: grads[k].reshape(weights[k].shape) for k in names}

    big_upd = big_names + ["w_ada"]
    delta, new_m, new_v = {}, {}, {}
    flat2 = lambda a: a.reshape(-1, a.shape[-1])
    d_, m_, v_ = _adamw([flat2(weights[k]) for k in big_upd], [flat2(grads[k]) for k in big_upd],
                        [flat2(m_in[k]) for k in big_upd], [flat2(v_in[k]) for k in big_upd], "adamw_big")
    for k, dd, mm, vv in zip(big_upd, d_, m_, v_):
        shp = weights[k].shape
        delta[k], new_m[k], new_v[k] = dd.reshape(shp), mm.reshape(shp), vv.reshape(shp)
    small_upd = [k for k in names if k not in big_upd]
    d_, m_, v_ = _adamw_many([flat2(weights[k]) for k in small_upd], [flat2(grads[k]) for k in small_upd],
                             [flat2(m_in[k]) for k in small_upd], [flat2(v_in[k]) for k in small_upd])
    for k, dd, mm, vv in zip(small_upd, d_, m_, v_):
        shp = weights[k].shape
        delta[k], new_m[k], new_v[k] = dd.reshape(shp), mm.reshape(shp), vv.reshape(shp)

    loss = lax.psum(loss_p[0, 0], ("x", "y", "c"))
    grad_x = grad_x2.reshape(Bl, L, D)
    return (loss, grad_x, *[grads[k] for k in names], *[delta[k] for k in names],
            *[new_m[k] for k in names], *[new_v[k] for k in names])
```

```python
import functools
from typing import Callable, NamedTuple

import jax
import jax.numpy as jnp
from jax import lax
from jax.experimental import pallas as pl
from jax.experimental.pallas import tpu as pltpu

F32 = jnp.float32
BF16 = jnp.bfloat16
MESH = pl.DeviceIdType.MESH
N_CHIPS = 4
N_DEV = 8
LANES = 128
SUBLANES = 8
V7X_VMEM_BYTES = 64 * 1024 * 1024
VMEM_LIMIT = V7X_VMEM_BYTES - 6 * 1024 * 1024
SSM_GROUP = 16
SSM_STATE = 64
S5_ROW_PAD = 4
HALO_ROWS = 16
RMS_EPS = 1e-6
ADAM_LR, ADAM_B1, ADAM_B2, ADAM_EPS, ADAM_WD, ADAM_STEP = 0.001, 0.9, 0.999, 1e-08, 0.01, 10

ANY = pl.BlockSpec(memory_space=pl.ANY)
VMEM_SPEC = pl.BlockSpec(memory_space=pltpu.VMEM)


def _cparams(**kw):
    return pltpu.CompilerParams(vmem_limit_bytes=VMEM_LIMIT, **kw)


def _dot(a, b):
    return jnp.dot(a, b, preferred_element_type=F32)


def _dot_nt(a, b):
    return lax.dot_general(a, b, (((1,), (1,)), ((), ())), preferred_element_type=F32)


def _dot_tn(a, b):
    return lax.dot_general(a, b, (((0,), (0,)), ((), ())), preferred_element_type=F32)


def _mesh_pos():
    return lax.axis_index("x"), lax.axis_index("y"), lax.axis_index("c")


def _allgather8(v, name, carried=None):
    r, c = v.shape

    def body(x_ref, out_ref, send_sems, recv_sems, local_sem):
        x, y, cc = _mesh_pos()
        me, sibling = (x, y, cc), (x, y, 1 - cc)
        chips = [(1 - x, y), (x, 1 - y), (1 - x, 1 - y)]

        def slot(px, py, pc):
            return out_ref.at[4 * px + 2 * py + pc]

        def copy(k, block, to, src=None):
            return pltpu.make_async_remote_copy(
                src_ref=slot(*block) if src is None else src, dst_ref=slot(*block),
                send_sem=send_sems.at[k], recv_sem=recv_sems.at[k], device_id=to, device_id_type=MESH)

        mine = pltpu.make_async_copy(x_ref, slot(*me), local_sem)
        mine.start()
        first = [copy(0, me, sibling, src=x_ref)]
        first += [copy(1 + j, me, (*chip, cc), src=x_ref) for j, chip in enumerate(chips)]
        for cp in first:
            cp.start()
        passed = [copy(4 + j, (*chip, cc), sibling) for j, chip in enumerate(chips)]
        for j, chip in enumerate(chips):
            copy(1 + j, (*chip, cc), me).wait_recv()
            passed[j].start()
        copy(0, sibling, me).wait_recv()
        for j, chip in enumerate(chips):
            copy(4 + j, (*chip, 1 - cc), me).wait_recv()
        for cp in first + passed:
            cp.wait_send()
        mine.wait()

    sems = [pltpu.SemaphoreType.DMA((7,)), pltpu.SemaphoreType.DMA((7,)), pltpu.SemaphoreType.DMA]
    out_shape = jax.ShapeDtypeStruct((N_DEV, r, c), v.dtype)
    if carried is None:
        return pl.pallas_call(body, out_shape=out_shape, in_specs=[VMEM_SPEC], out_specs=VMEM_SPEC,
                              scratch_shapes=sems, name=name)(v)
    (out,), extra = _call_carrying(body, carried, (v,), out_shape=[out_shape], in_specs=[VMEM_SPEC],
                                   out_specs=[VMEM_SPEC], scratch_shapes=sems, name=name)
    return out, extra


def _shard_region(ref, axis, shard_shape, q, half):
    R, C = shard_shape
    r0, nr = (0, R) if half is None else (half * (R // 2), R // 2)
    if axis == 1:
        return ref.at[pl.ds(r0, nr), pl.ds(q * C, C)]
    return ref.at[pl.ds(q * R + r0, nr), :]


class _Carried(NamedTuple):
    inputs: tuple
    out_shapes: tuple
    aliases: dict
    sems: tuple
    steps: Callable


def _carry_join(a, b):
    na_i, na_o, na_s = len(a.inputs), len(a.out_shapes), len(a.sems)

    def steps(ins, outs, sems):
        sa, fa = a.steps(ins[:na_i], outs[:na_o], sems[:na_s])
        sb, fb = b.steps(ins[na_i:], outs[na_o:], sems[na_s:])


        def start():
            sa()
            sb()

        def finish():
            fa()
            fb()

        return start, finish

    aliases = dict(a.aliases)
    aliases.update({na_i + i: na_o + o for i, o in b.aliases.items()})
    return _Carried(a.inputs + b.inputs, a.out_shapes + b.out_shapes, aliases, a.sems + b.sems, steps)


def _call_carrying(body, carried, args, *, out_shape, in_specs, out_specs, scratch_shapes=(), grid=None, aliases=None,
                   name, **kw):
    n_in, n_out, n_sc = len(in_specs), len(out_specs), len(scratch_shapes)
    n_ci, n_co = len(carried.inputs), len(carried.out_shapes)

    def wrapped(*refs):
        ins, refs = refs[:n_in], refs[n_in:]
        c_ins, refs = refs[:n_ci], refs[n_ci:]
        outs, refs = refs[:n_out], refs[n_out:]
        c_outs, refs = refs[:n_co], refs[n_co:]
        scratch, c_sems = refs[:n_sc], refs[n_sc:]
        start, finish = carried.steps(c_ins, c_outs, c_sems)
        if grid is None:
            start()
            body(*ins, *outs, *scratch)
            finish()
        else:
            ids = [pl.program_id(d) for d in range(len(grid))]
            first = functools.reduce(jnp.logical_and, [i == 0 for i in ids])
            last = functools.reduce(jnp.logical_and, [i == g - 1 for i, g in zip(ids, grid)])
            pl.when(first)(start)
            body(*ins, *outs, *scratch)
            pl.when(last)(finish)

    if grid is not None:
        kw["grid"] = grid
    io_aliases = dict(aliases or {})
    io_aliases.update({n_in + i: n_out + o for i, o in carried.aliases.items()})
    res = pl.pallas_call(
        wrapped, out_shape=list(out_shape) + list(carried.out_shapes),
        in_specs=list(in_specs) + [ANY] * n_ci, out_specs=list(out_specs) + [ANY] * n_co,
        scratch_shapes=list(scratch_shapes) + list(carried.sems),
        input_output_aliases=io_aliases, name=name, **kw,
    )(*args, *carried.inputs)
    return res[:n_out], res[n_out:]


def _run_carried(carried, name):
    return _call_carrying(lambda: None, carried, (), out_shape=(), in_specs=(), out_specs=(), name=name)[1]


def _place_in_slot(v, dev_arr, name):
    r, c = v.shape

    def body(d_ref, v_ref, o_ref):
        o_ref[...] = v_ref[...]

    return pl.pallas_call(
        body, out_shape=jax.ShapeDtypeStruct((N_DEV, r, c), v.dtype),
        grid_spec=pltpu.PrefetchScalarGridSpec(
            num_scalar_prefetch=1, grid=(1,), in_specs=[pl.BlockSpec((r, c), lambda i, d: (0, 0))],
            out_specs=pl.BlockSpec((None, r, c), lambda i, d: (d[0], 0, 0))),
        name=name)(dev_arr, v)


def _carry_allgather8(buf):
    def steps(ins, outs, sems):
        send_s, recv_s = sems
        out = outs[0]
        x, y, cc = _mesh_pos()
        me, sibling = (x, y, cc), (x, y, 1 - cc)
        chips = [(1 - x, y), (x, 1 - y), (1 - x, 1 - y)]

        def copy(k, block, to):
            px, py, pc = block
            slot = out.at[4 * px + 2 * py + pc]
            return pltpu.make_async_remote_copy(src_ref=slot, dst_ref=slot, send_sem=send_s.at[k], recv_sem=recv_s.at[k],
                                                device_id=to, device_id_type=MESH)

        first = [copy(0, me, sibling)] + [copy(1 + j, me, (*chip, cc)) for j, chip in enumerate(chips)]
        passed = [copy(4 + j, (*chip, cc), sibling) for j, chip in enumerate(chips)]

        def start():
            for cp in first:
                cp.start()

        def finish():
            for j, chip in enumerate(chips):
                copy(1 + j, (*chip, cc), me).wait_recv()
                passed[j].start()
            copy(0, sibling, me).wait_recv()
            for j, chip in enumerate(chips):
                copy(4 + j, (*chip, 1 - cc), me).wait_recv()
            for cp in first + passed:
                cp.wait_send()

        return start, finish

    return _Carried((buf,), (jax.ShapeDtypeStruct(buf.shape, buf.dtype),), {0: 0}, (pltpu.SemaphoreType.DMA((7,)),) * 2, steps)


def _carry_allgather(fulls, axes, shapes):
    n = len(fulls)
    return _Carried(tuple(fulls), tuple(jax.ShapeDtypeStruct(f.shape, f.dtype) for f in fulls),
                    {i: i for i in range(n)}, (pltpu.SemaphoreType.DMA((3 * n,)),) * 4,
                    lambda ins, outs, sems: _allgather_weights_steps(outs, axes, shapes, *sems))


def _allgather_weights_steps(outs, axes, shapes, send_s, recv_s, fsend_s, frecv_s):
    n = len(outs)
    x, y, c = _mesh_pos()
    q = 2 * x + y
    sibling = (x, y, 1 - c)
    chips = [(1 - x, y), (x, 1 - y), (1 - x, 1 - y)]

    def region(i, qq, half):
        return _shard_region(outs[i], axes[i], shapes[i], qq, half)

    def remote(src, dst, ss, rs, to):
        return pltpu.make_async_remote_copy(src_ref=src, dst_ref=dst, send_sem=ss, recv_sem=rs,
                                            device_id=to, device_id_type=MESH)

    def ici(i, j, qq):
        cx, cy = chips[j]
        reg = region(i, qq, c)
        return remote(reg, reg, send_s.at[3 * i + j], recv_s.at[3 * i + j], (cx, cy, c))

    def d2d(i, j, half):
        cx, cy = chips[j]
        reg = region(i, 2 * cx + cy, half)
        return remote(reg, reg, fsend_s.at[3 * i + j], frecv_s.at[3 * i + j], sibling)

    def start():
        for i in range(n):
            for j in range(3):
                ici(i, j, q).start()

    def finish():
        for i in range(n):
            for j, (cx, cy) in enumerate(chips):
                ici(i, j, 2 * cx + cy).wait_recv()
                d2d(i, j, c).start()
        for i in range(n):
            for j in range(3):
                d2d(i, j, 1 - c).wait_recv()
        for i in range(n):
            for j in range(3):
                ici(i, j, q).wait_send()
                d2d(i, j, c).wait_send()

    return start, finish


def _carry_sibling_exchange(grads, axes, shapes):
    n = len(grads)

    def steps(ins, theirs, sems):
        send_s, recv_s = sems
        x, y, c = _mesh_pos()

        def copies():
            return [pltpu.make_async_remote_copy(
                src_ref=_shard_region(ins[i], axes[i], shapes[i], qq, 1 - c), dst_ref=theirs[i].at[qq],
                send_sem=send_s.at[N_CHIPS * i + qq], recv_sem=recv_s.at[N_CHIPS * i + qq],
                device_id=(x, y, 1 - c), device_id_type=MESH) for i in range(n) for qq in range(N_CHIPS)]

        def start():
            for cp in copies():
                cp.start()

        def finish():
            for cp in copies():
                cp.wait()

        return start, finish

    stacked = tuple(jax.ShapeDtypeStruct((N_CHIPS, R // 2, C), F32) for (R, C) in shapes)
    return _Carried(tuple(grads), stacked, {}, (pltpu.SemaphoreType.DMA((N_CHIPS * n,)),) * 2, steps)


def _carry_chip_scatter(parts):
    n = len(parts)

    def steps(ins, outs, sems):
        send_s, recv_s = sems
        x, y, c = _mesh_pos()
        chips = [(1 - x, y), (x, 1 - y), (1 - x, 1 - y)]

        def copies():
            return [pltpu.make_async_remote_copy(
                src_ref=ins[i].at[2 * cx + cy], dst_ref=outs[i].at[j],
                send_sem=send_s.at[3 * i + j], recv_sem=recv_s.at[3 * i + j],
                device_id=(cx, cy, c), device_id_type=MESH) for i in range(n) for j, (cx, cy) in enumerate(chips)]

        def start():
            for cp in copies():
                cp.start()

        def finish():
            for cp in copies():
                cp.wait()

        return start, finish

    return _Carried(tuple(parts), tuple(jax.ShapeDtypeStruct((3,) + p.shape[1:], p.dtype) for p in parts), {},
                    (pltpu.SemaphoreType.DMA((3 * n,)),) * 2, steps)


def _carry_sibling_share(fulls):
    n = len(fulls)

    def steps(ins, outs, sems):
        send_s, recv_s = sems
        x, y, c = _mesh_pos()

        def copy(i, half):
            rh = fulls[i].shape[0] // 2
            rows = outs[i].at[pl.ds(half * rh, rh), :]
            return pltpu.make_async_remote_copy(src_ref=rows, dst_ref=rows, send_sem=send_s.at[i], recv_sem=recv_s.at[i],
                                                device_id=(x, y, 1 - c), device_id_type=MESH)

        def start():
            for i in range(n):
                copy(i, c).start()

        def finish():
            for i in range(n):
                copy(i, 1 - c).wait_recv()
                copy(i, c).wait_send()

        return start, finish

    return _Carried(tuple(fulls), tuple(jax.ShapeDtypeStruct(f.shape, f.dtype) for f in fulls),
                    {i: i for i in range(n)}, (pltpu.SemaphoreType.DMA((n,)),) * 2, steps)


def _row_block(rows, target=256):
    return target if rows % target == 0 else rows


BF16_TILE_ROWS = 16


def _common_steps(rows, most=8):
    ns = most
    while ns > 1 and any(r % (ns * BF16_TILE_ROWS) for r in rows):
        ns //= 2
    return ns


def _cast_into_full(ws, axes, pos, name):
    n = len(ws)
    ns = _common_steps([w.shape[0] for w in ws])
    in_specs, out_specs, out_shape = [], [], []
    for w, axis in zip(ws, axes):
        R, C = w.shape
        in_specs.append(pl.BlockSpec((R // ns, C), lambda i, s: (i, 0)))
        if axis == 1:
            out_shape.append(jax.ShapeDtypeStruct((R, N_CHIPS * C), BF16))
            out_specs.append(pl.BlockSpec((R // ns, C), lambda i, s: (i, s[0])))
        else:
            out_shape.append(jax.ShapeDtypeStruct((N_CHIPS * R, C), BF16))
            out_specs.append(pl.BlockSpec((R // ns, C), lambda i, s: (s[0] * ns + i, 0)))

    def body(s_ref, *refs):
        for k in range(n):
            refs[n + k][...] = refs[k][...].astype(BF16)

    return pl.pallas_call(
        body, out_shape=out_shape,
        grid_spec=pltpu.PrefetchScalarGridSpec(num_scalar_prefetch=1, grid=(ns,), in_specs=in_specs, out_specs=out_specs),
        compiler_params=_cparams(), name=name)(pos, *ws)


def _presum(gs, theirs, axes, shapes, pos, name):
    n = len(gs)
    ns = _common_steps([R // 2 for R, _ in shapes], most=2)
    in_specs, t_specs, out_shape = [], [], []
    for (R, C), axis in zip(shapes, axes):
        rb = R // 2 // ns
        if axis == 1:
            in_specs.append(pl.BlockSpec((rb, C), lambda k, i, s: (s[1] * ns + i, k)))
        else:
            in_specs.append(pl.BlockSpec((rb, C), lambda k, i, s: (k * 2 * ns + s[1] * ns + i, 0)))
        t_specs.append(pl.BlockSpec((None, rb, C), lambda k, i, s: (k, i, 0)))
        out_shape.append(jax.ShapeDtypeStruct((N_CHIPS, R // 2, C), BF16))

    def body(s_ref, *refs):
        for k in range(n):
            refs[2 * n + k][...] = (refs[k][...] + refs[n + k][...]).astype(BF16)

    return pl.pallas_call(
        body, out_shape=out_shape,
        grid_spec=pltpu.PrefetchScalarGridSpec(num_scalar_prefetch=1, grid=(N_CHIPS, ns), in_specs=in_specs + t_specs,
                                               out_specs=t_specs),
        compiler_params=_cparams(), name=name)(pos, *gs, *theirs)


def _sum_chips(owns, recvs, pos, name):
    n = len(owns)
    ns = _common_steps([o.shape[1] for o in owns], most=2)
    o_specs, r_specs, out_specs, out_shape = [], [], [], []
    for o in owns:
        _, Rh, C = o.shape
        rb = Rh // ns
        o_specs.append(pl.BlockSpec((None, rb, C), lambda i, s: (s[0], i, 0)))
        r_specs.append(pl.BlockSpec((3, rb, C), lambda i, s: (0, i, 0)))
        out_specs.append(pl.BlockSpec((rb, C), lambda i, s: (s[1] * ns + i, 0)))
        out_shape.append(jax.ShapeDtypeStruct((2 * Rh, C), F32))

    def body(s_ref, *refs):
        for k in range(n):
            acc = refs[k][...].astype(F32)
            for j in range(3):
                acc = acc + refs[n + k][j].astype(F32)
            refs[2 * n + k][...] = acc

    return pl.pallas_call(
        body, out_shape=out_shape,
        grid_spec=pltpu.PrefetchScalarGridSpec(num_scalar_prefetch=1, grid=(ns,), in_specs=o_specs + r_specs,
                                               out_specs=out_specs),
        compiler_params=_cparams(), name=name)(pos, *owns, *recvs)


def _sum_devices(parts, name):
    K, R, C = parts.shape

    def body(p_ref, o_ref):
        acc = p_ref[0]
        for k in range(1, K):
            acc = acc + p_ref[k]
        o_ref[...] = acc

    return pl.pallas_call(body, out_shape=jax.ShapeDtypeStruct((R, C), F32), name=name)(parts)


def _adamw_math(w, g, m, v):
    nm = ADAM_B1 * m + (1.0 - ADAM_B1) * g
    nv = ADAM_B2 * v + (1.0 - ADAM_B2) * (g * g)
    m_hat = nm / (1.0 - ADAM_B1 ** ADAM_STEP)
    v_hat = nv / (1.0 - ADAM_B2 ** ADAM_STEP)
    return -ADAM_LR * (m_hat / (jnp.sqrt(v_hat) + ADAM_EPS) + ADAM_WD * w), nm, nv


def _adamw_many(ws, gs, ms, vs):
    n = len(ws)

    def body(*refs):
        for k in range(n):
            w, g, m, v = (refs[j * n + k][...] for j in range(4))
            for j, val in enumerate(_adamw_math(w, g, m, v)):
                refs[(4 + j) * n + k][...] = val

    shapes = [jax.ShapeDtypeStruct(w.shape, F32) for w in ws]
    res = pl.pallas_call(body, out_shape=shapes * 3, compiler_params=_cparams(), name="adamw_small")(*ws, *gs, *ms, *vs)
    return res[:n], res[n:2 * n], res[2 * n:]


def _adamw(ws, gs, ms, vs, name):
    n = len(ws)
    ns = _common_steps([w.shape[0] for w in ws])
    specs = [pl.BlockSpec((w.shape[0] // ns, w.shape[1]), lambda i: (i, 0)) for w in ws]

    def body(*refs):
        for k in range(n):
            w, g, m, v = (refs[j * n + k][...] for j in range(4))
            for j, val in enumerate(_adamw_math(w, g, m, v)):
                refs[(4 + j) * n + k][...] = val

    shapes = [jax.ShapeDtypeStruct(w.shape, F32) for w in ws]
    res = pl.pallas_call(body, out_shape=shapes * 3, grid=(ns,), in_specs=specs * 4, out_specs=specs * 3,
                         compiler_params=_cparams(), name=name)(*ws, *gs, *ms, *vs)
    return res[:n], res[n:2 * n], res[2 * n:]


def _silu(v):
    return v * jax.nn.sigmoid(v)


def _ada_fwd(c_all, w_sh, b_sh):
    S, D = c_all.shape
    Ca = w_sh.shape[1]
    cb = 512 if Ca % 512 == 0 else Ca

    def body(c_ref, w_ref, b_ref, o_ref):
        act = _silu(c_ref[...]).astype(BF16)
        o_ref[...] = _dot(act, w_ref[...].astype(BF16)) + b_ref[...]

    return pl.pallas_call(
        body, out_shape=jax.ShapeDtypeStruct((S, Ca), F32), grid=(Ca // cb,),
        in_specs=[pl.BlockSpec((S, D), lambda j: (0, 0)), pl.BlockSpec((D, cb), lambda j: (0, j)),
                  pl.BlockSpec((1, cb), lambda j: (0, j))],
        out_specs=pl.BlockSpec((S, cb), lambda j: (0, j)), name="ada_fwd")(c_all, w_sh, b_sh)


def _ada_bwd(c_all, dmod_sh, dmod_all):
    S, D = c_all.shape
    Ca = dmod_sh.shape[1]
    C6 = dmod_all.shape[1]

    def body(c_ref, ds_ref, da_ref, gw_ref, gb_ref):
        act = _silu(c_ref[...]).astype(BF16)
        gw_ref[...] = _dot_tn(act, ds_ref[...].astype(BF16))
        gb_ref[...] = jnp.sum(da_ref[...], axis=0, keepdims=True)

    return pl.pallas_call(
        body, out_shape=[jax.ShapeDtypeStruct((D, Ca), F32), jax.ShapeDtypeStruct((1, C6), F32)],
        compiler_params=_cparams(), name="ada_bwd")(c_all, dmod_sh, dmod_all)


def _rms_fwd(xv):
    r = lax.rsqrt(jnp.mean(xv * xv, axis=-1, keepdims=True) + RMS_EPS)
    return xv * r, r


def _rms_bwd(dxh, xh, r):
    return r * (dxh - xh * jnp.mean(dxh * xh, axis=-1, keepdims=True))


def _const_spec(shape):
    nd = len(shape)
    return pl.BlockSpec(shape, lambda *_: (0,) * nd)


def _seq_spec(D, bps, rev_blocks=None):
    if rev_blocks is None:
        return pl.BlockSpec((None, 1, D), lambda i: (i // bps, 0, 0))
    return pl.BlockSpec((None, 1, D), lambda i: ((rev_blocks - 1 - i) // bps, 0, 0))


def _inproj_fwd(x2, sh1, sc1, n1g, w_in, carried):
    N, D = x2.shape
    IN = w_in.shape[1]
    Bl = sh1.shape[0]
    TB = _row_block(N // Bl, 512)
    bps = (N // Bl) // TB

    def body(x_ref, sh_ref, sc_ref, g_ref, w_ref, p_ref):
        xh, _ = _rms_fwd(x_ref[...])
        h = (xh * g_ref[...]) * (1.0 + sc_ref[...]) + sh_ref[...]
        p_ref[...] = _dot(h.astype(BF16), w_ref[...]).astype(BF16)

    (p2,), extra = _call_carrying(
        body, carried, (x2, sh1, sc1, n1g, w_in), out_shape=[jax.ShapeDtypeStruct((N, IN), BF16)], grid=(N // TB,),
        in_specs=[pl.BlockSpec((TB, D), lambda i: (i, 0)), _seq_spec(D, bps), _seq_spec(D, bps),
                  _const_spec((1, D)), _const_spec((D, IN))],
        out_specs=[pl.BlockSpec((TB, IN), lambda i: (i, 0))],
        compiler_params=_cparams(dimension_semantics=("arbitrary",)), name="inproj_fwd")
    return p2, extra


def _s5_dims(Bl, L, Ds):
    G = Ds // SSM_GROUP
    GP = G * SSM_STATE
    NP = GP // LANES
    T = min(64, L // 2)
    nb = 2 if (Ds // 2) % LANES == 0 else 1
    return G, GP, NP, T, nb


def _s5_disc_math(lr, li, ldt, bt_r, bt_i):
    dt = jnp.exp(ldt)
    er = jnp.exp(lr * dt)
    lbr = er * jnp.cos(li * dt)
    lbi = er * jnp.sin(li * dt)
    den = lr * lr + li * li
    fr = ((lbr - 1.0) * lr + lbi * li) / den
    fi = (lbi * lr - (lbr - 1.0) * li) / den
    return lbr, lbi, fr[None] * bt_r - fi[None] * bt_i, fr[None] * bt_i + fi[None] * bt_r


def _s5_disc(lr, li, ldt, bt_r, bt_i):
    def body(lr_ref, li_ref, ldt_ref, br_ref, bi_ref, o0, o1, o2, o3):
        res = _s5_disc_math(lr_ref[...], li_ref[...], ldt_ref[...], br_ref[...], bi_ref[...])
        for o, v in zip((o0, o1, o2, o3), res):
            o[...] = v

    S = jax.ShapeDtypeStruct
    return pl.pallas_call(body, out_shape=[S(lr.shape, F32)] * 2 + [S(bt_r.shape, F32)] * 2, name="s5_disc")(lr, li, ldt, bt_r, bt_i)


def _s5_disc_bwd(lr, li, ldt, bt_r, bt_i, dlbr, dlbi, dbbr, dbbi):
    def body(lr_ref, li_ref, ldt_ref, br_ref, bi_ref, g0, g1, g2, g3, o0, o1, o2, o3, o4):
        _, vjp = jax.vjp(_s5_disc_math, lr_ref[...], li_ref[...], ldt_ref[...], br_ref[...], bi_ref[...])
        res = vjp((g0[...], g1[...], g2[...], g3[...]))
        for o, v in zip((o0, o1, o2, o3, o4), res):
            o[...] = v

    S = jax.ShapeDtypeStruct
    return pl.pallas_call(body, out_shape=[S(lr.shape, F32)] * 2 + [S(ldt.shape, F32)] + [S(bt_r.shape, F32)] * 2,
                          name="s5_disc_bwd")(lr, li, ldt, bt_r, bt_i, dlbr, dlbi, dbbr, dbbi)


S5_SCAN_PANELS = 4
S5_SCAN_STEPS = 4


def _panel_scan(src_r, src_i, dst_r, dst_i, lr_ref, li_ref, car_r, car_i, NP, Bl, T, TP, adjoint):
    BT = Bl * TP
    PG = min(S5_SCAN_PANELS, NP)
    CH = S5_SCAN_STEPS
    for k0 in range(0, NP, PG):
        ks = list(range(k0, k0 + PG))
        lr = [jnp.broadcast_to(lr_ref[pl.ds(k, 1), :], (Bl, LANES)) for k in ks]
        li = [jnp.broadcast_to(li_ref[pl.ds(k, 1), :], (Bl, LANES)) for k in ks]

        def trip(cc, carry):
            t0 = (T // CH - 1 - cc) * CH if adjoint else cc * CH
            ts = [t0 + (CH - 1 - s if adjoint else s) for s in range(CH)]
            idx = [[pl.ds(k * BT + t, Bl, stride=TP) for t in ts] for k in ks]
            loaded = [[(src_r[ix, :], src_i[ix, :]) for ix in idx[j]] for j in range(PG)]
            results, new_carry = [], []
            for j in range(PG):
                ar, ai = carry[j]
                res = []
                for s in range(CH):
                    br, bi = loaded[j][s]
                    if adjoint:
                        ar, ai = br + lr[j] * ar + li[j] * ai, bi + lr[j] * ai - li[j] * ar
                    else:
                        ar, ai = lr[j] * ar - li[j] * ai + br, lr[j] * ai + li[j] * ar + bi
                    res.append((ar, ai))
                results.append(res)
                new_carry.append((ar, ai))
            for j in range(PG):
                for s in range(CH):
                    dst_r[idx[j][s], :] = results[j][s][0]
                    dst_i[idx[j][s], :] = results[j][s][1]
            return tuple(new_carry)

        init = tuple((car_r[pl.ds(k * SUBLANES, Bl), :], car_i[pl.ds(k * SUBLANES, Bl), :]) for k in ks)
        fin = lax.fori_loop(0, T // CH, trip, init, unroll=2)
        for j, k in enumerate(ks):
            car_r[pl.ds(k * SUBLANES, Bl), :] = fin[j][0]
            car_i[pl.ds(k * SUBLANES, Bl), :] = fin[j][1]


def _s5_fwd(p3, Bm_r, Bm_i, Cm_r, Cm_i, lam_r, lam_i, dsk, carried):
    Bl, L, _ = p3.shape
    Ds = dsk.shape[1]
    G, GP, NP, T, nb = _s5_dims(Bl, L, Ds)
    nT = L // T
    TP = T + S5_ROW_PAD
    BT = Bl * TP
    dsb, gpb, npb = Ds // nb, GP // nb, NP // nb

    def body(u_ref, br_ref, bi_ref, cr_ref, ci_ref, lr_ref, li_ref, dsk_ref, sr_ref, si_ref, sb_r, sb_i, y_ref,
             car_r, car_i, upad, ypad, bu_r, bu_i):
        i = pl.program_id(0)

        @pl.when(i == 0)
        def _():
            car_r[...] = jnp.zeros_like(car_r)
            car_i[...] = jnp.zeros_like(car_i)
            upad[...] = jnp.zeros_like(upad)

        zpad = jnp.zeros((S5_ROW_PAD, LANES), F32)
        for k in range(NP):
            for b in range(Bl):
                sr_ref[pl.ds(k * BT + b * TP + T, S5_ROW_PAD), :] = zpad
                si_ref[pl.ds(k * BT + b * TP + T, S5_ROW_PAD), :] = zpad
        for b in range(Bl):
            upad[pl.ds(b * TP, T), :] = u_ref[b].astype(F32)
        u = upad[...]
        ub = u.astype(BF16)
        for blk in range(nb):
            ubb = ub[:, blk * dsb:(blk + 1) * dsb]
            for bu_ref, b_ref in ((bu_r, br_ref), (bu_i, bi_ref)):
                res = _dot(ubb, b_ref[blk])
                for kk in range(npb):
                    k = blk * npb + kk
                    bu_ref[pl.ds(k * BT, BT), :] = res[:, kk * LANES:(kk + 1) * LANES]

        _panel_scan(bu_r, bu_i, sr_ref, si_ref, lr_ref, li_ref, car_r, car_i, NP, Bl, T, TP, adjoint=False)
        sb_r[...] = car_r[...]
        sb_i[...] = car_i[...]

        for blk in range(nb):
            s_r = jnp.concatenate([sr_ref[pl.ds((blk * npb + kk) * BT, BT), :] for kk in range(npb)], axis=1).astype(BF16)
            s_i = jnp.concatenate([si_ref[pl.ds((blk * npb + kk) * BT, BT), :] for kk in range(npb)], axis=1).astype(BF16)
            cols = slice(blk * dsb, (blk + 1) * dsb)
            ypad[:, cols] = _dot_nt(s_r, cr_ref[blk]) + _dot_nt(s_i, ci_ref[blk]) + dsk_ref[:, cols] * u[:, cols]
        for b in range(Bl):
            y_ref[b] = ypad[pl.ds(b * TP, T), :]

    S = jax.ShapeDtypeStruct
    state = S((nT, NP * BT, LANES), F32)
    bound = S((nT, NP * SUBLANES, LANES), F32)
    sspec = pl.BlockSpec((None, NP * BT, LANES), lambda i: (i, 0, 0))
    bspec = pl.BlockSpec((None, NP * SUBLANES, LANES), lambda i: (i, 0, 0))
    return _call_carrying(
        body, carried, (p3, Bm_r, Bm_i, Cm_r, Cm_i, lam_r, lam_i, dsk),
        out_shape=[state, state, bound, bound, S((Bl, L, Ds), F32)], grid=(nT,),
        in_specs=[pl.BlockSpec((Bl, T, Ds), lambda i: (0, i, 0)),
                  _const_spec((nb, dsb, gpb)), _const_spec((nb, dsb, gpb)),
                  _const_spec((nb, dsb, gpb)), _const_spec((nb, dsb, gpb)),
                  _const_spec((NP, LANES)), _const_spec((NP, LANES)), _const_spec((1, Ds))],
        out_specs=[sspec, sspec, bspec, bspec, pl.BlockSpec((Bl, T, Ds), lambda i: (0, i, 0))],
        scratch_shapes=[pltpu.VMEM((NP * SUBLANES, LANES), F32)] * 2 + [pltpu.VMEM((BT, Ds), F32)] * 2
        + [pltpu.VMEM((NP * BT, LANES), F32)] * 2,
        compiler_params=_cparams(dimension_semantics=("arbitrary",)), name="s5_fwd")


def _s5_bwd(dy3, p3, dp3, Sr, Si, Sb_r, Sb_i, Bm_r, Bm_i, Cm_r, Cm_i, lam_r, lam_i, dsk, carried):
    Bl, L, Ds = dy3.shape
    G, GP, NP, T, nb = _s5_dims(Bl, L, Ds)
    nT = L // T
    TP = T + S5_ROW_PAD
    BT = Bl * TP
    dsb, gpb, npb = Ds // nb, GP // nb, NP // nb

    def body(dy_ref, u_ref, dp_ref, sr_ref, si_ref, sbr_ref, sbi_ref, br_ref, bi_ref, cr_ref, ci_ref, lr_ref, li_ref, dsk_ref,
             du_ref, dbr_ref, dbi_ref, dcr_ref, dci_ref, dlr_ref, dli_ref, ddsk_ref,
             a_r, a_i, car_r, car_i, acc_r, acc_i, dypad, upad, dupad, q_r, q_i):
        i = pl.program_id(0)

        @pl.when(i == 0)
        def _():
            for ref in (car_r, car_i, acc_r, acc_i, dbr_ref, dbi_ref, dcr_ref, dci_ref, ddsk_ref, dypad, upad, a_r, a_i):
                ref[...] = jnp.zeros_like(ref)

        for b in range(Bl):
            dypad[pl.ds(b * TP, T), :] = dy_ref[b]
            upad[pl.ds(b * TP, T), :] = u_ref[b].astype(F32)
        dy = dypad[...]
        dyb = dy.astype(BF16)
        u = upad[...]
        ub = u.astype(BF16)
        for blk in range(nb):
            dyb_b = dyb[:, blk * dsb:(blk + 1) * dsb]
            for q_ref, c_ref in ((q_r, cr_ref), (q_i, ci_ref)):
                res = _dot(dyb_b, c_ref[blk])
                for kk in range(npb):
                    q_ref[pl.ds((blk * npb + kk) * BT, BT), :] = res[:, kk * LANES:(kk + 1) * LANES]

        _panel_scan(q_r, q_i, a_r, a_i, lr_ref, li_ref, car_r, car_i, NP, Bl, T, TP, adjoint=True)

        first_block = (i == nT - 1)
        for k in range(NP):
            rows = pl.ds(k * BT, BT)
            av_r, av_i = a_r[rows, :], a_i[rows, :]
            sp_r = pltpu.roll(sr_ref[rows, :], 1, 0)
            sp_i = pltpu.roll(si_ref[rows, :], 1, 0)
            acc = pl.ds(k * SUBLANES, SUBLANES)
            acc_r[acc, :] += jnp.sum((av_r * sp_r + av_i * sp_i).reshape(BT // SUBLANES, SUBLANES, LANES), axis=0)
            acc_i[acc, :] += jnp.sum((av_i * sp_r - av_r * sp_i).reshape(BT // SUBLANES, SUBLANES, LANES), axis=0)
            t0 = pl.ds(k * BT, Bl, stride=TP)
            a0_r, a0_i = a_r[t0, :], a_i[t0, :]
            brow = pl.ds(k * SUBLANES, Bl)
            sb_pr = jnp.where(first_block, 0.0, sbr_ref[brow, :])
            sb_pi = jnp.where(first_block, 0.0, sbi_ref[brow, :])
            acc_r[brow, :] += a0_r * sb_pr + a0_i * sb_pi
            acc_i[brow, :] += a0_i * sb_pr - a0_r * sb_pi

        ddsk_ref[...] += jnp.sum(dy * u, axis=0, keepdims=True)
        for blk in range(nb):
            cols = slice(blk * dsb, (blk + 1) * dsb)
            rows = [pl.ds((blk * npb + kk) * BT, BT) for kk in range(npb)]
            av_r = jnp.concatenate([a_r[r, :] for r in rows], axis=1).astype(BF16)
            av_i = jnp.concatenate([a_i[r, :] for r in rows], axis=1).astype(BF16)
            dupad[:, cols] = _dot_nt(av_r, br_ref[blk]) + _dot_nt(av_i, bi_ref[blk]) + dy[:, cols] * dsk_ref[:, cols]
            dbr_ref[blk] += _dot_tn(ub[:, cols], av_r)
            dbi_ref[blk] += _dot_tn(ub[:, cols], av_i)
            sv_r = jnp.concatenate([sr_ref[r, :] for r in rows], axis=1).astype(BF16)
            sv_i = jnp.concatenate([si_ref[r, :] for r in rows], axis=1).astype(BF16)
            dcr_ref[blk] += _dot_tn(dyb[:, cols], sv_r)
            dci_ref[blk] += _dot_tn(dyb[:, cols], sv_i)
        for b in range(Bl):
            du_ref[b] = dupad[pl.ds(b * TP, T), :].astype(BF16)

        @pl.when(i == nT - 1)
        def _():
            for k in range(NP):
                dlr_ref[pl.ds(k, 1), :] = jnp.sum(acc_r[pl.ds(k * SUBLANES, SUBLANES), :], axis=0, keepdims=True)
                dli_ref[pl.ds(k, 1), :] = jnp.sum(acc_i[pl.ds(k * SUBLANES, SUBLANES), :], axis=0, keepdims=True)

    S = jax.ShapeDtypeStruct
    rev = lambda i: nT - 1 - i
    sspec = pl.BlockSpec((None, NP * BT, LANES), lambda i: (rev(i), 0, 0))
    bspec = pl.BlockSpec((None, NP * SUBLANES, LANES), lambda i: (jnp.maximum(rev(i) - 1, 0), 0, 0))
    tspec = pl.BlockSpec((Bl, T, Ds), lambda i: (0, rev(i), 0))
    return _call_carrying(
        body, carried, (dy3, p3, dp3, Sr, Si, Sb_r, Sb_i, Bm_r, Bm_i, Cm_r, Cm_i, lam_r, lam_i, dsk),
        out_shape=[S(dp3.shape, dp3.dtype), S((nb, dsb, gpb), F32), S((nb, dsb, gpb), F32),
                   S((nb, dsb, gpb), F32), S((nb, dsb, gpb), F32), S((NP, LANES), F32), S((NP, LANES), F32), S((1, Ds), F32)],
        grid=(nT,),
        in_specs=[tspec, tspec, ANY, sspec, sspec, bspec, bspec,
                  _const_spec((nb, dsb, gpb)), _const_spec((nb, dsb, gpb)),
                  _const_spec((nb, dsb, gpb)), _const_spec((nb, dsb, gpb)),
                  _const_spec((NP, LANES)), _const_spec((NP, LANES)), _const_spec((1, Ds))],
        out_specs=[tspec, _const_spec((nb, dsb, gpb)), _const_spec((nb, dsb, gpb)),
                   _const_spec((nb, dsb, gpb)), _const_spec((nb, dsb, gpb)),
                   _const_spec((NP, LANES)), _const_spec((NP, LANES)), _const_spec((1, Ds))],
        aliases={2: 0},
        scratch_shapes=[pltpu.VMEM((NP * BT, LANES), F32)] * 2 + [pltpu.VMEM((NP * SUBLANES, LANES), F32)] * 4
        + [pltpu.VMEM((BT, Ds), F32)] * 3 + [pltpu.VMEM((NP * BT, LANES), F32)] * 2,
        compiler_params=_cparams(dimension_semantics=("arbitrary",)), name="s5_bwd")


def _mix_values(ylin, cb, cc, cx, gs, gc, halo_v, wglu, bglu, cw, wps, wpc, wout):
    yg, gelu_vjp = jax.vjp(jax.nn.gelu, ylin)
    sz = jax.nn.sigmoid(_dot(yg.astype(BF16), wglu) + bglu)
    ys = yg * sz
    v = cc * cx
    rows = lax.broadcasted_iota(jnp.int32, v.shape, 0)
    h6 = halo_v[HALO_ROWS - 2:HALO_ROWS - 1, :]
    h7 = halo_v[HALO_ROWS - 1:HALO_ROWS, :]
    v1 = jnp.where(rows == 0, h7, pltpu.roll(v, 1, 0))
    v2 = jnp.where(rows == 0, h6, jnp.where(rows == 1, h7, pltpu.roll(v, 2, 0)))
    cv = cw[0:1, :] * v2 + cw[1:2, :] * v1 + cw[2:3, :] * v
    yc = cb * cv
    ps = _dot(ys.astype(BF16), wps)
    pc = _dot(yc.astype(BF16), wpc)
    sgs = jax.nn.sigmoid(gs)
    sgc = jax.nn.sigmoid(gc)
    merged = sgs * ps + sgc * pc
    mo = _dot(merged.astype(BF16), wout)
    return dict(yg=yg, gelu_vjp=gelu_vjp, sz=sz, ys=ys, v=v, v1=v1, v2=v2, cv=cv, yc=yc, ps=ps, pc=pc,
                sgs=sgs, sgc=sgc, merged=merged, mo=mo)


def _mix_in_specs(TB, D, Ds, Dc, bps, blk):
    hb = TB // HALO_ROWS
    halo = lambda col: pl.BlockSpec((HALO_ROWS, Dc), lambda i: (jnp.maximum(blk(i) * hb - 1, 0), col))
    return [pl.BlockSpec((TB, Dc), lambda i: (blk(i), 1)), pl.BlockSpec((TB, Dc), lambda i: (blk(i), 2)),
            pl.BlockSpec((TB, Dc), lambda i: (blk(i), 3)), pl.BlockSpec((TB, D), lambda i: (blk(i), 2)),
            pl.BlockSpec((TB, D), lambda i: (blk(i), 3)), halo(2), halo(3),
            pl.BlockSpec((TB, Ds), lambda i: (blk(i), 0))]


def _mix_fwd(p2, ylin2, x2, g1, wglu, bglu, cw, wps, wpc, wout):
    N, D = x2.shape
    Ds = ylin2.shape[1]
    Dc = Ds
    Bl = g1.shape[0]
    TB = _row_block(N // Bl, 512)
    bps = (N // Bl) // TB

    def body(cb_ref, cc_ref, cx_ref, gs_ref, gc_ref, hcc_ref, hcx_ref, yl_ref, x_ref, g1_ref,
             wglu_ref, bglu_ref, cw_ref, wps_ref, wpc_ref, wout_ref, x1_ref):
        i = pl.program_id(0)
        f32 = lambda ref: ref[...].astype(F32)
        halo_v = jnp.where(i % bps == 0, 0.0, f32(hcc_ref) * f32(hcx_ref))
        f = _mix_values(yl_ref[...], f32(cb_ref), f32(cc_ref), f32(cx_ref), f32(gs_ref), f32(gc_ref), halo_v,
                        wglu_ref[...], bglu_ref[...], cw_ref[...], wps_ref[...], wpc_ref[...], wout_ref[...])
        x1_ref[...] = x_ref[...] + g1_ref[...] * f["mo"]

    return pl.pallas_call(
        body, out_shape=jax.ShapeDtypeStruct((N, D), F32), grid=(N // TB,),
        in_specs=_mix_in_specs(TB, D, Ds, Dc, bps, lambda i: i) + [
            pl.BlockSpec((TB, D), lambda i: (i, 0)), _seq_spec(D, bps),
            _const_spec((Ds, Ds)), _const_spec((1, Ds)), _const_spec((SUBLANES, Dc)),
            _const_spec((Ds, D)), _const_spec((Dc, D)), _const_spec((D, D))],
        out_specs=pl.BlockSpec((TB, D), lambda i: (i, 0)),
        compiler_params=_cparams(), name="mix_fwd",
    )(p2, p2, p2, p2, p2, p2, p2, ylin2, x2, g1, wglu, bglu, cw, wps, wpc, wout)


def _mix_bwd(p2, ylin2, dx1, g1, wglu, bglu, cw, wps, wpc, wout, carried):
    N, D = dx1.shape
    Ds = ylin2.shape[1]
    Dc = Ds
    IN = p2.shape[1]
    Bl = g1.shape[0]
    TB = _row_block(N // Bl)
    bps = (N // Bl) // TB
    nblk = N // TB
    rev = lambda i: nblk - 1 - i

    def body(cb_ref, cc_ref, cx_ref, gs_ref, gc_ref, hcc_ref, hcx_ref, yl_ref, dx1_ref, g1_ref,
             wglu_ref, bglu_ref, cw_ref, wps_ref, wpc_ref, wout_ref,
             dyl_ref, dp_ref, gwout_ref, gwps_ref, gwpc_ref, gwglu_ref, gbglu_ref, gcw_ref, dg1_ref, nxt):
        i = pl.program_id(0)
        blk = rev(i)

        @pl.when(i == 0)
        def _():
            for ref in (gwout_ref, gwps_ref, gwpc_ref, gwglu_ref, gbglu_ref, gcw_ref):
                ref[...] = jnp.zeros_like(ref)

        @pl.when(i % bps == 0)
        def _():
            nxt[...] = jnp.zeros_like(nxt)
            dg1_ref[...] = jnp.zeros_like(dg1_ref)

        f32 = lambda ref: ref[...].astype(F32)
        cb, cc, cx = f32(cb_ref), f32(cc_ref), f32(cx_ref)
        halo_v = jnp.where(blk % bps == 0, 0.0, f32(hcc_ref) * f32(hcx_ref))
        wglu, wps, wpc, wout, cw = wglu_ref[...], wps_ref[...], wpc_ref[...], wout_ref[...], cw_ref[...]
        f = _mix_values(yl_ref[...], cb, cc, cx, f32(gs_ref), f32(gc_ref), halo_v, wglu, bglu_ref[...], cw, wps, wpc, wout)

        dx1v = dx1_ref[...]
        dg1_ref[...] += jnp.sum(dx1v * f["mo"], axis=0, keepdims=True)
        dmo = (g1_ref[...] * dx1v).astype(BF16)
        gwout_ref[...] += _dot_tn(f["merged"].astype(BF16), dmo)
        dmerged = _dot_nt(dmo, wout)
        dps = dmerged * f["sgs"]
        dpc = dmerged * f["sgc"]
        dgs = dmerged * f["ps"] * f["sgs"] * (1.0 - f["sgs"])
        dgc = dmerged * f["pc"] * f["sgc"] * (1.0 - f["sgc"])
        dpsb, dpcb = dps.astype(BF16), dpc.astype(BF16)
        gwps_ref[...] += _dot_tn(f["ys"].astype(BF16), dpsb)
        gwpc_ref[...] += _dot_tn(f["yc"].astype(BF16), dpcb)
        dys = _dot_nt(dpsb, wps)
        dyc = _dot_nt(dpcb, wpc)

        dcb = dyc * f["cv"]
        dcv = dyc * cb
        rows = lax.broadcasted_iota(jnp.int32, dcv.shape, 0)
        n0, n1 = nxt[0:1, :], nxt[1:2, :]
        d1 = jnp.where(rows == TB - 1, n0, pltpu.roll(dcv, TB - 1, 0))
        d2 = jnp.where(rows == TB - 2, n0, jnp.where(rows == TB - 1, n1, pltpu.roll(dcv, TB - 2, 0)))
        dv = cw[2:3, :] * dcv + cw[1:2, :] * d1 + cw[0:1, :] * d2
        nxt[0:2, :] = dcv[0:2, :]
        gcw_ref[0:1, :] += jnp.sum(dcv * f["v2"], axis=0, keepdims=True)
        gcw_ref[1:2, :] += jnp.sum(dcv * f["v1"], axis=0, keepdims=True)
        gcw_ref[2:3, :] += jnp.sum(dcv * f["v"], axis=0, keepdims=True)

        dz = dys * f["yg"] * f["sz"] * (1.0 - f["sz"])
        dzb = dz.astype(BF16)
        gwglu_ref[...] += _dot_tn(f["yg"].astype(BF16), dzb)
        gbglu_ref[...] += jnp.sum(dz, axis=0, keepdims=True)
        dyg = dys * f["sz"] + _dot_nt(dzb, wglu)
        dyl_ref[...] = f["gelu_vjp"](dyg)[0]

        dp_ref[:, Ds:Ds + Dc] = dcb.astype(BF16)
        dp_ref[:, Ds + Dc:Ds + 2 * Dc] = (dv * cx).astype(BF16)
        dp_ref[:, Ds + 2 * Dc:Ds + 3 * Dc] = (dv * cc).astype(BF16)
        dp_ref[:, Ds + 3 * Dc:Ds + 3 * Dc + D] = dgs.astype(BF16)
        dp_ref[:, Ds + 3 * Dc + D:IN] = dgc.astype(BF16)

    S = jax.ShapeDtypeStruct
    return _call_carrying(
        body, carried, (p2, p2, p2, p2, p2, p2, p2, ylin2, dx1, g1, wglu, bglu, cw, wps, wpc, wout),
        out_shape=[S((N, Ds), F32), S((N, IN), BF16), S((D, D), F32), S((Ds, D), F32), S((Dc, D), F32),
                   S((Ds, Ds), F32), S((1, Ds), F32), S((SUBLANES, Dc), F32), S((Bl, 1, D), F32)],
        grid=(nblk,),
        in_specs=_mix_in_specs(TB, D, Ds, Dc, bps, rev) + [
            pl.BlockSpec((TB, D), lambda i: (rev(i), 0)), _seq_spec(D, bps, nblk),
            _const_spec((Ds, Ds)), _const_spec((1, Ds)), _const_spec((SUBLANES, Dc)),
            _const_spec((Ds, D)), _const_spec((Dc, D)), _const_spec((D, D))],
        out_specs=[pl.BlockSpec((TB, Ds), lambda i: (rev(i), 0)), pl.BlockSpec((TB, IN), lambda i: (rev(i), 0)),
                   _const_spec((D, D)), _const_spec((Ds, D)), _const_spec((Dc, D)), _const_spec((Ds, Ds)),
                   _const_spec((1, Ds)), _const_spec((SUBLANES, Dc)), _seq_spec(D, bps, nblk)],
        scratch_shapes=[pltpu.VMEM((SUBLANES, Dc), F32)],
        compiler_params=_cparams(dimension_semantics=("arbitrary",)), name="mix_bwd")


def _mlp_fwd_bwd(x1, tgt, sh2, sc2, g2, n2g, fg, w1, w2):
    N, D = x1.shape
    Dff = w1.shape[1]
    Bl = sh2.shape[0]
    TB = _row_block(N // Bl)
    bps = (N // Bl) // TB

    def body(x1_ref, t_ref, sh_ref, sc_ref, g2_ref, n2_ref, fg_ref, w1_ref, w2_ref,
             dx1_ref, h2_ref, da_ref, sq_ref, df_ref, loss_ref, gfg_ref, gn2_ref, dsh_ref, dsc_ref, dg2_ref):
        i = pl.program_id(0)

        @pl.when(i == 0)
        def _():
            for ref in (loss_ref, gfg_ref, gn2_ref):
                ref[...] = jnp.zeros_like(ref)

        @pl.when(i % bps == 0)
        def _():
            for ref in (dsh_ref, dsc_ref, dg2_ref):
                ref[...] = jnp.zeros_like(ref)

        x1v = x1_ref[...]
        sc, g2v, n2 = sc_ref[...], g2_ref[...], n2_ref[...]
        xh2, r2 = _rms_fwd(x1v)
        xn2 = xh2 * n2
        h2 = (xn2 * (1.0 + sc) + sh_ref[...]).astype(BF16)
        a = _dot(h2, w1_ref[...])
        ra = jnp.maximum(a, 0.0)
        sq = (ra * ra).astype(BF16)
        fv = _dot(sq, w2_ref[...])
        x2 = x1v + g2v * fv
        xh3, r3 = _rms_fwd(x2)
        err = xh3 * fg_ref[...] - t_ref[...]
        loss_ref[...] += 0.5 * jnp.sum(jnp.mean(err * err, axis=-1, keepdims=True), axis=0, keepdims=True)
        dy = err * (1.0 / D)
        gfg_ref[...] += jnp.sum(dy * xh3, axis=0, keepdims=True)
        dx2 = _rms_bwd(dy * fg_ref[...], xh3, r3)
        dg2_ref[...] += jnp.sum(dx2 * fv, axis=0, keepdims=True)
        df = (g2v * dx2).astype(BF16)
        dsq = _dot_nt(df, w2_ref[...])
        da = (2.0 * ra * dsq).astype(BF16)
        dh2 = _dot_nt(da, w1_ref[...])
        dsh_ref[...] += jnp.sum(dh2, axis=0, keepdims=True)
        dsc_ref[...] += jnp.sum(dh2 * xn2, axis=0, keepdims=True)
        dxn2 = dh2 * (1.0 + sc)
        gn2_ref[...] += jnp.sum(dxn2 * xh2, axis=0, keepdims=True)
        dx1_ref[...] = dx2 + _rms_bwd(dxn2 * n2, xh2, r2)
        h2_ref[...] = h2
        da_ref[...] = da
        sq_ref[...] = sq
        df_ref[...] = df

    S = jax.ShapeDtypeStruct
    row = lambda w: pl.BlockSpec((TB, w), lambda i: (i, 0))
    return pl.pallas_call(
        body,
        out_shape=[S((N, D), F32), S((N, D), BF16), S((N, Dff), BF16), S((N, Dff), BF16), S((N, D), BF16),
                   S((1, 1), F32), S((1, D), F32), S((1, D), F32), S((Bl, 1, D), F32), S((Bl, 1, D), F32), S((Bl, 1, D), F32)],
        grid=(N // TB,),
        in_specs=[row(D), row(D), _seq_spec(D, bps), _seq_spec(D, bps), _seq_spec(D, bps),
                  _const_spec((1, D)), _const_spec((1, D)), _const_spec((D, Dff)), _const_spec((Dff, D))],
        out_specs=[row(D), row(D), row(Dff), row(Dff), row(D), _const_spec((1, 1)), _const_spec((1, D)), _const_spec((1, D)),
                   _seq_spec(D, bps), _seq_spec(D, bps), _seq_spec(D, bps)],
        compiler_params=_cparams(dimension_semantics=("arbitrary",)), name="mlp_fwd_bwd",
    )(x1, tgt, sh2, sc2, g2, n2g, fg, w1, w2)


_NO_EXCHANGE = _Carried((), (), {}, (), lambda ins, outs, sems: ((lambda: None), (lambda: None)))


def _grad_w(a, b, name, carried=_NO_EXCHANGE):
    N, K1 = a.shape
    K2 = b.shape[1]
    t1 = 1024 if K1 % 1024 == 0 else K1
    t2 = 1024 if K2 % 1024 == 0 else K2
    tn = 2048 if N % 2048 == 0 else N

    def body(a_ref, b_ref, o_ref):
        @pl.when(pl.program_id(2) == 0)
        def _():
            o_ref[...] = jnp.zeros_like(o_ref)

        o_ref[...] += _dot_tn(a_ref[...], b_ref[...])

    (g,), extra = _call_carrying(
        body, carried, (a, b), out_shape=[jax.ShapeDtypeStruct((K1, K2), F32)], grid=(K1 // t1, K2 // t2, N // tn),
        in_specs=[pl.BlockSpec((tn, t1), lambda i, j, k: (k, i)), pl.BlockSpec((tn, t2), lambda i, j, k: (k, j))],
        out_specs=[pl.BlockSpec((t1, t2), lambda i, j, k: (i, j))],
        compiler_params=_cparams(dimension_semantics=("arbitrary", "arbitrary", "arbitrary")), name=name)
    return g, extra


def _inproj_bwd(x2, dx1, dp, sh1, sc1, n1g, w_in, carried):
    N, D = x2.shape
    IN = w_in.shape[1]
    Bl = sh1.shape[0]
    TB = _row_block(N // Bl, 512)
    bps = (N // Bl) // TB

    def body(x_ref, dx1_ref, dp_ref, sh_ref, sc_ref, g_ref, w_ref, gx_ref, h_ref, gn1_ref, dsh_ref, dsc_ref):
        i = pl.program_id(0)

        @pl.when(i == 0)
        def _():
            gn1_ref[...] = jnp.zeros_like(gn1_ref)

        @pl.when(i % bps == 0)
        def _():
            dsh_ref[...] = jnp.zeros_like(dsh_ref)
            dsc_ref[...] = jnp.zeros_like(dsc_ref)

        sc, n1 = sc_ref[...], g_ref[...]
        xh, r = _rms_fwd(x_ref[...])
        xn = xh * n1
        h_ref[...] = (xn * (1.0 + sc) + sh_ref[...]).astype(BF16)
        dh = _dot_nt(dp_ref[...], w_ref[...])
        dsh_ref[...] += jnp.sum(dh, axis=0, keepdims=True)
        dsc_ref[...] += jnp.sum(dh * xn, axis=0, keepdims=True)
        dxn = dh * (1.0 + sc)
        gn1_ref[...] += jnp.sum(dxn * xh, axis=0, keepdims=True)
        gx_ref[...] = dx1_ref[...] + _rms_bwd(dxn * n1, xh, r)

    S = jax.ShapeDtypeStruct
    row = lambda w: pl.BlockSpec((TB, w), lambda i: (i, 0))
    return _call_carrying(
        body, carried, (x2, dx1, dp, sh1, sc1, n1g, w_in),
        out_shape=[S((N, D), F32), S((N, D), BF16), S((1, D), F32), S((Bl, 1, D), F32), S((Bl, 1, D), F32)],
        grid=(N // TB,),
        in_specs=[row(D), row(D), row(IN), _seq_spec(D, bps), _seq_spec(D, bps), _const_spec((1, D)), _const_spec((D, IN))],
        out_specs=[row(D), row(D), _const_spec((1, D)), _seq_spec(D, bps), _seq_spec(D, bps)],
        compiler_params=_cparams(dimension_semantics=("arbitrary",)), name="inproj_bwd")


def _diag_mask(gb, a, b):
    rows = lax.broadcasted_iota(jnp.int32, (gb * a, gb * b), 0) // a
    cols = lax.broadcasted_iota(jnp.int32, (gb * a, gb * b), 1) // b
    return (rows == cols).astype(F32)


def _diag_blocks_from_groups(m, nb):
    G, a, b = m.shape
    gb = G // nb
    return jnp.tile(m.reshape(nb, gb * a, b), (1, 1, gb)) * _diag_mask(gb, a, b)[None]


def _groups_from_diag_blocks(d, G, a, b):
    nb = d.shape[0]
    gb = G // nb
    picked = (d * _diag_mask(gb, a, b)[None]).reshape(nb, gb * a, gb, b)
    return jnp.sum(picked, axis=2).reshape(G, a, b)


def _pad_rows(v, rows):
    return jnp.concatenate([v, jnp.zeros((rows - v.shape[0],) + v.shape[1:], v.dtype)], axis=0)


def _pack(vs):
    flat = jnp.concatenate([v.reshape(-1) for v in vs])
    n = flat.shape[0]
    tile = SUBLANES * LANES
    npad = -(-n // tile) * tile
    flat = jnp.concatenate([flat, jnp.zeros((npad - n,), flat.dtype)])
    return flat.reshape(npad // LANES, LANES)


def _unpack(packed, shapes):
    flat = packed.reshape(-1)
    out, off = [], 0
    for s in shapes:
        n = 1
        for d in s:
            n *= d
        out.append(flat[off:off + n].reshape(s))
        off += n
    return out


def kernel(x, c, norm1_g, norm2_g, w_ada, b_ada, w_in, lam_re, lam_im, log_dt, b_re, b_im, c_re, c_im, d_skip, w_glu, b_glu, conv_w, w_proj_ssm, w_proj_conv, w_out, w_ff1, w_ff2, final_g, loss_target, m_norm1_g, m_norm2_g, m_w_ada, m_b_ada, m_w_in, m_lam_re, m_lam_im, m_log_dt, m_b_re, m_b_im, m_c_re, m_c_im, m_d_skip, m_w_glu, m_b_glu, m_conv_w, m_w_proj_ssm, m_w_proj_conv, m_w_out, m_w_ff1, m_w_ff2, m_final_g, v_norm1_g, v_norm2_g, v_w_ada, v_b_ada, v_w_in, v_lam_re, v_lam_im, v_log_dt, v_b_re, v_b_im, v_c_re, v_c_im, v_d_skip, v_w_glu, v_b_glu, v_conv_w, v_w_proj_ssm, v_w_proj_conv, v_w_out, v_w_ff1, v_w_ff2, v_final_g):
    Bl, L, D = x.shape
    N = Bl * L
    Ds = Dc = D // 2
    G, H, P = Ds // SSM_GROUP, SSM_GROUP, SSM_STATE
    GP = G * P
    NP = GP // LANES
    nb = _s5_dims(Bl, L, Ds)[4]
    IN = Ds + 3 * Dc + 2 * D
    ax, ay, ac = _mesh_pos()
    q = 2 * ax + ay
    dev = 2 * q + ac

    big_names = ["w_in", "w_ff1", "w_ff2", "w_out", "w_proj_ssm", "w_proj_conv", "w_glu"]
    big_w = dict(w_in=w_in[0], w_ff1=w_ff1[0], w_ff2=w_ff2[0], w_out=w_out[0],
                 w_proj_ssm=w_proj_ssm[0], w_proj_conv=w_proj_conv[0], w_glu=w_glu[0])
    big_axis = dict(w_in=1, w_ff1=1, w_ff2=0, w_out=0, w_proj_ssm=1, w_proj_conv=1, w_glu=0)
    axes = [big_axis[k] for k in big_names]
    shard_shapes = [big_w[k].shape for k in big_names]
    pos = jnp.stack([q, ac]).astype(jnp.int32)
    own_only = dict(zip(big_names, _cast_into_full([big_w[k] for k in big_names], axes, pos, "cast_weights")))
    Dcs = conv_w.shape[2]
    first, (w_in_full,) = _allgather8(_pack([c, conv_w[0]]), "allgather_c_w_in",
                                      _carry_allgather([own_only["w_in"]], [big_axis["w_in"]], [big_w["w_in"].shape]))
    full = {"w_in": w_in_full}
    first = first.reshape(N_DEV, -1)
    c_all = first[:, :Bl * D].reshape(N_DEV * Bl, D)
    cw = first[0::2, Bl * D:Bl * D + 3 * Dcs].reshape(N_CHIPS, 3, Dcs).transpose(1, 0, 2).reshape(3, Dc)
    cw8 = _pad_rows(cw, SUBLANES)
    Ca = w_ada.shape[2]
    b_ada_sh = lax.dynamic_slice_in_dim(b_ada, q * Ca, Ca, axis=1)
    mod_part = _ada_fwd(c_all, w_ada[0], b_ada_sh)
    mod_g = _allgather8(mod_part, "allgather_mod")
    mod_all = mod_g[0::2].transpose(1, 0, 2).reshape(N_DEV * Bl, N_CHIPS * Ca)
    mod = lax.dynamic_slice_in_dim(mod_all, dev * Bl, Bl, axis=0)
    sh1, sc1, g1, sh2, sc2, g2 = [mod[:, k * D:(k + 1) * D].reshape(Bl, 1, D) for k in range(6)]

    ldt_c = log_dt[0].reshape(G, 1)
    bt_r = b_re[0].transpose(2, 0, 1)
    bt_i = b_im[0].transpose(2, 0, 1)
    lbr, lbi, bbt_r, bbt_i = _s5_disc(lam_re[0], lam_im[0], ldt_c, bt_r, bt_i)
    lam_r_p = lbr.reshape(NP, LANES)
    lam_i_p = lbi.reshape(NP, LANES)
    Bm_r = _diag_blocks_from_groups(bbt_r.transpose(1, 0, 2), nb).astype(BF16)
    Bm_i = _diag_blocks_from_groups(bbt_i.transpose(1, 0, 2), nb).astype(BF16)
    Cm_r = _diag_blocks_from_groups(c_re[0], nb).astype(BF16)
    Cm_i = _diag_blocks_from_groups(-c_im[0], nb).astype(BF16)

    x2 = x.reshape(N, D)
    mixer_w = ["w_out", "w_proj_ssm", "w_proj_conv", "w_glu"]
    mlp_w = ["w_ff1", "w_ff2"]
    layout = lambda ks: ([big_axis[k] for k in ks], [big_w[k].shape for k in ks])
    gather = lambda ks: _carry_allgather([own_only[k] for k in ks], *layout(ks))
    p2, gathered = _inproj_fwd(x2, sh1, sc1, norm1_g, full["w_in"], gather(mixer_w))
    full.update(zip(mixer_w, gathered))
    p3 = p2.reshape(Bl, L, IN)
    (Sr, Si, Sb_r, Sb_i, ylin3), gathered = _s5_fwd(p3, Bm_r, Bm_i, Cm_r, Cm_i, lam_r_p, lam_i_p, d_skip, gather(mlp_w))
    full.update(zip(mlp_w, gathered))
    ylin2 = ylin3.reshape(N, Ds)
    mix_w = (full["w_glu"], b_glu, cw8, full["w_proj_ssm"], full["w_proj_conv"], full["w_out"])
    x1 = _mix_fwd(p2, ylin2, x2, g1, *mix_w)

    (dx1, h2b, dab, sqb, dfb, loss_p, g_fg, g_n2, dsh2, dsc2, dg2) = _mlp_fwd_bwd(
        x1, loss_target.reshape(N, D), sh2, sc2, g2, norm2_g, final_g.reshape(1, D), full["w_ff1"], full["w_ff2"])
    g_full = {"w_ff1": _grad_w(h2b, dab, "grad_w_ff1")[0], "w_ff2": _grad_w(sqb, dfb, "grad_w_ff2")[0]}

    exchange = lambda ks: _carry_sibling_exchange([g_full[k] for k in ks], *layout(ks))
    presum = lambda ks, theirs: _presum([g_full[k] for k in ks], list(theirs), *layout(ks), pos, "presum_" + ks[0])
    chip_sum = lambda ks, parts, recv: _sum_chips(list(parts), list(recv), pos, "sum_" + ks[0])

    (dyl2, dp2, gw_out, gw_ps, gw_pc, gw_glu, gb_glu, gcw8, dg1), theirs_mlp = _mix_bwd(
        p2, ylin2, dx1, g1, *mix_w, exchange(mlp_w))
    g_full.update(w_out=gw_out, w_proj_ssm=gw_ps, w_proj_conv=gw_pc, w_glu=gw_glu)
    parts_mlp = presum(mlp_w, theirs_mlp)
    (dp3, dBm_r, dBm_i, dCm_r, dCm_i, dlam_r_p, dlam_i_p, g_dsk), extra = _s5_bwd(
        dyl2.reshape(Bl, L, Ds), p3, dp2.reshape(Bl, L, IN), Sr, Si, Sb_r, Sb_i, Bm_r, Bm_i, Cm_r, Cm_i, lam_r_p, lam_i_p,
        d_skip, _carry_join(_carry_chip_scatter(parts_mlp), exchange(mixer_w)))
    recv_mlp, theirs_mix = extra[:len(mlp_w)], extra[len(mlp_w):]
    halves_mlp = chip_sum(mlp_w, parts_mlp, recv_mlp)
    parts_mix = presum(mixer_w, theirs_mix)
    dp_all = dp3.reshape(N, IN)
    (grad_x2, hb, g_n1, dsh1, dsc1), _ = _inproj_bwd(x2, dx1, dp_all, sh1, sc1, norm1_g, full["w_in"], _NO_EXCHANGE)

    dbbt_r = _groups_from_diag_blocks(dBm_r, G, H, P).transpose(1, 0, 2)
    dbbt_i = _groups_from_diag_blocks(dBm_i, G, H, P).transpose(1, 0, 2)
    dc_re = _groups_from_diag_blocks(dCm_r, G, H, P)
    dc_im = -_groups_from_diag_blocks(dCm_i, G, H, P)
    dmod = jnp.concatenate([dsh1, dsc1, dg1, dsh2, dsc2, dg2], axis=-1).reshape(Bl, 6 * D)
    small = [g_n1, g_n2, g_fg, g_dsk, gb_glu, gcw8[:3], dlam_r_p, dlam_i_p, dbbt_r, dbbt_i, dc_re, dc_im]
    small_shapes = [v.shape for v in small]
    n_small = sum(int(v.size) for v in small)
    small_slots = _place_in_slot(_pack(small + [dmod]), jnp.reshape(dev, (1,)).astype(jnp.int32), "place_small")

    g_full["w_in"], extra = _grad_w(
        hb, dp_all, "grad_w_in",
        _carry_join(_carry_join(_carry_sibling_share(halves_mlp), _carry_chip_scatter(parts_mix)), _carry_allgather8(small_slots)))
    reduced = dict(zip(mlp_w, extra[:len(mlp_w)]))
    halves_mix = chip_sum(mixer_w, parts_mix, extra[len(mlp_w):len(mlp_w) + len(mixer_w)])
    gathered = extra[-1]
    theirs_in = _run_carried(exchange(["w_in"]), "rs_exchange_w_in")
    parts_in = presum(["w_in"], theirs_in)
    recv_in = _run_carried(_carry_chip_scatter(parts_in), "rs_scatter_w_in")
    halves_in = chip_sum(["w_in"], parts_in, recv_in)
    reduced.update(zip(mixer_w + ["w_in"], _run_carried(_carry_sibling_share(halves_mix + halves_in), "rs_share_rest")))

    red = _unpack(_sum_devices(gathered, "sum_small"), small_shapes)
    (r_n1, r_n2, r_fg, r_dsk, r_bglu, r_cw, r_dlr, r_dli, r_dbr, r_dbi, r_cre, r_cim) = red
    dmod_all = gathered.reshape(N_DEV, -1)[:, n_small:n_small + Bl * 6 * D].reshape(N_DEV * Bl, 6 * D)
    gw_ada, gb_ada = _ada_bwd(c_all, lax.dynamic_slice_in_dim(dmod_all, q * Ca, Ca, axis=1), dmod_all)
    g_lr, g_li, g_ldt, g_bt_r, g_bt_i = _s5_disc_bwd(lam_re[0], lam_im[0], ldt_c, bt_r, bt_i,
                                                   r_dlr.reshape(G, P), r_dli.reshape(G, P), r_dbr, r_dbi)

    grads = dict(
        norm1_g=r_n1, norm2_g=r_n2, w_ada=gw_ada, b_ada=gb_ada, lam_re=g_lr, lam_im=g_li, log_dt=g_ldt.reshape(1, G),
        b_re=g_bt_r.transpose(1, 2, 0), b_im=g_bt_i.transpose(1, 2, 0), c_re=r_cre, c_im=r_cim, d_skip=r_dsk,
        b_glu=r_bglu, conv_w=lax.dynamic_slice_in_dim(r_cw, q * Dcs, Dcs, axis=1), final_g=r_fg, **reduced)
    weights = dict(norm1_g=norm1_g, norm2_g=norm2_g, w_ada=w_ada, b_ada=b_ada, w_in=w_in, lam_re=lam_re, lam_im=lam_im,
                   log_dt=log_dt, b_re=b_re, b_im=b_im, c_re=c_re, c_im=c_im, d_skip=d_skip, w_glu=w_glu, b_glu=b_glu,
                   conv_w=conv_w, w_proj_ssm=w_proj_ssm, w_proj_conv=w_proj_conv, w_out=w_out, w_ff1=w_ff1, w_ff2=w_ff2,
                   final_g=final_g)
    m_in = dict(norm1_g=m_norm1_g, norm2_g=m_norm2_g, w_ada=m_w_ada, b_ada=m_b_ada, w_in=m_w_in, lam_re=m_lam_re,
                lam_im=m_lam_im, log_dt=m_log_dt, b_re=m_b_re, b_im=m_b_im, c_re=m_c_re, c_im=m_c_im, d_skip=m_d_skip,
                w_glu=m_w_glu, b_glu=m_b_glu, conv_w=m_conv_w, w_proj_ssm=m_w_proj_ssm, w_proj_conv=m_w_proj_conv,
                w_out=m_w_out, w_ff1=m_w_ff1, w_ff2=m_w_ff2, final_g=m_final_g)
    v_in = dict(norm1_g=v_norm1_g, norm2_g=v_norm2_g, w_ada=v_w_ada, b_ada=v_b_ada, w_in=v_w_in, lam_re=v_lam_re,
                lam_im=v_lam_im, log_dt=v_log_dt, b_re=v_b_re, b_im=v_b_im, c_re=v_c_re, c_im=v_c_im, d_skip=v_d_skip,
                w_glu=v_w_glu, b_glu=v_b_glu, conv_w=v_conv_w, w_proj_ssm=v_w_proj_ssm, w_proj_conv=v_w_proj_conv,
                w_out=v_w_out, w_ff1=v_w_ff1, w_ff2=v_w_ff2, final_g=v_final_g)
    names = list(weights)
    grads = {k: grads[k].reshape(weights[k].shape) for k in names}

    big_upd = big_names + ["w_ada"]
    delta, new_m, new_v = {}, {}, {}
    flat2 = lambda a: a.reshape(-1, a.shape[-1])
    d_, m_, v_ = _adamw([flat2(weights[k]) for k in big_upd], [flat2(grads[k]) for k in big_upd],
                        [flat2(m_in[k]) for k in big_upd], [flat2(v_in[k]) for k in big_upd], "adamw_big")
    for k, dd, mm, vv in zip(big_upd, d_, m_, v_):
        shp = weights[k].shape
        delta[k], new_m[k], new_v[k] = dd.reshape(shp), mm.reshape(shp), vv.reshape(shp)
    small_upd = [k for k in names if k not in big_upd]
    d_, m_, v_ = _adamw_many([flat2(weights[k]) for k in small_upd], [flat2(grads[k]) for k in small_upd],
                             [flat2(m_in[k]) for k in small_upd], [flat2(v_in[k]) for k in small_upd])
    for k, dd, mm, vv in zip(small_upd, d_, m_, v_):
        shp = weights[k].shape
        delta[k], new_m[k], new_v[k] = dd.reshape(shp), mm.reshape(shp), vv.reshape(shp)

    loss = lax.psum(loss_p[0, 0], ("x", "y", "c"))
    grad_x = grad_x2.reshape(Bl, L, D)
    return (loss, grad_x, *[grads[k] for k in names], *[delta[k] for k in names],
            *[new_m[k] for k in names], *[new_v[k] for k in names])
```

```python
import functools
from typing import Callable, NamedTuple

import jax
import jax.numpy as jnp
from jax import lax
from jax.experimental import pallas as pl
from jax.experimental.pallas import tpu as pltpu

F32 = jnp.float32
BF16 = jnp.bfloat16
MESH = pl.DeviceIdType.MESH
N_CHIPS = 4
N_DEV = 8
LANES = 128
SUBLANES = 8
V7X_VMEM_BYTES = 64 * 1024 * 1024
VMEM_LIMIT = V7X_VMEM_BYTES - 6 * 1024 * 1024
SSM_GROUP = 16
SSM_STATE = 64
S5_ROW_PAD = 4
HALO_ROWS = 16
RMS_EPS = 1e-6
ADAM_LR, ADAM_B1, ADAM_B2, ADAM_EPS, ADAM_WD, ADAM_STEP = 0.001, 0.9, 0.999, 1e-08, 0.01, 10

ANY = pl.BlockSpec(memory_space=pl.ANY)
VMEM_SPEC = pl.BlockSpec(memory_space=pltpu.VMEM)


def _cparams(**kw):
    return pltpu.CompilerParams(vmem_limit_bytes=VMEM_LIMIT, **kw)


def _dot(a, b):
    return jnp.dot(a, b, preferred_element_type=F32)


def _dot_nt(a, b):
    return lax.dot_general(a, b, (((1,), (1,)), ((), ())), preferred_element_type=F32)


def _dot_tn(a, b):
    return lax.dot_general(a, b, (((0,), (0,)), ((), ())), preferred_element_type=F32)


def _mesh_pos():
    return lax.axis_index("x"), lax.axis_index("y"), lax.axis_index("c")


def _allgather8(v, name, carried=None):
    r, c = v.shape

    def body(x_ref, out_ref, send_sems, recv_sems, local_sem):
        x, y, cc = _mesh_pos()
        me, sibling = (x, y, cc), (x, y, 1 - cc)
        chips = [(1 - x, y), (x, 1 - y), (1 - x, 1 - y)]

        def slot(px, py, pc):
            return out_ref.at[4 * px + 2 * py + pc]

        def copy(k, block, to, src=None):
            return pltpu.make_async_remote_copy(
                src_ref=slot(*block) if src is None else src, dst_ref=slot(*block),
                send_sem=send_sems.at[k], recv_sem=recv_sems.at[k], device_id=to, device_id_type=MESH)

        mine = pltpu.make_async_copy(x_ref, slot(*me), local_sem)
        mine.start()
        first = [copy(0, me, sibling, src=x_ref)]
        first += [copy(1 + j, me, (*chip, cc), src=x_ref) for j, chip in enumerate(chips)]
        for cp in first:
            cp.start()
        passed = [copy(4 + j, (*chip, cc), sibling) for j, chip in enumerate(chips)]
        for j, chip in enumerate(chips):
            copy(1 + j, (*chip, cc), me).wait_recv()
            passed[j].start()
        copy(0, sibling, me).wait_recv()
        for j, chip in enumerate(chips):
            copy(4 + j, (*chip, 1 - cc), me).wait_recv()
        for cp in first + passed:
            cp.wait_send()
        mine.wait()

    sems = [pltpu.SemaphoreType.DMA((7,)), pltpu.SemaphoreType.DMA((7,)), pltpu.SemaphoreType.DMA]
    out_shape = jax.ShapeDtypeStruct((N_DEV, r, c), v.dtype)
    if carried is None:
        return pl.pallas_call(body, out_shape=out_shape, in_specs=[VMEM_SPEC], out_specs=VMEM_SPEC,
                              scratch_shapes=sems, name=name)(v)
    (out,), extra = _call_carrying(body, carried, (v,), out_shape=[out_shape], in_specs=[VMEM_SPEC],
                                   out_specs=[VMEM_SPEC], scratch_shapes=sems, name=name)
    return out, extra


def _shard_region(ref, axis, shard_shape, q, half):
    R, C = shard_shape
    r0, nr = (0, R) if half is None else (half * (R // 2), R // 2)
    if axis == 1:
        return ref.at[pl.ds(r0, nr), pl.ds(q * C, C)]
    return ref.at[pl.ds(q * R + r0, nr), :]


class _Carried(NamedTuple):
    inputs: tuple
    out_shapes: tuple
    aliases: dict
    sems: tuple
    steps: Callable


def _carry_join(a, b):
    na_i, na_o, na_s = len(a.inputs), len(a.out_shapes), len(a.sems)

    def steps(ins, outs, sems):
        sa, fa = a.steps(ins[:na_i], outs[:na_o], sems[:na_s])
        sb, fb = b.steps(ins[na_i:], outs[na_o:], sems[na_s:])


        def start():
            sa()
            sb()

        def finish():
            fa()
            fb()

        return start, finish

    aliases = dict(a.aliases)
    aliases.update({na_i + i: na_o + o for i, o in b.aliases.items()})
    return _Carried(a.inputs + b.inputs, a.out_shapes + b.out_shapes, aliases, a.sems + b.sems, steps)


def _call_carrying(body, carried, args, *, out_shape, in_specs, out_specs, scratch_shapes=(), grid=None, aliases=None,
                   name, **kw):
    n_in, n_out, n_sc = len(in_specs), len(out_specs), len(scratch_shapes)
    n_ci, n_co = len(carried.inputs), len(carried.out_shapes)

    def wrapped(*refs):
        ins, refs = refs[:n_in], refs[n_in:]
        c_ins, refs = refs[:n_ci], refs[n_ci:]
        outs, refs = refs[:n_out], refs[n_out:]
        c_outs, refs = refs[:n_co], refs[n_co:]
        scratch, c_sems = refs[:n_sc], refs[n_sc:]
        start, finish = carried.steps(c_ins, c_outs, c_sems)
        if grid is None:
            start()
            body(*ins, *outs, *scratch)
            finish()
        else:
            ids = [pl.program_id(d) for d in range(len(grid))]
            first = functools.reduce(jnp.logical_and, [i == 0 for i in ids])
            last = functools.reduce(jnp.logical_and, [i == g - 1 for i, g in zip(ids, grid)])
            pl.when(first)(start)
            body(*ins, *outs, *scratch)
            pl.when(last)(finish)

    if grid is not None:
        kw["grid"] = grid
    io_aliases = dict(aliases or {})
    io_aliases.update({n_in + i: n_out + o for i, o in carried.aliases.items()})
    res = pl.pallas_call(
        wrapped, out_shape=list(out_shape) + list(carried.out_shapes),
        in_specs=list(in_specs) + [ANY] * n_ci, out_specs=list(out_specs) + [ANY] * n_co,
        scratch_shapes=list(scratch_shapes) + list(carried.sems),
        input_output_aliases=io_aliases, name=name, **kw,
    )(*args, *carried.inputs)
    return res[:n_out], res[n_out:]


def _run_carried(carried, name):
    return _call_carrying(lambda: None, carried, (), out_shape=(), in_specs=(), out_specs=(), name=name)[1]


def _place_in_slot(v, dev_arr, name):
    r, c = v.shape

    def body(d_ref, v_ref, o_ref):
        o_ref[...] = v_ref[...]

    return pl.pallas_call(
        body, out_shape=jax.ShapeDtypeStruct((N_DEV, r, c), v.dtype),
        grid_spec=pltpu.PrefetchScalarGridSpec(
            num_scalar_prefetch=1, grid=(1,), in_specs=[pl.BlockSpec((r, c), lambda i, d: (0, 0))],
            out_specs=pl.BlockSpec((None, r, c), lambda i, d: (d[0], 0, 0))),
        name=name)(dev_arr, v)


def _carry_allgather8(buf):
    def steps(ins, outs, sems):
        send_s, recv_s = sems
        out = outs[0]
        x, y, cc = _mesh_pos()
        me, sibling = (x, y, cc), (x, y, 1 - cc)
        chips = [(1 - x, y), (x, 1 - y), (1 - x, 1 - y)]

        def copy(k, block, to):
            px, py, pc = block
            slot = out.at[4 * px + 2 * py + pc]
            return pltpu.make_async_remote_copy(src_ref=slot, dst_ref=slot, send_sem=send_s.at[k], recv_sem=recv_s.at[k],
                                                device_id=to, device_id_type=MESH)

        first = [copy(0, me, sibling)] + [copy(1 + j, me, (*chip, cc)) for j, chip in enumerate(chips)]
        passed = [copy(4 + j, (*chip, cc), sibling) for j, chip in enumerate(chips)]

        def start():
            for cp in first:
                cp.start()

        def finish():
            for j, chip in enumerate(chips):
                copy(1 + j, (*chip, cc), me).wait_recv()
                passed[j].start()
            copy(0, sibling, me).wait_recv()
            for j, chip in enumerate(chips):
                copy(4 + j, (*chip, 1 - cc), me).wait_recv()
            for cp in first + passed:
                cp.wait_send()

        return start, finish

    return _Carried((buf,), (jax.ShapeDtypeStruct(buf.shape, buf.dtype),), {0: 0}, (pltpu.SemaphoreType.DMA((7,)),) * 2, steps)


def _carry_allgather(fulls, axes, shapes):
    n = len(fulls)
    return _Carried(tuple(fulls), tuple(jax.ShapeDtypeStruct(f.shape, f.dtype) for f in fulls),
                    {i: i for i in range(n)}, (pltpu.SemaphoreType.DMA((3 * n,)),) * 4,
                    lambda ins, outs, sems: _allgather_weights_steps(outs, axes, shapes, *sems))


def _allgather_weights_steps(outs, axes, shapes, send_s, recv_s, fsend_s, frecv_s):
    n = len(outs)
    x, y, c = _mesh_pos()
    q = 2 * x + y
    sibling = (x, y, 1 - c)
    chips = [(1 - x, y), (x, 1 - y), (1 - x, 1 - y)]

    def region(i, qq, half):
        return _shard_region(outs[i], axes[i], shapes[i], qq, half)

    def remote(src, dst, ss, rs, to):
        return pltpu.make_async_remote_copy(src_ref=src, dst_ref=dst, send_sem=ss, recv_sem=rs,
                                            device_id=to, device_id_type=MESH)

    def ici(i, j, qq):
        cx, cy = chips[j]
        reg = region(i, qq, c)
        return remote(reg, reg, send_s.at[3 * i + j], recv_s.at[3 * i + j], (cx, cy, c))

    def d2d(i, j, half):
        cx, cy = chips[j]
        reg = region(i, 2 * cx + cy, half)
        return remote(reg, reg, fsend_s.at[3 * i + j], frecv_s.at[3 * i + j], sibling)

    def start():
        for i in range(n):
            for j in range(3):
                ici(i, j, q).start()

    def finish():
        for i in range(n):
            for j, (cx, cy) in enumerate(chips):
                ici(i, j, 2 * cx + cy).wait_recv()
                d2d(i, j, c).start()
        for i in range(n):
            for j in range(3):
                d2d(i, j, 1 - c).wait_recv()
        for i in range(n):
            for j in range(3):
                ici(i, j, q).wait_send()
                d2d(i, j, c).wait_send()

    return start, finish


def _carry_sibling_exchange(grads, axes, shapes):
    n = len(grads)

    def steps(ins, theirs, sems):
        send_s, recv_s = sems
        x, y, c = _mesh_pos()

        def copies():
            return [pltpu.make_async_remote_copy(
                src_ref=_shard_region(ins[i], axes[i], shapes[i], qq, 1 - c), dst_ref=theirs[i].at[qq],
                send_sem=send_s.at[N_CHIPS * i + qq], recv_sem=recv_s.at[N_CHIPS * i + qq],
                device_id=(x, y, 1 - c), device_id_type=MESH) for i in range(n) for qq in range(N_CHIPS)]

        def start():
            for cp in copies():
                cp.start()

        def finish():
            for cp in copies():
                cp.wait()

        return start, finish

    stacked = tuple(jax.ShapeDtypeStruct((N_CHIPS, R // 2, C), F32) for (R, C) in shapes)
    return _Carried(tuple(grads), stacked, {}, (pltpu.SemaphoreType.DMA((N_CHIPS * n,)),) * 2, steps)


def _carry_chip_scatter(parts):
    n = len(parts)

    def steps(ins, outs, sems):
        send_s, recv_s = sems
        x, y, c = _mesh_pos()
        chips = [(1 - x, y), (x, 1 - y), (1 - x, 1 - y)]

        def copies():
            return [pltpu.make_async_remote_copy(
                src_ref=ins[i].at[2 * cx + cy], dst_ref=outs[i].at[j],
                send_sem=send_s.at[3 * i + j], recv_sem=recv_s.at[3 * i + j],
                device_id=(cx, cy, c), device_id_type=MESH) for i in range(n) for j, (cx, cy) in enumerate(chips)]

        def start():
            for cp in copies():
                cp.start()

        def finish():
            for cp in copies():
                cp.wait()

        return start, finish

    return _Carried(tuple(parts), tuple(jax.ShapeDtypeStruct((3,) + p.shape[1:], p.dtype) for p in parts), {},
                    (pltpu.SemaphoreType.DMA((3 * n,)),) * 2, steps)


def _carry_sibling_share(fulls):
    n = len(fulls)

    def steps(ins, outs, sems):
        send_s, recv_s = sems
        x, y, c = _mesh_pos()

        def copy(i, half):
            rh = fulls[i].shape[0] // 2
            rows = outs[i].at[pl.ds(half * rh, rh), :]
            return pltpu.make_async_remote_copy(src_ref=rows, dst_ref=rows, send_sem=send_s.at[i], recv_sem=recv_s.at[i],
                                                device_id=(x, y, 1 - c), device_id_type=MESH)

        def start():
            for i in range(n):
                copy(i, c).start()

        def finish():
            for i in range(n):
                copy(i, 1 - c).wait_recv()
                copy(i, c).wait_send()

        return start, finish

    return _Carried(tuple(fulls), tuple(jax.ShapeDtypeStruct(f.shape, f.dtype) for f in fulls),
                    {i: i for i in range(n)}, (pltpu.SemaphoreType.DMA((n,)),) * 2, steps)


def _row_block(rows, target=256):
    return target if rows % target == 0 else rows


BF16_TILE_ROWS = 16


def _common_steps(rows, most=8):
    ns = most
    while ns > 1 and any(r % (ns * BF16_TILE_ROWS) for r in rows):
        ns //= 2
    return ns


def _cast_into_full(ws, axes, pos, name):
    n = len(ws)
    ns = _common_steps([w.shape[0] for w in ws])
    in_specs, out_specs, out_shape = [], [], []
    for w, axis in zip(ws, axes):
        R, C = w.shape
        in_specs.append(pl.BlockSpec((R // ns, C), lambda i, s: (i, 0)))
        if axis == 1:
            out_shape.append(jax.ShapeDtypeStruct((R, N_CHIPS * C), BF16))
            out_specs.append(pl.BlockSpec((R // ns, C), lambda i, s: (i, s[0])))
        else:
            out_shape.append(jax.ShapeDtypeStruct((N_CHIPS * R, C), BF16))
            out_specs.append(pl.BlockSpec((R // ns, C), lambda i, s: (s[0] * ns + i, 0)))

    def body(s_ref, *refs):
        for k in range(n):
            refs[n + k][...] = refs[k][...].astype(BF16)

    return pl.pallas_call(
        body, out_shape=out_shape,
        grid_spec=pltpu.PrefetchScalarGridSpec(num_scalar_prefetch=1, grid=(ns,), in_specs=in_specs, out_specs=out_specs),
        compiler_params=_cparams(), name=name)(pos, *ws)


def _presum(gs, theirs, axes, shapes, pos, name):
    n = len(gs)
    ns = _common_steps([R // 2 for R, _ in shapes], most=2)
    in_specs, t_specs, out_shape = [], [], []
    for (R, C), axis in zip(shapes, axes):
        rb = R // 2 // ns
        if axis == 1:
            in_specs.append(pl.BlockSpec((rb, C), lambda k, i, s: (s[1] * ns + i, k)))
        else:
            in_specs.append(pl.BlockSpec((rb, C), lambda k, i, s: (k * 2 * ns + s[1] * ns + i, 0)))
        t_specs.append(pl.BlockSpec((None, rb, C), lambda k, i, s: (k, i, 0)))
        out_shape.append(jax.ShapeDtypeStruct((N_CHIPS, R // 2, C), BF16))

    def body(s_ref, *refs):
        for k in range(n):
            refs[2 * n + k][...] = (refs[k][...] + refs[n + k][...]).astype(BF16)

    return pl.pallas_call(
        body, out_shape=out_shape,
        grid_spec=pltpu.PrefetchScalarGridSpec(num_scalar_prefetch=1, grid=(N_CHIPS, ns), in_specs=in_specs + t_specs,
                                               out_specs=t_specs),
        compiler_params=_cparams(), name=name)(pos, *gs, *theirs)


def _sum_chips(owns, recvs, pos, name):
    n = len(owns)
    ns = _common_steps([o.shape[1] for o in owns], most=2)
    o_specs, r_specs, out_specs, out_shape = [], [], [], []
    for o in owns:
        _, Rh, C = o.shape
        rb = Rh // ns
        o_specs.append(pl.BlockSpec((None, rb, C), lambda i, s: (s[0], i, 0)))
        r_specs.append(pl.BlockSpec((3, rb, C), lambda i, s: (0, i, 0)))
        out_specs.append(pl.BlockSpec((rb, C), lambda i, s: (s[1] * ns + i, 0)))
        out_shape.append(jax.ShapeDtypeStruct((2 * Rh, C), F32))

    def body(s_ref, *refs):
        for k in range(n):
            acc = refs[k][...].astype(F32)
            for j in range(3):
                acc = acc + refs[n + k][j].astype(F32)
            refs[2 * n + k][...] = acc

    return pl.pallas_call(
        body, out_shape=out_shape,
        grid_spec=pltpu.PrefetchScalarGridSpec(num_scalar_prefetch=1, grid=(ns,), in_specs=o_specs + r_specs,
                                               out_specs=out_specs),
        compiler_params=_cparams(), name=name)(pos, *owns, *recvs)


def _sum_devices(parts, name):
    K, R, C = parts.shape

    def body(p_ref, o_ref):
        acc = p_ref[0]
        for k in range(1, K):
            acc = acc + p_ref[k]
        o_ref[...] = acc

    return pl.pallas_call(body, out_shape=jax.ShapeDtypeStruct((R, C), F32), name=name)(parts)


def _adamw_math(w, g, m, v):
    nm = ADAM_B1 * m + (1.0 - ADAM_B1) * g
    nv = ADAM_B2 * v + (1.0 - ADAM_B2) * (g * g)
    m_hat = nm / (1.0 - ADAM_B1 ** ADAM_STEP)
    v_hat = nv / (1.0 - ADAM_B2 ** ADAM_STEP)
    return -ADAM_LR * (m_hat / (jnp.sqrt(v_hat) + ADAM_EPS) + ADAM_WD * w), nm, nv


def _adamw_many(ws, gs, ms, vs):
    n = len(ws)

    def body(*refs):
        for k in range(n):
            w, g, m, v = (refs[j * n + k][...] for j in range(4))
            for j, val in enumerate(_adamw_math(w, g, m, v)):
                refs[(4 + j) * n + k][...] = val

    shapes = [jax.ShapeDtypeStruct(w.shape, F32) for w in ws]
    res = pl.pallas_call(body, out_shape=shapes * 3, compiler_params=_cparams(), name="adamw_small")(*ws, *gs, *ms, *vs)
    return res[:n], res[n:2 * n], res[2 * n:]


def _adamw(ws, gs, ms, vs, name):
    n = len(ws)
    ns = _common_steps([w.shape[0] for w in ws])
    specs = [pl.BlockSpec((w.shape[0] // ns, w.shape[1]), lambda i: (i, 0)) for w in ws]

    def body(*refs):
        for k in range(n):
            w, g, m, v = (refs[j * n + k][...] for j in range(4))
            for j, val in enumerate(_adamw_math(w, g, m, v)):
                refs[(4 + j) * n + k][...] = val

    shapes = [jax.ShapeDtypeStruct(w.shape, F32) for w in ws]
    res = pl.pallas_call(body, out_shape=shapes * 3, grid=(ns,), in_specs=specs * 4, out_specs=specs * 3,
                         compiler_params=_cparams(), name=name)(*ws, *gs, *ms, *vs)
    return res[:n], res[n:2 * n], res[2 * n:]


def _silu(v):
    return v * jax.nn.sigmoid(v)


def _ada_fwd(c_all, w_sh, b_sh):
    S, D = c_all.shape
    Ca = w_sh.shape[1]
    cb = 512 if Ca % 512 == 0 else Ca

    def body(c_ref, w_ref, b_ref, o_ref):
        act = _silu(c_ref[...]).astype(BF16)
        o_ref[...] = _dot(act, w_ref[...].astype(BF16)) + b_ref[...]

    return pl.pallas_call(
        body, out_shape=jax.ShapeDtypeStruct((S, Ca), F32), grid=(Ca // cb,),
        in_specs=[pl.BlockSpec((S, D), lambda j: (0, 0)), pl.BlockSpec((D, cb), lambda j: (0, j)),
                  pl.BlockSpec((1, cb), lambda j: (0, j))],
        out_specs=pl.BlockSpec((S, cb), lambda j: (0, j)), name="ada_fwd")(c_all, w_sh, b_sh)


def _ada_bwd(c_all, dmod_sh, dmod_all):
    S, D = c_all.shape
    Ca = dmod_sh.shape[1]
    C6 = dmod_all.shape[1]

    def body(c_ref, ds_ref, da_ref, gw_ref, gb_ref):
        act = _silu(c_ref[...]).astype(BF16)
        gw_ref[...] = _dot_tn(act, ds_ref[...].astype(BF16))
        gb_ref[...] = jnp.sum(da_ref[...], axis=0, keepdims=True)

    return pl.pallas_call(
        body, out_shape=[jax.ShapeDtypeStruct((D, Ca), F32), jax.ShapeDtypeStruct((1, C6), F32)],
        compiler_params=_cparams(), name="ada_bwd")(c_all, dmod_sh, dmod_all)


def _rms_fwd(xv):
    r = lax.rsqrt(jnp.mean(xv * xv, axis=-1, keepdims=True) + RMS_EPS)
    return xv * r, r


def _rms_bwd(dxh, xh, r):
    return r * (dxh - xh * jnp.mean(dxh * xh, axis=-1, keepdims=True))


def _const_spec(shape):
    nd = len(shape)
    return pl.BlockSpec(shape, lambda *_: (0,) * nd)


def _seq_spec(D, bps, rev_blocks=None):
    if rev_blocks is None:
        return pl.BlockSpec((None, 1, D), lambda i: (i // bps, 0, 0))
    return pl.BlockSpec((None, 1, D), lambda i: ((rev_blocks - 1 - i) // bps, 0, 0))


def _inproj_fwd(x2, sh1, sc1, n1g, w_halves, carried):
    N, D = x2.shape
    Dh, IN = w_halves[0].shape
    Bl = sh1.shape[0]
    TB = _row_block(N // Bl, 512)
    bps = (N // Bl) // TB

    def body(x_ref, sh_ref, sc_ref, g_ref, wt_ref, wb_ref, p_ref):
        xh, _ = _rms_fwd(x_ref[...])
        h = ((xh * g_ref[...]) * (1.0 + sc_ref[...]) + sh_ref[...]).astype(BF16)
        p_ref[...] = (_dot(h[:, :Dh], wt_ref[...]) + _dot(h[:, Dh:], wb_ref[...])).astype(BF16)

    (p2,), extra = _call_carrying(
        body, carried, (x2, sh1, sc1, n1g, *w_halves), out_shape=[jax.ShapeDtypeStruct((N, IN), BF16)], grid=(N // TB,),
        in_specs=[pl.BlockSpec((TB, D), lambda i: (i, 0)), _seq_spec(D, bps), _seq_spec(D, bps),
                  _const_spec((1, D)), _const_spec((Dh, IN)), _const_spec((Dh, IN))],
        out_specs=[pl.BlockSpec((TB, IN), lambda i: (i, 0))],
        compiler_params=_cparams(dimension_semantics=("arbitrary",)), name="inproj_fwd")
    return p2, extra


def _s5_dims(Bl, L, Ds):
    G = Ds // SSM_GROUP
    GP = G * SSM_STATE
    NP = GP // LANES
    T = min(64, L // 2)
    nb = 2 if (Ds // 2) % LANES == 0 else 1
    return G, GP, NP, T, nb


def _s5_disc_math(lr, li, ldt, bt_r, bt_i):
    dt = jnp.exp(ldt)
    er = jnp.exp(lr * dt)
    lbr = er * jnp.cos(li * dt)
    lbi = er * jnp.sin(li * dt)
    den = lr * lr + li * li
    fr = ((lbr - 1.0) * lr + lbi * li) / den
    fi = (lbi * lr - (lbr - 1.0) * li) / den
    return lbr, lbi, fr[None] * bt_r - fi[None] * bt_i, fr[None] * bt_i + fi[None] * bt_r


def _s5_disc(lr, li, ldt, bt_r, bt_i):
    def body(lr_ref, li_ref, ldt_ref, br_ref, bi_ref, o0, o1, o2, o3):
        res = _s5_disc_math(lr_ref[...], li_ref[...], ldt_ref[...], br_ref[...], bi_ref[...])
        for o, v in zip((o0, o1, o2, o3), res):
            o[...] = v

    S = jax.ShapeDtypeStruct
    return pl.pallas_call(body, out_shape=[S(lr.shape, F32)] * 2 + [S(bt_r.shape, F32)] * 2, name="s5_disc")(lr, li, ldt, bt_r, bt_i)


def _s5_disc_bwd(lr, li, ldt, bt_r, bt_i, dlbr, dlbi, dbbr, dbbi):
    def body(lr_ref, li_ref, ldt_ref, br_ref, bi_ref, g0, g1, g2, g3, o0, o1, o2, o3, o4):
        _, vjp = jax.vjp(_s5_disc_math, lr_ref[...], li_ref[...], ldt_ref[...], br_ref[...], bi_ref[...])
        res = vjp((g0[...], g1[...], g2[...], g3[...]))
        for o, v in zip((o0, o1, o2, o3, o4), res):
            o[...] = v

    S = jax.ShapeDtypeStruct
    return pl.pallas_call(body, out_shape=[S(lr.shape, F32)] * 2 + [S(ldt.shape, F32)] + [S(bt_r.shape, F32)] * 2,
                          name="s5_disc_bwd")(lr, li, ldt, bt_r, bt_i, dlbr, dlbi, dbbr, dbbi)


S5_SCAN_PANELS = 4
S5_SCAN_STEPS = 4


def _panel_scan(src_r, src_i, dst_r, dst_i, lr_ref, li_ref, car_r, car_i, NP, Bl, T, TP, adjoint):
    BT = Bl * TP
    PG = min(S5_SCAN_PANELS, NP)
    CH = S5_SCAN_STEPS
    for k0 in range(0, NP, PG):
        ks = list(range(k0, k0 + PG))
        lr = [jnp.broadcast_to(lr_ref[pl.ds(k, 1), :], (Bl, LANES)) for k in ks]
        li = [jnp.broadcast_to(li_ref[pl.ds(k, 1), :], (Bl, LANES)) for k in ks]

        def trip(cc, carry):
            t0 = (T // CH - 1 - cc) * CH if adjoint else cc * CH
            ts = [t0 + (CH - 1 - s if adjoint else s) for s in range(CH)]
            idx = [[pl.ds(k * BT + t, Bl, stride=TP) for t in ts] for k in ks]
            loaded = [[(src_r[ix, :], src_i[ix, :]) for ix in idx[j]] for j in range(PG)]
            results, new_carry = [], []
            for j in range(PG):
                ar, ai = carry[j]
                res = []
                for s in range(CH):
                    br, bi = loaded[j][s]
                    if adjoint:
                        ar, ai = br + lr[j] * ar + li[j] * ai, bi + lr[j] * ai - li[j] * ar
                    else:
                        ar, ai = lr[j] * ar - li[j] * ai + br, lr[j] * ai + li[j] * ar + bi
                    res.append((ar, ai))
                results.append(res)
                new_carry.append((ar, ai))
            for j in range(PG):
                for s in range(CH):
                    dst_r[idx[j][s], :] = results[j][s][0]
                    dst_i[idx[j][s], :] = results[j][s][1]
            return tuple(new_carry)

        init = tuple((car_r[pl.ds(k * SUBLANES, Bl), :], car_i[pl.ds(k * SUBLANES, Bl), :]) for k in ks)
        fin = lax.fori_loop(0, T // CH, trip, init, unroll=2)
        for j, k in enumerate(ks):
            car_r[pl.ds(k * SUBLANES, Bl), :] = fin[j][0]
            car_i[pl.ds(k * SUBLANES, Bl), :] = fin[j][1]


def _s5_fwd(p3, Bm_r, Bm_i, Cm_r, Cm_i, lam_r, lam_i, dsk, carried):
    Bl, L, _ = p3.shape
    Ds = dsk.shape[1]
    G, GP, NP, T, nb = _s5_dims(Bl, L, Ds)
    nT = L // T
    TP = T + S5_ROW_PAD
    BT = Bl * TP
    dsb, gpb, npb = Ds // nb, GP // nb, NP // nb

    def body(u_ref, br_ref, bi_ref, cr_ref, ci_ref, lr_ref, li_ref, dsk_ref, sr_ref, si_ref, sb_r, sb_i, y_ref,
             car_r, car_i, upad, ypad, bu_r, bu_i):
        i = pl.program_id(0)

        @pl.when(i == 0)
        def _():
            car_r[...] = jnp.zeros_like(car_r)
            car_i[...] = jnp.zeros_like(car_i)
            upad[...] = jnp.zeros_like(upad)

        zpad = jnp.zeros((S5_ROW_PAD, LANES), F32)
        for k in range(NP):
            for b in range(Bl):
                sr_ref[pl.ds(k * BT + b * TP + T, S5_ROW_PAD), :] = zpad
                si_ref[pl.ds(k * BT + b * TP + T, S5_ROW_PAD), :] = zpad
        for b in range(Bl):
            upad[pl.ds(b * TP, T), :] = u_ref[b].astype(F32)
        u = upad[...]
        ub = u.astype(BF16)
        for blk in range(nb):
            ubb = ub[:, blk * dsb:(blk + 1) * dsb]
            for bu_ref, b_ref in ((bu_r, br_ref), (bu_i, bi_ref)):
                res = _dot(ubb, b_ref[blk])
                for kk in range(npb):
                    k = blk * npb + kk
                    bu_ref[pl.ds(k * BT, BT), :] = res[:, kk * LANES:(kk + 1) * LANES]

        _panel_scan(bu_r, bu_i, sr_ref, si_ref, lr_ref, li_ref, car_r, car_i, NP, Bl, T, TP, adjoint=False)
        sb_r[...] = car_r[...]
        sb_i[...] = car_i[...]

        for blk in range(nb):
            s_r = jnp.concatenate([sr_ref[pl.ds((blk * npb + kk) * BT, BT), :] for kk in range(npb)], axis=1).astype(BF16)
            s_i = jnp.concatenate([si_ref[pl.ds((blk * npb + kk) * BT, BT), :] for kk in range(npb)], axis=1).astype(BF16)
            cols = slice(blk * dsb, (blk + 1) * dsb)
            ypad[:, cols] = _dot_nt(s_r, cr_ref[blk]) + _dot_nt(s_i, ci_ref[blk]) + dsk_ref[:, cols] * u[:, cols]
        for b in range(Bl):
            y_ref[b] = ypad[pl.ds(b * TP, T), :]

    S = jax.ShapeDtypeStruct
    state = S((nT, NP * BT, LANES), F32)
    bound = S((nT, NP * SUBLANES, LANES), F32)
    sspec = pl.BlockSpec((None, NP * BT, LANES), lambda i: (i, 0, 0))
    bspec = pl.BlockSpec((None, NP * SUBLANES, LANES), lambda i: (i, 0, 0))
    return _call_carrying(
        body, carried, (p3, Bm_r, Bm_i, Cm_r, Cm_i, lam_r, lam_i, dsk),
        out_shape=[state, state, bound, bound, S((Bl, L, Ds), F32)], grid=(nT,),
        in_specs=[pl.BlockSpec((Bl, T, Ds), lambda i: (0, i, 0)),
                  _const_spec((nb, dsb, gpb)), _const_spec((nb, dsb, gpb)),
                  _const_spec((nb, dsb, gpb)), _const_spec((nb, dsb, gpb)),
                  _const_spec((NP, LANES)), _const_spec((NP, LANES)), _const_spec((1, Ds))],
        out_specs=[sspec, sspec, bspec, bspec, pl.BlockSpec((Bl, T, Ds), lambda i: (0, i, 0))],
        scratch_shapes=[pltpu.VMEM((NP * SUBLANES, LANES), F32)] * 2 + [pltpu.VMEM((BT, Ds), F32)] * 2
        + [pltpu.VMEM((NP * BT, LANES), F32)] * 2,
        compiler_params=_cparams(dimension_semantics=("arbitrary",)), name="s5_fwd")


def _s5_bwd(dy3, p3, dp3, Sr, Si, Sb_r, Sb_i, Bm_r, Bm_i, Cm_r, Cm_i, lam_r, lam_i, dsk, carried):
    Bl, L, Ds = dy3.shape
    G, GP, NP, T, nb = _s5_dims(Bl, L, Ds)
    nT = L // T
    TP = T + S5_ROW_PAD
    BT = Bl * TP
    dsb, gpb, npb = Ds // nb, GP // nb, NP // nb

    def body(dy_ref, u_ref, dp_ref, sr_ref, si_ref, sbr_ref, sbi_ref, br_ref, bi_ref, cr_ref, ci_ref, lr_ref, li_ref, dsk_ref,
             du_ref, dbr_ref, dbi_ref, dcr_ref, dci_ref, dlr_ref, dli_ref, ddsk_ref,
             a_r, a_i, car_r, car_i, acc_r, acc_i, dypad, upad, dupad, q_r, q_i):
        i = pl.program_id(0)

        @pl.when(i == 0)
        def _():
            for ref in (car_r, car_i, acc_r, acc_i, dbr_ref, dbi_ref, dcr_ref, dci_ref, ddsk_ref, dypad, upad, a_r, a_i):
                ref[...] = jnp.zeros_like(ref)

        for b in range(Bl):
            dypad[pl.ds(b * TP, T), :] = dy_ref[b]
            upad[pl.ds(b * TP, T), :] = u_ref[b].astype(F32)
        dy = dypad[...]
        dyb = dy.astype(BF16)
        u = upad[...]
        ub = u.astype(BF16)
        for blk in range(nb):
            dyb_b = dyb[:, blk * dsb:(blk + 1) * dsb]
            for q_ref, c_ref in ((q_r, cr_ref), (q_i, ci_ref)):
                res = _dot(dyb_b, c_ref[blk])
                for kk in range(npb):
                    q_ref[pl.ds((blk * npb + kk) * BT, BT), :] = res[:, kk * LANES:(kk + 1) * LANES]

        _panel_scan(q_r, q_i, a_r, a_i, lr_ref, li_ref, car_r, car_i, NP, Bl, T, TP, adjoint=True)

        first_block = (i == nT - 1)
        for k in range(NP):
            rows = pl.ds(k * BT, BT)
            av_r, av_i = a_r[rows, :], a_i[rows, :]
            sp_r = pltpu.roll(sr_ref[rows, :], 1, 0)
            sp_i = pltpu.roll(si_ref[rows, :], 1, 0)
            acc = pl.ds(k * SUBLANES, SUBLANES)
            acc_r[acc, :] += jnp.sum((av_r * sp_r + av_i * sp_i).reshape(BT // SUBLANES, SUBLANES, LANES), axis=0)
            acc_i[acc, :] += jnp.sum((av_i * sp_r - av_r * sp_i).reshape(BT // SUBLANES, SUBLANES, LANES), axis=0)
            t0 = pl.ds(k * BT, Bl, stride=TP)
            a0_r, a0_i = a_r[t0, :], a_i[t0, :]
            brow = pl.ds(k * SUBLANES, Bl)
            sb_pr = jnp.where(first_block, 0.0, sbr_ref[brow, :])
            sb_pi = jnp.where(first_block, 0.0, sbi_ref[brow, :])
            acc_r[brow, :] += a0_r * sb_pr + a0_i * sb_pi
            acc_i[brow, :] += a0_i * sb_pr - a0_r * sb_pi

        ddsk_ref[...] += jnp.sum(dy * u, axis=0, keepdims=True)
        for blk in range(nb):
            cols = slice(blk * dsb, (blk + 1) * dsb)
            rows = [pl.ds((blk * npb + kk) * BT, BT) for kk in range(npb)]
            av_r = jnp.concatenate([a_r[r, :] for r in rows], axis=1).astype(BF16)
            av_i = jnp.concatenate([a_i[r, :] for r in rows], axis=1).astype(BF16)
            dupad[:, cols] = _dot_nt(av_r, br_ref[blk]) + _dot_nt(av_i, bi_ref[blk]) + dy[:, cols] * dsk_ref[:, cols]
            dbr_ref[blk] += _dot_tn(ub[:, cols], av_r)
            dbi_ref[blk] += _dot_tn(ub[:, cols], av_i)
            sv_r = jnp.concatenate([sr_ref[r, :] for r in rows], axis=1).astype(BF16)
            sv_i = jnp.concatenate([si_ref[r, :] for r in rows], axis=1).astype(BF16)
            dcr_ref[blk] += _dot_tn(dyb[:, cols], sv_r)
            dci_ref[blk] += _dot_tn(dyb[:, cols], sv_i)
        for b in range(Bl):
            du_ref[b] = dupad[pl.ds(b * TP, T), :].astype(BF16)

        @pl.when(i == nT - 1)
        def _():
            for k in range(NP):
                dlr_ref[pl.ds(k, 1), :] = jnp.sum(acc_r[pl.ds(k * SUBLANES, SUBLANES), :], axis=0, keepdims=True)
                dli_ref[pl.ds(k, 1), :] = jnp.sum(acc_i[pl.ds(k * SUBLANES, SUBLANES), :], axis=0, keepdims=True)

    S = jax.ShapeDtypeStruct
    rev = lambda i: nT - 1 - i
    sspec = pl.BlockSpec((None, NP * BT, LANES), lambda i: (rev(i), 0, 0))
    bspec = pl.BlockSpec((None, NP * SUBLANES, LANES), lambda i: (jnp.maximum(rev(i) - 1, 0), 0, 0))
    tspec = pl.BlockSpec((Bl, T, Ds), lambda i: (0, rev(i), 0))
    return _call_carrying(
        body, carried, (dy3, p3, dp3, Sr, Si, Sb_r, Sb_i, Bm_r, Bm_i, Cm_r, Cm_i, lam_r, lam_i, dsk),
        out_shape=[S(dp3.shape, dp3.dtype), S((nb, dsb, gpb), F32), S((nb, dsb, gpb), F32),
                   S((nb, dsb, gpb), F32), S((nb, dsb, gpb), F32), S((NP, LANES), F32), S((NP, LANES), F32), S((1, Ds), F32)],
        grid=(nT,),
        in_specs=[tspec, tspec, ANY, sspec, sspec, bspec, bspec,
                  _const_spec((nb, dsb, gpb)), _const_spec((nb, dsb, gpb)),
                  _const_spec((nb, dsb, gpb)), _const_spec((nb, dsb, gpb)),
                  _const_spec((NP, LANES)), _const_spec((NP, LANES)), _const_spec((1, Ds))],
        out_specs=[tspec, _const_spec((nb, dsb, gpb)), _const_spec((nb, dsb, gpb)),
                   _const_spec((nb, dsb, gpb)), _const_spec((nb, dsb, gpb)),
                   _const_spec((NP, LANES)), _const_spec((NP, LANES)), _const_spec((1, Ds))],
        aliases={2: 0},
        scratch_shapes=[pltpu.VMEM((NP * BT, LANES), F32)] * 2 + [pltpu.VMEM((NP * SUBLANES, LANES), F32)] * 4
        + [pltpu.VMEM((BT, Ds), F32)] * 3 + [pltpu.VMEM((NP * BT, LANES), F32)] * 2,
        compiler_params=_cparams(dimension_semantics=("arbitrary",)), name="s5_bwd")


def _mix_values(ylin, cb, cc, cx, gs, gc, halo_v, wglu, bglu, cw, wps, wpc, wout):
    yg, gelu_vjp = jax.vjp(jax.nn.gelu, ylin)
    sz = jax.nn.sigmoid(_dot(yg.astype(BF16), wglu) + bglu)
    ys = yg * sz
    v = cc * cx
    rows = lax.broadcasted_iota(jnp.int32, v.shape, 0)
    h6 = halo_v[HALO_ROWS - 2:HALO_ROWS - 1, :]
    h7 = halo_v[HALO_ROWS - 1:HALO_ROWS, :]
    v1 = jnp.where(rows == 0, h7, pltpu.roll(v, 1, 0))
    v2 = jnp.where(rows == 0, h6, jnp.where(rows == 1, h7, pltpu.roll(v, 2, 0)))
    cv = cw[0:1, :] * v2 + cw[1:2, :] * v1 + cw[2:3, :] * v
    yc = cb * cv
    ps = _dot(ys.astype(BF16), wps)
    pc = _dot(yc.astype(BF16), wpc)
    sgs = jax.nn.sigmoid(gs)
    sgc = jax.nn.sigmoid(gc)
    merged = sgs * ps + sgc * pc
    mo = _dot(merged.astype(BF16), wout)
    return dict(yg=yg, gelu_vjp=gelu_vjp, sz=sz, ys=ys, v=v, v1=v1, v2=v2, cv=cv, yc=yc, ps=ps, pc=pc,
                sgs=sgs, sgc=sgc, merged=merged, mo=mo)


def _mix_in_specs(TB, D, Ds, Dc, bps, blk):
    hb = TB // HALO_ROWS
    halo = lambda col: pl.BlockSpec((HALO_ROWS, Dc), lambda i: (jnp.maximum(blk(i) * hb - 1, 0), col))
    return [pl.BlockSpec((TB, Dc), lambda i: (blk(i), 1)), pl.BlockSpec((TB, Dc), lambda i: (blk(i), 2)),
            pl.BlockSpec((TB, Dc), lambda i: (blk(i), 3)), pl.BlockSpec((TB, D), lambda i: (blk(i), 2)),
            pl.BlockSpec((TB, D), lambda i: (blk(i), 3)), halo(2), halo(3),
            pl.BlockSpec((TB, Ds), lambda i: (blk(i), 0))]


def _mix_fwd(p2, ylin2, x2, g1, wglu, bglu, cw, wps, wpc, wout):
    N, D = x2.shape
    Ds = ylin2.shape[1]
    Dc = Ds
    Bl = g1.shape[0]
    TB = _row_block(N // Bl, 512)
    bps = (N // Bl) // TB

    def body(cb_ref, cc_ref, cx_ref, gs_ref, gc_ref, hcc_ref, hcx_ref, yl_ref, x_ref, g1_ref,
             wglu_ref, bglu_ref, cw_ref, wps_ref, wpc_ref, wout_ref, x1_ref):
        i = pl.program_id(0)
        f32 = lambda ref: ref[...].astype(F32)
        halo_v = jnp.where(i % bps == 0, 0.0, f32(hcc_ref) * f32(hcx_ref))
        f = _mix_values(yl_ref[...], f32(cb_ref), f32(cc_ref), f32(cx_ref), f32(gs_ref), f32(gc_ref), halo_v,
                        wglu_ref[...], bglu_ref[...], cw_ref[...], wps_ref[...], wpc_ref[...], wout_ref[...])
        x1_ref[...] = x_ref[...] + g1_ref[...] * f["mo"]

    return pl.pallas_call(
        body, out_shape=jax.ShapeDtypeStruct((N, D), F32), grid=(N // TB,),
        in_specs=_mix_in_specs(TB, D, Ds, Dc, bps, lambda i: i) + [
            pl.BlockSpec((TB, D), lambda i: (i, 0)), _seq_spec(D, bps),
            _const_spec((Ds, Ds)), _const_spec((1, Ds)), _const_spec((SUBLANES, Dc)),
            _const_spec((Ds, D)), _const_spec((Dc, D)), _const_spec((D, D))],
        out_specs=pl.BlockSpec((TB, D), lambda i: (i, 0)),
        compiler_params=_cparams(), name="mix_fwd",
    )(p2, p2, p2, p2, p2, p2, p2, ylin2, x2, g1, wglu, bglu, cw, wps, wpc, wout)


def _mix_bwd(p2, ylin2, dx1, g1, wglu, bglu, cw, wps, wpc, wout, carried):
    N, D = dx1.shape
    Ds = ylin2.shape[1]
    Dc = Ds
    IN = p2.shape[1]
    Bl = g1.shape[0]
    TB = _row_block(N // Bl)
    bps = (N // Bl) // TB
    nblk = N // TB
    rev = lambda i: nblk - 1 - i

    def body(cb_ref, cc_ref, cx_ref, gs_ref, gc_ref, hcc_ref, hcx_ref, yl_ref, dx1_ref, g1_ref,
             wglu_ref, bglu_ref, cw_ref, wps_ref, wpc_ref, wout_ref,
             dyl_ref, dp_ref, gwout_ref, gwps_ref, gwpc_ref, gwglu_ref, gbglu_ref, gcw_ref, dg1_ref, nxt):
        i = pl.program_id(0)
        blk = rev(i)

        @pl.when(i == 0)
        def _():
            for ref in (gwout_ref, gwps_ref, gwpc_ref, gwglu_ref, gbglu_ref, gcw_ref):
                ref[...] = jnp.zeros_like(ref)

        @pl.when(i % bps == 0)
        def _():
            nxt[...] = jnp.zeros_like(nxt)
            dg1_ref[...] = jnp.zeros_like(dg1_ref)

        f32 = lambda ref: ref[...].astype(F32)
        cb, cc, cx = f32(cb_ref), f32(cc_ref), f32(cx_ref)
        halo_v = jnp.where(blk % bps == 0, 0.0, f32(hcc_ref) * f32(hcx_ref))
        wglu, wps, wpc, wout, cw = wglu_ref[...], wps_ref[...], wpc_ref[...], wout_ref[...], cw_ref[...]
        f = _mix_values(yl_ref[...], cb, cc, cx, f32(gs_ref), f32(gc_ref), halo_v, wglu, bglu_ref[...], cw, wps, wpc, wout)

        dx1v = dx1_ref[...]
        dg1_ref[...] += jnp.sum(dx1v * f["mo"], axis=0, keepdims=True)
        dmo = (g1_ref[...] * dx1v).astype(BF16)
        gwout_ref[...] += _dot_tn(f["merged"].astype(BF16), dmo)
        dmerged = _dot_nt(dmo, wout)
        dps = dmerged * f["sgs"]
        dpc = dmerged * f["sgc"]
        dgs = dmerged * f["ps"] * f["sgs"] * (1.0 - f["sgs"])
        dgc = dmerged * f["pc"] * f["sgc"] * (1.0 - f["sgc"])
        dpsb, dpcb = dps.astype(BF16), dpc.astype(BF16)
        gwps_ref[...] += _dot_tn(f["ys"].astype(BF16), dpsb)
        gwpc_ref[...] += _dot_tn(f["yc"].astype(BF16), dpcb)
        dys = _dot_nt(dpsb, wps)
        dyc = _dot_nt(dpcb, wpc)

        dcb = dyc * f["cv"]
        dcv = dyc * cb
        rows = lax.broadcasted_iota(jnp.int32, dcv.shape, 0)
        n0, n1 = nxt[0:1, :], nxt[1:2, :]
        d1 = jnp.where(rows == TB - 1, n0, pltpu.roll(dcv, TB - 1, 0))
        d2 = jnp.where(rows == TB - 2, n0, jnp.where(rows == TB - 1, n1, pltpu.roll(dcv, TB - 2, 0)))
        dv = cw[2:3, :] * dcv + cw[1:2, :] * d1 + cw[0:1, :] * d2
        nxt[0:2, :] = dcv[0:2, :]
        gcw_ref[0:1, :] += jnp.sum(dcv * f["v2"], axis=0, keepdims=True)
        gcw_ref[1:2, :] += jnp.sum(dcv * f["v1"], axis=0, keepdims=True)
        gcw_ref[2:3, :] += jnp.sum(dcv * f["v"], axis=0, keepdims=True)

        dz = dys * f["yg"] * f["sz"] * (1.0 - f["sz"])
        dzb = dz.astype(BF16)
        gwglu_ref[...] += _dot_tn(f["yg"].astype(BF16), dzb)
        gbglu_ref[...] += jnp.sum(dz, axis=0, keepdims=True)
        dyg = dys * f["sz"] + _dot_nt(dzb, wglu)
        dyl_ref[...] = f["gelu_vjp"](dyg)[0]

        dp_ref[:, Ds:Ds + Dc] = dcb.astype(BF16)
        dp_ref[:, Ds + Dc:Ds + 2 * Dc] = (dv * cx).astype(BF16)
        dp_ref[:, Ds + 2 * Dc:Ds + 3 * Dc] = (dv * cc).astype(BF16)
        dp_ref[:, Ds + 3 * Dc:Ds + 3 * Dc + D] = dgs.astype(BF16)
        dp_ref[:, Ds + 3 * Dc + D:IN] = dgc.astype(BF16)

    S = jax.ShapeDtypeStruct
    return _call_carrying(
        body, carried, (p2, p2, p2, p2, p2, p2, p2, ylin2, dx1, g1, wglu, bglu, cw, wps, wpc, wout),
        out_shape=[S((N, Ds), F32), S((N, IN), BF16), S((D, D), F32), S((Ds, D), F32), S((Dc, D), F32),
                   S((Ds, Ds), F32), S((1, Ds), F32), S((SUBLANES, Dc), F32), S((Bl, 1, D), F32)],
        grid=(nblk,),
        in_specs=_mix_in_specs(TB, D, Ds, Dc, bps, rev) + [
            pl.BlockSpec((TB, D), lambda i: (rev(i), 0)), _seq_spec(D, bps, nblk),
            _const_spec((Ds, Ds)), _const_spec((1, Ds)), _const_spec((SUBLANES, Dc)),
            _const_spec((Ds, D)), _const_spec((Dc, D)), _const_spec((D, D))],
        out_specs=[pl.BlockSpec((TB, Ds), lambda i: (rev(i), 0)), pl.BlockSpec((TB, IN), lambda i: (rev(i), 0)),
                   _const_spec((D, D)), _const_spec((Ds, D)), _const_spec((Dc, D)), _const_spec((Ds, Ds)),
                   _const_spec((1, Ds)), _const_spec((SUBLANES, Dc)), _seq_spec(D, bps, nblk)],
        scratch_shapes=[pltpu.VMEM((SUBLANES, Dc), F32)],
        compiler_params=_cparams(dimension_semantics=("arbitrary",)), name="mix_bwd")


def _mlp_fwd_bwd(x1, tgt, sh2, sc2, g2, n2g, fg, w1, w2):
    N, D = x1.shape
    Dff = w1.shape[1]
    Bl = sh2.shape[0]
    TB = _row_block(N // Bl)
    bps = (N // Bl) // TB

    def body(x1_ref, t_ref, sh_ref, sc_ref, g2_ref, n2_ref, fg_ref, w1_ref, w2_ref,
             dx1_ref, h2_ref, da_ref, sq_ref, df_ref, loss_ref, gfg_ref, gn2_ref, dsh_ref, dsc_ref, dg2_ref):
        i = pl.program_id(0)

        @pl.when(i == 0)
        def _():
            for ref in (loss_ref, gfg_ref, gn2_ref):
                ref[...] = jnp.zeros_like(ref)

        @pl.when(i % bps == 0)
        def _():
            for ref in (dsh_ref, dsc_ref, dg2_ref):
                ref[...] = jnp.zeros_like(ref)

        x1v = x1_ref[...]
        sc, g2v, n2 = sc_ref[...], g2_ref[...], n2_ref[...]
        xh2, r2 = _rms_fwd(x1v)
        xn2 = xh2 * n2
        h2 = (xn2 * (1.0 + sc) + sh_ref[...]).astype(BF16)
        a = _dot(h2, w1_ref[...])
        ra = jnp.maximum(a, 0.0)
        sq = (ra * ra).astype(BF16)
        fv = _dot(sq, w2_ref[...])
        x2 = x1v + g2v * fv
        xh3, r3 = _rms_fwd(x2)
        err = xh3 * fg_ref[...] - t_ref[...]
        loss_ref[...] += 0.5 * jnp.sum(jnp.mean(err * err, axis=-1, keepdims=True), axis=0, keepdims=True)
        dy = err * (1.0 / D)
        gfg_ref[...] += jnp.sum(dy * xh3, axis=0, keepdims=True)
        dx2 = _rms_bwd(dy * fg_ref[...], xh3, r3)
        dg2_ref[...] += jnp.sum(dx2 * fv, axis=0, keepdims=True)
        df = (g2v * dx2).astype(BF16)
        dsq = _dot_nt(df, w2_ref[...])
        da = (2.0 * ra * dsq).astype(BF16)
        dh2 = _dot_nt(da, w1_ref[...])
        dsh_ref[...] += jnp.sum(dh2, axis=0, keepdims=True)
        dsc_ref[...] += jnp.sum(dh2 * xn2, axis=0, keepdims=True)
        dxn2 = dh2 * (1.0 + sc)
        gn2_ref[...] += jnp.sum(dxn2 * xh2, axis=0, keepdims=True)
        dx1_ref[...] = dx2 + _rms_bwd(dxn2 * n2, xh2, r2)
        h2_ref[...] = h2
        da_ref[...] = da
        sq_ref[...] = sq
        df_ref[...] = df

    S = jax.ShapeDtypeStruct
    row = lambda w: pl.BlockSpec((TB, w), lambda i: (i, 0))
    return pl.pallas_call(
        body,
        out_shape=[S((N, D), F32), S((N, D), BF16), S((N, Dff), BF16), S((N, Dff), BF16), S((N, D), BF16),
                   S((1, 1), F32), S((1, D), F32), S((1, D), F32), S((Bl, 1, D), F32), S((Bl, 1, D), F32), S((Bl, 1, D), F32)],
        grid=(N // TB,),
        in_specs=[row(D), row(D), _seq_spec(D, bps), _seq_spec(D, bps), _seq_spec(D, bps),
                  _const_spec((1, D)), _const_spec((1, D)), _const_spec((D, Dff)), _const_spec((Dff, D))],
        out_specs=[row(D), row(D), row(Dff), row(Dff), row(D), _const_spec((1, 1)), _const_spec((1, D)), _const_spec((1, D)),
                   _seq_spec(D, bps), _seq_spec(D, bps), _seq_spec(D, bps)],
        compiler_params=_cparams(dimension_semantics=("arbitrary",)), name="mlp_fwd_bwd",
    )(x1, tgt, sh2, sc2, g2, n2g, fg, w1, w2)


_NO_EXCHANGE = _Carried((), (), {}, (), lambda ins, outs, sems: ((lambda: None), (lambda: None)))


def _grad_w(a, b, name, carried=_NO_EXCHANGE):
    N, K1 = a.shape
    K2 = b.shape[1]
    t1 = 1024 if K1 % 1024 == 0 else K1
    t2 = 1024 if K2 % 1024 == 0 else K2
    tn = 2048 if N % 2048 == 0 else N

    def body(a_ref, b_ref, o_ref):
        @pl.when(pl.program_id(2) == 0)
        def _():
            o_ref[...] = jnp.zeros_like(o_ref)

        o_ref[...] += _dot_tn(a_ref[...], b_ref[...])

    (g,), extra = _call_carrying(
        body, carried, (a, b), out_shape=[jax.ShapeDtypeStruct((K1, K2), F32)], grid=(K1 // t1, K2 // t2, N // tn),
        in_specs=[pl.BlockSpec((tn, t1), lambda i, j, k: (k, i)), pl.BlockSpec((tn, t2), lambda i, j, k: (k, j))],
        out_specs=[pl.BlockSpec((t1, t2), lambda i, j, k: (i, j))],
        compiler_params=_cparams(dimension_semantics=("arbitrary", "arbitrary", "arbitrary")), name=name)
    return g, extra


def _inproj_bwd(x2, dx1, dp, sh1, sc1, n1g, w_halves, carried):
    N, D = x2.shape
    Dh, IN = w_halves[0].shape
    Bl = sh1.shape[0]
    TB = _row_block(N // Bl, 512)
    bps = (N // Bl) // TB

    def body(x_ref, dx1_ref, dp_ref, sh_ref, sc_ref, g_ref, wt_ref, wb_ref, gx_ref, h_ref, gn1_ref, dsh_ref, dsc_ref):
        i = pl.program_id(0)

        @pl.when(i == 0)
        def _():
            gn1_ref[...] = jnp.zeros_like(gn1_ref)

        @pl.when(i % bps == 0)
        def _():
            dsh_ref[...] = jnp.zeros_like(dsh_ref)
            dsc_ref[...] = jnp.zeros_like(dsc_ref)

        sc, n1 = sc_ref[...], g_ref[...]
        xh, r = _rms_fwd(x_ref[...])
        xn = xh * n1
        h_ref[...] = (xn * (1.0 + sc) + sh_ref[...]).astype(BF16)
        dpv = dp_ref[...]
        dh = jnp.concatenate([_dot_nt(dpv, wt_ref[...]), _dot_nt(dpv, wb_ref[...])], axis=1)
        dsh_ref[...] += jnp.sum(dh, axis=0, keepdims=True)
        dsc_ref[...] += jnp.sum(dh * xn, axis=0, keepdims=True)
        dxn = dh * (1.0 + sc)
        gn1_ref[...] += jnp.sum(dxn * xh, axis=0, keepdims=True)
        gx_ref[...] = dx1_ref[...] + _rms_bwd(dxn * n1, xh, r)

    S = jax.ShapeDtypeStruct
    row = lambda w: pl.BlockSpec((TB, w), lambda i: (i, 0))
    return _call_carrying(
        body, carried, (x2, dx1, dp, sh1, sc1, n1g, *w_halves),
        out_shape=[S((N, D), F32), S((N, D), BF16), S((1, D), F32), S((Bl, 1, D), F32), S((Bl, 1, D), F32)],
        grid=(N // TB,),
        in_specs=[row(D), row(D), row(IN), _seq_spec(D, bps), _seq_spec(D, bps), _const_spec((1, D)),
                  _const_spec((Dh, IN)), _const_spec((Dh, IN))],
        out_specs=[row(D), row(D), _const_spec((1, D)), _seq_spec(D, bps), _seq_spec(D, bps)],
        compiler_params=_cparams(dimension_semantics=("arbitrary",)), name="inproj_bwd")


def _diag_mask(gb, a, b):
    rows = lax.broadcasted_iota(jnp.int32, (gb * a, gb * b), 0) // a
    cols = lax.broadcasted_iota(jnp.int32, (gb * a, gb * b), 1) // b
    return (rows == cols).astype(F32)


def _diag_blocks_from_groups(m, nb):
    G, a, b = m.shape
    gb = G // nb
    return jnp.tile(m.reshape(nb, gb * a, b), (1, 1, gb)) * _diag_mask(gb, a, b)[None]


def _groups_from_diag_blocks(d, G, a, b):
    nb = d.shape[0]
    gb = G // nb
    picked = (d * _diag_mask(gb, a, b)[None]).reshape(nb, gb * a, gb, b)
    return jnp.sum(picked, axis=2).reshape(G, a, b)


def _pad_rows(v, rows):
    return jnp.concatenate([v, jnp.zeros((rows - v.shape[0],) + v.shape[1:], v.dtype)], axis=0)


def _pack(vs):
    flat = jnp.concatenate([v.reshape(-1) for v in vs])
    n = flat.shape[0]
    tile = SUBLANES * LANES
    npad = -(-n // tile) * tile
    flat = jnp.concatenate([flat, jnp.zeros((npad - n,), flat.dtype)])
    return flat.reshape(npad // LANES, LANES)


def _unpack(packed, shapes):
    flat = packed.reshape(-1)
    out, off = [], 0
    for s in shapes:
        n = 1
        for d in s:
            n *= d
        out.append(flat[off:off + n].reshape(s))
        off += n
    return out


def kernel(x, c, norm1_g, norm2_g, w_ada, b_ada, w_in, lam_re, lam_im, log_dt, b_re, b_im, c_re, c_im, d_skip, w_glu, b_glu, conv_w, w_proj_ssm, w_proj_conv, w_out, w_ff1, w_ff2, final_g, loss_target, m_norm1_g, m_norm2_g, m_w_ada, m_b_ada, m_w_in, m_lam_re, m_lam_im, m_log_dt, m_b_re, m_b_im, m_c_re, m_c_im, m_d_skip, m_w_glu, m_b_glu, m_conv_w, m_w_proj_ssm, m_w_proj_conv, m_w_out, m_w_ff1, m_w_ff2, m_final_g, v_norm1_g, v_norm2_g, v_w_ada, v_b_ada, v_w_in, v_lam_re, v_lam_im, v_log_dt, v_b_re, v_b_im, v_c_re, v_c_im, v_d_skip, v_w_glu, v_b_glu, v_conv_w, v_w_proj_ssm, v_w_proj_conv, v_w_out, v_w_ff1, v_w_ff2, v_final_g):
    Bl, L, D = x.shape
    N = Bl * L
    Ds = Dc = D // 2
    G, H, P = Ds // SSM_GROUP, SSM_GROUP, SSM_STATE
    GP = G * P
    NP = GP // LANES
    nb = _s5_dims(Bl, L, Ds)[4]
    IN = Ds + 3 * Dc + 2 * D
    ax, ay, ac = _mesh_pos()
    q = 2 * ax + ay
    dev = 2 * q + ac

    big_names = ["w_in", "w_ff1", "w_ff2", "w_out", "w_proj_ssm", "w_proj_conv", "w_glu"]
    big_w = dict(w_in=w_in[0], w_ff1=w_ff1[0], w_ff2=w_ff2[0], w_out=w_out[0],
                 w_proj_ssm=w_proj_ssm[0], w_proj_conv=w_proj_conv[0], w_glu=w_glu[0])
    big_axis = dict(w_in=1, w_ff1=1, w_ff2=0, w_out=0, w_proj_ssm=1, w_proj_conv=1, w_glu=0)
    axes = [big_axis[k] for k in big_names]
    shard_shapes = [big_w[k].shape for k in big_names]
    pos = jnp.stack([q, ac]).astype(jnp.int32)
    Dh = D // 2
    halves = [big_w["w_in"][:Dh], big_w["w_in"][Dh:]]
    casted = _cast_into_full(halves + [big_w[k] for k in big_names[1:]], [1, 1] + axes[1:], pos, "cast_weights")
    own_only = dict(zip(big_names[1:], casted[2:]))
    gather_half = lambda w: _carry_allgather([w], [1], [halves[0].shape])
    Dcs = conv_w.shape[2]
    first, (w_in_top,) = _allgather8(_pack([c, conv_w[0]]), "allgather_c_w_in_top", gather_half(casted[0]))
    full = {}
    first = first.reshape(N_DEV, -1)
    c_all = first[:, :Bl * D].reshape(N_DEV * Bl, D)
    cw = first[0::2, Bl * D:Bl * D + 3 * Dcs].reshape(N_CHIPS, 3, Dcs).transpose(1, 0, 2).reshape(3, Dc)
    cw8 = _pad_rows(cw, SUBLANES)
    Ca = w_ada.shape[2]
    b_ada_sh = lax.dynamic_slice_in_dim(b_ada, q * Ca, Ca, axis=1)
    mod_part = _ada_fwd(c_all, w_ada[0], b_ada_sh)
    mod_g, (w_in_bot,) = _allgather8(mod_part, "allgather_mod_w_in_bot", gather_half(casted[1]))
    w_in_halves = (w_in_top, w_in_bot)
    mod_all = mod_g[0::2].transpose(1, 0, 2).reshape(N_DEV * Bl, N_CHIPS * Ca)
    mod = lax.dynamic_slice_in_dim(mod_all, dev * Bl, Bl, axis=0)
    sh1, sc1, g1, sh2, sc2, g2 = [mod[:, k * D:(k + 1) * D].reshape(Bl, 1, D) for k in range(6)]

    ldt_c = log_dt[0].reshape(G, 1)
    bt_r = b_re[0].transpose(2, 0, 1)
    bt_i = b_im[0].transpose(2, 0, 1)
    lbr, lbi, bbt_r, bbt_i = _s5_disc(lam_re[0], lam_im[0], ldt_c, bt_r, bt_i)
    lam_r_p = lbr.reshape(NP, LANES)
    lam_i_p = lbi.reshape(NP, LANES)
    Bm_r = _diag_blocks_from_groups(bbt_r.transpose(1, 0, 2), nb).astype(BF16)
    Bm_i = _diag_blocks_from_groups(bbt_i.transpose(1, 0, 2), nb).astype(BF16)
    Cm_r = _diag_blocks_from_groups(c_re[0], nb).astype(BF16)
    Cm_i = _diag_blocks_from_groups(-c_im[0], nb).astype(BF16)

    x2 = x.reshape(N, D)
    mixer_w = ["w_out", "w_proj_ssm", "w_proj_conv", "w_glu"]
    mlp_w = ["w_ff1", "w_ff2"]
    layout = lambda ks: ([big_axis[k] for k in ks], [big_w[k].shape for k in ks])
    gather = lambda ks: _carry_allgather([own_only[k] for k in ks], *layout(ks))
    p2, gathered = _inproj_fwd(x2, sh1, sc1, norm1_g, w_in_halves, gather(mixer_w))
    full.update(zip(mixer_w, gathered))
    p3 = p2.reshape(Bl, L, IN)
    (Sr, Si, Sb_r, Sb_i, ylin3), gathered = _s5_fwd(p3, Bm_r, Bm_i, Cm_r, Cm_i, lam_r_p, lam_i_p, d_skip, gather(mlp_w))
    full.update(zip(mlp_w, gathered))
    ylin2 = ylin3.reshape(N, Ds)
    mix_w = (full["w_glu"], b_glu, cw8, full["w_proj_ssm"], full["w_proj_conv"], full["w_out"])
    x1 = _mix_fwd(p2, ylin2, x2, g1, *mix_w)

    (dx1, h2b, dab, sqb, dfb, loss_p, g_fg, g_n2, dsh2, dsc2, dg2) = _mlp_fwd_bwd(
        x1, loss_target.reshape(N, D), sh2, sc2, g2, norm2_g, final_g.reshape(1, D), full["w_ff1"], full["w_ff2"])
    g_full = {"w_ff1": _grad_w(h2b, dab, "grad_w_ff1")[0], "w_ff2": _grad_w(sqb, dfb, "grad_w_ff2")[0]}

    exchange = lambda ks: _carry_sibling_exchange([g_full[k] for k in ks], *layout(ks))
    presum = lambda ks, theirs: _presum([g_full[k] for k in ks], list(theirs), *layout(ks), pos, "presum_" + ks[0])
    chip_sum = lambda ks, parts, recv: _sum_chips(list(parts), list(recv), pos, "sum_" + ks[0])

    (dyl2, dp2, gw_out, gw_ps, gw_pc, gw_glu, gb_glu, gcw8, dg1), theirs_mlp = _mix_bwd(
        p2, ylin2, dx1, g1, *mix_w, exchange(mlp_w))
    g_full.update(w_out=gw_out, w_proj_ssm=gw_ps, w_proj_conv=gw_pc, w_glu=gw_glu)
    parts_mlp = presum(mlp_w, theirs_mlp)
    (dp3, dBm_r, dBm_i, dCm_r, dCm_i, dlam_r_p, dlam_i_p, g_dsk), extra = _s5_bwd(
        dyl2.reshape(Bl, L, Ds), p3, dp2.reshape(Bl, L, IN), Sr, Si, Sb_r, Sb_i, Bm_r, Bm_i, Cm_r, Cm_i, lam_r_p, lam_i_p,
        d_skip, _carry_join(_carry_chip_scatter(parts_mlp), exchange(mixer_w)))
    recv_mlp, theirs_mix = extra[:len(mlp_w)], extra[len(mlp_w):]
    halves_mlp = chip_sum(mlp_w, parts_mlp, recv_mlp)
    parts_mix = presum(mixer_w, theirs_mix)
    dp_all = dp3.reshape(N, IN)
    (grad_x2, hb, g_n1, dsh1, dsc1), _ = _inproj_bwd(x2, dx1, dp_all, sh1, sc1, norm1_g, w_in_halves, _NO_EXCHANGE)

    dbbt_r = _groups_from_diag_blocks(dBm_r, G, H, P).transpose(1, 0, 2)
    dbbt_i = _groups_from_diag_blocks(dBm_i, G, H, P).transpose(1, 0, 2)
    dc_re = _groups_from_diag_blocks(dCm_r, G, H, P)
    dc_im = -_groups_from_diag_blocks(dCm_i, G, H, P)
    dmod = jnp.concatenate([dsh1, dsc1, dg1, dsh2, dsc2, dg2], axis=-1).reshape(Bl, 6 * D)
    small = [g_n1, g_n2, g_fg, g_dsk, gb_glu, gcw8[:3], dlam_r_p, dlam_i_p, dbbt_r, dbbt_i, dc_re, dc_im]
    small_shapes = [v.shape for v in small]
    n_small = sum(int(v.size) for v in small)
    small_slots = _place_in_slot(_pack(small + [dmod]), jnp.reshape(dev, (1,)).astype(jnp.int32), "place_small")

    g_full["w_in"], extra = _grad_w(
        hb, dp_all, "grad_w_in",
        _carry_join(_carry_join(_carry_sibling_share(halves_mlp), _carry_chip_scatter(parts_mix)), _carry_allgather8(small_slots)))
    reduced = dict(zip(mlp_w, extra[:len(mlp_w)]))
    halves_mix = chip_sum(mixer_w, parts_mix, extra[len(mlp_w):len(mlp_w) + len(mixer_w)])
    gathered = extra[-1]
    theirs_in = _run_carried(exchange(["w_in"]), "rs_exchange_w_in")
    parts_in = presum(["w_in"], theirs_in)
    recv_in = _run_carried(_carry_chip_scatter(parts_in), "rs_scatter_w_in")
    halves_in = chip_sum(["w_in"], parts_in, recv_in)
    reduced.update(zip(mixer_w + ["w_in"], _run_carried(_carry_sibling_share(halves_mix + halves_in), "rs_share_rest")))

    red = _unpack(_sum_devices(gathered, "sum_small"), small_shapes)
    (r_n1, r_n2, r_fg, r_dsk, r_bglu, r_cw, r_dlr, r_dli, r_dbr, r_dbi, r_cre, r_cim) = red
    dmod_all = gathered.reshape(N_DEV, -1)[:, n_small:n_small + Bl * 6 * D].reshape(N_DEV * Bl, 6 * D)
    gw_ada, gb_ada = _ada_bwd(c_all, lax.dynamic_slice_in_dim(dmod_all, q * Ca, Ca, axis=1), dmod_all)
    g_lr, g_li, g_ldt, g_bt_r, g_bt_i = _s5_disc_bwd(lam_re[0], lam_im[0], ldt_c, bt_r, bt_i,
                                                   r_dlr.reshape(G, P), r_dli.reshape(G, P), r_dbr, r_dbi)

    grads = dict(
        norm1_g=r_n1, norm2_g=r_n2, w_ada=gw_ada, b_ada=gb_ada, lam_re=g_lr, lam_im=g_li, log_dt=g_ldt.reshape(1, G),
        b_re=g_bt_r.transpose(1, 2, 0), b_im=g_bt_i.transpose(1, 2, 0), c_re=r_cre, c_im=r_cim, d_skip=r_dsk,
        b_glu=r_bglu, conv_w=lax.dynamic_slice_in_dim(r_cw, q * Dcs, Dcs, axis=1), final_g=r_fg, **reduced)
    weights = dict(norm1_g=norm1_g, norm2_g=norm2_g, w_ada=w_ada, b_ada=b_ada, w_in=w_in, lam_re=lam_re, lam_im=lam_im,
                   log_dt=log_dt, b_re=b_re, b_im=b_im, c_re=c_re, c_im=c_im, d_skip=d_skip, w_glu=w_glu, b_glu=b_glu,
                   conv_w=conv_w, w_proj_ssm=w_proj_ssm, w_proj_conv=w_proj_conv, w_out=w_out, w_ff1=w_ff1, w_ff2=w_ff2,
                   final_g=final_g)
    m_in = dict(norm1_g=m_norm1_g, norm2_g=m_norm2_g, w_ada=m_w_ada, b_ada=m_b_ada, w_in=m_w_in, lam_re=m_lam_re,
                lam_im=m_lam_im, log_dt=m_log_dt, b_re=m_b_re, b_im=m_b_im, c_re=m_c_re, c_im=m_c_im, d_skip=m_d_skip,
                w_glu=m_w_glu, b_glu=m_b_glu, conv_w=m_conv_w, w_proj_ssm=m_w_proj_ssm, w_proj_conv=m_w_proj_conv,
                w_out=m_w_out, w_ff1=m_w_ff1, w_ff2=m_w_ff2, final_g=m_final_g)
    v_in = dict(norm1_g=v_norm1_g, norm2_g=v_norm2_g, w_ada=v_w_ada, b_ada=v_b_ada, w_in=v_w_in, lam_re=v_lam_re,
                lam_im=v_lam_im, log_dt=v_log_dt, b_re=v_b_re, b_im=v_b_im, c_re=v_c_re, c_im=v_c_im, d_skip=v_d_skip,
                w_glu=v_w_glu, b_glu=v_b_glu, conv_w=v_conv_w, w_proj_ssm=v_w_proj_ssm, w_proj_conv=v_w_proj_conv,
                w_out=v_w_out, w_ff1=v_w_ff1, w_ff2=v_w_ff2, final_g=v_final_g)
    names = list(weights)
    grads = {k: grads[k].reshape(weights[k].shape) for k in names}

    big_upd = big_names + ["w_ada"]
    delta, new_m, new_v = {}, {}, {}
    flat2 = lambda a: a.reshape(-1, a.shape[-1])
    d_, m_, v_ = _adamw([flat2(weights[k]) for k in big_upd], [flat2(grads[k]) for k in big_upd],
                        [flat2(m_in[k]) for k in big_upd], [flat2(v_in[k]) for k in big_upd], "adamw_big")
    for k, dd, mm, vv in zip(big_upd, d_, m_, v_):
        shp = weights[k].shape
        delta[k], new_m[k], new_v[k] = dd.reshape(shp), mm.reshape(shp), vv.reshape(shp)
    small_upd = [k for k in names if k not in big_upd]
    d_, m_, v_ = _adamw_many([flat2(weights[k]) for k in small_upd], [flat2(grads[k]) for k in small_upd],
                             [flat2(m_in[k]) for k in small_upd], [flat2(v_in[k]) for k in small_upd])
    for k, dd, mm, vv in zip(small_upd, d_, m_, v_):
        shp = weights[k].shape
        delta[k], new_m[k], new_v[k] = dd.reshape(shp), mm.reshape(shp), vv.reshape(shp)

    loss = lax.psum(loss_p[0, 0], ("x", "y", "c"))
    grad_x = grad_x2.reshape(Bl, L, D)
    return (loss, grad_x, *[grads[k] for k in names], *[delta[k] for k in names],
            *[new_m[k] for k in names], *[new_v[k] for k in names])
```

```python
import functools
from typing import Callable, NamedTuple

import jax
import jax.numpy as jnp
from jax import lax
from jax.experimental import pallas as pl
from jax.experimental.pallas import tpu as pltpu

F32 = jnp.float32
BF16 = jnp.bfloat16
MESH = pl.DeviceIdType.MESH
N_CHIPS = 4
N_DEV = 8
LANES = 128
SUBLANES = 8
V7X_VMEM_BYTES = 64 * 1024 * 1024
VMEM_LIMIT = V7X_VMEM_BYTES - 6 * 1024 * 1024
SSM_GROUP = 16
SSM_STATE = 64
S5_ROW_PAD = 4
HALO_ROWS = 16
RMS_EPS = 1e-6
ADAM_LR, ADAM_B1, ADAM_B2, ADAM_EPS, ADAM_WD, ADAM_STEP = 0.001, 0.9, 0.999, 1e-08, 0.01, 10

ANY = pl.BlockSpec(memory_space=pl.ANY)
VMEM_SPEC = pl.BlockSpec(memory_space=pltpu.VMEM)


def _cparams(**kw):
    return pltpu.CompilerParams(vmem_limit_bytes=VMEM_LIMIT, **kw)


def _dot(a, b):
    return jnp.dot(a, b, preferred_element_type=F32)


def _dot_nt(a, b):
    return lax.dot_general(a, b, (((1,), (1,)), ((), ())), preferred_element_type=F32)


def _dot_tn(a, b):
    return lax.dot_general(a, b, (((0,), (0,)), ((), ())), preferred_element_type=F32)


def _mesh_pos():
    return lax.axis_index("x"), lax.axis_index("y"), lax.axis_index("c")


def _allgather8(v, name, carried=None):
    r, c = v.shape

    def body(x_ref, out_ref, send_sems, recv_sems, local_sem):
        _gather8(x_ref, out_ref, send_sems, recv_sems, local_sem)

    out_shape = jax.ShapeDtypeStruct((N_DEV, r, c), v.dtype)
    if carried is None:
        return pl.pallas_call(body, out_shape=out_shape, in_specs=[VMEM_SPEC], out_specs=VMEM_SPEC,
                              scratch_shapes=_gather8_sems(), name=name)(v)
    (out,), extra = _call_carrying(body, carried, (v,), out_shape=[out_shape], in_specs=[VMEM_SPEC],
                                   out_specs=[VMEM_SPEC], scratch_shapes=_gather8_sems(), name=name)
    return out, extra


def _gather8_sems():
    return [pltpu.SemaphoreType.DMA((7,)), pltpu.SemaphoreType.DMA((7,)), pltpu.SemaphoreType.DMA]


def _gather8(x_ref, out_ref, send_sems, recv_sems, local_sem):
    x, y, cc = _mesh_pos()
    me, sibling = (x, y, cc), (x, y, 1 - cc)
    chips = [(1 - x, y), (x, 1 - y), (1 - x, 1 - y)]

    def slot(px, py, pc):
        return out_ref.at[4 * px + 2 * py + pc]

    def copy(k, block, to, src=None):
        return pltpu.make_async_remote_copy(
            src_ref=slot(*block) if src is None else src, dst_ref=slot(*block),
            send_sem=send_sems.at[k], recv_sem=recv_sems.at[k], device_id=to, device_id_type=MESH)

    mine = pltpu.make_async_copy(x_ref, slot(*me), local_sem)
    mine.start()
    first = [copy(0, me, sibling, src=x_ref)]
    first += [copy(1 + j, me, (*chip, cc), src=x_ref) for j, chip in enumerate(chips)]
    for cp in first:
        cp.start()
    passed = [copy(4 + j, (*chip, cc), sibling) for j, chip in enumerate(chips)]
    for j, chip in enumerate(chips):
        copy(1 + j, (*chip, cc), me).wait_recv()
        passed[j].start()
    copy(0, sibling, me).wait_recv()
    for j, chip in enumerate(chips):
        copy(4 + j, (*chip, 1 - cc), me).wait_recv()
    for cp in first + passed:
        cp.wait_send()
    mine.wait()


def _adaln_prologue(c8, w_sh, b_sh, carried):
    R, D = c8.shape
    Ca = w_sh.shape[1]
    cb = 512 if Ca % 512 == 0 else Ca

    def body(c_ref, w_ref, b_ref, cg_ref, mg_ref, mod_ref, *sems):
        _gather8(c_ref, cg_ref, *sems[:3])
        act = _silu(cg_ref[...].reshape(N_DEV * R, D)).astype(BF16)
        for j in range(Ca // cb):
            cols = slice(j * cb, (j + 1) * cb)
            mod_ref[:, cols] = _dot(act, w_ref[:, cols].astype(BF16)) + b_ref[:, cols]
        _gather8(mod_ref, mg_ref, *sems[3:])

    S = jax.ShapeDtypeStruct
    return _call_carrying(
        body, carried, (c8, w_sh, b_sh), out_shape=[S((N_DEV, R, D), F32), S((N_DEV, N_DEV * R, Ca), F32)],
        in_specs=[VMEM_SPEC] * 3, out_specs=[VMEM_SPEC] * 2,
        scratch_shapes=[pltpu.VMEM((N_DEV * R, Ca), F32)] + _gather8_sems() + _gather8_sems(),
        compiler_params=_cparams(), name="adaln_prologue")


def _shard_region(ref, axis, shard_shape, q, half):
    R, C = shard_shape
    r0, nr = (0, R) if half is None else (half * (R // 2), R // 2)
    if axis == 1:
        return ref.at[pl.ds(r0, nr), pl.ds(q * C, C)]
    return ref.at[pl.ds(q * R + r0, nr), :]


class _Carried(NamedTuple):
    inputs: tuple
    out_shapes: tuple
    aliases: dict
    sems: tuple
    steps: Callable


def _carry_join(a, b):
    na_i, na_o, na_s = len(a.inputs), len(a.out_shapes), len(a.sems)

    def steps(ins, outs, sems):
        sa, fa = a.steps(ins[:na_i], outs[:na_o], sems[:na_s])
        sb, fb = b.steps(ins[na_i:], outs[na_o:], sems[na_s:])


        def start():
            sa()
            sb()

        def finish():
            fa()
            fb()

        return start, finish

    aliases = dict(a.aliases)
    aliases.update({na_i + i: na_o + o for i, o in b.aliases.items()})
    return _Carried(a.inputs + b.inputs, a.out_shapes + b.out_shapes, aliases, a.sems + b.sems, steps)


def _call_carrying(body, carried, args, *, out_shape, in_specs, out_specs, scratch_shapes=(), grid=None, aliases=None,
                   name, **kw):
    n_in, n_out, n_sc = len(in_specs), len(out_specs), len(scratch_shapes)
    n_ci, n_co = len(carried.inputs), len(carried.out_shapes)

    def wrapped(*refs):
        ins, refs = refs[:n_in], refs[n_in:]
        c_ins, refs = refs[:n_ci], refs[n_ci:]
        outs, refs = refs[:n_out], refs[n_out:]
        c_outs, refs = refs[:n_co], refs[n_co:]
        scratch, c_sems = refs[:n_sc], refs[n_sc:]
        start, finish = carried.steps(c_ins, c_outs, c_sems)
        if grid is None:
            start()
            body(*ins, *outs, *scratch)
            finish()
        else:
            ids = [pl.program_id(d) for d in range(len(grid))]
            first = functools.reduce(jnp.logical_and, [i == 0 for i in ids])
            last = functools.reduce(jnp.logical_and, [i == g - 1 for i, g in zip(ids, grid)])
            pl.when(first)(start)
            body(*ins, *outs, *scratch)
            pl.when(last)(finish)

    if grid is not None:
        kw["grid"] = grid
    io_aliases = dict(aliases or {})
    io_aliases.update({n_in + i: n_out + o for i, o in carried.aliases.items()})
    res = pl.pallas_call(
        wrapped, out_shape=list(out_shape) + list(carried.out_shapes),
        in_specs=list(in_specs) + [ANY] * n_ci, out_specs=list(out_specs) + [ANY] * n_co,
        scratch_shapes=list(scratch_shapes) + list(carried.sems),
        input_output_aliases=io_aliases, name=name, **kw,
    )(*args, *carried.inputs)
    return res[:n_out], res[n_out:]


def _run_carried(carried, name):
    return _call_carrying(lambda: None, carried, (), out_shape=(), in_specs=(), out_specs=(), name=name)[1]


def _place_in_slot(v, dev_arr, name):
    r, c = v.shape

    def body(d_ref, v_ref, o_ref):
        o_ref[...] = v_ref[...]

    return pl.pallas_call(
        body, out_shape=jax.ShapeDtypeStruct((N_DEV, r, c), v.dtype),
        grid_spec=pltpu.PrefetchScalarGridSpec(
            num_scalar_prefetch=1, grid=(1,), in_specs=[pl.BlockSpec((r, c), lambda i, d: (0, 0))],
            out_specs=pl.BlockSpec((None, r, c), lambda i, d: (d[0], 0, 0))),
        name=name)(dev_arr, v)


def _carry_allgather8(buf):
    def steps(ins, outs, sems):
        send_s, recv_s = sems
        out = outs[0]
        x, y, cc = _mesh_pos()
        me, sibling = (x, y, cc), (x, y, 1 - cc)
        chips = [(1 - x, y), (x, 1 - y), (1 - x, 1 - y)]

        def copy(k, block, to):
            px, py, pc = block
            slot = out.at[4 * px + 2 * py + pc]
            return pltpu.make_async_remote_copy(src_ref=slot, dst_ref=slot, send_sem=send_s.at[k], recv_sem=recv_s.at[k],
                                                device_id=to, device_id_type=MESH)

        first = [copy(0, me, sibling)] + [copy(1 + j, me, (*chip, cc)) for j, chip in enumerate(chips)]
        passed = [copy(4 + j, (*chip, cc), sibling) for j, chip in enumerate(chips)]

        def start():
            for cp in first:
                cp.start()

        def finish():
            for j, chip in enumerate(chips):
                copy(1 + j, (*chip, cc), me).wait_recv()
                passed[j].start()
            copy(0, sibling, me).wait_recv()
            for j, chip in enumerate(chips):
                copy(4 + j, (*chip, 1 - cc), me).wait_recv()
            for cp in first + passed:
                cp.wait_send()

        return start, finish

    return _Carried((buf,), (jax.ShapeDtypeStruct(buf.shape, buf.dtype),), {0: 0}, (pltpu.SemaphoreType.DMA((7,)),) * 2, steps)


def _carry_allgather(fulls, axes, shapes):
    n = len(fulls)
    return _Carried(tuple(fulls), tuple(jax.ShapeDtypeStruct(f.shape, f.dtype) for f in fulls),
                    {i: i for i in range(n)}, (pltpu.SemaphoreType.DMA((3 * n,)),) * 4,
                    lambda ins, outs, sems: _allgather_weights_steps(outs, axes, shapes, *sems))


def _allgather_weights_steps(outs, axes, shapes, send_s, recv_s, fsend_s, frecv_s):
    n = len(outs)
    x, y, c = _mesh_pos()
    q = 2 * x + y
    sibling = (x, y, 1 - c)
    chips = [(1 - x, y), (x, 1 - y), (1 - x, 1 - y)]

    def region(i, qq, half):
        return _shard_region(outs[i], axes[i], shapes[i], qq, half)

    def remote(src, dst, ss, rs, to):
        return pltpu.make_async_remote_copy(src_ref=src, dst_ref=dst, send_sem=ss, recv_sem=rs,
                                            device_id=to, device_id_type=MESH)

    def ici(i, j, qq):
        cx, cy = chips[j]
        reg = region(i, qq, c)
        return remote(reg, reg, send_s.at[3 * i + j], recv_s.at[3 * i + j], (cx, cy, c))

    def d2d(i, j, half):
        cx, cy = chips[j]
        reg = region(i, 2 * cx + cy, half)
        return remote(reg, reg, fsend_s.at[3 * i + j], frecv_s.at[3 * i + j], sibling)

    def start():
        for i in range(n):
            for j in range(3):
                ici(i, j, q).start()

    def finish():
        for i in range(n):
            for j, (cx, cy) in enumerate(chips):
                ici(i, j, 2 * cx + cy).wait_recv()
                d2d(i, j, c).start()
        for i in range(n):
            for j in range(3):
                d2d(i, j, 1 - c).wait_recv()
        for i in range(n):
            for j in range(3):
                ici(i, j, q).wait_send()
                d2d(i, j, c).wait_send()

    return start, finish


def _carry_sibling_exchange(grads, axes, shapes):
    n = len(grads)

    def steps(ins, theirs, sems):
        send_s, recv_s = sems
        x, y, c = _mesh_pos()

        def copies():
            return [pltpu.make_async_remote_copy(
                src_ref=_shard_region(ins[i], axes[i], shapes[i], qq, 1 - c), dst_ref=theirs[i].at[qq],
                send_sem=send_s.at[N_CHIPS * i + qq], recv_sem=recv_s.at[N_CHIPS * i + qq],
                device_id=(x, y, 1 - c), device_id_type=MESH) for i in range(n) for qq in range(N_CHIPS)]

        def start():
            for cp in copies():
                cp.start()

        def finish():
            for cp in copies():
                cp.wait()

        return start, finish

    stacked = tuple(jax.ShapeDtypeStruct((N_CHIPS, R // 2, C), F32) for (R, C) in shapes)
    return _Carried(tuple(grads), stacked, {}, (pltpu.SemaphoreType.DMA((N_CHIPS * n,)),) * 2, steps)


def _carry_chip_scatter(parts):
    n = len(parts)

    def steps(ins, outs, sems):
        send_s, recv_s = sems
        x, y, c = _mesh_pos()
        chips = [(1 - x, y), (x, 1 - y), (1 - x, 1 - y)]

        def copies():
            return [pltpu.make_async_remote_copy(
                src_ref=ins[i].at[2 * cx + cy], dst_ref=outs[i].at[j],
                send_sem=send_s.at[3 * i + j], recv_sem=recv_s.at[3 * i + j],
                device_id=(cx, cy, c), device_id_type=MESH) for i in range(n) for j, (cx, cy) in enumerate(chips)]

        def start():
            for cp in copies():
                cp.start()

        def finish():
            for cp in copies():
                cp.wait()

        return start, finish

    return _Carried(tuple(parts), tuple(jax.ShapeDtypeStruct((3,) + p.shape[1:], p.dtype) for p in parts), {},
                    (pltpu.SemaphoreType.DMA((3 * n,)),) * 2, steps)


def _carry_sibling_share(fulls):
    n = len(fulls)

    def steps(ins, outs, sems):
        send_s, recv_s = sems
        x, y, c = _mesh_pos()

        def copy(i, half):
            rh = fulls[i].shape[0] // 2
            rows = outs[i].at[pl.ds(half * rh, rh), :]
            return pltpu.make_async_remote_copy(src_ref=rows, dst_ref=rows, send_sem=send_s.at[i], recv_sem=recv_s.at[i],
                                                device_id=(x, y, 1 - c), device_id_type=MESH)

        def start():
            for i in range(n):
                copy(i, c).start()

        def finish():
            for i in range(n):
                copy(i, 1 - c).wait_recv()
                copy(i, c).wait_send()

        return start, finish

    return _Carried(tuple(fulls), tuple(jax.ShapeDtypeStruct(f.shape, f.dtype) for f in fulls),
                    {i: i for i in range(n)}, (pltpu.SemaphoreType.DMA((n,)),) * 2, steps)


def _row_block(rows, target=256):
    return target if rows % target == 0 else rows


BF16_TILE_ROWS = 16


def _common_steps(rows, most=8):
    ns = most
    while ns > 1 and any(r % (ns * BF16_TILE_ROWS) for r in rows):
        ns //= 2
    return ns


def _cast_into_full(ws, axes, pos, name):
    n = len(ws)
    ns = _common_steps([w.shape[0] for w in ws])
    in_specs, out_specs, out_shape = [], [], []
    for w, axis in zip(ws, axes):
        R, C = w.shape
        in_specs.append(pl.BlockSpec((R // ns, C), lambda i, s: (i, 0)))
        if axis == 1:
            out_shape.append(jax.ShapeDtypeStruct((R, N_CHIPS * C), BF16))
            out_specs.append(pl.BlockSpec((R // ns, C), lambda i, s: (i, s[0])))
        else:
            out_shape.append(jax.ShapeDtypeStruct((N_CHIPS * R, C), BF16))
            out_specs.append(pl.BlockSpec((R // ns, C), lambda i, s: (s[0] * ns + i, 0)))

    def body(s_ref, *refs):
        for k in range(n):
            refs[n + k][...] = refs[k][...].astype(BF16)

    return pl.pallas_call(
        body, out_shape=out_shape,
        grid_spec=pltpu.PrefetchScalarGridSpec(num_scalar_prefetch=1, grid=(ns,), in_specs=in_specs, out_specs=out_specs),
        compiler_params=_cparams(), name=name)(pos, *ws)


def _presum(gs, theirs, axes, shapes, pos, name):
    n = len(gs)
    ns = _common_steps([R // 2 for R, _ in shapes], most=2)
    in_specs, t_specs, out_shape = [], [], []
    for (R, C), axis in zip(shapes, axes):
        rb = R // 2 // ns
        if axis == 1:
            in_specs.append(pl.BlockSpec((rb, C), lambda k, i, s: (s[1] * ns + i, k)))
        else:
            in_specs.append(pl.BlockSpec((rb, C), lambda k, i, s: (k * 2 * ns + s[1] * ns + i, 0)))
        t_specs.append(pl.BlockSpec((None, rb, C), lambda k, i, s: (k, i, 0)))
        out_shape.append(jax.ShapeDtypeStruct((N_CHIPS, R // 2, C), BF16))

    def body(s_ref, *refs):
        for k in range(n):
            refs[2 * n + k][...] = (refs[k][...] + refs[n + k][...]).astype(BF16)

    return pl.pallas_call(
        body, out_shape=out_shape,
        grid_spec=pltpu.PrefetchScalarGridSpec(num_scalar_prefetch=1, grid=(N_CHIPS, ns), in_specs=in_specs + t_specs,
                                               out_specs=t_specs),
        compiler_params=_cparams(), name=name)(pos, *gs, *theirs)


def _sum_chips(owns, recvs, pos, name):
    n = len(owns)
    ns = _common_steps([o.shape[1] for o in owns], most=2)
    o_specs, r_specs, out_specs, out_shape = [], [], [], []
    for o in owns:
        _, Rh, C = o.shape
        rb = Rh // ns
        o_specs.append(pl.BlockSpec((None, rb, C), lambda i, s: (s[0], i, 0)))
        r_specs.append(pl.BlockSpec((3, rb, C), lambda i, s: (0, i, 0)))
        out_specs.append(pl.BlockSpec((rb, C), lambda i, s: (s[1] * ns + i, 0)))
        out_shape.append(jax.ShapeDtypeStruct((2 * Rh, C), F32))

    def body(s_ref, *refs):
        for k in range(n):
            acc = refs[k][...].astype(F32)
            for j in range(3):
                acc = acc + refs[n + k][j].astype(F32)
            refs[2 * n + k][...] = acc

    return pl.pallas_call(
        body, out_shape=out_shape,
        grid_spec=pltpu.PrefetchScalarGridSpec(num_scalar_prefetch=1, grid=(ns,), in_specs=o_specs + r_specs,
                                               out_specs=out_specs),
        compiler_params=_cparams(), name=name)(pos, *owns, *recvs)


def _sum_devices(parts, name):
    K, R, C = parts.shape

    def body(p_ref, o_ref):
        acc = p_ref[0]
        for k in range(1, K):
            acc = acc + p_ref[k]
        o_ref[...] = acc

    return pl.pallas_call(body, out_shape=jax.ShapeDtypeStruct((R, C), F32), name=name)(parts)


def _adamw_math(w, g, m, v):
    nm = ADAM_B1 * m + (1.0 - ADAM_B1) * g
    nv = ADAM_B2 * v + (1.0 - ADAM_B2) * (g * g)
    m_hat = nm / (1.0 - ADAM_B1 ** ADAM_STEP)
    v_hat = nv / (1.0 - ADAM_B2 ** ADAM_STEP)
    return -ADAM_LR * (m_hat / (jnp.sqrt(v_hat) + ADAM_EPS) + ADAM_WD * w), nm, nv


def _adamw_many(ws, gs, ms, vs):
    n = len(ws)

    def body(*refs):
        for k in range(n):
            w, g, m, v = (refs[j * n + k][...] for j in range(4))
            for j, val in enumerate(_adamw_math(w, g, m, v)):
                refs[(4 + j) * n + k][...] = val

    shapes = [jax.ShapeDtypeStruct(w.shape, F32) for w in ws]
    res = pl.pallas_call(body, out_shape=shapes * 3, compiler_params=_cparams(), name="adamw_small")(*ws, *gs, *ms, *vs)
    return res[:n], res[n:2 * n], res[2 * n:]


def _adamw(ws, gs, ms, vs, name):
    n = len(ws)
    ns = _common_steps([w.shape[0] for w in ws])
    specs = [pl.BlockSpec((w.shape[0] // ns, w.shape[1]), lambda i: (i, 0)) for w in ws]

    def body(*refs):
        for k in range(n):
            w, g, m, v = (refs[j * n + k][...] for j in range(4))
            for j, val in enumerate(_adamw_math(w, g, m, v)):
                refs[(4 + j) * n + k][...] = val

    shapes = [jax.ShapeDtypeStruct(w.shape, F32) for w in ws]
    res = pl.pallas_call(body, out_shape=shapes * 3, grid=(ns,), in_specs=specs * 4, out_specs=specs * 3,
                         compiler_params=_cparams(), name=name)(*ws, *gs, *ms, *vs)
    return res[:n], res[n:2 * n], res[2 * n:]


def _silu(v):
    return v * jax.nn.sigmoid(v)


def _ada_fwd(c_all, w_sh, b_sh):
    S, D = c_all.shape
    Ca = w_sh.shape[1]
    cb = 512 if Ca % 512 == 0 else Ca

    def body(c_ref, w_ref, b_ref, o_ref):
        act = _silu(c_ref[...]).astype(BF16)
        o_ref[...] = _dot(act, w_ref[...].astype(BF16)) + b_ref[...]

    return pl.pallas_call(
        body, out_shape=jax.ShapeDtypeStruct((S, Ca), F32), grid=(Ca // cb,),
        in_specs=[pl.BlockSpec((S, D), lambda j: (0, 0)), pl.BlockSpec((D, cb), lambda j: (0, j)),
                  pl.BlockSpec((1, cb), lambda j: (0, j))],
        out_specs=pl.BlockSpec((S, cb), lambda j: (0, j)), name="ada_fwd")(c_all, w_sh, b_sh)


def _ada_bwd(c_all, dmod_sh, dmod_all):
    S, D = c_all.shape
    Ca = dmod_sh.shape[1]
    C6 = dmod_all.shape[1]

    def body(c_ref, ds_ref, da_ref, gw_ref, gb_ref):
        act = _silu(c_ref[...]).astype(BF16)
        gw_ref[...] = _dot_tn(act, ds_ref[...].astype(BF16))
        gb_ref[...] = jnp.sum(da_ref[...], axis=0, keepdims=True)

    return pl.pallas_call(
        body, out_shape=[jax.ShapeDtypeStruct((D, Ca), F32), jax.ShapeDtypeStruct((1, C6), F32)],
        compiler_params=_cparams(), name="ada_bwd")(c_all, dmod_sh, dmod_all)


def _rms_fwd(xv):
    r = lax.rsqrt(jnp.mean(xv * xv, axis=-1, keepdims=True) + RMS_EPS)
    return xv * r, r


def _rms_bwd(dxh, xh, r):
    return r * (dxh - xh * jnp.mean(dxh * xh, axis=-1, keepdims=True))


def _const_spec(shape):
    nd = len(shape)
    return pl.BlockSpec(shape, lambda *_: (0,) * nd)


def _seq_spec(D, bps, rev_blocks=None):
    if rev_blocks is None:
        return pl.BlockSpec((None, 1, D), lambda i: (i // bps, 0, 0))
    return pl.BlockSpec((None, 1, D), lambda i: ((rev_blocks - 1 - i) // bps, 0, 0))


def _inproj_fwd(x2, sh1, sc1, n1g, w_in, carried):
    N, D = x2.shape
    IN = w_in.shape[1]
    Bl = sh1.shape[0]
    TB = _row_block(N // Bl, 512)
    bps = (N // Bl) // TB

    def body(x_ref, sh_ref, sc_ref, g_ref, w_ref, p_ref):
        xh, _ = _rms_fwd(x_ref[...])
        h = (xh * g_ref[...]) * (1.0 + sc_ref[...]) + sh_ref[...]
        p_ref[...] = _dot(h.astype(BF16), w_ref[...]).astype(BF16)

    (p2,), extra = _call_carrying(
        body, carried, (x2, sh1, sc1, n1g, w_in), out_shape=[jax.ShapeDtypeStruct((N, IN), BF16)], grid=(N // TB,),
        in_specs=[pl.BlockSpec((TB, D), lambda i: (i, 0)), _seq_spec(D, bps), _seq_spec(D, bps),
                  _const_spec((1, D)), _const_spec((D, IN))],
        out_specs=[pl.BlockSpec((TB, IN), lambda i: (i, 0))],
        compiler_params=_cparams(dimension_semantics=("arbitrary",)), name="inproj_fwd")
    return p2, extra


def _s5_dims(Bl, L, Ds):
    G = Ds // SSM_GROUP
    GP = G * SSM_STATE
    NP = GP // LANES
    T = min(64, L // 2)
    nb = 2 if (Ds // 2) % LANES == 0 else 1
    return G, GP, NP, T, nb


def _s5_disc_math(lr, li, ldt, bt_r, bt_i):
    dt = jnp.exp(ldt)
    er = jnp.exp(lr * dt)
    lbr = er * jnp.cos(li * dt)
    lbi = er * jnp.sin(li * dt)
    den = lr * lr + li * li
    fr = ((lbr - 1.0) * lr + lbi * li) / den
    fi = (lbi * lr - (lbr - 1.0) * li) / den
    return lbr, lbi, fr[None] * bt_r - fi[None] * bt_i, fr[None] * bt_i + fi[None] * bt_r


def _s5_disc(lr, li, ldt, bt_r, bt_i):
    def body(lr_ref, li_ref, ldt_ref, br_ref, bi_ref, o0, o1, o2, o3):
        res = _s5_disc_math(lr_ref[...], li_ref[...], ldt_ref[...], br_ref[...], bi_ref[...])
        for o, v in zip((o0, o1, o2, o3), res):
            o[...] = v

    S = jax.ShapeDtypeStruct
    return pl.pallas_call(body, out_shape=[S(lr.shape, F32)] * 2 + [S(bt_r.shape, F32)] * 2, name="s5_disc")(lr, li, ldt, bt_r, bt_i)


def _s5_disc_bwd(lr, li, ldt, bt_r, bt_i, dlbr, dlbi, dbbr, dbbi):
    def body(lr_ref, li_ref, ldt_ref, br_ref, bi_ref, g0, g1, g2, g3, o0, o1, o2, o3, o4):
        _, vjp = jax.vjp(_s5_disc_math, lr_ref[...], li_ref[...], ldt_ref[...], br_ref[...], bi_ref[...])
        res = vjp((g0[...], g1[...], g2[...], g3[...]))
        for o, v in zip((o0, o1, o2, o3, o4), res):
            o[...] = v

    S = jax.ShapeDtypeStruct
    return pl.pallas_call(body, out_shape=[S(lr.shape, F32)] * 2 + [S(ldt.shape, F32)] + [S(bt_r.shape, F32)] * 2,
                          name="s5_disc_bwd")(lr, li, ldt, bt_r, bt_i, dlbr, dlbi, dbbr, dbbi)


S5_SCAN_PANELS = 4
S5_SCAN_STEPS = 4


def _panel_scan(src_r, src_i, dst_r, dst_i, lr_ref, li_ref, car_r, car_i, NP, Bl, T, TP, adjoint):
    BT = Bl * TP
    PG = min(S5_SCAN_PANELS, NP)
    CH = S5_SCAN_STEPS
    for k0 in range(0, NP, PG):
        ks = list(range(k0, k0 + PG))
        lr = [jnp.broadcast_to(lr_ref[pl.ds(k, 1), :], (Bl, LANES)) for k in ks]
        li = [jnp.broadcast_to(li_ref[pl.ds(k, 1), :], (Bl, LANES)) for k in ks]

        def trip(cc, carry):
            t0 = (T // CH - 1 - cc) * CH if adjoint else cc * CH
            ts = [t0 + (CH - 1 - s if adjoint else s) for s in range(CH)]
            idx = [[pl.ds(k * BT + t, Bl, stride=TP) for t in ts] for k in ks]
            loaded = [[(src_r[ix, :], src_i[ix, :]) for ix in idx[j]] for j in range(PG)]
            results, new_carry = [], []
            for j in range(PG):
                ar, ai = carry[j]
                res = []
                for s in range(CH):
                    br, bi = loaded[j][s]
                    if adjoint:
                        ar, ai = br + lr[j] * ar + li[j] * ai, bi + lr[j] * ai - li[j] * ar
                    else:
                        ar, ai = lr[j] * ar - li[j] * ai + br, lr[j] * ai + li[j] * ar + bi
                    res.append((ar, ai))
                results.append(res)
                new_carry.append((ar, ai))
            for j in range(PG):
                for s in range(CH):
                    dst_r[idx[j][s], :] = results[j][s][0]
                    dst_i[idx[j][s], :] = results[j][s][1]
            return tuple(new_carry)

        init = tuple((car_r[pl.ds(k * SUBLANES, Bl), :], car_i[pl.ds(k * SUBLANES, Bl), :]) for k in ks)
        fin = lax.fori_loop(0, T // CH, trip, init, unroll=2)
        for j, k in enumerate(ks):
            car_r[pl.ds(k * SUBLANES, Bl), :] = fin[j][0]
            car_i[pl.ds(k * SUBLANES, Bl), :] = fin[j][1]


def _s5_fwd(p3, Bm_r, Bm_i, Cm_r, Cm_i, lam_r, lam_i, dsk, carried):
    Bl, L, _ = p3.shape
    Ds = dsk.shape[1]
    G, GP, NP, T, nb = _s5_dims(Bl, L, Ds)
    nT = L // T
    TP = T + S5_ROW_PAD
    BT = Bl * TP
    dsb, gpb, npb = Ds // nb, GP // nb, NP // nb

    def body(u_ref, br_ref, bi_ref, cr_ref, ci_ref, lr_ref, li_ref, dsk_ref, sr_ref, si_ref, sb_r, sb_i, y_ref,
             car_r, car_i, upad, ypad, bu_r, bu_i):
        i = pl.program_id(0)

        @pl.when(i == 0)
        def _():
            car_r[...] = jnp.zeros_like(car_r)
            car_i[...] = jnp.zeros_like(car_i)
            upad[...] = jnp.zeros_like(upad)

        zpad = jnp.zeros((S5_ROW_PAD, LANES), F32)
        for k in range(NP):
            for b in range(Bl):
                sr_ref[pl.ds(k * BT + b * TP + T, S5_ROW_PAD), :] = zpad
                si_ref[pl.ds(k * BT + b * TP + T, S5_ROW_PAD), :] = zpad
        for b in range(Bl):
            upad[pl.ds(b * TP, T), :] = u_ref[b].astype(F32)
        u = upad[...]
        ub = u.astype(BF16)
        for blk in range(nb):
            ubb = ub[:, blk * dsb:(blk + 1) * dsb]
            for bu_ref, b_ref in ((bu_r, br_ref), (bu_i, bi_ref)):
                res = _dot(ubb, b_ref[blk])
                for kk in range(npb):
                    k = blk * npb + kk
                    bu_ref[pl.ds(k * BT, BT), :] = res[:, kk * LANES:(kk + 1) * LANES]

        _panel_scan(bu_r, bu_i, sr_ref, si_ref, lr_ref, li_ref, car_r, car_i, NP, Bl, T, TP, adjoint=False)
        sb_r[...] = car_r[...]
        sb_i[...] = car_i[...]

        for blk in range(nb):
            s_r = jnp.concatenate([sr_ref[pl.ds((blk * npb + kk) * BT, BT), :] for kk in range(npb)], axis=1).astype(BF16)
            s_i = jnp.concatenate([si_ref[pl.ds((blk * npb + kk) * BT, BT), :] for kk in range(npb)], axis=1).astype(BF16)
            cols = slice(blk * dsb, (blk + 1) * dsb)
            ypad[:, cols] = _dot_nt(s_r, cr_ref[blk]) + _dot_nt(s_i, ci_ref[blk]) + dsk_ref[:, cols] * u[:, cols]
        for b in range(Bl):
            y_ref[b] = ypad[pl.ds(b * TP, T), :]

    S = jax.ShapeDtypeStruct
    state = S((nT, NP * BT, LANES), F32)
    bound = S((nT, NP * SUBLANES, LANES), F32)
    sspec = pl.BlockSpec((None, NP * BT, LANES), lambda i: (i, 0, 0))
    bspec = pl.BlockSpec((None, NP * SUBLANES, LANES), lambda i: (i, 0, 0))
    return _call_carrying(
        body, carried, (p3, Bm_r, Bm_i, Cm_r, Cm_i, lam_r, lam_i, dsk),
        out_shape=[state, state, bound, bound, S((Bl, L, Ds), F32)], grid=(nT,),
        in_specs=[pl.BlockSpec((Bl, T, Ds), lambda i: (0, i, 0)),
                  _const_spec((nb, dsb, gpb)), _const_spec((nb, dsb, gpb)),
                  _const_spec((nb, dsb, gpb)), _const_spec((nb, dsb, gpb)),
                  _const_spec((NP, LANES)), _const_spec((NP, LANES)), _const_spec((1, Ds))],
        out_specs=[sspec, sspec, bspec, bspec, pl.BlockSpec((Bl, T, Ds), lambda i: (0, i, 0))],
        scratch_shapes=[pltpu.VMEM((NP * SUBLANES, LANES), F32)] * 2 + [pltpu.VMEM((BT, Ds), F32)] * 2
        + [pltpu.VMEM((NP * BT, LANES), F32)] * 2,
        compiler_params=_cparams(dimension_semantics=("arbitrary",)), name="s5_fwd")


def _s5_bwd(dy3, p3, dp3, Sr, Si, Sb_r, Sb_i, Bm_r, Bm_i, Cm_r, Cm_i, lam_r, lam_i, dsk, carried):
    Bl, L, Ds = dy3.shape
    G, GP, NP, T, nb = _s5_dims(Bl, L, Ds)
    nT = L // T
    TP = T + S5_ROW_PAD
    BT = Bl * TP
    dsb, gpb, npb = Ds // nb, GP // nb, NP // nb

    def body(dy_ref, u_ref, dp_ref, sr_ref, si_ref, sbr_ref, sbi_ref, br_ref, bi_ref, cr_ref, ci_ref, lr_ref, li_ref, dsk_ref,
             du_ref, dbr_ref, dbi_ref, dcr_ref, dci_ref, dlr_ref, dli_ref, ddsk_ref,
             a_r, a_i, car_r, car_i, acc_r, acc_i, dypad, upad, dupad, q_r, q_i):
        i = pl.program_id(0)

        @pl.when(i == 0)
        def _():
            for ref in (car_r, car_i, acc_r, acc_i, dbr_ref, dbi_ref, dcr_ref, dci_ref, ddsk_ref, dypad, upad, a_r, a_i):
                ref[...] = jnp.zeros_like(ref)

        for b in range(Bl):
            dypad[pl.ds(b * TP, T), :] = dy_ref[b]
            upad[pl.ds(b * TP, T), :] = u_ref[b].astype(F32)
        dy = dypad[...]
        dyb = dy.astype(BF16)
        u = upad[...]
        ub = u.astype(BF16)
        for blk in range(nb):
            dyb_b = dyb[:, blk * dsb:(blk + 1) * dsb]
            for q_ref, c_ref in ((q_r, cr_ref), (q_i, ci_ref)):
                res = _dot(dyb_b, c_ref[blk])
                for kk in range(npb):
                    q_ref[pl.ds((blk * npb + kk) * BT, BT), :] = res[:, kk * LANES:(kk + 1) * LANES]

        _panel_scan(q_r, q_i, a_r, a_i, lr_ref, li_ref, car_r, car_i, NP, Bl, T, TP, adjoint=True)

        first_block = (i == nT - 1)
        for k in range(NP):
            rows = pl.ds(k * BT, BT)
            av_r, av_i = a_r[rows, :], a_i[rows, :]
            sp_r = pltpu.roll(sr_ref[rows, :], 1, 0)
            sp_i = pltpu.roll(si_ref[rows, :], 1, 0)
            acc = pl.ds(k * SUBLANES, SUBLANES)
            acc_r[acc, :] += jnp.sum((av_r * sp_r + av_i * sp_i).reshape(BT // SUBLANES, SUBLANES, LANES), axis=0)
            acc_i[acc, :] += jnp.sum((av_i * sp_r - av_r * sp_i).reshape(BT // SUBLANES, SUBLANES, LANES), axis=0)
            t0 = pl.ds(k * BT, Bl, stride=TP)
            a0_r, a0_i = a_r[t0, :], a_i[t0, :]
            brow = pl.ds(k * SUBLANES, Bl)
            sb_pr = jnp.where(first_block, 0.0, sbr_ref[brow, :])
            sb_pi = jnp.where(first_block, 0.0, sbi_ref[brow, :])
            acc_r[brow, :] += a0_r * sb_pr + a0_i * sb_pi
            acc_i[brow, :] += a0_i * sb_pr - a0_r * sb_pi

        ddsk_ref[...] += jnp.sum(dy * u, axis=0, keepdims=True)
        for blk in range(nb):
            cols = slice(blk * dsb, (blk + 1) * dsb)
            rows = [pl.ds((blk * npb + kk) * BT, BT) for kk in range(npb)]
            av_r = jnp.concatenate([a_r[r, :] for r in rows], axis=1).astype(BF16)
            av_i = jnp.concatenate([a_i[r, :] for r in rows], axis=1).astype(BF16)
            dupad[:, cols] = _dot_nt(av_r, br_ref[blk]) + _dot_nt(av_i, bi_ref[blk]) + dy[:, cols] * dsk_ref[:, cols]
            dbr_ref[blk] += _dot_tn(ub[:, cols], av_r)
            dbi_ref[blk] += _dot_tn(ub[:, cols], av_i)
            sv_r = jnp.concatenate([sr_ref[r, :] for r in rows], axis=1).astype(BF16)
            sv_i = jnp.concatenate([si_ref[r, :] for r in rows], axis=1).astype(BF16)
            dcr_ref[blk] += _dot_tn(dyb[:, cols], sv_r)
            dci_ref[blk] += _dot_tn(dyb[:, cols], sv_i)
        for b in range(Bl):
            du_ref[b] = dupad[pl.ds(b * TP, T), :].astype(BF16)

        @pl.when(i == nT - 1)
        def _():
            for k in range(NP):
                dlr_ref[pl.ds(k, 1), :] = jnp.sum(acc_r[pl.ds(k * SUBLANES, SUBLANES), :], axis=0, keepdims=True)
                dli_ref[pl.ds(k, 1), :] = jnp.sum(acc_i[pl.ds(k * SUBLANES, SUBLANES), :], axis=0, keepdims=True)

    S = jax.ShapeDtypeStruct
    rev = lambda i: nT - 1 - i
    sspec = pl.BlockSpec((None, NP * BT, LANES), lambda i: (rev(i), 0, 0))
    bspec = pl.BlockSpec((None, NP * SUBLANES, LANES), lambda i: (jnp.maximum(rev(i) - 1, 0), 0, 0))
    tspec = pl.BlockSpec((Bl, T, Ds), lambda i: (0, rev(i), 0))
    return _call_carrying(
        body, carried, (dy3, p3, dp3, Sr, Si, Sb_r, Sb_i, Bm_r, Bm_i, Cm_r, Cm_i, lam_r, lam_i, dsk),
        out_shape=[S(dp3.shape, dp3.dtype), S((nb, dsb, gpb), F32), S((nb, dsb, gpb), F32),
                   S((nb, dsb, gpb), F32), S((nb, dsb, gpb), F32), S((NP, LANES), F32), S((NP, LANES), F32), S((1, Ds), F32)],
        grid=(nT,),
        in_specs=[tspec, tspec, ANY, sspec, sspec, bspec, bspec,
                  _const_spec((nb, dsb, gpb)), _const_spec((nb, dsb, gpb)),
                  _const_spec((nb, dsb, gpb)), _const_spec((nb, dsb, gpb)),
                  _const_spec((NP, LANES)), _const_spec((NP, LANES)), _const_spec((1, Ds))],
        out_specs=[tspec, _const_spec((nb, dsb, gpb)), _const_spec((nb, dsb, gpb)),
                   _const_spec((nb, dsb, gpb)), _const_spec((nb, dsb, gpb)),
                   _const_spec((NP, LANES)), _const_spec((NP, LANES)), _const_spec((1, Ds))],
        aliases={2: 0},
        scratch_shapes=[pltpu.VMEM((NP * BT, LANES), F32)] * 2 + [pltpu.VMEM((NP * SUBLANES, LANES), F32)] * 4
        + [pltpu.VMEM((BT, Ds), F32)] * 3 + [pltpu.VMEM((NP * BT, LANES), F32)] * 2,
        compiler_params=_cparams(dimension_semantics=("arbitrary",)), name="s5_bwd")


def _mix_values(ylin, cb, cc, cx, gs, gc, halo_v, wglu, bglu, cw, wps, wpc, wout):
    yg, gelu_vjp = jax.vjp(jax.nn.gelu, ylin)
    sz = jax.nn.sigmoid(_dot(yg.astype(BF16), wglu) + bglu)
    ys = yg * sz
    v = cc * cx
    rows = lax.broadcasted_iota(jnp.int32, v.shape, 0)
    h6 = halo_v[HALO_ROWS - 2:HALO_ROWS - 1, :]
    h7 = halo_v[HALO_ROWS - 1:HALO_ROWS, :]
    v1 = jnp.where(rows == 0, h7, pltpu.roll(v, 1, 0))
    v2 = jnp.where(rows == 0, h6, jnp.where(rows == 1, h7, pltpu.roll(v, 2, 0)))
    cv = cw[0:1, :] * v2 + cw[1:2, :] * v1 + cw[2:3, :] * v
    yc = cb * cv
    ps = _dot(ys.astype(BF16), wps)
    pc = _dot(yc.astype(BF16), wpc)
    sgs = jax.nn.sigmoid(gs)
    sgc = jax.nn.sigmoid(gc)
    merged = sgs * ps + sgc * pc
    mo = _dot(merged.astype(BF16), wout)
    return dict(yg=yg, gelu_vjp=gelu_vjp, sz=sz, ys=ys, v=v, v1=v1, v2=v2, cv=cv, yc=yc, ps=ps, pc=pc,
                sgs=sgs, sgc=sgc, merged=merged, mo=mo)


def _mix_in_specs(TB, D, Ds, Dc, bps, blk):
    hb = TB // HALO_ROWS
    halo = lambda col: pl.BlockSpec((HALO_ROWS, Dc), lambda i: (jnp.maximum(blk(i) * hb - 1, 0), col))
    return [pl.BlockSpec((TB, Dc), lambda i: (blk(i), 1)), pl.BlockSpec((TB, Dc), lambda i: (blk(i), 2)),
            pl.BlockSpec((TB, Dc), lambda i: (blk(i), 3)), pl.BlockSpec((TB, D), lambda i: (blk(i), 2)),
            pl.BlockSpec((TB, D), lambda i: (blk(i), 3)), halo(2), halo(3),
            pl.BlockSpec((TB, Ds), lambda i: (blk(i), 0))]


def _mix_fwd(p2, ylin2, x2, g1, wglu, bglu, cw, wps, wpc, wout):
    N, D = x2.shape
    Ds = ylin2.shape[1]
    Dc = Ds
    Bl = g1.shape[0]
    TB = _row_block(N // Bl, 512)
    bps = (N // Bl) // TB

    def body(cb_ref, cc_ref, cx_ref, gs_ref, gc_ref, hcc_ref, hcx_ref, yl_ref, x_ref, g1_ref,
             wglu_ref, bglu_ref, cw_ref, wps_ref, wpc_ref, wout_ref, x1_ref):
        i = pl.program_id(0)
        f32 = lambda ref: ref[...].astype(F32)
        halo_v = jnp.where(i % bps == 0, 0.0, f32(hcc_ref) * f32(hcx_ref))
        f = _mix_values(yl_ref[...], f32(cb_ref), f32(cc_ref), f32(cx_ref), f32(gs_ref), f32(gc_ref), halo_v,
                        wglu_ref[...], bglu_ref[...], cw_ref[...], wps_ref[...], wpc_ref[...], wout_ref[...])
        x1_ref[...] = x_ref[...] + g1_ref[...] * f["mo"]

    return pl.pallas_call(
        body, out_shape=jax.ShapeDtypeStruct((N, D), F32), grid=(N // TB,),
        in_specs=_mix_in_specs(TB, D, Ds, Dc, bps, lambda i: i) + [
            pl.BlockSpec((TB, D), lambda i: (i, 0)), _seq_spec(D, bps),
            _const_spec((Ds, Ds)), _const_spec((1, Ds)), _const_spec((SUBLANES, Dc)),
            _const_spec((Ds, D)), _const_spec((Dc, D)), _const_spec((D, D))],
        out_specs=pl.BlockSpec((TB, D), lambda i: (i, 0)),
        compiler_params=_cparams(), name="mix_fwd",
    )(p2, p2, p2, p2, p2, p2, p2, ylin2, x2, g1, wglu, bglu, cw, wps, wpc, wout)


def _mix_bwd(p2, ylin2, dx1, g1, wglu, bglu, cw, wps, wpc, wout, carried):
    N, D = dx1.shape
    Ds = ylin2.shape[1]
    Dc = Ds
    IN = p2.shape[1]
    Bl = g1.shape[0]
    TB = _row_block(N // Bl)
    bps = (N // Bl) // TB
    nblk = N // TB
    rev = lambda i: nblk - 1 - i

    def body(cb_ref, cc_ref, cx_ref, gs_ref, gc_ref, hcc_ref, hcx_ref, yl_ref, dx1_ref, g1_ref,
             wglu_ref, bglu_ref, cw_ref, wps_ref, wpc_ref, wout_ref,
             dyl_ref, dp_ref, gwout_ref, gwps_ref, gwpc_ref, gwglu_ref, gbglu_ref, gcw_ref, dg1_ref, nxt):
        i = pl.program_id(0)
        blk = rev(i)

        @pl.when(i == 0)
        def _():
            for ref in (gwout_ref, gwps_ref, gwpc_ref, gwglu_ref, gbglu_ref, gcw_ref):
                ref[...] = jnp.zeros_like(ref)

        @pl.when(i % bps == 0)
        def _():
            nxt[...] = jnp.zeros_like(nxt)
            dg1_ref[...] = jnp.zeros_like(dg1_ref)

        f32 = lambda ref: ref[...].astype(F32)
        cb, cc, cx = f32(cb_ref), f32(cc_ref), f32(cx_ref)
        halo_v = jnp.where(blk % bps == 0, 0.0, f32(hcc_ref) * f32(hcx_ref))
        wglu, wps, wpc, wout, cw = wglu_ref[...], wps_ref[...], wpc_ref[...], wout_ref[...], cw_ref[...]
        f = _mix_values(yl_ref[...], cb, cc, cx, f32(gs_ref), f32(gc_ref), halo_v, wglu, bglu_ref[...], cw, wps, wpc, wout)

        dx1v = dx1_ref[...]
        dg1_ref[...] += jnp.sum(dx1v * f["mo"], axis=0, keepdims=True)
        dmo = (g1_ref[...] * dx1v).astype(BF16)
        gwout_ref[...] += _dot_tn(f["merged"].astype(BF16), dmo)
        dmerged = _dot_nt(dmo, wout)
        dps = dmerged * f["sgs"]
        dpc = dmerged * f["sgc"]
        dgs = dmerged * f["ps"] * f["sgs"] * (1.0 - f["sgs"])
        dgc = dmerged * f["pc"] * f["sgc"] * (1.0 - f["sgc"])
        dpsb, dpcb = dps.astype(BF16), dpc.astype(BF16)
        gwps_ref[...] += _dot_tn(f["ys"].astype(BF16), dpsb)
        gwpc_ref[...] += _dot_tn(f["yc"].astype(BF16), dpcb)
        dys = _dot_nt(dpsb, wps)
        dyc = _dot_nt(dpcb, wpc)

        dcb = dyc * f["cv"]
        dcv = dyc * cb
        rows = lax.broadcasted_iota(jnp.int32, dcv.shape, 0)
        n0, n1 = nxt[0:1, :], nxt[1:2, :]
        d1 = jnp.where(rows == TB - 1, n0, pltpu.roll(dcv, TB - 1, 0))
        d2 = jnp.where(rows == TB - 2, n0, jnp.where(rows == TB - 1, n1, pltpu.roll(dcv, TB - 2, 0)))
        dv = cw[2:3, :] * dcv + cw[1:2, :] * d1 + cw[0:1, :] * d2
        nxt[0:2, :] = dcv[0:2, :]
        gcw_ref[0:1, :] += jnp.sum(dcv * f["v2"], axis=0, keepdims=True)
        gcw_ref[1:2, :] += jnp.sum(dcv * f["v1"], axis=0, keepdims=True)
        gcw_ref[2:3, :] += jnp.sum(dcv * f["v"], axis=0, keepdims=True)

        dz = dys * f["yg"] * f["sz"] * (1.0 - f["sz"])
        dzb = dz.astype(BF16)
        gwglu_ref[...] += _dot_tn(f["yg"].astype(BF16), dzb)
        gbglu_ref[...] += jnp.sum(dz, axis=0, keepdims=True)
        dyg = dys * f["sz"] + _dot_nt(dzb, wglu)
        dyl_ref[...] = f["gelu_vjp"](dyg)[0]

        dp_ref[:, Ds:Ds + Dc] = dcb.astype(BF16)
        dp_ref[:, Ds + Dc:Ds + 2 * Dc] = (dv * cx).astype(BF16)
        dp_ref[:, Ds + 2 * Dc:Ds + 3 * Dc] = (dv * cc).astype(BF16)
        dp_ref[:, Ds + 3 * Dc:Ds + 3 * Dc + D] = dgs.astype(BF16)
        dp_ref[:, Ds + 3 * Dc + D:IN] = dgc.astype(BF16)

    S = jax.ShapeDtypeStruct
    return _call_carrying(
        body, carried, (p2, p2, p2, p2, p2, p2, p2, ylin2, dx1, g1, wglu, bglu, cw, wps, wpc, wout),
        out_shape=[S((N, Ds), F32), S((N, IN), BF16), S((D, D), F32), S((Ds, D), F32), S((Dc, D), F32),
                   S((Ds, Ds), F32), S((1, Ds), F32), S((SUBLANES, Dc), F32), S((Bl, 1, D), F32)],
        grid=(nblk,),
        in_specs=_mix_in_specs(TB, D, Ds, Dc, bps, rev) + [
            pl.BlockSpec((TB, D), lambda i: (rev(i), 0)), _seq_spec(D, bps, nblk),
            _const_spec((Ds, Ds)), _const_spec((1, Ds)), _const_spec((SUBLANES, Dc)),
            _const_spec((Ds, D)), _const_spec((Dc, D)), _const_spec((D, D))],
        out_specs=[pl.BlockSpec((TB, Ds), lambda i: (rev(i), 0)), pl.BlockSpec((TB, IN), lambda i: (rev(i), 0)),
                   _const_spec((D, D)), _const_spec((Ds, D)), _const_spec((Dc, D)), _const_spec((Ds, Ds)),
                   _const_spec((1, Ds)), _const_spec((SUBLANES, Dc)), _seq_spec(D, bps, nblk)],
        scratch_shapes=[pltpu.VMEM((SUBLANES, Dc), F32)],
        compiler_params=_cparams(dimension_semantics=("arbitrary",)), name="mix_bwd")


def _mlp_fwd_bwd(x1, tgt, sh2, sc2, g2, n2g, fg, w1, w2):
    N, D = x1.shape
    Dff = w1.shape[1]
    Bl = sh2.shape[0]
    TB = _row_block(N // Bl)
    bps = (N // Bl) // TB

    def body(x1_ref, t_ref, sh_ref, sc_ref, g2_ref, n2_ref, fg_ref, w1_ref, w2_ref,
             dx1_ref, h2_ref, da_ref, sq_ref, df_ref, loss_ref, gfg_ref, gn2_ref, dsh_ref, dsc_ref, dg2_ref):
        i = pl.program_id(0)

        @pl.when(i == 0)
        def _():
            for ref in (loss_ref, gfg_ref, gn2_ref):
                ref[...] = jnp.zeros_like(ref)

        @pl.when(i % bps == 0)
        def _():
            for ref in (dsh_ref, dsc_ref, dg2_ref):
                ref[...] = jnp.zeros_like(ref)

        x1v = x1_ref[...]
        sc, g2v, n2 = sc_ref[...], g2_ref[...], n2_ref[...]
        xh2, r2 = _rms_fwd(x1v)
        xn2 = xh2 * n2
        h2 = (xn2 * (1.0 + sc) + sh_ref[...]).astype(BF16)
        a = _dot(h2, w1_ref[...])
        ra = jnp.maximum(a, 0.0)
        sq = (ra * ra).astype(BF16)
        fv = _dot(sq, w2_ref[...])
        x2 = x1v + g2v * fv
        xh3, r3 = _rms_fwd(x2)
        err = xh3 * fg_ref[...] - t_ref[...]
        loss_ref[...] += 0.5 * jnp.sum(jnp.mean(err * err, axis=-1, keepdims=True), axis=0, keepdims=True)
        dy = err * (1.0 / D)
        gfg_ref[...] += jnp.sum(dy * xh3, axis=0, keepdims=True)
        dx2 = _rms_bwd(dy * fg_ref[...], xh3, r3)
        dg2_ref[...] += jnp.sum(dx2 * fv, axis=0, keepdims=True)
        df = (g2v * dx2).astype(BF16)
        dsq = _dot_nt(df, w2_ref[...])
        da = (2.0 * ra * dsq).astype(BF16)
        dh2 = _dot_nt(da, w1_ref[...])
        dsh_ref[...] += jnp.sum(dh2, axis=0, keepdims=True)
        dsc_ref[...] += jnp.sum(dh2 * xn2, axis=0, keepdims=True)
        dxn2 = dh2 * (1.0 + sc)
        gn2_ref[...] += jnp.sum(dxn2 * xh2, axis=0, keepdims=True)
        dx1_ref[...] = dx2 + _rms_bwd(dxn2 * n2, xh2, r2)
        h2_ref[...] = h2
        da_ref[...] = da
        sq_ref[...] = sq
        df_ref[...] = df

    S = jax.ShapeDtypeStruct
    row = lambda w: pl.BlockSpec((TB, w), lambda i: (i, 0))
    return pl.pallas_call(
        body,
        out_shape=[S((N, D), F32), S((N, D), BF16), S((N, Dff), BF16), S((N, Dff), BF16), S((N, D), BF16),
                   S((1, 1), F32), S((1, D), F32), S((1, D), F32), S((Bl, 1, D), F32), S((Bl, 1, D), F32), S((Bl, 1, D), F32)],
        grid=(N // TB,),
        in_specs=[row(D), row(D), _seq_spec(D, bps), _seq_spec(D, bps), _seq_spec(D, bps),
                  _const_spec((1, D)), _const_spec((1, D)), _const_spec((D, Dff)), _const_spec((Dff, D))],
        out_specs=[row(D), row(D), row(Dff), row(Dff), row(D), _const_spec((1, 1)), _const_spec((1, D)), _const_spec((1, D)),
                   _seq_spec(D, bps), _seq_spec(D, bps), _seq_spec(D, bps)],
        compiler_params=_cparams(dimension_semantics=("arbitrary",)), name="mlp_fwd_bwd",
    )(x1, tgt, sh2, sc2, g2, n2g, fg, w1, w2)


_NO_EXCHANGE = _Carried((), (), {}, (), lambda ins, outs, sems: ((lambda: None), (lambda: None)))


def _grad_w(a, b, name, carried=_NO_EXCHANGE):
    N, K1 = a.shape
    K2 = b.shape[1]
    t1 = 1024 if K1 % 1024 == 0 else K1
    t2 = 1024 if K2 % 1024 == 0 else K2
    tn = 2048 if N % 2048 == 0 else N

    def body(a_ref, b_ref, o_ref):
        @pl.when(pl.program_id(2) == 0)
        def _():
            o_ref[...] = jnp.zeros_like(o_ref)

        o_ref[...] += _dot_tn(a_ref[...], b_ref[...])

    (g,), extra = _call_carrying(
        body, carried, (a, b), out_shape=[jax.ShapeDtypeStruct((K1, K2), F32)], grid=(K1 // t1, K2 // t2, N // tn),
        in_specs=[pl.BlockSpec((tn, t1), lambda i, j, k: (k, i)), pl.BlockSpec((tn, t2), lambda i, j, k: (k, j))],
        out_specs=[pl.BlockSpec((t1, t2), lambda i, j, k: (i, j))],
        compiler_params=_cparams(dimension_semantics=("arbitrary", "arbitrary", "arbitrary")), name=name)
    return g, extra


def _inproj_bwd(x2, dx1, dp, sh1, sc1, n1g, w_in, carried):
    N, D = x2.shape
    IN = w_in.shape[1]
    Bl = sh1.shape[0]
    TB = _row_block(N // Bl, 512)
    bps = (N // Bl) // TB

    def body(x_ref, dx1_ref, dp_ref, sh_ref, sc_ref, g_ref, w_ref, gx_ref, h_ref, gn1_ref, dsh_ref, dsc_ref):
        i = pl.program_id(0)

        @pl.when(i == 0)
        def _():
            gn1_ref[...] = jnp.zeros_like(gn1_ref)

        @pl.when(i % bps == 0)
        def _():
            dsh_ref[...] = jnp.zeros_like(dsh_ref)
            dsc_ref[...] = jnp.zeros_like(dsc_ref)

        sc, n1 = sc_ref[...], g_ref[...]
        xh, r = _rms_fwd(x_ref[...])
        xn = xh * n1
        h_ref[...] = (xn * (1.0 + sc) + sh_ref[...]).astype(BF16)
        dh = _dot_nt(dp_ref[...], w_ref[...])
        dsh_ref[...] += jnp.sum(dh, axis=0, keepdims=True)
        dsc_ref[...] += jnp.sum(dh * xn, axis=0, keepdims=True)
        dxn = dh * (1.0 + sc)
        gn1_ref[...] += jnp.sum(dxn * xh, axis=0, keepdims=True)
        gx_ref[...] = dx1_ref[...] + _rms_bwd(dxn * n1, xh, r)

    S = jax.ShapeDtypeStruct
    row = lambda w: pl.BlockSpec((TB, w), lambda i: (i, 0))
    return _call_carrying(
        body, carried, (x2, dx1, dp, sh1, sc1, n1g, w_in),
        out_shape=[S((N, D), F32), S((N, D), BF16), S((1, D), F32), S((Bl, 1, D), F32), S((Bl, 1, D), F32)],
        grid=(N // TB,),
        in_specs=[row(D), row(D), row(IN), _seq_spec(D, bps), _seq_spec(D, bps), _const_spec((1, D)), _const_spec((D, IN))],
        out_specs=[row(D), row(D), _const_spec((1, D)), _seq_spec(D, bps), _seq_spec(D, bps)],
        compiler_params=_cparams(dimension_semantics=("arbitrary",)), name="inproj_bwd")


def _diag_mask(gb, a, b):
    rows = lax.broadcasted_iota(jnp.int32, (gb * a, gb * b), 0) // a
    cols = lax.broadcasted_iota(jnp.int32, (gb * a, gb * b), 1) // b
    return (rows == cols).astype(F32)


def _diag_blocks_from_groups(m, nb):
    G, a, b = m.shape
    gb = G // nb
    return jnp.tile(m.reshape(nb, gb * a, b), (1, 1, gb)) * _diag_mask(gb, a, b)[None]


def _groups_from_diag_blocks(d, G, a, b):
    nb = d.shape[0]
    gb = G // nb
    picked = (d * _diag_mask(gb, a, b)[None]).reshape(nb, gb * a, gb, b)
    return jnp.sum(picked, axis=2).reshape(G, a, b)


def _pad_rows(v, rows):
    return jnp.concatenate([v, jnp.zeros((rows - v.shape[0],) + v.shape[1:], v.dtype)], axis=0)


def _pack(vs):
    flat = jnp.concatenate([v.reshape(-1) for v in vs])
    n = flat.shape[0]
    tile = SUBLANES * LANES
    npad = -(-n // tile) * tile
    flat = jnp.concatenate([flat, jnp.zeros((npad - n,), flat.dtype)])
    return flat.reshape(npad // LANES, LANES)


def _unpack(packed, shapes):
    flat = packed.reshape(-1)
    out, off = [], 0
    for s in shapes:
        n = 1
        for d in s:
            n *= d
        out.append(flat[off:off + n].reshape(s))
        off += n
    return out


def kernel(x, c, norm1_g, norm2_g, w_ada, b_ada, w_in, lam_re, lam_im, log_dt, b_re, b_im, c_re, c_im, d_skip, w_glu, b_glu, conv_w, w_proj_ssm, w_proj_conv, w_out, w_ff1, w_ff2, final_g, loss_target, m_norm1_g, m_norm2_g, m_w_ada, m_b_ada, m_w_in, m_lam_re, m_lam_im, m_log_dt, m_b_re, m_b_im, m_c_re, m_c_im, m_d_skip, m_w_glu, m_b_glu, m_conv_w, m_w_proj_ssm, m_w_proj_conv, m_w_out, m_w_ff1, m_w_ff2, m_final_g, v_norm1_g, v_norm2_g, v_w_ada, v_b_ada, v_w_in, v_lam_re, v_lam_im, v_log_dt, v_b_re, v_b_im, v_c_re, v_c_im, v_d_skip, v_w_glu, v_b_glu, v_conv_w, v_w_proj_ssm, v_w_proj_conv, v_w_out, v_w_ff1, v_w_ff2, v_final_g):
    Bl, L, D = x.shape
    N = Bl * L
    Ds = Dc = D // 2
    G, H, P = Ds // SSM_GROUP, SSM_GROUP, SSM_STATE
    GP = G * P
    NP = GP // LANES
    nb = _s5_dims(Bl, L, Ds)[4]
    IN = Ds + 3 * Dc + 2 * D
    ax, ay, ac = _mesh_pos()
    q = 2 * ax + ay
    dev = 2 * q + ac

    big_names = ["w_in", "w_ff1", "w_ff2", "w_out", "w_proj_ssm", "w_proj_conv", "w_glu"]
    big_w = dict(w_in=w_in[0], w_ff1=w_ff1[0], w_ff2=w_ff2[0], w_out=w_out[0],
                 w_proj_ssm=w_proj_ssm[0], w_proj_conv=w_proj_conv[0], w_glu=w_glu[0])
    big_axis = dict(w_in=1, w_ff1=1, w_ff2=0, w_out=0, w_proj_ssm=1, w_proj_conv=1, w_glu=0)
    axes = [big_axis[k] for k in big_names]
    shard_shapes = [big_w[k].shape for k in big_names]
    pos = jnp.stack([q, ac]).astype(jnp.int32)
    own_only = dict(zip(big_names, _cast_into_full([big_w[k] for k in big_names], axes, pos, "cast_weights")))
    Dcs = conv_w.shape[2]
    assert Bl + 3 <= SUBLANES
    c8 = jnp.zeros((SUBLANES, D), F32).at[:Bl].set(c).at[Bl:Bl + 3, :Dcs].set(conv_w[0])
    Ca = w_ada.shape[2]
    b_ada_sh = lax.dynamic_slice_in_dim(b_ada, q * Ca, Ca, axis=1)
    (cg, mg), (w_in_full,) = _adaln_prologue(
        c8, w_ada[0], b_ada_sh, _carry_allgather([own_only["w_in"]], [big_axis["w_in"]], [big_w["w_in"].shape]))
    full = {"w_in": w_in_full}
    c_all = cg[:, :Bl].reshape(N_DEV * Bl, D)
    cw8 = _pad_rows(cg[0::2, Bl:Bl + 3, :Dcs].transpose(1, 0, 2).reshape(3, Dc), SUBLANES)
    mod = lax.dynamic_slice_in_dim(mg[0::2], dev * SUBLANES, Bl, axis=1)
    mod = mod.transpose(1, 0, 2).reshape(Bl, N_CHIPS * Ca)
    sh1, sc1, g1, sh2, sc2, g2 = [mod[:, k * D:(k + 1) * D].reshape(Bl, 1, D) for k in range(6)]

    ldt_c = log_dt[0].reshape(G, 1)
    bt_r = b_re[0].transpose(2, 0, 1)
    bt_i = b_im[0].transpose(2, 0, 1)
    lbr, lbi, bbt_r, bbt_i = _s5_disc(lam_re[0], lam_im[0], ldt_c, bt_r, bt_i)
    lam_r_p = lbr.reshape(NP, LANES)
    lam_i_p = lbi.reshape(NP, LANES)
    Bm_r = _diag_blocks_from_groups(bbt_r.transpose(1, 0, 2), nb).astype(BF16)
    Bm_i = _diag_blocks_from_groups(bbt_i.transpose(1, 0, 2), nb).astype(BF16)
    Cm_r = _diag_blocks_from_groups(c_re[0], nb).astype(BF16)
    Cm_i = _diag_blocks_from_groups(-c_im[0], nb).astype(BF16)

    x2 = x.reshape(N, D)
    mixer_w = ["w_out", "w_proj_ssm", "w_proj_conv", "w_glu"]
    mlp_w = ["w_ff1", "w_ff2"]
    layout = lambda ks: ([big_axis[k] for k in ks], [big_w[k].shape for k in ks])
    gather = lambda ks: _carry_allgather([own_only[k] for k in ks], *layout(ks))
    p2, gathered = _inproj_fwd(x2, sh1, sc1, norm1_g, full["w_in"], gather(mixer_w))
    full.update(zip(mixer_w, gathered))
    p3 = p2.reshape(Bl, L, IN)
    (Sr, Si, Sb_r, Sb_i, ylin3), gathered = _s5_fwd(p3, Bm_r, Bm_i, Cm_r, Cm_i, lam_r_p, lam_i_p, d_skip, gather(mlp_w))
    full.update(zip(mlp_w, gathered))
    ylin2 = ylin3.reshape(N, Ds)
    mix_w = (full["w_glu"], b_glu, cw8, full["w_proj_ssm"], full["w_proj_conv"], full["w_out"])
    x1 = _mix_fwd(p2, ylin2, x2, g1, *mix_w)

    (dx1, h2b, dab, sqb, dfb, loss_p, g_fg, g_n2, dsh2, dsc2, dg2) = _mlp_fwd_bwd(
        x1, loss_target.reshape(N, D), sh2, sc2, g2, norm2_g, final_g.reshape(1, D), full["w_ff1"], full["w_ff2"])
    g_full = {"w_ff1": _grad_w(h2b, dab, "grad_w_ff1")[0], "w_ff2": _grad_w(sqb, dfb, "grad_w_ff2")[0]}

    exchange = lambda ks: _carry_sibling_exchange([g_full[k] for k in ks], *layout(ks))
    presum = lambda ks, theirs: _presum([g_full[k] for k in ks], list(theirs), *layout(ks), pos, "presum_" + ks[0])
    chip_sum = lambda ks, parts, recv: _sum_chips(list(parts), list(recv), pos, "sum_" + ks[0])

    (dyl2, dp2, gw_out, gw_ps, gw_pc, gw_glu, gb_glu, gcw8, dg1), theirs_mlp = _mix_bwd(
        p2, ylin2, dx1, g1, *mix_w, exchange(mlp_w))
    g_full.update(w_out=gw_out, w_proj_ssm=gw_ps, w_proj_conv=gw_pc, w_glu=gw_glu)
    parts_mlp = presum(mlp_w, theirs_mlp)
    (dp3, dBm_r, dBm_i, dCm_r, dCm_i, dlam_r_p, dlam_i_p, g_dsk), extra = _s5_bwd(
        dyl2.reshape(Bl, L, Ds), p3, dp2.reshape(Bl, L, IN), Sr, Si, Sb_r, Sb_i, Bm_r, Bm_i, Cm_r, Cm_i, lam_r_p, lam_i_p,
        d_skip, _carry_join(_carry_chip_scatter(parts_mlp), exchange(mixer_w)))
    recv_mlp, theirs_mix = extra[:len(mlp_w)], extra[len(mlp_w):]
    halves_mlp = chip_sum(mlp_w, parts_mlp, recv_mlp)
    parts_mix = presum(mixer_w, theirs_mix)
    dp_all = dp3.reshape(N, IN)
    (grad_x2, hb, g_n1, dsh1, dsc1), _ = _inproj_bwd(x2, dx1, dp_all, sh1, sc1, norm1_g, full["w_in"], _NO_EXCHANGE)

    dbbt_r = _groups_from_diag_blocks(dBm_r, G, H, P).transpose(1, 0, 2)
    dbbt_i = _groups_from_diag_blocks(dBm_i, G, H, P).transpose(1, 0, 2)
    dc_re = _groups_from_diag_blocks(dCm_r, G, H, P)
    dc_im = -_groups_from_diag_blocks(dCm_i, G, H, P)
    dmod = jnp.concatenate([dsh1, dsc1, dg1, dsh2, dsc2, dg2], axis=-1).reshape(Bl, 6 * D)
    small = [g_n1, g_n2, g_fg, g_dsk, gb_glu, gcw8[:3], dlam_r_p, dlam_i_p, dbbt_r, dbbt_i, dc_re, dc_im, loss_p]
    small_shapes = [v.shape for v in small]
    n_small = sum(int(v.size) for v in small)
    small_slots = _place_in_slot(_pack(small + [dmod]), jnp.reshape(dev, (1,)).astype(jnp.int32), "place_small")

    g_full["w_in"], extra = _grad_w(
        hb, dp_all, "grad_w_in",
        _carry_join(_carry_join(_carry_sibling_share(halves_mlp), _carry_chip_scatter(parts_mix)), _carry_allgather8(small_slots)))
    reduced = dict(zip(mlp_w, extra[:len(mlp_w)]))
    halves_mix = chip_sum(mixer_w, parts_mix, extra[len(mlp_w):len(mlp_w) + len(mixer_w)])
    gathered = extra[-1]
    theirs_in = _run_carried(exchange(["w_in"]), "rs_exchange_w_in")
    parts_in = presum(["w_in"], theirs_in)
    recv_in = _run_carried(_carry_chip_scatter(parts_in), "rs_scatter_w_in")
    halves_in = chip_sum(["w_in"], parts_in, recv_in)
    reduced.update(zip(mixer_w + ["w_in"], _run_carried(_carry_sibling_share(halves_mix + halves_in), "rs_share_rest")))

    red = _unpack(_sum_devices(gathered, "sum_small"), small_shapes)
    (r_n1, r_n2, r_fg, r_dsk, r_bglu, r_cw, r_dlr, r_dli, r_dbr, r_dbi, r_cre, r_cim, r_loss) = red
    dmod_all = gathered.reshape(N_DEV, -1)[:, n_small:n_small + Bl * 6 * D].reshape(N_DEV * Bl, 6 * D)
    gw_ada, gb_ada = _ada_bwd(c_all, lax.dynamic_slice_in_dim(dmod_all, q * Ca, Ca, axis=1), dmod_all)
    g_lr, g_li, g_ldt, g_bt_r, g_bt_i = _s5_disc_bwd(lam_re[0], lam_im[0], ldt_c, bt_r, bt_i,
                                                   r_dlr.reshape(G, P), r_dli.reshape(G, P), r_dbr, r_dbi)

    grads = dict(
        norm1_g=r_n1, norm2_g=r_n2, w_ada=gw_ada, b_ada=gb_ada, lam_re=g_lr, lam_im=g_li, log_dt=g_ldt.reshape(1, G),
        b_re=g_bt_r.transpose(1, 2, 0), b_im=g_bt_i.transpose(1, 2, 0), c_re=r_cre, c_im=r_cim, d_skip=r_dsk,
        b_glu=r_bglu, conv_w=lax.dynamic_slice_in_dim(r_cw, q * Dcs, Dcs, axis=1), final_g=r_fg, **reduced)
    weights = dict(norm1_g=norm1_g, norm2_g=norm2_g, w_ada=w_ada, b_ada=b_ada, w_in=w_in, lam_re=lam_re, lam_im=lam_im,
                   log_dt=log_dt, b_re=b_re, b_im=b_im, c_re=c_re, c_im=c_im, d_skip=d_skip, w_glu=w_glu, b_glu=b_glu,
                   conv_w=conv_w, w_proj_ssm=w_proj_ssm, w_proj_conv=w_proj_conv, w_out=w_out, w_ff1=w_ff1, w_ff2=w_ff2,
                   final_g=final_g)
    m_in = dict(norm1_g=m_norm1_g, norm2_g=m_norm2_g, w_ada=m_w_ada, b_ada=m_b_ada, w_in=m_w_in, lam_re=m_lam_re,
                lam_im=m_lam_im, log_dt=m_log_dt, b_re=m_b_re, b_im=m_b_im, c_re=m_c_re, c_im=m_c_im, d_skip=m_d_skip,
                w_glu=m_w_glu, b_glu=m_b_glu, conv_w=m_conv_w, w_proj_ssm=m_w_proj_ssm, w_proj_conv=m_w_proj_conv,
                w_out=m_w_out, w_ff1=m_w_ff1, w_ff2=m_w_ff2, final_g=m_final_g)
    v_in = dict(norm1_g=v_norm1_g, norm2_g=v_norm2_g, w_ada=v_w_ada, b_ada=v_b_ada, w_in=v_w_in, lam_re=v_lam_re,
                lam_im=v_lam_im, log_dt=v_log_dt, b_re=v_b_re, b_im=v_b_im, c_re=v_c_re, c_im=v_c_im, d_skip=v_d_skip,
                w_glu=v_w_glu, b_glu=v_b_glu, conv_w=v_conv_w, w_proj_ssm=v_w_proj_ssm, w_proj_conv=v_w_proj_conv,
                w_out=v_w_out, w_ff1=v_w_ff1, w_ff2=v_w_ff2, final_g=v_final_g)
    names = list(weights)
    grads = {k: grads[k].reshape(weights[k].shape) for k in names}

    big_upd = big_names + ["w_ada"]
    delta, new_m, new_v = {}, {}, {}
    flat2 = lambda a: a.reshape(-1, a.shape[-1])
    d_, m_, v_ = _adamw([flat2(weights[k]) for k in big_upd], [flat2(grads[k]) for k in big_upd],
                        [flat2(m_in[k]) for k in big_upd], [flat2(v_in[k]) for k in big_upd], "adamw_big")
    for k, dd, mm, vv in zip(big_upd, d_, m_, v_):
        shp = weights[k].shape
        delta[k], new_m[k], new_v[k] = dd.reshape(shp), mm.reshape(shp), vv.reshape(shp)
    small_upd = [k for k in names if k not in big_upd]
    d_, m_, v_ = _adamw_many([flat2(weights[k]) for k in small_upd], [flat2(grads[k]) for k in small_upd],
                             [flat2(m_in[k]) for k in small_upd], [flat2(v_in[k]) for k in small_upd])
    for k, dd, mm, vv in zip(small_upd, d_, m_, v_):
        shp = weights[k].shape
        delta[k], new_m[k], new_v[k] = dd.reshape(shp), mm.reshape(shp), vv.reshape(shp)

    loss = r_loss[0, 0]
    grad_x = grad_x2.reshape(Bl, L, D)
    return (loss, grad_x, *[grads[k] for k in names], *[delta[k] for k in names],
            *[new_m[k] for k in names], *[new_v[k] for k in names])
```

```python
import functools
from typing import Callable, NamedTuple

import jax
import jax.numpy as jnp
from jax import lax
from jax.experimental import pallas as pl
from jax.experimental.pallas import tpu as pltpu

F32 = jnp.float32
BF16 = jnp.bfloat16
MESH = pl.DeviceIdType.MESH
N_CHIPS = 4
N_DEV = 8
LANES = 128
SUBLANES = 8
V7X_VMEM_BYTES = 64 * 1024 * 1024
VMEM_LIMIT = V7X_VMEM_BYTES - 6 * 1024 * 1024
SSM_GROUP = 16
SSM_STATE = 64
S5_ROW_PAD = 4
HALO_ROWS = 16
RMS_EPS = 1e-6
ADAM_LR, ADAM_B1, ADAM_B2, ADAM_EPS, ADAM_WD, ADAM_STEP = 0.001, 0.9, 0.999, 1e-08, 0.01, 10

ANY = pl.BlockSpec(memory_space=pl.ANY)
VMEM_SPEC = pl.BlockSpec(memory_space=pltpu.VMEM)


def _cparams(**kw):
    return pltpu.CompilerParams(vmem_limit_bytes=VMEM_LIMIT, **kw)


def _dot(a, b):
    return jnp.dot(a, b, preferred_element_type=F32)


def _dot_nt(a, b):
    return lax.dot_general(a, b, (((1,), (1,)), ((), ())), preferred_element_type=F32)


def _dot_tn(a, b):
    return lax.dot_general(a, b, (((0,), (0,)), ((), ())), preferred_element_type=F32)


def _mesh_pos():
    return lax.axis_index("x"), lax.axis_index("y"), lax.axis_index("c")


def _allgather8(v, name, carried=None):
    r, c = v.shape

    def body(x_ref, out_ref, send_sems, recv_sems, local_sem):
        x, y, cc = _mesh_pos()
        me, sibling = (x, y, cc), (x, y, 1 - cc)
        chips = [(1 - x, y), (x, 1 - y), (1 - x, 1 - y)]

        def slot(px, py, pc):
            return out_ref.at[4 * px + 2 * py + pc]

        def copy(k, block, to, src=None):
            return pltpu.make_async_remote_copy(
                src_ref=slot(*block) if src is None else src, dst_ref=slot(*block),
                send_sem=send_sems.at[k], recv_sem=recv_sems.at[k], device_id=to, device_id_type=MESH)

        mine = pltpu.make_async_copy(x_ref, slot(*me), local_sem)
        mine.start()
        first = [copy(0, me, sibling, src=x_ref)]
        first += [copy(1 + j, me, (*chip, cc), src=x_ref) for j, chip in enumerate(chips)]
        for cp in first:
            cp.start()
        passed = [copy(4 + j, (*chip, cc), sibling) for j, chip in enumerate(chips)]
        for j, chip in enumerate(chips):
            copy(1 + j, (*chip, cc), me).wait_recv()
            passed[j].start()
        copy(0, sibling, me).wait_recv()
        for j, chip in enumerate(chips):
            copy(4 + j, (*chip, 1 - cc), me).wait_recv()
        for cp in first + passed:
            cp.wait_send()
        mine.wait()

    sems = [pltpu.SemaphoreType.DMA((7,)), pltpu.SemaphoreType.DMA((7,)), pltpu.SemaphoreType.DMA]
    out_shape = jax.ShapeDtypeStruct((N_DEV, r, c), v.dtype)
    if carried is None:
        return pl.pallas_call(body, out_shape=out_shape, in_specs=[VMEM_SPEC], out_specs=VMEM_SPEC,
                              scratch_shapes=sems, name=name)(v)
    (out,), extra = _call_carrying(body, carried, (v,), out_shape=[out_shape], in_specs=[VMEM_SPEC],
                                   out_specs=[VMEM_SPEC], scratch_shapes=sems, name=name)
    return out, extra


def _shard_region(ref, axis, shard_shape, q, half):
    R, C = shard_shape
    r0, nr = (0, R) if half is None else (half * (R // 2), R // 2)
    if axis == 1:
        return ref.at[pl.ds(r0, nr), pl.ds(q * C, C)]
    return ref.at[pl.ds(q * R + r0, nr), :]


class _Carried(NamedTuple):
    inputs: tuple
    out_shapes: tuple
    aliases: dict
    sems: tuple
    steps: Callable


def _carry_join(a, b):
    na_i, na_o, na_s = len(a.inputs), len(a.out_shapes), len(a.sems)

    def steps(ins, outs, sems):
        sa, fa = a.steps(ins[:na_i], outs[:na_o], sems[:na_s])
        sb, fb = b.steps(ins[na_i:], outs[na_o:], sems[na_s:])


        def start():
            sa()
            sb()

        def finish():
            fa()
            fb()

        return start, finish

    aliases = dict(a.aliases)
    aliases.update({na_i + i: na_o + o for i, o in b.aliases.items()})
    return _Carried(a.inputs + b.inputs, a.out_shapes + b.out_shapes, aliases, a.sems + b.sems, steps)


def _call_carrying(body, carried, args, *, out_shape, in_specs, out_specs, scratch_shapes=(), grid=None, aliases=None,
                   name, **kw):
    n_in, n_out, n_sc = len(in_specs), len(out_specs), len(scratch_shapes)
    n_ci, n_co = len(carried.inputs), len(carried.out_shapes)

    def wrapped(*refs):
        ins, refs = refs[:n_in], refs[n_in:]
        c_ins, refs = refs[:n_ci], refs[n_ci:]
        outs, refs = refs[:n_out], refs[n_out:]
        c_outs, refs = refs[:n_co], refs[n_co:]
        scratch, c_sems = refs[:n_sc], refs[n_sc:]
        fns = carried.steps(c_ins, c_outs, c_sems)
        start, finish = fns[0], fns[-1]
        relay = fns[1] if len(fns) == 3 else None
        if grid is None:
            start()
            body(*ins, *outs, *scratch)
            if relay is not None:
                relay()
            finish()
        else:
            ids = [pl.program_id(d) for d in range(len(grid))]
            first = functools.reduce(jnp.logical_and, [i == 0 for i in ids])
            last = functools.reduce(jnp.logical_and, [i == g - 1 for i, g in zip(ids, grid)])
            pl.when(first)(start)
            body(*ins, *outs, *scratch)
            if relay is not None:
                pl.when(ids[0] == (3 * grid[0]) // 4 if len(grid) == 1 else last)(relay)
            pl.when(last)(finish)

    if grid is not None:
        kw["grid"] = grid
    io_aliases = dict(aliases or {})
    io_aliases.update({n_in + i: n_out + o for i, o in carried.aliases.items()})
    res = pl.pallas_call(
        wrapped, out_shape=list(out_shape) + list(carried.out_shapes),
        in_specs=list(in_specs) + [ANY] * n_ci, out_specs=list(out_specs) + [ANY] * n_co,
        scratch_shapes=list(scratch_shapes) + list(carried.sems),
        input_output_aliases=io_aliases, name=name, **kw,
    )(*args, *carried.inputs)
    return res[:n_out], res[n_out:]


def _run_carried(carried, name):
    return _call_carrying(lambda: None, carried, (), out_shape=(), in_specs=(), out_specs=(), name=name)[1]


def _place_in_slot(v, dev_arr, name):
    r, c = v.shape

    def body(d_ref, v_ref, o_ref):
        o_ref[...] = v_ref[...]

    return pl.pallas_call(
        body, out_shape=jax.ShapeDtypeStruct((N_DEV, r, c), v.dtype),
        grid_spec=pltpu.PrefetchScalarGridSpec(
            num_scalar_prefetch=1, grid=(1,), in_specs=[pl.BlockSpec((r, c), lambda i, d: (0, 0))],
            out_specs=pl.BlockSpec((None, r, c), lambda i, d: (d[0], 0, 0))),
        name=name)(dev_arr, v)


def _carry_allgather8(buf):
    def steps(ins, outs, sems):
        send_s, recv_s = sems
        out = outs[0]
        x, y, cc = _mesh_pos()
        me, sibling = (x, y, cc), (x, y, 1 - cc)
        chips = [(1 - x, y), (x, 1 - y), (1 - x, 1 - y)]

        def copy(k, block, to):
            px, py, pc = block
            slot = out.at[4 * px + 2 * py + pc]
            return pltpu.make_async_remote_copy(src_ref=slot, dst_ref=slot, send_sem=send_s.at[k], recv_sem=recv_s.at[k],
                                                device_id=to, device_id_type=MESH)

        first = [copy(0, me, sibling)] + [copy(1 + j, me, (*chip, cc)) for j, chip in enumerate(chips)]
        passed = [copy(4 + j, (*chip, cc), sibling) for j, chip in enumerate(chips)]

        def start():
            for cp in first:
                cp.start()

        def finish():
            for j, chip in enumerate(chips):
                copy(1 + j, (*chip, cc), me).wait_recv()
                passed[j].start()
            copy(0, sibling, me).wait_recv()
            for j, chip in enumerate(chips):
                copy(4 + j, (*chip, 1 - cc), me).wait_recv()
            for cp in first + passed:
                cp.wait_send()

        return start, finish

    return _Carried((buf,), (jax.ShapeDtypeStruct(buf.shape, buf.dtype),), {0: 0}, (pltpu.SemaphoreType.DMA((7,)),) * 2, steps)


def _carry_allgather(fulls, axes, shapes):
    n = len(fulls)
    return _Carried(tuple(fulls), tuple(jax.ShapeDtypeStruct(f.shape, f.dtype) for f in fulls),
                    {i: i for i in range(n)}, (pltpu.SemaphoreType.DMA((3 * n,)),) * 4,
                    lambda ins, outs, sems: _allgather_weights_steps(outs, axes, shapes, *sems))


def _allgather_weights_steps(outs, axes, shapes, send_s, recv_s, fsend_s, frecv_s):
    n = len(outs)
    x, y, c = _mesh_pos()
    q = 2 * x + y
    sibling = (x, y, 1 - c)
    chips = [(1 - x, y), (x, 1 - y), (1 - x, 1 - y)]

    def region(i, qq, half):
        return _shard_region(outs[i], axes[i], shapes[i], qq, half)

    def remote(src, dst, ss, rs, to):
        return pltpu.make_async_remote_copy(src_ref=src, dst_ref=dst, send_sem=ss, recv_sem=rs,
                                            device_id=to, device_id_type=MESH)

    def ici(i, j, qq):
        cx, cy = chips[j]
        reg = region(i, qq, c)
        return remote(reg, reg, send_s.at[3 * i + j], recv_s.at[3 * i + j], (cx, cy, c))

    def d2d(i, j, half):
        cx, cy = chips[j]
        reg = region(i, 2 * cx + cy, half)
        return remote(reg, reg, fsend_s.at[3 * i + j], frecv_s.at[3 * i + j], sibling)

    def start():
        for i in range(n):
            for j in range(3):
                ici(i, j, q).start()

    def relay():
        for i in range(n):
            for j, (cx, cy) in enumerate(chips):
                ici(i, j, 2 * cx + cy).wait_recv()
                d2d(i, j, c).start()

    def finish():
        for i in range(n):
            for j in range(3):
                d2d(i, j, 1 - c).wait_recv()
        for i in range(n):
            for j in range(3):
                ici(i, j, q).wait_send()
                d2d(i, j, c).wait_send()

    return start, relay, finish


def _carry_sibling_exchange(grads, axes, shapes):
    n = len(grads)

    def steps(ins, theirs, sems):
        send_s, recv_s = sems
        x, y, c = _mesh_pos()

        def copies():
            return [pltpu.make_async_remote_copy(
                src_ref=_shard_region(ins[i], axes[i], shapes[i], qq, 1 - c), dst_ref=theirs[i].at[qq],
                send_sem=send_s.at[N_CHIPS * i + qq], recv_sem=recv_s.at[N_CHIPS * i + qq],
                device_id=(x, y, 1 - c), device_id_type=MESH) for i in range(n) for qq in range(N_CHIPS)]

        def start():
            for cp in copies():
                cp.start()

        def finish():
            for cp in copies():
                cp.wait()

        return start, finish

    stacked = tuple(jax.ShapeDtypeStruct((N_CHIPS, R // 2, C), F32) for (R, C) in shapes)
    return _Carried(tuple(grads), stacked, {}, (pltpu.SemaphoreType.DMA((N_CHIPS * n,)),) * 2, steps)


def _carry_chip_scatter(parts):
    n = len(parts)

    def steps(ins, outs, sems):
        send_s, recv_s = sems
        x, y, c = _mesh_pos()
        chips = [(1 - x, y), (x, 1 - y), (1 - x, 1 - y)]

        def copies():
            return [pltpu.make_async_remote_copy(
                src_ref=ins[i].at[2 * cx + cy], dst_ref=outs[i].at[j],
                send_sem=send_s.at[3 * i + j], recv_sem=recv_s.at[3 * i + j],
                device_id=(cx, cy, c), device_id_type=MESH) for i in range(n) for j, (cx, cy) in enumerate(chips)]

        def start():
            for cp in copies():
                cp.start()

        def finish():
            for cp in copies():
                cp.wait()

        return start, finish

    return _Carried(tuple(parts), tuple(jax.ShapeDtypeStruct((3,) + p.shape[1:], p.dtype) for p in parts), {},
                    (pltpu.SemaphoreType.DMA((3 * n,)),) * 2, steps)


def _carry_sibling_share(fulls):
    n = len(fulls)

    def steps(ins, outs, sems):
        send_s, recv_s = sems
        x, y, c = _mesh_pos()

        def copy(i, half):
            rh = fulls[i].shape[0] // 2
            rows = outs[i].at[pl.ds(half * rh, rh), :]
            return pltpu.make_async_remote_copy(src_ref=rows, dst_ref=rows, send_sem=send_s.at[i], recv_sem=recv_s.at[i],
                                                device_id=(x, y, 1 - c), device_id_type=MESH)

        def start():
            for i in range(n):
                copy(i, c).start()

        def finish():
            for i in range(n):
                copy(i, 1 - c).wait_recv()
                copy(i, c).wait_send()

        return start, finish

    return _Carried(tuple(fulls), tuple(jax.ShapeDtypeStruct(f.shape, f.dtype) for f in fulls),
                    {i: i for i in range(n)}, (pltpu.SemaphoreType.DMA((n,)),) * 2, steps)


def _row_block(rows, target=256):
    return target if rows % target == 0 else rows


BF16_TILE_ROWS = 16


def _common_steps(rows, most=8):
    ns = most
    while ns > 1 and any(r % (ns * BF16_TILE_ROWS) for r in rows):
        ns //= 2
    return ns


def _cast_into_full(ws, axes, pos, name):
    n = len(ws)
    ns = _common_steps([w.shape[0] for w in ws])
    in_specs, out_specs, out_shape = [], [], []
    for w, axis in zip(ws, axes):
        R, C = w.shape
        in_specs.append(pl.BlockSpec((R // ns, C), lambda i, s: (i, 0)))
        if axis == 1:
            out_shape.append(jax.ShapeDtypeStruct((R, N_CHIPS * C), BF16))
            out_specs.append(pl.BlockSpec((R // ns, C), lambda i, s: (i, s[0])))
        else:
            out_shape.append(jax.ShapeDtypeStruct((N_CHIPS * R, C), BF16))
            out_specs.append(pl.BlockSpec((R // ns, C), lambda i, s: (s[0] * ns + i, 0)))

    def body(s_ref, *refs):
        for k in range(n):
            refs[n + k][...] = refs[k][...].astype(BF16)

    return pl.pallas_call(
        body, out_shape=out_shape,
        grid_spec=pltpu.PrefetchScalarGridSpec(num_scalar_prefetch=1, grid=(ns,), in_specs=in_specs, out_specs=out_specs),
        compiler_params=_cparams(), name=name)(pos, *ws)


def _presum(gs, theirs, axes, shapes, pos, name):
    n = len(gs)
    ns = _common_steps([R // 2 for R, _ in shapes], most=2)
    in_specs, t_specs, out_shape = [], [], []
    for (R, C), axis in zip(shapes, axes):
        rb = R // 2 // ns
        if axis == 1:
            in_specs.append(pl.BlockSpec((rb, C), lambda k, i, s: (s[1] * ns + i, k)))
        else:
            in_specs.append(pl.BlockSpec((rb, C), lambda k, i, s: (k * 2 * ns + s[1] * ns + i, 0)))
        t_specs.append(pl.BlockSpec((None, rb, C), lambda k, i, s: (k, i, 0)))
        out_shape.append(jax.ShapeDtypeStruct((N_CHIPS, R // 2, C), BF16))

    def body(s_ref, *refs):
        for k in range(n):
            refs[2 * n + k][...] = (refs[k][...] + refs[n + k][...]).astype(BF16)

    return pl.pallas_call(
        body, out_shape=out_shape,
        grid_spec=pltpu.PrefetchScalarGridSpec(num_scalar_prefetch=1, grid=(N_CHIPS, ns), in_specs=in_specs + t_specs,
                                               out_specs=t_specs),
        compiler_params=_cparams(), name=name)(pos, *gs, *theirs)


def _sum_chips(owns, recvs, pos, name):
    n = len(owns)
    ns = _common_steps([o.shape[1] for o in owns], most=2)
    o_specs, r_specs, out_specs, out_shape = [], [], [], []
    for o in owns:
        _, Rh, C = o.shape
        rb = Rh // ns
        o_specs.append(pl.BlockSpec((None, rb, C), lambda i, s: (s[0], i, 0)))
        r_specs.append(pl.BlockSpec((3, rb, C), lambda i, s: (0, i, 0)))
        out_specs.append(pl.BlockSpec((rb, C), lambda i, s: (s[1] * ns + i, 0)))
        out_shape.append(jax.ShapeDtypeStruct((2 * Rh, C), F32))

    def body(s_ref, *refs):
        for k in range(n):
            acc = refs[k][...].astype(F32)
            for j in range(3):
                acc = acc + refs[n + k][j].astype(F32)
            refs[2 * n + k][...] = acc

    return pl.pallas_call(
        body, out_shape=out_shape,
        grid_spec=pltpu.PrefetchScalarGridSpec(num_scalar_prefetch=1, grid=(ns,), in_specs=o_specs + r_specs,
                                               out_specs=out_specs),
        compiler_params=_cparams(), name=name)(pos, *owns, *recvs)


def _sum_devices(parts, name):
    K, R, C = parts.shape

    def body(p_ref, o_ref):
        acc = p_ref[0]
        for k in range(1, K):
            acc = acc + p_ref[k]
        o_ref[...] = acc

    return pl.pallas_call(body, out_shape=jax.ShapeDtypeStruct((R, C), F32), name=name)(parts)


def _adamw_math(w, g, m, v):
    nm = ADAM_B1 * m + (1.0 - ADAM_B1) * g
    nv = ADAM_B2 * v + (1.0 - ADAM_B2) * (g * g)
    m_hat = nm / (1.0 - ADAM_B1 ** ADAM_STEP)
    v_hat = nv / (1.0 - ADAM_B2 ** ADAM_STEP)
    return -ADAM_LR * (m_hat / (jnp.sqrt(v_hat) + ADAM_EPS) + ADAM_WD * w), nm, nv


def _adamw_many(ws, gs, ms, vs):
    n = len(ws)

    def body(*refs):
        for k in range(n):
            w, g, m, v = (refs[j * n + k][...] for j in range(4))
            for j, val in enumerate(_adamw_math(w, g, m, v)):
                refs[(4 + j) * n + k][...] = val

    shapes = [jax.ShapeDtypeStruct(w.shape, F32) for w in ws]
    res = pl.pallas_call(body, out_shape=shapes * 3, compiler_params=_cparams(), name="adamw_small")(*ws, *gs, *ms, *vs)
    return res[:n], res[n:2 * n], res[2 * n:]


def _adamw(ws, gs, ms, vs, name):
    n = len(ws)
    ns = _common_steps([w.shape[0] for w in ws])
    specs = [pl.BlockSpec((w.shape[0] // ns, w.shape[1]), lambda i: (i, 0)) for w in ws]

    def body(*refs):
        for k in range(n):
            w, g, m, v = (refs[j * n + k][...] for j in range(4))
            for j, val in enumerate(_adamw_math(w, g, m, v)):
                refs[(4 + j) * n + k][...] = val

    shapes = [jax.ShapeDtypeStruct(w.shape, F32) for w in ws]
    res = pl.pallas_call(body, out_shape=shapes * 3, grid=(ns,), in_specs=specs * 4, out_specs=specs * 3,
                         compiler_params=_cparams(), name=name)(*ws, *gs, *ms, *vs)
    return res[:n], res[n:2 * n], res[2 * n:]


def _silu(v):
    return v * jax.nn.sigmoid(v)


def _ada_fwd(c_all, w_sh, b_sh):
    S, D = c_all.shape
    Ca = w_sh.shape[1]
    cb = 512 if Ca % 512 == 0 else Ca

    def body(c_ref, w_ref, b_ref, o_ref):
        act = _silu(c_ref[...]).astype(BF16)
        o_ref[...] = _dot(act, w_ref[...].astype(BF16)) + b_ref[...]

    return pl.pallas_call(
        body, out_shape=jax.ShapeDtypeStruct((S, Ca), F32), grid=(Ca // cb,),
        in_specs=[pl.BlockSpec((S, D), lambda j: (0, 0)), pl.BlockSpec((D, cb), lambda j: (0, j)),
                  pl.BlockSpec((1, cb), lambda j: (0, j))],
        out_specs=pl.BlockSpec((S, cb), lambda j: (0, j)), name="ada_fwd")(c_all, w_sh, b_sh)


def _ada_bwd(c_all, dmod_sh, dmod_all):
    S, D = c_all.shape
    Ca = dmod_sh.shape[1]
    C6 = dmod_all.shape[1]

    def body(c_ref, ds_ref, da_ref, gw_ref, gb_ref):
        act = _silu(c_ref[...]).astype(BF16)
        gw_ref[...] = _dot_tn(act, ds_ref[...].astype(BF16))
        gb_ref[...] = jnp.sum(da_ref[...], axis=0, keepdims=True)

    return pl.pallas_call(
        body, out_shape=[jax.ShapeDtypeStruct((D, Ca), F32), jax.ShapeDtypeStruct((1, C6), F32)],
        compiler_params=_cparams(), name="ada_bwd")(c_all, dmod_sh, dmod_all)


def _rms_fwd(xv):
    r = lax.rsqrt(jnp.mean(xv * xv, axis=-1, keepdims=True) + RMS_EPS)
    return xv * r, r


def _rms_bwd(dxh, xh, r):
    return r * (dxh - xh * jnp.mean(dxh * xh, axis=-1, keepdims=True))


def _const_spec(shape):
    nd = len(shape)
    return pl.BlockSpec(shape, lambda *_: (0,) * nd)


def _seq_spec(D, bps, rev_blocks=None):
    if rev_blocks is None:
        return pl.BlockSpec((None, 1, D), lambda i: (i // bps, 0, 0))
    return pl.BlockSpec((None, 1, D), lambda i: ((rev_blocks - 1 - i) // bps, 0, 0))


def _inproj_fwd(x2, sh1, sc1, n1g, w_in, carried):
    N, D = x2.shape
    IN = w_in.shape[1]
    Bl = sh1.shape[0]
    TB = _row_block(N // Bl, 512)
    bps = (N // Bl) // TB

    def body(x_ref, sh_ref, sc_ref, g_ref, w_ref, p_ref):
        xh, _ = _rms_fwd(x_ref[...])
        h = (xh * g_ref[...]) * (1.0 + sc_ref[...]) + sh_ref[...]
        p_ref[...] = _dot(h.astype(BF16), w_ref[...]).astype(BF16)

    (p2,), extra = _call_carrying(
        body, carried, (x2, sh1, sc1, n1g, w_in), out_shape=[jax.ShapeDtypeStruct((N, IN), BF16)], grid=(N // TB,),
        in_specs=[pl.BlockSpec((TB, D), lambda i: (i, 0)), _seq_spec(D, bps), _seq_spec(D, bps),
                  _const_spec((1, D)), _const_spec((D, IN))],
        out_specs=[pl.BlockSpec((TB, IN), lambda i: (i, 0))],
        compiler_params=_cparams(dimension_semantics=("arbitrary",)), name="inproj_fwd")
    return p2, extra


def _s5_dims(Bl, L, Ds):
    G = Ds // SSM_GROUP
    GP = G * SSM_STATE
    NP = GP // LANES
    T = min(64, L // 2)
    nb = 2 if (Ds // 2) % LANES == 0 else 1
    return G, GP, NP, T, nb


def _s5_disc_math(lr, li, ldt, bt_r, bt_i):
    dt = jnp.exp(ldt)
    er = jnp.exp(lr * dt)
    lbr = er * jnp.cos(li * dt)
    lbi = er * jnp.sin(li * dt)
    den = lr * lr + li * li
    fr = ((lbr - 1.0) * lr + lbi * li) / den
    fi = (lbi * lr - (lbr - 1.0) * li) / den
    return lbr, lbi, fr[None] * bt_r - fi[None] * bt_i, fr[None] * bt_i + fi[None] * bt_r


def _s5_disc(lr, li, ldt, bt_r, bt_i):
    def body(lr_ref, li_ref, ldt_ref, br_ref, bi_ref, o0, o1, o2, o3):
        res = _s5_disc_math(lr_ref[...], li_ref[...], ldt_ref[...], br_ref[...], bi_ref[...])
        for o, v in zip((o0, o1, o2, o3), res):
            o[...] = v

    S = jax.ShapeDtypeStruct
    return pl.pallas_call(body, out_shape=[S(lr.shape, F32)] * 2 + [S(bt_r.shape, F32)] * 2, name="s5_disc")(lr, li, ldt, bt_r, bt_i)


def _s5_disc_bwd(lr, li, ldt, bt_r, bt_i, dlbr, dlbi, dbbr, dbbi):
    def body(lr_ref, li_ref, ldt_ref, br_ref, bi_ref, g0, g1, g2, g3, o0, o1, o2, o3, o4):
        _, vjp = jax.vjp(_s5_disc_math, lr_ref[...], li_ref[...], ldt_ref[...], br_ref[...], bi_ref[...])
        res = vjp((g0[...], g1[...], g2[...], g3[...]))
        for o, v in zip((o0, o1, o2, o3, o4), res):
            o[...] = v

    S = jax.ShapeDtypeStruct
    return pl.pallas_call(body, out_shape=[S(lr.shape, F32)] * 2 + [S(ldt.shape, F32)] + [S(bt_r.shape, F32)] * 2,
                          name="s5_disc_bwd")(lr, li, ldt, bt_r, bt_i, dlbr, dlbi, dbbr, dbbi)


S5_SCAN_PANELS = 4
S5_SCAN_STEPS = 4


def _panel_scan(src_r, src_i, dst_r, dst_i, lr_ref, li_ref, car_r, car_i, NP, Bl, T, TP, adjoint):
    BT = Bl * TP
    PG = min(S5_SCAN_PANELS, NP)
    CH = S5_SCAN_STEPS
    for k0 in range(0, NP, PG):
        ks = list(range(k0, k0 + PG))
        lr = [jnp.broadcast_to(lr_ref[pl.ds(k, 1), :], (Bl, LANES)) for k in ks]
        li = [jnp.broadcast_to(li_ref[pl.ds(k, 1), :], (Bl, LANES)) for k in ks]

        def trip(cc, carry):
            t0 = (T // CH - 1 - cc) * CH if adjoint else cc * CH
            ts = [t0 + (CH - 1 - s if adjoint else s) for s in range(CH)]
            idx = [[pl.ds(k * BT + t, Bl, stride=TP) for t in ts] for k in ks]
            loaded = [[(src_r[ix, :], src_i[ix, :]) for ix in idx[j]] for j in range(PG)]
            results, new_carry = [], []
            for j in range(PG):
                ar, ai = carry[j]
                res = []
                for s in range(CH):
                    br, bi = loaded[j][s]
                    if adjoint:
                        ar, ai = br + lr[j] * ar + li[j] * ai, bi + lr[j] * ai - li[j] * ar
                    else:
                        ar, ai = lr[j] * ar - li[j] * ai + br, lr[j] * ai + li[j] * ar + bi
                    res.append((ar, ai))
                results.append(res)
                new_carry.append((ar, ai))
            for j in range(PG):
                for s in range(CH):
                    dst_r[idx[j][s], :] = results[j][s][0]
                    dst_i[idx[j][s], :] = results[j][s][1]
            return tuple(new_carry)

        init = tuple((car_r[pl.ds(k * SUBLANES, Bl), :], car_i[pl.ds(k * SUBLANES, Bl), :]) for k in ks)
        fin = lax.fori_loop(0, T // CH, trip, init, unroll=2)
        for j, k in enumerate(ks):
            car_r[pl.ds(k * SUBLANES, Bl), :] = fin[j][0]
            car_i[pl.ds(k * SUBLANES, Bl), :] = fin[j][1]


def _s5_fwd(p3, Bm_r, Bm_i, Cm_r, Cm_i, lam_r, lam_i, dsk, carried):
    Bl, L, _ = p3.shape
    Ds = dsk.shape[1]
    G, GP, NP, T, nb = _s5_dims(Bl, L, Ds)
    nT = L // T
    TP = T + S5_ROW_PAD
    BT = Bl * TP
    dsb, gpb, npb = Ds // nb, GP // nb, NP // nb

    def body(u_ref, br_ref, bi_ref, cr_ref, ci_ref, lr_ref, li_ref, dsk_ref, sr_ref, si_ref, sb_r, sb_i, y_ref,
             car_r, car_i, upad, ypad, bu_r, bu_i):
        i = pl.program_id(0)

        @pl.when(i == 0)
        def _():
            car_r[...] = jnp.zeros_like(car_r)
            car_i[...] = jnp.zeros_like(car_i)
            upad[...] = jnp.zeros_like(upad)

        zpad = jnp.zeros((S5_ROW_PAD, LANES), F32)
        for k in range(NP):
            for b in range(Bl):
                sr_ref[pl.ds(k * BT + b * TP + T, S5_ROW_PAD), :] = zpad
                si_ref[pl.ds(k * BT + b * TP + T, S5_ROW_PAD), :] = zpad
        for b in range(Bl):
            upad[pl.ds(b * TP, T), :] = u_ref[b].astype(F32)
        u = upad[...]
        ub = u.astype(BF16)
        for blk in range(nb):
            ubb = ub[:, blk * dsb:(blk + 1) * dsb]
            for bu_ref, b_ref in ((bu_r, br_ref), (bu_i, bi_ref)):
                res = _dot(ubb, b_ref[blk])
                for kk in range(npb):
                    k = blk * npb + kk
                    bu_ref[pl.ds(k * BT, BT), :] = res[:, kk * LANES:(kk + 1) * LANES]

        _panel_scan(bu_r, bu_i, sr_ref, si_ref, lr_ref, li_ref, car_r, car_i, NP, Bl, T, TP, adjoint=False)
        sb_r[...] = car_r[...]
        sb_i[...] = car_i[...]

        for blk in range(nb):
            s_r = jnp.concatenate([sr_ref[pl.ds((blk * npb + kk) * BT, BT), :] for kk in range(npb)], axis=1).astype(BF16)
            s_i = jnp.concatenate([si_ref[pl.ds((blk * npb + kk) * BT, BT), :] for kk in range(npb)], axis=1).astype(BF16)
            cols = slice(blk * dsb, (blk + 1) * dsb)
            ypad[:, cols] = _dot_nt(s_r, cr_ref[blk]) + _dot_nt(s_i, ci_ref[blk]) + dsk_ref[:, cols] * u[:, cols]
        for b in range(Bl):
            y_ref[b] = ypad[pl.ds(b * TP, T), :]

    S = jax.ShapeDtypeStruct
    state = S((nT, NP * BT, LANES), F32)
    bound = S((nT, NP * SUBLANES, LANES), F32)
    sspec = pl.BlockSpec((None, NP * BT, LANES), lambda i: (i, 0, 0))
    bspec = pl.BlockSpec((None, NP * SUBLANES, LANES), lambda i: (i, 0, 0))
    return _call_carrying(
        body, carried, (p3, Bm_r, Bm_i, Cm_r, Cm_i, lam_r, lam_i, dsk),
        out_shape=[state, state, bound, bound, S((Bl, L, Ds), F32)], grid=(nT,),
        in_specs=[pl.BlockSpec((Bl, T, Ds), lambda i: (0, i, 0)),
                  _const_spec((nb, dsb, gpb)), _const_spec((nb, dsb, gpb)),
                  _const_spec((nb, dsb, gpb)), _const_spec((nb, dsb, gpb)),
                  _const_spec((NP, LANES)), _const_spec((NP, LANES)), _const_spec((1, Ds))],
        out_specs=[sspec, sspec, bspec, bspec, pl.BlockSpec((Bl, T, Ds), lambda i: (0, i, 0))],
        scratch_shapes=[pltpu.VMEM((NP * SUBLANES, LANES), F32)] * 2 + [pltpu.VMEM((BT, Ds), F32)] * 2
        + [pltpu.VMEM((NP * BT, LANES), F32)] * 2,
        compiler_params=_cparams(dimension_semantics=("arbitrary",)), name="s5_fwd")


def _s5_bwd(dy3, p3, dp3, Sr, Si, Sb_r, Sb_i, Bm_r, Bm_i, Cm_r, Cm_i, lam_r, lam_i, dsk, carried):
    Bl, L, Ds = dy3.shape
    G, GP, NP, T, nb = _s5_dims(Bl, L, Ds)
    nT = L // T
    TP = T + S5_ROW_PAD
    BT = Bl * TP
    dsb, gpb, npb = Ds // nb, GP // nb, NP // nb

    def body(dy_ref, u_ref, dp_ref, sr_ref, si_ref, sbr_ref, sbi_ref, br_ref, bi_ref, cr_ref, ci_ref, lr_ref, li_ref, dsk_ref,
             du_ref, dbr_ref, dbi_ref, dcr_ref, dci_ref, dlr_ref, dli_ref, ddsk_ref,
             a_r, a_i, car_r, car_i, acc_r, acc_i, dypad, upad, dupad, q_r, q_i):
        i = pl.program_id(0)

        @pl.when(i == 0)
        def _():
            for ref in (car_r, car_i, acc_r, acc_i, dbr_ref, dbi_ref, dcr_ref, dci_ref, ddsk_ref, dypad, upad, a_r, a_i):
                ref[...] = jnp.zeros_like(ref)

        for b in range(Bl):
            dypad[pl.ds(b * TP, T), :] = dy_ref[b]
            upad[pl.ds(b * TP, T), :] = u_ref[b].astype(F32)
        dy = dypad[...]
        dyb = dy.astype(BF16)
        u = upad[...]
        ub = u.astype(BF16)
        for blk in range(nb):
            dyb_b = dyb[:, blk * dsb:(blk + 1) * dsb]
            for q_ref, c_ref in ((q_r, cr_ref), (q_i, ci_ref)):
                res = _dot(dyb_b, c_ref[blk])
                for kk in range(npb):
                    q_ref[pl.ds((blk * npb + kk) * BT, BT), :] = res[:, kk * LANES:(kk + 1) * LANES]

        _panel_scan(q_r, q_i, a_r, a_i, lr_ref, li_ref, car_r, car_i, NP, Bl, T, TP, adjoint=True)

        first_block = (i == nT - 1)
        for k in range(NP):
            rows = pl.ds(k * BT, BT)
            av_r, av_i = a_r[rows, :], a_i[rows, :]
            sp_r = pltpu.roll(sr_ref[rows, :], 1, 0)
            sp_i = pltpu.roll(si_ref[rows, :], 1, 0)
            acc = pl.ds(k * SUBLANES, SUBLANES)
            acc_r[acc, :] += jnp.sum((av_r * sp_r + av_i * sp_i).reshape(BT // SUBLANES, SUBLANES, LANES), axis=0)
            acc_i[acc, :] += jnp.sum((av_i * sp_r - av_r * sp_i).reshape(BT // SUBLANES, SUBLANES, LANES), axis=0)
            t0 = pl.ds(k * BT, Bl, stride=TP)
            a0_r, a0_i = a_r[t0, :], a_i[t0, :]
            brow = pl.ds(k * SUBLANES, Bl)
            sb_pr = jnp.where(first_block, 0.0, sbr_ref[brow, :])
            sb_pi = jnp.where(first_block, 0.0, sbi_ref[brow, :])
            acc_r[brow, :] += a0_r * sb_pr + a0_i * sb_pi
            acc_i[brow, :] += a0_i * sb_pr - a0_r * sb_pi

        ddsk_ref[...] += jnp.sum(dy * u, axis=0, keepdims=True)
        for blk in range(nb):
            cols = slice(blk * dsb, (blk + 1) * dsb)
            rows = [pl.ds((blk * npb + kk) * BT, BT) for kk in range(npb)]
            av_r = jnp.concatenate([a_r[r, :] for r in rows], axis=1).astype(BF16)
            av_i = jnp.concatenate([a_i[r, :] for r in rows], axis=1).astype(BF16)
            dupad[:, cols] = _dot_nt(av_r, br_ref[blk]) + _dot_nt(av_i, bi_ref[blk]) + dy[:, cols] * dsk_ref[:, cols]
            dbr_ref[blk] += _dot_tn(ub[:, cols], av_r)
            dbi_ref[blk] += _dot_tn(ub[:, cols], av_i)
            sv_r = jnp.concatenate([sr_ref[r, :] for r in rows], axis=1).astype(BF16)
            sv_i = jnp.concatenate([si_ref[r, :] for r in rows], axis=1).astype(BF16)
            dcr_ref[blk] += _dot_tn(dyb[:, cols], sv_r)
            dci_ref[blk] += _dot_tn(dyb[:, cols], sv_i)
        for b in range(Bl):
            du_ref[b] = dupad[pl.ds(b * TP, T), :].astype(BF16)

        @pl.when(i == nT - 1)
        def _():
            for k in range(NP):
                dlr_ref[pl.ds(k, 1), :] = jnp.sum(acc_r[pl.ds(k * SUBLANES, SUBLANES), :], axis=0, keepdims=True)
                dli_ref[pl.ds(k, 1), :] = jnp.sum(acc_i[pl.ds(k * SUBLANES, SUBLANES), :], axis=0, keepdims=True)

    S = jax.ShapeDtypeStruct
    rev = lambda i: nT - 1 - i
    sspec = pl.BlockSpec((None, NP * BT, LANES), lambda i: (rev(i), 0, 0))
    bspec = pl.BlockSpec((None, NP * SUBLANES, LANES), lambda i: (jnp.maximum(rev(i) - 1, 0), 0, 0))
    tspec = pl.BlockSpec((Bl, T, Ds), lambda i: (0, rev(i), 0))
    return _call_carrying(
        body, carried, (dy3, p3, dp3, Sr, Si, Sb_r, Sb_i, Bm_r, Bm_i, Cm_r, Cm_i, lam_r, lam_i, dsk),
        out_shape=[S(dp3.shape, dp3.dtype), S((nb, dsb, gpb), F32), S((nb, dsb, gpb), F32),
                   S((nb, dsb, gpb), F32), S((nb, dsb, gpb), F32), S((NP, LANES), F32), S((NP, LANES), F32), S((1, Ds), F32)],
        grid=(nT,),
        in_specs=[tspec, tspec, ANY, sspec, sspec, bspec, bspec,
                  _const_spec((nb, dsb, gpb)), _const_spec((nb, dsb, gpb)),
                  _const_spec((nb, dsb, gpb)), _const_spec((nb, dsb, gpb)),
                  _const_spec((NP, LANES)), _const_spec((NP, LANES)), _const_spec((1, Ds))],
        out_specs=[tspec, _const_spec((nb, dsb, gpb)), _const_spec((nb, dsb, gpb)),
                   _const_spec((nb, dsb, gpb)), _const_spec((nb, dsb, gpb)),
                   _const_spec((NP, LANES)), _const_spec((NP, LANES)), _const_spec((1, Ds))],
        aliases={2: 0},
        scratch_shapes=[pltpu.VMEM((NP * BT, LANES), F32)] * 2 + [pltpu.VMEM((NP * SUBLANES, LANES), F32)] * 4
        + [pltpu.VMEM((BT, Ds), F32)] * 3 + [pltpu.VMEM((NP * BT, LANES), F32)] * 2,
        compiler_params=_cparams(dimension_semantics=("arbitrary",)), name="s5_bwd")


def _mix_values(ylin, cb, cc, cx, gs, gc, halo_v, wglu, bglu, cw, wps, wpc, wout):
    yg, gelu_vjp = jax.vjp(jax.nn.gelu, ylin)
    sz = jax.nn.sigmoid(_dot(yg.astype(BF16), wglu) + bglu)
    ys = yg * sz
    v = cc * cx
    rows = lax.broadcasted_iota(jnp.int32, v.shape, 0)
    h6 = halo_v[HALO_ROWS - 2:HALO_ROWS - 1, :]
    h7 = halo_v[HALO_ROWS - 1:HALO_ROWS, :]
    v1 = jnp.where(rows == 0, h7, pltpu.roll(v, 1, 0))
    v2 = jnp.where(rows == 0, h6, jnp.where(rows == 1, h7, pltpu.roll(v, 2, 0)))
    cv = cw[0:1, :] * v2 + cw[1:2, :] * v1 + cw[2:3, :] * v
    yc = cb * cv
    ps = _dot(ys.astype(BF16), wps)
    pc = _dot(yc.astype(BF16), wpc)
    sgs = jax.nn.sigmoid(gs)
    sgc = jax.nn.sigmoid(gc)
    merged = sgs * ps + sgc * pc
    mo = _dot(merged.astype(BF16), wout)
    return dict(yg=yg, gelu_vjp=gelu_vjp, sz=sz, ys=ys, v=v, v1=v1, v2=v2, cv=cv, yc=yc, ps=ps, pc=pc,
                sgs=sgs, sgc=sgc, merged=merged, mo=mo)


def _mix_in_specs(TB, D, Ds, Dc, bps, blk):
    hb = TB // HALO_ROWS
    halo = lambda col: pl.BlockSpec((HALO_ROWS, Dc), lambda i: (jnp.maximum(blk(i) * hb - 1, 0), col))
    return [pl.BlockSpec((TB, Dc), lambda i: (blk(i), 1)), pl.BlockSpec((TB, Dc), lambda i: (blk(i), 2)),
            pl.BlockSpec((TB, Dc), lambda i: (blk(i), 3)), pl.BlockSpec((TB, D), lambda i: (blk(i), 2)),
            pl.BlockSpec((TB, D), lambda i: (blk(i), 3)), halo(2), halo(3),
            pl.BlockSpec((TB, Ds), lambda i: (blk(i), 0))]


def _mix_fwd(p2, ylin2, x2, g1, wglu, bglu, cw, wps, wpc, wout):
    N, D = x2.shape
    Ds = ylin2.shape[1]
    Dc = Ds
    Bl = g1.shape[0]
    TB = _row_block(N // Bl, 512)
    bps = (N // Bl) // TB

    def body(cb_ref, cc_ref, cx_ref, gs_ref, gc_ref, hcc_ref, hcx_ref, yl_ref, x_ref, g1_ref,
             wglu_ref, bglu_ref, cw_ref, wps_ref, wpc_ref, wout_ref, x1_ref):
        i = pl.program_id(0)
        f32 = lambda ref: ref[...].astype(F32)
        halo_v = jnp.where(i % bps == 0, 0.0, f32(hcc_ref) * f32(hcx_ref))
        f = _mix_values(yl_ref[...], f32(cb_ref), f32(cc_ref), f32(cx_ref), f32(gs_ref), f32(gc_ref), halo_v,
                        wglu_ref[...], bglu_ref[...], cw_ref[...], wps_ref[...], wpc_ref[...], wout_ref[...])
        x1_ref[...] = x_ref[...] + g1_ref[...] * f["mo"]

    return pl.pallas_call(
        body, out_shape=jax.ShapeDtypeStruct((N, D), F32), grid=(N // TB,),
        in_specs=_mix_in_specs(TB, D, Ds, Dc, bps, lambda i: i) + [
            pl.BlockSpec((TB, D), lambda i: (i, 0)), _seq_spec(D, bps),
            _const_spec((Ds, Ds)), _const_spec((1, Ds)), _const_spec((SUBLANES, Dc)),
            _const_spec((Ds, D)), _const_spec((Dc, D)), _const_spec((D, D))],
        out_specs=pl.BlockSpec((TB, D), lambda i: (i, 0)),
        compiler_params=_cparams(), name="mix_fwd",
    )(p2, p2, p2, p2, p2, p2, p2, ylin2, x2, g1, wglu, bglu, cw, wps, wpc, wout)


def _mix_bwd(p2, ylin2, dx1, g1, wglu, bglu, cw, wps, wpc, wout, carried):
    N, D = dx1.shape
    Ds = ylin2.shape[1]
    Dc = Ds
    IN = p2.shape[1]
    Bl = g1.shape[0]
    TB = _row_block(N // Bl)
    bps = (N // Bl) // TB
    nblk = N // TB
    rev = lambda i: nblk - 1 - i

    def body(cb_ref, cc_ref, cx_ref, gs_ref, gc_ref, hcc_ref, hcx_ref, yl_ref, dx1_ref, g1_ref,
             wglu_ref, bglu_ref, cw_ref, wps_ref, wpc_ref, wout_ref,
             dyl_ref, dp_ref, gwout_ref, gwps_ref, gwpc_ref, gwglu_ref, gbglu_ref, gcw_ref, dg1_ref, nxt):
        i = pl.program_id(0)
        blk = rev(i)

        @pl.when(i == 0)
        def _():
            for ref in (gwout_ref, gwps_ref, gwpc_ref, gwglu_ref, gbglu_ref, gcw_ref):
                ref[...] = jnp.zeros_like(ref)

        @pl.when(i % bps == 0)
        def _():
            nxt[...] = jnp.zeros_like(nxt)
            dg1_ref[...] = jnp.zeros_like(dg1_ref)

        f32 = lambda ref: ref[...].astype(F32)
        cb, cc, cx = f32(cb_ref), f32(cc_ref), f32(cx_ref)
        halo_v = jnp.where(blk % bps == 0, 0.0, f32(hcc_ref) * f32(hcx_ref))
        wglu, wps, wpc, wout, cw = wglu_ref[...], wps_ref[...], wpc_ref[...], wout_ref[...], cw_ref[...]
        f = _mix_values(yl_ref[...], cb, cc, cx, f32(gs_ref), f32(gc_ref), halo_v, wglu, bglu_ref[...], cw, wps, wpc, wout)

        dx1v = dx1_ref[...]
        dg1_ref[...] += jnp.sum(dx1v * f["mo"], axis=0, keepdims=True)
        dmo = (g1_ref[...] * dx1v).astype(BF16)
        gwout_ref[...] += _dot_tn(f["merged"].astype(BF16), dmo)
        dmerged = _dot_nt(dmo, wout)
        dps = dmerged * f["sgs"]
        dpc = dmerged * f["sgc"]
        dgs = dmerged * f["ps"] * f["sgs"] * (1.0 - f["sgs"])
        dgc = dmerged * f["pc"] * f["sgc"] * (1.0 - f["sgc"])
        dpsb, dpcb = dps.astype(BF16), dpc.astype(BF16)
        gwps_ref[...] += _dot_tn(f["ys"].astype(BF16), dpsb)
        gwpc_ref[...] += _dot_tn(f["yc"].astype(BF16), dpcb)
        dys = _dot_nt(dpsb, wps)
        dyc = _dot_nt(dpcb, wpc)

        dcb = dyc * f["cv"]
        dcv = dyc * cb
        rows = lax.broadcasted_iota(jnp.int32, dcv.shape, 0)
        n0, n1 = nxt[0:1, :], nxt[1:2, :]
        d1 = jnp.where(rows == TB - 1, n0, pltpu.roll(dcv, TB - 1, 0))
        d2 = jnp.where(rows == TB - 2, n0, jnp.where(rows == TB - 1, n1, pltpu.roll(dcv, TB - 2, 0)))
        dv = cw[2:3, :] * dcv + cw[1:2, :] * d1 + cw[0:1, :] * d2
        nxt[0:2, :] = dcv[0:2, :]
        gcw_ref[0:1, :] += jnp.sum(dcv * f["v2"], axis=0, keepdims=True)
        gcw_ref[1:2, :] += jnp.sum(dcv * f["v1"], axis=0, keepdims=True)
        gcw_ref[2:3, :] += jnp.sum(dcv * f["v"], axis=0, keepdims=True)

        dz = dys * f["yg"] * f["sz"] * (1.0 - f["sz"])
        dzb = dz.astype(BF16)
        gwglu_ref[...] += _dot_tn(f["yg"].astype(BF16), dzb)
        gbglu_ref[...] += jnp.sum(dz, axis=0, keepdims=True)
        dyg = dys * f["sz"] + _dot_nt(dzb, wglu)
        dyl_ref[...] = f["gelu_vjp"](dyg)[0]

        dp_ref[:, Ds:Ds + Dc] = dcb.astype(BF16)
        dp_ref[:, Ds + Dc:Ds + 2 * Dc] = (dv * cx).astype(BF16)
        dp_ref[:, Ds + 2 * Dc:Ds + 3 * Dc] = (dv * cc).astype(BF16)
        dp_ref[:, Ds + 3 * Dc:Ds + 3 * Dc + D] = dgs.astype(BF16)
        dp_ref[:, Ds + 3 * Dc + D:IN] = dgc.astype(BF16)

    S = jax.ShapeDtypeStruct
    return _call_carrying(
        body, carried, (p2, p2, p2, p2, p2, p2, p2, ylin2, dx1, g1, wglu, bglu, cw, wps, wpc, wout),
        out_shape=[S((N, Ds), F32), S((N, IN), BF16), S((D, D), F32), S((Ds, D), F32), S((Dc, D), F32),
                   S((Ds, Ds), F32), S((1, Ds), F32), S((SUBLANES, Dc), F32), S((Bl, 1, D), F32)],
        grid=(nblk,),
        in_specs=_mix_in_specs(TB, D, Ds, Dc, bps, rev) + [
            pl.BlockSpec((TB, D), lambda i: (rev(i), 0)), _seq_spec(D, bps, nblk),
            _const_spec((Ds, Ds)), _const_spec((1, Ds)), _const_spec((SUBLANES, Dc)),
            _const_spec((Ds, D)), _const_spec((Dc, D)), _const_spec((D, D))],
        out_specs=[pl.BlockSpec((TB, Ds), lambda i: (rev(i), 0)), pl.BlockSpec((TB, IN), lambda i: (rev(i), 0)),
                   _const_spec((D, D)), _const_spec((Ds, D)), _const_spec((Dc, D)), _const_spec((Ds, Ds)),
                   _const_spec((1, Ds)), _const_spec((SUBLANES, Dc)), _seq_spec(D, bps, nblk)],
        scratch_shapes=[pltpu.VMEM((SUBLANES, Dc), F32)],
        compiler_params=_cparams(dimension_semantics=("arbitrary",)), name="mix_bwd")


def _mlp_fwd_bwd(x1, tgt, sh2, sc2, g2, n2g, fg, w1, w2):
    N, D = x1.shape
    Dff = w1.shape[1]
    Bl = sh2.shape[0]
    TB = _row_block(N // Bl)
    bps = (N // Bl) // TB

    def body(x1_ref, t_ref, sh_ref, sc_ref, g2_ref, n2_ref, fg_ref, w1_ref, w2_ref,
             dx1_ref, h2_ref, da_ref, sq_ref, df_ref, loss_ref, gfg_ref, gn2_ref, dsh_ref, dsc_ref, dg2_ref):
        i = pl.program_id(0)

        @pl.when(i == 0)
        def _():
            for ref in (loss_ref, gfg_ref, gn2_ref):
                ref[...] = jnp.zeros_like(ref)

        @pl.when(i % bps == 0)
        def _():
            for ref in (dsh_ref, dsc_ref, dg2_ref):
                ref[...] = jnp.zeros_like(ref)

        x1v = x1_ref[...]
        sc, g2v, n2 = sc_ref[...], g2_ref[...], n2_ref[...]
        xh2, r2 = _rms_fwd(x1v)
        xn2 = xh2 * n2
        h2 = (xn2 * (1.0 + sc) + sh_ref[...]).astype(BF16)
        a = _dot(h2, w1_ref[...])
        ra = jnp.maximum(a, 0.0)
        sq = (ra * ra).astype(BF16)
        fv = _dot(sq, w2_ref[...])
        x2 = x1v + g2v * fv
        xh3, r3 = _rms_fwd(x2)
        err = xh3 * fg_ref[...] - t_ref[...]
        loss_ref[...] += 0.5 * jnp.sum(jnp.mean(err * err, axis=-1, keepdims=True), axis=0, keepdims=True)
        dy = err * (1.0 / D)
        gfg_ref[...] += jnp.sum(dy * xh3, axis=0, keepdims=True)
        dx2 = _rms_bwd(dy * fg_ref[...], xh3, r3)
        dg2_ref[...] += jnp.sum(dx2 * fv, axis=0, keepdims=True)
        df = (g2v * dx2).astype(BF16)
        dsq = _dot_nt(df, w2_ref[...])
        da = (2.0 * ra * dsq).astype(BF16)
        dh2 = _dot_nt(da, w1_ref[...])
        dsh_ref[...] += jnp.sum(dh2, axis=0, keepdims=True)
        dsc_ref[...] += jnp.sum(dh2 * xn2, axis=0, keepdims=True)
        dxn2 = dh2 * (1.0 + sc)
        gn2_ref[...] += jnp.sum(dxn2 * xh2, axis=0, keepdims=True)
        dx1_ref[...] = dx2 + _rms_bwd(dxn2 * n2, xh2, r2)
        h2_ref[...] = h2
        da_ref[...] = da
        sq_ref[...] = sq
        df_ref[...] = df

    S = jax.ShapeDtypeStruct
    row = lambda w: pl.BlockSpec((TB, w), lambda i: (i, 0))
    return pl.pallas_call(
        body,
        out_shape=[S((N, D), F32), S((N, D), BF16), S((N, Dff), BF16), S((N, Dff), BF16), S((N, D), BF16),
                   S((1, 1), F32), S((1, D), F32), S((1, D), F32), S((Bl, 1, D), F32), S((Bl, 1, D), F32), S((Bl, 1, D), F32)],
        grid=(N // TB,),
        in_specs=[row(D), row(D), _seq_spec(D, bps), _seq_spec(D, bps), _seq_spec(D, bps),
                  _const_spec((1, D)), _const_spec((1, D)), _const_spec((D, Dff)), _const_spec((Dff, D))],
        out_specs=[row(D), row(D), row(Dff), row(Dff), row(D), _const_spec((1, 1)), _const_spec((1, D)), _const_spec((1, D)),
                   _seq_spec(D, bps), _seq_spec(D, bps), _seq_spec(D, bps)],
        compiler_params=_cparams(dimension_semantics=("arbitrary",)), name="mlp_fwd_bwd",
    )(x1, tgt, sh2, sc2, g2, n2g, fg, w1, w2)


_NO_EXCHANGE = _Carried((), (), {}, (), lambda ins, outs, sems: ((lambda: None), (lambda: None)))


def _grad_w(a, b, name, carried=_NO_EXCHANGE):
    N, K1 = a.shape
    K2 = b.shape[1]
    t1 = 1024 if K1 % 1024 == 0 else K1
    t2 = 1024 if K2 % 1024 == 0 else K2
    tn = 2048 if N % 2048 == 0 else N

    def body(a_ref, b_ref, o_ref):
        @pl.when(pl.program_id(2) == 0)
        def _():
            o_ref[...] = jnp.zeros_like(o_ref)

        o_ref[...] += _dot_tn(a_ref[...], b_ref[...])

    (g,), extra = _call_carrying(
        body, carried, (a, b), out_shape=[jax.ShapeDtypeStruct((K1, K2), F32)], grid=(K1 // t1, K2 // t2, N // tn),
        in_specs=[pl.BlockSpec((tn, t1), lambda i, j, k: (k, i)), pl.BlockSpec((tn, t2), lambda i, j, k: (k, j))],
        out_specs=[pl.BlockSpec((t1, t2), lambda i, j, k: (i, j))],
        compiler_params=_cparams(dimension_semantics=("arbitrary", "arbitrary", "arbitrary")), name=name)
    return g, extra


def _inproj_bwd(x2, dx1, dp, sh1, sc1, n1g, w_in, carried):
    N, D = x2.shape
    IN = w_in.shape[1]
    Bl = sh1.shape[0]
    TB = _row_block(N // Bl, 512)
    bps = (N // Bl) // TB

    def body(x_ref, dx1_ref, dp_ref, sh_ref, sc_ref, g_ref, w_ref, gx_ref, h_ref, gn1_ref, dsh_ref, dsc_ref):
        i = pl.program_id(0)

        @pl.when(i == 0)
        def _():
            gn1_ref[...] = jnp.zeros_like(gn1_ref)

        @pl.when(i % bps == 0)
        def _():
            dsh_ref[...] = jnp.zeros_like(dsh_ref)
            dsc_ref[...] = jnp.zeros_like(dsc_ref)

        sc, n1 = sc_ref[...], g_ref[...]
        xh, r = _rms_fwd(x_ref[...])
        xn = xh * n1
        h_ref[...] = (xn * (1.0 + sc) + sh_ref[...]).astype(BF16)
        dh = _dot_nt(dp_ref[...], w_ref[...])
        dsh_ref[...] += jnp.sum(dh, axis=0, keepdims=True)
        dsc_ref[...] += jnp.sum(dh * xn, axis=0, keepdims=True)
        dxn = dh * (1.0 + sc)
        gn1_ref[...] += jnp.sum(dxn * xh, axis=0, keepdims=True)
        gx_ref[...] = dx1_ref[...] + _rms_bwd(dxn * n1, xh, r)

    S = jax.ShapeDtypeStruct
    row = lambda w: pl.BlockSpec((TB, w), lambda i: (i, 0))
    return _call_carrying(
        body, carried, (x2, dx1, dp, sh1, sc1, n1g, w_in),
        out_shape=[S((N, D), F32), S((N, D), BF16), S((1, D), F32), S((Bl, 1, D), F32), S((Bl, 1, D), F32)],
        grid=(N // TB,),
        in_specs=[row(D), row(D), row(IN), _seq_spec(D, bps), _seq_spec(D, bps), _const_spec((1, D)), _const_spec((D, IN))],
        out_specs=[row(D), row(D), _const_spec((1, D)), _seq_spec(D, bps), _seq_spec(D, bps)],
        compiler_params=_cparams(dimension_semantics=("arbitrary",)), name="inproj_bwd")


def _diag_mask(gb, a, b):
    rows = lax.broadcasted_iota(jnp.int32, (gb * a, gb * b), 0) // a
    cols = lax.broadcasted_iota(jnp.int32, (gb * a, gb * b), 1) // b
    return (rows == cols).astype(F32)


def _diag_blocks_from_groups(m, nb):
    G, a, b = m.shape
    gb = G // nb
    return jnp.tile(m.reshape(nb, gb * a, b), (1, 1, gb)) * _diag_mask(gb, a, b)[None]


def _groups_from_diag_blocks(d, G, a, b):
    nb = d.shape[0]
    gb = G // nb
    picked = (d * _diag_mask(gb, a, b)[None]).reshape(nb, gb * a, gb, b)
    return jnp.sum(picked, axis=2).reshape(G, a, b)


def _pad_rows(v, rows):
    return jnp.concatenate([v, jnp.zeros((rows - v.shape[0],) + v.shape[1:], v.dtype)], axis=0)


def _pack(vs):
    flat = jnp.concatenate([v.reshape(-1) for v in vs])
    n = flat.shape[0]
    tile = SUBLANES * LANES
    npad = -(-n // tile) * tile
    flat = jnp.concatenate([flat, jnp.zeros((npad - n,), flat.dtype)])
    return flat.reshape(npad // LANES, LANES)


def _unpack(packed, shapes):
    flat = packed.reshape(-1)
    out, off = [], 0
    for s in shapes:
        n = 1
        for d in s:
            n *= d
        out.append(flat[off:off + n].reshape(s))
        off += n
    return out


def kernel(x, c, norm1_g, norm2_g, w_ada, b_ada, w_in, lam_re, lam_im, log_dt, b_re, b_im, c_re, c_im, d_skip, w_glu, b_glu, conv_w, w_proj_ssm, w_proj_conv, w_out, w_ff1, w_ff2, final_g, loss_target, m_norm1_g, m_norm2_g, m_w_ada, m_b_ada, m_w_in, m_lam_re, m_lam_im, m_log_dt, m_b_re, m_b_im, m_c_re, m_c_im, m_d_skip, m_w_glu, m_b_glu, m_conv_w, m_w_proj_ssm, m_w_proj_conv, m_w_out, m_w_ff1, m_w_ff2, m_final_g, v_norm1_g, v_norm2_g, v_w_ada, v_b_ada, v_w_in, v_lam_re, v_lam_im, v_log_dt, v_b_re, v_b_im, v_c_re, v_c_im, v_d_skip, v_w_glu, v_b_glu, v_conv_w, v_w_proj_ssm, v_w_proj_conv, v_w_out, v_w_ff1, v_w_ff2, v_final_g):
    Bl, L, D = x.shape
    N = Bl * L
    Ds = Dc = D // 2
    G, H, P = Ds // SSM_GROUP, SSM_GROUP, SSM_STATE
    GP = G * P
    NP = GP // LANES
    nb = _s5_dims(Bl, L, Ds)[4]
    IN = Ds + 3 * Dc + 2 * D
    ax, ay, ac = _mesh_pos()
    q = 2 * ax + ay
    dev = 2 * q + ac

    big_names = ["w_in", "w_ff1", "w_ff2", "w_out", "w_proj_ssm", "w_proj_conv", "w_glu"]
    big_w = dict(w_in=w_in[0], w_ff1=w_ff1[0], w_ff2=w_ff2[0], w_out=w_out[0],
                 w_proj_ssm=w_proj_ssm[0], w_proj_conv=w_proj_conv[0], w_glu=w_glu[0])
    big_axis = dict(w_in=1, w_ff1=1, w_ff2=0, w_out=0, w_proj_ssm=1, w_proj_conv=1, w_glu=0)
    axes = [big_axis[k] for k in big_names]
    shard_shapes = [big_w[k].shape for k in big_names]
    pos = jnp.stack([q, ac]).astype(jnp.int32)
    own_only = dict(zip(big_names, _cast_into_full([big_w[k] for k in big_names], axes, pos, "cast_weights")))
    Dcs = conv_w.shape[2]
    first, (w_in_full,) = _allgather8(_pack([c, conv_w[0]]), "allgather_c_w_in",
                                      _carry_allgather([own_only["w_in"]], [big_axis["w_in"]], [big_w["w_in"].shape]))
    full = {"w_in": w_in_full}
    first = first.reshape(N_DEV, -1)
    c_all = first[:, :Bl * D].reshape(N_DEV * Bl, D)
    cw = first[0::2, Bl * D:Bl * D + 3 * Dcs].reshape(N_CHIPS, 3, Dcs).transpose(1, 0, 2).reshape(3, Dc)
    cw8 = _pad_rows(cw, SUBLANES)
    Ca = w_ada.shape[2]
    b_ada_sh = lax.dynamic_slice_in_dim(b_ada, q * Ca, Ca, axis=1)
    mod_part = _ada_fwd(c_all, w_ada[0], b_ada_sh)
    mod_g = _allgather8(mod_part, "allgather_mod")
    mod_all = mod_g[0::2].transpose(1, 0, 2).reshape(N_DEV * Bl, N_CHIPS * Ca)
    mod = lax.dynamic_slice_in_dim(mod_all, dev * Bl, Bl, axis=0)
    sh1, sc1, g1, sh2, sc2, g2 = [mod[:, k * D:(k + 1) * D].reshape(Bl, 1, D) for k in range(6)]

    ldt_c = log_dt[0].reshape(G, 1)
    bt_r = b_re[0].transpose(2, 0, 1)
    bt_i = b_im[0].transpose(2, 0, 1)
    lbr, lbi, bbt_r, bbt_i = _s5_disc(lam_re[0], lam_im[0], ldt_c, bt_r, bt_i)
    lam_r_p = lbr.reshape(NP, LANES)
    lam_i_p = lbi.reshape(NP, LANES)
    Bm_r = _diag_blocks_from_groups(bbt_r.transpose(1, 0, 2), nb).astype(BF16)
    Bm_i = _diag_blocks_from_groups(bbt_i.transpose(1, 0, 2), nb).astype(BF16)
    Cm_r = _diag_blocks_from_groups(c_re[0], nb).astype(BF16)
    Cm_i = _diag_blocks_from_groups(-c_im[0], nb).astype(BF16)

    x2 = x.reshape(N, D)
    mixer_w = ["w_out", "w_proj_ssm", "w_proj_conv", "w_glu"]
    mlp_w = ["w_ff1", "w_ff2"]
    layout = lambda ks: ([big_axis[k] for k in ks], [big_w[k].shape for k in ks])
    gather = lambda ks: _carry_allgather([own_only[k] for k in ks], *layout(ks))
    p2, gathered = _inproj_fwd(x2, sh1, sc1, norm1_g, full["w_in"], gather(mixer_w))
    full.update(zip(mixer_w, gathered))
    p3 = p2.reshape(Bl, L, IN)
    (Sr, Si, Sb_r, Sb_i, ylin3), gathered = _s5_fwd(p3, Bm_r, Bm_i, Cm_r, Cm_i, lam_r_p, lam_i_p, d_skip, gather(mlp_w))
    full.update(zip(mlp_w, gathered))
    ylin2 = ylin3.reshape(N, Ds)
    mix_w = (full["w_glu"], b_glu, cw8, full["w_proj_ssm"], full["w_proj_conv"], full["w_out"])
    x1 = _mix_fwd(p2, ylin2, x2, g1, *mix_w)

    (dx1, h2b, dab, sqb, dfb, loss_p, g_fg, g_n2, dsh2, dsc2, dg2) = _mlp_fwd_bwd(
        x1, loss_target.reshape(N, D), sh2, sc2, g2, norm2_g, final_g.reshape(1, D), full["w_ff1"], full["w_ff2"])
    g_full = {"w_ff1": _grad_w(h2b, dab, "grad_w_ff1")[0], "w_ff2": _grad_w(sqb, dfb, "grad_w_ff2")[0]}

    exchange = lambda ks: _carry_sibling_exchange([g_full[k] for k in ks], *layout(ks))
    presum = lambda ks, theirs: _presum([g_full[k] for k in ks], list(theirs), *layout(ks), pos, "presum_" + ks[0])
    chip_sum = lambda ks, parts, recv: _sum_chips(list(parts), list(recv), pos, "sum_" + ks[0])

    (dyl2, dp2, gw_out, gw_ps, gw_pc, gw_glu, gb_glu, gcw8, dg1), theirs_mlp = _mix_bwd(
        p2, ylin2, dx1, g1, *mix_w, exchange(mlp_w))
    g_full.update(w_out=gw_out, w_proj_ssm=gw_ps, w_proj_conv=gw_pc, w_glu=gw_glu)
    parts_mlp = presum(mlp_w, theirs_mlp)
    (dp3, dBm_r, dBm_i, dCm_r, dCm_i, dlam_r_p, dlam_i_p, g_dsk), extra = _s5_bwd(
        dyl2.reshape(Bl, L, Ds), p3, dp2.reshape(Bl, L, IN), Sr, Si, Sb_r, Sb_i, Bm_r, Bm_i, Cm_r, Cm_i, lam_r_p, lam_i_p,
        d_skip, _carry_join(_carry_chip_scatter(parts_mlp), exchange(mixer_w)))
    recv_mlp, theirs_mix = extra[:len(mlp_w)], extra[len(mlp_w):]
    halves_mlp = chip_sum(mlp_w, parts_mlp, recv_mlp)
    parts_mix = presum(mixer_w, theirs_mix)
    dp_all = dp3.reshape(N, IN)
    (grad_x2, hb, g_n1, dsh1, dsc1), _ = _inproj_bwd(x2, dx1, dp_all, sh1, sc1, norm1_g, full["w_in"], _NO_EXCHANGE)

    dbbt_r = _groups_from_diag_blocks(dBm_r, G, H, P).transpose(1, 0, 2)
    dbbt_i = _groups_from_diag_blocks(dBm_i, G, H, P).transpose(1, 0, 2)
    dc_re = _groups_from_diag_blocks(dCm_r, G, H, P)
    dc_im = -_groups_from_diag_blocks(dCm_i, G, H, P)
    dmod = jnp.concatenate([dsh1, dsc1, dg1, dsh2, dsc2, dg2], axis=-1).reshape(Bl, 6 * D)
    small = [g_n1, g_n2, g_fg, g_dsk, gb_glu, gcw8[:3], dlam_r_p, dlam_i_p, dbbt_r, dbbt_i, dc_re, dc_im]
    small_shapes = [v.shape for v in small]
    n_small = sum(int(v.size) for v in small)
    small_slots = _place_in_slot(_pack(small + [dmod]), jnp.reshape(dev, (1,)).astype(jnp.int32), "place_small")

    g_full["w_in"], extra = _grad_w(
        hb, dp_all, "grad_w_in",
        _carry_join(_carry_join(_carry_sibling_share(halves_mlp), _carry_chip_scatter(parts_mix)), _carry_allgather8(small_slots)))
    reduced = dict(zip(mlp_w, extra[:len(mlp_w)]))
    halves_mix = chip_sum(mixer_w, parts_mix, extra[len(mlp_w):len(mlp_w) + len(mixer_w)])
    gathered = extra[-1]
    theirs_in = _run_carried(exchange(["w_in"]), "rs_exchange_w_in")
    parts_in = presum(["w_in"], theirs_in)
    recv_in = _run_carried(_carry_chip_scatter(parts_in), "rs_scatter_w_in")
    halves_in = chip_sum(["w_in"], parts_in, recv_in)
    reduced.update(zip(mixer_w + ["w_in"], _run_carried(_carry_sibling_share(halves_mix + halves_in), "rs_share_rest")))

    red = _unpack(_sum_devices(gathered, "sum_small"), small_shapes)
    (r_n1, r_n2, r_fg, r_dsk, r_bglu, r_cw, r_dlr, r_dli, r_dbr, r_dbi, r_cre, r_cim) = red
    dmod_all = gathered.reshape(N_DEV, -1)[:, n_small:n_small + Bl * 6 * D].reshape(N_DEV * Bl, 6 * D)
    gw_ada, gb_ada = _ada_bwd(c_all, lax.dynamic_slice_in_dim(dmod_all, q * Ca, Ca, axis=1), dmod_all)
    g_lr, g_li, g_ldt, g_bt_r, g_bt_i = _s5_disc_bwd(lam_re[0], lam_im[0], ldt_c, bt_r, bt_i,
                                                   r_dlr.reshape(G, P), r_dli.reshape(G, P), r_dbr, r_dbi)

    grads = dict(
        norm1_g=r_n1, norm2_g=r_n2, w_ada=gw_ada, b_ada=gb_ada, lam_re=g_lr, lam_im=g_li, log_dt=g_ldt.reshape(1, G),
        b_re=g_bt_r.transpose(1, 2, 0), b_im=g_bt_i.transpose(1, 2, 0), c_re=r_cre, c_im=r_cim, d_skip=r_dsk,
        b_glu=r_bglu, conv_w=lax.dynamic_slice_in_dim(r_cw, q * Dcs, Dcs, axis=1), final_g=r_fg, **reduced)
    weights = dict(norm1_g=norm1_g, norm2_g=norm2_g, w_ada=w_ada, b_ada=b_ada, w_in=w_in, lam_re=lam_re, lam_im=lam_im,
                   log_dt=log_dt, b_re=b_re, b_im=b_im, c_re=c_re, c_im=c_im, d_skip=d_skip, w_glu=w_glu, b_glu=b_glu,
                   conv_w=conv_w, w_proj_ssm=w_proj_ssm, w_proj_conv=w_proj_conv, w_out=w_out, w_ff1=w_ff1, w_ff2=w_ff2,
                   final_g=final_g)
    m_in = dict(norm1_g=m_norm1_g, norm2_g=m_norm2_g, w_ada=m_w_ada, b_ada=m_b_ada, w_in=m_w_in, lam_re=m_lam_re,
                lam_im=m_lam_im, log_dt=m_log_dt, b_re=m_b_re, b_im=m_b_im, c_re=m_c_re, c_im=m_c_im, d_skip=m_d_skip,
                w_glu=m_w_glu, b_glu=m_b_glu, conv_w=m_conv_w, w_proj_ssm=m_w_proj_ssm, w_proj_conv=m_w_proj_conv,
                w_out=m_w_out, w_ff1=m_w_ff1, w_ff2=m_w_ff2, final_g=m_final_g)
    v_in = dict(norm1_g=v_norm1_g, norm2_g=v_norm2_g, w_ada=v_w_ada, b_ada=v_b_ada, w_in=v_w_in, lam_re=v_lam_re,
                lam_im=v_lam_im, log_dt=v_log_dt, b_re=v_b_re, b_im=v_b_im, c_re=v_c_re, c_im=v_c_im, d_skip=v_d_skip,
                w_glu=v_w_glu, b_glu=v_b_glu, conv_w=v_conv_w, w_proj_ssm=v_w_proj_ssm, w_proj_conv=v_w_proj_conv,
                w_out=v_w_out, w_ff1=v_w_ff1, w_ff2=v_w_ff2, final_g=v_final_g)
    names = list(weights)
    grads = {k: grads[k].reshape(weights[k].shape) for k in names}

    big_upd = big_names + ["w_ada"]
    delta, new_m, new_v = {}, {}, {}
    flat2 = lambda a: a.reshape(-1, a.shape[-1])
    d_, m_, v_ = _adamw([flat2(weights[k]) for k in big_upd], [flat2(grads[k]) for k in big_upd],
                        [flat2(m_in[k]) for k in big_upd], [flat2(v_in[k]) for k in big_upd], "adamw_big")
    for k, dd, mm, vv in zip(big_upd, d_, m_, v_):
        shp = weights[k].shape
        delta[k], new_m[k], new_v[k] = dd.reshape(shp), mm.reshape(shp), vv.reshape(shp)
    small_upd = [k for k in names if k not in big_upd]
    d_, m_, v_ = _adamw_many([flat2(weights[k]) for k in small_upd], [flat2(grads[k]) for k in small_upd],
                             [flat2(m_in[k]) for k in small_upd], [flat2(v_in[k]) for k in small_upd])
    for k, dd, mm, vv in zip(small_upd, d_, m_, v_):
        shp = weights[k].shape
        delta[k], new_m[k], new_v[k] = dd.reshape(shp), mm.reshape(shp), vv.reshape(shp)

    loss = lax.psum(loss_p[0, 0], ("x", "y", "c"))
    grad_x = grad_x2.reshape(Bl, L, D)
    return (loss, grad_x, *[grads[k] for k in names], *[delta[k] for k in names],
            *[new_m[k] for k in names], *[new_v[k] for k in names])
```

```python
import functools
from typing import Callable, NamedTuple

import jax
import jax.numpy as jnp
from jax import lax
from jax.experimental import pallas as pl
from jax.experimental.pallas import tpu as pltpu

F32 = jnp.float32
BF16 = jnp.bfloat16
MESH = pl.DeviceIdType.MESH
N_CHIPS = 4
N_DEV = 8
LANES = 128
SUBLANES = 8
V7X_VMEM_BYTES = 64 * 1024 * 1024
VMEM_LIMIT = V7X_VMEM_BYTES - 6 * 1024 * 1024
SSM_GROUP = 16
SSM_STATE = 64
S5_ROW_PAD = 4
HALO_ROWS = 16
RMS_EPS = 1e-6
ADAM_LR, ADAM_B1, ADAM_B2, ADAM_EPS, ADAM_WD, ADAM_STEP = 0.001, 0.9, 0.999, 1e-08, 0.01, 10

ANY = pl.BlockSpec(memory_space=pl.ANY)
VMEM_SPEC = pl.BlockSpec(memory_space=pltpu.VMEM)


def _cparams(**kw):
    return pltpu.CompilerParams(vmem_limit_bytes=VMEM_LIMIT, **kw)


def _dot(a, b):
    return jnp.dot(a, b, preferred_element_type=F32)


def _dot_nt(a, b):
    return lax.dot_general(a, b, (((1,), (1,)), ((), ())), preferred_element_type=F32)


def _dot_tn(a, b):
    return lax.dot_general(a, b, (((0,), (0,)), ((), ())), preferred_element_type=F32)


def _mesh_pos():
    return lax.axis_index("x"), lax.axis_index("y"), lax.axis_index("c")


def _allgather8(v, name, carried=None):
    r, c = v.shape

    def body(x_ref, out_ref, send_sems, recv_sems, local_sem):
        x, y, cc = _mesh_pos()
        me, sibling = (x, y, cc), (x, y, 1 - cc)
        chips = [(1 - x, y), (x, 1 - y), (1 - x, 1 - y)]

        def slot(px, py, pc):
            return out_ref.at[4 * px + 2 * py + pc]

        def copy(k, block, to, src=None):
            return pltpu.make_async_remote_copy(
                src_ref=slot(*block) if src is None else src, dst_ref=slot(*block),
                send_sem=send_sems.at[k], recv_sem=recv_sems.at[k], device_id=to, device_id_type=MESH)

        mine = pltpu.make_async_copy(x_ref, slot(*me), local_sem)
        mine.start()
        first = [copy(0, me, sibling, src=x_ref)]
        first += [copy(1 + j, me, (*chip, cc), src=x_ref) for j, chip in enumerate(chips)]
        for cp in first:
            cp.start()
        passed = [copy(4 + j, (*chip, cc), sibling) for j, chip in enumerate(chips)]
        for j, chip in enumerate(chips):
            copy(1 + j, (*chip, cc), me).wait_recv()
            passed[j].start()
        copy(0, sibling, me).wait_recv()
        for j, chip in enumerate(chips):
            copy(4 + j, (*chip, 1 - cc), me).wait_recv()
        for cp in first + passed:
            cp.wait_send()
        mine.wait()

    sems = [pltpu.SemaphoreType.DMA((7,)), pltpu.SemaphoreType.DMA((7,)), pltpu.SemaphoreType.DMA]
    out_shape = jax.ShapeDtypeStruct((N_DEV, r, c), v.dtype)
    if carried is None:
        return pl.pallas_call(body, out_shape=out_shape, in_specs=[VMEM_SPEC], out_specs=VMEM_SPEC,
                              scratch_shapes=sems, name=name)(v)
    (out,), extra = _call_carrying(body, carried, (v,), out_shape=[out_shape], in_specs=[VMEM_SPEC],
                                   out_specs=[VMEM_SPEC], scratch_shapes=sems, name=name)
    return out, extra


def _shard_region(ref, axis, shard_shape, q, half):
    R, C = shard_shape
    r0, nr = (0, R) if half is None else (half * (R // 2), R // 2)
    if axis == 1:
        return ref.at[pl.ds(r0, nr), pl.ds(q * C, C)]
    return ref.at[pl.ds(q * R + r0, nr), :]


class _Carried(NamedTuple):
    inputs: tuple
    out_shapes: tuple
    aliases: dict
    sems: tuple
    steps: Callable


def _carry_join(a, b):
    na_i, na_o, na_s = len(a.inputs), len(a.out_shapes), len(a.sems)

    def steps(ins, outs, sems):
        sa, fa = a.steps(ins[:na_i], outs[:na_o], sems[:na_s])
        sb, fb = b.steps(ins[na_i:], outs[na_o:], sems[na_s:])


        def start():
            sa()
            sb()

        def finish():
            fa()
            fb()

        return start, finish

    aliases = dict(a.aliases)
    aliases.update({na_i + i: na_o + o for i, o in b.aliases.items()})
    return _Carried(a.inputs + b.inputs, a.out_shapes + b.out_shapes, aliases, a.sems + b.sems, steps)


def _call_carrying(body, carried, args, *, out_shape, in_specs, out_specs, scratch_shapes=(), grid=None, aliases=None,
                   name, **kw):
    n_in, n_out, n_sc = len(in_specs), len(out_specs), len(scratch_shapes)
    n_ci, n_co = len(carried.inputs), len(carried.out_shapes)

    def wrapped(*refs):
        ins, refs = refs[:n_in], refs[n_in:]
        c_ins, refs = refs[:n_ci], refs[n_ci:]
        outs, refs = refs[:n_out], refs[n_out:]
        c_outs, refs = refs[:n_co], refs[n_co:]
        scratch, c_sems = refs[:n_sc], refs[n_sc:]
        fns = carried.steps(c_ins, c_outs, c_sems)
        start, finish = fns[0], fns[-1]
        relay = fns[1] if len(fns) == 3 else None
        if grid is None:
            start()
            body(*ins, *outs, *scratch)
            if relay is not None:
                relay()
            finish()
        else:
            ids = [pl.program_id(d) for d in range(len(grid))]
            first = functools.reduce(jnp.logical_and, [i == 0 for i in ids])
            last = functools.reduce(jnp.logical_and, [i == g - 1 for i, g in zip(ids, grid)])
            pl.when(first)(start)
            body(*ins, *outs, *scratch)
            if relay is not None:
                pl.when(ids[0] == (3 * grid[0]) // 4 if len(grid) == 1 else last)(relay)
            pl.when(last)(finish)

    if grid is not None:
        kw["grid"] = grid
    io_aliases = dict(aliases or {})
    io_aliases.update({n_in + i: n_out + o for i, o in carried.aliases.items()})
    res = pl.pallas_call(
        wrapped, out_shape=list(out_shape) + list(carried.out_shapes),
        in_specs=list(in_specs) + [ANY] * n_ci, out_specs=list(out_specs) + [ANY] * n_co,
        scratch_shapes=list(scratch_shapes) + list(carried.sems),
        input_output_aliases=io_aliases, name=name, **kw,
    )(*args, *carried.inputs)
    return res[:n_out], res[n_out:]


def _run_carried(carried, name):
    return _call_carrying(lambda: None, carried, (), out_shape=(), in_specs=(), out_specs=(), name=name)[1]


def _place_in_slot(v, dev_arr, name):
    r, c = v.shape

    def body(d_ref, v_ref, o_ref):
        o_ref[...] = v_ref[...]

    return pl.pallas_call(
        body, out_shape=jax.ShapeDtypeStruct((N_DEV, r, c), v.dtype),
        grid_spec=pltpu.PrefetchScalarGridSpec(
            num_scalar_prefetch=1, grid=(1,), in_specs=[pl.BlockSpec((r, c), lambda i, d: (0, 0))],
            out_specs=pl.BlockSpec((None, r, c), lambda i, d: (d[0], 0, 0))),
        name=name)(dev_arr, v)


def _carry_allgather8(buf):
    def steps(ins, outs, sems):
        send_s, recv_s = sems
        out = outs[0]
        x, y, cc = _mesh_pos()
        me, sibling = (x, y, cc), (x, y, 1 - cc)
        chips = [(1 - x, y), (x, 1 - y), (1 - x, 1 - y)]

        def copy(k, block, to):
            px, py, pc = block
            slot = out.at[4 * px + 2 * py + pc]
            return pltpu.make_async_remote_copy(src_ref=slot, dst_ref=slot, send_sem=send_s.at[k], recv_sem=recv_s.at[k],
                                                device_id=to, device_id_type=MESH)

        first = [copy(0, me, sibling)] + [copy(1 + j, me, (*chip, cc)) for j, chip in enumerate(chips)]
        passed = [copy(4 + j, (*chip, cc), sibling) for j, chip in enumerate(chips)]

        def start():
            for cp in first:
                cp.start()

        def finish():
            for j, chip in enumerate(chips):
                copy(1 + j, (*chip, cc), me).wait_recv()
                passed[j].start()
            copy(0, sibling, me).wait_recv()
            for j, chip in enumerate(chips):
                copy(4 + j, (*chip, 1 - cc), me).wait_recv()
            for cp in first + passed:
                cp.wait_send()

        return start, finish

    return _Carried((buf,), (jax.ShapeDtypeStruct(buf.shape, buf.dtype),), {0: 0}, (pltpu.SemaphoreType.DMA((7,)),) * 2, steps)


def _carry_allgather(fulls, axes, shapes):
    n = len(fulls)
    return _Carried(tuple(fulls), tuple(jax.ShapeDtypeStruct(f.shape, f.dtype) for f in fulls),
                    {i: i for i in range(n)}, (pltpu.SemaphoreType.DMA((3 * n,)),) * 4,
                    lambda ins, outs, sems: _allgather_weights_steps(outs, axes, shapes, *sems))


def _allgather_weights_steps(outs, axes, shapes, send_s, recv_s, fsend_s, frecv_s):
    n = len(outs)
    x, y, c = _mesh_pos()
    q = 2 * x + y
    sibling = (x, y, 1 - c)
    chips = [(1 - x, y), (x, 1 - y), (1 - x, 1 - y)]

    def region(i, qq, half):
        return _shard_region(outs[i], axes[i], shapes[i], qq, half)

    def remote(src, dst, ss, rs, to):
        return pltpu.make_async_remote_copy(src_ref=src, dst_ref=dst, send_sem=ss, recv_sem=rs,
                                            device_id=to, device_id_type=MESH)

    def ici(i, j, qq):
        cx, cy = chips[j]
        reg = region(i, qq, c)
        return remote(reg, reg, send_s.at[3 * i + j], recv_s.at[3 * i + j], (cx, cy, c))

    def d2d(i, j, half):
        cx, cy = chips[j]
        reg = region(i, 2 * cx + cy, half)
        return remote(reg, reg, fsend_s.at[3 * i + j], frecv_s.at[3 * i + j], sibling)

    def start():
        for i in range(n):
            for j in range(3):
                ici(i, j, q).start()

    def relay():
        for i in range(n):
            for j, (cx, cy) in enumerate(chips):
                ici(i, j, 2 * cx + cy).wait_recv()
                d2d(i, j, c).start()

    def finish():
        for i in range(n):
            for j in range(3):
                d2d(i, j, 1 - c).wait_recv()
        for i in range(n):
            for j in range(3):
                ici(i, j, q).wait_send()
                d2d(i, j, c).wait_send()

    return start, relay, finish


def _carry_sibling_exchange(grads, axes, shapes):
    n = len(grads)

    def steps(ins, theirs, sems):
        send_s, recv_s = sems
        x, y, c = _mesh_pos()

        def copies():
            return [pltpu.make_async_remote_copy(
                src_ref=_shard_region(ins[i], axes[i], shapes[i], qq, 1 - c), dst_ref=theirs[i].at[qq],
                send_sem=send_s.at[N_CHIPS * i + qq], recv_sem=recv_s.at[N_CHIPS * i + qq],
                device_id=(x, y, 1 - c), device_id_type=MESH) for i in range(n) for qq in range(N_CHIPS)]

        def start():
            for cp in copies():
                cp.start()

        def finish():
            for cp in copies():
                cp.wait()

        return start, finish

    stacked = tuple(jax.ShapeDtypeStruct((N_CHIPS, R // 2, C), F32) for (R, C) in shapes)
    return _Carried(tuple(grads), stacked, {}, (pltpu.SemaphoreType.DMA((N_CHIPS * n,)),) * 2, steps)


def _carry_chip_scatter(parts):
    n = len(parts)

    def steps(ins, outs, sems):
        send_s, recv_s = sems
        x, y, c = _mesh_pos()
        chips = [(1 - x, y), (x, 1 - y), (1 - x, 1 - y)]

        def copies():
            return [pltpu.make_async_remote_copy(
                src_ref=ins[i].at[2 * cx + cy], dst_ref=outs[i].at[j],
                send_sem=send_s.at[3 * i + j], recv_sem=recv_s.at[3 * i + j],
                device_id=(cx, cy, c), device_id_type=MESH) for i in range(n) for j, (cx, cy) in enumerate(chips)]

        def start():
            for cp in copies():
                cp.start()

        def finish():
            for cp in copies():
                cp.wait()

        return start, finish

    return _Carried(tuple(parts), tuple(jax.ShapeDtypeStruct((3,) + p.shape[1:], p.dtype) for p in parts), {},
                    (pltpu.SemaphoreType.DMA((3 * n,)),) * 2, steps)


def _carry_sibling_share(fulls):
    n = len(fulls)

    def steps(ins, outs, sems):
        send_s, recv_s = sems
        x, y, c = _mesh_pos()

        def copy(i, half):
            rh = fulls[i].shape[0] // 2
            rows = outs[i].at[pl.ds(half * rh, rh), :]
            return pltpu.make_async_remote_copy(src_ref=rows, dst_ref=rows, send_sem=send_s.at[i], recv_sem=recv_s.at[i],
                                                device_id=(x, y, 1 - c), device_id_type=MESH)

        def start():
            for i in range(n):
                copy(i, c).start()

        def finish():
            for i in range(n):
                copy(i, 1 - c).wait_recv()
                copy(i, c).wait_send()

        return start, finish

    return _Carried(tuple(fulls), tuple(jax.ShapeDtypeStruct(f.shape, f.dtype) for f in fulls),
                    {i: i for i in range(n)}, (pltpu.SemaphoreType.DMA((n,)),) * 2, steps)


def _row_block(rows, target=256):
    return target if rows % target == 0 else rows


BF16_TILE_ROWS = 16


def _common_steps(rows, most=8):
    ns = most
    while ns > 1 and any(r % (ns * BF16_TILE_ROWS) for r in rows):
        ns //= 2
    return ns


def _cast_into_full(ws, axes, pos, name):
    n = len(ws)
    ns = _common_steps([w.shape[0] for w in ws])
    in_specs, out_specs, out_shape = [], [], []
    for w, axis in zip(ws, axes):
        R, C = w.shape
        in_specs.append(pl.BlockSpec((R // ns, C), lambda i, s: (i, 0)))
        if axis == 1:
            out_shape.append(jax.ShapeDtypeStruct((R, N_CHIPS * C), BF16))
            out_specs.append(pl.BlockSpec((R // ns, C), lambda i, s: (i, s[0])))
        else:
            out_shape.append(jax.ShapeDtypeStruct((N_CHIPS * R, C), BF16))
            out_specs.append(pl.BlockSpec((R // ns, C), lambda i, s: (s[0] * ns + i, 0)))

    def body(s_ref, *refs):
        for k in range(n):
            refs[n + k][...] = refs[k][...].astype(BF16)

    return pl.pallas_call(
        body, out_shape=out_shape,
        grid_spec=pltpu.PrefetchScalarGridSpec(num_scalar_prefetch=1, grid=(ns,), in_specs=in_specs, out_specs=out_specs),
        compiler_params=_cparams(), name=name)(pos, *ws)


def _presum(gs, theirs, axes, shapes, pos, name):
    n = len(gs)
    ns = _common_steps([R // 2 for R, _ in shapes], most=2)
    in_specs, t_specs, out_shape = [], [], []
    for (R, C), axis in zip(shapes, axes):
        rb = R // 2 // ns
        if axis == 1:
            in_specs.append(pl.BlockSpec((rb, C), lambda k, i, s: (s[1] * ns + i, k)))
        else:
            in_specs.append(pl.BlockSpec((rb, C), lambda k, i, s: (k * 2 * ns + s[1] * ns + i, 0)))
        t_specs.append(pl.BlockSpec((None, rb, C), lambda k, i, s: (k, i, 0)))
        out_shape.append(jax.ShapeDtypeStruct((N_CHIPS, R // 2, C), BF16))

    def body(s_ref, *refs):
        for k in range(n):
            refs[2 * n + k][...] = (refs[k][...] + refs[n + k][...]).astype(BF16)

    return pl.pallas_call(
        body, out_shape=out_shape,
        grid_spec=pltpu.PrefetchScalarGridSpec(num_scalar_prefetch=1, grid=(N_CHIPS, ns), in_specs=in_specs + t_specs,
                                               out_specs=t_specs),
        compiler_params=_cparams(), name=name)(pos, *gs, *theirs)


def _sum_chips(owns, recvs, pos, name):
    n = len(owns)
    ns = _common_steps([o.shape[1] for o in owns], most=2)
    o_specs, r_specs, out_specs, out_shape = [], [], [], []
    for o in owns:
        _, Rh, C = o.shape
        rb = Rh // ns
        o_specs.append(pl.BlockSpec((None, rb, C), lambda i, s: (s[0], i, 0)))
        r_specs.append(pl.BlockSpec((3, rb, C), lambda i, s: (0, i, 0)))
        out_specs.append(pl.BlockSpec((rb, C), lambda i, s: (s[1] * ns + i, 0)))
        out_shape.append(jax.ShapeDtypeStruct((2 * Rh, C), F32))

    def body(s_ref, *refs):
        for k in range(n):
            acc = refs[k][...].astype(F32)
            for j in range(3):
                acc = acc + refs[n + k][j].astype(F32)
            refs[2 * n + k][...] = acc

    return pl.pallas_call(
        body, out_shape=out_shape,
        grid_spec=pltpu.PrefetchScalarGridSpec(num_scalar_prefetch=1, grid=(ns,), in_specs=o_specs + r_specs,
                                               out_specs=out_specs),
        compiler_params=_cparams(), name=name)(pos, *owns, *recvs)


def _sum_devices(parts, name):
    K, R, C = parts.shape

    def body(p_ref, o_ref):
        acc = p_ref[0]
        for k in range(1, K):
            acc = acc + p_ref[k]
        o_ref[...] = acc

    return pl.pallas_call(body, out_shape=jax.ShapeDtypeStruct((R, C), F32), name=name)(parts)


def _adamw_math(w, g, m, v):
    nm = ADAM_B1 * m + (1.0 - ADAM_B1) * g
    nv = ADAM_B2 * v + (1.0 - ADAM_B2) * (g * g)
    m_hat = nm / (1.0 - ADAM_B1 ** ADAM_STEP)
    v_hat = nv / (1.0 - ADAM_B2 ** ADAM_STEP)
    return -ADAM_LR * (m_hat / (jnp.sqrt(v_hat) + ADAM_EPS) + ADAM_WD * w), nm, nv


def _adamw_many(ws, gs, ms, vs):
    n = len(ws)

    def body(*refs):
        for k in range(n):
            w, g, m, v = (refs[j * n + k][...] for j in range(4))
            for j, val in enumerate(_adamw_math(w, g, m, v)):
                refs[(4 + j) * n + k][...] = val

    shapes = [jax.ShapeDtypeStruct(w.shape, F32) for w in ws]
    res = pl.pallas_call(body, out_shape=shapes * 3, compiler_params=_cparams(), name="adamw_small")(*ws, *gs, *ms, *vs)
    return res[:n], res[n:2 * n], res[2 * n:]


def _adamw(ws, gs, ms, vs, name):
    n = len(ws)
    ns = _common_steps([w.shape[0] for w in ws])
    specs = [pl.BlockSpec((w.shape[0] // ns, w.shape[1]), lambda i: (i, 0)) for w in ws]

    def body(*refs):
        for k in range(n):
            w, g, m, v = (refs[j * n + k][...] for j in range(4))
            for j, val in enumerate(_adamw_math(w, g, m, v) + (g,)):
                refs[(4 + j) * n + k][...] = val

    shapes = [jax.ShapeDtypeStruct(w.shape, F32) for w in ws]
    res = pl.pallas_call(body, out_shape=shapes * 4, grid=(ns,), in_specs=specs * 4, out_specs=specs * 4,
                         compiler_params=_cparams(), name=name)(*ws, *gs, *ms, *vs)
    return res[:n], res[n:2 * n], res[2 * n:3 * n], res[3 * n:]


def _silu(v):
    return v * jax.nn.sigmoid(v)


def _ada_fwd(c_all, w_sh, b_sh):
    S, D = c_all.shape
    Ca = w_sh.shape[1]
    cb = 512 if Ca % 512 == 0 else Ca

    def body(c_ref, w_ref, b_ref, o_ref):
        act = _silu(c_ref[...]).astype(BF16)
        o_ref[...] = _dot(act, w_ref[...].astype(BF16)) + b_ref[...]

    return pl.pallas_call(
        body, out_shape=jax.ShapeDtypeStruct((S, Ca), F32), grid=(Ca // cb,),
        in_specs=[pl.BlockSpec((S, D), lambda j: (0, 0)), pl.BlockSpec((D, cb), lambda j: (0, j)),
                  pl.BlockSpec((1, cb), lambda j: (0, j))],
        out_specs=pl.BlockSpec((S, cb), lambda j: (0, j)), name="ada_fwd")(c_all, w_sh, b_sh)


def _ada_bwd(c_all, dmod_sh, dmod_all):
    S, D = c_all.shape
    Ca = dmod_sh.shape[1]
    C6 = dmod_all.shape[1]

    def body(c_ref, ds_ref, da_ref, gw_ref, gb_ref):
        act = _silu(c_ref[...]).astype(BF16)
        gw_ref[...] = _dot_tn(act, ds_ref[...].astype(BF16))
        gb_ref[...] = jnp.sum(da_ref[...], axis=0, keepdims=True)

    return pl.pallas_call(
        body, out_shape=[jax.ShapeDtypeStruct((D, Ca), F32), jax.ShapeDtypeStruct((1, C6), F32)],
        compiler_params=_cparams(), name="ada_bwd")(c_all, dmod_sh, dmod_all)


def _rms_fwd(xv):
    r = lax.rsqrt(jnp.mean(xv * xv, axis=-1, keepdims=True) + RMS_EPS)
    return xv * r, r


def _rms_bwd(dxh, xh, r):
    return r * (dxh - xh * jnp.mean(dxh * xh, axis=-1, keepdims=True))


def _const_spec(shape):
    nd = len(shape)
    return pl.BlockSpec(shape, lambda *_: (0,) * nd)


def _seq_spec(D, bps, rev_blocks=None):
    if rev_blocks is None:
        return pl.BlockSpec((None, 1, D), lambda i: (i // bps, 0, 0))
    return pl.BlockSpec((None, 1, D), lambda i: ((rev_blocks - 1 - i) // bps, 0, 0))


def _inproj_fwd(x2, sh1, sc1, n1g, w_in, carried):
    N, D = x2.shape
    IN = w_in.shape[1]
    Bl = sh1.shape[0]
    TB = _row_block(N // Bl, 512)
    bps = (N // Bl) // TB

    def body(x_ref, sh_ref, sc_ref, g_ref, w_ref, p_ref):
        xh, _ = _rms_fwd(x_ref[...])
        h = (xh * g_ref[...]) * (1.0 + sc_ref[...]) + sh_ref[...]
        p_ref[...] = _dot(h.astype(BF16), w_ref[...]).astype(BF16)

    (p2,), extra = _call_carrying(
        body, carried, (x2, sh1, sc1, n1g, w_in), out_shape=[jax.ShapeDtypeStruct((N, IN), BF16)], grid=(N // TB,),
        in_specs=[pl.BlockSpec((TB, D), lambda i: (i, 0)), _seq_spec(D, bps), _seq_spec(D, bps),
                  _const_spec((1, D)), _const_spec((D, IN))],
        out_specs=[pl.BlockSpec((TB, IN), lambda i: (i, 0))],
        compiler_params=_cparams(dimension_semantics=("arbitrary",)), name="inproj_fwd")
    return p2, extra


def _s5_dims(Bl, L, Ds):
    G = Ds // SSM_GROUP
    GP = G * SSM_STATE
    NP = GP // LANES
    T = min(64, L // 2)
    nb = 2 if (Ds // 2) % LANES == 0 else 1
    return G, GP, NP, T, nb


def _s5_disc_math(lr, li, ldt, bt_r, bt_i):
    dt = jnp.exp(ldt)
    er = jnp.exp(lr * dt)
    lbr = er * jnp.cos(li * dt)
    lbi = er * jnp.sin(li * dt)
    den = lr * lr + li * li
    fr = ((lbr - 1.0) * lr + lbi * li) / den
    fi = (lbi * lr - (lbr - 1.0) * li) / den
    return lbr, lbi, fr[None] * bt_r - fi[None] * bt_i, fr[None] * bt_i + fi[None] * bt_r


def _s5_disc(lr, li, ldt, bt_r, bt_i):
    def body(lr_ref, li_ref, ldt_ref, br_ref, bi_ref, o0, o1, o2, o3):
        res = _s5_disc_math(lr_ref[...], li_ref[...], ldt_ref[...], br_ref[...], bi_ref[...])
        for o, v in zip((o0, o1, o2, o3), res):
            o[...] = v

    S = jax.ShapeDtypeStruct
    return pl.pallas_call(body, out_shape=[S(lr.shape, F32)] * 2 + [S(bt_r.shape, F32)] * 2, name="s5_disc")(lr, li, ldt, bt_r, bt_i)


def _s5_disc_bwd(lr, li, ldt, bt_r, bt_i, dlbr, dlbi, dbbr, dbbi):
    def body(lr_ref, li_ref, ldt_ref, br_ref, bi_ref, g0, g1, g2, g3, o0, o1, o2, o3, o4):
        _, vjp = jax.vjp(_s5_disc_math, lr_ref[...], li_ref[...], ldt_ref[...], br_ref[...], bi_ref[...])
        res = vjp((g0[...], g1[...], g2[...], g3[...]))
        for o, v in zip((o0, o1, o2, o3, o4), res):
            o[...] = v

    S = jax.ShapeDtypeStruct
    return pl.pallas_call(body, out_shape=[S(lr.shape, F32)] * 2 + [S(ldt.shape, F32)] + [S(bt_r.shape, F32)] * 2,
                          name="s5_disc_bwd")(lr, li, ldt, bt_r, bt_i, dlbr, dlbi, dbbr, dbbi)


S5_SCAN_PANELS = 4
S5_SCAN_STEPS = 4


def _panel_scan(src_r, src_i, dst_r, dst_i, lr_ref, li_ref, car_r, car_i, NP, Bl, T, TP, adjoint):
    BT = Bl * TP
    PG = min(S5_SCAN_PANELS, NP)
    CH = S5_SCAN_STEPS
    for k0 in range(0, NP, PG):
        ks = list(range(k0, k0 + PG))
        lr = [jnp.broadcast_to(lr_ref[pl.ds(k, 1), :], (Bl, LANES)) for k in ks]
        li = [jnp.broadcast_to(li_ref[pl.ds(k, 1), :], (Bl, LANES)) for k in ks]

        def trip(cc, carry):
            t0 = (T // CH - 1 - cc) * CH if adjoint else cc * CH
            ts = [t0 + (CH - 1 - s if adjoint else s) for s in range(CH)]
            idx = [[pl.ds(k * BT + t, Bl, stride=TP) for t in ts] for k in ks]
            loaded = [[(src_r[ix, :], src_i[ix, :]) for ix in idx[j]] for j in range(PG)]
            results, new_carry = [], []
            for j in range(PG):
                ar, ai = carry[j]
                res = []
                for s in range(CH):
                    br, bi = loaded[j][s]
                    if adjoint:
                        ar, ai = br + lr[j] * ar + li[j] * ai, bi + lr[j] * ai - li[j] * ar
                    else:
                        ar, ai = lr[j] * ar - li[j] * ai + br, lr[j] * ai + li[j] * ar + bi
                    res.append((ar, ai))
                results.append(res)
                new_carry.append((ar, ai))
            for j in range(PG):
                for s in range(CH):
                    dst_r[idx[j][s], :] = results[j][s][0]
                    dst_i[idx[j][s], :] = results[j][s][1]
            return tuple(new_carry)

        init = tuple((car_r[pl.ds(k * SUBLANES, Bl), :], car_i[pl.ds(k * SUBLANES, Bl), :]) for k in ks)
        fin = lax.fori_loop(0, T // CH, trip, init, unroll=2)
        for j, k in enumerate(ks):
            car_r[pl.ds(k * SUBLANES, Bl), :] = fin[j][0]
            car_i[pl.ds(k * SUBLANES, Bl), :] = fin[j][1]


def _s5_fwd(p3, Bm_r, Bm_i, Cm_r, Cm_i, lam_r, lam_i, dsk, carried):
    Bl, L, _ = p3.shape
    Ds = dsk.shape[1]
    G, GP, NP, T, nb = _s5_dims(Bl, L, Ds)
    nT = L // T
    TP = T + S5_ROW_PAD
    BT = Bl * TP
    dsb, gpb, npb = Ds // nb, GP // nb, NP // nb

    def body(u_ref, br_ref, bi_ref, cr_ref, ci_ref, lr_ref, li_ref, dsk_ref, sr_ref, si_ref, sb_r, sb_i, y_ref,
             car_r, car_i, upad, ypad, bu_r, bu_i):
        i = pl.program_id(0)

        @pl.when(i == 0)
        def _():
            car_r[...] = jnp.zeros_like(car_r)
            car_i[...] = jnp.zeros_like(car_i)
            upad[...] = jnp.zeros_like(upad)

        zpad = jnp.zeros((S5_ROW_PAD, LANES), F32)
        for k in range(NP):
            for b in range(Bl):
                sr_ref[pl.ds(k * BT + b * TP + T, S5_ROW_PAD), :] = zpad
                si_ref[pl.ds(k * BT + b * TP + T, S5_ROW_PAD), :] = zpad
        for b in range(Bl):
            upad[pl.ds(b * TP, T), :] = u_ref[b].astype(F32)
        u = upad[...]
        ub = u.astype(BF16)
        for blk in range(nb):
            ubb = ub[:, blk * dsb:(blk + 1) * dsb]
            for bu_ref, b_ref in ((bu_r, br_ref), (bu_i, bi_ref)):
                res = _dot(ubb, b_ref[blk])
                for kk in range(npb):
                    k = blk * npb + kk
                    bu_ref[pl.ds(k * BT, BT), :] = res[:, kk * LANES:(kk + 1) * LANES]

        _panel_scan(bu_r, bu_i, sr_ref, si_ref, lr_ref, li_ref, car_r, car_i, NP, Bl, T, TP, adjoint=False)
        sb_r[...] = car_r[...]
        sb_i[...] = car_i[...]

        for blk in range(nb):
            s_r = jnp.concatenate([sr_ref[pl.ds((blk * npb + kk) * BT, BT), :] for kk in range(npb)], axis=1).astype(BF16)
            s_i = jnp.concatenate([si_ref[pl.ds((blk * npb + kk) * BT, BT), :] for kk in range(npb)], axis=1).astype(BF16)
            cols = slice(blk * dsb, (blk + 1) * dsb)
            ypad[:, cols] = _dot_nt(s_r, cr_ref[blk]) + _dot_nt(s_i, ci_ref[blk]) + dsk_ref[:, cols] * u[:, cols]
        for b in range(Bl):
            y_ref[b] = ypad[pl.ds(b * TP, T), :]

    S = jax.ShapeDtypeStruct
    state = S((nT, NP * BT, LANES), F32)
    bound = S((nT, NP * SUBLANES, LANES), F32)
    sspec = pl.BlockSpec((None, NP * BT, LANES), lambda i: (i, 0, 0))
    bspec = pl.BlockSpec((None, NP * SUBLANES, LANES), lambda i: (i, 0, 0))
    return _call_carrying(
        body, carried, (p3, Bm_r, Bm_i, Cm_r, Cm_i, lam_r, lam_i, dsk),
        out_shape=[state, state, bound, bound, S((Bl, L, Ds), F32)], grid=(nT,),
        in_specs=[pl.BlockSpec((Bl, T, Ds), lambda i: (0, i, 0)),
                  _const_spec((nb, dsb, gpb)), _const_spec((nb, dsb, gpb)),
                  _const_spec((nb, dsb, gpb)), _const_spec((nb, dsb, gpb)),
                  _const_spec((NP, LANES)), _const_spec((NP, LANES)), _const_spec((1, Ds))],
        out_specs=[sspec, sspec, bspec, bspec, pl.BlockSpec((Bl, T, Ds), lambda i: (0, i, 0))],
        scratch_shapes=[pltpu.VMEM((NP * SUBLANES, LANES), F32)] * 2 + [pltpu.VMEM((BT, Ds), F32)] * 2
        + [pltpu.VMEM((NP * BT, LANES), F32)] * 2,
        compiler_params=_cparams(dimension_semantics=("arbitrary",)), name="s5_fwd")


def _s5_bwd(dy3, p3, dp3, Sr, Si, Sb_r, Sb_i, Bm_r, Bm_i, Cm_r, Cm_i, lam_r, lam_i, dsk, carried):
    Bl, L, Ds = dy3.shape
    G, GP, NP, T, nb = _s5_dims(Bl, L, Ds)
    nT = L // T
    TP = T + S5_ROW_PAD
    BT = Bl * TP
    dsb, gpb, npb = Ds // nb, GP // nb, NP // nb

    def body(dy_ref, u_ref, dp_ref, sr_ref, si_ref, sbr_ref, sbi_ref, br_ref, bi_ref, cr_ref, ci_ref, lr_ref, li_ref, dsk_ref,
             du_ref, dbr_ref, dbi_ref, dcr_ref, dci_ref, dlr_ref, dli_ref, ddsk_ref,
             a_r, a_i, car_r, car_i, acc_r, acc_i, dypad, upad, dupad, q_r, q_i):
        i = pl.program_id(0)

        @pl.when(i == 0)
        def _():
            for ref in (car_r, car_i, acc_r, acc_i, dbr_ref, dbi_ref, dcr_ref, dci_ref, ddsk_ref, dypad, upad, a_r, a_i):
                ref[...] = jnp.zeros_like(ref)

        for b in range(Bl):
            dypad[pl.ds(b * TP, T), :] = dy_ref[b]
            upad[pl.ds(b * TP, T), :] = u_ref[b].astype(F32)
        dy = dypad[...]
        dyb = dy.astype(BF16)
        u = upad[...]
        ub = u.astype(BF16)
        for blk in range(nb):
            dyb_b = dyb[:, blk * dsb:(blk + 1) * dsb]
            for q_ref, c_ref in ((q_r, cr_ref), (q_i, ci_ref)):
                res = _dot(dyb_b, c_ref[blk])
                for kk in range(npb):
                    q_ref[pl.ds((blk * npb + kk) * BT, BT), :] = res[:, kk * LANES:(kk + 1) * LANES]

        _panel_scan(q_r, q_i, a_r, a_i, lr_ref, li_ref, car_r, car_i, NP, Bl, T, TP, adjoint=True)

        first_block = (i == nT - 1)
        for k in range(NP):
            rows = pl.ds(k * BT, BT)
            av_r, av_i = a_r[rows, :], a_i[rows, :]
            sp_r = pltpu.roll(sr_ref[rows, :], 1, 0)
            sp_i = pltpu.roll(si_ref[rows, :], 1, 0)
            acc = pl.ds(k * SUBLANES, SUBLANES)
            acc_r[acc, :] += jnp.sum((av_r * sp_r + av_i * sp_i).reshape(BT // SUBLANES, SUBLANES, LANES), axis=0)
            acc_i[acc, :] += jnp.sum((av_i * sp_r - av_r * sp_i).reshape(BT // SUBLANES, SUBLANES, LANES), axis=0)
            t0 = pl.ds(k * BT, Bl, stride=TP)
            a0_r, a0_i = a_r[t0, :], a_i[t0, :]
            brow = pl.ds(k * SUBLANES, Bl)
            sb_pr = jnp.where(first_block, 0.0, sbr_ref[brow, :])
            sb_pi = jnp.where(first_block, 0.0, sbi_ref[brow, :])
            acc_r[brow, :] += a0_r * sb_pr + a0_i * sb_pi
            acc_i[brow, :] += a0_i * sb_pr - a0_r * sb_pi

        ddsk_ref[...] += jnp.sum(dy * u, axis=0, keepdims=True)
        for blk in range(nb):
            cols = slice(blk * dsb, (blk + 1) * dsb)
            rows = [pl.ds((blk * npb + kk) * BT, BT) for kk in range(npb)]
            av_r = jnp.concatenate([a_r[r, :] for r in rows], axis=1).astype(BF16)
            av_i = jnp.concatenate([a_i[r, :] for r in rows], axis=1).astype(BF16)
            dupad[:, cols] = _dot_nt(av_r, br_ref[blk]) + _dot_nt(av_i, bi_ref[blk]) + dy[:, cols] * dsk_ref[:, cols]
            dbr_ref[blk] += _dot_tn(ub[:, cols], av_r)
            dbi_ref[blk] += _dot_tn(ub[:, cols], av_i)
            sv_r = jnp.concatenate([sr_ref[r, :] for r in rows], axis=1).astype(BF16)
            sv_i = jnp.concatenate([si_ref[r, :] for r in rows], axis=1).astype(BF16)
            dcr_ref[blk] += _dot_tn(dyb[:, cols], sv_r)
            dci_ref[blk] += _dot_tn(dyb[:, cols], sv_i)
        for b in range(Bl):
            du_ref[b] = dupad[pl.ds(b * TP, T), :].astype(BF16)

        @pl.when(i == nT - 1)
        def _():
            for k in range(NP):
                dlr_ref[pl.ds(k, 1), :] = jnp.sum(acc_r[pl.ds(k * SUBLANES, SUBLANES), :], axis=0, keepdims=True)
                dli_ref[pl.ds(k, 1), :] = jnp.sum(acc_i[pl.ds(k * SUBLANES, SUBLANES), :], axis=0, keepdims=True)

    S = jax.ShapeDtypeStruct
    rev = lambda i: nT - 1 - i
    sspec = pl.BlockSpec((None, NP * BT, LANES), lambda i: (rev(i), 0, 0))
    bspec = pl.BlockSpec((None, NP * SUBLANES, LANES), lambda i: (jnp.maximum(rev(i) - 1, 0), 0, 0))
    tspec = pl.BlockSpec((Bl, T, Ds), lambda i: (0, rev(i), 0))
    return _call_carrying(
        body, carried, (dy3, p3, dp3, Sr, Si, Sb_r, Sb_i, Bm_r, Bm_i, Cm_r, Cm_i, lam_r, lam_i, dsk),
        out_shape=[S(dp3.shape, dp3.dtype), S((nb, dsb, gpb), F32), S((nb, dsb, gpb), F32),
                   S((nb, dsb, gpb), F32), S((nb, dsb, gpb), F32), S((NP, LANES), F32), S((NP, LANES), F32), S((1, Ds), F32)],
        grid=(nT,),
        in_specs=[tspec, tspec, ANY, sspec, sspec, bspec, bspec,
                  _const_spec((nb, dsb, gpb)), _const_spec((nb, dsb, gpb)),
                  _const_spec((nb, dsb, gpb)), _const_spec((nb, dsb, gpb)),
                  _const_spec((NP, LANES)), _const_spec((NP, LANES)), _const_spec((1, Ds))],
        out_specs=[tspec, _const_spec((nb, dsb, gpb)), _const_spec((nb, dsb, gpb)),
                   _const_spec((nb, dsb, gpb)), _const_spec((nb, dsb, gpb)),
                   _const_spec((NP, LANES)), _const_spec((NP, LANES)), _const_spec((1, Ds))],
        aliases={2: 0},
        scratch_shapes=[pltpu.VMEM((NP * BT, LANES), F32)] * 2 + [pltpu.VMEM((NP * SUBLANES, LANES), F32)] * 4
        + [pltpu.VMEM((BT, Ds), F32)] * 3 + [pltpu.VMEM((NP * BT, LANES), F32)] * 2,
        compiler_params=_cparams(dimension_semantics=("arbitrary",)), name="s5_bwd")


def _mix_values(ylin, cb, cc, cx, gs, gc, halo_v, wglu, bglu, cw, wps, wpc, wout):
    yg, gelu_vjp = jax.vjp(jax.nn.gelu, ylin)
    sz = jax.nn.sigmoid(_dot(yg.astype(BF16), wglu) + bglu)
    ys = yg * sz
    v = cc * cx
    rows = lax.broadcasted_iota(jnp.int32, v.shape, 0)
    h6 = halo_v[HALO_ROWS - 2:HALO_ROWS - 1, :]
    h7 = halo_v[HALO_ROWS - 1:HALO_ROWS, :]
    v1 = jnp.where(rows == 0, h7, pltpu.roll(v, 1, 0))
    v2 = jnp.where(rows == 0, h6, jnp.where(rows == 1, h7, pltpu.roll(v, 2, 0)))
    cv = cw[0:1, :] * v2 + cw[1:2, :] * v1 + cw[2:3, :] * v
    yc = cb * cv
    ps = _dot(ys.astype(BF16), wps)
    pc = _dot(yc.astype(BF16), wpc)
    sgs = jax.nn.sigmoid(gs)
    sgc = jax.nn.sigmoid(gc)
    merged = sgs * ps + sgc * pc
    mo = _dot(merged.astype(BF16), wout)
    return dict(yg=yg, gelu_vjp=gelu_vjp, sz=sz, ys=ys, v=v, v1=v1, v2=v2, cv=cv, yc=yc, ps=ps, pc=pc,
                sgs=sgs, sgc=sgc, merged=merged, mo=mo)


def _mix_in_specs(TB, D, Ds, Dc, bps, blk):
    hb = TB // HALO_ROWS
    halo = lambda col: pl.BlockSpec((HALO_ROWS, Dc), lambda i: (jnp.maximum(blk(i) * hb - 1, 0), col))
    return [pl.BlockSpec((TB, Dc), lambda i: (blk(i), 1)), pl.BlockSpec((TB, Dc), lambda i: (blk(i), 2)),
            pl.BlockSpec((TB, Dc), lambda i: (blk(i), 3)), pl.BlockSpec((TB, D), lambda i: (blk(i), 2)),
            pl.BlockSpec((TB, D), lambda i: (blk(i), 3)), halo(2), halo(3),
            pl.BlockSpec((TB, Ds), lambda i: (blk(i), 0))]


def _mix_fwd(p2, ylin2, x2, g1, wglu, bglu, cw, wps, wpc, wout):
    N, D = x2.shape
    Ds = ylin2.shape[1]
    Dc = Ds
    Bl = g1.shape[0]
    TB = _row_block(N // Bl, 512)
    bps = (N // Bl) // TB

    def body(cb_ref, cc_ref, cx_ref, gs_ref, gc_ref, hcc_ref, hcx_ref, yl_ref, x_ref, g1_ref,
             wglu_ref, bglu_ref, cw_ref, wps_ref, wpc_ref, wout_ref, x1_ref):
        i = pl.program_id(0)
        f32 = lambda ref: ref[...].astype(F32)
        halo_v = jnp.where(i % bps == 0, 0.0, f32(hcc_ref) * f32(hcx_ref))
        f = _mix_values(yl_ref[...], f32(cb_ref), f32(cc_ref), f32(cx_ref), f32(gs_ref), f32(gc_ref), halo_v,
                        wglu_ref[...], bglu_ref[...], cw_ref[...], wps_ref[...], wpc_ref[...], wout_ref[...])
        x1_ref[...] = x_ref[...] + g1_ref[...] * f["mo"]

    return pl.pallas_call(
        body, out_shape=jax.ShapeDtypeStruct((N, D), F32), grid=(N // TB,),
        in_specs=_mix_in_specs(TB, D, Ds, Dc, bps, lambda i: i) + [
            pl.BlockSpec((TB, D), lambda i: (i, 0)), _seq_spec(D, bps),
            _const_spec((Ds, Ds)), _const_spec((1, Ds)), _const_spec((SUBLANES, Dc)),
            _const_spec((Ds, D)), _const_spec((Dc, D)), _const_spec((D, D))],
        out_specs=pl.BlockSpec((TB, D), lambda i: (i, 0)),
        compiler_params=_cparams(), name="mix_fwd",
    )(p2, p2, p2, p2, p2, p2, p2, ylin2, x2, g1, wglu, bglu, cw, wps, wpc, wout)


def _mix_bwd(p2, ylin2, dx1, g1, wglu, bglu, cw, wps, wpc, wout, carried):
    N, D = dx1.shape
    Ds = ylin2.shape[1]
    Dc = Ds
    IN = p2.shape[1]
    Bl = g1.shape[0]
    TB = _row_block(N // Bl)
    bps = (N // Bl) // TB
    nblk = N // TB
    rev = lambda i: nblk - 1 - i

    def body(cb_ref, cc_ref, cx_ref, gs_ref, gc_ref, hcc_ref, hcx_ref, yl_ref, dx1_ref, g1_ref,
             wglu_ref, bglu_ref, cw_ref, wps_ref, wpc_ref, wout_ref,
             dyl_ref, dp_ref, gwout_ref, gwps_ref, gwpc_ref, gwglu_ref, gbglu_ref, gcw_ref, dg1_ref, nxt):
        i = pl.program_id(0)
        blk = rev(i)

        @pl.when(i == 0)
        def _():
            for ref in (gwout_ref, gwps_ref, gwpc_ref, gwglu_ref, gbglu_ref, gcw_ref):
                ref[...] = jnp.zeros_like(ref)

        @pl.when(i % bps == 0)
        def _():
            nxt[...] = jnp.zeros_like(nxt)
            dg1_ref[...] = jnp.zeros_like(dg1_ref)

        f32 = lambda ref: ref[...].astype(F32)
        cb, cc, cx = f32(cb_ref), f32(cc_ref), f32(cx_ref)
        halo_v = jnp.where(blk % bps == 0, 0.0, f32(hcc_ref) * f32(hcx_ref))
        wglu, wps, wpc, wout, cw = wglu_ref[...], wps_ref[...], wpc_ref[...], wout_ref[...], cw_ref[...]
        f = _mix_values(yl_ref[...], cb, cc, cx, f32(gs_ref), f32(gc_ref), halo_v, wglu, bglu_ref[...], cw, wps, wpc, wout)

        dx1v = dx1_ref[...]
        dg1_ref[...] += jnp.sum(dx1v * f["mo"], axis=0, keepdims=True)
        dmo = (g1_ref[...] * dx1v).astype(BF16)
        gwout_ref[...] += _dot_tn(f["merged"].astype(BF16), dmo)
        dmerged = _dot_nt(dmo, wout)
        dps = dmerged * f["sgs"]
        dpc = dmerged * f["sgc"]
        dgs = dmerged * f["ps"] * f["sgs"] * (1.0 - f["sgs"])
        dgc = dmerged * f["pc"] * f["sgc"] * (1.0 - f["sgc"])
        dpsb, dpcb = dps.astype(BF16), dpc.astype(BF16)
        gwps_ref[...] += _dot_tn(f["ys"].astype(BF16), dpsb)
        gwpc_ref[...] += _dot_tn(f["yc"].astype(BF16), dpcb)
        dys = _dot_nt(dpsb, wps)
        dyc = _dot_nt(dpcb, wpc)

        dcb = dyc * f["cv"]
        dcv = dyc * cb
        rows = lax.broadcasted_iota(jnp.int32, dcv.shape, 0)
        n0, n1 = nxt[0:1, :], nxt[1:2, :]
        d1 = jnp.where(rows == TB - 1, n0, pltpu.roll(dcv, TB - 1, 0))
        d2 = jnp.where(rows == TB - 2, n0, jnp.where(rows == TB - 1, n1, pltpu.roll(dcv, TB - 2, 0)))
        dv = cw[2:3, :] * dcv + cw[1:2, :] * d1 + cw[0:1, :] * d2
        nxt[0:2, :] = dcv[0:2, :]
        gcw_ref[0:1, :] += jnp.sum(dcv * f["v2"], axis=0, keepdims=True)
        gcw_ref[1:2, :] += jnp.sum(dcv * f["v1"], axis=0, keepdims=True)
        gcw_ref[2:3, :] += jnp.sum(dcv * f["v"], axis=0, keepdims=True)

        dz = dys * f["yg"] * f["sz"] * (1.0 - f["sz"])
        dzb = dz.astype(BF16)
        gwglu_ref[...] += _dot_tn(f["yg"].astype(BF16), dzb)
        gbglu_ref[...] += jnp.sum(dz, axis=0, keepdims=True)
        dyg = dys * f["sz"] + _dot_nt(dzb, wglu)
        dyl_ref[...] = f["gelu_vjp"](dyg)[0]

        dp_ref[:, Ds:Ds + Dc] = dcb.astype(BF16)
        dp_ref[:, Ds + Dc:Ds + 2 * Dc] = (dv * cx).astype(BF16)
        dp_ref[:, Ds + 2 * Dc:Ds + 3 * Dc] = (dv * cc).astype(BF16)
        dp_ref[:, Ds + 3 * Dc:Ds + 3 * Dc + D] = dgs.astype(BF16)
        dp_ref[:, Ds + 3 * Dc + D:IN] = dgc.astype(BF16)

    S = jax.ShapeDtypeStruct
    return _call_carrying(
        body, carried, (p2, p2, p2, p2, p2, p2, p2, ylin2, dx1, g1, wglu, bglu, cw, wps, wpc, wout),
        out_shape=[S((N, Ds), F32), S((N, IN), BF16), S((D, D), F32), S((Ds, D), F32), S((Dc, D), F32),
                   S((Ds, Ds), F32), S((1, Ds), F32), S((SUBLANES, Dc), F32), S((Bl, 1, D), F32)],
        grid=(nblk,),
        in_specs=_mix_in_specs(TB, D, Ds, Dc, bps, rev) + [
            pl.BlockSpec((TB, D), lambda i: (rev(i), 0)), _seq_spec(D, bps, nblk),
            _const_spec((Ds, Ds)), _const_spec((1, Ds)), _const_spec((SUBLANES, Dc)),
            _const_spec((Ds, D)), _const_spec((Dc, D)), _const_spec((D, D))],
        out_specs=[pl.BlockSpec((TB, Ds), lambda i: (rev(i), 0)), pl.BlockSpec((TB, IN), lambda i: (rev(i), 0)),
                   _const_spec((D, D)), _const_spec((Ds, D)), _const_spec((Dc, D)), _const_spec((Ds, Ds)),
                   _const_spec((1, Ds)), _const_spec((SUBLANES, Dc)), _seq_spec(D, bps, nblk)],
        scratch_shapes=[pltpu.VMEM((SUBLANES, Dc), F32)],
        compiler_params=_cparams(dimension_semantics=("arbitrary",)), name="mix_bwd")


def _mlp_fwd_bwd(x1, tgt, sh2, sc2, g2, n2g, fg, w1, w2):
    N, D = x1.shape
    Dff = w1.shape[1]
    Bl = sh2.shape[0]
    TB = _row_block(N // Bl)
    bps = (N // Bl) // TB

    def body(x1_ref, t_ref, sh_ref, sc_ref, g2_ref, n2_ref, fg_ref, w1_ref, w2_ref,
             dx1_ref, h2_ref, da_ref, sq_ref, df_ref, loss_ref, gfg_ref, gn2_ref, dsh_ref, dsc_ref, dg2_ref):
        i = pl.program_id(0)

        @pl.when(i == 0)
        def _():
            for ref in (loss_ref, gfg_ref, gn2_ref):
                ref[...] = jnp.zeros_like(ref)

        @pl.when(i % bps == 0)
        def _():
            for ref in (dsh_ref, dsc_ref, dg2_ref):
                ref[...] = jnp.zeros_like(ref)

        x1v = x1_ref[...]
        sc, g2v, n2 = sc_ref[...], g2_ref[...], n2_ref[...]
        xh2, r2 = _rms_fwd(x1v)
        xn2 = xh2 * n2
        h2 = (xn2 * (1.0 + sc) + sh_ref[...]).astype(BF16)
        a = _dot(h2, w1_ref[...])
        ra = jnp.maximum(a, 0.0)
        sq = (ra * ra).astype(BF16)
        fv = _dot(sq, w2_ref[...])
        x2 = x1v + g2v * fv
        xh3, r3 = _rms_fwd(x2)
        err = xh3 * fg_ref[...] - t_ref[...]
        loss_ref[...] += 0.5 * jnp.sum(jnp.mean(err * err, axis=-1, keepdims=True), axis=0, keepdims=True)
        dy = err * (1.0 / D)
        gfg_ref[...] += jnp.sum(dy * xh3, axis=0, keepdims=True)
        dx2 = _rms_bwd(dy * fg_ref[...], xh3, r3)
        dg2_ref[...] += jnp.sum(dx2 * fv, axis=0, keepdims=True)
        df = (g2v * dx2).astype(BF16)
        dsq = _dot_nt(df, w2_ref[...])
        da = (2.0 * ra * dsq).astype(BF16)
        dh2 = _dot_nt(da, w1_ref[...])
        dsh_ref[...] += jnp.sum(dh2, axis=0, keepdims=True)
        dsc_ref[...] += jnp.sum(dh2 * xn2, axis=0, keepdims=True)
        dxn2 = dh2 * (1.0 + sc)
        gn2_ref[...] += jnp.sum(dxn2 * xh2, axis=0, keepdims=True)
        dx1_ref[...] = dx2 + _rms_bwd(dxn2 * n2, xh2, r2)
        h2_ref[...] = h2
        da_ref[...] = da
        sq_ref[...] = sq
        df_ref[...] = df

    S = jax.ShapeDtypeStruct
    row = lambda w: pl.BlockSpec((TB, w), lambda i: (i, 0))
    return pl.pallas_call(
        body,
        out_shape=[S((N, D), F32), S((N, D), BF16), S((N, Dff), BF16), S((N, Dff), BF16), S((N, D), BF16),
                   S((1, 1), F32), S((1, D), F32), S((1, D), F32), S((Bl, 1, D), F32), S((Bl, 1, D), F32), S((Bl, 1, D), F32)],
        grid=(N // TB,),
        in_specs=[row(D), row(D), _seq_spec(D, bps), _seq_spec(D, bps), _seq_spec(D, bps),
                  _const_spec((1, D)), _const_spec((1, D)), _const_spec((D, Dff)), _const_spec((Dff, D))],
        out_specs=[row(D), row(D), row(Dff), row(Dff), row(D), _const_spec((1, 1)), _const_spec((1, D)), _const_spec((1, D)),
                   _seq_spec(D, bps), _seq_spec(D, bps), _seq_spec(D, bps)],
        compiler_params=_cparams(dimension_semantics=("arbitrary",)), name="mlp_fwd_bwd",
    )(x1, tgt, sh2, sc2, g2, n2g, fg, w1, w2)


_NO_EXCHANGE = _Carried((), (), {}, (), lambda ins, outs, sems: ((lambda: None), (lambda: None)))


def _grad_w(a, b, name, carried=_NO_EXCHANGE):
    N, K1 = a.shape
    K2 = b.shape[1]
    t1 = 1024 if K1 % 1024 == 0 else K1
    t2 = 1024 if K2 % 1024 == 0 else K2
    tn = 2048 if N % 2048 == 0 else N

    def body(a_ref, b_ref, o_ref):
        @pl.when(pl.program_id(2) == 0)
        def _():
            o_ref[...] = jnp.zeros_like(o_ref)

        o_ref[...] += _dot_tn(a_ref[...], b_ref[...])

    (g,), extra = _call_carrying(
        body, carried, (a, b), out_shape=[jax.ShapeDtypeStruct((K1, K2), F32)], grid=(K1 // t1, K2 // t2, N // tn),
        in_specs=[pl.BlockSpec((tn, t1), lambda i, j, k: (k, i)), pl.BlockSpec((tn, t2), lambda i, j, k: (k, j))],
        out_specs=[pl.BlockSpec((t1, t2), lambda i, j, k: (i, j))],
        compiler_params=_cparams(dimension_semantics=("arbitrary", "arbitrary", "arbitrary")), name=name)
    return g, extra


def _inproj_bwd(x2, dx1, dp, sh1, sc1, n1g, w_in, carried):
    N, D = x2.shape
    IN = w_in.shape[1]
    Bl = sh1.shape[0]
    TB = _row_block(N // Bl, 512)
    bps = (N // Bl) // TB

    def body(x_ref, dx1_ref, dp_ref, sh_ref, sc_ref, g_ref, w_ref, gx_ref, h_ref, gn1_ref, dsh_ref, dsc_ref):
        i = pl.program_id(0)

        @pl.when(i == 0)
        def _():
            gn1_ref[...] = jnp.zeros_like(gn1_ref)

        @pl.when(i % bps == 0)
        def _():
            dsh_ref[...] = jnp.zeros_like(dsh_ref)
            dsc_ref[...] = jnp.zeros_like(dsc_ref)

        sc, n1 = sc_ref[...], g_ref[...]
        xh, r = _rms_fwd(x_ref[...])
        xn = xh * n1
        h_ref[...] = (xn * (1.0 + sc) + sh_ref[...]).astype(BF16)
        dh = _dot_nt(dp_ref[...], w_ref[...])
        dsh_ref[...] += jnp.sum(dh, axis=0, keepdims=True)
        dsc_ref[...] += jnp.sum(dh * xn, axis=0, keepdims=True)
        dxn = dh * (1.0 + sc)
        gn1_ref[...] += jnp.sum(dxn * xh, axis=0, keepdims=True)
        gx_ref[...] = dx1_ref[...] + _rms_bwd(dxn * n1, xh, r)

    S = jax.ShapeDtypeStruct
    row = lambda w: pl.BlockSpec((TB, w), lambda i: (i, 0))
    return _call_carrying(
        body, carried, (x2, dx1, dp, sh1, sc1, n1g, w_in),
        out_shape=[S((N, D), F32), S((N, D), BF16), S((1, D), F32), S((Bl, 1, D), F32), S((Bl, 1, D), F32)],
        grid=(N // TB,),
        in_specs=[row(D), row(D), row(IN), _seq_spec(D, bps), _seq_spec(D, bps), _const_spec((1, D)), _const_spec((D, IN))],
        out_specs=[row(D), row(D), _const_spec((1, D)), _seq_spec(D, bps), _seq_spec(D, bps)],
        compiler_params=_cparams(dimension_semantics=("arbitrary",)), name="inproj_bwd")


def _diag_mask(gb, a, b):
    rows = lax.broadcasted_iota(jnp.int32, (gb * a, gb * b), 0) // a
    cols = lax.broadcasted_iota(jnp.int32, (gb * a, gb * b), 1) // b
    return (rows == cols).astype(F32)


def _diag_blocks_from_groups(m, nb):
    G, a, b = m.shape
    gb = G // nb
    return jnp.tile(m.reshape(nb, gb * a, b), (1, 1, gb)) * _diag_mask(gb, a, b)[None]


def _groups_from_diag_blocks(d, G, a, b):
    nb = d.shape[0]
    gb = G // nb
    picked = (d * _diag_mask(gb, a, b)[None]).reshape(nb, gb * a, gb, b)
    return jnp.sum(picked, axis=2).reshape(G, a, b)


def _pad_rows(v, rows):
    return jnp.concatenate([v, jnp.zeros((rows - v.shape[0],) + v.shape[1:], v.dtype)], axis=0)


def _pack(vs):
    flat = jnp.concatenate([v.reshape(-1) for v in vs])
    n = flat.shape[0]
    tile = SUBLANES * LANES
    npad = -(-n // tile) * tile
    flat = jnp.concatenate([flat, jnp.zeros((npad - n,), flat.dtype)])
    return flat.reshape(npad // LANES, LANES)


def _unpack(packed, shapes):
    flat = packed.reshape(-1)
    out, off = [], 0
    for s in shapes:
        n = 1
        for d in s:
            n *= d
        out.append(flat[off:off + n].reshape(s))
        off += n
    return out


def kernel(x, c, norm1_g, norm2_g, w_ada, b_ada, w_in, lam_re, lam_im, log_dt, b_re, b_im, c_re, c_im, d_skip, w_glu, b_glu, conv_w, w_proj_ssm, w_proj_conv, w_out, w_ff1, w_ff2, final_g, loss_target, m_norm1_g, m_norm2_g, m_w_ada, m_b_ada, m_w_in, m_lam_re, m_lam_im, m_log_dt, m_b_re, m_b_im, m_c_re, m_c_im, m_d_skip, m_w_glu, m_b_glu, m_conv_w, m_w_proj_ssm, m_w_proj_conv, m_w_out, m_w_ff1, m_w_ff2, m_final_g, v_norm1_g, v_norm2_g, v_w_ada, v_b_ada, v_w_in, v_lam_re, v_lam_im, v_log_dt, v_b_re, v_b_im, v_c_re, v_c_im, v_d_skip, v_w_glu, v_b_glu, v_conv_w, v_w_proj_ssm, v_w_proj_conv, v_w_out, v_w_ff1, v_w_ff2, v_final_g):
    Bl, L, D = x.shape
    N = Bl * L
    Ds = Dc = D // 2
    G, H, P = Ds // SSM_GROUP, SSM_GROUP, SSM_STATE
    GP = G * P
    NP = GP // LANES
    nb = _s5_dims(Bl, L, Ds)[4]
    IN = Ds + 3 * Dc + 2 * D
    ax, ay, ac = _mesh_pos()
    q = 2 * ax + ay
    dev = 2 * q + ac

    big_names = ["w_in", "w_ff1", "w_ff2", "w_out", "w_proj_ssm", "w_proj_conv", "w_glu"]
    big_w = dict(w_in=w_in[0], w_ff1=w_ff1[0], w_ff2=w_ff2[0], w_out=w_out[0],
                 w_proj_ssm=w_proj_ssm[0], w_proj_conv=w_proj_conv[0], w_glu=w_glu[0])
    big_axis = dict(w_in=1, w_ff1=1, w_ff2=0, w_out=0, w_proj_ssm=1, w_proj_conv=1, w_glu=0)
    axes = [big_axis[k] for k in big_names]
    shard_shapes = [big_w[k].shape for k in big_names]
    pos = jnp.stack([q, ac]).astype(jnp.int32)
    own_only = dict(zip(big_names, _cast_into_full([big_w[k] for k in big_names], axes, pos, "cast_weights")))
    Dcs = conv_w.shape[2]
    first, (w_in_full,) = _allgather8(_pack([c, conv_w[0]]), "allgather_c_w_in",
                                      _carry_allgather([own_only["w_in"]], [big_axis["w_in"]], [big_w["w_in"].shape]))
    full = {"w_in": w_in_full}
    first = first.reshape(N_DEV, -1)
    c_all = first[:, :Bl * D].reshape(N_DEV * Bl, D)
    cw = first[0::2, Bl * D:Bl * D + 3 * Dcs].reshape(N_CHIPS, 3, Dcs).transpose(1, 0, 2).reshape(3, Dc)
    cw8 = _pad_rows(cw, SUBLANES)
    Ca = w_ada.shape[2]
    b_ada_sh = lax.dynamic_slice_in_dim(b_ada, q * Ca, Ca, axis=1)
    mod_part = _ada_fwd(c_all, w_ada[0], b_ada_sh)
    mod_g = _allgather8(mod_part, "allgather_mod")
    mod_all = mod_g[0::2].transpose(1, 0, 2).reshape(N_DEV * Bl, N_CHIPS * Ca)
    mod = lax.dynamic_slice_in_dim(mod_all, dev * Bl, Bl, axis=0)
    sh1, sc1, g1, sh2, sc2, g2 = [mod[:, k * D:(k + 1) * D].reshape(Bl, 1, D) for k in range(6)]

    ldt_c = log_dt[0].reshape(G, 1)
    bt_r = b_re[0].transpose(2, 0, 1)
    bt_i = b_im[0].transpose(2, 0, 1)
    lbr, lbi, bbt_r, bbt_i = _s5_disc(lam_re[0], lam_im[0], ldt_c, bt_r, bt_i)
    lam_r_p = lbr.reshape(NP, LANES)
    lam_i_p = lbi.reshape(NP, LANES)
    Bm_r = _diag_blocks_from_groups(bbt_r.transpose(1, 0, 2), nb).astype(BF16)
    Bm_i = _diag_blocks_from_groups(bbt_i.transpose(1, 0, 2), nb).astype(BF16)
    Cm_r = _diag_blocks_from_groups(c_re[0], nb).astype(BF16)
    Cm_i = _diag_blocks_from_groups(-c_im[0], nb).astype(BF16)

    x2 = x.reshape(N, D)
    mixer_w = ["w_out", "w_proj_ssm", "w_proj_conv", "w_glu"]
    mlp_w = ["w_ff1", "w_ff2"]
    layout = lambda ks: ([big_axis[k] for k in ks], [big_w[k].shape for k in ks])
    gather = lambda ks: _carry_allgather([own_only[k] for k in ks], *layout(ks))
    p2, gathered = _inproj_fwd(x2, sh1, sc1, norm1_g, full["w_in"], gather(mixer_w))
    full.update(zip(mixer_w, gathered))
    p3 = p2.reshape(Bl, L, IN)
    (Sr, Si, Sb_r, Sb_i, ylin3), gathered = _s5_fwd(p3, Bm_r, Bm_i, Cm_r, Cm_i, lam_r_p, lam_i_p, d_skip, gather(mlp_w))
    full.update(zip(mlp_w, gathered))
    ylin2 = ylin3.reshape(N, Ds)
    mix_w = (full["w_glu"], b_glu, cw8, full["w_proj_ssm"], full["w_proj_conv"], full["w_out"])
    x1 = _mix_fwd(p2, ylin2, x2, g1, *mix_w)

    (dx1, h2b, dab, sqb, dfb, loss_p, g_fg, g_n2, dsh2, dsc2, dg2) = _mlp_fwd_bwd(
        x1, loss_target.reshape(N, D), sh2, sc2, g2, norm2_g, final_g.reshape(1, D), full["w_ff1"], full["w_ff2"])
    g_full = {"w_ff1": _grad_w(h2b, dab, "grad_w_ff1")[0], "w_ff2": _grad_w(sqb, dfb, "grad_w_ff2")[0]}

    exchange = lambda ks: _carry_sibling_exchange([g_full[k] for k in ks], *layout(ks))
    presum = lambda ks, theirs: _presum([g_full[k] for k in ks], list(theirs), *layout(ks), pos, "presum_" + ks[0])
    chip_sum = lambda ks, parts, recv: _sum_chips(list(parts), list(recv), pos, "sum_" + ks[0])

    (dyl2, dp2, gw_out, gw_ps, gw_pc, gw_glu, gb_glu, gcw8, dg1), theirs_mlp = _mix_bwd(
        p2, ylin2, dx1, g1, *mix_w, exchange(mlp_w))
    g_full.update(w_out=gw_out, w_proj_ssm=gw_ps, w_proj_conv=gw_pc, w_glu=gw_glu)
    parts_mlp = presum(mlp_w, theirs_mlp)
    (dp3, dBm_r, dBm_i, dCm_r, dCm_i, dlam_r_p, dlam_i_p, g_dsk), extra = _s5_bwd(
        dyl2.reshape(Bl, L, Ds), p3, dp2.reshape(Bl, L, IN), Sr, Si, Sb_r, Sb_i, Bm_r, Bm_i, Cm_r, Cm_i, lam_r_p, lam_i_p,
        d_skip, _carry_join(_carry_chip_scatter(parts_mlp), exchange(mixer_w)))
    recv_mlp, theirs_mix = extra[:len(mlp_w)], extra[len(mlp_w):]
    halves_mlp = chip_sum(mlp_w, parts_mlp, recv_mlp)
    parts_mix = presum(mixer_w, theirs_mix)
    dp_all = dp3.reshape(N, IN)
    (grad_x2, hb, g_n1, dsh1, dsc1), _ = _inproj_bwd(x2, dx1, dp_all, sh1, sc1, norm1_g, full["w_in"], _NO_EXCHANGE)

    dbbt_r = _groups_from_diag_blocks(dBm_r, G, H, P).transpose(1, 0, 2)
    dbbt_i = _groups_from_diag_blocks(dBm_i, G, H, P).transpose(1, 0, 2)
    dc_re = _groups_from_diag_blocks(dCm_r, G, H, P)
    dc_im = -_groups_from_diag_blocks(dCm_i, G, H, P)
    dmod = jnp.concatenate([dsh1, dsc1, dg1, dsh2, dsc2, dg2], axis=-1).reshape(Bl, 6 * D)
    small = [g_n1, g_n2, g_fg, g_dsk, gb_glu, gcw8[:3], dlam_r_p, dlam_i_p, dbbt_r, dbbt_i, dc_re, dc_im]
    small_shapes = [v.shape for v in small]
    n_small = sum(int(v.size) for v in small)
    small_slots = _place_in_slot(_pack(small + [dmod]), jnp.reshape(dev, (1,)).astype(jnp.int32), "place_small")

    g_full["w_in"], extra = _grad_w(
        hb, dp_all, "grad_w_in",
        _carry_join(_carry_join(_carry_sibling_share(halves_mlp), _carry_chip_scatter(parts_mix)), _carry_allgather8(small_slots)))
    reduced = dict(zip(mlp_w, extra[:len(mlp_w)]))
    halves_mix = chip_sum(mixer_w, parts_mix, extra[len(mlp_w):len(mlp_w) + len(mixer_w)])
    gathered = extra[-1]
    theirs_in = _run_carried(exchange(["w_in"]), "rs_exchange_w_in")
    parts_in = presum(["w_in"], theirs_in)
    recv_in = _run_carried(_carry_chip_scatter(parts_in), "rs_scatter_w_in")
    halves_in = chip_sum(["w_in"], parts_in, recv_in)
    reduced.update(zip(mixer_w + ["w_in"], _run_carried(_carry_sibling_share(halves_mix + halves_in), "rs_share_rest")))

    red = _unpack(_sum_devices(gathered, "sum_small"), small_shapes)
    (r_n1, r_n2, r_fg, r_dsk, r_bglu, r_cw, r_dlr, r_dli, r_dbr, r_dbi, r_cre, r_cim) = red
    dmod_all = gathered.reshape(N_DEV, -1)[:, n_small:n_small + Bl * 6 * D].reshape(N_DEV * Bl, 6 * D)
    gw_ada, gb_ada = _ada_bwd(c_all, lax.dynamic_slice_in_dim(dmod_all, q * Ca, Ca, axis=1), dmod_all)
    g_lr, g_li, g_ldt, g_bt_r, g_bt_i = _s5_disc_bwd(lam_re[0], lam_im[0], ldt_c, bt_r, bt_i,
                                                   r_dlr.reshape(G, P), r_dli.reshape(G, P), r_dbr, r_dbi)

    grads = dict(
        norm1_g=r_n1, norm2_g=r_n2, w_ada=gw_ada, b_ada=gb_ada, lam_re=g_lr, lam_im=g_li, log_dt=g_ldt.reshape(1, G),
        b_re=g_bt_r.transpose(1, 2, 0), b_im=g_bt_i.transpose(1, 2, 0), c_re=r_cre, c_im=r_cim, d_skip=r_dsk,
        b_glu=r_bglu, conv_w=lax.dynamic_slice_in_dim(r_cw, q * Dcs, Dcs, axis=1), final_g=r_fg, **reduced)
    weights = dict(norm1_g=norm1_g, norm2_g=norm2_g, w_ada=w_ada, b_ada=b_ada, w_in=w_in, lam_re=lam_re, lam_im=lam_im,
                   log_dt=log_dt, b_re=b_re, b_im=b_im, c_re=c_re, c_im=c_im, d_skip=d_skip, w_glu=w_glu, b_glu=b_glu,
                   conv_w=conv_w, w_proj_ssm=w_proj_ssm, w_proj_conv=w_proj_conv, w_out=w_out, w_ff1=w_ff1, w_ff2=w_ff2,
                   final_g=final_g)
    m_in = dict(norm1_g=m_norm1_g, norm2_g=m_norm2_g, w_ada=m_w_ada, b_ada=m_b_ada, w_in=m_w_in, lam_re=m_lam_re,
                lam_im=m_lam_im, log_dt=m_log_dt, b_re=m_b_re, b_im=m_b_im, c_re=m_c_re, c_im=m_c_im, d_skip=m_d_skip,
                w_glu=m_w_glu, b_glu=m_b_glu, conv_w=m_conv_w, w_proj_ssm=m_w_proj_ssm, w_proj_conv=m_w_proj_conv,
                w_out=m_w_out, w_ff1=m_w_ff1, w_ff2=m_w_ff2, final_g=m_final_g)
    v_in = dict(norm1_g=v_norm1_g, norm2_g=v_norm2_g, w_ada=v_w_ada, b_ada=v_b_ada, w_in=v_w_in, lam_re=v_lam_re,
                lam_im=v_lam_im, log_dt=v_log_dt, b_re=v_b_re, b_im=v_b_im, c_re=v_c_re, c_im=v_c_im, d_skip=v_d_skip,
                w_glu=v_w_glu, b_glu=v_b_glu, conv_w=v_conv_w, w_proj_ssm=v_w_proj_ssm, w_proj_conv=v_w_proj_conv,
                w_out=v_w_out, w_ff1=v_w_ff1, w_ff2=v_w_ff2, final_g=v_final_g)
    names = list(weights)
    grads = {k: grads[k].reshape(weights[k].shape) for k in names}

    big_upd = big_names + ["w_ada"]
    delta, new_m, new_v = {}, {}, {}
    flat2 = lambda a: a.reshape(-1, a.shape[-1])
    d_, m_, v_, g_ = _adamw([flat2(weights[k]) for k in big_upd], [flat2(grads[k]) for k in big_upd],
                            [flat2(m_in[k]) for k in big_upd], [flat2(v_in[k]) for k in big_upd], "adamw_big")
    for k, dd, mm, vv, gg in zip(big_upd, d_, m_, v_, g_):
        shp = weights[k].shape
        delta[k], new_m[k], new_v[k], grads[k] = dd.reshape(shp), mm.reshape(shp), vv.reshape(shp), gg.reshape(shp)
    small_upd = [k for k in names if k not in big_upd]
    d_, m_, v_ = _adamw_many([flat2(weights[k]) for k in small_upd], [flat2(grads[k]) for k in small_upd],
                             [flat2(m_in[k]) for k in small_upd], [flat2(v_in[k]) for k in small_upd])
    for k, dd, mm, vv in zip(small_upd, d_, m_, v_):
        shp = weights[k].shape
        delta[k], new_m[k], new_v[k] = dd.reshape(shp), mm.reshape(shp), vv.reshape(shp)

    loss = lax.psum(loss_p[0, 0], ("x", "y", "c"))
    grad_x = grad_x2.reshape(Bl, L, D)
    return (loss, grad_x, *[grads[k] for k in names], *[delta[k] for k in names],
            *[new_m[k] for k in names], *[new_v[k] for k in names])
```

```python
import functools
from typing import Callable, NamedTuple

import jax
import jax.numpy as jnp
from jax import lax
from jax.experimental import pallas as pl
from jax.experimental.pallas import tpu as pltpu

F32 = jnp.float32
BF16 = jnp.bfloat16
MESH = pl.DeviceIdType.MESH
N_CHIPS = 4
N_DEV = 8
LANES = 128
SUBLANES = 8
V7X_VMEM_BYTES = 64 * 1024 * 1024
VMEM_LIMIT = V7X_VMEM_BYTES - 6 * 1024 * 1024
SSM_GROUP = 16
SSM_STATE = 64
S5_ROW_PAD = 4
HALO_ROWS = 16
RMS_EPS = 1e-6
ADAM_LR, ADAM_B1, ADAM_B2, ADAM_EPS, ADAM_WD, ADAM_STEP = 0.001, 0.9, 0.999, 1e-08, 0.01, 10

ANY = pl.BlockSpec(memory_space=pl.ANY)
VMEM_SPEC = pl.BlockSpec(memory_space=pltpu.VMEM)


def _cparams(**kw):
    return pltpu.CompilerParams(vmem_limit_bytes=VMEM_LIMIT, **kw)


def _dot(a, b):
    return jnp.dot(a, b, preferred_element_type=F32)


def _dot_nt(a, b):
    return lax.dot_general(a, b, (((1,), (1,)), ((), ())), preferred_element_type=F32)


def _dot_tn(a, b):
    return lax.dot_general(a, b, (((0,), (0,)), ((), ())), preferred_element_type=F32)


def _mesh_pos():
    return lax.axis_index("x"), lax.axis_index("y"), lax.axis_index("c")


def _allgather8(v, name, carried=None):
    r, c = v.shape

    def body(x_ref, out_ref, send_sems, recv_sems, local_sem):
        x, y, cc = _mesh_pos()
        me, sibling = (x, y, cc), (x, y, 1 - cc)
        chips = [(1 - x, y), (x, 1 - y), (1 - x, 1 - y)]

        def slot(px, py, pc):
            return out_ref.at[4 * px + 2 * py + pc]

        def copy(k, block, to, src=None):
            return pltpu.make_async_remote_copy(
                src_ref=slot(*block) if src is None else src, dst_ref=slot(*block),
                send_sem=send_sems.at[k], recv_sem=recv_sems.at[k], device_id=to, device_id_type=MESH)

        mine = pltpu.make_async_copy(x_ref, slot(*me), local_sem)
        mine.start()
        first = [copy(0, me, sibling, src=x_ref)]
        first += [copy(1 + j, me, (*chip, cc), src=x_ref) for j, chip in enumerate(chips)]
        for cp in first:
            cp.start()
        passed = [copy(4 + j, (*chip, cc), sibling) for j, chip in enumerate(chips)]
        for j, chip in enumerate(chips):
            copy(1 + j, (*chip, cc), me).wait_recv()
            passed[j].start()
        copy(0, sibling, me).wait_recv()
        for j, chip in enumerate(chips):
            copy(4 + j, (*chip, 1 - cc), me).wait_recv()
        for cp in first + passed:
            cp.wait_send()
        mine.wait()

    sems = [pltpu.SemaphoreType.DMA((7,)), pltpu.SemaphoreType.DMA((7,)), pltpu.SemaphoreType.DMA]
    out_shape = jax.ShapeDtypeStruct((N_DEV, r, c), v.dtype)
    if carried is None:
        return pl.pallas_call(body, out_shape=out_shape, in_specs=[VMEM_SPEC], out_specs=VMEM_SPEC,
                              scratch_shapes=sems, name=name)(v)
    (out,), extra = _call_carrying(body, carried, (v,), out_shape=[out_shape], in_specs=[VMEM_SPEC],
                                   out_specs=[VMEM_SPEC], scratch_shapes=sems, name=name)
    return out, extra


def _shard_region(ref, axis, shard_shape, q, half):
    R, C = shard_shape
    r0, nr = (0, R) if half is None else (half * (R // 2), R // 2)
    if axis == 1:
        return ref.at[pl.ds(r0, nr), pl.ds(q * C, C)]
    return ref.at[pl.ds(q * R + r0, nr), :]


class _Carried(NamedTuple):
    inputs: tuple
    out_shapes: tuple
    aliases: dict
    sems: tuple
    steps: Callable


def _carry_join(a, b):
    na_i, na_o, na_s = len(a.inputs), len(a.out_shapes), len(a.sems)

    def steps(ins, outs, sems):
        sa, fa = a.steps(ins[:na_i], outs[:na_o], sems[:na_s])
        sb, fb = b.steps(ins[na_i:], outs[na_o:], sems[na_s:])


        def start():
            sa()
            sb()

        def finish():
            fa()
            fb()

        return start, finish

    aliases = dict(a.aliases)
    aliases.update({na_i + i: na_o + o for i, o in b.aliases.items()})
    return _Carried(a.inputs + b.inputs, a.out_shapes + b.out_shapes, aliases, a.sems + b.sems, steps)


def _call_carrying(body, carried, args, *, out_shape, in_specs, out_specs, scratch_shapes=(), grid=None, aliases=None,
                   name, **kw):
    n_in, n_out, n_sc = len(in_specs), len(out_specs), len(scratch_shapes)
    n_ci, n_co = len(carried.inputs), len(carried.out_shapes)

    def wrapped(*refs):
        ins, refs = refs[:n_in], refs[n_in:]
        c_ins, refs = refs[:n_ci], refs[n_ci:]
        outs, refs = refs[:n_out], refs[n_out:]
        c_outs, refs = refs[:n_co], refs[n_co:]
        scratch, c_sems = refs[:n_sc], refs[n_sc:]
        fns = carried.steps(c_ins, c_outs, c_sems)
        start, finish = fns[0], fns[-1]
        relay = fns[1] if len(fns) == 3 else None
        if grid is None:
            start()
            body(*ins, *outs, *scratch)
            if relay is not None:
                relay()
            finish()
        else:
            ids = [pl.program_id(d) for d in range(len(grid))]
            first = functools.reduce(jnp.logical_and, [i == 0 for i in ids])
            last = functools.reduce(jnp.logical_and, [i == g - 1 for i, g in zip(ids, grid)])
            pl.when(first)(start)
            body(*ins, *outs, *scratch)
            if relay is not None:
                pl.when(ids[0] == (3 * grid[0]) // 4 if len(grid) == 1 else last)(relay)
            pl.when(last)(finish)

    if grid is not None:
        kw["grid"] = grid
    io_aliases = dict(aliases or {})
    io_aliases.update({n_in + i: n_out + o for i, o in carried.aliases.items()})
    res = pl.pallas_call(
        wrapped, out_shape=list(out_shape) + list(carried.out_shapes),
        in_specs=list(in_specs) + [ANY] * n_ci, out_specs=list(out_specs) + [ANY] * n_co,
        scratch_shapes=list(scratch_shapes) + list(carried.sems),
        input_output_aliases=io_aliases, name=name, **kw,
    )(*args, *carried.inputs)
    return res[:n_out], res[n_out:]


def _run_carried(carried, name):
    return _call_carrying(lambda: None, carried, (), out_shape=(), in_specs=(), out_specs=(), name=name)[1]


def _place_in_slot(v, dev_arr, name):
    r, c = v.shape

    def body(d_ref, v_ref, o_ref):
        o_ref[...] = v_ref[...]

    return pl.pallas_call(
        body, out_shape=jax.ShapeDtypeStruct((N_DEV, r, c), v.dtype),
        grid_spec=pltpu.PrefetchScalarGridSpec(
            num_scalar_prefetch=1, grid=(1,), in_specs=[pl.BlockSpec((r, c), lambda i, d: (0, 0))],
            out_specs=pl.BlockSpec((None, r, c), lambda i, d: (d[0], 0, 0))),
        name=name)(dev_arr, v)


def _carry_allgather8(buf):
    def steps(ins, outs, sems):
        send_s, recv_s = sems
        out = outs[0]
        x, y, cc = _mesh_pos()
        me, sibling = (x, y, cc), (x, y, 1 - cc)
        chips = [(1 - x, y), (x, 1 - y), (1 - x, 1 - y)]

        def copy(k, block, to):
            px, py, pc = block
            slot = out.at[4 * px + 2 * py + pc]
            return pltpu.make_async_remote_copy(src_ref=slot, dst_ref=slot, send_sem=send_s.at[k], recv_sem=recv_s.at[k],
                                                device_id=to, device_id_type=MESH)

        first = [copy(0, me, sibling)] + [copy(1 + j, me, (*chip, cc)) for j, chip in enumerate(chips)]
        passed = [copy(4 + j, (*chip, cc), sibling) for j, chip in enumerate(chips)]

        def start():
            for cp in first:
                cp.start()

        def finish():
            for j, chip in enumerate(chips):
                copy(1 + j, (*chip, cc), me).wait_recv()
                passed[j].start()
            copy(0, sibling, me).wait_recv()
            for j, chip in enumerate(chips):
                copy(4 + j, (*chip, 1 - cc), me).wait_recv()
            for cp in first + passed:
                cp.wait_send()

        return start, finish

    return _Carried((buf,), (jax.ShapeDtypeStruct(buf.shape, buf.dtype),), {0: 0}, (pltpu.SemaphoreType.DMA((7,)),) * 2, steps)


def _carry_allgather(fulls, axes, shapes):
    n = len(fulls)
    return _Carried(tuple(fulls), tuple(jax.ShapeDtypeStruct(f.shape, f.dtype) for f in fulls),
                    {i: i for i in range(n)}, (pltpu.SemaphoreType.DMA((3 * n,)),) * 4,
                    lambda ins, outs, sems: _allgather_weights_steps(outs, axes, shapes, *sems))


def _allgather_weights_steps(outs, axes, shapes, send_s, recv_s, fsend_s, frecv_s):
    n = len(outs)
    x, y, c = _mesh_pos()
    q = 2 * x + y
    sibling = (x, y, 1 - c)
    chips = [(1 - x, y), (x, 1 - y), (1 - x, 1 - y)]

    def region(i, qq, half):
        return _shard_region(outs[i], axes[i], shapes[i], qq, half)

    def remote(src, dst, ss, rs, to):
        return pltpu.make_async_remote_copy(src_ref=src, dst_ref=dst, send_sem=ss, recv_sem=rs,
                                            device_id=to, device_id_type=MESH)

    def ici(i, j, qq):
        cx, cy = chips[j]
        reg = region(i, qq, c)
        return remote(reg, reg, send_s.at[3 * i + j], recv_s.at[3 * i + j], (cx, cy, c))

    def d2d(i, j, half):
        cx, cy = chips[j]
        reg = region(i, 2 * cx + cy, half)
        return remote(reg, reg, fsend_s.at[3 * i + j], frecv_s.at[3 * i + j], sibling)

    def start():
        for i in range(n):
            for j in range(3):
                ici(i, j, q).start()

    def relay():
        for i in range(n):
            for j, (cx, cy) in enumerate(chips):
                ici(i, j, 2 * cx + cy).wait_recv()
                d2d(i, j, c).start()

    def finish():
        for i in range(n):
            for j in range(3):
                d2d(i, j, 1 - c).wait_recv()
        for i in range(n):
            for j in range(3):
                ici(i, j, q).wait_send()
                d2d(i, j, c).wait_send()

    return start, relay, finish


def _carry_sibling_exchange(grads, axes, shapes):
    n = len(grads)

    def steps(ins, theirs, sems):
        send_s, recv_s = sems
        x, y, c = _mesh_pos()

        def copies():
            return [pltpu.make_async_remote_copy(
                src_ref=_shard_region(ins[i], axes[i], shapes[i], qq, 1 - c), dst_ref=theirs[i].at[qq],
                send_sem=send_s.at[N_CHIPS * i + qq], recv_sem=recv_s.at[N_CHIPS * i + qq],
                device_id=(x, y, 1 - c), device_id_type=MESH) for i in range(n) for qq in range(N_CHIPS)]

        def start():
            for cp in copies():
                cp.start()

        def finish():
            for cp in copies():
                cp.wait()

        return start, finish

    stacked = tuple(jax.ShapeDtypeStruct((N_CHIPS, R // 2, C), F32) for (R, C) in shapes)
    return _Carried(tuple(grads), stacked, {}, (pltpu.SemaphoreType.DMA((N_CHIPS * n,)),) * 2, steps)


def _carry_chip_scatter(parts):
    n = len(parts)

    def steps(ins, outs, sems):
        send_s, recv_s = sems
        x, y, c = _mesh_pos()
        chips = [(1 - x, y), (x, 1 - y), (1 - x, 1 - y)]

        def copies():
            return [pltpu.make_async_remote_copy(
                src_ref=ins[i].at[2 * cx + cy], dst_ref=outs[i].at[j],
                send_sem=send_s.at[3 * i + j], recv_sem=recv_s.at[3 * i + j],
                device_id=(cx, cy, c), device_id_type=MESH) for i in range(n) for j, (cx, cy) in enumerate(chips)]

        def start():
            for cp in copies():
                cp.start()

        def finish():
            for cp in copies():
                cp.wait()

        return start, finish

    return _Carried(tuple(parts), tuple(jax.ShapeDtypeStruct((3,) + p.shape[1:], p.dtype) for p in parts), {},
                    (pltpu.SemaphoreType.DMA((3 * n,)),) * 2, steps)


def _carry_sibling_share(fulls):
    n = len(fulls)

    def steps(ins, outs, sems):
        send_s, recv_s = sems
        x, y, c = _mesh_pos()

        def copy(i, half):
            rh = fulls[i].shape[0] // 2
            rows = outs[i].at[pl.ds(half * rh, rh), :]
            return pltpu.make_async_remote_copy(src_ref=rows, dst_ref=rows, send_sem=send_s.at[i], recv_sem=recv_s.at[i],
                                                device_id=(x, y, 1 - c), device_id_type=MESH)

        def start():
            for i in range(n):
                copy(i, c).start()

        def finish():
            for i in range(n):
                copy(i, 1 - c).wait_recv()
                copy(i, c).wait_send()

        return start, finish

    return _Carried(tuple(fulls), tuple(jax.ShapeDtypeStruct(f.shape, f.dtype) for f in fulls),
                    {i: i for i in range(n)}, (pltpu.SemaphoreType.DMA((n,)),) * 2, steps)


def _row_block(rows, target=256):
    return target if rows % target == 0 else rows


BF16_TILE_ROWS = 16


def _common_steps(rows, most=8):
    ns = most
    while ns > 1 and any(r % (ns * BF16_TILE_ROWS) for r in rows):
        ns //= 2
    return ns


def _cast_into_full(ws, axes, pos, name):
    n = len(ws)
    ns = _common_steps([w.shape[0] for w in ws])
    in_specs, out_specs, out_shape = [], [], []
    for w, axis in zip(ws, axes):
        R, C = w.shape
        in_specs.append(pl.BlockSpec((R // ns, C), lambda i, s: (i, 0)))
        if axis == 1:
            out_shape.append(jax.ShapeDtypeStruct((R, N_CHIPS * C), BF16))
            out_specs.append(pl.BlockSpec((R // ns, C), lambda i, s: (i, s[0])))
        else:
            out_shape.append(jax.ShapeDtypeStruct((N_CHIPS * R, C), BF16))
            out_specs.append(pl.BlockSpec((R // ns, C), lambda i, s: (s[0] * ns + i, 0)))

    def body(s_ref, *refs):
        for k in range(n):
            refs[n + k][...] = refs[k][...].astype(BF16)

    return pl.pallas_call(
        body, out_shape=out_shape,
        grid_spec=pltpu.PrefetchScalarGridSpec(num_scalar_prefetch=1, grid=(ns,), in_specs=in_specs, out_specs=out_specs),
        compiler_params=_cparams(), name=name)(pos, *ws)


def _presum(gs, theirs, axes, shapes, pos, name):
    n = len(gs)
    ns = _common_steps([R // 2 for R, _ in shapes], most=2)
    in_specs, t_specs, out_shape = [], [], []
    for (R, C), axis in zip(shapes, axes):
        rb = R // 2 // ns
        if axis == 1:
            in_specs.append(pl.BlockSpec((rb, C), lambda k, i, s: (s[1] * ns + i, k)))
        else:
            in_specs.append(pl.BlockSpec((rb, C), lambda k, i, s: (k * 2 * ns + s[1] * ns + i, 0)))
        t_specs.append(pl.BlockSpec((None, rb, C), lambda k, i, s: (k, i, 0)))
        out_shape.append(jax.ShapeDtypeStruct((N_CHIPS, R // 2, C), BF16))

    def body(s_ref, *refs):
        for k in range(n):
            refs[2 * n + k][...] = (refs[k][...] + refs[n + k][...]).astype(BF16)

    return pl.pallas_call(
        body, out_shape=out_shape,
        grid_spec=pltpu.PrefetchScalarGridSpec(num_scalar_prefetch=1, grid=(N_CHIPS, ns), in_specs=in_specs + t_specs,
                                               out_specs=t_specs),
        compiler_params=_cparams(), name=name)(pos, *gs, *theirs)


def _sum_chips(owns, recvs, pos, name):
    n = len(owns)
    ns = _common_steps([o.shape[1] for o in owns], most=2)
    o_specs, r_specs, out_specs, out_shape = [], [], [], []
    for o in owns:
        _, Rh, C = o.shape
        rb = Rh // ns
        o_specs.append(pl.BlockSpec((None, rb, C), lambda i, s: (s[0], i, 0)))
        r_specs.append(pl.BlockSpec((3, rb, C), lambda i, s: (0, i, 0)))
        out_specs.append(pl.BlockSpec((rb, C), lambda i, s: (s[1] * ns + i, 0)))
        out_shape.append(jax.ShapeDtypeStruct((2 * Rh, C), F32))

    def body(s_ref, *refs):
        for k in range(n):
            acc = refs[k][...].astype(F32)
            for j in range(3):
                acc = acc + refs[n + k][j].astype(F32)
            refs[2 * n + k][...] = acc

    return pl.pallas_call(
        body, out_shape=out_shape,
        grid_spec=pltpu.PrefetchScalarGridSpec(num_scalar_prefetch=1, grid=(ns,), in_specs=o_specs + r_specs,
                                               out_specs=out_specs),
        compiler_params=_cparams(), name=name)(pos, *owns, *recvs)


def _sum_devices(parts, name):
    K, R, C = parts.shape

    def body(p_ref, o_ref):
        acc = p_ref[0]
        for k in range(1, K):
            acc = acc + p_ref[k]
        o_ref[...] = acc

    return pl.pallas_call(body, out_shape=jax.ShapeDtypeStruct((R, C), F32), name=name)(parts)


def _adamw_math(w, g, m, v):
    nm = ADAM_B1 * m + (1.0 - ADAM_B1) * g
    nv = ADAM_B2 * v + (1.0 - ADAM_B2) * (g * g)
    m_hat = nm / (1.0 - ADAM_B1 ** ADAM_STEP)
    v_hat = nv / (1.0 - ADAM_B2 ** ADAM_STEP)
    return -ADAM_LR * (m_hat / (jnp.sqrt(v_hat) + ADAM_EPS) + ADAM_WD * w), nm, nv


def _adamw_many(ws, gs, ms, vs):
    n = len(ws)

    def body(*refs):
        for k in range(n):
            w, g, m, v = (refs[j * n + k][...] for j in range(4))
            for j, val in enumerate(_adamw_math(w, g, m, v)):
                refs[(4 + j) * n + k][...] = val

    shapes = [jax.ShapeDtypeStruct(w.shape, F32) for w in ws]
    res = pl.pallas_call(body, out_shape=shapes * 3, compiler_params=_cparams(), name="adamw_small")(*ws, *gs, *ms, *vs)
    return res[:n], res[n:2 * n], res[2 * n:]


def _adamw(ws, gs, ms, vs, name):
    n = len(ws)
    ns = _common_steps([w.shape[0] for w in ws])
    specs = [pl.BlockSpec((w.shape[0] // ns, w.shape[1]), lambda i: (i, 0)) for w in ws]

    def body(*refs):
        for k in range(n):
            w, g, m, v = (refs[j * n + k][...] for j in range(4))
            for j, val in enumerate(_adamw_math(w, g, m, v) + (g,)):
                refs[(4 + j) * n + k][...] = val

    shapes = [jax.ShapeDtypeStruct(w.shape, F32) for w in ws]
    res = pl.pallas_call(body, out_shape=shapes * 4, grid=(ns,), in_specs=specs * 4, out_specs=specs * 4,
                         compiler_params=_cparams(), name=name)(*ws, *gs, *ms, *vs)
    return res[:n], res[n:2 * n], res[2 * n:3 * n], res[3 * n:]


def _silu(v):
    return v * jax.nn.sigmoid(v)


def _ada_fwd(c_all, w_sh, b_sh):
    S, D = c_all.shape
    Ca = w_sh.shape[1]
    cb = 512 if Ca % 512 == 0 else Ca

    def body(c_ref, w_ref, b_ref, o_ref):
        act = _silu(c_ref[...]).astype(BF16)
        o_ref[...] = _dot(act, w_ref[...].astype(BF16)) + b_ref[...]

    return pl.pallas_call(
        body, out_shape=jax.ShapeDtypeStruct((S, Ca), F32), grid=(Ca // cb,),
        in_specs=[pl.BlockSpec((S, D), lambda j: (0, 0)), pl.BlockSpec((D, cb), lambda j: (0, j)),
                  pl.BlockSpec((1, cb), lambda j: (0, j))],
        out_specs=pl.BlockSpec((S, cb), lambda j: (0, j)), name="ada_fwd")(c_all, w_sh, b_sh)


def _ada_bwd(c_all, dmod_sh, dmod_all):
    S, D = c_all.shape
    Ca = dmod_sh.shape[1]
    C6 = dmod_all.shape[1]

    def body(c_ref, ds_ref, da_ref, gw_ref, gb_ref):
        act = _silu(c_ref[...]).astype(BF16)
        gw_ref[...] = _dot_tn(act, ds_ref[...].astype(BF16))
        gb_ref[...] = jnp.sum(da_ref[...], axis=0, keepdims=True)

    return pl.pallas_call(
        body, out_shape=[jax.ShapeDtypeStruct((D, Ca), F32), jax.ShapeDtypeStruct((1, C6), F32)],
        compiler_params=_cparams(), name="ada_bwd")(c_all, dmod_sh, dmod_all)


def _rms_fwd(xv):
    r = lax.rsqrt(jnp.mean(xv * xv, axis=-1, keepdims=True) + RMS_EPS)
    return xv * r, r


def _rms_bwd(dxh, xh, r):
    return r * (dxh - xh * jnp.mean(dxh * xh, axis=-1, keepdims=True))


def _const_spec(shape):
    nd = len(shape)
    return pl.BlockSpec(shape, lambda *_: (0,) * nd)


def _seq_spec(D, bps, rev_blocks=None):
    if rev_blocks is None:
        return pl.BlockSpec((None, 1, D), lambda i: (i // bps, 0, 0))
    return pl.BlockSpec((None, 1, D), lambda i: ((rev_blocks - 1 - i) // bps, 0, 0))


def _inproj_fwd(x2, sh1, sc1, n1g, w_in, carried):
    N, D = x2.shape
    IN = w_in.shape[1]
    Bl = sh1.shape[0]
    TB = _row_block(N // Bl, 512)
    bps = (N // Bl) // TB

    def body(x_ref, sh_ref, sc_ref, g_ref, w_ref, p_ref):
        xh, _ = _rms_fwd(x_ref[...])
        h = (xh * g_ref[...]) * (1.0 + sc_ref[...]) + sh_ref[...]
        p_ref[...] = _dot(h.astype(BF16), w_ref[...]).astype(BF16)

    (p2,), extra = _call_carrying(
        body, carried, (x2, sh1, sc1, n1g, w_in), out_shape=[jax.ShapeDtypeStruct((N, IN), BF16)], grid=(N // TB,),
        in_specs=[pl.BlockSpec((TB, D), lambda i: (i, 0)), _seq_spec(D, bps), _seq_spec(D, bps),
                  _const_spec((1, D)), _const_spec((D, IN))],
        out_specs=[pl.BlockSpec((TB, IN), lambda i: (i, 0))],
        compiler_params=_cparams(dimension_semantics=("arbitrary",)), name="inproj_fwd")
    return p2, extra


def _s5_dims(Bl, L, Ds):
    G = Ds // SSM_GROUP
    GP = G * SSM_STATE
    NP = GP // LANES
    T = min(64, L // 2)
    nb = 2 if (Ds // 2) % LANES == 0 else 1
    return G, GP, NP, T, nb


def _s5_disc_math(lr, li, ldt, bt_r, bt_i):
    dt = jnp.exp(ldt)
    er = jnp.exp(lr * dt)
    lbr = er * jnp.cos(li * dt)
    lbi = er * jnp.sin(li * dt)
    den = lr * lr + li * li
    fr = ((lbr - 1.0) * lr + lbi * li) / den
    fi = (lbi * lr - (lbr - 1.0) * li) / den
    return lbr, lbi, fr[None] * bt_r - fi[None] * bt_i, fr[None] * bt_i + fi[None] * bt_r


def _s5_disc(lr, li, ldt, bt_r, bt_i):
    def body(lr_ref, li_ref, ldt_ref, br_ref, bi_ref, o0, o1, o2, o3):
        res = _s5_disc_math(lr_ref[...], li_ref[...], ldt_ref[...], br_ref[...], bi_ref[...])
        for o, v in zip((o0, o1, o2, o3), res):
            o[...] = v

    S = jax.ShapeDtypeStruct
    return pl.pallas_call(body, out_shape=[S(lr.shape, F32)] * 2 + [S(bt_r.shape, F32)] * 2, name="s5_disc")(lr, li, ldt, bt_r, bt_i)


def _s5_disc_bwd(lr, li, ldt, bt_r, bt_i, dlbr, dlbi, dbbr, dbbi):
    def body(lr_ref, li_ref, ldt_ref, br_ref, bi_ref, g0, g1, g2, g3, o0, o1, o2, o3, o4):
        _, vjp = jax.vjp(_s5_disc_math, lr_ref[...], li_ref[...], ldt_ref[...], br_ref[...], bi_ref[...])
        res = vjp((g0[...], g1[...], g2[...], g3[...]))
        for o, v in zip((o0, o1, o2, o3, o4), res):
            o[...] = v

    S = jax.ShapeDtypeStruct
    return pl.pallas_call(body, out_shape=[S(lr.shape, F32)] * 2 + [S(ldt.shape, F32)] + [S(bt_r.shape, F32)] * 2,
                          name="s5_disc_bwd")(lr, li, ldt, bt_r, bt_i, dlbr, dlbi, dbbr, dbbi)


S5_SCAN_PANELS = 4
S5_SCAN_STEPS = 4


def _panel_scan(src_r, src_i, dst_r, dst_i, lr_ref, li_ref, car_r, car_i, NP, Bl, T, TP, adjoint):
    BT = Bl * TP
    PG = min(S5_SCAN_PANELS, NP)
    CH = S5_SCAN_STEPS
    for k0 in range(0, NP, PG):
        ks = list(range(k0, k0 + PG))
        lr = [jnp.broadcast_to(lr_ref[pl.ds(k, 1), :], (Bl, LANES)) for k in ks]
        li = [jnp.broadcast_to(li_ref[pl.ds(k, 1), :], (Bl, LANES)) for k in ks]

        def trip(cc, carry):
            t0 = (T // CH - 1 - cc) * CH if adjoint else cc * CH
            ts = [t0 + (CH - 1 - s if adjoint else s) for s in range(CH)]
            idx = [[pl.ds(k * BT + t, Bl, stride=TP) for t in ts] for k in ks]
            loaded = [[(src_r[ix, :], src_i[ix, :]) for ix in idx[j]] for j in range(PG)]
            results, new_carry = [], []
            for j in range(PG):
                ar, ai = carry[j]
                res = []
                for s in range(CH):
                    br, bi = loaded[j][s]
                    if adjoint:
                        ar, ai = br + lr[j] * ar + li[j] * ai, bi + lr[j] * ai - li[j] * ar
                    else:
                        ar, ai = lr[j] * ar - li[j] * ai + br, lr[j] * ai + li[j] * ar + bi
                    res.append((ar, ai))
                results.append(res)
                new_carry.append((ar, ai))
            for j in range(PG):
                for s in range(CH):
                    dst_r[idx[j][s], :] = results[j][s][0]
                    dst_i[idx[j][s], :] = results[j][s][1]
            return tuple(new_carry)

        init = tuple((car_r[pl.ds(k * SUBLANES, Bl), :], car_i[pl.ds(k * SUBLANES, Bl), :]) for k in ks)
        fin = lax.fori_loop(0, T // CH, trip, init, unroll=2)
        for j, k in enumerate(ks):
            car_r[pl.ds(k * SUBLANES, Bl), :] = fin[j][0]
            car_i[pl.ds(k * SUBLANES, Bl), :] = fin[j][1]


def _s5_fwd(p3, Bm_r, Bm_i, Cm_r, Cm_i, lam_r, lam_i, dsk, carried):
    Bl, L, _ = p3.shape
    Ds = dsk.shape[1]
    G, GP, NP, T, nb = _s5_dims(Bl, L, Ds)
    nT = L // T
    TP = T + S5_ROW_PAD
    BT = Bl * TP
    dsb, gpb, npb = Ds // nb, GP // nb, NP // nb

    def body(u_ref, br_ref, bi_ref, cr_ref, ci_ref, lr_ref, li_ref, dsk_ref, sr_ref, si_ref, sb_r, sb_i, y_ref,
             car_r, car_i, upad, ypad, bu_r, bu_i):
        i = pl.program_id(0)

        @pl.when(i == 0)
        def _():
            car_r[...] = jnp.zeros_like(car_r)
            car_i[...] = jnp.zeros_like(car_i)
            upad[...] = jnp.zeros_like(upad)

        zpad = jnp.zeros((S5_ROW_PAD, LANES), F32)
        for k in range(NP):
            for b in range(Bl):
                sr_ref[pl.ds(k * BT + b * TP + T, S5_ROW_PAD), :] = zpad
                si_ref[pl.ds(k * BT + b * TP + T, S5_ROW_PAD), :] = zpad
        for b in range(Bl):
            upad[pl.ds(b * TP, T), :] = u_ref[b].astype(F32)
        u = upad[...]
        ub = u.astype(BF16)
        for blk in range(nb):
            ubb = ub[:, blk * dsb:(blk + 1) * dsb]
            for bu_ref, b_ref in ((bu_r, br_ref), (bu_i, bi_ref)):
                res = _dot(ubb, b_ref[blk])
                for kk in range(npb):
                    k = blk * npb + kk
                    bu_ref[pl.ds(k * BT, BT), :] = res[:, kk * LANES:(kk + 1) * LANES]

        _panel_scan(bu_r, bu_i, sr_ref, si_ref, lr_ref, li_ref, car_r, car_i, NP, Bl, T, TP, adjoint=False)
        sb_r[...] = car_r[...]
        sb_i[...] = car_i[...]

        for blk in range(nb):
            s_r = jnp.concatenate([sr_ref[pl.ds((blk * npb + kk) * BT, BT), :] for kk in range(npb)], axis=1).astype(BF16)
            s_i = jnp.concatenate([si_ref[pl.ds((blk * npb + kk) * BT, BT), :] for kk in range(npb)], axis=1).astype(BF16)
            cols = slice(blk * dsb, (blk + 1) * dsb)
            ypad[:, cols] = _dot_nt(s_r, cr_ref[blk]) + _dot_nt(s_i, ci_ref[blk]) + dsk_ref[:, cols] * u[:, cols]
        for b in range(Bl):
            y_ref[b] = ypad[pl.ds(b * TP, T), :]

    S = jax.ShapeDtypeStruct
    state = S((nT, NP * BT, LANES), F32)
    bound = S((nT, NP * SUBLANES, LANES), F32)
    sspec = pl.BlockSpec((None, NP * BT, LANES), lambda i: (i, 0, 0))
    bspec = pl.BlockSpec((None, NP * SUBLANES, LANES), lambda i: (i, 0, 0))
    return _call_carrying(
        body, carried, (p3, Bm_r, Bm_i, Cm_r, Cm_i, lam_r, lam_i, dsk),
        out_shape=[state, state, bound, bound, S((Bl, L, Ds), F32)], grid=(nT,),
        in_specs=[pl.BlockSpec((Bl, T, Ds), lambda i: (0, i, 0)),
                  _const_spec((nb, dsb, gpb)), _const_spec((nb, dsb, gpb)),
                  _const_spec((nb, dsb, gpb)), _const_spec((nb, dsb, gpb)),
                  _const_spec((NP, LANES)), _const_spec((NP, LANES)), _const_spec((1, Ds))],
        out_specs=[sspec, sspec, bspec, bspec, pl.BlockSpec((Bl, T, Ds), lambda i: (0, i, 0))],
        scratch_shapes=[pltpu.VMEM((NP * SUBLANES, LANES), F32)] * 2 + [pltpu.VMEM((BT, Ds), F32)] * 2
        + [pltpu.VMEM((NP * BT, LANES), F32)] * 2,
        compiler_params=_cparams(dimension_semantics=("arbitrary",)), name="s5_fwd")


def _s5_bwd(dy3, p3, dp3, Sr, Si, Sb_r, Sb_i, Bm_r, Bm_i, Cm_r, Cm_i, lam_r, lam_i, dsk, carried):
    Bl, L, Ds = dy3.shape
    G, GP, NP, T, nb = _s5_dims(Bl, L, Ds)
    nT = L // T
    TP = T + S5_ROW_PAD
    BT = Bl * TP
    dsb, gpb, npb = Ds // nb, GP // nb, NP // nb

    def body(dy_ref, u_ref, dp_ref, sr_ref, si_ref, sbr_ref, sbi_ref, br_ref, bi_ref, cr_ref, ci_ref, lr_ref, li_ref, dsk_ref,
             du_ref, dbr_ref, dbi_ref, dcr_ref, dci_ref, dlr_ref, dli_ref, ddsk_ref,
             a_r, a_i, car_r, car_i, acc_r, acc_i, dypad, upad, dupad, q_r, q_i):
        i = pl.program_id(0)

        @pl.when(i == 0)
        def _():
            for ref in (car_r, car_i, acc_r, acc_i, dbr_ref, dbi_ref, dcr_ref, dci_ref, ddsk_ref, dypad, upad, a_r, a_i):
                ref[...] = jnp.zeros_like(ref)

        for b in range(Bl):
            dypad[pl.ds(b * TP, T), :] = dy_ref[b]
            upad[pl.ds(b * TP, T), :] = u_ref[b].astype(F32)
        dy = dypad[...]
        dyb = dy.astype(BF16)
        u = upad[...]
        ub = u.astype(BF16)
        for blk in range(nb):
            dyb_b = dyb[:, blk * dsb:(blk + 1) * dsb]
            for q_ref, c_ref in ((q_r, cr_ref), (q_i, ci_ref)):
                res = _dot(dyb_b, c_ref[blk])
                for kk in range(npb):
                    q_ref[pl.ds((blk * npb + kk) * BT, BT), :] = res[:, kk * LANES:(kk + 1) * LANES]

        _panel_scan(q_r, q_i, a_r, a_i, lr_ref, li_ref, car_r, car_i, NP, Bl, T, TP, adjoint=True)

        first_block = (i == nT - 1)
        for k in range(NP):
            rows = pl.ds(k * BT, BT)
            av_r, av_i = a_r[rows, :], a_i[rows, :]
            sp_r = pltpu.roll(sr_ref[rows, :], 1, 0)
            sp_i = pltpu.roll(si_ref[rows, :], 1, 0)
            acc = pl.ds(k * SUBLANES, SUBLANES)
            acc_r[acc, :] += jnp.sum((av_r * sp_r + av_i * sp_i).reshape(BT // SUBLANES, SUBLANES, LANES), axis=0)
            acc_i[acc, :] += jnp.sum((av_i * sp_r - av_r * sp_i).reshape(BT // SUBLANES, SUBLANES, LANES), axis=0)
            t0 = pl.ds(k * BT, Bl, stride=TP)
            a0_r, a0_i = a_r[t0, :], a_i[t0, :]
            brow = pl.ds(k * SUBLANES, Bl)
            sb_pr = jnp.where(first_block, 0.0, sbr_ref[brow, :])
            sb_pi = jnp.where(first_block, 0.0, sbi_ref[brow, :])
            acc_r[brow, :] += a0_r * sb_pr + a0_i * sb_pi
            acc_i[brow, :] += a0_i * sb_pr - a0_r * sb_pi

        ddsk_ref[...] += jnp.sum(dy * u, axis=0, keepdims=True)
        for blk in range(nb):
            cols = slice(blk * dsb, (blk + 1) * dsb)
            rows = [pl.ds((blk * npb + kk) * BT, BT) for kk in range(npb)]
            av_r = jnp.concatenate([a_r[r, :] for r in rows], axis=1).astype(BF16)
            av_i = jnp.concatenate([a_i[r, :] for r in rows], axis=1).astype(BF16)
            dupad[:, cols] = _dot_nt(av_r, br_ref[blk]) + _dot_nt(av_i, bi_ref[blk]) + dy[:, cols] * dsk_ref[:, cols]
            dbr_ref[blk] += _dot_tn(ub[:, cols], av_r)
            dbi_ref[blk] += _dot_tn(ub[:, cols], av_i)
            sv_r = jnp.concatenate([sr_ref[r, :] for r in rows], axis=1).astype(BF16)
            sv_i = jnp.concatenate([si_ref[r, :] for r in rows], axis=1).astype(BF16)
            dcr_ref[blk] += _dot_tn(dyb[:, cols], sv_r)
            dci_ref[blk] += _dot_tn(dyb[:, cols], sv_i)
        for b in range(Bl):
            du_ref[b] = dupad[pl.ds(b * TP, T), :].astype(BF16)

        @pl.when(i == nT - 1)
        def _():
            for k in range(NP):
                dlr_ref[pl.ds(k, 1), :] = jnp.sum(acc_r[pl.ds(k * SUBLANES, SUBLANES), :], axis=0, keepdims=True)
                dli_ref[pl.ds(k, 1), :] = jnp.sum(acc_i[pl.ds(k * SUBLANES, SUBLANES), :], axis=0, keepdims=True)

    S = jax.ShapeDtypeStruct
    rev = lambda i: nT - 1 - i
    sspec = pl.BlockSpec((None, NP * BT, LANES), lambda i: (rev(i), 0, 0))
    bspec = pl.BlockSpec((None, NP * SUBLANES, LANES), lambda i: (jnp.maximum(rev(i) - 1, 0), 0, 0))
    tspec = pl.BlockSpec((Bl, T, Ds), lambda i: (0, rev(i), 0))
    return _call_carrying(
        body, carried, (dy3, p3, dp3, Sr, Si, Sb_r, Sb_i, Bm_r, Bm_i, Cm_r, Cm_i, lam_r, lam_i, dsk),
        out_shape=[S(dp3.shape, dp3.dtype), S((nb, dsb, gpb), F32), S((nb, dsb, gpb), F32),
                   S((nb, dsb, gpb), F32), S((nb, dsb, gpb), F32), S((NP, LANES), F32), S((NP, LANES), F32), S((1, Ds), F32)],
        grid=(nT,),
        in_specs=[tspec, tspec, ANY, sspec, sspec, bspec, bspec,
                  _const_spec((nb, dsb, gpb)), _const_spec((nb, dsb, gpb)),
                  _const_spec((nb, dsb, gpb)), _const_spec((nb, dsb, gpb)),
                  _const_spec((NP, LANES)), _const_spec((NP, LANES)), _const_spec((1, Ds))],
        out_specs=[tspec, _const_spec((nb, dsb, gpb)), _const_spec((nb, dsb, gpb)),
                   _const_spec((nb, dsb, gpb)), _const_spec((nb, dsb, gpb)),
                   _const_spec((NP, LANES)), _const_spec((NP, LANES)), _const_spec((1, Ds))],
        aliases={2: 0},
        scratch_shapes=[pltpu.VMEM((NP * BT, LANES), F32)] * 2 + [pltpu.VMEM((NP * SUBLANES, LANES), F32)] * 4
        + [pltpu.VMEM((BT, Ds), F32)] * 3 + [pltpu.VMEM((NP * BT, LANES), F32)] * 2,
        compiler_params=_cparams(dimension_semantics=("arbitrary",)), name="s5_bwd")


def _mix_values(ylin, cb, cc, cx, gs, gc, halo_v, wglu, bglu, cw, wps, wpc, wout):
    yg, gelu_vjp = jax.vjp(jax.nn.gelu, ylin)
    sz = jax.nn.sigmoid(_dot(yg.astype(BF16), wglu) + bglu)
    ys = yg * sz
    v = cc * cx
    rows = lax.broadcasted_iota(jnp.int32, v.shape, 0)
    h6 = halo_v[HALO_ROWS - 2:HALO_ROWS - 1, :]
    h7 = halo_v[HALO_ROWS - 1:HALO_ROWS, :]
    v1 = jnp.where(rows == 0, h7, pltpu.roll(v, 1, 0))
    v2 = jnp.where(rows == 0, h6, jnp.where(rows == 1, h7, pltpu.roll(v, 2, 0)))
    cv = cw[0:1, :] * v2 + cw[1:2, :] * v1 + cw[2:3, :] * v
    yc = cb * cv
    ps = _dot(ys.astype(BF16), wps)
    pc = _dot(yc.astype(BF16), wpc)
    sgs = jax.nn.sigmoid(gs)
    sgc = jax.nn.sigmoid(gc)
    merged = sgs * ps + sgc * pc
    mo = _dot(merged.astype(BF16), wout)
    return dict(yg=yg, gelu_vjp=gelu_vjp, sz=sz, ys=ys, v=v, v1=v1, v2=v2, cv=cv, yc=yc, ps=ps, pc=pc,
                sgs=sgs, sgc=sgc, merged=merged, mo=mo)


def _mix_in_specs(TB, D, Ds, Dc, bps, blk):
    hb = TB // HALO_ROWS
    halo = lambda col: pl.BlockSpec((HALO_ROWS, Dc), lambda i: (jnp.maximum(blk(i) * hb - 1, 0), col))
    return [pl.BlockSpec((TB, Dc), lambda i: (blk(i), 1)), pl.BlockSpec((TB, Dc), lambda i: (blk(i), 2)),
            pl.BlockSpec((TB, Dc), lambda i: (blk(i), 3)), pl.BlockSpec((TB, D), lambda i: (blk(i), 2)),
            pl.BlockSpec((TB, D), lambda i: (blk(i), 3)), halo(2), halo(3),
            pl.BlockSpec((TB, Ds), lambda i: (blk(i), 0))]


def _mix_fwd(p2, ylin2, x2, g1, wglu, bglu, cw, wps, wpc, wout):
    N, D = x2.shape
    Ds = ylin2.shape[1]
    Dc = Ds
    Bl = g1.shape[0]
    TB = _row_block(N // Bl, 512)
    bps = (N // Bl) // TB

    def body(cb_ref, cc_ref, cx_ref, gs_ref, gc_ref, hcc_ref, hcx_ref, yl_ref, x_ref, g1_ref,
             wglu_ref, bglu_ref, cw_ref, wps_ref, wpc_ref, wout_ref, x1_ref):
        i = pl.program_id(0)
        f32 = lambda ref: ref[...].astype(F32)
        halo_v = jnp.where(i % bps == 0, 0.0, f32(hcc_ref) * f32(hcx_ref))
        f = _mix_values(yl_ref[...], f32(cb_ref), f32(cc_ref), f32(cx_ref), f32(gs_ref), f32(gc_ref), halo_v,
                        wglu_ref[...], bglu_ref[...], cw_ref[...], wps_ref[...], wpc_ref[...], wout_ref[...])
        x1_ref[...] = x_ref[...] + g1_ref[...] * f["mo"]

    return pl.pallas_call(
        body, out_shape=jax.ShapeDtypeStruct((N, D), F32), grid=(N // TB,),
        in_specs=_mix_in_specs(TB, D, Ds, Dc, bps, lambda i: i) + [
            pl.BlockSpec((TB, D), lambda i: (i, 0)), _seq_spec(D, bps),
            _const_spec((Ds, Ds)), _const_spec((1, Ds)), _const_spec((SUBLANES, Dc)),
            _const_spec((Ds, D)), _const_spec((Dc, D)), _const_spec((D, D))],
        out_specs=pl.BlockSpec((TB, D), lambda i: (i, 0)),
        compiler_params=_cparams(), name="mix_fwd",
    )(p2, p2, p2, p2, p2, p2, p2, ylin2, x2, g1, wglu, bglu, cw, wps, wpc, wout)


def _mix_bwd(p2, ylin2, dx1, g1, wglu, bglu, cw, wps, wpc, wout, carried):
    N, D = dx1.shape
    Ds = ylin2.shape[1]
    Dc = Ds
    IN = p2.shape[1]
    Bl = g1.shape[0]
    TB = _row_block(N // Bl)
    bps = (N // Bl) // TB
    nblk = N // TB
    rev = lambda i: nblk - 1 - i

    def body(cb_ref, cc_ref, cx_ref, gs_ref, gc_ref, hcc_ref, hcx_ref, yl_ref, dx1_ref, g1_ref,
             wglu_ref, bglu_ref, cw_ref, wps_ref, wpc_ref, wout_ref,
             dyl_ref, dp_ref, gwout_ref, gwps_ref, gwpc_ref, gwglu_ref, gbglu_ref, gcw_ref, dg1_ref, nxt):
        i = pl.program_id(0)
        blk = rev(i)

        @pl.when(i == 0)
        def _():
            for ref in (gwout_ref, gwps_ref, gwpc_ref, gwglu_ref, gbglu_ref, gcw_ref):
                ref[...] = jnp.zeros_like(ref)

        @pl.when(i % bps == 0)
        def _():
            nxt[...] = jnp.zeros_like(nxt)
            dg1_ref[...] = jnp.zeros_like(dg1_ref)

        f32 = lambda ref: ref[...].astype(F32)
        cb, cc, cx = f32(cb_ref), f32(cc_ref), f32(cx_ref)
        halo_v = jnp.where(blk % bps == 0, 0.0, f32(hcc_ref) * f32(hcx_ref))
        wglu, wps, wpc, wout, cw = wglu_ref[...], wps_ref[...], wpc_ref[...], wout_ref[...], cw_ref[...]
        f = _mix_values(yl_ref[...], cb, cc, cx, f32(gs_ref), f32(gc_ref), halo_v, wglu, bglu_ref[...], cw, wps, wpc, wout)

        dx1v = dx1_ref[...]
        dg1_ref[...] += jnp.sum(dx1v * f["mo"], axis=0, keepdims=True)
        dmo = (g1_ref[...] * dx1v).astype(BF16)
        gwout_ref[...] += _dot_tn(f["merged"].astype(BF16), dmo)
        dmerged = _dot_nt(dmo, wout)
        dps = dmerged * f["sgs"]
        dpc = dmerged * f["sgc"]
        dgs = dmerged * f["ps"] * f["sgs"] * (1.0 - f["sgs"])
        dgc = dmerged * f["pc"] * f["sgc"] * (1.0 - f["sgc"])
        dpsb, dpcb = dps.astype(BF16), dpc.astype(BF16)
        gwps_ref[...] += _dot_tn(f["ys"].astype(BF16), dpsb)
        gwpc_ref[...] += _dot_tn(f["yc"].astype(BF16), dpcb)
        dys = _dot_nt(dpsb, wps)
        dyc = _dot_nt(dpcb, wpc)

        dcb = dyc * f["cv"]
        dcv = dyc * cb
        rows = lax.broadcasted_iota(jnp.int32, dcv.shape, 0)
        n0, n1 = nxt[0:1, :], nxt[1:2, :]
        d1 = jnp.where(rows == TB - 1, n0, pltpu.roll(dcv, TB - 1, 0))
        d2 = jnp.where(rows == TB - 2, n0, jnp.where(rows == TB - 1, n1, pltpu.roll(dcv, TB - 2, 0)))
        dv = cw[2:3, :] * dcv + cw[1:2, :] * d1 + cw[0:1, :] * d2
        nxt[0:2, :] = dcv[0:2, :]
        gcw_ref[0:1, :] += jnp.sum(dcv * f["v2"], axis=0, keepdims=True)
        gcw_ref[1:2, :] += jnp.sum(dcv * f["v1"], axis=0, keepdims=True)
        gcw_ref[2:3, :] += jnp.sum(dcv * f["v"], axis=0, keepdims=True)

        dz = dys * f["yg"] * f["sz"] * (1.0 - f["sz"])
        dzb = dz.astype(BF16)
        gwglu_ref[...] += _dot_tn(f["yg"].astype(BF16), dzb)
        gbglu_ref[...] += jnp.sum(dz, axis=0, keepdims=True)
        dyg = dys * f["sz"] + _dot_nt(dzb, wglu)
        dyl_ref[...] = f["gelu_vjp"](dyg)[0]

        dp_ref[:, Ds:Ds + Dc] = dcb.astype(BF16)
        dp_ref[:, Ds + Dc:Ds + 2 * Dc] = (dv * cx).astype(BF16)
        dp_ref[:, Ds + 2 * Dc:Ds + 3 * Dc] = (dv * cc).astype(BF16)
        dp_ref[:, Ds + 3 * Dc:Ds + 3 * Dc + D] = dgs.astype(BF16)
        dp_ref[:, Ds + 3 * Dc + D:IN] = dgc.astype(BF16)

    S = jax.ShapeDtypeStruct
    return _call_carrying(
        body, carried, (p2, p2, p2, p2, p2, p2, p2, ylin2, dx1, g1, wglu, bglu, cw, wps, wpc, wout),
        out_shape=[S((N, Ds), F32), S((N, IN), BF16), S((D, D), F32), S((Ds, D), F32), S((Dc, D), F32),
                   S((Ds, Ds), F32), S((1, Ds), F32), S((SUBLANES, Dc), F32), S((Bl, 1, D), F32)],
        grid=(nblk,),
        in_specs=_mix_in_specs(TB, D, Ds, Dc, bps, rev) + [
            pl.BlockSpec((TB, D), lambda i: (rev(i), 0)), _seq_spec(D, bps, nblk),
            _const_spec((Ds, Ds)), _const_spec((1, Ds)), _const_spec((SUBLANES, Dc)),
            _const_spec((Ds, D)), _const_spec((Dc, D)), _const_spec((D, D))],
        out_specs=[pl.BlockSpec((TB, Ds), lambda i: (rev(i), 0)), pl.BlockSpec((TB, IN), lambda i: (rev(i), 0)),
                   _const_spec((D, D)), _const_spec((Ds, D)), _const_spec((Dc, D)), _const_spec((Ds, Ds)),
                   _const_spec((1, Ds)), _const_spec((SUBLANES, Dc)), _seq_spec(D, bps, nblk)],
        scratch_shapes=[pltpu.VMEM((SUBLANES, Dc), F32)],
        compiler_params=_cparams(dimension_semantics=("arbitrary",)), name="mix_bwd")


def _mlp_fwd_bwd(x1, tgt, sh2, sc2, g2, n2g, fg, w1, w2):
    N, D = x1.shape
    Dff = w1.shape[1]
    Bl = sh2.shape[0]
    TB = _row_block(N // Bl)
    bps = (N // Bl) // TB

    def body(x1_ref, t_ref, sh_ref, sc_ref, g2_ref, n2_ref, fg_ref, w1_ref, w2_ref,
             dx1_ref, h2_ref, da_ref, sq_ref, df_ref, loss_ref, gfg_ref, gn2_ref, dsh_ref, dsc_ref, dg2_ref):
        i = pl.program_id(0)

        @pl.when(i == 0)
        def _():
            for ref in (loss_ref, gfg_ref, gn2_ref):
                ref[...] = jnp.zeros_like(ref)

        @pl.when(i % bps == 0)
        def _():
            for ref in (dsh_ref, dsc_ref, dg2_ref):
                ref[...] = jnp.zeros_like(ref)

        x1v = x1_ref[...]
        sc, g2v, n2 = sc_ref[...], g2_ref[...], n2_ref[...]
        xh2, r2 = _rms_fwd(x1v)
        xn2 = xh2 * n2
        h2 = (xn2 * (1.0 + sc) + sh_ref[...]).astype(BF16)
        a = _dot(h2, w1_ref[...])
        ra = jnp.maximum(a, 0.0)
        sq = (ra * ra).astype(BF16)
        fv = _dot(sq, w2_ref[...])
        x2 = x1v + g2v * fv
        xh3, r3 = _rms_fwd(x2)
        err = xh3 * fg_ref[...] - t_ref[...]
        loss_ref[...] += 0.5 * jnp.sum(jnp.mean(err * err, axis=-1, keepdims=True), axis=0, keepdims=True)
        dy = err * (1.0 / D)
        gfg_ref[...] += jnp.sum(dy * xh3, axis=0, keepdims=True)
        dx2 = _rms_bwd(dy * fg_ref[...], xh3, r3)
        dg2_ref[...] += jnp.sum(dx2 * fv, axis=0, keepdims=True)
        df = (g2v * dx2).astype(BF16)
        dsq = _dot_nt(df, w2_ref[...])
        da = (2.0 * ra * dsq).astype(BF16)
        dh2 = _dot_nt(da, w1_ref[...])
        dsh_ref[...] += jnp.sum(dh2, axis=0, keepdims=True)
        dsc_ref[...] += jnp.sum(dh2 * xn2, axis=0, keepdims=True)
        dxn2 = dh2 * (1.0 + sc)
        gn2_ref[...] += jnp.sum(dxn2 * xh2, axis=0, keepdims=True)
        dx1_ref[...] = dx2 + _rms_bwd(dxn2 * n2, xh2, r2)
        h2_ref[...] = h2
        da_ref[...] = da
        sq_ref[...] = sq
        df_ref[...] = df

    S = jax.ShapeDtypeStruct
    row = lambda w: pl.BlockSpec((TB, w), lambda i: (i, 0))
    return pl.pallas_call(
        body,
        out_shape=[S((N, D), F32), S((N, D), BF16), S((N, Dff), BF16), S((N, Dff), BF16), S((N, D), BF16),
                   S((1, 1), F32), S((1, D), F32), S((1, D), F32), S((Bl, 1, D), F32), S((Bl, 1, D), F32), S((Bl, 1, D), F32)],
        grid=(N // TB,),
        in_specs=[row(D), row(D), _seq_spec(D, bps), _seq_spec(D, bps), _seq_spec(D, bps),
                  _const_spec((1, D)), _const_spec((1, D)), _const_spec((D, Dff)), _const_spec((Dff, D))],
        out_specs=[row(D), row(D), row(Dff), row(Dff), row(D), _const_spec((1, 1)), _const_spec((1, D)), _const_spec((1, D)),
                   _seq_spec(D, bps), _seq_spec(D, bps), _seq_spec(D, bps)],
        compiler_params=_cparams(dimension_semantics=("arbitrary",)), name="mlp_fwd_bwd",
    )(x1, tgt, sh2, sc2, g2, n2g, fg, w1, w2)


_NO_EXCHANGE = _Carried((), (), {}, (), lambda ins, outs, sems: ((lambda: None), (lambda: None)))


def _grad_w(a, b, name, carried=_NO_EXCHANGE):
    N, K1 = a.shape
    K2 = b.shape[1]
    t1 = 1024 if K1 % 1024 == 0 else K1
    t2 = 1024 if K2 % 1024 == 0 else K2
    tn = 4096 if N % 4096 == 0 else N

    def body(a_ref, b_ref, o_ref):
        @pl.when(pl.program_id(2) == 0)
        def _():
            o_ref[...] = jnp.zeros_like(o_ref)

        o_ref[...] += _dot_tn(a_ref[...], b_ref[...])

    (g,), extra = _call_carrying(
        body, carried, (a, b), out_shape=[jax.ShapeDtypeStruct((K1, K2), F32)], grid=(K1 // t1, K2 // t2, N // tn),
        in_specs=[pl.BlockSpec((tn, t1), lambda i, j, k: (k, i)), pl.BlockSpec((tn, t2), lambda i, j, k: (k, j))],
        out_specs=[pl.BlockSpec((t1, t2), lambda i, j, k: (i, j))],
        compiler_params=_cparams(dimension_semantics=("arbitrary", "arbitrary", "arbitrary")), name=name)
    return g, extra


def _inproj_bwd(x2, dx1, dp, sh1, sc1, n1g, w_in, carried):
    N, D = x2.shape
    IN = w_in.shape[1]
    Bl = sh1.shape[0]
    TB = _row_block(N // Bl, 512)
    bps = (N // Bl) // TB

    def body(x_ref, dx1_ref, dp_ref, sh_ref, sc_ref, g_ref, w_ref, gx_ref, h_ref, gn1_ref, dsh_ref, dsc_ref):
        i = pl.program_id(0)

        @pl.when(i == 0)
        def _():
            gn1_ref[...] = jnp.zeros_like(gn1_ref)

        @pl.when(i % bps == 0)
        def _():
            dsh_ref[...] = jnp.zeros_like(dsh_ref)
            dsc_ref[...] = jnp.zeros_like(dsc_ref)

        sc, n1 = sc_ref[...], g_ref[...]
        xh, r = _rms_fwd(x_ref[...])
        xn = xh * n1
        h_ref[...] = (xn * (1.0 + sc) + sh_ref[...]).astype(BF16)
        dh = _dot_nt(dp_ref[...], w_ref[...])
        dsh_ref[...] += jnp.sum(dh, axis=0, keepdims=True)
        dsc_ref[...] += jnp.sum(dh * xn, axis=0, keepdims=True)
        dxn = dh * (1.0 + sc)
        gn1_ref[...] += jnp.sum(dxn * xh, axis=0, keepdims=True)
        gx_ref[...] = dx1_ref[...] + _rms_bwd(dxn * n1, xh, r)

    S = jax.ShapeDtypeStruct
    row = lambda w: pl.BlockSpec((TB, w), lambda i: (i, 0))
    return _call_carrying(
        body, carried, (x2, dx1, dp, sh1, sc1, n1g, w_in),
        out_shape=[S((N, D), F32), S((N, D), BF16), S((1, D), F32), S((Bl, 1, D), F32), S((Bl, 1, D), F32)],
        grid=(N // TB,),
        in_specs=[row(D), row(D), row(IN), _seq_spec(D, bps), _seq_spec(D, bps), _const_spec((1, D)), _const_spec((D, IN))],
        out_specs=[row(D), row(D), _const_spec((1, D)), _seq_spec(D, bps), _seq_spec(D, bps)],
        compiler_params=_cparams(dimension_semantics=("arbitrary",)), name="inproj_bwd")


def _diag_mask(gb, a, b):
    rows = lax.broadcasted_iota(jnp.int32, (gb * a, gb * b), 0) // a
    cols = lax.broadcasted_iota(jnp.int32, (gb * a, gb * b), 1) // b
    return (rows == cols).astype(F32)


def _diag_blocks_from_groups(m, nb):
    G, a, b = m.shape
    gb = G // nb
    return jnp.tile(m.reshape(nb, gb * a, b), (1, 1, gb)) * _diag_mask(gb, a, b)[None]


def _groups_from_diag_blocks(d, G, a, b):
    nb = d.shape[0]
    gb = G // nb
    picked = (d * _diag_mask(gb, a, b)[None]).reshape(nb, gb * a, gb, b)
    return jnp.sum(picked, axis=2).reshape(G, a, b)


def _pad_rows(v, rows):
    return jnp.concatenate([v, jnp.zeros((rows - v.shape[0],) + v.shape[1:], v.dtype)], axis=0)


def _pack(vs):
    flat = jnp.concatenate([v.reshape(-1) for v in vs])
    n = flat.shape[0]
    tile = SUBLANES * LANES
    npad = -(-n // tile) * tile
    flat = jnp.concatenate([flat, jnp.zeros((npad - n,), flat.dtype)])
    return flat.reshape(npad // LANES, LANES)


def _unpack(packed, shapes):
    flat = packed.reshape(-1)
    out, off = [], 0
    for s in shapes:
        n = 1
        for d in s:
            n *= d
        out.append(flat[off:off + n].reshape(s))
        off += n
    return out


def kernel(x, c, norm1_g, norm2_g, w_ada, b_ada, w_in, lam_re, lam_im, log_dt, b_re, b_im, c_re, c_im, d_skip, w_glu, b_glu, conv_w, w_proj_ssm, w_proj_conv, w_out, w_ff1, w_ff2, final_g, loss_target, m_norm1_g, m_norm2_g, m_w_ada, m_b_ada, m_w_in, m_lam_re, m_lam_im, m_log_dt, m_b_re, m_b_im, m_c_re, m_c_im, m_d_skip, m_w_glu, m_b_glu, m_conv_w, m_w_proj_ssm, m_w_proj_conv, m_w_out, m_w_ff1, m_w_ff2, m_final_g, v_norm1_g, v_norm2_g, v_w_ada, v_b_ada, v_w_in, v_lam_re, v_lam_im, v_log_dt, v_b_re, v_b_im, v_c_re, v_c_im, v_d_skip, v_w_glu, v_b_glu, v_conv_w, v_w_proj_ssm, v_w_proj_conv, v_w_out, v_w_ff1, v_w_ff2, v_final_g):
    Bl, L, D = x.shape
    N = Bl * L
    Ds = Dc = D // 2
    G, H, P = Ds // SSM_GROUP, SSM_GROUP, SSM_STATE
    GP = G * P
    NP = GP // LANES
    nb = _s5_dims(Bl, L, Ds)[4]
    IN = Ds + 3 * Dc + 2 * D
    ax, ay, ac = _mesh_pos()
    q = 2 * ax + ay
    dev = 2 * q + ac

    big_names = ["w_in", "w_ff1", "w_ff2", "w_out", "w_proj_ssm", "w_proj_conv", "w_glu"]
    big_w = dict(w_in=w_in[0], w_ff1=w_ff1[0], w_ff2=w_ff2[0], w_out=w_out[0],
                 w_proj_ssm=w_proj_ssm[0], w_proj_conv=w_proj_conv[0], w_glu=w_glu[0])
    big_axis = dict(w_in=1, w_ff1=1, w_ff2=0, w_out=0, w_proj_ssm=1, w_proj_conv=1, w_glu=0)
    axes = [big_axis[k] for k in big_names]
    shard_shapes = [big_w[k].shape for k in big_names]
    pos = jnp.stack([q, ac]).astype(jnp.int32)
    own_only = dict(zip(big_names, _cast_into_full([big_w[k] for k in big_names], axes, pos, "cast_weights")))
    Dcs = conv_w.shape[2]
    first, (w_in_full,) = _allgather8(_pack([c, conv_w[0]]), "allgather_c_w_in",
                                      _carry_allgather([own_only["w_in"]], [big_axis["w_in"]], [big_w["w_in"].shape]))
    full = {"w_in": w_in_full}
    first = first.reshape(N_DEV, -1)
    c_all = first[:, :Bl * D].reshape(N_DEV * Bl, D)
    cw = first[0::2, Bl * D:Bl * D + 3 * Dcs].reshape(N_CHIPS, 3, Dcs).transpose(1, 0, 2).reshape(3, Dc)
    cw8 = _pad_rows(cw, SUBLANES)
    Ca = w_ada.shape[2]
    b_ada_sh = lax.dynamic_slice_in_dim(b_ada, q * Ca, Ca, axis=1)
    mod_part = _ada_fwd(c_all, w_ada[0], b_ada_sh)
    mod_g = _allgather8(mod_part, "allgather_mod")
    mod_all = mod_g[0::2].transpose(1, 0, 2).reshape(N_DEV * Bl, N_CHIPS * Ca)
    mod = lax.dynamic_slice_in_dim(mod_all, dev * Bl, Bl, axis=0)
    sh1, sc1, g1, sh2, sc2, g2 = [mod[:, k * D:(k + 1) * D].reshape(Bl, 1, D) for k in range(6)]

    ldt_c = log_dt[0].reshape(G, 1)
    bt_r = b_re[0].transpose(2, 0, 1)
    bt_i = b_im[0].transpose(2, 0, 1)
    lbr, lbi, bbt_r, bbt_i = _s5_disc(lam_re[0], lam_im[0], ldt_c, bt_r, bt_i)
    lam_r_p = lbr.reshape(NP, LANES)
    lam_i_p = lbi.reshape(NP, LANES)
    Bm_r = _diag_blocks_from_groups(bbt_r.transpose(1, 0, 2), nb).astype(BF16)
    Bm_i = _diag_blocks_from_groups(bbt_i.transpose(1, 0, 2), nb).astype(BF16)
    Cm_r = _diag_blocks_from_groups(c_re[0], nb).astype(BF16)
    Cm_i = _diag_blocks_from_groups(-c_im[0], nb).astype(BF16)

    x2 = x.reshape(N, D)
    mixer_w = ["w_out", "w_proj_ssm", "w_proj_conv", "w_glu"]
    mlp_w = ["w_ff1", "w_ff2"]
    layout = lambda ks: ([big_axis[k] for k in ks], [big_w[k].shape for k in ks])
    gather = lambda ks: _carry_allgather([own_only[k] for k in ks], *layout(ks))
    p2, gathered = _inproj_fwd(x2, sh1, sc1, norm1_g, full["w_in"], gather(mixer_w))
    full.update(zip(mixer_w, gathered))
    p3 = p2.reshape(Bl, L, IN)
    (Sr, Si, Sb_r, Sb_i, ylin3), gathered = _s5_fwd(p3, Bm_r, Bm_i, Cm_r, Cm_i, lam_r_p, lam_i_p, d_skip, gather(mlp_w))
    full.update(zip(mlp_w, gathered))
    ylin2 = ylin3.reshape(N, Ds)
    mix_w = (full["w_glu"], b_glu, cw8, full["w_proj_ssm"], full["w_proj_conv"], full["w_out"])
    x1 = _mix_fwd(p2, ylin2, x2, g1, *mix_w)

    (dx1, h2b, dab, sqb, dfb, loss_p, g_fg, g_n2, dsh2, dsc2, dg2) = _mlp_fwd_bwd(
        x1, loss_target.reshape(N, D), sh2, sc2, g2, norm2_g, final_g.reshape(1, D), full["w_ff1"], full["w_ff2"])
    g_full = {"w_ff1": _grad_w(h2b, dab, "grad_w_ff1")[0], "w_ff2": _grad_w(sqb, dfb, "grad_w_ff2")[0]}

    exchange = lambda ks: _carry_sibling_exchange([g_full[k] for k in ks], *layout(ks))
    presum = lambda ks, theirs: _presum([g_full[k] for k in ks], list(theirs), *layout(ks), pos, "presum_" + ks[0])
    chip_sum = lambda ks, parts, recv: _sum_chips(list(parts), list(recv), pos, "sum_" + ks[0])

    (dyl2, dp2, gw_out, gw_ps, gw_pc, gw_glu, gb_glu, gcw8, dg1), theirs_mlp = _mix_bwd(
        p2, ylin2, dx1, g1, *mix_w, exchange(mlp_w))
    g_full.update(w_out=gw_out, w_proj_ssm=gw_ps, w_proj_conv=gw_pc, w_glu=gw_glu)
    parts_mlp = presum(mlp_w, theirs_mlp)
    (dp3, dBm_r, dBm_i, dCm_r, dCm_i, dlam_r_p, dlam_i_p, g_dsk), extra = _s5_bwd(
        dyl2.reshape(Bl, L, Ds), p3, dp2.reshape(Bl, L, IN), Sr, Si, Sb_r, Sb_i, Bm_r, Bm_i, Cm_r, Cm_i, lam_r_p, lam_i_p,
        d_skip, _carry_join(_carry_chip_scatter(parts_mlp), exchange(mixer_w)))
    recv_mlp, theirs_mix = extra[:len(mlp_w)], extra[len(mlp_w):]
    halves_mlp = chip_sum(mlp_w, parts_mlp, recv_mlp)
    parts_mix = presum(mixer_w, theirs_mix)
    dp_all = dp3.reshape(N, IN)
    (grad_x2, hb, g_n1, dsh1, dsc1), _ = _inproj_bwd(x2, dx1, dp_all, sh1, sc1, norm1_g, full["w_in"], _NO_EXCHANGE)

    dbbt_r = _groups_from_diag_blocks(dBm_r, G, H, P).transpose(1, 0, 2)
    dbbt_i = _groups_from_diag_blocks(dBm_i, G, H, P).transpose(1, 0, 2)
    dc_re = _groups_from_diag_blocks(dCm_r, G, H, P)
    dc_im = -_groups_from_diag_blocks(dCm_i, G, H, P)
    dmod = jnp.concatenate([dsh1, dsc1, dg1, dsh2, dsc2, dg2], axis=-1).reshape(Bl, 6 * D)
    small = [g_n1, g_n2, g_fg, g_dsk, gb_glu, gcw8[:3], dlam_r_p, dlam_i_p, dbbt_r, dbbt_i, dc_re, dc_im]
    small_shapes = [v.shape for v in small]
    n_small = sum(int(v.size) for v in small)
    small_slots = _place_in_slot(_pack(small + [dmod]), jnp.reshape(dev, (1,)).astype(jnp.int32), "place_small")

    g_full["w_in"], extra = _grad_w(
        hb, dp_all, "grad_w_in",
        _carry_join(_carry_join(_carry_sibling_share(halves_mlp), _carry_chip_scatter(parts_mix)), _carry_allgather8(small_slots)))
    reduced = dict(zip(mlp_w, extra[:len(mlp_w)]))
    halves_mix = chip_sum(mixer_w, parts_mix, extra[len(mlp_w):len(mlp_w) + len(mixer_w)])
    gathered = extra[-1]
    theirs_in = _run_carried(exchange(["w_in"]), "rs_exchange_w_in")
    parts_in = presum(["w_in"], theirs_in)
    recv_in = _run_carried(_carry_chip_scatter(parts_in), "rs_scatter_w_in")
    halves_in = chip_sum(["w_in"], parts_in, recv_in)
    reduced.update(zip(mixer_w + ["w_in"], _run_carried(_carry_sibling_share(halves_mix + halves_in), "rs_share_rest")))

    red = _unpack(_sum_devices(gathered, "sum_small"), small_shapes)
    (r_n1, r_n2, r_fg, r_dsk, r_bglu, r_cw, r_dlr, r_dli, r_dbr, r_dbi, r_cre, r_cim) = red
    dmod_all = gathered.reshape(N_DEV, -1)[:, n_small:n_small + Bl * 6 * D].reshape(N_DEV * Bl, 6 * D)
    gw_ada, gb_ada = _ada_bwd(c_all, lax.dynamic_slice_in_dim(dmod_all, q * Ca, Ca, axis=1), dmod_all)
    g_lr, g_li, g_ldt, g_bt_r, g_bt_i = _s5_disc_bwd(lam_re[0], lam_im[0], ldt_c, bt_r, bt_i,
                                                   r_dlr.reshape(G, P), r_dli.reshape(G, P), r_dbr, r_dbi)

    grads = dict(
        norm1_g=r_n1, norm2_g=r_n2, w_ada=gw_ada, b_ada=gb_ada, lam_re=g_lr, lam_im=g_li, log_dt=g_ldt.reshape(1, G),
        b_re=g_bt_r.transpose(1, 2, 0), b_im=g_bt_i.transpose(1, 2, 0), c_re=r_cre, c_im=r_cim, d_skip=r_dsk,
        b_glu=r_bglu, conv_w=lax.dynamic_slice_in_dim(r_cw, q * Dcs, Dcs, axis=1), final_g=r_fg, **reduced)
    weights = dict(norm1_g=norm1_g, norm2_g=norm2_g, w_ada=w_ada, b_ada=b_ada, w_in=w_in, lam_re=lam_re, lam_im=lam_im,
                   log_dt=log_dt, b_re=b_re, b_im=b_im, c_re=c_re, c_im=c_im, d_skip=d_skip, w_glu=w_glu, b_glu=b_glu,
                   conv_w=conv_w, w_proj_ssm=w_proj_ssm, w_proj_conv=w_proj_conv, w_out=w_out, w_ff1=w_ff1, w_ff2=w_ff2,
                   final_g=final_g)
    m_in = dict(norm1_g=m_norm1_g, norm2_g=m_norm2_g, w_ada=m_w_ada, b_ada=m_b_ada, w_in=m_w_in, lam_re=m_lam_re,
                lam_im=m_lam_im, log_dt=m_log_dt, b_re=m_b_re, b_im=m_b_im, c_re=m_c_re, c_im=m_c_im, d_skip=m_d_skip,
                w_glu=m_w_glu, b_glu=m_b_glu, conv_w=m_conv_w, w_proj_ssm=m_w_proj_ssm, w_proj_conv=m_w_proj_conv,
                w_out=m_w_out, w_ff1=m_w_ff1, w_ff2=m_w_ff2, final_g=m_final_g)
    v_in = dict(norm1_g=v_norm1_g, norm2_g=v_norm2_g, w_ada=v_w_ada, b_ada=v_b_ada, w_in=v_w_in, lam_re=v_lam_re,
                lam_im=v_lam_im, log_dt=v_log_dt, b_re=v_b_re, b_im=v_b_im, c_re=v_c_re, c_im=v_c_im, d_skip=v_d_skip,
                w_glu=v_w_glu, b_glu=v_b_glu, conv_w=v_conv_w, w_proj_ssm=v_w_proj_ssm, w_proj_conv=v_w_proj_conv,
                w_out=v_w_out, w_ff1=v_w_ff1, w_ff2=v_w_ff2, final_g=v_final_g)
    names = list(weights)
    grads = {k: grads[k].reshape(weights[k].shape) for k in names}

    big_upd = big_names + ["w_ada"]
    delta, new_m, new_v = {}, {}, {}
    flat2 = lambda a: a.reshape(-1, a.shape[-1])
    d_, m_, v_, g_ = _adamw([flat2(weights[k]) for k in big_upd], [flat2(grads[k]) for k in big_upd],
                            [flat2(m_in[k]) for k in big_upd], [flat2(v_in[k]) for k in big_upd], "adamw_big")
    for k, dd, mm, vv, gg in zip(big_upd, d_, m_, v_, g_):
        shp = weights[k].shape
        delta[k], new_m[k], new_v[k], grads[k] = dd.reshape(shp), mm.reshape(shp), vv.reshape(shp), gg.reshape(shp)
    small_upd = [k for k in names if k not in big_upd]
    d_, m_, v_ = _adamw_many([flat2(weights[k]) for k in small_upd], [flat2(grads[k]) for k in small_upd],
                             [flat2(m_in[k]) for k in small_upd], [flat2(v_in[k]) for k in small_upd])
    for k, dd, mm, vv in zip(small_upd, d_, m_, v_):
        shp = weights[k].shape
        delta[k], new_m[k], new_v[k] = dd.reshape(shp), mm.reshape(shp), vv.reshape(shp)

    loss = lax.psum(loss_p[0, 0], ("x", "y", "c"))
    grad_x = grad_x2.reshape(Bl, L, D)
    return (loss, grad_x, *[grads[k] for k in names], *[delta[k] for k in names],
            *[new_m[k] for k in names], *[new_v[k] for k in names])
```

```python
import functools
from typing import Callable, NamedTuple

import jax
import jax.numpy as jnp
from jax import lax
from jax.experimental import pallas as pl
from jax.experimental.pallas import tpu as pltpu

F32 = jnp.float32
BF16 = jnp.bfloat16
MESH = pl.DeviceIdType.MESH
N_CHIPS = 4
N_DEV = 8
LANES = 128
SUBLANES = 8
V7X_VMEM_BYTES = 64 * 1024 * 1024
VMEM_LIMIT = V7X_VMEM_BYTES - 6 * 1024 * 1024
SSM_GROUP = 16
SSM_STATE = 64
S5_ROW_PAD = 4
HALO_ROWS = 16
RMS_EPS = 1e-6
ADAM_LR, ADAM_B1, ADAM_B2, ADAM_EPS, ADAM_WD, ADAM_STEP = 0.001, 0.9, 0.999, 1e-08, 0.01, 10

ANY = pl.BlockSpec(memory_space=pl.ANY)
VMEM_SPEC = pl.BlockSpec(memory_space=pltpu.VMEM)


def _cparams(vmem_limit_bytes=VMEM_LIMIT, **kw):
    return pltpu.CompilerParams(vmem_limit_bytes=vmem_limit_bytes, **kw)


def _dot(a, b):
    return jnp.dot(a, b, preferred_element_type=F32)


def _dot_nt(a, b):
    return lax.dot_general(a, b, (((1,), (1,)), ((), ())), preferred_element_type=F32)


def _dot_tn(a, b):
    return lax.dot_general(a, b, (((0,), (0,)), ((), ())), preferred_element_type=F32)


def _mesh_pos():
    return lax.axis_index("x"), lax.axis_index("y"), lax.axis_index("c")


def _allgather8(v, name, carried=None):
    r, c = v.shape

    def body(x_ref, out_ref, send_sems, recv_sems, local_sem):
        x, y, cc = _mesh_pos()
        me, sibling = (x, y, cc), (x, y, 1 - cc)
        chips = [(1 - x, y), (x, 1 - y), (1 - x, 1 - y)]

        def slot(px, py, pc):
            return out_ref.at[4 * px + 2 * py + pc]

        def copy(k, block, to, src=None):
            return pltpu.make_async_remote_copy(
                src_ref=slot(*block) if src is None else src, dst_ref=slot(*block),
                send_sem=send_sems.at[k], recv_sem=recv_sems.at[k], device_id=to, device_id_type=MESH)

        mine = pltpu.make_async_copy(x_ref, slot(*me), local_sem)
        mine.start()
        first = [copy(0, me, sibling, src=x_ref)]
        first += [copy(1 + j, me, (*chip, cc), src=x_ref) for j, chip in enumerate(chips)]
        for cp in first:
            cp.start()
        passed = [copy(4 + j, (*chip, cc), sibling) for j, chip in enumerate(chips)]
        for j, chip in enumerate(chips):
            copy(1 + j, (*chip, cc), me).wait_recv()
            passed[j].start()
        copy(0, sibling, me).wait_recv()
        for j, chip in enumerate(chips):
            copy(4 + j, (*chip, 1 - cc), me).wait_recv()
        for cp in first + passed:
            cp.wait_send()
        mine.wait()

    sems = [pltpu.SemaphoreType.DMA((7,)), pltpu.SemaphoreType.DMA((7,)), pltpu.SemaphoreType.DMA]
    out_shape = jax.ShapeDtypeStruct((N_DEV, r, c), v.dtype)
    if carried is None:
        return pl.pallas_call(body, out_shape=out_shape, in_specs=[VMEM_SPEC], out_specs=VMEM_SPEC,
                              scratch_shapes=sems, name=name)(v)
    (out,), extra = _call_carrying(body, carried, (v,), out_shape=[out_shape], in_specs=[VMEM_SPEC],
                                   out_specs=[VMEM_SPEC], scratch_shapes=sems, name=name)
    return out, extra


def _shard_region(ref, axis, shard_shape, q, half):
    R, C = shard_shape
    r0, nr = (0, R) if half is None else (half * (R // 2), R // 2)
    if axis == 1:
        return ref.at[pl.ds(r0, nr), pl.ds(q * C, C)]
    return ref.at[pl.ds(q * R + r0, nr), :]


class _Carried(NamedTuple):
    inputs: tuple
    out_shapes: tuple
    aliases: dict
    sems: tuple
    steps: Callable


def _carry_join(a, b):
    na_i, na_o, na_s = len(a.inputs), len(a.out_shapes), len(a.sems)

    def steps(ins, outs, sems):
        sa, fa = a.steps(ins[:na_i], outs[:na_o], sems[:na_s])
        sb, fb = b.steps(ins[na_i:], outs[na_o:], sems[na_s:])


        def start():
            sa()
            sb()

        def finish():
            fa()
            fb()

        return start, finish

    aliases = dict(a.aliases)
    aliases.update({na_i + i: na_o + o for i, o in b.aliases.items()})
    return _Carried(a.inputs + b.inputs, a.out_shapes + b.out_shapes, aliases, a.sems + b.sems, steps)


def _call_carrying(body, carried, args, *, out_shape, in_specs, out_specs, scratch_shapes=(), grid=None, aliases=None,
                   name, **kw):
    n_in, n_out, n_sc = len(in_specs), len(out_specs), len(scratch_shapes)
    n_ci, n_co = len(carried.inputs), len(carried.out_shapes)

    def wrapped(*refs):
        ins, refs = refs[:n_in], refs[n_in:]
        c_ins, refs = refs[:n_ci], refs[n_ci:]
        outs, refs = refs[:n_out], refs[n_out:]
        c_outs, refs = refs[:n_co], refs[n_co:]
        scratch, c_sems = refs[:n_sc], refs[n_sc:]
        fns = carried.steps(c_ins, c_outs, c_sems)
        start, finish = fns[0], fns[-1]
        relay = fns[1] if len(fns) == 3 else None
        if grid is None:
            start()
            body(*ins, *outs, *scratch)
            if relay is not None:
                relay()
            finish()
        else:
            ids = [pl.program_id(d) for d in range(len(grid))]
            first = functools.reduce(jnp.logical_and, [i == 0 for i in ids])
            last = functools.reduce(jnp.logical_and, [i == g - 1 for i, g in zip(ids, grid)])
            pl.when(first)(start)
            body(*ins, *outs, *scratch)
            if relay is not None:
                pl.when(ids[0] == (3 * grid[0]) // 4 if len(grid) == 1 else last)(relay)
            pl.when(last)(finish)

    if grid is not None:
        kw["grid"] = grid
    io_aliases = dict(aliases or {})
    io_aliases.update({n_in + i: n_out + o for i, o in carried.aliases.items()})
    res = pl.pallas_call(
        wrapped, out_shape=list(out_shape) + list(carried.out_shapes),
        in_specs=list(in_specs) + [ANY] * n_ci, out_specs=list(out_specs) + [ANY] * n_co,
        scratch_shapes=list(scratch_shapes) + list(carried.sems),
        input_output_aliases=io_aliases, name=name, **kw,
    )(*args, *carried.inputs)
    return res[:n_out], res[n_out:]


def _run_carried(carried, name):
    return _call_carrying(lambda: None, carried, (), out_shape=(), in_specs=(), out_specs=(), name=name)[1]


def _place_in_slot(v, dev_arr, name):
    r, c = v.shape

    def body(d_ref, v_ref, o_ref):
        o_ref[...] = v_ref[...]

    return pl.pallas_call(
        body, out_shape=jax.ShapeDtypeStruct((N_DEV, r, c), v.dtype),
        grid_spec=pltpu.PrefetchScalarGridSpec(
            num_scalar_prefetch=1, grid=(1,), in_specs=[pl.BlockSpec((r, c), lambda i, d: (0, 0))],
            out_specs=pl.BlockSpec((None, r, c), lambda i, d: (d[0], 0, 0))),
        name=name)(dev_arr, v)


def _carry_allgather8(buf):
    def steps(ins, outs, sems):
        send_s, recv_s = sems
        out = outs[0]
        x, y, cc = _mesh_pos()
        me, sibling = (x, y, cc), (x, y, 1 - cc)
        chips = [(1 - x, y), (x, 1 - y), (1 - x, 1 - y)]

        def copy(k, block, to):
            px, py, pc = block
            slot = out.at[4 * px + 2 * py + pc]
            return pltpu.make_async_remote_copy(src_ref=slot, dst_ref=slot, send_sem=send_s.at[k], recv_sem=recv_s.at[k],
                                                device_id=to, device_id_type=MESH)

        first = [copy(0, me, sibling)] + [copy(1 + j, me, (*chip, cc)) for j, chip in enumerate(chips)]
        passed = [copy(4 + j, (*chip, cc), sibling) for j, chip in enumerate(chips)]

        def start():
            for cp in first:
                cp.start()

        def finish():
            for j, chip in enumerate(chips):
                copy(1 + j, (*chip, cc), me).wait_recv()
                passed[j].start()
            copy(0, sibling, me).wait_recv()
            for j, chip in enumerate(chips):
                copy(4 + j, (*chip, 1 - cc), me).wait_recv()
            for cp in first + passed:
                cp.wait_send()

        return start, finish

    return _Carried((buf,), (jax.ShapeDtypeStruct(buf.shape, buf.dtype),), {0: 0}, (pltpu.SemaphoreType.DMA((7,)),) * 2, steps)


def _carry_allgather(fulls, axes, shapes):
    n = len(fulls)
    return _Carried(tuple(fulls), tuple(jax.ShapeDtypeStruct(f.shape, f.dtype) for f in fulls),
                    {i: i for i in range(n)}, (pltpu.SemaphoreType.DMA((3 * n,)),) * 4,
                    lambda ins, outs, sems: _allgather_weights_steps(outs, axes, shapes, *sems))


def _allgather_weights_steps(outs, axes, shapes, send_s, recv_s, fsend_s, frecv_s):
    n = len(outs)
    x, y, c = _mesh_pos()
    q = 2 * x + y
    sibling = (x, y, 1 - c)
    chips = [(1 - x, y), (x, 1 - y), (1 - x, 1 - y)]

    def region(i, qq, half):
        return _shard_region(outs[i], axes[i], shapes[i], qq, half)

    def remote(src, dst, ss, rs, to):
        return pltpu.make_async_remote_copy(src_ref=src, dst_ref=dst, send_sem=ss, recv_sem=rs,
                                            device_id=to, device_id_type=MESH)

    def ici(i, j, qq):
        cx, cy = chips[j]
        reg = region(i, qq, c)
        return remote(reg, reg, send_s.at[3 * i + j], recv_s.at[3 * i + j], (cx, cy, c))

    def d2d(i, j, half):
        cx, cy = chips[j]
        reg = region(i, 2 * cx + cy, half)
        return remote(reg, reg, fsend_s.at[3 * i + j], frecv_s.at[3 * i + j], sibling)

    def start():
        for i in range(n):
            for j in range(3):
                ici(i, j, q).start()

    def relay():
        for i in range(n):
            for j, (cx, cy) in enumerate(chips):
                ici(i, j, 2 * cx + cy).wait_recv()
                d2d(i, j, c).start()

    def finish():
        for i in range(n):
            for j in range(3):
                d2d(i, j, 1 - c).wait_recv()
        for i in range(n):
            for j in range(3):
                ici(i, j, q).wait_send()
                d2d(i, j, c).wait_send()

    return start, relay, finish


def _carry_sibling_exchange(grads, axes, shapes):
    n = len(grads)

    def steps(ins, theirs, sems):
        send_s, recv_s = sems
        x, y, c = _mesh_pos()

        def copies():
            return [pltpu.make_async_remote_copy(
                src_ref=_shard_region(ins[i], axes[i], shapes[i], qq, 1 - c), dst_ref=theirs[i].at[qq],
                send_sem=send_s.at[N_CHIPS * i + qq], recv_sem=recv_s.at[N_CHIPS * i + qq],
                device_id=(x, y, 1 - c), device_id_type=MESH) for i in range(n) for qq in range(N_CHIPS)]

        def start():
            for cp in copies():
                cp.start()

        def finish():
            for cp in copies():
                cp.wait()

        return start, finish

    stacked = tuple(jax.ShapeDtypeStruct((N_CHIPS, R // 2, C), F32) for (R, C) in shapes)
    return _Carried(tuple(grads), stacked, {}, (pltpu.SemaphoreType.DMA((N_CHIPS * n,)),) * 2, steps)


def _carry_chip_scatter(parts):
    n = len(parts)

    def steps(ins, outs, sems):
        send_s, recv_s = sems
        x, y, c = _mesh_pos()
        chips = [(1 - x, y), (x, 1 - y), (1 - x, 1 - y)]

        def copies():
            return [pltpu.make_async_remote_copy(
                src_ref=ins[i].at[2 * cx + cy], dst_ref=outs[i].at[j],
                send_sem=send_s.at[3 * i + j], recv_sem=recv_s.at[3 * i + j],
                device_id=(cx, cy, c), device_id_type=MESH) for i in range(n) for j, (cx, cy) in enumerate(chips)]

        def start():
            for cp in copies():
                cp.start()

        def finish():
            for cp in copies():
                cp.wait()

        return start, finish

    return _Carried(tuple(parts), tuple(jax.ShapeDtypeStruct((3,) + p.shape[1:], p.dtype) for p in parts), {},
                    (pltpu.SemaphoreType.DMA((3 * n,)),) * 2, steps)


def _carry_sibling_share(fulls):
    n = len(fulls)

    def steps(ins, outs, sems):
        send_s, recv_s = sems
        x, y, c = _mesh_pos()

        def copy(i, half):
            rh = fulls[i].shape[0] // 2
            rows = outs[i].at[pl.ds(half * rh, rh), :]
            return pltpu.make_async_remote_copy(src_ref=rows, dst_ref=rows, send_sem=send_s.at[i], recv_sem=recv_s.at[i],
                                                device_id=(x, y, 1 - c), device_id_type=MESH)

        def start():
            for i in range(n):
                copy(i, c).start()

        def finish():
            for i in range(n):
                copy(i, 1 - c).wait_recv()
                copy(i, c).wait_send()

        return start, finish

    return _Carried(tuple(fulls), tuple(jax.ShapeDtypeStruct(f.shape, f.dtype) for f in fulls),
                    {i: i for i in range(n)}, (pltpu.SemaphoreType.DMA((n,)),) * 2, steps)


def _row_block(rows, target=256):
    return target if rows % target == 0 else rows


BF16_TILE_ROWS = 16


def _common_steps(rows, most=8):
    ns = most
    while ns > 1 and any(r % (ns * BF16_TILE_ROWS) for r in rows):
        ns //= 2
    return ns


def _cast_into_full(ws, axes, pos, name):
    n = len(ws)
    ns = _common_steps([w.shape[0] for w in ws])
    in_specs, out_specs, out_shape = [], [], []
    for w, axis in zip(ws, axes):
        R, C = w.shape
        in_specs.append(pl.BlockSpec((R // ns, C), lambda i, s: (i, 0)))
        if axis == 1:
            out_shape.append(jax.ShapeDtypeStruct((R, N_CHIPS * C), BF16))
            out_specs.append(pl.BlockSpec((R // ns, C), lambda i, s: (i, s[0])))
        else:
            out_shape.append(jax.ShapeDtypeStruct((N_CHIPS * R, C), BF16))
            out_specs.append(pl.BlockSpec((R // ns, C), lambda i, s: (s[0] * ns + i, 0)))

    def body(s_ref, *refs):
        for k in range(n):
            refs[n + k][...] = refs[k][...].astype(BF16)

    return pl.pallas_call(
        body, out_shape=out_shape,
        grid_spec=pltpu.PrefetchScalarGridSpec(num_scalar_prefetch=1, grid=(ns,), in_specs=in_specs, out_specs=out_specs),
        compiler_params=_cparams(), name=name)(pos, *ws)


def _presum(gs, theirs, axes, shapes, pos, name):
    n = len(gs)
    ns = _common_steps([R // 2 for R, _ in shapes], most=2)
    in_specs, t_specs, out_shape = [], [], []
    for (R, C), axis in zip(shapes, axes):
        rb = R // 2 // ns
        if axis == 1:
            in_specs.append(pl.BlockSpec((rb, C), lambda k, i, s: (s[1] * ns + i, k)))
        else:
            in_specs.append(pl.BlockSpec((rb, C), lambda k, i, s: (k * 2 * ns + s[1] * ns + i, 0)))
        t_specs.append(pl.BlockSpec((None, rb, C), lambda k, i, s: (k, i, 0)))
        out_shape.append(jax.ShapeDtypeStruct((N_CHIPS, R // 2, C), BF16))

    def body(s_ref, *refs):
        for k in range(n):
            refs[2 * n + k][...] = (refs[k][...] + refs[n + k][...]).astype(BF16)

    return pl.pallas_call(
        body, out_shape=out_shape,
        grid_spec=pltpu.PrefetchScalarGridSpec(num_scalar_prefetch=1, grid=(N_CHIPS, ns), in_specs=in_specs + t_specs,
                                               out_specs=t_specs),
        compiler_params=_cparams(), name=name)(pos, *gs, *theirs)


def _sum_chips(owns, recvs, pos, name):
    n = len(owns)
    ns = _common_steps([o.shape[1] for o in owns], most=2)
    o_specs, r_specs, out_specs, out_shape = [], [], [], []
    for o in owns:
        _, Rh, C = o.shape
        rb = Rh // ns
        o_specs.append(pl.BlockSpec((None, rb, C), lambda i, s: (s[0], i, 0)))
        r_specs.append(pl.BlockSpec((3, rb, C), lambda i, s: (0, i, 0)))
        out_specs.append(pl.BlockSpec((rb, C), lambda i, s: (s[1] * ns + i, 0)))
        out_shape.append(jax.ShapeDtypeStruct((2 * Rh, C), F32))

    def body(s_ref, *refs):
        for k in range(n):
            acc = refs[k][...].astype(F32)
            for j in range(3):
                acc = acc + refs[n + k][j].astype(F32)
            refs[2 * n + k][...] = acc

    return pl.pallas_call(
        body, out_shape=out_shape,
        grid_spec=pltpu.PrefetchScalarGridSpec(num_scalar_prefetch=1, grid=(ns,), in_specs=o_specs + r_specs,
                                               out_specs=out_specs),
        compiler_params=_cparams(), name=name)(pos, *owns, *recvs)


def _sum_devices(parts, name):
    K, R, C = parts.shape

    def body(p_ref, o_ref):
        acc = p_ref[0]
        for k in range(1, K):
            acc = acc + p_ref[k]
        o_ref[...] = acc

    return pl.pallas_call(body, out_shape=jax.ShapeDtypeStruct((R, C), F32), name=name)(parts)


def _adamw_math(w, g, m, v):
    nm = ADAM_B1 * m + (1.0 - ADAM_B1) * g
    nv = ADAM_B2 * v + (1.0 - ADAM_B2) * (g * g)
    m_hat = nm / (1.0 - ADAM_B1 ** ADAM_STEP)
    v_hat = nv / (1.0 - ADAM_B2 ** ADAM_STEP)
    return -ADAM_LR * (m_hat / (jnp.sqrt(v_hat) + ADAM_EPS) + ADAM_WD * w), nm, nv


def _adamw_many(ws, gs, ms, vs):
    n = len(ws)

    def body(*refs):
        for k in range(n):
            w, g, m, v = (refs[j * n + k][...] for j in range(4))
            for j, val in enumerate(_adamw_math(w, g, m, v)):
                refs[(4 + j) * n + k][...] = val

    shapes = [jax.ShapeDtypeStruct(w.shape, F32) for w in ws]
    res = pl.pallas_call(body, out_shape=shapes * 3, compiler_params=_cparams(), name="adamw_small")(*ws, *gs, *ms, *vs)
    return res[:n], res[n:2 * n], res[2 * n:]


def _adamw(ws, gs, ms, vs, name):
    n = len(ws)
    ns = _common_steps([w.shape[0] for w in ws])
    specs = [pl.BlockSpec((w.shape[0] // ns, w.shape[1]), lambda i: (i, 0)) for w in ws]

    def body(*refs):
        for k in range(n):
            w, g, m, v = (refs[j * n + k][...] for j in range(4))
            for j, val in enumerate(_adamw_math(w, g, m, v) + (g,)):
                refs[(4 + j) * n + k][...] = val

    shapes = [jax.ShapeDtypeStruct(w.shape, F32) for w in ws]
    res = pl.pallas_call(body, out_shape=shapes * 4, grid=(ns,), in_specs=specs * 4, out_specs=specs * 4,
                         compiler_params=_cparams(), name=name)(*ws, *gs, *ms, *vs)
    return res[:n], res[n:2 * n], res[2 * n:3 * n], res[3 * n:]


def _silu(v):
    return v * jax.nn.sigmoid(v)


def _ada_fwd(c_all, w_sh, b_sh):
    S, D = c_all.shape
    Ca = w_sh.shape[1]
    cb = 512 if Ca % 512 == 0 else Ca

    def body(c_ref, w_ref, b_ref, o_ref):
        act = _silu(c_ref[...]).astype(BF16)
        o_ref[...] = _dot(act, w_ref[...].astype(BF16)) + b_ref[...]

    return pl.pallas_call(
        body, out_shape=jax.ShapeDtypeStruct((S, Ca), F32), grid=(Ca // cb,),
        in_specs=[pl.BlockSpec((S, D), lambda j: (0, 0)), pl.BlockSpec((D, cb), lambda j: (0, j)),
                  pl.BlockSpec((1, cb), lambda j: (0, j))],
        out_specs=pl.BlockSpec((S, cb), lambda j: (0, j)), name="ada_fwd")(c_all, w_sh, b_sh)


def _ada_bwd(c_all, dmod_sh, dmod_all):
    S, D = c_all.shape
    Ca = dmod_sh.shape[1]
    C6 = dmod_all.shape[1]

    def body(c_ref, ds_ref, da_ref, gw_ref, gb_ref):
        act = _silu(c_ref[...]).astype(BF16)
        gw_ref[...] = _dot_tn(act, ds_ref[...].astype(BF16))
        gb_ref[...] = jnp.sum(da_ref[...], axis=0, keepdims=True)

    return pl.pallas_call(
        body, out_shape=[jax.ShapeDtypeStruct((D, Ca), F32), jax.ShapeDtypeStruct((1, C6), F32)],
        compiler_params=_cparams(), name="ada_bwd")(c_all, dmod_sh, dmod_all)


def _rms_fwd(xv):
    r = lax.rsqrt(jnp.mean(xv * xv, axis=-1, keepdims=True) + RMS_EPS)
    return xv * r, r


def _rms_bwd(dxh, xh, r):
    return r * (dxh - xh * jnp.mean(dxh * xh, axis=-1, keepdims=True))


def _const_spec(shape):
    nd = len(shape)
    return pl.BlockSpec(shape, lambda *_: (0,) * nd)


def _seq_spec(D, bps, rev_blocks=None):
    if rev_blocks is None:
        return pl.BlockSpec((None, 1, D), lambda i: (i // bps, 0, 0))
    return pl.BlockSpec((None, 1, D), lambda i: ((rev_blocks - 1 - i) // bps, 0, 0))


def _inproj_fwd(x2, sh1, sc1, n1g, w_in, carried):
    N, D = x2.shape
    IN = w_in.shape[1]
    Bl = sh1.shape[0]
    TB = _row_block(N // Bl, 512)
    bps = (N // Bl) // TB

    def body(x_ref, sh_ref, sc_ref, g_ref, w_ref, p_ref):
        xh, _ = _rms_fwd(x_ref[...])
        h = (xh * g_ref[...]) * (1.0 + sc_ref[...]) + sh_ref[...]
        p_ref[...] = _dot(h.astype(BF16), w_ref[...]).astype(BF16)

    (p2,), extra = _call_carrying(
        body, carried, (x2, sh1, sc1, n1g, w_in), out_shape=[jax.ShapeDtypeStruct((N, IN), BF16)], grid=(N // TB,),
        in_specs=[pl.BlockSpec((TB, D), lambda i: (i, 0)), _seq_spec(D, bps), _seq_spec(D, bps),
                  _const_spec((1, D)), _const_spec((D, IN))],
        out_specs=[pl.BlockSpec((TB, IN), lambda i: (i, 0))],
        compiler_params=_cparams(dimension_semantics=("arbitrary",)), name="inproj_fwd")
    return p2, extra


def _s5_dims(Bl, L, Ds):
    G = Ds // SSM_GROUP
    GP = G * SSM_STATE
    NP = GP // LANES
    T = min(64, L // 2)
    nb = 2 if (Ds // 2) % LANES == 0 else 1
    return G, GP, NP, T, nb


def _s5_disc_math(lr, li, ldt, bt_r, bt_i):
    dt = jnp.exp(ldt)
    er = jnp.exp(lr * dt)
    lbr = er * jnp.cos(li * dt)
    lbi = er * jnp.sin(li * dt)
    den = lr * lr + li * li
    fr = ((lbr - 1.0) * lr + lbi * li) / den
    fi = (lbi * lr - (lbr - 1.0) * li) / den
    return lbr, lbi, fr[None] * bt_r - fi[None] * bt_i, fr[None] * bt_i + fi[None] * bt_r


def _s5_disc(lr, li, ldt, bt_r, bt_i):
    def body(lr_ref, li_ref, ldt_ref, br_ref, bi_ref, o0, o1, o2, o3):
        res = _s5_disc_math(lr_ref[...], li_ref[...], ldt_ref[...], br_ref[...], bi_ref[...])
        for o, v in zip((o0, o1, o2, o3), res):
            o[...] = v

    S = jax.ShapeDtypeStruct
    return pl.pallas_call(body, out_shape=[S(lr.shape, F32)] * 2 + [S(bt_r.shape, F32)] * 2, name="s5_disc")(lr, li, ldt, bt_r, bt_i)


def _s5_disc_bwd(lr, li, ldt, bt_r, bt_i, dlbr, dlbi, dbbr, dbbi):
    def body(lr_ref, li_ref, ldt_ref, br_ref, bi_ref, g0, g1, g2, g3, o0, o1, o2, o3, o4):
        _, vjp = jax.vjp(_s5_disc_math, lr_ref[...], li_ref[...], ldt_ref[...], br_ref[...], bi_ref[...])
        res = vjp((g0[...], g1[...], g2[...], g3[...]))
        for o, v in zip((o0, o1, o2, o3, o4), res):
            o[...] = v

    S = jax.ShapeDtypeStruct
    return pl.pallas_call(body, out_shape=[S(lr.shape, F32)] * 2 + [S(ldt.shape, F32)] + [S(bt_r.shape, F32)] * 2,
                          name="s5_disc_bwd")(lr, li, ldt, bt_r, bt_i, dlbr, dlbi, dbbr, dbbi)


S5_SCAN_PANELS = 4
S5_SCAN_STEPS = 4


def _panel_scan(src_r, src_i, dst_r, dst_i, lr_ref, li_ref, car_r, car_i, NP, Bl, T, TP, adjoint):
    BT = Bl * TP
    PG = min(S5_SCAN_PANELS, NP)
    CH = S5_SCAN_STEPS
    for k0 in range(0, NP, PG):
        ks = list(range(k0, k0 + PG))
        lr = [jnp.broadcast_to(lr_ref[pl.ds(k, 1), :], (Bl, LANES)) for k in ks]
        li = [jnp.broadcast_to(li_ref[pl.ds(k, 1), :], (Bl, LANES)) for k in ks]

        def trip(cc, carry):
            t0 = (T // CH - 1 - cc) * CH if adjoint else cc * CH
            ts = [t0 + (CH - 1 - s if adjoint else s) for s in range(CH)]
            idx = [[pl.ds(k * BT + t, Bl, stride=TP) for t in ts] for k in ks]
            loaded = [[(src_r[ix, :], src_i[ix, :]) for ix in idx[j]] for j in range(PG)]
            results, new_carry = [], []
            for j in range(PG):
                ar, ai = carry[j]
                res = []
                for s in range(CH):
                    br, bi = loaded[j][s]
                    if adjoint:
                        ar, ai = br + lr[j] * ar + li[j] * ai, bi + lr[j] * ai - li[j] * ar
                    else:
                        ar, ai = lr[j] * ar - li[j] * ai + br, lr[j] * ai + li[j] * ar + bi
                    res.append((ar, ai))
                results.append(res)
                new_carry.append((ar, ai))
            for j in range(PG):
                for s in range(CH):
                    dst_r[idx[j][s], :] = results[j][s][0]
                    dst_i[idx[j][s], :] = results[j][s][1]
            return tuple(new_carry)

        init = tuple((car_r[pl.ds(k * SUBLANES, Bl), :], car_i[pl.ds(k * SUBLANES, Bl), :]) for k in ks)
        fin = lax.fori_loop(0, T // CH, trip, init, unroll=2)
        for j, k in enumerate(ks):
            car_r[pl.ds(k * SUBLANES, Bl), :] = fin[j][0]
            car_i[pl.ds(k * SUBLANES, Bl), :] = fin[j][1]


def _s5_fwd(p3, Bm_r, Bm_i, Cm_r, Cm_i, lam_r, lam_i, dsk, carried):
    Bl, L, _ = p3.shape
    Ds = dsk.shape[1]
    G, GP, NP, T, nb = _s5_dims(Bl, L, Ds)
    nT = L // T
    TP = T + S5_ROW_PAD
    BT = Bl * TP
    dsb, gpb, npb = Ds // nb, GP // nb, NP // nb

    def body(u_ref, br_ref, bi_ref, cr_ref, ci_ref, lr_ref, li_ref, dsk_ref, sr_ref, si_ref, sb_r, sb_i, y_ref,
             car_r, car_i, upad, ypad, bu_r, bu_i):
        i = pl.program_id(0)

        @pl.when(i == 0)
        def _():
            car_r[...] = jnp.zeros_like(car_r)
            car_i[...] = jnp.zeros_like(car_i)
            upad[...] = jnp.zeros_like(upad)

        zpad = jnp.zeros((S5_ROW_PAD, LANES), F32)
        for k in range(NP):
            for b in range(Bl):
                sr_ref[pl.ds(k * BT + b * TP + T, S5_ROW_PAD), :] = zpad
                si_ref[pl.ds(k * BT + b * TP + T, S5_ROW_PAD), :] = zpad
        for b in range(Bl):
            upad[pl.ds(b * TP, T), :] = u_ref[b].astype(F32)
        u = upad[...]
        ub = u.astype(BF16)
        for blk in range(nb):
            ubb = ub[:, blk * dsb:(blk + 1) * dsb]
            for bu_ref, b_ref in ((bu_r, br_ref), (bu_i, bi_ref)):
                res = _dot(ubb, b_ref[blk])
                for kk in range(npb):
                    k = blk * npb + kk
                    bu_ref[pl.ds(k * BT, BT), :] = res[:, kk * LANES:(kk + 1) * LANES]

        _panel_scan(bu_r, bu_i, sr_ref, si_ref, lr_ref, li_ref, car_r, car_i, NP, Bl, T, TP, adjoint=False)
        sb_r[...] = car_r[...]
        sb_i[...] = car_i[...]

        for blk in range(nb):
            s_r = jnp.concatenate([sr_ref[pl.ds((blk * npb + kk) * BT, BT), :] for kk in range(npb)], axis=1).astype(BF16)
            s_i = jnp.concatenate([si_ref[pl.ds((blk * npb + kk) * BT, BT), :] for kk in range(npb)], axis=1).astype(BF16)
            cols = slice(blk * dsb, (blk + 1) * dsb)
            ypad[:, cols] = _dot_nt(s_r, cr_ref[blk]) + _dot_nt(s_i, ci_ref[blk]) + dsk_ref[:, cols] * u[:, cols]
        for b in range(Bl):
            y_ref[b] = ypad[pl.ds(b * TP, T), :]

    S = jax.ShapeDtypeStruct
    state = S((nT, NP * BT, LANES), F32)
    bound = S((nT, NP * SUBLANES, LANES), F32)
    sspec = pl.BlockSpec((None, NP * BT, LANES), lambda i: (i, 0, 0))
    bspec = pl.BlockSpec((None, NP * SUBLANES, LANES), lambda i: (i, 0, 0))
    return _call_carrying(
        body, carried, (p3, Bm_r, Bm_i, Cm_r, Cm_i, lam_r, lam_i, dsk),
        out_shape=[state, state, bound, bound, S((Bl, L, Ds), F32)], grid=(nT,),
        in_specs=[pl.BlockSpec((Bl, T, Ds), lambda i: (0, i, 0)),
                  _const_spec((nb, dsb, gpb)), _const_spec((nb, dsb, gpb)),
                  _const_spec((nb, dsb, gpb)), _const_spec((nb, dsb, gpb)),
                  _const_spec((NP, LANES)), _const_spec((NP, LANES)), _const_spec((1, Ds))],
        out_specs=[sspec, sspec, bspec, bspec, pl.BlockSpec((Bl, T, Ds), lambda i: (0, i, 0))],
        scratch_shapes=[pltpu.VMEM((NP * SUBLANES, LANES), F32)] * 2 + [pltpu.VMEM((BT, Ds), F32)] * 2
        + [pltpu.VMEM((NP * BT, LANES), F32)] * 2,
        compiler_params=_cparams(dimension_semantics=("arbitrary",)), name="s5_fwd")


def _s5_bwd(dy3, p3, dp3, Sr, Si, Sb_r, Sb_i, Bm_r, Bm_i, Cm_r, Cm_i, lam_r, lam_i, dsk, carried):
    Bl, L, Ds = dy3.shape
    G, GP, NP, T, nb = _s5_dims(Bl, L, Ds)
    nT = L // T
    TP = T + S5_ROW_PAD
    BT = Bl * TP
    dsb, gpb, npb = Ds // nb, GP // nb, NP // nb

    def body(dy_ref, u_ref, dp_ref, sr_ref, si_ref, sbr_ref, sbi_ref, br_ref, bi_ref, cr_ref, ci_ref, lr_ref, li_ref, dsk_ref,
             du_ref, dbr_ref, dbi_ref, dcr_ref, dci_ref, dlr_ref, dli_ref, ddsk_ref,
             a_r, a_i, car_r, car_i, acc_r, acc_i, dypad, upad, dupad, q_r, q_i):
        i = pl.program_id(0)

        @pl.when(i == 0)
        def _():
            for ref in (car_r, car_i, acc_r, acc_i, dbr_ref, dbi_ref, dcr_ref, dci_ref, ddsk_ref, dypad, upad, a_r, a_i):
                ref[...] = jnp.zeros_like(ref)

        for b in range(Bl):
            dypad[pl.ds(b * TP, T), :] = dy_ref[b]
            upad[pl.ds(b * TP, T), :] = u_ref[b].astype(F32)
        dy = dypad[...]
        dyb = dy.astype(BF16)
        u = upad[...]
        ub = u.astype(BF16)
        for blk in range(nb):
            dyb_b = dyb[:, blk * dsb:(blk + 1) * dsb]
            for q_ref, c_ref in ((q_r, cr_ref), (q_i, ci_ref)):
                res = _dot(dyb_b, c_ref[blk])
                for kk in range(npb):
                    q_ref[pl.ds((blk * npb + kk) * BT, BT), :] = res[:, kk * LANES:(kk + 1) * LANES]

        _panel_scan(q_r, q_i, a_r, a_i, lr_ref, li_ref, car_r, car_i, NP, Bl, T, TP, adjoint=True)

        first_block = (i == nT - 1)
        for k in range(NP):
            rows = pl.ds(k * BT, BT)
            av_r, av_i = a_r[rows, :], a_i[rows, :]
            sp_r = pltpu.roll(sr_ref[rows, :], 1, 0)
            sp_i = pltpu.roll(si_ref[rows, :], 1, 0)
            acc = pl.ds(k * SUBLANES, SUBLANES)
            acc_r[acc, :] += jnp.sum((av_r * sp_r + av_i * sp_i).reshape(BT // SUBLANES, SUBLANES, LANES), axis=0)
            acc_i[acc, :] += jnp.sum((av_i * sp_r - av_r * sp_i).reshape(BT // SUBLANES, SUBLANES, LANES), axis=0)
            t0 = pl.ds(k * BT, Bl, stride=TP)
            a0_r, a0_i = a_r[t0, :], a_i[t0, :]
            brow = pl.ds(k * SUBLANES, Bl)
            sb_pr = jnp.where(first_block, 0.0, sbr_ref[brow, :])
            sb_pi = jnp.where(first_block, 0.0, sbi_ref[brow, :])
            acc_r[brow, :] += a0_r * sb_pr + a0_i * sb_pi
            acc_i[brow, :] += a0_i * sb_pr - a0_r * sb_pi

        ddsk_ref[...] += jnp.sum(dy * u, axis=0, keepdims=True)
        for blk in range(nb):
            cols = slice(blk * dsb, (blk + 1) * dsb)
            rows = [pl.ds((blk * npb + kk) * BT, BT) for kk in range(npb)]
            av_r = jnp.concatenate([a_r[r, :] for r in rows], axis=1).astype(BF16)
            av_i = jnp.concatenate([a_i[r, :] for r in rows], axis=1).astype(BF16)
            dupad[:, cols] = _dot_nt(av_r, br_ref[blk]) + _dot_nt(av_i, bi_ref[blk]) + dy[:, cols] * dsk_ref[:, cols]
            dbr_ref[blk] += _dot_tn(ub[:, cols], av_r)
            dbi_ref[blk] += _dot_tn(ub[:, cols], av_i)
            sv_r = jnp.concatenate([sr_ref[r, :] for r in rows], axis=1).astype(BF16)
            sv_i = jnp.concatenate([si_ref[r, :] for r in rows], axis=1).astype(BF16)
            dcr_ref[blk] += _dot_tn(dyb[:, cols], sv_r)
            dci_ref[blk] += _dot_tn(dyb[:, cols], sv_i)
        for b in range(Bl):
            du_ref[b] = dupad[pl.ds(b * TP, T), :].astype(BF16)

        @pl.when(i == nT - 1)
        def _():
            for k in range(NP):
                dlr_ref[pl.ds(k, 1), :] = jnp.sum(acc_r[pl.ds(k * SUBLANES, SUBLANES), :], axis=0, keepdims=True)
                dli_ref[pl.ds(k, 1), :] = jnp.sum(acc_i[pl.ds(k * SUBLANES, SUBLANES), :], axis=0, keepdims=True)

    S = jax.ShapeDtypeStruct
    rev = lambda i: nT - 1 - i
    sspec = pl.BlockSpec((None, NP * BT, LANES), lambda i: (rev(i), 0, 0))
    bspec = pl.BlockSpec((None, NP * SUBLANES, LANES), lambda i: (jnp.maximum(rev(i) - 1, 0), 0, 0))
    tspec = pl.BlockSpec((Bl, T, Ds), lambda i: (0, rev(i), 0))
    return _call_carrying(
        body, carried, (dy3, p3, dp3, Sr, Si, Sb_r, Sb_i, Bm_r, Bm_i, Cm_r, Cm_i, lam_r, lam_i, dsk),
        out_shape=[S(dp3.shape, dp3.dtype), S((nb, dsb, gpb), F32), S((nb, dsb, gpb), F32),
                   S((nb, dsb, gpb), F32), S((nb, dsb, gpb), F32), S((NP, LANES), F32), S((NP, LANES), F32), S((1, Ds), F32)],
        grid=(nT,),
        in_specs=[tspec, tspec, ANY, sspec, sspec, bspec, bspec,
                  _const_spec((nb, dsb, gpb)), _const_spec((nb, dsb, gpb)),
                  _const_spec((nb, dsb, gpb)), _const_spec((nb, dsb, gpb)),
                  _const_spec((NP, LANES)), _const_spec((NP, LANES)), _const_spec((1, Ds))],
        out_specs=[tspec, _const_spec((nb, dsb, gpb)), _const_spec((nb, dsb, gpb)),
                   _const_spec((nb, dsb, gpb)), _const_spec((nb, dsb, gpb)),
                   _const_spec((NP, LANES)), _const_spec((NP, LANES)), _const_spec((1, Ds))],
        aliases={2: 0},
        scratch_shapes=[pltpu.VMEM((NP * BT, LANES), F32)] * 2 + [pltpu.VMEM((NP * SUBLANES, LANES), F32)] * 4
        + [pltpu.VMEM((BT, Ds), F32)] * 3 + [pltpu.VMEM((NP * BT, LANES), F32)] * 2,
        compiler_params=_cparams(dimension_semantics=("arbitrary",)), name="s5_bwd")


def _mix_values(ylin, cb, cc, cx, gs, gc, halo_v, wglu, bglu, cw, wps, wpc, wout):
    yg, gelu_vjp = jax.vjp(jax.nn.gelu, ylin)
    sz = jax.nn.sigmoid(_dot(yg.astype(BF16), wglu) + bglu)
    ys = yg * sz
    v = cc * cx
    rows = lax.broadcasted_iota(jnp.int32, v.shape, 0)
    h6 = halo_v[HALO_ROWS - 2:HALO_ROWS - 1, :]
    h7 = halo_v[HALO_ROWS - 1:HALO_ROWS, :]
    v1 = jnp.where(rows == 0, h7, pltpu.roll(v, 1, 0))
    v2 = jnp.where(rows == 0, h6, jnp.where(rows == 1, h7, pltpu.roll(v, 2, 0)))
    cv = cw[0:1, :] * v2 + cw[1:2, :] * v1 + cw[2:3, :] * v
    yc = cb * cv
    ps = _dot(ys.astype(BF16), wps)
    pc = _dot(yc.astype(BF16), wpc)
    sgs = jax.nn.sigmoid(gs)
    sgc = jax.nn.sigmoid(gc)
    merged = sgs * ps + sgc * pc
    mo = _dot(merged.astype(BF16), wout)
    return dict(yg=yg, gelu_vjp=gelu_vjp, sz=sz, ys=ys, v=v, v1=v1, v2=v2, cv=cv, yc=yc, ps=ps, pc=pc,
                sgs=sgs, sgc=sgc, merged=merged, mo=mo)


def _mix_in_specs(TB, D, Ds, Dc, bps, blk):
    hb = TB // HALO_ROWS
    halo = lambda col: pl.BlockSpec((HALO_ROWS, Dc), lambda i: (jnp.maximum(blk(i) * hb - 1, 0), col))
    return [pl.BlockSpec((TB, Dc), lambda i: (blk(i), 1)), pl.BlockSpec((TB, Dc), lambda i: (blk(i), 2)),
            pl.BlockSpec((TB, Dc), lambda i: (blk(i), 3)), pl.BlockSpec((TB, D), lambda i: (blk(i), 2)),
            pl.BlockSpec((TB, D), lambda i: (blk(i), 3)), halo(2), halo(3),
            pl.BlockSpec((TB, Ds), lambda i: (blk(i), 0))]


def _mix_fwd(p2, ylin2, x2, g1, wglu, bglu, cw, wps, wpc, wout):
    N, D = x2.shape
    Ds = ylin2.shape[1]
    Dc = Ds
    Bl = g1.shape[0]
    TB = _row_block(N // Bl, 512)
    bps = (N // Bl) // TB

    def body(cb_ref, cc_ref, cx_ref, gs_ref, gc_ref, hcc_ref, hcx_ref, yl_ref, x_ref, g1_ref,
             wglu_ref, bglu_ref, cw_ref, wps_ref, wpc_ref, wout_ref, x1_ref):
        i = pl.program_id(0)
        f32 = lambda ref: ref[...].astype(F32)
        halo_v = jnp.where(i % bps == 0, 0.0, f32(hcc_ref) * f32(hcx_ref))
        f = _mix_values(yl_ref[...], f32(cb_ref), f32(cc_ref), f32(cx_ref), f32(gs_ref), f32(gc_ref), halo_v,
                        wglu_ref[...], bglu_ref[...], cw_ref[...], wps_ref[...], wpc_ref[...], wout_ref[...])
        x1_ref[...] = x_ref[...] + g1_ref[...] * f["mo"]

    return pl.pallas_call(
        body, out_shape=jax.ShapeDtypeStruct((N, D), F32), grid=(N // TB,),
        in_specs=_mix_in_specs(TB, D, Ds, Dc, bps, lambda i: i) + [
            pl.BlockSpec((TB, D), lambda i: (i, 0)), _seq_spec(D, bps),
            _const_spec((Ds, Ds)), _const_spec((1, Ds)), _const_spec((SUBLANES, Dc)),
            _const_spec((Ds, D)), _const_spec((Dc, D)), _const_spec((D, D))],
        out_specs=pl.BlockSpec((TB, D), lambda i: (i, 0)),
        compiler_params=_cparams(), name="mix_fwd",
    )(p2, p2, p2, p2, p2, p2, p2, ylin2, x2, g1, wglu, bglu, cw, wps, wpc, wout)


def _mix_bwd(p2, ylin2, dx1, g1, wglu, bglu, cw, wps, wpc, wout, carried):
    N, D = dx1.shape
    Ds = ylin2.shape[1]
    Dc = Ds
    IN = p2.shape[1]
    Bl = g1.shape[0]
    TB = _row_block(N // Bl, 512)
    bps = (N // Bl) // TB
    nblk = N // TB
    rev = lambda i: nblk - 1 - i

    def body(cb_ref, cc_ref, cx_ref, gs_ref, gc_ref, hcc_ref, hcx_ref, yl_ref, dx1_ref, g1_ref,
             wglu_ref, bglu_ref, cw_ref, wps_ref, wpc_ref, wout_ref,
             dyl_ref, dp_ref, gwout_ref, gwps_ref, gwpc_ref, gwglu_ref, gbglu_ref, gcw_ref, dg1_ref, nxt):
        i = pl.program_id(0)
        blk = rev(i)

        @pl.when(i == 0)
        def _():
            for ref in (gwout_ref, gwps_ref, gwpc_ref, gwglu_ref, gbglu_ref, gcw_ref):
                ref[...] = jnp.zeros_like(ref)

        @pl.when(i % bps == 0)
        def _():
            nxt[...] = jnp.zeros_like(nxt)
            dg1_ref[...] = jnp.zeros_like(dg1_ref)

        f32 = lambda ref: ref[...].astype(F32)
        cb, cc, cx = f32(cb_ref), f32(cc_ref), f32(cx_ref)
        halo_v = jnp.where(blk % bps == 0, 0.0, f32(hcc_ref) * f32(hcx_ref))
        wglu, wps, wpc, wout, cw = wglu_ref[...], wps_ref[...], wpc_ref[...], wout_ref[...], cw_ref[...]
        f = _mix_values(yl_ref[...], cb, cc, cx, f32(gs_ref), f32(gc_ref), halo_v, wglu, bglu_ref[...], cw, wps, wpc, wout)

        dx1v = dx1_ref[...]
        dg1_ref[...] += jnp.sum(dx1v * f["mo"], axis=0, keepdims=True)
        dmo = (g1_ref[...] * dx1v).astype(BF16)
        gwout_ref[...] += _dot_tn(f["merged"].astype(BF16), dmo)
        dmerged = _dot_nt(dmo, wout)
        dps = dmerged * f["sgs"]
        dpc = dmerged * f["sgc"]
        dgs = dmerged * f["ps"] * f["sgs"] * (1.0 - f["sgs"])
        dgc = dmerged * f["pc"] * f["sgc"] * (1.0 - f["sgc"])
        dpsb, dpcb = dps.astype(BF16), dpc.astype(BF16)
        gwps_ref[...] += _dot_tn(f["ys"].astype(BF16), dpsb)
        gwpc_ref[...] += _dot_tn(f["yc"].astype(BF16), dpcb)
        dys = _dot_nt(dpsb, wps)
        dyc = _dot_nt(dpcb, wpc)

        dcb = dyc * f["cv"]
        dcv = dyc * cb
        rows = lax.broadcasted_iota(jnp.int32, dcv.shape, 0)
        n0, n1 = nxt[0:1, :], nxt[1:2, :]
        d1 = jnp.where(rows == TB - 1, n0, pltpu.roll(dcv, TB - 1, 0))
        d2 = jnp.where(rows == TB - 2, n0, jnp.where(rows == TB - 1, n1, pltpu.roll(dcv, TB - 2, 0)))
        dv = cw[2:3, :] * dcv + cw[1:2, :] * d1 + cw[0:1, :] * d2
        nxt[0:2, :] = dcv[0:2, :]
        gcw_ref[0:1, :] += jnp.sum(dcv * f["v2"], axis=0, keepdims=True)
        gcw_ref[1:2, :] += jnp.sum(dcv * f["v1"], axis=0, keepdims=True)
        gcw_ref[2:3, :] += jnp.sum(dcv * f["v"], axis=0, keepdims=True)

        dz = dys * f["yg"] * f["sz"] * (1.0 - f["sz"])
        dzb = dz.astype(BF16)
        gwglu_ref[...] += _dot_tn(f["yg"].astype(BF16), dzb)
        gbglu_ref[...] += jnp.sum(dz, axis=0, keepdims=True)
        dyg = dys * f["sz"] + _dot_nt(dzb, wglu)
        dyl_ref[...] = f["gelu_vjp"](dyg)[0]

        dp_ref[:, Ds:Ds + Dc] = dcb.astype(BF16)
        dp_ref[:, Ds + Dc:Ds + 2 * Dc] = (dv * cx).astype(BF16)
        dp_ref[:, Ds + 2 * Dc:Ds + 3 * Dc] = (dv * cc).astype(BF16)
        dp_ref[:, Ds + 3 * Dc:Ds + 3 * Dc + D] = dgs.astype(BF16)
        dp_ref[:, Ds + 3 * Dc + D:IN] = dgc.astype(BF16)

    S = jax.ShapeDtypeStruct
    return _call_carrying(
        body, carried, (p2, p2, p2, p2, p2, p2, p2, ylin2, dx1, g1, wglu, bglu, cw, wps, wpc, wout),
        out_shape=[S((N, Ds), F32), S((N, IN), BF16), S((D, D), F32), S((Ds, D), F32), S((Dc, D), F32),
                   S((Ds, Ds), F32), S((1, Ds), F32), S((SUBLANES, Dc), F32), S((Bl, 1, D), F32)],
        grid=(nblk,),
        in_specs=_mix_in_specs(TB, D, Ds, Dc, bps, rev) + [
            pl.BlockSpec((TB, D), lambda i: (rev(i), 0)), _seq_spec(D, bps, nblk),
            _const_spec((Ds, Ds)), _const_spec((1, Ds)), _const_spec((SUBLANES, Dc)),
            _const_spec((Ds, D)), _const_spec((Dc, D)), _const_spec((D, D))],
        out_specs=[pl.BlockSpec((TB, Ds), lambda i: (rev(i), 0)), pl.BlockSpec((TB, IN), lambda i: (rev(i), 0)),
                   _const_spec((D, D)), _const_spec((Ds, D)), _const_spec((Dc, D)), _const_spec((Ds, Ds)),
                   _const_spec((1, Ds)), _const_spec((SUBLANES, Dc)), _seq_spec(D, bps, nblk)],
        scratch_shapes=[pltpu.VMEM((SUBLANES, Dc), F32)],
        compiler_params=_cparams(vmem_limit_bytes=V7X_VMEM_BYTES - 1024 * 1024, dimension_semantics=("arbitrary",)),
        name="mix_bwd")


def _mlp_fwd_bwd(x1, tgt, sh2, sc2, g2, n2g, fg, w1, w2):
    N, D = x1.shape
    Dff = w1.shape[1]
    Bl = sh2.shape[0]
    TB = _row_block(N // Bl)
    bps = (N // Bl) // TB

    def body(x1_ref, t_ref, sh_ref, sc_ref, g2_ref, n2_ref, fg_ref, w1_ref, w2_ref,
             dx1_ref, h2_ref, da_ref, sq_ref, df_ref, loss_ref, gfg_ref, gn2_ref, dsh_ref, dsc_ref, dg2_ref):
        i = pl.program_id(0)

        @pl.when(i == 0)
        def _():
            for ref in (loss_ref, gfg_ref, gn2_ref):
                ref[...] = jnp.zeros_like(ref)

        @pl.when(i % bps == 0)
        def _():
            for ref in (dsh_ref, dsc_ref, dg2_ref):
                ref[...] = jnp.zeros_like(ref)

        x1v = x1_ref[...]
        sc, g2v, n2 = sc_ref[...], g2_ref[...], n2_ref[...]
        xh2, r2 = _rms_fwd(x1v)
        xn2 = xh2 * n2
        h2 = (xn2 * (1.0 + sc) + sh_ref[...]).astype(BF16)
        a = _dot(h2, w1_ref[...])
        ra = jnp.maximum(a, 0.0)
        sq = (ra * ra).astype(BF16)
        fv = _dot(sq, w2_ref[...])
        x2 = x1v + g2v * fv
        xh3, r3 = _rms_fwd(x2)
        err = xh3 * fg_ref[...] - t_ref[...]
        loss_ref[...] += 0.5 * jnp.sum(jnp.mean(err * err, axis=-1, keepdims=True), axis=0, keepdims=True)
        dy = err * (1.0 / D)
        gfg_ref[...] += jnp.sum(dy * xh3, axis=0, keepdims=True)
        dx2 = _rms_bwd(dy * fg_ref[...], xh3, r3)
        dg2_ref[...] += jnp.sum(dx2 * fv, axis=0, keepdims=True)
        df = (g2v * dx2).astype(BF16)
        dsq = _dot_nt(df, w2_ref[...])
        da = (2.0 * ra * dsq).astype(BF16)
        dh2 = _dot_nt(da, w1_ref[...])
        dsh_ref[...] += jnp.sum(dh2, axis=0, keepdims=True)
        dsc_ref[...] += jnp.sum(dh2 * xn2, axis=0, keepdims=True)
        dxn2 = dh2 * (1.0 + sc)
        gn2_ref[...] += jnp.sum(dxn2 * xh2, axis=0, keepdims=True)
        dx1_ref[...] = dx2 + _rms_bwd(dxn2 * n2, xh2, r2)
        h2_ref[...] = h2
        da_ref[...] = da
        sq_ref[...] = sq
        df_ref[...] = df

    S = jax.ShapeDtypeStruct
    row = lambda w: pl.BlockSpec((TB, w), lambda i: (i, 0))
    return pl.pallas_call(
        body,
        out_shape=[S((N, D), F32), S((N, D), BF16), S((N, Dff), BF16), S((N, Dff), BF16), S((N, D), BF16),
                   S((1, 1), F32), S((1, D), F32), S((1, D), F32), S((Bl, 1, D), F32), S((Bl, 1, D), F32), S((Bl, 1, D), F32)],
        grid=(N // TB,),
        in_specs=[row(D), row(D), _seq_spec(D, bps), _seq_spec(D, bps), _seq_spec(D, bps),
                  _const_spec((1, D)), _const_spec((1, D)), _const_spec((D, Dff)), _const_spec((Dff, D))],
        out_specs=[row(D), row(D), row(Dff), row(Dff), row(D), _const_spec((1, 1)), _const_spec((1, D)), _const_spec((1, D)),
                   _seq_spec(D, bps), _seq_spec(D, bps), _seq_spec(D, bps)],
        compiler_params=_cparams(dimension_semantics=("arbitrary",)), name="mlp_fwd_bwd",
    )(x1, tgt, sh2, sc2, g2, n2g, fg, w1, w2)


_NO_EXCHANGE = _Carried((), (), {}, (), lambda ins, outs, sems: ((lambda: None), (lambda: None)))


def _grad_w(a, b, name, carried=_NO_EXCHANGE):
    N, K1 = a.shape
    K2 = b.shape[1]
    t1 = 1024 if K1 % 1024 == 0 else K1
    t2 = 1024 if K2 % 1024 == 0 else K2
    tn = 2048 if N % 2048 == 0 else N

    def body(a_ref, b_ref, o_ref):
        @pl.when(pl.program_id(2) == 0)
        def _():
            o_ref[...] = jnp.zeros_like(o_ref)

        o_ref[...] += _dot_tn(a_ref[...], b_ref[...])

    (g,), extra = _call_carrying(
        body, carried, (a, b), out_shape=[jax.ShapeDtypeStruct((K1, K2), F32)], grid=(K1 // t1, K2 // t2, N // tn),
        in_specs=[pl.BlockSpec((tn, t1), lambda i, j, k: (k, i)), pl.BlockSpec((tn, t2), lambda i, j, k: (k, j))],
        out_specs=[pl.BlockSpec((t1, t2), lambda i, j, k: (i, j))],
        compiler_params=_cparams(dimension_semantics=("arbitrary", "arbitrary", "arbitrary")), name=name)
    return g, extra


def _inproj_bwd(x2, dx1, dp, sh1, sc1, n1g, w_in, carried):
    N, D = x2.shape
    IN = w_in.shape[1]
    Bl = sh1.shape[0]
    TB = _row_block(N // Bl, 512)
    bps = (N // Bl) // TB

    def body(x_ref, dx1_ref, dp_ref, sh_ref, sc_ref, g_ref, w_ref, gx_ref, h_ref, gn1_ref, dsh_ref, dsc_ref):
        i = pl.program_id(0)

        @pl.when(i == 0)
        def _():
            gn1_ref[...] = jnp.zeros_like(gn1_ref)

        @pl.when(i % bps == 0)
        def _():
            dsh_ref[...] = jnp.zeros_like(dsh_ref)
            dsc_ref[...] = jnp.zeros_like(dsc_ref)

        sc, n1 = sc_ref[...], g_ref[...]
        xh, r = _rms_fwd(x_ref[...])
        xn = xh * n1
        h_ref[...] = (xn * (1.0 + sc) + sh_ref[...]).astype(BF16)
        dh = _dot_nt(dp_ref[...], w_ref[...])
        dsh_ref[...] += jnp.sum(dh, axis=0, keepdims=True)
        dsc_ref[...] += jnp.sum(dh * xn, axis=0, keepdims=True)
        dxn = dh * (1.0 + sc)
        gn1_ref[...] += jnp.sum(dxn * xh, axis=0, keepdims=True)
        gx_ref[...] = dx1_ref[...] + _rms_bwd(dxn * n1, xh, r)

    S = jax.ShapeDtypeStruct
    row = lambda w: pl.BlockSpec((TB, w), lambda i: (i, 0))
    return _call_carrying(
        body, carried, (x2, dx1, dp, sh1, sc1, n1g, w_in),
        out_shape=[S((N, D), F32), S((N, D), BF16), S((1, D), F32), S((Bl, 1, D), F32), S((Bl, 1, D), F32)],
        grid=(N // TB,),
        in_specs=[row(D), row(D), row(IN), _seq_spec(D, bps), _seq_spec(D, bps), _const_spec((1, D)), _const_spec((D, IN))],
        out_specs=[row(D), row(D), _const_spec((1, D)), _seq_spec(D, bps), _seq_spec(D, bps)],
        compiler_params=_cparams(dimension_semantics=("arbitrary",)), name="inproj_bwd")


def _diag_mask(gb, a, b):
    rows = lax.broadcasted_iota(jnp.int32, (gb * a, gb * b), 0) // a
    cols = lax.broadcasted_iota(jnp.int32, (gb * a, gb * b), 1) // b
    return (rows == cols).astype(F32)


def _diag_blocks_from_groups(m, nb):
    G, a, b = m.shape
    gb = G // nb
    return jnp.tile(m.reshape(nb, gb * a, b), (1, 1, gb)) * _diag_mask(gb, a, b)[None]


def _groups_from_diag_blocks(d, G, a, b):
    nb = d.shape[0]
    gb = G // nb
    picked = (d * _diag_mask(gb, a, b)[None]).reshape(nb, gb * a, gb, b)
    return jnp.sum(picked, axis=2).reshape(G, a, b)


def _pad_rows(v, rows):
    return jnp.concatenate([v, jnp.zeros((rows - v.shape[0],) + v.shape[1:], v.dtype)], axis=0)


def _pack(vs):
    flat = jnp.concatenate([v.reshape(-1) for v in vs])
    n = flat.shape[0]
    tile = SUBLANES * LANES
    npad = -(-n // tile) * tile
    flat = jnp.concatenate([flat, jnp.zeros((npad - n,), flat.dtype)])
    return flat.reshape(npad // LANES, LANES)


def _unpack(packed, shapes):
    flat = packed.reshape(-1)
    out, off = [], 0
    for s in shapes:
        n = 1
        for d in s:
            n *= d
        out.append(flat[off:off + n].reshape(s))
        off += n
    return out


def kernel(x, c, norm1_g, norm2_g, w_ada, b_ada, w_in, lam_re, lam_im, log_dt, b_re, b_im, c_re, c_im, d_skip, w_glu, b_glu, conv_w, w_proj_ssm, w_proj_conv, w_out, w_ff1, w_ff2, final_g, loss_target, m_norm1_g, m_norm2_g, m_w_ada, m_b_ada, m_w_in, m_lam_re, m_lam_im, m_log_dt, m_b_re, m_b_im, m_c_re, m_c_im, m_d_skip, m_w_glu, m_b_glu, m_conv_w, m_w_proj_ssm, m_w_proj_conv, m_w_out, m_w_ff1, m_w_ff2, m_final_g, v_norm1_g, v_norm2_g, v_w_ada, v_b_ada, v_w_in, v_lam_re, v_lam_im, v_log_dt, v_b_re, v_b_im, v_c_re, v_c_im, v_d_skip, v_w_glu, v_b_glu, v_conv_w, v_w_proj_ssm, v_w_proj_conv, v_w_out, v_w_ff1, v_w_ff2, v_final_g):
    Bl, L, D = x.shape
    N = Bl * L
    Ds = Dc = D // 2
    G, H, P = Ds // SSM_GROUP, SSM_GROUP, SSM_STATE
    GP = G * P
    NP = GP // LANES
    nb = _s5_dims(Bl, L, Ds)[4]
    IN = Ds + 3 * Dc + 2 * D
    ax, ay, ac = _mesh_pos()
    q = 2 * ax + ay
    dev = 2 * q + ac

    big_names = ["w_in", "w_ff1", "w_ff2", "w_out", "w_proj_ssm", "w_proj_conv", "w_glu"]
    big_w = dict(w_in=w_in[0], w_ff1=w_ff1[0], w_ff2=w_ff2[0], w_out=w_out[0],
                 w_proj_ssm=w_proj_ssm[0], w_proj_conv=w_proj_conv[0], w_glu=w_glu[0])
    big_axis = dict(w_in=1, w_ff1=1, w_ff2=0, w_out=0, w_proj_ssm=1, w_proj_conv=1, w_glu=0)
    axes = [big_axis[k] for k in big_names]
    shard_shapes = [big_w[k].shape for k in big_names]
    pos = jnp.stack([q, ac]).astype(jnp.int32)
    own_only = dict(zip(big_names, _cast_into_full([big_w[k] for k in big_names], axes, pos, "cast_weights")))
    Dcs = conv_w.shape[2]
    first, (w_in_full,) = _allgather8(_pack([c, conv_w[0]]), "allgather_c_w_in",
                                      _carry_allgather([own_only["w_in"]], [big_axis["w_in"]], [big_w["w_in"].shape]))
    full = {"w_in": w_in_full}
    first = first.reshape(N_DEV, -1)
    c_all = first[:, :Bl * D].reshape(N_DEV * Bl, D)
    cw = first[0::2, Bl * D:Bl * D + 3 * Dcs].reshape(N_CHIPS, 3, Dcs).transpose(1, 0, 2).reshape(3, Dc)
    cw8 = _pad_rows(cw, SUBLANES)
    Ca = w_ada.shape[2]
    b_ada_sh = lax.dynamic_slice_in_dim(b_ada, q * Ca, Ca, axis=1)
    mod_part = _ada_fwd(c_all, w_ada[0], b_ada_sh)
    mod_g = _allgather8(mod_part, "allgather_mod")
    mod_all = mod_g[0::2].transpose(1, 0, 2).reshape(N_DEV * Bl, N_CHIPS * Ca)
    mod = lax.dynamic_slice_in_dim(mod_all, dev * Bl, Bl, axis=0)
    sh1, sc1, g1, sh2, sc2, g2 = [mod[:, k * D:(k + 1) * D].reshape(Bl, 1, D) for k in range(6)]

    ldt_c = log_dt[0].reshape(G, 1)
    bt_r = b_re[0].transpose(2, 0, 1)
    bt_i = b_im[0].transpose(2, 0, 1)
    lbr, lbi, bbt_r, bbt_i = _s5_disc(lam_re[0], lam_im[0], ldt_c, bt_r, bt_i)
    lam_r_p = lbr.reshape(NP, LANES)
    lam_i_p = lbi.reshape(NP, LANES)
    Bm_r = _diag_blocks_from_groups(bbt_r.transpose(1, 0, 2), nb).astype(BF16)
    Bm_i = _diag_blocks_from_groups(bbt_i.transpose(1, 0, 2), nb).astype(BF16)
    Cm_r = _diag_blocks_from_groups(c_re[0], nb).astype(BF16)
    Cm_i = _diag_blocks_from_groups(-c_im[0], nb).astype(BF16)

    x2 = x.reshape(N, D)
    mixer_w = ["w_out", "w_proj_ssm", "w_proj_conv", "w_glu"]
    mlp_w = ["w_ff1", "w_ff2"]
    layout = lambda ks: ([big_axis[k] for k in ks], [big_w[k].shape for k in ks])
    gather = lambda ks: _carry_allgather([own_only[k] for k in ks], *layout(ks))
    p2, gathered = _inproj_fwd(x2, sh1, sc1, norm1_g, full["w_in"], gather(mixer_w))
    full.update(zip(mixer_w, gathered))
    p3 = p2.reshape(Bl, L, IN)
    (Sr, Si, Sb_r, Sb_i, ylin3), gathered = _s5_fwd(p3, Bm_r, Bm_i, Cm_r, Cm_i, lam_r_p, lam_i_p, d_skip, gather(mlp_w))
    full.update(zip(mlp_w, gathered))
    ylin2 = ylin3.reshape(N, Ds)
    mix_w = (full["w_glu"], b_glu, cw8, full["w_proj_ssm"], full["w_proj_conv"], full["w_out"])
    x1 = _mix_fwd(p2, ylin2, x2, g1, *mix_w)

    (dx1, h2b, dab, sqb, dfb, loss_p, g_fg, g_n2, dsh2, dsc2, dg2) = _mlp_fwd_bwd(
        x1, loss_target.reshape(N, D), sh2, sc2, g2, norm2_g, final_g.reshape(1, D), full["w_ff1"], full["w_ff2"])
    g_full = {"w_ff1": _grad_w(h2b, dab, "grad_w_ff1")[0], "w_ff2": _grad_w(sqb, dfb, "grad_w_ff2")[0]}

    exchange = lambda ks: _carry_sibling_exchange([g_full[k] for k in ks], *layout(ks))
    presum = lambda ks, theirs: _presum([g_full[k] for k in ks], list(theirs), *layout(ks), pos, "presum_" + ks[0])
    chip_sum = lambda ks, parts, recv: _sum_chips(list(parts), list(recv), pos, "sum_" + ks[0])

    (dyl2, dp2, gw_out, gw_ps, gw_pc, gw_glu, gb_glu, gcw8, dg1), theirs_mlp = _mix_bwd(
        p2, ylin2, dx1, g1, *mix_w, exchange(mlp_w))
    g_full.update(w_out=gw_out, w_proj_ssm=gw_ps, w_proj_conv=gw_pc, w_glu=gw_glu)
    parts_mlp = presum(mlp_w, theirs_mlp)
    (dp3, dBm_r, dBm_i, dCm_r, dCm_i, dlam_r_p, dlam_i_p, g_dsk), extra = _s5_bwd(
        dyl2.reshape(Bl, L, Ds), p3, dp2.reshape(Bl, L, IN), Sr, Si, Sb_r, Sb_i, Bm_r, Bm_i, Cm_r, Cm_i, lam_r_p, lam_i_p,
        d_skip, _carry_join(_carry_chip_scatter(parts_mlp), exchange(mixer_w)))
    recv_mlp, theirs_mix = extra[:len(mlp_w)], extra[len(mlp_w):]
    halves_mlp = chip_sum(mlp_w, parts_mlp, recv_mlp)
    parts_mix = presum(mixer_w, theirs_mix)
    dp_all = dp3.reshape(N, IN)
    (grad_x2, hb, g_n1, dsh1, dsc1), _ = _inproj_bwd(x2, dx1, dp_all, sh1, sc1, norm1_g, full["w_in"], _NO_EXCHANGE)

    dbbt_r = _groups_from_diag_blocks(dBm_r, G, H, P).transpose(1, 0, 2)
    dbbt_i = _groups_from_diag_blocks(dBm_i, G, H, P).transpose(1, 0, 2)
    dc_re = _groups_from_diag_blocks(dCm_r, G, H, P)
    dc_im = -_groups_from_diag_blocks(dCm_i, G, H, P)
    dmod = jnp.concatenate([dsh1, dsc1, dg1, dsh2, dsc2, dg2], axis=-1).reshape(Bl, 6 * D)
    small = [g_n1, g_n2, g_fg, g_dsk, gb_glu, gcw8[:3], dlam_r_p, dlam_i_p, dbbt_r, dbbt_i, dc_re, dc_im]
    small_shapes = [v.shape for v in small]
    n_small = sum(int(v.size) for v in small)
    small_slots = _place_in_slot(_pack(small + [dmod]), jnp.reshape(dev, (1,)).astype(jnp.int32), "place_small")

    g_full["w_in"], extra = _grad_w(
        hb, dp_all, "grad_w_in",
        _carry_join(_carry_join(_carry_sibling_share(halves_mlp), _carry_chip_scatter(parts_mix)), _carry_allgather8(small_slots)))
    reduced = dict(zip(mlp_w, extra[:len(mlp_w)]))
    halves_mix = chip_sum(mixer_w, parts_mix, extra[len(mlp_w):len(mlp_w) + len(mixer_w)])
    gathered = extra[-1]
    theirs_in = _run_carried(exchange(["w_in"]), "rs_exchange_w_in")
    parts_in = presum(["w_in"], theirs_in)
    recv_in = _run_carried(_carry_chip_scatter(parts_in), "rs_scatter_w_in")
    halves_in = chip_sum(["w_in"], parts_in, recv_in)
    reduced.update(zip(mixer_w + ["w_in"], _run_carried(_carry_sibling_share(halves_mix + halves_in), "rs_share_rest")))

    red = _unpack(_sum_devices(gathered, "sum_small"), small_shapes)
    (r_n1, r_n2, r_fg, r_dsk, r_bglu, r_cw, r_dlr, r_dli, r_dbr, r_dbi, r_cre, r_cim) = red
    dmod_all = gathered.reshape(N_DEV, -1)[:, n_small:n_small + Bl * 6 * D].reshape(N_DEV * Bl, 6 * D)
    gw_ada, gb_ada = _ada_bwd(c_all, lax.dynamic_slice_in_dim(dmod_all, q * Ca, Ca, axis=1), dmod_all)
    g_lr, g_li, g_ldt, g_bt_r, g_bt_i = _s5_disc_bwd(lam_re[0], lam_im[0], ldt_c, bt_r, bt_i,
                                                   r_dlr.reshape(G, P), r_dli.reshape(G, P), r_dbr, r_dbi)

    grads = dict(
        norm1_g=r_n1, norm2_g=r_n2, w_ada=gw_ada, b_ada=gb_ada, lam_re=g_lr, lam_im=g_li, log_dt=g_ldt.reshape(1, G),
        b_re=g_bt_r.transpose(1, 2, 0), b_im=g_bt_i.transpose(1, 2, 0), c_re=r_cre, c_im=r_cim, d_skip=r_dsk,
        b_glu=r_bglu, conv_w=lax.dynamic_slice_in_dim(r_cw, q * Dcs, Dcs, axis=1), final_g=r_fg, **reduced)
    weights = dict(norm1_g=norm1_g, norm2_g=norm2_g, w_ada=w_ada, b_ada=b_ada, w_in=w_in, lam_re=lam_re, lam_im=lam_im,
                   log_dt=log_dt, b_re=b_re, b_im=b_im, c_re=c_re, c_im=c_im, d_skip=d_skip, w_glu=w_glu, b_glu=b_glu,
                   conv_w=conv_w, w_proj_ssm=w_proj_ssm, w_proj_conv=w_proj_conv, w_out=w_out, w_ff1=w_ff1, w_ff2=w_ff2,
                   final_g=final_g)
    m_in = dict(norm1_g=m_norm1_g, norm2_g=m_norm2_g, w_ada=m_w_ada, b_ada=m_b_ada, w_in=m_w_in, lam_re=m_lam_re,
                lam_im=m_lam_im, log_dt=m_log_dt, b_re=m_b_re, b_im=m_b_im, c_re=m_c_re, c_im=m_c_im, d_skip=m_d_skip,
                w_glu=m_w_glu, b_glu=m_b_glu, conv_w=m_conv_w, w_proj_ssm=m_w_proj_ssm, w_proj_conv=m_w_proj_conv,
                w_out=m_w_out, w_ff1=m_w_ff1, w_ff2=m_w_ff2, final_g=m_final_g)
    v_in = dict(norm1_g=v_norm1_g, norm2_g=v_norm2_g, w_ada=v_w_ada, b_ada=v_b_ada, w_in=v_w_in, lam_re=v_lam_re,
                lam_im=v_lam_im, log_dt=v_log_dt, b_re=v_b_re, b_im=v_b_im, c_re=v_c_re, c_im=v_c_im, d_skip=v_d_skip,
                w_glu=v_w_glu, b_glu=v_b_glu, conv_w=v_conv_w, w_proj_ssm=v_w_proj_ssm, w_proj_conv=v_w_proj_conv,
                w_out=v_w_out, w_ff1=v_w_ff1, w_ff2=v_w_ff2, final_g=v_final_g)
    names = list(weights)
    grads = {k: grads[k].reshape(weights[k].shape) for k in names}

    big_upd = big_names + ["w_ada"]
    delta, new_m, new_v = {}, {}, {}
    flat2 = lambda a: a.reshape(-1, a.shape[-1])
    d_, m_, v_, g_ = _adamw([flat2(weights[k]) for k in big_upd], [flat2(grads[k]) for k in big_upd],
                            [flat2(m_in[k]) for k in big_upd], [flat2(v_in[k]) for k in big_upd], "adamw_big")
    for k, dd, mm, vv, gg in zip(big_upd, d_, m_, v_, g_):
        shp = weights[k].shape
        delta[k], new_m[k], new_v[k], grads[k] = dd.reshape(shp), mm.reshape(shp), vv.reshape(shp), gg.reshape(shp)
    small_upd = [k for k in names if k not in big_upd]
    d_, m_, v_ = _adamw_many([flat2(weights[k]) for k in small_upd], [flat2(grads[k]) for k in small_upd],
                             [flat2(m_in[k]) for k in small_upd], [flat2(v_in[k]) for k in small_upd])
    for k, dd, mm, vv in zip(small_upd, d_, m_, v_):
        shp = weights[k].shape
        delta[k], new_m[k], new_v[k] = dd.reshape(shp), mm.reshape(shp), vv.reshape(shp)

    loss = lax.psum(loss_p[0, 0], ("x", "y", "c"))
    grad_x = grad_x2.reshape(Bl, L, D)
    return (loss, grad_x, *[grads[k] for k in names], *[delta[k] for k in names],
            *[new_m[k] for k in names], *[new_v[k] for k in names])
```

```python
import functools
from typing import Callable, NamedTuple

import jax
import jax.numpy as jnp
from jax import lax
from jax.experimental import pallas as pl
from jax.experimental.pallas import tpu as pltpu

F32 = jnp.float32
BF16 = jnp.bfloat16
MESH = pl.DeviceIdType.MESH
N_CHIPS = 4
N_DEV = 8
LANES = 128
SUBLANES = 8
V7X_VMEM_BYTES = 64 * 1024 * 1024
VMEM_LIMIT = V7X_VMEM_BYTES - 6 * 1024 * 1024
SSM_GROUP = 16
SSM_STATE = 64
S5_ROW_PAD = 4
HALO_ROWS = 16
RMS_EPS = 1e-6
ADAM_LR, ADAM_B1, ADAM_B2, ADAM_EPS, ADAM_WD, ADAM_STEP = 0.001, 0.9, 0.999, 1e-08, 0.01, 10

ANY = pl.BlockSpec(memory_space=pl.ANY)
VMEM_SPEC = pl.BlockSpec(memory_space=pltpu.VMEM)


def _cparams(vmem_limit_bytes=VMEM_LIMIT, **kw):
    return pltpu.CompilerParams(vmem_limit_bytes=vmem_limit_bytes, **kw)


def _dot(a, b):
    return jnp.dot(a, b, preferred_element_type=F32)


def _dot_nt(a, b):
    return lax.dot_general(a, b, (((1,), (1,)), ((), ())), preferred_element_type=F32)


def _dot_tn(a, b):
    return lax.dot_general(a, b, (((0,), (0,)), ((), ())), preferred_element_type=F32)


def _mesh_pos():
    return lax.axis_index("x"), lax.axis_index("y"), lax.axis_index("c")


def _allgather8(v, name, carried=None):
    r, c = v.shape

    def body(x_ref, out_ref, send_sems, recv_sems, local_sem):
        x, y, cc = _mesh_pos()
        me, sibling = (x, y, cc), (x, y, 1 - cc)
        chips = [(1 - x, y), (x, 1 - y), (1 - x, 1 - y)]

        def slot(px, py, pc):
            return out_ref.at[4 * px + 2 * py + pc]

        def copy(k, block, to, src=None):
            return pltpu.make_async_remote_copy(
                src_ref=slot(*block) if src is None else src, dst_ref=slot(*block),
                send_sem=send_sems.at[k], recv_sem=recv_sems.at[k], device_id=to, device_id_type=MESH)

        mine = pltpu.make_async_copy(x_ref, slot(*me), local_sem)
        mine.start()
        first = [copy(0, me, sibling, src=x_ref)]
        first += [copy(1 + j, me, (*chip, cc), src=x_ref) for j, chip in enumerate(chips)]
        for cp in first:
            cp.start()
        passed = [copy(4 + j, (*chip, cc), sibling) for j, chip in enumerate(chips)]
        for j, chip in enumerate(chips):
            copy(1 + j, (*chip, cc), me).wait_recv()
            passed[j].start()
        copy(0, sibling, me).wait_recv()
        for j, chip in enumerate(chips):
            copy(4 + j, (*chip, 1 - cc), me).wait_recv()
        for cp in first + passed:
            cp.wait_send()
        mine.wait()

    sems = [pltpu.SemaphoreType.DMA((7,)), pltpu.SemaphoreType.DMA((7,)), pltpu.SemaphoreType.DMA]
    out_shape = jax.ShapeDtypeStruct((N_DEV, r, c), v.dtype)
    if carried is None:
        return pl.pallas_call(body, out_shape=out_shape, in_specs=[VMEM_SPEC], out_specs=VMEM_SPEC,
                              scratch_shapes=sems, name=name)(v)
    (out,), extra = _call_carrying(body, carried, (v,), out_shape=[out_shape], in_specs=[VMEM_SPEC],
                                   out_specs=[VMEM_SPEC], scratch_shapes=sems, name=name)
    return out, extra


def _shard_region(ref, axis, shard_shape, q, half):
    R, C = shard_shape
    r0, nr = (0, R) if half is None else (half * (R // 2), R // 2)
    if axis == 1:
        return ref.at[pl.ds(r0, nr), pl.ds(q * C, C)]
    return ref.at[pl.ds(q * R + r0, nr), :]


class _Carried(NamedTuple):
    inputs: tuple
    out_shapes: tuple
    aliases: dict
    sems: tuple
    steps: Callable


def _carry_join(a, b):
    na_i, na_o, na_s = len(a.inputs), len(a.out_shapes), len(a.sems)

    def steps(ins, outs, sems):
        sa, fa = a.steps(ins[:na_i], outs[:na_o], sems[:na_s])
        sb, fb = b.steps(ins[na_i:], outs[na_o:], sems[na_s:])


        def start():
            sa()
            sb()

        def finish():
            fa()
            fb()

        return start, finish

    aliases = dict(a.aliases)
    aliases.update({na_i + i: na_o + o for i, o in b.aliases.items()})
    return _Carried(a.inputs + b.inputs, a.out_shapes + b.out_shapes, aliases, a.sems + b.sems, steps)


def _call_carrying(body, carried, args, *, out_shape, in_specs, out_specs, scratch_shapes=(), grid=None, aliases=None,
                   name, **kw):
    n_in, n_out, n_sc = len(in_specs), len(out_specs), len(scratch_shapes)
    n_ci, n_co = len(carried.inputs), len(carried.out_shapes)

    def wrapped(*refs):
        ins, refs = refs[:n_in], refs[n_in:]
        c_ins, refs = refs[:n_ci], refs[n_ci:]
        outs, refs = refs[:n_out], refs[n_out:]
        c_outs, refs = refs[:n_co], refs[n_co:]
        scratch, c_sems = refs[:n_sc], refs[n_sc:]
        fns = carried.steps(c_ins, c_outs, c_sems)
        start, finish = fns[0], fns[-1]
        relay = fns[1] if len(fns) == 3 else None
        if grid is None:
            start()
            body(*ins, *outs, *scratch)
            if relay is not None:
                relay()
            finish()
        else:
            ids = [pl.program_id(d) for d in range(len(grid))]
            first = functools.reduce(jnp.logical_and, [i == 0 for i in ids])
            last = functools.reduce(jnp.logical_and, [i == g - 1 for i, g in zip(ids, grid)])
            pl.when(first)(start)
            body(*ins, *outs, *scratch)
            if relay is not None:
                pl.when(ids[0] == (3 * grid[0]) // 4 if len(grid) == 1 else last)(relay)
            pl.when(last)(finish)

    if grid is not None:
        kw["grid"] = grid
    io_aliases = dict(aliases or {})
    io_aliases.update({n_in + i: n_out + o for i, o in carried.aliases.items()})
    res = pl.pallas_call(
        wrapped, out_shape=list(out_shape) + list(carried.out_shapes),
        in_specs=list(in_specs) + [ANY] * n_ci, out_specs=list(out_specs) + [ANY] * n_co,
        scratch_shapes=list(scratch_shapes) + list(carried.sems),
        input_output_aliases=io_aliases, name=name, **kw,
    )(*args, *carried.inputs)
    return res[:n_out], res[n_out:]


def _run_carried(carried, name):
    return _call_carrying(lambda: None, carried, (), out_shape=(), in_specs=(), out_specs=(), name=name)[1]


def _place_in_slot(v, dev_arr, name):
    r, c = v.shape

    def body(d_ref, v_ref, o_ref):
        o_ref[...] = v_ref[...]

    return pl.pallas_call(
        body, out_shape=jax.ShapeDtypeStruct((N_DEV, r, c), v.dtype),
        grid_spec=pltpu.PrefetchScalarGridSpec(
            num_scalar_prefetch=1, grid=(1,), in_specs=[pl.BlockSpec((r, c), lambda i, d: (0, 0))],
            out_specs=pl.BlockSpec((None, r, c), lambda i, d: (d[0], 0, 0))),
        name=name)(dev_arr, v)


def _carry_allgather8(buf):
    def steps(ins, outs, sems):
        send_s, recv_s = sems
        out = outs[0]
        x, y, cc = _mesh_pos()
        me, sibling = (x, y, cc), (x, y, 1 - cc)
        chips = [(1 - x, y), (x, 1 - y), (1 - x, 1 - y)]

        def copy(k, block, to):
            px, py, pc = block
            slot = out.at[4 * px + 2 * py + pc]
            return pltpu.make_async_remote_copy(src_ref=slot, dst_ref=slot, send_sem=send_s.at[k], recv_sem=recv_s.at[k],
                                                device_id=to, device_id_type=MESH)

        first = [copy(0, me, sibling)] + [copy(1 + j, me, (*chip, cc)) for j, chip in enumerate(chips)]
        passed = [copy(4 + j, (*chip, cc), sibling) for j, chip in enumerate(chips)]

        def start():
            for cp in first:
                cp.start()

        def finish():
            for j, chip in enumerate(chips):
                copy(1 + j, (*chip, cc), me).wait_recv()
                passed[j].start()
            copy(0, sibling, me).wait_recv()
            for j, chip in enumerate(chips):
                copy(4 + j, (*chip, 1 - cc), me).wait_recv()
            for cp in first + passed:
                cp.wait_send()

        return start, finish

    return _Carried((buf,), (jax.ShapeDtypeStruct(buf.shape, buf.dtype),), {0: 0}, (pltpu.SemaphoreType.DMA((7,)),) * 2, steps)


def _carry_allgather(fulls, axes, shapes):
    n = len(fulls)
    return _Carried(tuple(fulls), tuple(jax.ShapeDtypeStruct(f.shape, f.dtype) for f in fulls),
                    {i: i for i in range(n)}, (pltpu.SemaphoreType.DMA((3 * n,)),) * 4,
                    lambda ins, outs, sems: _allgather_weights_steps(outs, axes, shapes, *sems))


def _allgather_weights_steps(outs, axes, shapes, send_s, recv_s, fsend_s, frecv_s):
    n = len(outs)
    x, y, c = _mesh_pos()
    q = 2 * x + y
    sibling = (x, y, 1 - c)
    chips = [(1 - x, y), (x, 1 - y), (1 - x, 1 - y)]

    def region(i, qq, half):
        return _shard_region(outs[i], axes[i], shapes[i], qq, half)

    def remote(src, dst, ss, rs, to):
        return pltpu.make_async_remote_copy(src_ref=src, dst_ref=dst, send_sem=ss, recv_sem=rs,
                                            device_id=to, device_id_type=MESH)

    def ici(i, j, qq):
        cx, cy = chips[j]
        reg = region(i, qq, c)
        return remote(reg, reg, send_s.at[3 * i + j], recv_s.at[3 * i + j], (cx, cy, c))

    def d2d(i, j, half):
        cx, cy = chips[j]
        reg = region(i, 2 * cx + cy, half)
        return remote(reg, reg, fsend_s.at[3 * i + j], frecv_s.at[3 * i + j], sibling)

    def start():
        for i in range(n):
            for j in range(3):
                ici(i, j, q).start()

    def relay():
        for i in range(n):
            for j, (cx, cy) in enumerate(chips):
                ici(i, j, 2 * cx + cy).wait_recv()
                d2d(i, j, c).start()

    def finish():
        for i in range(n):
            for j in range(3):
                d2d(i, j, 1 - c).wait_recv()
        for i in range(n):
            for j in range(3):
                ici(i, j, q).wait_send()
                d2d(i, j, c).wait_send()

    return start, relay, finish


def _carry_sibling_exchange(grads, axes, shapes):
    n = len(grads)

    def steps(ins, theirs, sems):
        send_s, recv_s = sems
        x, y, c = _mesh_pos()

        def copies():
            return [pltpu.make_async_remote_copy(
                src_ref=_shard_region(ins[i], axes[i], shapes[i], qq, 1 - c), dst_ref=theirs[i].at[qq],
                send_sem=send_s.at[N_CHIPS * i + qq], recv_sem=recv_s.at[N_CHIPS * i + qq],
                device_id=(x, y, 1 - c), device_id_type=MESH) for i in range(n) for qq in range(N_CHIPS)]

        def start():
            for cp in copies():
                cp.start()

        def finish():
            for cp in copies():
                cp.wait()

        return start, finish

    stacked = tuple(jax.ShapeDtypeStruct((N_CHIPS, R // 2, C), F32) for (R, C) in shapes)
    return _Carried(tuple(grads), stacked, {}, (pltpu.SemaphoreType.DMA((N_CHIPS * n,)),) * 2, steps)


def _carry_chip_scatter(parts):
    n = len(parts)

    def steps(ins, outs, sems):
        send_s, recv_s = sems
        x, y, c = _mesh_pos()
        chips = [(1 - x, y), (x, 1 - y), (1 - x, 1 - y)]

        def copies():
            return [pltpu.make_async_remote_copy(
                src_ref=ins[i].at[2 * cx + cy], dst_ref=outs[i].at[j],
                send_sem=send_s.at[3 * i + j], recv_sem=recv_s.at[3 * i + j],
                device_id=(cx, cy, c), device_id_type=MESH) for i in range(n) for j, (cx, cy) in enumerate(chips)]

        def start():
            for cp in copies():
                cp.start()

        def finish():
            for cp in copies():
                cp.wait()

        return start, finish

    return _Carried(tuple(parts), tuple(jax.ShapeDtypeStruct((3,) + p.shape[1:], p.dtype) for p in parts), {},
                    (pltpu.SemaphoreType.DMA((3 * n,)),) * 2, steps)


def _carry_sibling_share(fulls):
    n = len(fulls)

    def steps(ins, outs, sems):
        send_s, recv_s = sems
        x, y, c = _mesh_pos()

        def copy(i, half):
            rh = fulls[i].shape[0] // 2
            rows = outs[i].at[pl.ds(half * rh, rh), :]
            return pltpu.make_async_remote_copy(src_ref=rows, dst_ref=rows, send_sem=send_s.at[i], recv_sem=recv_s.at[i],
                                                device_id=(x, y, 1 - c), device_id_type=MESH)

        def start():
            for i in range(n):
                copy(i, c).start()

        def finish():
            for i in range(n):
                copy(i, 1 - c).wait_recv()
                copy(i, c).wait_send()

        return start, finish

    return _Carried(tuple(fulls), tuple(jax.ShapeDtypeStruct(f.shape, f.dtype) for f in fulls),
                    {i: i for i in range(n)}, (pltpu.SemaphoreType.DMA((n,)),) * 2, steps)


def _row_block(rows, target=256):
    return target if rows % target == 0 else rows


BF16_TILE_ROWS = 16


def _common_steps(rows, most=8):
    ns = most
    while ns > 1 and any(r % (ns * BF16_TILE_ROWS) for r in rows):
        ns //= 2
    return ns


def _cast_into_full(ws, axes, pos, name):
    n = len(ws)
    ns = _common_steps([w.shape[0] for w in ws])
    in_specs, out_specs, out_shape = [], [], []
    for w, axis in zip(ws, axes):
        R, C = w.shape
        in_specs.append(pl.BlockSpec((R // ns, C), lambda i, s: (i, 0)))
        if axis == 1:
            out_shape.append(jax.ShapeDtypeStruct((R, N_CHIPS * C), BF16))
            out_specs.append(pl.BlockSpec((R // ns, C), lambda i, s: (i, s[0])))
        else:
            out_shape.append(jax.ShapeDtypeStruct((N_CHIPS * R, C), BF16))
            out_specs.append(pl.BlockSpec((R // ns, C), lambda i, s: (s[0] * ns + i, 0)))

    def body(s_ref, *refs):
        for k in range(n):
            refs[n + k][...] = refs[k][...].astype(BF16)

    return pl.pallas_call(
        body, out_shape=out_shape,
        grid_spec=pltpu.PrefetchScalarGridSpec(num_scalar_prefetch=1, grid=(ns,), in_specs=in_specs, out_specs=out_specs),
        compiler_params=_cparams(), name=name)(pos, *ws)


def _presum(gs, theirs, axes, shapes, pos, name):
    n = len(gs)
    ns = _common_steps([R // 2 for R, _ in shapes], most=2)
    in_specs, t_specs, out_shape = [], [], []
    for (R, C), axis in zip(shapes, axes):
        rb = R // 2 // ns
        if axis == 1:
            in_specs.append(pl.BlockSpec((rb, C), lambda k, i, s: (s[1] * ns + i, k)))
        else:
            in_specs.append(pl.BlockSpec((rb, C), lambda k, i, s: (k * 2 * ns + s[1] * ns + i, 0)))
        t_specs.append(pl.BlockSpec((None, rb, C), lambda k, i, s: (k, i, 0)))
        out_shape.append(jax.ShapeDtypeStruct((N_CHIPS, R // 2, C), BF16))

    def body(s_ref, *refs):
        for k in range(n):
            refs[2 * n + k][...] = (refs[k][...] + refs[n + k][...]).astype(BF16)

    return pl.pallas_call(
        body, out_shape=out_shape,
        grid_spec=pltpu.PrefetchScalarGridSpec(num_scalar_prefetch=1, grid=(N_CHIPS, ns), in_specs=in_specs + t_specs,
                                               out_specs=t_specs),
        compiler_params=_cparams(), name=name)(pos, *gs, *theirs)


def _sum_chips(owns, recvs, pos, name):
    n = len(owns)
    ns = _common_steps([o.shape[1] for o in owns], most=2)
    o_specs, r_specs, out_specs, out_shape = [], [], [], []
    for o in owns:
        _, Rh, C = o.shape
        rb = Rh // ns
        o_specs.append(pl.BlockSpec((None, rb, C), lambda i, s: (s[0], i, 0)))
        r_specs.append(pl.BlockSpec((3, rb, C), lambda i, s: (0, i, 0)))
        out_specs.append(pl.BlockSpec((rb, C), lambda i, s: (s[1] * ns + i, 0)))
        out_shape.append(jax.ShapeDtypeStruct((2 * Rh, C), F32))

    def body(s_ref, *refs):
        for k in range(n):
            acc = refs[k][...].astype(F32)
            for j in range(3):
                acc = acc + refs[n + k][j].astype(F32)
            refs[2 * n + k][...] = acc

    return pl.pallas_call(
        body, out_shape=out_shape,
        grid_spec=pltpu.PrefetchScalarGridSpec(num_scalar_prefetch=1, grid=(ns,), in_specs=o_specs + r_specs,
                                               out_specs=out_specs),
        compiler_params=_cparams(), name=name)(pos, *owns, *recvs)


def _sum_devices(parts, name):
    K, R, C = parts.shape

    def body(p_ref, o_ref):
        acc = p_ref[0]
        for k in range(1, K):
            acc = acc + p_ref[k]
        o_ref[...] = acc

    return pl.pallas_call(body, out_shape=jax.ShapeDtypeStruct((R, C), F32), name=name)(parts)


def _adamw_math(w, g, m, v):
    nm = ADAM_B1 * m + (1.0 - ADAM_B1) * g
    nv = ADAM_B2 * v + (1.0 - ADAM_B2) * (g * g)
    m_hat = nm / (1.0 - ADAM_B1 ** ADAM_STEP)
    v_hat = nv / (1.0 - ADAM_B2 ** ADAM_STEP)
    return -ADAM_LR * (m_hat / (jnp.sqrt(v_hat) + ADAM_EPS) + ADAM_WD * w), nm, nv


def _adamw_many(ws, gs, ms, vs):
    n = len(ws)

    def body(*refs):
        for k in range(n):
            w, g, m, v = (refs[j * n + k][...] for j in range(4))
            for j, val in enumerate(_adamw_math(w, g, m, v)):
                refs[(4 + j) * n + k][...] = val

    shapes = [jax.ShapeDtypeStruct(w.shape, F32) for w in ws]
    res = pl.pallas_call(body, out_shape=shapes * 3, compiler_params=_cparams(), name="adamw_small")(*ws, *gs, *ms, *vs)
    return res[:n], res[n:2 * n], res[2 * n:]


def _adamw(ws, gs, ms, vs, name):
    n = len(ws)
    ns = _common_steps([w.shape[0] for w in ws])
    specs = [pl.BlockSpec((w.shape[0] // ns, w.shape[1]), lambda i: (i, 0)) for w in ws]

    def body(*refs):
        for k in range(n):
            w, g, m, v = (refs[j * n + k][...] for j in range(4))
            for j, val in enumerate(_adamw_math(w, g, m, v) + (g,)):
                refs[(4 + j) * n + k][...] = val

    shapes = [jax.ShapeDtypeStruct(w.shape, F32) for w in ws]
    res = pl.pallas_call(body, out_shape=shapes * 4, grid=(ns,), in_specs=specs * 4, out_specs=specs * 4,
                         compiler_params=_cparams(), name=name)(*ws, *gs, *ms, *vs)
    return res[:n], res[n:2 * n], res[2 * n:3 * n], res[3 * n:]


def _silu(v):
    return v * jax.nn.sigmoid(v)


def _ada_fwd(c_all, w_sh, b_sh):
    S, D = c_all.shape
    Ca = w_sh.shape[1]
    cb = 512 if Ca % 512 == 0 else Ca

    def body(c_ref, w_ref, b_ref, o_ref):
        act = _silu(c_ref[...]).astype(BF16)
        o_ref[...] = _dot(act, w_ref[...].astype(BF16)) + b_ref[...]

    return pl.pallas_call(
        body, out_shape=jax.ShapeDtypeStruct((S, Ca), F32), grid=(Ca // cb,),
        in_specs=[pl.BlockSpec((S, D), lambda j: (0, 0)), pl.BlockSpec((D, cb), lambda j: (0, j)),
                  pl.BlockSpec((1, cb), lambda j: (0, j))],
        out_specs=pl.BlockSpec((S, cb), lambda j: (0, j)), name="ada_fwd")(c_all, w_sh, b_sh)


def _ada_bwd(c_all, dmod_sh, dmod_all):
    S, D = c_all.shape
    Ca = dmod_sh.shape[1]
    C6 = dmod_all.shape[1]

    def body(c_ref, ds_ref, da_ref, gw_ref, gb_ref):
        act = _silu(c_ref[...]).astype(BF16)
        gw_ref[...] = _dot_tn(act, ds_ref[...].astype(BF16))
        gb_ref[...] = jnp.sum(da_ref[...], axis=0, keepdims=True)

    return pl.pallas_call(
        body, out_shape=[jax.ShapeDtypeStruct((D, Ca), F32), jax.ShapeDtypeStruct((1, C6), F32)],
        compiler_params=_cparams(), name="ada_bwd")(c_all, dmod_sh, dmod_all)


def _rms_fwd(xv):
    r = lax.rsqrt(jnp.mean(xv * xv, axis=-1, keepdims=True) + RMS_EPS)
    return xv * r, r


def _rms_bwd(dxh, xh, r):
    return r * (dxh - xh * jnp.mean(dxh * xh, axis=-1, keepdims=True))


def _const_spec(shape, single_buffer=False):
    nd = len(shape)
    if single_buffer:
        return pl.BlockSpec(shape, lambda *_: (0,) * nd, pipeline_mode=pl.Buffered(1))
    return pl.BlockSpec(shape, lambda *_: (0,) * nd)


def _seq_spec(D, bps, rev_blocks=None):
    if rev_blocks is None:
        return pl.BlockSpec((None, 1, D), lambda i: (i // bps, 0, 0))
    return pl.BlockSpec((None, 1, D), lambda i: ((rev_blocks - 1 - i) // bps, 0, 0))


def _inproj_fwd(x2, sh1, sc1, n1g, w_in, carried):
    N, D = x2.shape
    IN = w_in.shape[1]
    Bl = sh1.shape[0]
    TB = _row_block(N // Bl, 1024)
    bps = (N // Bl) // TB

    def body(x_ref, sh_ref, sc_ref, g_ref, w_ref, p_ref):
        xh, _ = _rms_fwd(x_ref[...])
        h = (xh * g_ref[...]) * (1.0 + sc_ref[...]) + sh_ref[...]
        p_ref[...] = _dot(h.astype(BF16), w_ref[...]).astype(BF16)

    (p2,), extra = _call_carrying(
        body, carried, (x2, sh1, sc1, n1g, w_in), out_shape=[jax.ShapeDtypeStruct((N, IN), BF16)], grid=(N // TB,),
        in_specs=[pl.BlockSpec((TB, D), lambda i: (i, 0)), _seq_spec(D, bps), _seq_spec(D, bps),
                  _const_spec((1, D)), _const_spec((D, IN), single_buffer=True)],
        out_specs=[pl.BlockSpec((TB, IN), lambda i: (i, 0))],
        compiler_params=_cparams(dimension_semantics=("arbitrary",)), name="inproj_fwd")
    return p2, extra


def _s5_dims(Bl, L, Ds):
    G = Ds // SSM_GROUP
    GP = G * SSM_STATE
    NP = GP // LANES
    T = min(64, L // 2)
    nb = 2 if (Ds // 2) % LANES == 0 else 1
    return G, GP, NP, T, nb


def _s5_disc_math(lr, li, ldt, bt_r, bt_i):
    dt = jnp.exp(ldt)
    er = jnp.exp(lr * dt)
    lbr = er * jnp.cos(li * dt)
    lbi = er * jnp.sin(li * dt)
    den = lr * lr + li * li
    fr = ((lbr - 1.0) * lr + lbi * li) / den
    fi = (lbi * lr - (lbr - 1.0) * li) / den
    return lbr, lbi, fr[None] * bt_r - fi[None] * bt_i, fr[None] * bt_i + fi[None] * bt_r


def _s5_disc(lr, li, ldt, bt_r, bt_i):
    def body(lr_ref, li_ref, ldt_ref, br_ref, bi_ref, o0, o1, o2, o3):
        res = _s5_disc_math(lr_ref[...], li_ref[...], ldt_ref[...], br_ref[...], bi_ref[...])
        for o, v in zip((o0, o1, o2, o3), res):
            o[...] = v

    S = jax.ShapeDtypeStruct
    return pl.pallas_call(body, out_shape=[S(lr.shape, F32)] * 2 + [S(bt_r.shape, F32)] * 2, name="s5_disc")(lr, li, ldt, bt_r, bt_i)


def _s5_disc_bwd(lr, li, ldt, bt_r, bt_i, dlbr, dlbi, dbbr, dbbi):
    def body(lr_ref, li_ref, ldt_ref, br_ref, bi_ref, g0, g1, g2, g3, o0, o1, o2, o3, o4):
        _, vjp = jax.vjp(_s5_disc_math, lr_ref[...], li_ref[...], ldt_ref[...], br_ref[...], bi_ref[...])
        res = vjp((g0[...], g1[...], g2[...], g3[...]))
        for o, v in zip((o0, o1, o2, o3, o4), res):
            o[...] = v

    S = jax.ShapeDtypeStruct
    return pl.pallas_call(body, out_shape=[S(lr.shape, F32)] * 2 + [S(ldt.shape, F32)] + [S(bt_r.shape, F32)] * 2,
                          name="s5_disc_bwd")(lr, li, ldt, bt_r, bt_i, dlbr, dlbi, dbbr, dbbi)


S5_SCAN_PANELS = 4
S5_SCAN_STEPS = 4


def _panel_scan(src_r, src_i, dst_r, dst_i, lr_ref, li_ref, car_r, car_i, NP, Bl, T, TP, adjoint):
    BT = Bl * TP
    PG = min(S5_SCAN_PANELS, NP)
    CH = S5_SCAN_STEPS
    for k0 in range(0, NP, PG):
        ks = list(range(k0, k0 + PG))
        lr = [jnp.broadcast_to(lr_ref[pl.ds(k, 1), :], (Bl, LANES)) for k in ks]
        li = [jnp.broadcast_to(li_ref[pl.ds(k, 1), :], (Bl, LANES)) for k in ks]

        def trip(cc, carry):
            t0 = (T // CH - 1 - cc) * CH if adjoint else cc * CH
            ts = [t0 + (CH - 1 - s if adjoint else s) for s in range(CH)]
            idx = [[pl.ds(k * BT + t, Bl, stride=TP) for t in ts] for k in ks]
            loaded = [[(src_r[ix, :], src_i[ix, :]) for ix in idx[j]] for j in range(PG)]
            results, new_carry = [], []
            for j in range(PG):
                ar, ai = carry[j]
                res = []
                for s in range(CH):
                    br, bi = loaded[j][s]
                    if adjoint:
                        ar, ai = br + lr[j] * ar + li[j] * ai, bi + lr[j] * ai - li[j] * ar
                    else:
                        ar, ai = lr[j] * ar - li[j] * ai + br, lr[j] * ai + li[j] * ar + bi
                    res.append((ar, ai))
                results.append(res)
                new_carry.append((ar, ai))
            for j in range(PG):
                for s in range(CH):
                    dst_r[idx[j][s], :] = results[j][s][0]
                    dst_i[idx[j][s], :] = results[j][s][1]
            return tuple(new_carry)

        init = tuple((car_r[pl.ds(k * SUBLANES, Bl), :], car_i[pl.ds(k * SUBLANES, Bl), :]) for k in ks)
        fin = lax.fori_loop(0, T // CH, trip, init, unroll=2)
        for j, k in enumerate(ks):
            car_r[pl.ds(k * SUBLANES, Bl), :] = fin[j][0]
            car_i[pl.ds(k * SUBLANES, Bl), :] = fin[j][1]


def _s5_fwd(p3, Bm_r, Bm_i, Cm_r, Cm_i, lam_r, lam_i, dsk, carried):
    Bl, L, _ = p3.shape
    Ds = dsk.shape[1]
    G, GP, NP, T, nb = _s5_dims(Bl, L, Ds)
    nT = L // T
    TP = T + S5_ROW_PAD
    BT = Bl * TP
    dsb, gpb, npb = Ds // nb, GP // nb, NP // nb

    def body(u_ref, br_ref, bi_ref, cr_ref, ci_ref, lr_ref, li_ref, dsk_ref, sr_ref, si_ref, sb_r, sb_i, y_ref,
             car_r, car_i, upad, ypad, bu_r, bu_i):
        i = pl.program_id(0)

        @pl.when(i == 0)
        def _():
            car_r[...] = jnp.zeros_like(car_r)
            car_i[...] = jnp.zeros_like(car_i)
            upad[...] = jnp.zeros_like(upad)

        zpad = jnp.zeros((S5_ROW_PAD, LANES), F32)
        for k in range(NP):
            for b in range(Bl):
                sr_ref[pl.ds(k * BT + b * TP + T, S5_ROW_PAD), :] = zpad
                si_ref[pl.ds(k * BT + b * TP + T, S5_ROW_PAD), :] = zpad
        for b in range(Bl):
            upad[pl.ds(b * TP, T), :] = u_ref[b].astype(F32)
        u = upad[...]
        ub = u.astype(BF16)
        for blk in range(nb):
            ubb = ub[:, blk * dsb:(blk + 1) * dsb]
            for bu_ref, b_ref in ((bu_r, br_ref), (bu_i, bi_ref)):
                res = _dot(ubb, b_ref[blk])
                for kk in range(npb):
                    k = blk * npb + kk
                    bu_ref[pl.ds(k * BT, BT), :] = res[:, kk * LANES:(kk + 1) * LANES]

        _panel_scan(bu_r, bu_i, sr_ref, si_ref, lr_ref, li_ref, car_r, car_i, NP, Bl, T, TP, adjoint=False)
        sb_r[...] = car_r[...]
        sb_i[...] = car_i[...]

        for blk in range(nb):
            s_r = jnp.concatenate([sr_ref[pl.ds((blk * npb + kk) * BT, BT), :] for kk in range(npb)], axis=1).astype(BF16)
            s_i = jnp.concatenate([si_ref[pl.ds((blk * npb + kk) * BT, BT), :] for kk in range(npb)], axis=1).astype(BF16)
            cols = slice(blk * dsb, (blk + 1) * dsb)
            ypad[:, cols] = _dot_nt(s_r, cr_ref[blk]) + _dot_nt(s_i, ci_ref[blk]) + dsk_ref[:, cols] * u[:, cols]
        for b in range(Bl):
            y_ref[b] = ypad[pl.ds(b * TP, T), :]

    S = jax.ShapeDtypeStruct
    state = S((nT, NP * BT, LANES), F32)
    bound = S((nT, NP * SUBLANES, LANES), F32)
    sspec = pl.BlockSpec((None, NP * BT, LANES), lambda i: (i, 0, 0))
    bspec = pl.BlockSpec((None, NP * SUBLANES, LANES), lambda i: (i, 0, 0))
    return _call_carrying(
        body, carried, (p3, Bm_r, Bm_i, Cm_r, Cm_i, lam_r, lam_i, dsk),
        out_shape=[state, state, bound, bound, S((Bl, L, Ds), F32)], grid=(nT,),
        in_specs=[pl.BlockSpec((Bl, T, Ds), lambda i: (0, i, 0)),
                  _const_spec((nb, dsb, gpb)), _const_spec((nb, dsb, gpb)),
                  _const_spec((nb, dsb, gpb)), _const_spec((nb, dsb, gpb)),
                  _const_spec((NP, LANES)), _const_spec((NP, LANES)), _const_spec((1, Ds))],
        out_specs=[sspec, sspec, bspec, bspec, pl.BlockSpec((Bl, T, Ds), lambda i: (0, i, 0))],
        scratch_shapes=[pltpu.VMEM((NP * SUBLANES, LANES), F32)] * 2 + [pltpu.VMEM((BT, Ds), F32)] * 2
        + [pltpu.VMEM((NP * BT, LANES), F32)] * 2,
        compiler_params=_cparams(dimension_semantics=("arbitrary",)), name="s5_fwd")


def _s5_bwd(dy3, p3, dp3, Sr, Si, Sb_r, Sb_i, Bm_r, Bm_i, Cm_r, Cm_i, lam_r, lam_i, dsk, carried):
    Bl, L, Ds = dy3.shape
    G, GP, NP, T, nb = _s5_dims(Bl, L, Ds)
    nT = L // T
    TP = T + S5_ROW_PAD
    BT = Bl * TP
    dsb, gpb, npb = Ds // nb, GP // nb, NP // nb

    def body(dy_ref, u_ref, dp_ref, sr_ref, si_ref, sbr_ref, sbi_ref, br_ref, bi_ref, cr_ref, ci_ref, lr_ref, li_ref, dsk_ref,
             du_ref, dbr_ref, dbi_ref, dcr_ref, dci_ref, dlr_ref, dli_ref, ddsk_ref,
             a_r, a_i, car_r, car_i, acc_r, acc_i, dypad, upad, dupad, q_r, q_i):
        i = pl.program_id(0)

        @pl.when(i == 0)
        def _():
            for ref in (car_r, car_i, acc_r, acc_i, dbr_ref, dbi_ref, dcr_ref, dci_ref, ddsk_ref, dypad, upad, a_r, a_i):
                ref[...] = jnp.zeros_like(ref)

        for b in range(Bl):
            dypad[pl.ds(b * TP, T), :] = dy_ref[b]
            upad[pl.ds(b * TP, T), :] = u_ref[b].astype(F32)
        dy = dypad[...]
        dyb = dy.astype(BF16)
        u = upad[...]
        ub = u.astype(BF16)
        for blk in range(nb):
            dyb_b = dyb[:, blk * dsb:(blk + 1) * dsb]
            for q_ref, c_ref in ((q_r, cr_ref), (q_i, ci_ref)):
                res = _dot(dyb_b, c_ref[blk])
                for kk in range(npb):
                    q_ref[pl.ds((blk * npb + kk) * BT, BT), :] = res[:, kk * LANES:(kk + 1) * LANES]

        _panel_scan(q_r, q_i, a_r, a_i, lr_ref, li_ref, car_r, car_i, NP, Bl, T, TP, adjoint=True)

        first_block = (i == nT - 1)
        for k in range(NP):
            rows = pl.ds(k * BT, BT)
            av_r, av_i = a_r[rows, :], a_i[rows, :]
            sp_r = pltpu.roll(sr_ref[rows, :], 1, 0)
            sp_i = pltpu.roll(si_ref[rows, :], 1, 0)
            acc = pl.ds(k * SUBLANES, SUBLANES)
            acc_r[acc, :] += jnp.sum((av_r * sp_r + av_i * sp_i).reshape(BT // SUBLANES, SUBLANES, LANES), axis=0)
            acc_i[acc, :] += jnp.sum((av_i * sp_r - av_r * sp_i).reshape(BT // SUBLANES, SUBLANES, LANES), axis=0)
            t0 = pl.ds(k * BT, Bl, stride=TP)
            a0_r, a0_i = a_r[t0, :], a_i[t0, :]
            brow = pl.ds(k * SUBLANES, Bl)
            sb_pr = jnp.where(first_block, 0.0, sbr_ref[brow, :])
            sb_pi = jnp.where(first_block, 0.0, sbi_ref[brow, :])
            acc_r[brow, :] += a0_r * sb_pr + a0_i * sb_pi
            acc_i[brow, :] += a0_i * sb_pr - a0_r * sb_pi

        ddsk_ref[...] += jnp.sum(dy * u, axis=0, keepdims=True)
        for blk in range(nb):
            cols = slice(blk * dsb, (blk + 1) * dsb)
            rows = [pl.ds((blk * npb + kk) * BT, BT) for kk in range(npb)]
            av_r = jnp.concatenate([a_r[r, :] for r in rows], axis=1).astype(BF16)
            av_i = jnp.concatenate([a_i[r, :] for r in rows], axis=1).astype(BF16)
            dupad[:, cols] = _dot_nt(av_r, br_ref[blk]) + _dot_nt(av_i, bi_ref[blk]) + dy[:, cols] * dsk_ref[:, cols]
            dbr_ref[blk] += _dot_tn(ub[:, cols], av_r)
            dbi_ref[blk] += _dot_tn(ub[:, cols], av_i)
            sv_r = jnp.concatenate([sr_ref[r, :] for r in rows], axis=1).astype(BF16)
            sv_i = jnp.concatenate([si_ref[r, :] for r in rows], axis=1).astype(BF16)
            dcr_ref[blk] += _dot_tn(dyb[:, cols], sv_r)
            dci_ref[blk] += _dot_tn(dyb[:, cols], sv_i)
        for b in range(Bl):
            du_ref[b] = dupad[pl.ds(b * TP, T), :].astype(BF16)

        @pl.when(i == nT - 1)
        def _():
            for k in range(NP):
                dlr_ref[pl.ds(k, 1), :] = jnp.sum(acc_r[pl.ds(k * SUBLANES, SUBLANES), :], axis=0, keepdims=True)
                dli_ref[pl.ds(k, 1), :] = jnp.sum(acc_i[pl.ds(k * SUBLANES, SUBLANES), :], axis=0, keepdims=True)

    S = jax.ShapeDtypeStruct
    rev = lambda i: nT - 1 - i
    sspec = pl.BlockSpec((None, NP * BT, LANES), lambda i: (rev(i), 0, 0))
    bspec = pl.BlockSpec((None, NP * SUBLANES, LANES), lambda i: (jnp.maximum(rev(i) - 1, 0), 0, 0))
    tspec = pl.BlockSpec((Bl, T, Ds), lambda i: (0, rev(i), 0))
    return _call_carrying(
        body, carried, (dy3, p3, dp3, Sr, Si, Sb_r, Sb_i, Bm_r, Bm_i, Cm_r, Cm_i, lam_r, lam_i, dsk),
        out_shape=[S(dp3.shape, dp3.dtype), S((nb, dsb, gpb), F32), S((nb, dsb, gpb), F32),
                   S((nb, dsb, gpb), F32), S((nb, dsb, gpb), F32), S((NP, LANES), F32), S((NP, LANES), F32), S((1, Ds), F32)],
        grid=(nT,),
        in_specs=[tspec, tspec, ANY, sspec, sspec, bspec, bspec,
                  _const_spec((nb, dsb, gpb)), _const_spec((nb, dsb, gpb)),
                  _const_spec((nb, dsb, gpb)), _const_spec((nb, dsb, gpb)),
                  _const_spec((NP, LANES)), _const_spec((NP, LANES)), _const_spec((1, Ds))],
        out_specs=[tspec, _const_spec((nb, dsb, gpb)), _const_spec((nb, dsb, gpb)),
                   _const_spec((nb, dsb, gpb)), _const_spec((nb, dsb, gpb)),
                   _const_spec((NP, LANES)), _const_spec((NP, LANES)), _const_spec((1, Ds))],
        aliases={2: 0},
        scratch_shapes=[pltpu.VMEM((NP * BT, LANES), F32)] * 2 + [pltpu.VMEM((NP * SUBLANES, LANES), F32)] * 4
        + [pltpu.VMEM((BT, Ds), F32)] * 3 + [pltpu.VMEM((NP * BT, LANES), F32)] * 2,
        compiler_params=_cparams(dimension_semantics=("arbitrary",)), name="s5_bwd")


def _mix_values(ylin, cb, cc, cx, gs, gc, halo_v, wglu, bglu, cw, wps, wpc, wout):
    yg, gelu_vjp = jax.vjp(jax.nn.gelu, ylin)
    sz = jax.nn.sigmoid(_dot(yg.astype(BF16), wglu) + bglu)
    ys = yg * sz
    v = cc * cx
    rows = lax.broadcasted_iota(jnp.int32, v.shape, 0)
    h6 = halo_v[HALO_ROWS - 2:HALO_ROWS - 1, :]
    h7 = halo_v[HALO_ROWS - 1:HALO_ROWS, :]
    v1 = jnp.where(rows == 0, h7, pltpu.roll(v, 1, 0))
    v2 = jnp.where(rows == 0, h6, jnp.where(rows == 1, h7, pltpu.roll(v, 2, 0)))
    cv = cw[0:1, :] * v2 + cw[1:2, :] * v1 + cw[2:3, :] * v
    yc = cb * cv
    ps = _dot(ys.astype(BF16), wps)
    pc = _dot(yc.astype(BF16), wpc)
    sgs = jax.nn.sigmoid(gs)
    sgc = jax.nn.sigmoid(gc)
    merged = sgs * ps + sgc * pc
    mo = _dot(merged.astype(BF16), wout)
    return dict(yg=yg, gelu_vjp=gelu_vjp, sz=sz, ys=ys, v=v, v1=v1, v2=v2, cv=cv, yc=yc, ps=ps, pc=pc,
                sgs=sgs, sgc=sgc, merged=merged, mo=mo)


def _mix_in_specs(TB, D, Ds, Dc, bps, blk):
    hb = TB // HALO_ROWS
    halo = lambda col: pl.BlockSpec((HALO_ROWS, Dc), lambda i: (jnp.maximum(blk(i) * hb - 1, 0), col))
    return [pl.BlockSpec((TB, Dc), lambda i: (blk(i), 1)), pl.BlockSpec((TB, Dc), lambda i: (blk(i), 2)),
            pl.BlockSpec((TB, Dc), lambda i: (blk(i), 3)), pl.BlockSpec((TB, D), lambda i: (blk(i), 2)),
            pl.BlockSpec((TB, D), lambda i: (blk(i), 3)), halo(2), halo(3),
            pl.BlockSpec((TB, Ds), lambda i: (blk(i), 0))]


def _mix_fwd(p2, ylin2, x2, g1, wglu, bglu, cw, wps, wpc, wout):
    N, D = x2.shape
    Ds = ylin2.shape[1]
    Dc = Ds
    Bl = g1.shape[0]
    TB = _row_block(N // Bl, 512)
    bps = (N // Bl) // TB

    def body(cb_ref, cc_ref, cx_ref, gs_ref, gc_ref, hcc_ref, hcx_ref, yl_ref, x_ref, g1_ref,
             wglu_ref, bglu_ref, cw_ref, wps_ref, wpc_ref, wout_ref, x1_ref):
        i = pl.program_id(0)
        f32 = lambda ref: ref[...].astype(F32)
        halo_v = jnp.where(i % bps == 0, 0.0, f32(hcc_ref) * f32(hcx_ref))
        f = _mix_values(yl_ref[...], f32(cb_ref), f32(cc_ref), f32(cx_ref), f32(gs_ref), f32(gc_ref), halo_v,
                        wglu_ref[...], bglu_ref[...], cw_ref[...], wps_ref[...], wpc_ref[...], wout_ref[...])
        x1_ref[...] = x_ref[...] + g1_ref[...] * f["mo"]

    return pl.pallas_call(
        body, out_shape=jax.ShapeDtypeStruct((N, D), F32), grid=(N // TB,),
        in_specs=_mix_in_specs(TB, D, Ds, Dc, bps, lambda i: i) + [
            pl.BlockSpec((TB, D), lambda i: (i, 0)), _seq_spec(D, bps),
            _const_spec((Ds, Ds)), _const_spec((1, Ds)), _const_spec((SUBLANES, Dc)),
            _const_spec((Ds, D)), _const_spec((Dc, D)), _const_spec((D, D))],
        out_specs=pl.BlockSpec((TB, D), lambda i: (i, 0)),
        compiler_params=_cparams(), name="mix_fwd",
    )(p2, p2, p2, p2, p2, p2, p2, ylin2, x2, g1, wglu, bglu, cw, wps, wpc, wout)


def _mix_bwd(p2, ylin2, dx1, g1, wglu, bglu, cw, wps, wpc, wout, carried):
    N, D = dx1.shape
    Ds = ylin2.shape[1]
    Dc = Ds
    IN = p2.shape[1]
    Bl = g1.shape[0]
    TB = _row_block(N // Bl, 512)
    bps = (N // Bl) // TB
    nblk = N // TB
    rev = lambda i: nblk - 1 - i

    def body(cb_ref, cc_ref, cx_ref, gs_ref, gc_ref, hcc_ref, hcx_ref, yl_ref, dx1_ref, g1_ref,
             wglu_ref, bglu_ref, cw_ref, wps_ref, wpc_ref, wout_ref,
             dyl_ref, dp_ref, gwout_ref, gwps_ref, gwpc_ref, gwglu_ref, gbglu_ref, gcw_ref, dg1_ref, nxt):
        i = pl.program_id(0)
        blk = rev(i)

        @pl.when(i == 0)
        def _():
            for ref in (gwout_ref, gwps_ref, gwpc_ref, gwglu_ref, gbglu_ref, gcw_ref):
                ref[...] = jnp.zeros_like(ref)

        @pl.when(i % bps == 0)
        def _():
            nxt[...] = jnp.zeros_like(nxt)
            dg1_ref[...] = jnp.zeros_like(dg1_ref)

        f32 = lambda ref: ref[...].astype(F32)
        cb, cc, cx = f32(cb_ref), f32(cc_ref), f32(cx_ref)
        halo_v = jnp.where(blk % bps == 0, 0.0, f32(hcc_ref) * f32(hcx_ref))
        wglu, wps, wpc, wout, cw = wglu_ref[...], wps_ref[...], wpc_ref[...], wout_ref[...], cw_ref[...]
        f = _mix_values(yl_ref[...], cb, cc, cx, f32(gs_ref), f32(gc_ref), halo_v, wglu, bglu_ref[...], cw, wps, wpc, wout)

        dx1v = dx1_ref[...]
        dg1_ref[...] += jnp.sum(dx1v * f["mo"], axis=0, keepdims=True)
        dmo = (g1_ref[...] * dx1v).astype(BF16)
        gwout_ref[...] += _dot_tn(f["merged"].astype(BF16), dmo)
        dmerged = _dot_nt(dmo, wout)
        dps = dmerged * f["sgs"]
        dpc = dmerged * f["sgc"]
        dgs = dmerged * f["ps"] * f["sgs"] * (1.0 - f["sgs"])
        dgc = dmerged * f["pc"] * f["sgc"] * (1.0 - f["sgc"])
        dpsb, dpcb = dps.astype(BF16), dpc.astype(BF16)
        gwps_ref[...] += _dot_tn(f["ys"].astype(BF16), dpsb)
        gwpc_ref[...] += _dot_tn(f["yc"].astype(BF16), dpcb)
        dys = _dot_nt(dpsb, wps)
        dyc = _dot_nt(dpcb, wpc)

        dcb = dyc * f["cv"]
        dcv = dyc * cb
        rows = lax.broadcasted_iota(jnp.int32, dcv.shape, 0)
        n0, n1 = nxt[0:1, :], nxt[1:2, :]
        d1 = jnp.where(rows == TB - 1, n0, pltpu.roll(dcv, TB - 1, 0))
        d2 = jnp.where(rows == TB - 2, n0, jnp.where(rows == TB - 1, n1, pltpu.roll(dcv, TB - 2, 0)))
        dv = cw[2:3, :] * dcv + cw[1:2, :] * d1 + cw[0:1, :] * d2
        nxt[0:2, :] = dcv[0:2, :]
        gcw_ref[0:1, :] += jnp.sum(dcv * f["v2"], axis=0, keepdims=True)
        gcw_ref[1:2, :] += jnp.sum(dcv * f["v1"], axis=0, keepdims=True)
        gcw_ref[2:3, :] += jnp.sum(dcv * f["v"], axis=0, keepdims=True)

        dz = dys * f["yg"] * f["sz"] * (1.0 - f["sz"])
        dzb = dz.astype(BF16)
        gwglu_ref[...] += _dot_tn(f["yg"].astype(BF16), dzb)
        gbglu_ref[...] += jnp.sum(dz, axis=0, keepdims=True)
        dyg = dys * f["sz"] + _dot_nt(dzb, wglu)
        dyl_ref[...] = f["gelu_vjp"](dyg)[0]

        dp_ref[:, Ds:Ds + Dc] = dcb.astype(BF16)
        dp_ref[:, Ds + Dc:Ds + 2 * Dc] = (dv * cx).astype(BF16)
        dp_ref[:, Ds + 2 * Dc:Ds + 3 * Dc] = (dv * cc).astype(BF16)
        dp_ref[:, Ds + 3 * Dc:Ds + 3 * Dc + D] = dgs.astype(BF16)
        dp_ref[:, Ds + 3 * Dc + D:IN] = dgc.astype(BF16)

    S = jax.ShapeDtypeStruct
    return _call_carrying(
        body, carried, (p2, p2, p2, p2, p2, p2, p2, ylin2, dx1, g1, wglu, bglu, cw, wps, wpc, wout),
        out_shape=[S((N, Ds), F32), S((N, IN), BF16), S((D, D), F32), S((Ds, D), F32), S((Dc, D), F32),
                   S((Ds, Ds), F32), S((1, Ds), F32), S((SUBLANES, Dc), F32), S((Bl, 1, D), F32)],
        grid=(nblk,),
        in_specs=_mix_in_specs(TB, D, Ds, Dc, bps, rev) + [
            pl.BlockSpec((TB, D), lambda i: (rev(i), 0)), _seq_spec(D, bps, nblk),
            _const_spec((Ds, Ds)), _const_spec((1, Ds)), _const_spec((SUBLANES, Dc)),
            _const_spec((Ds, D)), _const_spec((Dc, D)), _const_spec((D, D))],
        out_specs=[pl.BlockSpec((TB, Ds), lambda i: (rev(i), 0)), pl.BlockSpec((TB, IN), lambda i: (rev(i), 0)),
                   _const_spec((D, D)), _const_spec((Ds, D)), _const_spec((Dc, D)), _const_spec((Ds, Ds)),
                   _const_spec((1, Ds)), _const_spec((SUBLANES, Dc)), _seq_spec(D, bps, nblk)],
        scratch_shapes=[pltpu.VMEM((SUBLANES, Dc), F32)],
        compiler_params=_cparams(vmem_limit_bytes=V7X_VMEM_BYTES - 1024 * 1024, dimension_semantics=("arbitrary",)),
        name="mix_bwd")


def _mlp_fwd_bwd(x1, tgt, sh2, sc2, g2, n2g, fg, w1, w2):
    N, D = x1.shape
    Dff = w1.shape[1]
    Bl = sh2.shape[0]
    TB = _row_block(N // Bl)
    bps = (N // Bl) // TB

    def body(x1_ref, t_ref, sh_ref, sc_ref, g2_ref, n2_ref, fg_ref, w1_ref, w2_ref,
             dx1_ref, h2_ref, da_ref, sq_ref, df_ref, loss_ref, gfg_ref, gn2_ref, dsh_ref, dsc_ref, dg2_ref):
        i = pl.program_id(0)

        @pl.when(i == 0)
        def _():
            for ref in (loss_ref, gfg_ref, gn2_ref):
                ref[...] = jnp.zeros_like(ref)

        @pl.when(i % bps == 0)
        def _():
            for ref in (dsh_ref, dsc_ref, dg2_ref):
                ref[...] = jnp.zeros_like(ref)

        x1v = x1_ref[...]
        sc, g2v, n2 = sc_ref[...], g2_ref[...], n2_ref[...]
        xh2, r2 = _rms_fwd(x1v)
        xn2 = xh2 * n2
        h2 = (xn2 * (1.0 + sc) + sh_ref[...]).astype(BF16)
        a = _dot(h2, w1_ref[...])
        ra = jnp.maximum(a, 0.0)
        sq = (ra * ra).astype(BF16)
        fv = _dot(sq, w2_ref[...])
        x2 = x1v + g2v * fv
        xh3, r3 = _rms_fwd(x2)
        err = xh3 * fg_ref[...] - t_ref[...]
        loss_ref[...] += 0.5 * jnp.sum(jnp.mean(err * err, axis=-1, keepdims=True), axis=0, keepdims=True)
        dy = err * (1.0 / D)
        gfg_ref[...] += jnp.sum(dy * xh3, axis=0, keepdims=True)
        dx2 = _rms_bwd(dy * fg_ref[...], xh3, r3)
        dg2_ref[...] += jnp.sum(dx2 * fv, axis=0, keepdims=True)
        df = (g2v * dx2).astype(BF16)
        dsq = _dot_nt(df, w2_ref[...])
        da = (2.0 * ra * dsq).astype(BF16)
        dh2 = _dot_nt(da, w1_ref[...])
        dsh_ref[...] += jnp.sum(dh2, axis=0, keepdims=True)
        dsc_ref[...] += jnp.sum(dh2 * xn2, axis=0, keepdims=True)
        dxn2 = dh2 * (1.0 + sc)
        gn2_ref[...] += jnp.sum(dxn2 * xh2, axis=0, keepdims=True)
        dx1_ref[...] = dx2 + _rms_bwd(dxn2 * n2, xh2, r2)
        h2_ref[...] = h2
        da_ref[...] = da
        sq_ref[...] = sq
        df_ref[...] = df

    S = jax.ShapeDtypeStruct
    row = lambda w: pl.BlockSpec((TB, w), lambda i: (i, 0))
    return pl.pallas_call(
        body,
        out_shape=[S((N, D), F32), S((N, D), BF16), S((N, Dff), BF16), S((N, Dff), BF16), S((N, D), BF16),
                   S((1, 1), F32), S((1, D), F32), S((1, D), F32), S((Bl, 1, D), F32), S((Bl, 1, D), F32), S((Bl, 1, D), F32)],
        grid=(N // TB,),
        in_specs=[row(D), row(D), _seq_spec(D, bps), _seq_spec(D, bps), _seq_spec(D, bps),
                  _const_spec((1, D)), _const_spec((1, D)), _const_spec((D, Dff)), _const_spec((Dff, D))],
        out_specs=[row(D), row(D), row(Dff), row(Dff), row(D), _const_spec((1, 1)), _const_spec((1, D)), _const_spec((1, D)),
                   _seq_spec(D, bps), _seq_spec(D, bps), _seq_spec(D, bps)],
        compiler_params=_cparams(dimension_semantics=("arbitrary",)), name="mlp_fwd_bwd",
    )(x1, tgt, sh2, sc2, g2, n2g, fg, w1, w2)


_NO_EXCHANGE = _Carried((), (), {}, (), lambda ins, outs, sems: ((lambda: None), (lambda: None)))


def _grad_w(a, b, name, carried=_NO_EXCHANGE):
    N, K1 = a.shape
    K2 = b.shape[1]
    t1 = 1024 if K1 % 1024 == 0 else K1
    t2 = 1024 if K2 % 1024 == 0 else K2
    tn = 2048 if N % 2048 == 0 else N

    def body(a_ref, b_ref, o_ref):
        @pl.when(pl.program_id(2) == 0)
        def _():
            o_ref[...] = jnp.zeros_like(o_ref)

        o_ref[...] += _dot_tn(a_ref[...], b_ref[...])

    (g,), extra = _call_carrying(
        body, carried, (a, b), out_shape=[jax.ShapeDtypeStruct((K1, K2), F32)], grid=(K1 // t1, K2 // t2, N // tn),
        in_specs=[pl.BlockSpec((tn, t1), lambda i, j, k: (k, i)), pl.BlockSpec((tn, t2), lambda i, j, k: (k, j))],
        out_specs=[pl.BlockSpec((t1, t2), lambda i, j, k: (i, j))],
        compiler_params=_cparams(dimension_semantics=("arbitrary", "arbitrary", "arbitrary")), name=name)
    return g, extra


def _inproj_bwd(x2, dx1, dp, sh1, sc1, n1g, w_in, carried):
    N, D = x2.shape
    IN = w_in.shape[1]
    Bl = sh1.shape[0]
    TB = _row_block(N // Bl, 512)
    bps = (N // Bl) // TB

    def body(x_ref, dx1_ref, dp_ref, sh_ref, sc_ref, g_ref, w_ref, gx_ref, h_ref, gn1_ref, dsh_ref, dsc_ref):
        i = pl.program_id(0)

        @pl.when(i == 0)
        def _():
            gn1_ref[...] = jnp.zeros_like(gn1_ref)

        @pl.when(i % bps == 0)
        def _():
            dsh_ref[...] = jnp.zeros_like(dsh_ref)
            dsc_ref[...] = jnp.zeros_like(dsc_ref)

        sc, n1 = sc_ref[...], g_ref[...]
        xh, r = _rms_fwd(x_ref[...])
        xn = xh * n1
        h_ref[...] = (xn * (1.0 + sc) + sh_ref[...]).astype(BF16)
        dh = _dot_nt(dp_ref[...], w_ref[...])
        dsh_ref[...] += jnp.sum(dh, axis=0, keepdims=True)
        dsc_ref[...] += jnp.sum(dh * xn, axis=0, keepdims=True)
        dxn = dh * (1.0 + sc)
        gn1_ref[...] += jnp.sum(dxn * xh, axis=0, keepdims=True)
        gx_ref[...] = dx1_ref[...] + _rms_bwd(dxn * n1, xh, r)

    S = jax.ShapeDtypeStruct
    row = lambda w: pl.BlockSpec((TB, w), lambda i: (i, 0))
    return _call_carrying(
        body, carried, (x2, dx1, dp, sh1, sc1, n1g, w_in),
        out_shape=[S((N, D), F32), S((N, D), BF16), S((1, D), F32), S((Bl, 1, D), F32), S((Bl, 1, D), F32)],
        grid=(N // TB,),
        in_specs=[row(D), row(D), row(IN), _seq_spec(D, bps), _seq_spec(D, bps), _const_spec((1, D)), _const_spec((D, IN))],
        out_specs=[row(D), row(D), _const_spec((1, D)), _seq_spec(D, bps), _seq_spec(D, bps)],
        compiler_params=_cparams(dimension_semantics=("arbitrary",)), name="inproj_bwd")


def _diag_mask(gb, a, b):
    rows = lax.broadcasted_iota(jnp.int32, (gb * a, gb * b), 0) // a
    cols = lax.broadcasted_iota(jnp.int32, (gb * a, gb * b), 1) // b
    return (rows == cols).astype(F32)


def _diag_blocks_from_groups(m, nb):
    G, a, b = m.shape
    gb = G // nb
    return jnp.tile(m.reshape(nb, gb * a, b), (1, 1, gb)) * _diag_mask(gb, a, b)[None]


def _groups_from_diag_blocks(d, G, a, b):
    nb = d.shape[0]
    gb = G // nb
    picked = (d * _diag_mask(gb, a, b)[None]).reshape(nb, gb * a, gb, b)
    return jnp.sum(picked, axis=2).reshape(G, a, b)


def _pad_rows(v, rows):
    return jnp.concatenate([v, jnp.zeros((rows - v.shape[0],) + v.shape[1:], v.dtype)], axis=0)


def _pack(vs):
    flat = jnp.concatenate([v.reshape(-1) for v in vs])
    n = flat.shape[0]
    tile = SUBLANES * LANES
    npad = -(-n // tile) * tile
    flat = jnp.concatenate([flat, jnp.zeros((npad - n,), flat.dtype)])
    return flat.reshape(npad // LANES, LANES)


def _unpack(packed, shapes):
    flat = packed.reshape(-1)
    out, off = [], 0
    for s in shapes:
        n = 1
        for d in s:
            n *= d
        out.append(flat[off:off + n].reshape(s))
        off += n
    return out


def kernel(x, c, norm1_g, norm2_g, w_ada, b_ada, w_in, lam_re, lam_im, log_dt, b_re, b_im, c_re, c_im, d_skip, w_glu, b_glu, conv_w, w_proj_ssm, w_proj_conv, w_out, w_ff1, w_ff2, final_g, loss_target, m_norm1_g, m_norm2_g, m_w_ada, m_b_ada, m_w_in, m_lam_re, m_lam_im, m_log_dt, m_b_re, m_b_im, m_c_re, m_c_im, m_d_skip, m_w_glu, m_b_glu, m_conv_w, m_w_proj_ssm, m_w_proj_conv, m_w_out, m_w_ff1, m_w_ff2, m_final_g, v_norm1_g, v_norm2_g, v_w_ada, v_b_ada, v_w_in, v_lam_re, v_lam_im, v_log_dt, v_b_re, v_b_im, v_c_re, v_c_im, v_d_skip, v_w_glu, v_b_glu, v_conv_w, v_w_proj_ssm, v_w_proj_conv, v_w_out, v_w_ff1, v_w_ff2, v_final_g):
    Bl, L, D = x.shape
    N = Bl * L
    Ds = Dc = D // 2
    G, H, P = Ds // SSM_GROUP, SSM_GROUP, SSM_STATE
    GP = G * P
    NP = GP // LANES
    nb = _s5_dims(Bl, L, Ds)[4]
    IN = Ds + 3 * Dc + 2 * D
    ax, ay, ac = _mesh_pos()
    q = 2 * ax + ay
    dev = 2 * q + ac

    big_names = ["w_in", "w_ff1", "w_ff2", "w_out", "w_proj_ssm", "w_proj_conv", "w_glu"]
    big_w = dict(w_in=w_in[0], w_ff1=w_ff1[0], w_ff2=w_ff2[0], w_out=w_out[0],
                 w_proj_ssm=w_proj_ssm[0], w_proj_conv=w_proj_conv[0], w_glu=w_glu[0])
    big_axis = dict(w_in=1, w_ff1=1, w_ff2=0, w_out=0, w_proj_ssm=1, w_proj_conv=1, w_glu=0)
    axes = [big_axis[k] for k in big_names]
    shard_shapes = [big_w[k].shape for k in big_names]
    pos = jnp.stack([q, ac]).astype(jnp.int32)
    own_only = dict(zip(big_names, _cast_into_full([big_w[k] for k in big_names], axes, pos, "cast_weights")))
    Dcs = conv_w.shape[2]
    first, (w_in_full,) = _allgather8(_pack([c, conv_w[0]]), "allgather_c_w_in",
                                      _carry_allgather([own_only["w_in"]], [big_axis["w_in"]], [big_w["w_in"].shape]))
    full = {"w_in": w_in_full}
    first = first.reshape(N_DEV, -1)
    c_all = first[:, :Bl * D].reshape(N_DEV * Bl, D)
    cw = first[0::2, Bl * D:Bl * D + 3 * Dcs].reshape(N_CHIPS, 3, Dcs).transpose(1, 0, 2).reshape(3, Dc)
    cw8 = _pad_rows(cw, SUBLANES)
    Ca = w_ada.shape[2]
    b_ada_sh = lax.dynamic_slice_in_dim(b_ada, q * Ca, Ca, axis=1)
    mod_part = _ada_fwd(c_all, w_ada[0], b_ada_sh)
    mod_g = _allgather8(mod_part, "allgather_mod")
    mod_all = mod_g[0::2].transpose(1, 0, 2).reshape(N_DEV * Bl, N_CHIPS * Ca)
    mod = lax.dynamic_slice_in_dim(mod_all, dev * Bl, Bl, axis=0)
    sh1, sc1, g1, sh2, sc2, g2 = [mod[:, k * D:(k + 1) * D].reshape(Bl, 1, D) for k in range(6)]

    ldt_c = log_dt[0].reshape(G, 1)
    bt_r = b_re[0].transpose(2, 0, 1)
    bt_i = b_im[0].transpose(2, 0, 1)
    lbr, lbi, bbt_r, bbt_i = _s5_disc(lam_re[0], lam_im[0], ldt_c, bt_r, bt_i)
    lam_r_p = lbr.reshape(NP, LANES)
    lam_i_p = lbi.reshape(NP, LANES)
    Bm_r = _diag_blocks_from_groups(bbt_r.transpose(1, 0, 2), nb).astype(BF16)
    Bm_i = _diag_blocks_from_groups(bbt_i.transpose(1, 0, 2), nb).astype(BF16)
    Cm_r = _diag_blocks_from_groups(c_re[0], nb).astype(BF16)
    Cm_i = _diag_blocks_from_groups(-c_im[0], nb).astype(BF16)

    x2 = x.reshape(N, D)
    mixer_w = ["w_out", "w_proj_ssm", "w_proj_conv", "w_glu"]
    mlp_w = ["w_ff1", "w_ff2"]
    layout = lambda ks: ([big_axis[k] for k in ks], [big_w[k].shape for k in ks])
    gather = lambda ks: _carry_allgather([own_only[k] for k in ks], *layout(ks))
    p2, gathered = _inproj_fwd(x2, sh1, sc1, norm1_g, full["w_in"], gather(mixer_w))
    full.update(zip(mixer_w, gathered))
    p3 = p2.reshape(Bl, L, IN)
    (Sr, Si, Sb_r, Sb_i, ylin3), gathered = _s5_fwd(p3, Bm_r, Bm_i, Cm_r, Cm_i, lam_r_p, lam_i_p, d_skip, gather(mlp_w))
    full.update(zip(mlp_w, gathered))
    ylin2 = ylin3.reshape(N, Ds)
    mix_w = (full["w_glu"], b_glu, cw8, full["w_proj_ssm"], full["w_proj_conv"], full["w_out"])
    x1 = _mix_fwd(p2, ylin2, x2, g1, *mix_w)

    (dx1, h2b, dab, sqb, dfb, loss_p, g_fg, g_n2, dsh2, dsc2, dg2) = _mlp_fwd_bwd(
        x1, loss_target.reshape(N, D), sh2, sc2, g2, norm2_g, final_g.reshape(1, D), full["w_ff1"], full["w_ff2"])
    g_full = {"w_ff1": _grad_w(h2b, dab, "grad_w_ff1")[0], "w_ff2": _grad_w(sqb, dfb, "grad_w_ff2")[0]}

    exchange = lambda ks: _carry_sibling_exchange([g_full[k] for k in ks], *layout(ks))
    presum = lambda ks, theirs: _presum([g_full[k] for k in ks], list(theirs), *layout(ks), pos, "presum_" + ks[0])
    chip_sum = lambda ks, parts, recv: _sum_chips(list(parts), list(recv), pos, "sum_" + ks[0])

    (dyl2, dp2, gw_out, gw_ps, gw_pc, gw_glu, gb_glu, gcw8, dg1), theirs_mlp = _mix_bwd(
        p2, ylin2, dx1, g1, *mix_w, exchange(mlp_w))
    g_full.update(w_out=gw_out, w_proj_ssm=gw_ps, w_proj_conv=gw_pc, w_glu=gw_glu)
    parts_mlp = presum(mlp_w, theirs_mlp)
    (dp3, dBm_r, dBm_i, dCm_r, dCm_i, dlam_r_p, dlam_i_p, g_dsk), extra = _s5_bwd(
        dyl2.reshape(Bl, L, Ds), p3, dp2.reshape(Bl, L, IN), Sr, Si, Sb_r, Sb_i, Bm_r, Bm_i, Cm_r, Cm_i, lam_r_p, lam_i_p,
        d_skip, _carry_join(_carry_chip_scatter(parts_mlp), exchange(mixer_w)))
    recv_mlp, theirs_mix = extra[:len(mlp_w)], extra[len(mlp_w):]
    halves_mlp = chip_sum(mlp_w, parts_mlp, recv_mlp)
    parts_mix = presum(mixer_w, theirs_mix)
    dp_all = dp3.reshape(N, IN)
    (grad_x2, hb, g_n1, dsh1, dsc1), _ = _inproj_bwd(x2, dx1, dp_all, sh1, sc1, norm1_g, full["w_in"], _NO_EXCHANGE)

    dbbt_r = _groups_from_diag_blocks(dBm_r, G, H, P).transpose(1, 0, 2)
    dbbt_i = _groups_from_diag_blocks(dBm_i, G, H, P).transpose(1, 0, 2)
    dc_re = _groups_from_diag_blocks(dCm_r, G, H, P)
    dc_im = -_groups_from_diag_blocks(dCm_i, G, H, P)
    dmod = jnp.concatenate([dsh1, dsc1, dg1, dsh2, dsc2, dg2], axis=-1).reshape(Bl, 6 * D)
    small = [g_n1, g_n2, g_fg, g_dsk, gb_glu, gcw8[:3], dlam_r_p, dlam_i_p, dbbt_r, dbbt_i, dc_re, dc_im]
    small_shapes = [v.shape for v in small]
    n_small = sum(int(v.size) for v in small)
    small_slots = _place_in_slot(_pack(small + [dmod]), jnp.reshape(dev, (1,)).astype(jnp.int32), "place_small")

    g_full["w_in"], extra = _grad_w(
        hb, dp_all, "grad_w_in",
        _carry_join(_carry_join(_carry_sibling_share(halves_mlp), _carry_chip_scatter(parts_mix)), _carry_allgather8(small_slots)))
    reduced = dict(zip(mlp_w, extra[:len(mlp_w)]))
    halves_mix = chip_sum(mixer_w, parts_mix, extra[len(mlp_w):len(mlp_w) + len(mixer_w)])
    gathered = extra[-1]
    theirs_in = _run_carried(exchange(["w_in"]), "rs_exchange_w_in")
    parts_in = presum(["w_in"], theirs_in)
    recv_in = _run_carried(_carry_chip_scatter(parts_in), "rs_scatter_w_in")
    halves_in = chip_sum(["w_in"], parts_in, recv_in)
    reduced.update(zip(mixer_w + ["w_in"], _run_carried(_carry_sibling_share(halves_mix + halves_in), "rs_share_rest")))

    red = _unpack(_sum_devices(gathered, "sum_small"), small_shapes)
    (r_n1, r_n2, r_fg, r_dsk, r_bglu, r_cw, r_dlr, r_dli, r_dbr, r_dbi, r_cre, r_cim) = red
    dmod_all = gathered.reshape(N_DEV, -1)[:, n_small:n_small + Bl * 6 * D].reshape(N_DEV * Bl, 6 * D)
    gw_ada, gb_ada = _ada_bwd(c_all, lax.dynamic_slice_in_dim(dmod_all, q * Ca, Ca, axis=1), dmod_all)
    g_lr, g_li, g_ldt, g_bt_r, g_bt_i = _s5_disc_bwd(lam_re[0], lam_im[0], ldt_c, bt_r, bt_i,
                                                   r_dlr.reshape(G, P), r_dli.reshape(G, P), r_dbr, r_dbi)

    grads = dict(
        norm1_g=r_n1, norm2_g=r_n2, w_ada=gw_ada, b_ada=gb_ada, lam_re=g_lr, lam_im=g_li, log_dt=g_ldt.reshape(1, G),
        b_re=g_bt_r.transpose(1, 2, 0), b_im=g_bt_i.transpose(1, 2, 0), c_re=r_cre, c_im=r_cim, d_skip=r_dsk,
        b_glu=r_bglu, conv_w=lax.dynamic_slice_in_dim(r_cw, q * Dcs, Dcs, axis=1), final_g=r_fg, **reduced)
    weights = dict(norm1_g=norm1_g, norm2_g=norm2_g, w_ada=w_ada, b_ada=b_ada, w_in=w_in, lam_re=lam_re, lam_im=lam_im,
                   log_dt=log_dt, b_re=b_re, b_im=b_im, c_re=c_re, c_im=c_im, d_skip=d_skip, w_glu=w_glu, b_glu=b_glu,
                   conv_w=conv_w, w_proj_ssm=w_proj_ssm, w_proj_conv=w_proj_conv, w_out=w_out, w_ff1=w_ff1, w_ff2=w_ff2,
                   final_g=final_g)
    m_in = dict(norm1_g=m_norm1_g, norm2_g=m_norm2_g, w_ada=m_w_ada, b_ada=m_b_ada, w_in=m_w_in, lam_re=m_lam_re,
                lam_im=m_lam_im, log_dt=m_log_dt, b_re=m_b_re, b_im=m_b_im, c_re=m_c_re, c_im=m_c_im, d_skip=m_d_skip,
                w_glu=m_w_glu, b_glu=m_b_glu, conv_w=m_conv_w, w_proj_ssm=m_w_proj_ssm, w_proj_conv=m_w_proj_conv,
                w_out=m_w_out, w_ff1=m_w_ff1, w_ff2=m_w_ff2, final_g=m_final_g)
    v_in = dict(norm1_g=v_norm1_g, norm2_g=v_norm2_g, w_ada=v_w_ada, b_ada=v_b_ada, w_in=v_w_in, lam_re=v_lam_re,
                lam_im=v_lam_im, log_dt=v_log_dt, b_re=v_b_re, b_im=v_b_im, c_re=v_c_re, c_im=v_c_im, d_skip=v_d_skip,
                w_glu=v_w_glu, b_glu=v_b_glu, conv_w=v_conv_w, w_proj_ssm=v_w_proj_ssm, w_proj_conv=v_w_proj_conv,
                w_out=v_w_out, w_ff1=v_w_ff1, w_ff2=v_w_ff2, final_g=v_final_g)
    names = list(weights)
    grads = {k: grads[k].reshape(weights[k].shape) for k in names}

    big_upd = big_names + ["w_ada"]
    delta, new_m, new_v = {}, {}, {}
    flat2 = lambda a: a.reshape(-1, a.shape[-1])
    d_, m_, v_, g_ = _adamw([flat2(weights[k]) for k in big_upd], [flat2(grads[k]) for k in big_upd],
                            [flat2(m_in[k]) for k in big_upd], [flat2(v_in[k]) for k in big_upd], "adamw_big")
    for k, dd, mm, vv, gg in zip(big_upd, d_, m_, v_, g_):
        shp = weights[k].shape
        delta[k], new_m[k], new_v[k], grads[k] = dd.reshape(shp), mm.reshape(shp), vv.reshape(shp), gg.reshape(shp)
    small_upd = [k for k in names if k not in big_upd]
    d_, m_, v_ = _adamw_many([flat2(weights[k]) for k in small_upd], [flat2(grads[k]) for k in small_upd],
                             [flat2(m_in[k]) for k in small_upd], [flat2(v_in[k]) for k in small_upd])
    for k, dd, mm, vv in zip(small_upd, d_, m_, v_):
        shp = weights[k].shape
        delta[k], new_m[k], new_v[k] = dd.reshape(shp), mm.reshape(shp), vv.reshape(shp)

    loss = lax.psum(loss_p[0, 0], ("x", "y", "c"))
    grad_x = grad_x2.reshape(Bl, L, D)
    return (loss, grad_x, *[grads[k] for k in names], *[delta[k] for k in names],
            *[new_m[k] for k in names], *[new_v[k] for k in names])
```
